```python
import math
import jax, jax.numpy as jnp
from jax import lax
import numpy as np

D_MODEL = 1024
BATCH = 8
SEQ = 8192
DEPTH = 1

D_MIX = D_MODEL
GLA_WIDTH = D_MIX // 2
GLA_HEADS = 4
GLA_DK = GLA_WIDTH // 2 // GLA_HEADS
GLA_DV = GLA_WIDTH // GLA_HEADS
GLA_RANK = 16
GLA_TAU = 16.0
GLA_CHUNK = 64
DSA_WIDTH = D_MIX - GLA_WIDTH
DSA_HEADS = 8
DSA_DH = DSA_WIDTH // DSA_HEADS
DSA_PATTERN = ((128, 1), (512, 4), (2048, 16))
DSA_BLOCK = 128
REL_BUCKETS = 32
REL_MAX_DIST = 2048
D_FF = 4 * D_MODEL
EPS = 1e-6
NEG = -1e30

IN_SPLITS = (
    GLA_HEADS * GLA_DK,
    GLA_HEADS * GLA_DK,
    GLA_WIDTH,
    GLA_WIDTH,
    GLA_RANK,
    DSA_WIDTH,
    DSA_WIDTH,
    DSA_WIDTH,
)
D_IN = sum(IN_SPLITS)

kernel_name = "hymba_gla_dilated_swa_block"


def rmsnorm(x, g):
    xf = x.astype(jnp.float32)
    y = xf * lax.rsqrt(jnp.mean(xf * xf, axis=-1, keepdims=True) + EPS)
    return (y * g.astype(jnp.float32)).astype(x.dtype)


def t5_bucket(dist):
    max_exact = REL_BUCKETS // 2
    n = np.maximum(dist, 0)
    large = max_exact + (np.log(np.maximum(n, 1) / max_exact)
                         / math.log(REL_MAX_DIST / max_exact)
                         * (REL_BUCKETS - max_exact)).astype(np.int32)
    large = np.minimum(large, REL_BUCKETS - 1)
    return np.where(n < max_exact, n, large).astype(np.int32)


def gla_mixer(q, k, v, glog):
    B, S, H, dk = q.shape
    dv = v.shape[-1]
    C = GLA_CHUNK
    n = S // C
    f32 = jnp.float32
    q = (q.astype(f32) * dk ** -0.5).reshape(B, n, C, H, dk)
    k = k.astype(f32).reshape(B, n, C, H, dk)
    v = v.astype(f32).reshape(B, n, C, H, dv)
    b = jnp.cumsum(glog.astype(f32).reshape(B, n, C, H, dk), axis=2)
    b_last = b[:, :, -1]
    q_dec = q * jnp.exp(b)
    k_inv = k * jnp.exp(-b)
    k_end = k * jnp.exp(b_last[:, :, None] - b)
    causal = jnp.tril(jnp.ones((C, C), dtype=bool))
    att = jnp.einsum('bnihk,bnjhk->bnhij', q_dec, k_inv)
    att = jnp.where(causal, att, 0.0)
    o_intra = jnp.einsum('bnhij,bnjhv->bnihv', att, v)
    inc = jnp.einsum('bnjhk,bnjhv->bnhkv', k_end, v)

    def step(state, inp):
        decay, upd = inp
        return decay[..., None] * state + upd, state

    _, s_prev = lax.scan(step, jnp.zeros((B, H, dk, dv), f32),
                         (jnp.exp(b_last).swapaxes(0, 1), inc.swapaxes(0, 1)))
    s_prev = s_prev.swapaxes(0, 1)
    o_inter = jnp.einsum('bnihk,bnhkv->bnihv', q_dec, s_prev)
    return (o_intra + o_inter).reshape(B, S, H, dv)


def dilated_branch(q, k, v, rel_bias, window, dilation):
    B, S, H, Dh = q.shape
    L = DSA_BLOCK
    span = window // dilation
    unit = dilation * L
    Sp = -(-S // unit) * unit
    n = Sp // dilation
    nb = n // L
    f32 = jnp.float32

    def to_sub(t):
        t = jnp.pad(t, ((0, 0), (0, Sp - S), (0, 0), (0, 0)))
        return t.reshape(B, n, dilation, H, Dh).transpose(0, 2, 3, 1, 4)

    def band(t):
        tb = jnp.pad(t, ((0, 0), (0, 0), (0, 0), (L, 0), (0, 0))).reshape(B, dilation, H, nb + 1, L, Dh)
        return jnp.concatenate([tb[:, :, :, :-1], tb[:, :, :, 1:]], axis=4)

    qb = to_sub(q).reshape(B, dilation, H, nb, L, Dh)
    kb = band(to_sub(k))
    vb = band(to_sub(v))

    steps = L + np.arange(L)[:, None] - np.arange(2 * L)[None, :]
    in_band = (steps >= 0) & (steps <= span)
    key_idx = np.arange(nb)[:, None, None] * L + np.arange(2 * L)[None, None, :] - L
    mask = jnp.asarray(in_band[None] & (key_idx >= 0))
    bias = jnp.transpose(rel_bias[t5_bucket(steps * dilation)], (2, 0, 1)).astype(f32)

    s = jnp.einsum('bdhcqe,bdhcke->bdhcqk', qb, kb).astype(f32) + bias[:, None]
    s = jnp.where(mask, s, NEG)
    m = jnp.max(s, axis=-1, keepdims=True)
    p = jnp.exp(s - m)
    den = jnp.sum(p, axis=-1, keepdims=True)
    o = jnp.einsum('bdhcqk,bdhcke->bdhcqe', p, vb.astype(f32)) / den
    lse = (m + jnp.log(den))[..., 0]
    o = o.reshape(B, dilation, H, n, Dh).transpose(0, 3, 1, 2, 4).reshape(B, Sp, H, Dh)[:, :S]
    lse = lse.reshape(B, dilation, H, n).transpose(0, 3, 1, 2).reshape(B, Sp, H)[:, :S]
    return o, lse


def dilated_mixer(q, k, v, rel_bias):
    q = q * DSA_DH ** -0.5
    outs, lses = [], []
    for window, dilation in DSA_PATTERN:
        o, lse = dilated_branch(q, k, v, rel_bias, window, dilation)
        outs.append(o)
        lses.append(lse)
    w = jax.nn.softmax(jnp.stack(lses, axis=0), axis=0)
    return jnp.sum(w[..., None] * jnp.stack(outs, axis=0), axis=0)


def _fwd_setup_inputs(seed: int = 0) -> dict:
    key = jax.random.key(seed)
    ks = jax.random.split(key, 16)
    f32 = jnp.float32
    nrm = lambda k, shape, scale: jax.random.normal(k, shape, f32) * scale
    return {
        "x": jax.random.normal(ks[0], (BATCH, SEQ, D_MODEL), f32),
        "attn_norm_g": 1.0 + nrm(ks[1], (DEPTH, D_MODEL), 0.02),
        "w_in": nrm(ks[2], (DEPTH, D_MODEL, D_IN), D_MODEL ** -0.5),
        "gla_gate_w2": nrm(ks[3], (DEPTH, GLA_RANK, GLA_HEADS * GLA_DK), GLA_RANK ** -0.5),
        "gla_gate_b": nrm(ks[4], (DEPTH, GLA_HEADS * GLA_DK), 0.1),
        "gla_norm_g": 1.0 + nrm(ks[5], (DEPTH, GLA_WIDTH), 0.02),
        "rel_bias": nrm(ks[6], (REL_BUCKETS, DSA_HEADS), 0.1),
        "w_out": nrm(ks[7], (DEPTH, D_MIX, D_MODEL), D_MIX ** -0.5),
        "mlp_norm_g": 1.0 + nrm(ks[8], (DEPTH, D_MODEL), 0.02),
        "w_ff1": nrm(ks[9], (DEPTH, D_MODEL, D_FF), D_MODEL ** -0.5),
        "w_ff2": nrm(ks[10], (DEPTH, D_FF, D_MODEL), D_FF ** -0.5),
        "final_norm_g": 1.0 + nrm(ks[11], (D_MODEL,), 0.02),
    }


def _fwd_reference(x, attn_norm_g, w_in, gla_gate_w2, gla_gate_b, gla_norm_g, rel_bias,
              w_out, mlp_norm_g, w_ff1, w_ff2, final_norm_g):
    B, S, _ = x.shape
    split_at = [int(i) for i in np.cumsum(IN_SPLITS)[:-1]]
    h = x
    for l in range(DEPTH):
        nx = rmsnorm(h, attn_norm_g[l])
        proj = jnp.einsum('bsd,dp->bsp', nx, w_in[l])
        gq, gk, gv, gr, glow, dq, dk_, dv_ = jnp.split(proj, split_at, axis=-1)

        gate_pre = (jnp.einsum('bsr,rk->bsk', glow, gla_gate_w2[l]) + gla_gate_b[l]).astype(jnp.float32)
        glog = jax.nn.log_sigmoid(gate_pre) / GLA_TAU
        o_a = gla_mixer(gq.reshape(B, S, GLA_HEADS, GLA_DK),
                        gk.reshape(B, S, GLA_HEADS, GLA_DK),
                        gv.reshape(B, S, GLA_HEADS, GLA_DV),
                        glog.reshape(B, S, GLA_HEADS, GLA_DK))
        o_a = o_a * lax.rsqrt(jnp.mean(o_a * o_a, axis=-1, keepdims=True) + EPS)
        o_a = o_a * gla_norm_g[l].astype(jnp.float32).reshape(GLA_HEADS, GLA_DV)
        o_a = o_a.reshape(B, S, GLA_WIDTH) * jax.nn.silu(gr.astype(jnp.float32))

        o_b = dilated_mixer(dq.reshape(B, S, DSA_HEADS, DSA_DH),
                            dk_.reshape(B, S, DSA_HEADS, DSA_DH),
                            dv_.reshape(B, S, DSA_HEADS, DSA_DH),
                            rel_bias).reshape(B, S, DSA_WIDTH)

        mixed = jnp.concatenate([o_a, o_b], axis=-1).astype(h.dtype)
        h = h + jnp.einsum('bsm,md->bsd', mixed, w_out[l])

        nm = rmsnorm(h, mlp_norm_g[l])
        a = jnp.square(jax.nn.relu(jnp.einsum('bsd,df->bsf', nm, w_ff1[l])))
        h = h + jnp.einsum('bsf,fd->bsd', a, w_ff2[l])
    return rmsnorm(h, final_norm_g)


import jax as _jax
import jax.numpy as _jnp

TWIN_FORMAT = 'train_step'
FWD_PARAMS = ['x', 'attn_norm_g', 'w_in', 'gla_gate_w2', 'gla_gate_b', 'gla_norm_g', 'rel_bias', 'w_out', 'mlp_norm_g', 'w_ff1', 'w_ff2', 'final_norm_g']
TWIN_WEIGHTS = ['attn_norm_g', 'w_in', 'gla_gate_w2', 'gla_gate_b', 'gla_norm_g', 'rel_bias', 'w_out', 'mlp_norm_g', 'w_ff1', 'w_ff2', 'final_norm_g']
TWIN_DIFF_INPUT = 'x'
TWIN_INPUTS = ['x', 'attn_norm_g', 'w_in', 'gla_gate_w2', 'gla_gate_b', 'gla_norm_g', 'rel_bias', 'w_out', 'mlp_norm_g', 'w_ff1', 'w_ff2', 'final_norm_g', 'loss_target', 'm_attn_norm_g', 'm_w_in', 'm_gla_gate_w2', 'm_gla_gate_b', 'm_gla_norm_g', 'm_rel_bias', 'm_w_out', 'm_mlp_norm_g', 'm_w_ff1', 'm_w_ff2', 'm_final_norm_g', 'v_attn_norm_g', 'v_w_in', 'v_gla_gate_w2', 'v_gla_gate_b', 'v_gla_norm_g', 'v_rel_bias', 'v_w_out', 'v_mlp_norm_g', 'v_w_ff1', 'v_w_ff2', 'v_final_norm_g']
TWIN_OUTPUTS = ['loss', 'grad_x', 'grad_attn_norm_g', 'grad_w_in', 'grad_gla_gate_w2', 'grad_gla_gate_b', 'grad_gla_norm_g', 'grad_rel_bias', 'grad_w_out', 'grad_mlp_norm_g', 'grad_w_ff1', 'grad_w_ff2', 'grad_final_norm_g', 'delta_attn_norm_g', 'delta_w_in', 'delta_gla_gate_w2', 'delta_gla_gate_b', 'delta_gla_norm_g', 'delta_rel_bias', 'delta_w_out', 'delta_mlp_norm_g', 'delta_w_ff1', 'delta_w_ff2', 'delta_final_norm_g', 'new_m_attn_norm_g', 'new_m_w_in', 'new_m_gla_gate_w2', 'new_m_gla_gate_b', 'new_m_gla_norm_g', 'new_m_rel_bias', 'new_m_w_out', 'new_m_mlp_norm_g', 'new_m_w_ff1', 'new_m_w_ff2', 'new_m_final_norm_g', 'new_v_attn_norm_g', 'new_v_w_in', 'new_v_gla_gate_w2', 'new_v_gla_gate_b', 'new_v_gla_norm_g', 'new_v_rel_bias', 'new_v_w_out', 'new_v_mlp_norm_g', 'new_v_w_ff1', 'new_v_w_ff2', 'new_v_final_norm_g']
TWIN_LEAF_KINDS = {'loss': 'loss', 'grad_x': 'grad_x', 'grad_attn_norm_g': 'grad_w', 'grad_w_in': 'grad_w', 'grad_gla_gate_w2': 'grad_w', 'grad_gla_gate_b': 'grad_w', 'grad_gla_norm_g': 'grad_w', 'grad_rel_bias': 'grad_w', 'grad_w_out': 'grad_w', 'grad_mlp_norm_g': 'grad_w', 'grad_w_ff1': 'grad_w', 'grad_w_ff2': 'grad_w', 'grad_final_norm_g': 'grad_w', 'delta_attn_norm_g': 'delta_w', 'delta_w_in': 'delta_w', 'delta_gla_gate_w2': 'delta_w', 'delta_gla_gate_b': 'delta_w', 'delta_gla_norm_g': 'delta_w', 'delta_rel_bias': 'delta_w', 'delta_w_out': 'delta_w', 'delta_mlp_norm_g': 'delta_w', 'delta_w_ff1': 'delta_w', 'delta_w_ff2': 'delta_w', 'delta_final_norm_g': 'delta_w', 'new_m_attn_norm_g': 'new_m', 'new_m_w_in': 'new_m', 'new_m_gla_gate_w2': 'new_m', 'new_m_gla_gate_b': 'new_m', 'new_m_gla_norm_g': 'new_m', 'new_m_rel_bias': 'new_m', 'new_m_w_out': 'new_m', 'new_m_mlp_norm_g': 'new_m', 'new_m_w_ff1': 'new_m', 'new_m_w_ff2': 'new_m', 'new_m_final_norm_g': 'new_m', 'new_v_attn_norm_g': 'new_v', 'new_v_w_in': 'new_v', 'new_v_gla_gate_w2': 'new_v', 'new_v_gla_gate_b': 'new_v', 'new_v_gla_norm_g': 'new_v', 'new_v_rel_bias': 'new_v', 'new_v_w_out': 'new_v', 'new_v_mlp_norm_g': 'new_v', 'new_v_w_ff1': 'new_v', 'new_v_w_ff2': 'new_v', 'new_v_final_norm_g': 'new_v'}


def _forward(args):
    return _fwd_reference(*[args[k] for k in FWD_PARAMS])


def _output_shape():
    def fwd():
        inp = _fwd_setup_inputs(0)
        return _fwd_reference(*[inp[k] for k in FWD_PARAMS])
    out = _jax.eval_shape(fwd)
    return out.shape, out.dtype

N_MICROBATCH = 1
ADAM_LR = 0.001
ADAM_B1 = 0.9
ADAM_B2 = 0.999
ADAM_EPS = 1e-08
ADAM_WD = 0.01
ADAM_STEP = 10
PER_EXAMPLE_BATCH_AXIS = {'x': 0, 'loss_target': 0}
SHARED_INPUTS = []
_WEIGHT_DTYPES = {'attn_norm_g': _jnp.float32, 'w_in': _jnp.float32, 'gla_gate_w2': _jnp.float32, 'gla_gate_b': _jnp.float32, 'gla_norm_g': _jnp.float32, 'rel_bias': _jnp.float32, 'w_out': _jnp.float32, 'mlp_norm_g': _jnp.float32, 'w_ff1': _jnp.float32, 'w_ff2': _jnp.float32, 'final_norm_g': _jnp.float32}
MOMENT_SCALE = {'attn_norm_g': 2.289009e-01, 'w_in': 1.302148e-01, 'gla_gate_w2': 2.502829e-02, 'gla_gate_b': 9.255951e-02, 'gla_norm_g': 1.523978e-01, 'rel_bias': 6.758574e-02, 'w_out': 1.123928e-01, 'mlp_norm_g': 2.345256e-01, 'w_ff1': 1.089492e-01, 'w_ff2': 2.359939e-01, 'final_norm_g': 6.458628e+01}


def _to_microbatches(a, axis):
    t = _jnp.moveaxis(a, axis, 0)
    t = t.reshape((N_MICROBATCH, t.shape[0] // N_MICROBATCH) + t.shape[1:])
    return _jnp.moveaxis(t, 1, axis + 1)


def setup_inputs(seed: int = 0) -> dict:
    inp = _fwd_setup_inputs(seed)
    key = _jax.random.fold_in(_jax.random.key(seed), 7919)
    shape, _ = _output_shape()
    out = dict(inp)
    out["loss_target"] = _jax.random.normal(_jax.random.fold_in(key, 0), shape, _jnp.float32)
    for i, name in enumerate(TWIN_WEIGHTS):
        w = inp[name].astype(_jnp.float32)
        if MOMENT_SCALE is None:
            s = _jnp.sqrt(_jnp.mean(_jnp.square(w)) + 1e-30)
        else:
            s = MOMENT_SCALE[name]
        km, kv = _jax.random.split(_jax.random.fold_in(key, i + 1))
        out[name] = w
        out["m_" + name] = s * _jax.random.normal(km, w.shape, _jnp.float32)
        out["v_" + name] = (s * s) * _jax.random.uniform(kv, w.shape, _jnp.float32, 0.5, 1.5)
    if N_MICROBATCH > 1:
        for name, axis in PER_EXAMPLE_BATCH_AXIS.items():
            out[name] = _to_microbatches(out[name], axis)
    return {'x': out['x'], 'attn_norm_g': out['attn_norm_g'], 'w_in': out['w_in'], 'gla_gate_w2': out['gla_gate_w2'], 'gla_gate_b': out['gla_gate_b'], 'gla_norm_g': out['gla_norm_g'], 'rel_bias': out['rel_bias'], 'w_out': out['w_out'], 'mlp_norm_g': out['mlp_norm_g'], 'w_ff1': out['w_ff1'], 'w_ff2': out['w_ff2'], 'final_norm_g': out['final_norm_g'], 'loss_target': out['loss_target'], 'm_attn_norm_g': out['m_attn_norm_g'], 'm_w_in': out['m_w_in'], 'm_gla_gate_w2': out['m_gla_gate_w2'], 'm_gla_gate_b': out['m_gla_gate_b'], 'm_gla_norm_g': out['m_gla_norm_g'], 'm_rel_bias': out['m_rel_bias'], 'm_w_out': out['m_w_out'], 'm_mlp_norm_g': out['m_mlp_norm_g'], 'm_w_ff1': out['m_w_ff1'], 'm_w_ff2': out['m_w_ff2'], 'm_final_norm_g': out['m_final_norm_g'], 'v_attn_norm_g': out['v_attn_norm_g'], 'v_w_in': out['v_w_in'], 'v_gla_gate_w2': out['v_gla_gate_w2'], 'v_gla_gate_b': out['v_gla_gate_b'], 'v_gla_norm_g': out['v_gla_norm_g'], 'v_rel_bias': out['v_rel_bias'], 'v_w_out': out['v_w_out'], 'v_mlp_norm_g': out['v_mlp_norm_g'], 'v_w_ff1': out['v_w_ff1'], 'v_w_ff2': out['v_w_ff2'], 'v_final_norm_g': out['v_final_norm_g']}


def _loss(weights, diff, rest, loss_target):
    with _jax.named_scope("forward"):
        args = {**rest, TWIN_DIFF_INPUT: diff, **{k: w.astype(_WEIGHT_DTYPES[k]) for k, w in weights.items()}}
        y = _forward(args)
    with _jax.named_scope("loss_head"):
        err = _jnp.square(y.astype(_jnp.float32) - loss_target)
        return 0.5 * _jnp.sum(_jnp.mean(err, axis=-1)) if err.ndim else 0.5 * err


def _adamw(w, g, m, v):
    m = ADAM_B1 * m + (1.0 - ADAM_B1) * g
    v = ADAM_B2 * v + (1.0 - ADAM_B2) * _jnp.square(g)
    m_hat = m / (1.0 - ADAM_B1 ** ADAM_STEP)
    v_hat = v / (1.0 - ADAM_B2 ** ADAM_STEP)
    delta = -ADAM_LR * (m_hat / (_jnp.sqrt(v_hat) + ADAM_EPS) + ADAM_WD * w)
    return delta, m, v


def reference(x, attn_norm_g, w_in, gla_gate_w2, gla_gate_b, gla_norm_g, rel_bias, w_out, mlp_norm_g, w_ff1, w_ff2, final_norm_g, loss_target, m_attn_norm_g, m_w_in, m_gla_gate_w2, m_gla_gate_b, m_gla_norm_g, m_rel_bias, m_w_out, m_mlp_norm_g, m_w_ff1, m_w_ff2, m_final_norm_g, v_attn_norm_g, v_w_in, v_gla_gate_w2, v_gla_gate_b, v_gla_norm_g, v_rel_bias, v_w_out, v_mlp_norm_g, v_w_ff1, v_w_ff2, v_final_norm_g):
    given = dict(x=x, attn_norm_g=attn_norm_g, w_in=w_in, gla_gate_w2=gla_gate_w2, gla_gate_b=gla_gate_b, gla_norm_g=gla_norm_g, rel_bias=rel_bias, w_out=w_out, mlp_norm_g=mlp_norm_g, w_ff1=w_ff1, w_ff2=w_ff2, final_norm_g=final_norm_g, loss_target=loss_target, m_attn_norm_g=m_attn_norm_g, m_w_in=m_w_in, m_gla_gate_w2=m_gla_gate_w2, m_gla_gate_b=m_gla_gate_b, m_gla_norm_g=m_gla_norm_g, m_rel_bias=m_rel_bias, m_w_out=m_w_out, m_mlp_norm_g=m_mlp_norm_g, m_w_ff1=m_w_ff1, m_w_ff2=m_w_ff2, m_final_norm_g=m_final_norm_g, v_attn_norm_g=v_attn_norm_g, v_w_in=v_w_in, v_gla_gate_w2=v_gla_gate_w2, v_gla_gate_b=v_gla_gate_b, v_gla_norm_g=v_gla_norm_g, v_rel_bias=v_rel_bias, v_w_out=v_w_out, v_mlp_norm_g=v_mlp_norm_g, v_w_ff1=v_w_ff1, v_w_ff2=v_w_ff2, v_final_norm_g=v_final_norm_g)
    weights = {n: given[n] for n in TWIN_WEIGHTS}
    shared = {n: given[n] for n in SHARED_INPUTS}
    per_example = {n: given[n] for n in ['x']}
    grad_fn = _jax.value_and_grad(_loss, argnums=(0, 1))

    def one_microbatch(ex, loss_target):
        ex = dict(ex)
        diff = ex.pop(TWIN_DIFF_INPUT)
        return grad_fn(weights, diff, {**shared, **ex}, loss_target)

    if N_MICROBATCH == 1:
        loss, (grad_w, grad_x) = one_microbatch(per_example, given["loss_target"])
    else:
        def body(carry, xs):
            loss_sum, grad_sum = carry
            l_k, (gw_k, gx_k) = one_microbatch(xs[0], xs[1])
            with _jax.named_scope("update"):
                return (loss_sum + l_k, _jax.tree.map(_jnp.add, grad_sum, gw_k)), gx_k

        init = (_jnp.zeros((), _jnp.float32), _jax.tree.map(_jnp.zeros_like, weights))
        (loss, grad_w), grad_x = _jax.lax.scan(body, init, (per_example, given["loss_target"]))
    with _jax.named_scope("update"):
        delta_w, new_m, new_v = {}, {}, {}
        for n in TWIN_WEIGHTS:
            delta_w[n], new_m[n], new_v[n] = _adamw(weights[n], grad_w[n], given["m_" + n], given["v_" + n])
    return (loss, grad_x, *[grad_w[n] for n in TWIN_WEIGHTS], *[delta_w[n] for n in TWIN_WEIGHTS],
            *[new_m[n] for n in TWIN_WEIGHTS], *[new_v[n] for n in TWIN_WEIGHTS])
```

```python
import functools
import math

import jax
import jax.numpy as jnp
import numpy as np
from jax import lax
from jax.experimental import pallas as pl
from jax.experimental.pallas import tpu as pltpu

F32 = jnp.float32
BF16 = jnp.bfloat16
MESH = pl.DeviceIdType.MESH

D_MODEL = 1024
GLA_WIDTH = 512
GLA_HEADS = 4
GLA_DK = 64
GLA_DV = 128
GLA_QK = GLA_HEADS * GLA_DK
GLA_RANK = 16
GLA_TAU = 16.0
GLA_CHUNK = 64
DSA_WIDTH = 512
DSA_HEADS = 8
DSA_DH = 64
DSA_DILATIONS = (1, 4, 16)
DSA_SPAN = 128
DSA_BLOCK = 128
DSA_SUPER = DSA_BLOCK * DSA_DILATIONS[-1]
REL_BUCKETS = 32
REL_MAX_DIST = 2048
D_FF = 4096
D_IN = 3088
EPS = 1e-6
NEG = -1e30
QK_SCALE = 0.125

ADAM_LR = 0.001
ADAM_B1 = 0.9
ADAM_B2 = 0.999
ADAM_EPS = 1e-08
ADAM_WD = 0.01
ADAM_STEP = 10

LANE = 128
P_GQ, P_GK, P_GV, P_GR = 0, 256, 512, 1024
P_GLOW = 1536
P_A = 1664
P_DQ, P_DK, P_DV = 1664, 2176, 2688
P_ALL = 3200
R_GQ, R_GK, R_GV, R_GR, R_GLOW, R_DQ, R_DK, R_DV = 0, 256, 512, 1024, 1536, 1552, 2064, 2576

VMEM_LIMIT = 56 * 1024 * 1024


def _params(sem=("arbitrary",), vmem=VMEM_LIMIT):
    return pltpu.CompilerParams(dimension_semantics=sem, vmem_limit_bytes=vmem)


def _dot(a, b):
    return jnp.dot(a, b, preferred_element_type=F32)


def _dot_nt(a, b):
    return lax.dot_general(a, b, (((1,), (1,)), ((), ())), preferred_element_type=F32)


def _dot_tn(a, b):
    return lax.dot_general(a, b, (((0,), (0,)), ((), ())), preferred_element_type=F32)


def _split3(x):
    x1 = x.astype(BF16)
    r1 = x - x1.astype(F32)
    x2 = r1.astype(BF16)
    x3 = (r1 - x2.astype(F32)).astype(BF16)
    return x1, x2, x3


def _dot_exact_lhs(m_bf16, x):
    x1, x2, x3 = _split3(x)
    return _dot(m_bf16, x1) + _dot(m_bf16, x2) + _dot(m_bf16, x3)


def _rstd(xf):
    return lax.rsqrt(jnp.mean(xf * xf, axis=-1, keepdims=True) + EPS)


def _load_once(hbm_ref, vmem_ref, sem):
    cp = pltpu.make_async_copy(hbm_ref, vmem_ref, sem)
    cp.start()
    cp.wait()


ANY = pl.BlockSpec(memory_space=pl.ANY)


def inproj(x, g1, wp):
    T = x.shape[0]
    tm = 256

    def body(x_ref, g_ref, w_hbm, proj_ref, nx_ref, w_vmem, sem):
        @pl.when(pl.program_id(0) == 0)
        def _():
            _load_once(w_hbm, w_vmem, sem)

        xf = x_ref[...]
        nx = ((xf * _rstd(xf)) * g_ref[...]).astype(BF16)
        nx_ref[...] = nx
        proj_ref[...] = _dot(nx, w_vmem[...])

    return pl.pallas_call(
        body,
        grid=(T // tm,),
        in_specs=[pl.BlockSpec((tm, D_MODEL), lambda i: (i, 0)), pl.BlockSpec((1, D_MODEL), lambda i: (0, 0)), ANY],
        out_specs=[pl.BlockSpec((tm, P_ALL), lambda i: (i, 0)), pl.BlockSpec((tm, D_MODEL), lambda i: (i, 0))],
        out_shape=[jax.ShapeDtypeStruct((T, P_ALL), F32), jax.ShapeDtypeStruct((T, D_MODEL), BF16)],
        scratch_shapes=[pltpu.VMEM((D_MODEL, P_ALL), BF16), pltpu.SemaphoreType.DMA],
        compiler_params=_params(),
        name="inproj",
    )(x, g1, wp)


GLA_CHUNKS_PER_STEP = 8
GLA_ROWS = GLA_CHUNK * GLA_CHUNKS_PER_STEP


def _gla_masks():
    lane = lax.broadcasted_iota(jnp.int32, (1, GLA_QK), 1)
    return [(lane >= h * GLA_DK) & (lane < (h + 1) * GLA_DK) for h in range(GLA_HEADS)]


def _log_sigmoid(x):
    return jnp.minimum(x, 0.0) - jnp.log(1.0 + jnp.exp(-jnp.abs(x)))


def _sigmoid(x):
    return 1.0 / (1.0 + jnp.exp(-x))


def _gla_chunk_common(proj_ref, rows, w2, bg, tri):
    q = proj_ref[rows, P_GQ:P_GQ + GLA_QK]
    k = proj_ref[rows, P_GK:P_GK + GLA_QK]
    glow = proj_ref[rows, P_GLOW:P_GLOW + LANE].astype(BF16)
    gpre = _dot(glow, w2) + bg
    glog = _log_sigmoid(gpre) / GLA_TAU
    b = _dot_exact_lhs(tri, glog)
    bl = b[GLA_CHUNK - 1:GLA_CHUNK, :]
    eb = jnp.exp(b)
    enb = jnp.exp(-b)
    eke = jnp.exp(bl - b)
    qd = (q * QK_SCALE) * eb
    ki = k * enb
    ke = k * eke
    return glow, gpre, eb, enb, eke, jnp.exp(bl), qd, ki, ke


def gla_fwd(proj, w2p, bg, gn):
    T = proj.shape[0]
    n_steps = T // GLA_ROWS
    n_chunks = T // GLA_CHUNK

    def body(proj_ref, w2_ref, bg_ref, gn_ref, oa_ref, opre_ref, sprev_ref, st_ref):
        @pl.when(pl.program_id(0) == 0)
        def _():
            st_ref[...] = jnp.zeros_like(st_ref)

        masks = _gla_masks()
        ri = lax.broadcasted_iota(jnp.int32, (GLA_CHUNK, GLA_CHUNK), 0)
        ci = lax.broadcasted_iota(jnp.int32, (GLA_CHUNK, GLA_CHUNK), 1)
        causal = ri >= ci
        tri = causal.astype(BF16)
        w2 = w2_ref[...].astype(BF16)
        bg = bg_ref[...]

        def chunk(c, carry):
            rows = pl.ds(pl.multiple_of(c * GLA_CHUNK, GLA_CHUNK), GLA_CHUNK)
            _, _, _, _, _, ebl, qd, ki, ke = _gla_chunk_common(proj_ref, rows, w2, bg, tri)
            st = st_ref[...]
            sprev_ref[c] = st
            st_b = st.astype(BF16)
            ki_b = ki.astype(BF16)
            inc = jnp.zeros_like(st)
            for h in range(GLA_HEADS):
                vl = slice(P_GV + h * GLA_DV, P_GV + (h + 1) * GLA_DV)
                v_h = proj_ref[rows, vl].astype(BF16)
                qd_h = jnp.where(masks[h], qd, 0.0).astype(BF16)
                ke_h = jnp.where(masks[h], ke, 0.0).astype(BF16)
                att = jnp.where(causal, _dot_nt(qd_h, ki_b), 0.0)
                o_h = _dot(att.astype(BF16), v_h) + _dot_nt(qd_h, st_b)
                inc = inc + _dot_tn(v_h, ke_h)
                ol = slice(h * GLA_DV, (h + 1) * GLA_DV)
                opre_ref[rows, ol] = o_h
                r_h = proj_ref[rows, P_GR + h * GLA_DV:P_GR + (h + 1) * GLA_DV]
                on = o_h * _rstd(o_h)
                oa_ref[rows, ol] = ((on * gn_ref[:, ol]) * (r_h * _sigmoid(r_h))).astype(BF16)
            st_ref[...] = st * ebl + inc
            return carry

        lax.fori_loop(0, GLA_CHUNKS_PER_STEP, chunk, 0)

    return pl.pallas_call(
        body,
        grid=(n_steps,),
        in_specs=[
            pl.BlockSpec((GLA_ROWS, P_ALL), lambda i: (i, 0)),
            pl.BlockSpec((LANE, GLA_QK), lambda i: (0, 0)),
            pl.BlockSpec((1, GLA_QK), lambda i: (0, 0)),
            pl.BlockSpec((1, GLA_WIDTH), lambda i: (0, 0)),
        ],
        out_specs=[
            pl.BlockSpec((GLA_ROWS, GLA_WIDTH), lambda i: (i, 0)),
            pl.BlockSpec((GLA_ROWS, GLA_WIDTH), lambda i: (i, 0)),
            pl.BlockSpec((GLA_CHUNKS_PER_STEP, GLA_DV, GLA_QK), lambda i: (i, 0, 0)),
        ],
        out_shape=[
            jax.ShapeDtypeStruct((T, GLA_WIDTH), BF16),
            jax.ShapeDtypeStruct((T, GLA_WIDTH), F32),
            jax.ShapeDtypeStruct((n_chunks, GLA_DV, GLA_QK), F32),
        ],
        scratch_shapes=[pltpu.VMEM((GLA_DV, GLA_QK), F32)],
        compiler_params=_params(),
        name="gla_fwd",
    )(proj, w2p, bg, gn)


def gla_bwd(proj, w2p, bg, gn, opre, sprev, dmixed):
    T = proj.shape[0]
    n_steps = T // GLA_ROWS

    def body(proj_ref, w2_ref, bg_ref, gn_ref, opre_ref, sprev_ref, doa_ref, da_ref, dw2_ref, dbg_ref, dgn_ref, dst_ref):
        @pl.when(pl.program_id(0) == 0)
        def _():
            dst_ref[...] = jnp.zeros_like(dst_ref)
            dw2_ref[...] = jnp.zeros_like(dw2_ref)
            dbg_ref[...] = jnp.zeros_like(dbg_ref)
            dgn_ref[...] = jnp.zeros_like(dgn_ref)

        masks = _gla_masks()
        ri = lax.broadcasted_iota(jnp.int32, (GLA_CHUNK, GLA_CHUNK), 0)
        ci = lax.broadcasted_iota(jnp.int32, (GLA_CHUNK, GLA_CHUNK), 1)
        causal = ri >= ci
        tri = causal.astype(BF16)
        tri_t = (ri <= ci).astype(BF16)
        last_row = lax.broadcasted_iota(jnp.int32, (GLA_CHUNK, GLA_QK), 0) == GLA_CHUNK - 1
        w2 = w2_ref[...].astype(BF16)
        bg = bg_ref[...]

        def chunk(j, carry):
            c = GLA_CHUNKS_PER_STEP - 1 - j
            rows = pl.ds(pl.multiple_of(c * GLA_CHUNK, GLA_CHUNK), GLA_CHUNK)
            glow, gpre, eb, enb, eke, ebl, qd, ki, ke = _gla_chunk_common(proj_ref, rows, w2, bg, tri)
            st = sprev_ref[c]
            dst = dst_ref[...]
            st_b = st.astype(BF16)
            dst_b = dst.astype(BF16)
            qd_b = qd.astype(BF16)
            ki_b = ki.astype(BF16)
            dqd = jnp.zeros_like(qd)
            dki = jnp.zeros_like(qd)
            dke = jnp.zeros_like(qd)
            dst_add = jnp.zeros_like(st)
            for h in range(GLA_HEADS):
                ol = slice(h * GLA_DV, (h + 1) * GLA_DV)
                o_h = opre_ref[rows, ol]
                doa = doa_ref[rows, ol]
                r_h = proj_ref[rows, P_GR + h * GLA_DV:P_GR + (h + 1) * GLA_DV]
                g_h = gn_ref[:, ol]
                sig = _sigmoid(r_h)
                rs = _rstd(o_h)
                on = o_h * rs
                d_ong = doa * (r_h * sig)
                da_ref[rows, P_GR + h * GLA_DV:P_GR + (h + 1) * GLA_DV] = (
                    doa * (on * g_h) * (sig * (1.0 + r_h * (1.0 - sig)))).astype(BF16)
                dgn_ref[:, ol] += jnp.sum(d_ong * on, axis=0, keepdims=True)
                d_on = d_ong * g_h
                do_h = rs * (d_on - on * jnp.mean(d_on * on, axis=-1, keepdims=True))
                do_b = do_h.astype(BF16)
                v_h = proj_ref[rows, P_GV + h * GLA_DV:P_GV + (h + 1) * GLA_DV].astype(BF16)
                qd_h = jnp.where(masks[h], qd, 0.0).astype(BF16)
                ke_h = jnp.where(masks[h], ke, 0.0).astype(BF16)
                att = jnp.where(causal, _dot_nt(qd_h, ki_b), 0.0).astype(BF16)
                d_att = jnp.where(causal, _dot_nt(do_b, v_h), 0.0).astype(BF16)
                dv_h = _dot_tn(att, do_b) + _dot_nt(ke_h, dst_b)
                da_ref[rows, P_GV + h * GLA_DV:P_GV + (h + 1) * GLA_DV] = dv_h.astype(BF16)
                dqd = dqd + jnp.where(masks[h], _dot(d_att, ki_b) + _dot(do_b, st_b), 0.0)
                dki = dki + _dot_tn(d_att, qd_h)
                dke = dke + jnp.where(masks[h], _dot(v_h, dst_b), 0.0)
                dst_add = dst_add + _dot_tn(do_b, qd_h)
            debl = jnp.sum(dst * st, axis=0, keepdims=True)
            dst_ref[...] = dst * ebl + dst_add
            da_ref[rows, P_GQ:P_GQ + GLA_QK] = (dqd * eb * QK_SCALE).astype(BF16)
            da_ref[rows, P_GK:P_GK + GLA_QK] = (dki * enb + dke * eke).astype(BF16)
            dkk = dke * ke
            db = dqd * qd - dki * ki - dkk
            dbl = jnp.sum(dkk, axis=0, keepdims=True) + debl * ebl
            db = db + jnp.where(last_row, dbl, 0.0)
            dglog = _dot_exact_lhs(tri_t, db)
            dgpre = (dglog / GLA_TAU) * _sigmoid(-gpre)
            dgpre_b = dgpre.astype(BF16)
            da_ref[rows, P_GLOW:P_GLOW + LANE] = _dot_nt(dgpre_b, w2).astype(BF16)
            dw2_ref[...] += _dot_tn(glow, dgpre_b)
            dbg_ref[...] += jnp.sum(dgpre, axis=0, keepdims=True)
            return carry

        lax.fori_loop(0, GLA_CHUNKS_PER_STEP, chunk, 0)

    rev = lambda i: (n_steps - 1 - i, 0)
    return pl.pallas_call(
        body,
        grid=(n_steps,),
        in_specs=[
            pl.BlockSpec((GLA_ROWS, P_ALL), rev),
            pl.BlockSpec((LANE, GLA_QK), lambda i: (0, 0)),
            pl.BlockSpec((1, GLA_QK), lambda i: (0, 0)),
            pl.BlockSpec((1, GLA_WIDTH), lambda i: (0, 0)),
            pl.BlockSpec((GLA_ROWS, GLA_WIDTH), rev),
            pl.BlockSpec((GLA_CHUNKS_PER_STEP, GLA_DV, GLA_QK), lambda i: (n_steps - 1 - i, 0, 0)),
            pl.BlockSpec((GLA_ROWS, GLA_WIDTH), rev),
        ],
        out_specs=[
            pl.BlockSpec((GLA_ROWS, P_A), rev),
            pl.BlockSpec((LANE, GLA_QK), lambda i: (0, 0)),
            pl.BlockSpec((1, GLA_QK), lambda i: (0, 0)),
            pl.BlockSpec((1, GLA_WIDTH), lambda i: (0, 0)),
        ],
        out_shape=[
            jax.ShapeDtypeStruct((T, P_A), BF16),
            jax.ShapeDtypeStruct((LANE, GLA_QK), F32),
            jax.ShapeDtypeStruct((1, GLA_QK), F32),
            jax.ShapeDtypeStruct((1, GLA_WIDTH), F32),
        ],
        scratch_shapes=[pltpu.VMEM((GLA_DV, GLA_QK), F32)],
        compiler_params=_params(),
        name="gla_bwd",
    )(proj, w2p, bg, gn, opre, sprev, dmixed)


def _t5_bucket(dist):
    max_exact = REL_BUCKETS // 2
    n = np.maximum(dist, 0)
    large = max_exact + (np.log(np.maximum(n, 1) / max_exact) / math.log(REL_MAX_DIST / max_exact)
                         * (REL_BUCKETS - max_exact)).astype(np.int32)
    large = np.minimum(large, REL_BUCKETS - 1)
    return np.where(n < max_exact, n, large).astype(np.int32)


def _bucket_ids():
    L = DSA_BLOCK
    steps = L + np.arange(L)[:, None] - np.arange(2 * L)[None, :]
    in_band = (steps >= 0) & (steps <= DSA_SPAN)
    return np.stack([np.where(in_band, _t5_bucket(steps * d), -1) for d in DSA_DILATIONS]).astype(np.int32)


def bias_tables(rel_bias):
    ids = jnp.asarray(_bucket_ids())
    nd = len(DSA_DILATIONS)

    def body(rel_ref, ids_ref, tab_ref):
        h = pl.program_id(1)
        idt = ids_ref[0]
        acc = jnp.where(idt < 0, NEG, 0.0).astype(F32)
        for b in range(REL_BUCKETS):
            acc = jnp.where(idt == b, rel_ref[b, h], acc)
        tab_ref[0, 0] = acc

    return pl.pallas_call(
        body,
        grid=(nd, DSA_HEADS),
        in_specs=[pl.BlockSpec(memory_space=pltpu.SMEM), pl.BlockSpec((1, DSA_BLOCK, 2 * DSA_BLOCK), lambda d, h: (d, 0, 0))],
        out_specs=pl.BlockSpec((1, 1, DSA_BLOCK, 2 * DSA_BLOCK), lambda d, h: (d, h, 0, 0)),
        out_shape=jax.ShapeDtypeStruct((nd, DSA_HEADS, DSA_BLOCK, 2 * DSA_BLOCK), F32),
        compiler_params=_params(("arbitrary", "arbitrary")),
        name="bias_tables",
    )(rel_bias, ids)


def bias_tables_bwd(dtab):
    ids = jnp.asarray(_bucket_ids())
    nd = len(DSA_DILATIONS)

    def body(dtab_ref, ids_ref, drel_ref):
        @pl.when((pl.program_id(0) == 0) & (pl.program_id(1) == 0))
        def _():
            for b in range(REL_BUCKETS):
                for h in range(DSA_HEADS):
                    drel_ref[b, h] = 0.0

        h = pl.program_id(1)
        idt = ids_ref[0]
        g = dtab_ref[0, 0]
        for b in range(REL_BUCKETS):
            drel_ref[b, h] += jnp.sum(jnp.where(idt == b, g, 0.0))

    return pl.pallas_call(
        body,
        grid=(nd, DSA_HEADS),
        in_specs=[pl.BlockSpec((1, 1, DSA_BLOCK, 2 * DSA_BLOCK), lambda d, h: (d, h, 0, 0)),
                  pl.BlockSpec((1, DSA_BLOCK, 2 * DSA_BLOCK), lambda d, h: (d, 0, 0))],
        out_specs=pl.BlockSpec(memory_space=pltpu.SMEM),
        out_shape=jax.ShapeDtypeStruct((REL_BUCKETS, DSA_HEADS), F32),
        compiler_params=_params(("arbitrary", "arbitrary")),
        name="bias_tables_bwd",
    )(dtab, ids)


DSA_PAIRS = DSA_HEADS // 2
DSA_COMBINE_ROWS = 256


def _dsa_units(d):
    return d, DSA_SUPER // (DSA_BLOCK * d)


def _dsa_specs(T):
    nsb = T // DSA_SUPER
    qcol, kcol, vcol = P_DQ // LANE, P_DK // LANE, P_DV // LANE
    return nsb, qcol, kcol, vcol


def _head_lane_mask():
    return lax.broadcasted_iota(jnp.int32, (1, LANE), 1) < DSA_DH


def _first_block_penalty(first):
    col = lax.broadcasted_iota(jnp.int32, (DSA_BLOCK, 2 * DSA_BLOCK), 1)
    return jnp.where(first & (col < DSA_BLOCK), NEG, 0.0).astype(F32)


def dsa_fwd(proj, tab):
    T = proj.shape[0]
    nsb, qcol, kcol, vcol = _dsa_specs(T)
    S = DSA_SUPER

    def body(q_ref, kp_ref, kc_ref, vp_ref, vc_ref, tab_ref, out_ref, lse_ref, kk, vv, ob, lb):
        sb = pl.program_id(1)
        kk[0:S, :] = kp_ref[...]
        kk[S:2 * S, :] = kc_ref[...]
        vv[0:S, :] = vp_ref[...]
        vv[S:2 * S, :] = vc_ref[...]
        head0 = _head_lane_mask()

        for di, d in enumerate(DSA_DILATIONS):
            n_res, n_blk = _dsa_units(d)

            def unit(u, carry, di=di, d=d, n_blk=n_blk):
                r = u // n_blk
                c = u % n_blk
                q0 = r + d * DSA_BLOCK * c
                qrows = pl.ds(q0, DSA_BLOCK, stride=d) if d > 1 else pl.ds(q0, DSA_BLOCK)
                krows = pl.ds(S + q0 - d * DSA_BLOCK, 2 * DSA_BLOCK, stride=d) if d > 1 else pl.ds(S + q0 - DSA_BLOCK, 2 * DSA_BLOCK)
                q2 = q_ref[qrows, :] * QK_SCALE
                k2 = kk[krows, :].astype(BF16)
                v2 = vv[krows, :].astype(BF16)
                pen = _first_block_penalty((sb == 0) & (c == 0))
                o_pair = None
                l_pair = None
                for hh in range(2):
                    sel = head0 if hh == 0 else jnp.logical_not(head0)
                    q_h = jnp.where(sel, q2, 0.0).astype(BF16)
                    s = _dot_nt(q_h, k2) + (tab_ref[di, hh] + pen)
                    m = jnp.max(s, axis=-1, keepdims=True)
                    p = jnp.exp(s - m)
                    den = jnp.sum(p, axis=-1, keepdims=True)
                    o_h = _dot(p.astype(BF16), v2) / den
                    l_h = jnp.broadcast_to(m + jnp.log(den), (DSA_BLOCK, LANE))
                    o_pair = o_h if hh == 0 else jnp.where(head0, o_pair, o_h)
                    l_pair = l_h if hh == 0 else jnp.where(head0, l_pair, l_h)
                ob[di, qrows, :] = o_pair
                lb[di, qrows, :] = l_pair
                return carry

            lax.fori_loop(0, n_res * n_blk, unit, 0)

        def combine(i, carry):
            rows = pl.ds(pl.multiple_of(i * DSA_COMBINE_ROWS, DSA_COMBINE_ROWS), DSA_COMBINE_ROWS)
            l0, l1, l2 = lb[0, rows, :], lb[1, rows, :], lb[2, rows, :]
            mx = jnp.maximum(jnp.maximum(l0, l1), l2)
            e0, e1, e2 = jnp.exp(l0 - mx), jnp.exp(l1 - mx), jnp.exp(l2 - mx)
            den = e0 + e1 + e2
            out_ref[rows, :] = (e0 * ob[0, rows, :] + e1 * ob[1, rows, :] + e2 * ob[2, rows, :]) / den
            lse_ref[rows, :] = mx + jnp.log(den)
            return carry

        lax.fori_loop(0, S // DSA_COMBINE_ROWS, combine, 0)

    prev = lambda col: (lambda hp, sb: (jnp.maximum(sb - 1, 0), col + hp))
    cur = lambda col: (lambda hp, sb: (sb, col + hp))
    blk = lambda f: pl.BlockSpec((S, LANE), f)
    return pl.pallas_call(
        body,
        grid=(DSA_PAIRS, nsb),
        in_specs=[blk(cur(qcol)), blk(prev(kcol)), blk(cur(kcol)), blk(prev(vcol)), blk(cur(vcol)),
                  pl.BlockSpec((len(DSA_DILATIONS), 2, DSA_BLOCK, 2 * DSA_BLOCK), lambda hp, sb: (0, hp, 0, 0))],
        out_specs=[blk(lambda hp, sb: (sb, hp)), blk(lambda hp, sb: (sb, hp))],
        out_shape=[jax.ShapeDtypeStruct((T, DSA_WIDTH), F32), jax.ShapeDtypeStruct((T, DSA_WIDTH), F32)],
        scratch_shapes=[pltpu.VMEM((2 * S, LANE), F32), pltpu.VMEM((2 * S, LANE), F32),
                        pltpu.VMEM((len(DSA_DILATIONS), S, LANE), F32), pltpu.VMEM((len(DSA_DILATIONS), S, LANE), F32)],
        compiler_params=_params(("arbitrary", "arbitrary")),
        name="dsa_fwd",
    )(proj, proj, proj, proj, proj, tab)


def dsa_bwd(proj, tab, ob_out, lse, dmixed):
    T = proj.shape[0]
    nsb, qcol, kcol, vcol = _dsa_specs(T)
    S = DSA_SUPER
    nd = len(DSA_DILATIONS)
    ocol = GLA_WIDTH // LANE

    def body(q_ref, kp_ref, kc_ref, vp_ref, vc_ref, tab_ref, o_ref, lse_ref, do_ref,
             dq_ref, dk_ref, dv_ref, dtab_ref, kk, vv, dqa, dkk, dvv):
        j = pl.program_id(1)
        sb = nsb - 1 - j
        kk[0:S, :] = kp_ref[...]
        kk[S:2 * S, :] = kc_ref[...]
        vv[0:S, :] = vp_ref[...]
        vv[S:2 * S, :] = vc_ref[...]
        head0 = _head_lane_mask()

        @pl.when(j == 0)
        def _():
            dtab_ref[...] = jnp.zeros_like(dtab_ref)
            dkk[S:2 * S, :] = jnp.zeros((S, LANE), F32)
            dvv[S:2 * S, :] = jnp.zeros((S, LANE), F32)

        @pl.when(j > 0)
        def _():
            dkk[S:2 * S, :] = dkk[0:S, :]
            dvv[S:2 * S, :] = dvv[0:S, :]

        dkk[0:S, :] = jnp.zeros((S, LANE), F32)
        dvv[0:S, :] = jnp.zeros((S, LANE), F32)
        dqa[...] = jnp.zeros_like(dqa)

        for di, d in enumerate(DSA_DILATIONS):
            n_res, n_blk = _dsa_units(d)

            def unit(u, carry, di=di, d=d, n_blk=n_blk):
                r = u // n_blk
                c = u % n_blk
                q0 = r + d * DSA_BLOCK * c
                qrows = pl.ds(q0, DSA_BLOCK, stride=d) if d > 1 else pl.ds(q0, DSA_BLOCK)
                krows = pl.ds(S + q0 - d * DSA_BLOCK, 2 * DSA_BLOCK, stride=d) if d > 1 else pl.ds(S + q0 - DSA_BLOCK, 2 * DSA_BLOCK)
                q2 = q_ref[qrows, :] * QK_SCALE
                k2 = kk[krows, :].astype(BF16)
                v2 = vv[krows, :].astype(BF16)
                do2 = do_ref[qrows, :]
                o2 = o_ref[qrows, :]
                l2 = lse_ref[qrows, :]
                pen = _first_block_penalty((sb == 0) & (c == 0))
                dq_pair = None
                dk_pair = None
                dv_pair = None
                for hh in range(2):
                    sel = head0 if hh == 0 else jnp.logical_not(head0)
                    q_h = jnp.where(sel, q2, 0.0).astype(BF16)
                    do_h = jnp.where(sel, do2, 0.0)
                    do_b = do_h.astype(BF16)
                    delta = jnp.sum(do_h * o2, axis=-1, keepdims=True)
                    lse_h = jnp.max(jnp.where(sel, l2, -jnp.inf), axis=-1, keepdims=True)
                    s = _dot_nt(q_h, k2) + (tab_ref[di, hh] + pen)
                    p = jnp.exp(s - lse_h)
                    dp = _dot_nt(do_b, v2)
                    ds = p * (dp - delta)
                    dtab_ref[di, hh] += ds
                    ds_b = ds.astype(BF16)
                    dq_h = _dot(ds_b, k2)
                    dk_h = _dot_tn(ds_b, q_h)
                    dv_h = _dot_tn(p.astype(BF16), do_b)
                    dq_pair = dq_h if hh == 0 else jnp.where(head0, dq_pair, dq_h)
                    dk_pair = dk_h if hh == 0 else dk_pair + dk_h
                    dv_pair = dv_h if hh == 0 else dv_pair + dv_h
                dqa[qrows, :] += dq_pair * QK_SCALE
                dkk[krows, :] += dk_pair
                dvv[krows, :] += dv_pair
                return carry

            lax.fori_loop(0, n_res * n_blk, unit, 0)

        dq_ref[...] = dqa[...].astype(BF16)
        dk_ref[...] = dkk[S:2 * S, :].astype(BF16)
        dv_ref[...] = dvv[S:2 * S, :].astype(BF16)

    prev = lambda col: (lambda hp, j: (jnp.maximum(nsb - 2 - j, 0), col + hp))
    cur = lambda col: (lambda hp, j: (nsb - 1 - j, col + hp))
    blk = lambda f: pl.BlockSpec((S, LANE), f)
    out_blk = blk(lambda hp, j: (nsb - 1 - j, hp))
    tab_blk = pl.BlockSpec((nd, 2, DSA_BLOCK, 2 * DSA_BLOCK), lambda hp, j: (0, hp, 0, 0))
    return pl.pallas_call(
        body,
        grid=(DSA_PAIRS, nsb),
        in_specs=[blk(cur(qcol)), blk(prev(kcol)), blk(cur(kcol)), blk(prev(vcol)), blk(cur(vcol)), tab_blk,
                  out_blk, out_blk, blk(cur(ocol))],
        out_specs=[out_blk, out_blk, out_blk, tab_blk],
        out_shape=[jax.ShapeDtypeStruct((T, DSA_WIDTH), BF16)] * 3
        + [jax.ShapeDtypeStruct((nd, DSA_HEADS, DSA_BLOCK, 2 * DSA_BLOCK), F32)],
        scratch_shapes=[pltpu.VMEM((2 * S, LANE), F32), pltpu.VMEM((2 * S, LANE), F32), pltpu.VMEM((S, LANE), F32),
                        pltpu.VMEM((2 * S, LANE), F32), pltpu.VMEM((2 * S, LANE), F32)],
        compiler_params=_params(("arbitrary", "arbitrary")),
        name="dsa_bwd",
    )(proj, proj, proj, proj, proj, tab, ob_out, lse, dmixed)


FF_BLOCKS = 4
FF_BLOCK = D_FF // FF_BLOCKS


def post_fused(x, oa, ob, tgt, g2, gf, wout, wff1, wff2):
    T = x.shape[0]
    tm = 256
    inv_d = 1.0 / D_MODEL

    def body(x_ref, oa_ref, ob_ref, tgt_ref, g2_ref, gf_ref, wout_hbm, wff1_hbm, wff2_hbm,
             mixed_ref, nm_ref, a_ref, dpre_ref, dh2_ref, dh1_ref, dmixed_ref, loss_ref, dgf_ref, dg2_ref,
             wout_v, wff1_v, wff2_v, sems):
        @pl.when(pl.program_id(0) == 0)
        def _():
            cps = [pltpu.make_async_copy(s, d, sems.at[i])
                   for i, (s, d) in enumerate([(wout_hbm, wout_v), (wff1_hbm, wff1_v), (wff2_hbm, wff2_v)])]
            for cp in cps:
                cp.start()
            for cp in cps:
                cp.wait()
            loss_ref[...] = jnp.zeros_like(loss_ref)
            dgf_ref[...] = jnp.zeros_like(dgf_ref)
            dg2_ref[...] = jnp.zeros_like(dg2_ref)

        mixed = jnp.concatenate([oa_ref[...], ob_ref[...].astype(BF16)], axis=1)
        mixed_ref[...] = mixed
        h1 = x_ref[...] + _dot(mixed, wout_v[...])
        rs1 = _rstd(h1)
        hn1 = h1 * rs1
        g2 = g2_ref[...]
        nm = (hn1 * g2).astype(BF16)
        nm_ref[...] = nm
        relu = []
        mlp = jnp.zeros((tm, D_MODEL), F32)
        for j in range(FF_BLOCKS):
            cols = slice(j * FF_BLOCK, (j + 1) * FF_BLOCK)
            r_j = jnp.maximum(_dot(nm, wff1_v[j]), 0.0)
            a_j = (r_j * r_j).astype(BF16)
            a_ref[:, cols] = a_j
            relu.append(r_j)
            mlp = mlp + _dot(a_j, wff2_v[cols, :])
        h2 = h1 + mlp
        rsf = _rstd(h2)
        hnf = h2 * rsf
        gf = gf_ref[...]
        diff = hnf * gf - tgt_ref[...]
        loss_ref[...] += 0.5 * jnp.sum(jnp.sum(diff * diff, axis=-1, keepdims=True) * inv_d, axis=0, keepdims=True)
        dy = diff * inv_d
        dgf_ref[...] += jnp.sum(dy * hnf, axis=0, keepdims=True)
        dhnf = dy * gf
        dh2 = rsf * (dhnf - hnf * jnp.mean(dhnf * hnf, axis=-1, keepdims=True))
        dh2_b = dh2.astype(BF16)
        dh2_ref[...] = dh2_b
        dnm = jnp.zeros((tm, D_MODEL), F32)
        for j in range(FF_BLOCKS):
            cols = slice(j * FF_BLOCK, (j + 1) * FF_BLOCK)
            dpre_j = (_dot_nt(dh2_b, wff2_v[cols, :]) * (2.0 * relu[j])).astype(BF16)
            dpre_ref[:, cols] = dpre_j
            dnm = dnm + _dot_nt(dpre_j, wff1_v[j])
        dg2_ref[...] += jnp.sum(dnm * hn1, axis=0, keepdims=True)
        dhn1 = dnm * g2
        dh1 = dh2 + rs1 * (dhn1 - hn1 * jnp.mean(dhn1 * hn1, axis=-1, keepdims=True))
        dh1_ref[...] = dh1
        dmixed_ref[...] = _dot_nt(dh1.astype(BF16), wout_v[...])

    row = lambda w: pl.BlockSpec((tm, w), lambda i: (i, 0))
    vec = lambda w: pl.BlockSpec((1, w), lambda i: (0, 0))
    return pl.pallas_call(
        body,
        grid=(T // tm,),
        in_specs=[row(D_MODEL), row(GLA_WIDTH), row(DSA_WIDTH), row(D_MODEL), vec(D_MODEL), vec(D_MODEL), ANY, ANY, ANY],
        out_specs=[row(D_MODEL), row(D_MODEL), row(D_FF), row(D_FF), row(D_MODEL), row(D_MODEL), row(D_MODEL),
                   vec(1), vec(D_MODEL), vec(D_MODEL)],
        out_shape=[
            jax.ShapeDtypeStruct((T, D_MODEL), BF16),
            jax.ShapeDtypeStruct((T, D_MODEL), BF16),
            jax.ShapeDtypeStruct((T, D_FF), BF16),
            jax.ShapeDtypeStruct((T, D_FF), BF16),
            jax.ShapeDtypeStruct((T, D_MODEL), BF16),
            jax.ShapeDtypeStruct((T, D_MODEL), F32),
            jax.ShapeDtypeStruct((T, D_MODEL), F32),
            jax.ShapeDtypeStruct((1, 1), F32),
            jax.ShapeDtypeStruct((1, D_MODEL), F32),
            jax.ShapeDtypeStruct((1, D_MODEL), F32),
        ],
        scratch_shapes=[pltpu.VMEM((D_MODEL, D_MODEL), BF16), pltpu.VMEM((FF_BLOCKS, D_MODEL, FF_BLOCK), BF16),
                        pltpu.VMEM((D_FF, D_MODEL), BF16), pltpu.SemaphoreType.DMA((3,))],
        compiler_params=_params(),
        name="post_fused",
    )(x, oa, ob, tgt, g2, gf, wout, wff1, wff2)


WGRAD_TOKENS = 1024


def wgrad(a, b, name, bm=None, bn=None, col_blocked=False):
    T, M = a.shape
    N = b.shape[1]
    bm = M if bm is None else bm
    bn = N if bn is None else bn
    tk = min(WGRAD_TOKENS, T)

    def body(a_ref, b_ref, o_ref):
        part = _dot_tn(a_ref[...].astype(BF16), b_ref[...].astype(BF16))
        out = o_ref.at[0] if col_blocked else o_ref

        @pl.when(pl.program_id(2) == 0)
        def _():
            out[...] = part

        @pl.when(pl.program_id(2) > 0)
        def _():
            out[...] += part

    if col_blocked:
        assert bm == M
        out_spec = pl.BlockSpec((1, M, bn), lambda i, j, k: (j, 0, 0))
        out_shape = jax.ShapeDtypeStruct((N // bn, M, bn), F32)
    else:
        out_spec = pl.BlockSpec((bm, bn), lambda i, j, k: (i, j))
        out_shape = jax.ShapeDtypeStruct((M, N), F32)
    return pl.pallas_call(
        body,
        grid=(M // bm, N // bn, T // tk),
        in_specs=[pl.BlockSpec((tk, bm), lambda i, j, k: (k, i)), pl.BlockSpec((tk, bn), lambda i, j, k: (k, j))],
        out_specs=out_spec,
        out_shape=out_shape,
        compiler_params=_params(("arbitrary", "arbitrary", "arbitrary")),
        name=name,
    )(a, b)


def dx_final(x, dh1, g1, da, dq, dk, dv, wp):
    T = x.shape[0]
    tm = 256

    def body(x_ref, dh1_ref, g_ref, da_ref, dq_ref, dk_ref, dv_ref, w_hbm, dx_ref, dg_ref, w_vmem, sem):
        @pl.when(pl.program_id(0) == 0)
        def _():
            _load_once(w_hbm, w_vmem, sem)
            dg_ref[...] = jnp.zeros_like(dg_ref)

        dnx = (_dot_nt(da_ref[...], w_vmem[:, 0:P_A]) + _dot_nt(dq_ref[...], w_vmem[:, P_DQ:P_DQ + DSA_WIDTH])
               + _dot_nt(dk_ref[...], w_vmem[:, P_DK:P_DK + DSA_WIDTH]) + _dot_nt(dv_ref[...], w_vmem[:, P_DV:P_DV + DSA_WIDTH]))
        xf = x_ref[...]
        rs = _rstd(xf)
        hn = xf * rs
        dg_ref[...] += jnp.sum(dnx * hn, axis=0, keepdims=True)
        dhn = dnx * g_ref[...]
        dx_ref[...] = dh1_ref[...] + rs * (dhn - hn * jnp.mean(dhn * hn, axis=-1, keepdims=True))

    row = lambda w: pl.BlockSpec((tm, w), lambda i: (i, 0))
    vec = pl.BlockSpec((1, D_MODEL), lambda i: (0, 0))
    return pl.pallas_call(
        body,
        grid=(T // tm,),
        in_specs=[row(D_MODEL), row(D_MODEL), vec, row(P_A), row(DSA_WIDTH), row(DSA_WIDTH), row(DSA_WIDTH), ANY],
        out_specs=[row(D_MODEL), vec],
        out_shape=[jax.ShapeDtypeStruct((T, D_MODEL), F32), jax.ShapeDtypeStruct((1, D_MODEL), F32)],
        scratch_shapes=[pltpu.VMEM((D_MODEL, P_ALL), BF16), pltpu.SemaphoreType.DMA],
        compiler_params=_params(),
        name="dx_final",
    )(x, dh1, g1, da, dq, dk, dv, wp)


def adamw(w, g, m, v, name):
    R, C = w.shape
    br = 256 if R % 256 == 0 else R

    def body(w_ref, g_ref, m_ref, v_ref, d_ref, nm_ref, nv_ref):
        g = g_ref[...]
        m_new = ADAM_B1 * m_ref[...] + (1.0 - ADAM_B1) * g
        v_new = ADAM_B2 * v_ref[...] + (1.0 - ADAM_B2) * (g * g)
        m_hat = m_new / (1.0 - ADAM_B1 ** ADAM_STEP)
        v_hat = v_new / (1.0 - ADAM_B2 ** ADAM_STEP)
        d_ref[...] = -ADAM_LR * (m_hat / (jnp.sqrt(v_hat) + ADAM_EPS) + ADAM_WD * w_ref[...])
        nm_ref[...] = m_new
        nv_ref[...] = v_new

    spec = pl.BlockSpec((br, C), lambda i: (i, 0))
    return pl.pallas_call(
        body,
        grid=(R // br,),
        in_specs=[spec] * 4,
        out_specs=[spec] * 3,
        out_shape=[jax.ShapeDtypeStruct((R, C), F32)] * 3,
        compiler_params=_params(),
        name=name,
    )(w, g, m, v)


def _place():
    return lax.axis_index("x"), lax.axis_index("y"), lax.axis_index("c")


def _other_chips(x, y):
    return [(1 - x, y), (x, 1 - y), (1 - x, 1 - y)]


def gather_weights(shards):
    n = len(shards)

    def body(*refs):
        ins, outs = refs[:n], refs[n:2 * n]
        send_sems, recv_sems, local_sems = refs[2 * n:]
        x, y, c = _place()
        me, sib = (x, y, c), (x, y, 1 - c)
        mine = 2 * x + y
        chips = _other_chips(x, y)

        def region(a, owner, half):
            h = ins[a].shape[0] // 2
            return outs[a].at[owner, pl.ds(half * h, h)]

        def copy(a, k, owner, half, to, src=None):
            return pltpu.make_async_remote_copy(
                src_ref=region(a, owner, half) if src is None else src, dst_ref=region(a, owner, half),
                send_sem=send_sems.at[a, k], recv_sem=recv_sems.at[a, k], device_id=to, device_id_type=MESH)

        own = [pltpu.make_async_copy(ins[a], outs[a].at[mine], local_sems.at[a]) for a in range(n)]
        for cp in own:
            cp.start()
        first = []
        for a in range(n):
            h = ins[a].shape[0] // 2
            for t, (cx, cy) in enumerate(chips):
                first.append(copy(a, t, mine, c, (cx, cy, c), src=ins[a].at[pl.ds(c * h, h)]))
        for cp in first:
            cp.start()
        passed = []
        for a in range(n):
            for t, (cx, cy) in enumerate(chips):
                copy(a, t, 2 * cx + cy, c, me).wait_recv()
                fw = copy(a, 3 + t, 2 * cx + cy, c, sib)
                fw.start()
                passed.append(fw)
        for a in range(n):
            for t, (cx, cy) in enumerate(chips):
                copy(a, 3 + t, 2 * cx + cy, 1 - c, me).wait_recv()
        for cp in first + passed:
            cp.wait_send()
        for cp in own:
            cp.wait()

    return pl.pallas_call(
        body,
        in_specs=[ANY] * n,
        out_specs=[ANY] * n,
        out_shape=[jax.ShapeDtypeStruct((4,) + s.shape, s.dtype) for s in shards],
        scratch_shapes=[pltpu.SemaphoreType.DMA((n, 6)), pltpu.SemaphoreType.DMA((n, 6)), pltpu.SemaphoreType.DMA((n,))],
        name="gather_weights",
    )(*shards)


def swap_halves(grads):
    n = len(grads)

    def body(*refs):
        ins, outs = refs[:n], refs[n:2 * n]
        send_sems, recv_sems = refs[2 * n:]
        x, y, c = _place()
        cps = []
        for a in range(n):
            h = ins[a].shape[1] // 2
            cps.append(pltpu.make_async_remote_copy(
                src_ref=ins[a].at[:, pl.ds((1 - c) * h, h)], dst_ref=outs[a],
                send_sem=send_sems.at[a], recv_sem=recv_sems.at[a], device_id=(x, y, 1 - c), device_id_type=MESH))
        for cp in cps:
            cp.start()
        for cp in cps:
            cp.wait()

    return pl.pallas_call(
        body,
        in_specs=[ANY] * n,
        out_specs=[ANY] * n,
        out_shape=[jax.ShapeDtypeStruct((4, g.shape[1] // 2, g.shape[2]), g.dtype) for g in grads],
        scratch_shapes=[pltpu.SemaphoreType.DMA((n,)), pltpu.SemaphoreType.DMA((n,))],
        name="swap_halves",
    )(*grads)


def add_halves(g, got, core, name):
    _, R, C = g.shape
    h = R // 2
    br = 128
    nb = h // br

    def body(core_ref, g_ref, got_ref, o_ref):
        o_ref[...] = (g_ref[...] + got_ref[...]).astype(BF16)

    return pl.pallas_call(
        body,
        grid_spec=pltpu.PrefetchScalarGridSpec(
            num_scalar_prefetch=1,
            grid=(4, nb),
            in_specs=[pl.BlockSpec((1, br, C), lambda s, i, core: (s, core[0] * nb + i, 0)),
                      pl.BlockSpec((1, br, C), lambda s, i, core: (s, i, 0))],
            out_specs=pl.BlockSpec((1, br, C), lambda s, i, core: (s, i, 0)),
        ),
        out_shape=jax.ShapeDtypeStruct((4, h, C), BF16),
        compiler_params=_params(("arbitrary", "arbitrary")),
        name=name,
    )(core, g, got)


def scatter_chips(sums):
    n = len(sums)

    def body(*refs):
        ins, outs = refs[:n], refs[n:2 * n]
        send_sems, recv_sems, local_sems = refs[2 * n:]
        x, y, c = _place()
        mine = 2 * x + y
        chips = _other_chips(x, y)
        own = [pltpu.make_async_copy(ins[a].at[mine], outs[a].at[mine], local_sems.at[a]) for a in range(n)]
        for cp in own:
            cp.start()
        cps = []
        for a in range(n):
            for t, (cx, cy) in enumerate(chips):
                cps.append(pltpu.make_async_remote_copy(
                    src_ref=ins[a].at[2 * cx + cy], dst_ref=outs[a].at[mine],
                    send_sem=send_sems.at[a, t], recv_sem=recv_sems.at[a, t], device_id=(cx, cy, c), device_id_type=MESH))
        for cp in cps:
            cp.start()
        for a in range(n):
            for t, (cx, cy) in enumerate(chips):
                pltpu.make_async_remote_copy(
                    src_ref=ins[a].at[mine], dst_ref=outs[a].at[2 * cx + cy],
                    send_sem=send_sems.at[a, t], recv_sem=recv_sems.at[a, t], device_id=(x, y, c), device_id_type=MESH).wait_recv()
        for cp in cps:
            cp.wait_send()
        for cp in own:
            cp.wait()

    return pl.pallas_call(
        body,
        in_specs=[ANY] * n,
        out_specs=[ANY] * n,
        out_shape=[jax.ShapeDtypeStruct(s.shape, s.dtype) for s in sums],
        scratch_shapes=[pltpu.SemaphoreType.DMA((n, 3)), pltpu.SemaphoreType.DMA((n, 3)), pltpu.SemaphoreType.DMA((n,))],
        name="scatter_chips",
    )(*sums)


def sum_slots(parts, name):
    S, R, C = parts.shape
    br = 128 if R % 128 == 0 else R

    def body(p_ref, o_ref):
        acc = p_ref[0].astype(F32)
        for s in range(1, S):
            acc = acc + p_ref[s].astype(F32)
        o_ref[...] = acc

    return pl.pallas_call(
        body,
        grid=(R // br,),
        in_specs=[pl.BlockSpec((S, br, C), lambda i: (0, i, 0))],
        out_specs=pl.BlockSpec((br, C), lambda i: (i, 0)),
        out_shape=jax.ShapeDtypeStruct((R, C), F32),
        compiler_params=_params(),
        name=name,
    )(parts)


def join_halves(halves):
    n = len(halves)

    def body(*refs):
        ins, outs = refs[:n], refs[n:2 * n]
        send_sems, recv_sems, local_sems = refs[2 * n:]
        x, y, c = _place()
        own, cps = [], []
        for a in range(n):
            h = ins[a].shape[0]
            dst = outs[a].at[pl.ds(c * h, h)]
            own.append(pltpu.make_async_copy(ins[a], dst, local_sems.at[a]))
            cps.append(pltpu.make_async_remote_copy(
                src_ref=ins[a], dst_ref=dst, send_sem=send_sems.at[a], recv_sem=recv_sems.at[a],
                device_id=(x, y, 1 - c), device_id_type=MESH))
        for cp in own + cps:
            cp.start()
        for a in range(n):
            h = ins[a].shape[0]
            pltpu.make_async_remote_copy(
                src_ref=ins[a], dst_ref=outs[a].at[pl.ds((1 - c) * h, h)], send_sem=send_sems.at[a], recv_sem=recv_sems.at[a],
                device_id=(x, y, c), device_id_type=MESH).wait_recv()
        for cp in cps:
            cp.wait_send()
        for cp in own:
            cp.wait()

    return pl.pallas_call(
        body,
        in_specs=[ANY] * n,
        out_specs=[ANY] * n,
        out_shape=[jax.ShapeDtypeStruct((2 * s.shape[0], s.shape[1]), s.dtype) for s in halves],
        scratch_shapes=[pltpu.SemaphoreType.DMA((n,)), pltpu.SemaphoreType.DMA((n,)), pltpu.SemaphoreType.DMA((n,))],
        name="join_halves",
    )(*halves)


SMALL_ROWS = 64


def gather_small(vec):
    def body(v_ref, o_ref, send_sems, recv_sems, local_sem):
        x, y, c = _place()
        flips = [(fx, fy, fc) for fx in (0, 1) for fy in (0, 1) for fc in (0, 1)][1:]

        def peer(f):
            return (1 - x if f[0] else x, 1 - y if f[1] else y, 1 - c if f[2] else c)

        slot = lambda p: 4 * p[0] + 2 * p[1] + p[2]
        own = pltpu.make_async_copy(v_ref, o_ref.at[slot((x, y, c))], local_sem)
        own.start()
        cps = [pltpu.make_async_remote_copy(
            src_ref=v_ref, dst_ref=o_ref.at[slot((x, y, c))], send_sem=send_sems.at[k], recv_sem=recv_sems.at[k],
            device_id=peer(f), device_id_type=MESH) for k, f in enumerate(flips)]
        for cp in cps:
            cp.start()
        for k, f in enumerate(flips):
            pltpu.make_async_remote_copy(
                src_ref=v_ref, dst_ref=o_ref.at[slot(peer(f))], send_sem=send_sems.at[k], recv_sem=recv_sems.at[k],
                device_id=(x, y, c), device_id_type=MESH).wait_recv()
        for cp in cps:
            cp.wait_send()
        own.wait()

    return pl.pallas_call(
        body,
        in_specs=[ANY],
        out_specs=ANY,
        out_shape=jax.ShapeDtypeStruct((8,) + vec.shape, vec.dtype),
        scratch_shapes=[pltpu.SemaphoreType.DMA((7,)), pltpu.SemaphoreType.DMA((7,)), pltpu.SemaphoreType.DMA],
        name="gather_small",
    )(vec)


GLOW_PAD = LANE - GLA_RANK


def kernel(x, attn_norm_g, w_in, gla_gate_w2, gla_gate_b, gla_norm_g, rel_bias, w_out, mlp_norm_g, w_ff1, w_ff2, final_norm_g, loss_target, m_attn_norm_g, m_w_in, m_gla_gate_w2, m_gla_gate_b, m_gla_norm_g, m_rel_bias, m_w_out, m_mlp_norm_g, m_w_ff1, m_w_ff2, m_final_norm_g, v_attn_norm_g, v_w_in, v_gla_gate_w2, v_gla_gate_b, v_gla_norm_g, v_rel_bias, v_w_out, v_mlp_norm_g, v_w_ff1, v_w_ff2, v_final_norm_g):
    xs, tgt = x[0], loss_target[0]
    T = xs.shape[0]
    cx, cy, cc = _place()
    chip = 2 * cx + cy
    gf = final_norm_g.reshape(1, D_MODEL)

    win_g, wout_g, wff1, wff2_g, w2_g = gather_weights(
        [w_in[0].astype(BF16), w_out[0].astype(BF16), w_ff1[0].astype(BF16), w_ff2[0].astype(BF16), gla_gate_w2[0]])
    win = jnp.transpose(win_g, (1, 0, 2)).reshape(D_MODEL, D_IN)
    n_glow = R_GLOW + GLA_RANK
    wp = jnp.concatenate([win[:, :n_glow], jnp.zeros((D_MODEL, GLOW_PAD), BF16), win[:, n_glow:]], axis=1)
    wout = wout_g.reshape(D_MODEL, D_MODEL)
    wff2 = wff2_g.reshape(D_FF, D_MODEL)
    w2 = jnp.transpose(w2_g, (1, 0, 2)).reshape(GLA_RANK, GLA_QK)
    w2p = jnp.concatenate([w2, jnp.zeros((GLOW_PAD, GLA_QK), F32)], axis=0)

    proj, nx = inproj(xs, attn_norm_g, wp)
    oa, opre, sprev = gla_fwd(proj, w2p, gla_gate_b, gla_norm_g)
    tab = bias_tables(rel_bias)
    ob, lse = dsa_fwd(proj, tab)
    mixed, nm, act, dpre, dh2, dh1, dmixed, loss, dgf, dg2 = post_fused(xs, oa, ob, tgt, mlp_norm_g, gf, wout, wff1, wff2)
    da, dw2p, dbg, dgn = gla_bwd(proj, w2p, gla_gate_b, gla_norm_g, opre, sprev, dmixed)
    dq, dk, dv, dtab = dsa_bwd(proj, tab, ob, lse, dmixed)
    drel = bias_tables_bwd(dtab)
    dxs, dg1 = dx_final(xs, dh1, attn_norm_g, da, dq, dk, dv, wp)

    dwff2 = wgrad(act, dh2, "wgrad_ff2", bm=FF_BLOCK)
    dwff1 = wgrad(nm, dpre, "wgrad_ff1", bn=FF_BLOCK, col_blocked=True)
    dwout = wgrad(mixed, dh1, "wgrad_out")
    dwa = wgrad(nx, da, "wgrad_in_gla")
    dwq = wgrad(nx, dq, "wgrad_in_q")
    dwk = wgrad(nx, dk, "wgrad_in_k")
    dwv = wgrad(nx, dv, "wgrad_in_v")
    dwin = jnp.concatenate([dwa[:, :n_glow], dwq, dwk, dwv], axis=1)
    big = [
        jnp.transpose(dwin.reshape(D_MODEL, 4, D_IN // 4), (1, 0, 2)),
        dwout.reshape(4, D_MODEL // 4, D_MODEL),
        dwff1,
        dwff2.reshape(4, FF_BLOCK, D_MODEL),
    ]

    names = ["in", "out", "ff1", "ff2"]
    got = swap_halves(big)
    core = cc.astype(jnp.int32).reshape(1)
    sums = [add_halves(g, r, core, "add_halves_" + s) for g, r, s in zip(big, got, names)]
    slots = scatter_chips(sums)
    halves = [sum_slots(p, "sum_chips_" + s) for p, s in zip(slots, names)]
    g_win, g_wout, g_wff1, g_wff2 = join_halves(halves)

    small = jnp.concatenate([dg1.reshape(-1), dbg.reshape(-1), dgn.reshape(-1), drel.reshape(-1), dg2.reshape(-1),
                             dgf.reshape(-1), dw2p[:GLA_RANK].reshape(-1)]).reshape(SMALL_ROWS, LANE)
    tot = sum_slots(gather_small(small), "sum_small").reshape(-1)
    sizes = [D_MODEL, GLA_QK, GLA_WIDTH, REL_BUCKETS * DSA_HEADS, D_MODEL, D_MODEL, GLA_RANK * GLA_QK]
    offs = np.concatenate([[0], np.cumsum(sizes)])
    piece = lambda i: tot[int(offs[i]):int(offs[i + 1])]
    g_g1 = piece(0).reshape(1, D_MODEL)
    g_bg = piece(1).reshape(1, GLA_QK)
    g_gn = piece(2).reshape(1, GLA_WIDTH)
    g_rel = piece(3).reshape(REL_BUCKETS, DSA_HEADS)
    g_g2 = piece(4).reshape(1, D_MODEL)
    g_gf = piece(5).reshape(1, D_MODEL)
    g_w2 = lax.dynamic_slice_in_dim(piece(6).reshape(GLA_RANK, GLA_QK), chip * (GLA_QK // 4), GLA_QK // 4, axis=1)

    loss_all = lax.psum(loss[0, 0], ("x", "y", "c"))

    upd = [
        ("attn_norm_g", attn_norm_g, g_g1, m_attn_norm_g, v_attn_norm_g),
        ("w_in", w_in[0], g_win, m_w_in[0], v_w_in[0]),
        ("gla_gate_w2", gla_gate_w2[0], g_w2, m_gla_gate_w2[0], v_gla_gate_w2[0]),
        ("gla_gate_b", gla_gate_b, g_bg, m_gla_gate_b, v_gla_gate_b),
        ("gla_norm_g", gla_norm_g, g_gn, m_gla_norm_g, v_gla_norm_g),
        ("rel_bias", rel_bias, g_rel, m_rel_bias, v_rel_bias),
        ("w_out", w_out[0], g_wout, m_w_out[0], v_w_out[0]),
        ("mlp_norm_g", mlp_norm_g, g_g2, m_mlp_norm_g, v_mlp_norm_g),
        ("w_ff1", w_ff1[0], g_wff1, m_w_ff1[0], v_w_ff1[0]),
        ("w_ff2", w_ff2[0], g_wff2, m_w_ff2[0], v_w_ff2[0]),
        ("final_norm_g", gf, g_gf, m_final_norm_g.reshape(1, D_MODEL), v_final_norm_g.reshape(1, D_MODEL)),
    ]
    shapes = [attn_norm_g.shape, w_in.shape, gla_gate_w2.shape, gla_gate_b.shape, gla_norm_g.shape, rel_bias.shape,
              w_out.shape, mlp_norm_g.shape, w_ff1.shape, w_ff2.shape, final_norm_g.shape]
    grads, deltas, new_m, new_v = [], [], [], []
    for (name, w, g, m, v), shape in zip(upd, shapes):
        d, nm_, nv_ = adamw(w, g, m, v, "adamw_" + name)
        grads.append(g.reshape(shape))
        deltas.append(d.reshape(shape))
        new_m.append(nm_.reshape(shape))
        new_v.append(nv_.reshape(shape))
    return (loss_all, dxs.reshape(1, T, D_MODEL), *grads, *deltas, *new_m, *new_v)
```

```python
import functools
import math

import jax
import jax.numpy as jnp
import numpy as np
from jax import lax
from jax.experimental import pallas as pl
from jax.experimental.pallas import tpu as pltpu

F32 = jnp.float32
BF16 = jnp.bfloat16
MESH = pl.DeviceIdType.MESH

D_MODEL = 1024
GLA_WIDTH = 512
GLA_HEADS = 4
GLA_DK = 64
GLA_DV = 128
GLA_QK = GLA_HEADS * GLA_DK
GLA_RANK = 16
GLA_TAU = 16.0
GLA_CHUNK = 64
DSA_WIDTH = 512
DSA_HEADS = 8
DSA_DH = 64
DSA_DILATIONS = (1, 4, 16)
DSA_SPAN = 128
DSA_BLOCK = 128
DSA_SUPER = DSA_BLOCK * DSA_DILATIONS[-1]
REL_BUCKETS = 32
REL_MAX_DIST = 2048
D_FF = 4096
D_IN = 3088
EPS = 1e-6
NEG = -1e30
QK_SCALE = 0.125

ADAM_LR = 0.001
ADAM_B1 = 0.9
ADAM_B2 = 0.999
ADAM_EPS = 1e-08
ADAM_WD = 0.01
ADAM_STEP = 10

LANE = 128
P_GQ, P_GK, P_GV, P_GR = 0, 256, 512, 1024
P_GLOW = 1536
P_A = 1664
P_DQ, P_DK, P_DV = 1664, 2176, 2688
P_ALL = 3200
R_GQ, R_GK, R_GV, R_GR, R_GLOW, R_DQ, R_DK, R_DV = 0, 256, 512, 1024, 1536, 1552, 2064, 2576

VMEM_LIMIT = 56 * 1024 * 1024


def _params(sem=("arbitrary",), vmem=VMEM_LIMIT):
    return pltpu.CompilerParams(dimension_semantics=sem, vmem_limit_bytes=vmem)


def _dot(a, b):
    return jnp.dot(a, b, preferred_element_type=F32)


def _dot_nt(a, b):
    return lax.dot_general(a, b, (((1,), (1,)), ((), ())), preferred_element_type=F32)


def _dot_tn(a, b):
    return lax.dot_general(a, b, (((0,), (0,)), ((), ())), preferred_element_type=F32)


def _split3(x):
    x1 = x.astype(BF16)
    r1 = x - x1.astype(F32)
    x2 = r1.astype(BF16)
    x3 = (r1 - x2.astype(F32)).astype(BF16)
    return x1, x2, x3


def _dot_exact_lhs(m_bf16, x):
    x1, x2, x3 = _split3(x)
    return _dot(m_bf16, x1) + _dot(m_bf16, x2) + _dot(m_bf16, x3)


def _rstd(xf):
    return lax.rsqrt(jnp.mean(xf * xf, axis=-1, keepdims=True) + EPS)


def _load_once(hbm_ref, vmem_ref, sem):
    cp = pltpu.make_async_copy(hbm_ref, vmem_ref, sem)
    cp.start()
    cp.wait()


ANY = pl.BlockSpec(memory_space=pl.ANY)


def inproj(x, g1, wp):
    T = x.shape[0]
    tm = 256

    def body(x_ref, g_ref, w_hbm, proj_ref, nx_ref, w_vmem, sem):
        @pl.when(pl.program_id(0) == 0)
        def _():
            _load_once(w_hbm, w_vmem, sem)

        xf = x_ref[...]
        nx = ((xf * _rstd(xf)) * g_ref[...]).astype(BF16)
        nx_ref[...] = nx
        proj_ref[...] = _dot(nx, w_vmem[...])

    return pl.pallas_call(
        body,
        grid=(T // tm,),
        in_specs=[pl.BlockSpec((tm, D_MODEL), lambda i: (i, 0)), pl.BlockSpec((1, D_MODEL), lambda i: (0, 0)), ANY],
        out_specs=[pl.BlockSpec((tm, P_ALL), lambda i: (i, 0)), pl.BlockSpec((tm, D_MODEL), lambda i: (i, 0))],
        out_shape=[jax.ShapeDtypeStruct((T, P_ALL), F32), jax.ShapeDtypeStruct((T, D_MODEL), BF16)],
        scratch_shapes=[pltpu.VMEM((D_MODEL, P_ALL), BF16), pltpu.SemaphoreType.DMA],
        compiler_params=_params(),
        name="inproj",
    )(x, g1, wp)


GLA_CHUNKS_PER_STEP = 8
GLA_ROWS = GLA_CHUNK * GLA_CHUNKS_PER_STEP
GLA_UNROLL = 4


def _gla_masks():
    lane = lax.broadcasted_iota(jnp.int32, (1, GLA_QK), 1)
    return [(lane >= h * GLA_DK) & (lane < (h + 1) * GLA_DK) for h in range(GLA_HEADS)]


def _log_sigmoid(x):
    return jnp.minimum(x, 0.0) - jnp.log(1.0 + jnp.exp(-jnp.abs(x)))


def _sigmoid(x):
    return 1.0 / (1.0 + jnp.exp(-x))


def _phased_loop(n, unroll, load, compute, store, init):
    def body(i, carry):
        steps = [i * unroll + u for u in range(unroll)]
        loaded = [load(s) for s in steps]
        results = []
        for vals in loaded:
            carry, res = compute(vals, carry)
            results.append(res)
        for s, res in zip(steps, results):
            store(s, res)
        return carry

    return lax.fori_loop(0, n // unroll, body, init)


def _gla_chunk_rows(c):
    return pl.ds(pl.multiple_of(c * GLA_CHUNK, GLA_CHUNK), GLA_CHUNK)


def _gla_chunk_common(q, k, glow, w2, bg, tri):
    gpre = _dot(glow, w2) + bg
    glog = _log_sigmoid(gpre) / GLA_TAU
    b = _dot_exact_lhs(tri, glog)
    bl = b[GLA_CHUNK - 1:GLA_CHUNK, :]
    eb = jnp.exp(b)
    enb = jnp.exp(-b)
    eke = jnp.exp(bl - b)
    qd = (q * QK_SCALE) * eb
    ki = k * enb
    ke = k * eke
    return gpre, eb, enb, eke, jnp.exp(bl), qd, ki, ke


def _head_cols(h):
    return slice(h * GLA_DV, (h + 1) * GLA_DV)


def gla_fwd(proj, w2p, bg, gn):
    T = proj.shape[0]
    n_steps = T // GLA_ROWS
    n_chunks = T // GLA_CHUNK

    def body(proj_ref, w2_ref, bg_ref, gn_ref, oa_ref, opre_ref, sprev_ref, st_ref):
        @pl.when(pl.program_id(0) == 0)
        def _():
            st_ref[...] = jnp.zeros_like(st_ref)

        masks = _gla_masks()
        ri = lax.broadcasted_iota(jnp.int32, (GLA_CHUNK, GLA_CHUNK), 0)
        ci = lax.broadcasted_iota(jnp.int32, (GLA_CHUNK, GLA_CHUNK), 1)
        causal = ri >= ci
        tri = causal.astype(BF16)
        w2 = w2_ref[...].astype(BF16)
        bg = bg_ref[...]

        gn = gn_ref[...]

        def load(c):
            rows = _gla_chunk_rows(c)
            return (proj_ref[rows, P_GQ:P_GQ + GLA_QK], proj_ref[rows, P_GK:P_GK + GLA_QK],
                    proj_ref[rows, P_GV:P_GV + GLA_WIDTH], proj_ref[rows, P_GR:P_GR + GLA_WIDTH],
                    proj_ref[rows, P_GLOW:P_GLOW + LANE])

        def compute(vals, st):
            q, k, v, r, glow = vals
            _, _, _, _, ebl, qd, ki, ke = _gla_chunk_common(q, k, glow.astype(BF16), w2, bg, tri)
            st_b = st.astype(BF16)
            ki_b = ki.astype(BF16)
            inc = jnp.zeros_like(st)
            o_heads = []
            for h in range(GLA_HEADS):
                v_h = v[:, _head_cols(h)].astype(BF16)
                qd_h = jnp.where(masks[h], qd, 0.0).astype(BF16)
                ke_h = jnp.where(masks[h], ke, 0.0).astype(BF16)
                att = jnp.where(causal, _dot_nt(qd_h, ki_b), 0.0)
                o_heads.append(_dot(att.astype(BF16), v_h) + _dot_nt(qd_h, st_b))
                inc = inc + _dot_tn(v_h, ke_h)
            o = jnp.concatenate(o_heads, axis=1)
            on = jnp.concatenate([o_h * _rstd(o_h) for o_h in o_heads], axis=1)
            oa = ((on * gn) * (r * _sigmoid(r))).astype(BF16)
            return st * ebl + inc, (st, o, oa)

        def store(c, res):
            rows = _gla_chunk_rows(c)
            sprev_ref[c], opre_ref[rows, :], oa_ref[rows, :] = res

        st_ref[...] = _phased_loop(GLA_CHUNKS_PER_STEP, GLA_UNROLL, load, compute, store, st_ref[...])

    return pl.pallas_call(
        body,
        grid=(n_steps,),
        in_specs=[
            pl.BlockSpec((GLA_ROWS, P_ALL), lambda i: (i, 0)),
            pl.BlockSpec((LANE, GLA_QK), lambda i: (0, 0)),
            pl.BlockSpec((1, GLA_QK), lambda i: (0, 0)),
            pl.BlockSpec((1, GLA_WIDTH), lambda i: (0, 0)),
        ],
        out_specs=[
            pl.BlockSpec((GLA_ROWS, GLA_WIDTH), lambda i: (i, 0)),
            pl.BlockSpec((GLA_ROWS, GLA_WIDTH), lambda i: (i, 0)),
            pl.BlockSpec((GLA_CHUNKS_PER_STEP, GLA_DV, GLA_QK), lambda i: (i, 0, 0)),
        ],
        out_shape=[
            jax.ShapeDtypeStruct((T, GLA_WIDTH), BF16),
            jax.ShapeDtypeStruct((T, GLA_WIDTH), F32),
            jax.ShapeDtypeStruct((n_chunks, GLA_DV, GLA_QK), F32),
        ],
        scratch_shapes=[pltpu.VMEM((GLA_DV, GLA_QK), F32)],
        compiler_params=_params(),
        name="gla_fwd",
    )(proj, w2p, bg, gn)


def gla_bwd(proj, w2p, bg, gn, opre, sprev, dmixed):
    T = proj.shape[0]
    n_steps = T // GLA_ROWS

    def body(proj_ref, w2_ref, bg_ref, gn_ref, opre_ref, sprev_ref, doa_ref, da_ref, dw2_ref, dbg_ref, dgn_ref, dst_ref):
        @pl.when(pl.program_id(0) == 0)
        def _():
            dst_ref[...] = jnp.zeros_like(dst_ref)
            dw2_ref[...] = jnp.zeros_like(dw2_ref)
            dbg_ref[...] = jnp.zeros_like(dbg_ref)
            dgn_ref[...] = jnp.zeros_like(dgn_ref)

        masks = _gla_masks()
        ri = lax.broadcasted_iota(jnp.int32, (GLA_CHUNK, GLA_CHUNK), 0)
        ci = lax.broadcasted_iota(jnp.int32, (GLA_CHUNK, GLA_CHUNK), 1)
        causal = ri >= ci
        tri = causal.astype(BF16)
        tri_t = (ri <= ci).astype(BF16)
        last_row = lax.broadcasted_iota(jnp.int32, (GLA_CHUNK, GLA_QK), 0) == GLA_CHUNK - 1
        w2 = w2_ref[...].astype(BF16)
        bg = bg_ref[...]

        gn = gn_ref[...]

        def load(j):
            c = GLA_CHUNKS_PER_STEP - 1 - j
            rows = _gla_chunk_rows(c)
            return (proj_ref[rows, P_GQ:P_GQ + GLA_QK], proj_ref[rows, P_GK:P_GK + GLA_QK],
                    proj_ref[rows, P_GV:P_GV + GLA_WIDTH], proj_ref[rows, P_GR:P_GR + GLA_WIDTH],
                    proj_ref[rows, P_GLOW:P_GLOW + LANE], opre_ref[rows, :], doa_ref[rows, :], sprev_ref[c])

        def compute(vals, carry):
            dst, dw2, dbg, dgn = carry
            q, k, v, r, glow, o, doa, st = vals
            glow = glow.astype(BF16)
            gpre, eb, enb, eke, ebl, qd, ki, ke = _gla_chunk_common(q, k, glow, w2, bg, tri)
            sig = _sigmoid(r)
            rs = jnp.concatenate([jnp.broadcast_to(_rstd(o[:, _head_cols(h)]), (GLA_CHUNK, GLA_DV))
                                  for h in range(GLA_HEADS)], axis=1)
            on = o * rs
            d_ong = doa * (r * sig)
            dr = doa * (on * gn) * (sig * (1.0 + r * (1.0 - sig)))
            dgn = dgn + jnp.sum(d_ong * on, axis=0, keepdims=True)
            d_on = d_ong * gn
            t = d_on * on
            mean_t = jnp.concatenate([jnp.broadcast_to(jnp.mean(t[:, _head_cols(h)], axis=-1, keepdims=True), (GLA_CHUNK, GLA_DV))
                                      for h in range(GLA_HEADS)], axis=1)
            do = rs * (d_on - on * mean_t)
            st_b = st.astype(BF16)
            dst_b = dst.astype(BF16)
            ki_b = ki.astype(BF16)
            dqd = jnp.zeros_like(qd)
            dki = jnp.zeros_like(qd)
            dke = jnp.zeros_like(qd)
            dst_add = jnp.zeros_like(st)
            dv_heads = []
            for h in range(GLA_HEADS):
                do_b = do[:, _head_cols(h)].astype(BF16)
                v_h = v[:, _head_cols(h)].astype(BF16)
                qd_h = jnp.where(masks[h], qd, 0.0).astype(BF16)
                ke_h = jnp.where(masks[h], ke, 0.0).astype(BF16)
                att = jnp.where(causal, _dot_nt(qd_h, ki_b), 0.0).astype(BF16)
                d_att = jnp.where(causal, _dot_nt(do_b, v_h), 0.0).astype(BF16)
                dv_heads.append(_dot_tn(att, do_b) + _dot_nt(ke_h, dst_b))
                dqd = dqd + jnp.where(masks[h], _dot(d_att, ki_b) + _dot(do_b, st_b), 0.0)
                dki = dki + _dot_tn(d_att, qd_h)
                dke = dke + jnp.where(masks[h], _dot(v_h, dst_b), 0.0)
                dst_add = dst_add + _dot_tn(do_b, qd_h)
            debl = jnp.sum(dst * st, axis=0, keepdims=True)
            dkk = dke * ke
            db = dqd * qd - dki * ki - dkk
            dbl = jnp.sum(dkk, axis=0, keepdims=True) + debl * ebl
            db = db + jnp.where(last_row, dbl, 0.0)
            dglog = _dot_exact_lhs(tri_t, db)
            dgpre = (dglog / GLA_TAU) * _sigmoid(-gpre)
            dgpre_b = dgpre.astype(BF16)
            da = jnp.concatenate([dqd * eb * QK_SCALE, dki * enb + dke * eke] + dv_heads + [dr, _dot_nt(dgpre_b, w2)],
                                 axis=1).astype(BF16)
            carry = (dst * ebl + dst_add, dw2 + _dot_tn(glow, dgpre_b), dbg + jnp.sum(dgpre, axis=0, keepdims=True), dgn)
            return carry, da

        def store(j, da):
            da_ref[_gla_chunk_rows(GLA_CHUNKS_PER_STEP - 1 - j), :] = da

        carry = (dst_ref[...], jnp.zeros((LANE, GLA_QK), F32), jnp.zeros((1, GLA_QK), F32), jnp.zeros((1, GLA_WIDTH), F32))
        dst, dw2, dbg, dgn = _phased_loop(GLA_CHUNKS_PER_STEP, GLA_UNROLL, load, compute, store, carry)
        dst_ref[...] = dst
        dw2_ref[...] += dw2
        dbg_ref[...] += dbg
        dgn_ref[...] += dgn

    rev = lambda i: (n_steps - 1 - i, 0)
    return pl.pallas_call(
        body,
        grid=(n_steps,),
        in_specs=[
            pl.BlockSpec((GLA_ROWS, P_ALL), rev),
            pl.BlockSpec((LANE, GLA_QK), lambda i: (0, 0)),
            pl.BlockSpec((1, GLA_QK), lambda i: (0, 0)),
            pl.BlockSpec((1, GLA_WIDTH), lambda i: (0, 0)),
            pl.BlockSpec((GLA_ROWS, GLA_WIDTH), rev),
            pl.BlockSpec((GLA_CHUNKS_PER_STEP, GLA_DV, GLA_QK), lambda i: (n_steps - 1 - i, 0, 0)),
            pl.BlockSpec((GLA_ROWS, GLA_WIDTH), rev),
        ],
        out_specs=[
            pl.BlockSpec((GLA_ROWS, P_A), rev),
            pl.BlockSpec((LANE, GLA_QK), lambda i: (0, 0)),
            pl.BlockSpec((1, GLA_QK), lambda i: (0, 0)),
            pl.BlockSpec((1, GLA_WIDTH), lambda i: (0, 0)),
        ],
        out_shape=[
            jax.ShapeDtypeStruct((T, P_A), BF16),
            jax.ShapeDtypeStruct((LANE, GLA_QK), F32),
            jax.ShapeDtypeStruct((1, GLA_QK), F32),
            jax.ShapeDtypeStruct((1, GLA_WIDTH), F32),
        ],
        scratch_shapes=[pltpu.VMEM((GLA_DV, GLA_QK), F32)],
        compiler_params=_params(),
        name="gla_bwd",
    )(proj, w2p, bg, gn, opre, sprev, dmixed)


def _t5_bucket(dist):
    max_exact = REL_BUCKETS // 2
    n = np.maximum(dist, 0)
    large = max_exact + (np.log(np.maximum(n, 1) / max_exact) / math.log(REL_MAX_DIST / max_exact)
                         * (REL_BUCKETS - max_exact)).astype(np.int32)
    large = np.minimum(large, REL_BUCKETS - 1)
    return np.where(n < max_exact, n, large).astype(np.int32)


def _bucket_ids():
    L = DSA_BLOCK
    steps = L + np.arange(L)[:, None] - np.arange(2 * L)[None, :]
    in_band = (steps >= 0) & (steps <= DSA_SPAN)
    return np.stack([np.where(in_band, _t5_bucket(steps * d), -1) for d in DSA_DILATIONS]).astype(np.int32)


def bias_tables(rel_bias):
    ids = jnp.asarray(_bucket_ids())
    nd = len(DSA_DILATIONS)

    def body(rel_ref, ids_ref, tab_ref):
        h = pl.program_id(1)
        idt = ids_ref[0]
        acc = jnp.where(idt < 0, NEG, 0.0).astype(F32)
        for b in range(REL_BUCKETS):
            acc = jnp.where(idt == b, rel_ref[b, h], acc)
        tab_ref[0, 0] = acc

    return pl.pallas_call(
        body,
        grid=(nd, DSA_HEADS),
        in_specs=[pl.BlockSpec(memory_space=pltpu.SMEM), pl.BlockSpec((1, DSA_BLOCK, 2 * DSA_BLOCK), lambda d, h: (d, 0, 0))],
        out_specs=pl.BlockSpec((1, 1, DSA_BLOCK, 2 * DSA_BLOCK), lambda d, h: (d, h, 0, 0)),
        out_shape=jax.ShapeDtypeStruct((nd, DSA_HEADS, DSA_BLOCK, 2 * DSA_BLOCK), F32),
        compiler_params=_params(("arbitrary", "arbitrary")),
        name="bias_tables",
    )(rel_bias, ids)


def bias_tables_bwd(dtab):
    ids = jnp.asarray(_bucket_ids())
    nd = len(DSA_DILATIONS)

    def body(dtab_ref, ids_ref, drel_ref):
        @pl.when((pl.program_id(0) == 0) & (pl.program_id(1) == 0))
        def _():
            for b in range(REL_BUCKETS):
                for h in range(DSA_HEADS):
                    drel_ref[b, h] = 0.0

        h = pl.program_id(1)
        idt = ids_ref[0]
        g = dtab_ref[0, 0]
        for b in range(REL_BUCKETS):
            drel_ref[b, h] += jnp.sum(jnp.where(idt == b, g, 0.0))

    return pl.pallas_call(
        body,
        grid=(nd, DSA_HEADS),
        in_specs=[pl.BlockSpec((1, 1, DSA_BLOCK, 2 * DSA_BLOCK), lambda d, h: (d, h, 0, 0)),
                  pl.BlockSpec((1, DSA_BLOCK, 2 * DSA_BLOCK), lambda d, h: (d, 0, 0))],
        out_specs=pl.BlockSpec(memory_space=pltpu.SMEM),
        out_shape=jax.ShapeDtypeStruct((REL_BUCKETS, DSA_HEADS), F32),
        compiler_params=_params(("arbitrary", "arbitrary")),
        name="bias_tables_bwd",
    )(dtab, ids)


DSA_PAIRS = DSA_HEADS // 2
DSA_UNROLL = 8
DSA_COMBINE_ROWS = 256


def _dsa_units(d):
    return d, DSA_SUPER // (DSA_BLOCK * d)


def _dsa_specs(T):
    nsb = T // DSA_SUPER
    qcol, kcol, vcol = P_DQ // LANE, P_DK // LANE, P_DV // LANE
    return nsb, qcol, kcol, vcol


def _head_lane_mask():
    return lax.broadcasted_iota(jnp.int32, (1, LANE), 1) < DSA_DH


def _first_block_penalty(first):
    col = lax.broadcasted_iota(jnp.int32, (2 * DSA_BLOCK, 2 * DSA_BLOCK), 1)
    return jnp.where(first & (col < DSA_BLOCK), NEG, 0.0).astype(F32)


def _pair_tiles(tab):
    return tab.reshape(len(DSA_DILATIONS), DSA_PAIRS, 2 * DSA_BLOCK, 2 * DSA_BLOCK)


def _stack_heads(t, head0):
    return jnp.concatenate([jnp.where(head0, t, 0.0), jnp.where(head0, 0.0, t)], axis=0)


def dsa_fwd(proj, tab):
    T = proj.shape[0]
    nsb, qcol, kcol, vcol = _dsa_specs(T)
    S = DSA_SUPER

    def body(q_ref, kp_ref, kc_ref, vp_ref, vc_ref, tab_ref, out_ref, lse_ref, kk, vv, ob, lb):
        sb = pl.program_id(1)
        kk[0:S, :] = kp_ref[...]
        kk[S:2 * S, :] = kc_ref[...]
        vv[0:S, :] = vp_ref[...]
        vv[S:2 * S, :] = vc_ref[...]
        head0 = _head_lane_mask()

        for di, d in enumerate(DSA_DILATIONS):
            n_res, n_blk = _dsa_units(d)

            def unit(u, carry, di=di, d=d, n_blk=n_blk):
                r = u // n_blk
                c = u % n_blk
                q0 = r + d * DSA_BLOCK * c
                qrows = pl.ds(q0, DSA_BLOCK, stride=d) if d > 1 else pl.ds(q0, DSA_BLOCK)
                krows = pl.ds(S + q0 - d * DSA_BLOCK, 2 * DSA_BLOCK, stride=d) if d > 1 else pl.ds(S + q0 - DSA_BLOCK, 2 * DSA_BLOCK)
                q2 = q_ref[qrows, :] * QK_SCALE
                k2 = kk[krows, :].astype(BF16)
                v2 = vv[krows, :].astype(BF16)
                qs = _stack_heads(q2, head0).astype(BF16)
                s = _dot_nt(qs, k2) + (tab_ref[di, 0] + _first_block_penalty((sb == 0) & (c == 0)))
                m = jnp.max(s, axis=-1, keepdims=True)
                p = jnp.exp(s - m)
                den = jnp.sum(p, axis=-1, keepdims=True)
                o = _dot(p.astype(BF16), v2) / den
                l = jnp.broadcast_to(m + jnp.log(den), (2 * DSA_BLOCK, LANE))
                ob[di, qrows, :] = jnp.where(head0, o[:DSA_BLOCK], o[DSA_BLOCK:])
                lb[di, qrows, :] = jnp.where(head0, l[:DSA_BLOCK], l[DSA_BLOCK:])
                return carry

            lax.fori_loop(0, n_res * n_blk, unit, 0, unroll=DSA_UNROLL)

        def combine(i, carry):
            rows = pl.ds(pl.multiple_of(i * DSA_COMBINE_ROWS, DSA_COMBINE_ROWS), DSA_COMBINE_ROWS)
            l0, l1, l2 = lb[0, rows, :], lb[1, rows, :], lb[2, rows, :]
            mx = jnp.maximum(jnp.maximum(l0, l1), l2)
            e0, e1, e2 = jnp.exp(l0 - mx), jnp.exp(l1 - mx), jnp.exp(l2 - mx)
            den = e0 + e1 + e2
            out_ref[rows, :] = (e0 * ob[0, rows, :] + e1 * ob[1, rows, :] + e2 * ob[2, rows, :]) / den
            lse_ref[rows, :] = mx + jnp.log(den)
            return carry

        lax.fori_loop(0, S // DSA_COMBINE_ROWS, combine, 0)

    prev = lambda col: (lambda hp, sb: (jnp.maximum(sb - 1, 0), col + hp))
    cur = lambda col: (lambda hp, sb: (sb, col + hp))
    blk = lambda f: pl.BlockSpec((S, LANE), f)
    return pl.pallas_call(
        body,
        grid=(DSA_PAIRS, nsb),
        in_specs=[blk(cur(qcol)), blk(prev(kcol)), blk(cur(kcol)), blk(prev(vcol)), blk(cur(vcol)),
                  pl.BlockSpec((len(DSA_DILATIONS), 1, 2 * DSA_BLOCK, 2 * DSA_BLOCK), lambda hp, sb: (0, hp, 0, 0))],
        out_specs=[blk(lambda hp, sb: (sb, hp)), blk(lambda hp, sb: (sb, hp))],
        out_shape=[jax.ShapeDtypeStruct((T, DSA_WIDTH), F32), jax.ShapeDtypeStruct((T, DSA_WIDTH), F32)],
        scratch_shapes=[pltpu.VMEM((2 * S, LANE), F32), pltpu.VMEM((2 * S, LANE), F32),
                        pltpu.VMEM((len(DSA_DILATIONS), S, LANE), F32), pltpu.VMEM((len(DSA_DILATIONS), S, LANE), F32)],
        compiler_params=_params(("arbitrary", "arbitrary")),
        name="dsa_fwd",
    )(proj, proj, proj, proj, proj, _pair_tiles(tab))


def dsa_bwd(proj, tab, ob_out, lse, dmixed):
    T = proj.shape[0]
    nsb, qcol, kcol, vcol = _dsa_specs(T)
    S = DSA_SUPER
    nd = len(DSA_DILATIONS)
    ocol = GLA_WIDTH // LANE

    def body(q_ref, kp_ref, kc_ref, vp_ref, vc_ref, tab_ref, o_ref, lse_ref, do_ref,
             dq_ref, dk_ref, dv_ref, dtab_ref, kk, vv, dqa, dkk, dvv):
        j = pl.program_id(1)
        sb = nsb - 1 - j
        kk[0:S, :] = kp_ref[...]
        kk[S:2 * S, :] = kc_ref[...]
        vv[0:S, :] = vp_ref[...]
        vv[S:2 * S, :] = vc_ref[...]
        head0 = _head_lane_mask()

        @pl.when(j == 0)
        def _():
            dtab_ref[...] = jnp.zeros_like(dtab_ref)
            dkk[S:2 * S, :] = jnp.zeros((S, LANE), F32)
            dvv[S:2 * S, :] = jnp.zeros((S, LANE), F32)

        @pl.when(j > 0)
        def _():
            dkk[S:2 * S, :] = dkk[0:S, :]
            dvv[S:2 * S, :] = dvv[0:S, :]

        dkk[0:S, :] = jnp.zeros((S, LANE), F32)
        dvv[0:S, :] = jnp.zeros((S, LANE), F32)
        dqa[...] = jnp.zeros_like(dqa)

        for di, d in enumerate(DSA_DILATIONS):
            n_res, n_blk = _dsa_units(d)

            def unit(u, carry, di=di, d=d, n_blk=n_blk):
                r = u // n_blk
                c = u % n_blk
                q0 = r + d * DSA_BLOCK * c
                qrows = pl.ds(q0, DSA_BLOCK, stride=d) if d > 1 else pl.ds(q0, DSA_BLOCK)
                krows = pl.ds(S + q0 - d * DSA_BLOCK, 2 * DSA_BLOCK, stride=d) if d > 1 else pl.ds(S + q0 - DSA_BLOCK, 2 * DSA_BLOCK)
                q2 = q_ref[qrows, :] * QK_SCALE
                k2 = kk[krows, :].astype(BF16)
                v2 = vv[krows, :].astype(BF16)
                do2 = do_ref[qrows, :]
                o2 = o_ref[qrows, :]
                l2 = lse_ref[qrows, :]
                qs = _stack_heads(q2, head0).astype(BF16)
                dos = _stack_heads(do2, head0)
                dos_b = dos.astype(BF16)
                delta = jnp.sum(dos * jnp.concatenate([o2, o2], axis=0), axis=-1, keepdims=True)
                lse = jnp.concatenate([jnp.max(jnp.where(head0, l2, -jnp.inf), axis=-1, keepdims=True),
                                       jnp.max(jnp.where(head0, -jnp.inf, l2), axis=-1, keepdims=True)], axis=0)
                s = _dot_nt(qs, k2) + (tab_ref[di, 0] + _first_block_penalty((sb == 0) & (c == 0)))
                p = jnp.exp(s - lse)
                ds = p * (_dot_nt(dos_b, v2) - delta)
                dtab_ref[di, 0] += ds
                ds_b = ds.astype(BF16)
                dq = _dot(ds_b, k2)
                dqa[qrows, :] += jnp.where(head0, dq[:DSA_BLOCK], dq[DSA_BLOCK:]) * QK_SCALE
                dkk[krows, :] += _dot_tn(ds_b, qs)
                dvv[krows, :] += _dot_tn(p.astype(BF16), dos_b)
                return carry

            lax.fori_loop(0, n_res * n_blk, unit, 0, unroll=DSA_UNROLL)

        dq_ref[...] = dqa[...].astype(BF16)
        dk_ref[...] = dkk[S:2 * S, :].astype(BF16)
        dv_ref[...] = dvv[S:2 * S, :].astype(BF16)

    prev = lambda col: (lambda hp, j: (jnp.maximum(nsb - 2 - j, 0), col + hp))
    cur = lambda col: (lambda hp, j: (nsb - 1 - j, col + hp))
    blk = lambda f: pl.BlockSpec((S, LANE), f)
    out_blk = blk(lambda hp, j: (nsb - 1 - j, hp))
    tab_blk = pl.BlockSpec((nd, 1, 2 * DSA_BLOCK, 2 * DSA_BLOCK), lambda hp, j: (0, hp, 0, 0))
    dq, dk, dv, dtab = pl.pallas_call(
        body,
        grid=(DSA_PAIRS, nsb),
        in_specs=[blk(cur(qcol)), blk(prev(kcol)), blk(cur(kcol)), blk(prev(vcol)), blk(cur(vcol)), tab_blk,
                  out_blk, out_blk, blk(cur(ocol))],
        out_specs=[out_blk, out_blk, out_blk, tab_blk],
        out_shape=[jax.ShapeDtypeStruct((T, DSA_WIDTH), BF16)] * 3
        + [jax.ShapeDtypeStruct((nd, DSA_PAIRS, 2 * DSA_BLOCK, 2 * DSA_BLOCK), F32)],
        scratch_shapes=[pltpu.VMEM((2 * S, LANE), F32), pltpu.VMEM((2 * S, LANE), F32), pltpu.VMEM((S, LANE), F32),
                        pltpu.VMEM((2 * S, LANE), F32), pltpu.VMEM((2 * S, LANE), F32)],
        compiler_params=_params(("arbitrary", "arbitrary")),
        name="dsa_bwd",
    )(proj, proj, proj, proj, proj, _pair_tiles(tab), ob_out, lse, dmixed)
    return dq, dk, dv, dtab.reshape(nd, DSA_HEADS, DSA_BLOCK, 2 * DSA_BLOCK)


FF_BLOCKS = 4
FF_BLOCK = D_FF // FF_BLOCKS


def post_fused(x, oa, ob, tgt, g2, gf, wout, wff1, wff2):
    T = x.shape[0]
    tm = 256
    inv_d = 1.0 / D_MODEL

    def body(x_ref, oa_ref, ob_ref, tgt_ref, g2_ref, gf_ref, wout_hbm, wff1_hbm, wff2_hbm,
             mixed_ref, nm_ref, a_ref, dpre_ref, dh2_ref, dh1_ref, dmixed_ref, loss_ref, dgf_ref, dg2_ref,
             wout_v, wff1_v, wff2_v, sems):
        @pl.when(pl.program_id(0) == 0)
        def _():
            cps = [pltpu.make_async_copy(s, d, sems.at[i])
                   for i, (s, d) in enumerate([(wout_hbm, wout_v), (wff1_hbm, wff1_v), (wff2_hbm, wff2_v)])]
            for cp in cps:
                cp.start()
            for cp in cps:
                cp.wait()
            loss_ref[...] = jnp.zeros_like(loss_ref)
            dgf_ref[...] = jnp.zeros_like(dgf_ref)
            dg2_ref[...] = jnp.zeros_like(dg2_ref)

        mixed = jnp.concatenate([oa_ref[...], ob_ref[...].astype(BF16)], axis=1)
        mixed_ref[...] = mixed
        h1 = x_ref[...] + _dot(mixed, wout_v[...])
        rs1 = _rstd(h1)
        hn1 = h1 * rs1
        g2 = g2_ref[...]
        nm = (hn1 * g2).astype(BF16)
        nm_ref[...] = nm
        relu = []
        mlp = jnp.zeros((tm, D_MODEL), F32)
        for j in range(FF_BLOCKS):
            cols = slice(j * FF_BLOCK, (j + 1) * FF_BLOCK)
            r_j = jnp.maximum(_dot(nm, wff1_v[j]), 0.0)
            a_j = (r_j * r_j).astype(BF16)
            a_ref[:, cols] = a_j
            relu.append(r_j)
            mlp = mlp + _dot(a_j, wff2_v[cols, :])
        h2 = h1 + mlp
        rsf = _rstd(h2)
        hnf = h2 * rsf
        gf = gf_ref[...]
        diff = hnf * gf - tgt_ref[...]
        loss_ref[...] += 0.5 * jnp.sum(jnp.sum(diff * diff, axis=-1, keepdims=True) * inv_d, axis=0, keepdims=True)
        dy = diff * inv_d
        dgf_ref[...] += jnp.sum(dy * hnf, axis=0, keepdims=True)
        dhnf = dy * gf
        dh2 = rsf * (dhnf - hnf * jnp.mean(dhnf * hnf, axis=-1, keepdims=True))
        dh2_b = dh2.astype(BF16)
        dh2_ref[...] = dh2_b
        dnm = jnp.zeros((tm, D_MODEL), F32)
        for j in range(FF_BLOCKS):
            cols = slice(j * FF_BLOCK, (j + 1) * FF_BLOCK)
            dpre_j = (_dot_nt(dh2_b, wff2_v[cols, :]) * (2.0 * relu[j])).astype(BF16)
            dpre_ref[:, cols] = dpre_j
            dnm = dnm + _dot_nt(dpre_j, wff1_v[j])
        dg2_ref[...] += jnp.sum(dnm * hn1, axis=0, keepdims=True)
        dhn1 = dnm * g2
        dh1 = dh2 + rs1 * (dhn1 - hn1 * jnp.mean(dhn1 * hn1, axis=-1, keepdims=True))
        dh1_ref[...] = dh1
        dmixed_ref[...] = _dot_nt(dh1.astype(BF16), wout_v[...])

    row = lambda w: pl.BlockSpec((tm, w), lambda i: (i, 0))
    vec = lambda w: pl.BlockSpec((1, w), lambda i: (0, 0))
    return pl.pallas_call(
        body,
        grid=(T // tm,),
        in_specs=[row(D_MODEL), row(GLA_WIDTH), row(DSA_WIDTH), row(D_MODEL), vec(D_MODEL), vec(D_MODEL), ANY, ANY, ANY],
        out_specs=[row(D_MODEL), row(D_MODEL), row(D_FF), row(D_FF), row(D_MODEL), row(D_MODEL), row(D_MODEL),
                   vec(1), vec(D_MODEL), vec(D_MODEL)],
        out_shape=[
            jax.ShapeDtypeStruct((T, D_MODEL), BF16),
            jax.ShapeDtypeStruct((T, D_MODEL), BF16),
            jax.ShapeDtypeStruct((T, D_FF), BF16),
            jax.ShapeDtypeStruct((T, D_FF), BF16),
            jax.ShapeDtypeStruct((T, D_MODEL), BF16),
            jax.ShapeDtypeStruct((T, D_MODEL), F32),
            jax.ShapeDtypeStruct((T, D_MODEL), F32),
            jax.ShapeDtypeStruct((1, 1), F32),
            jax.ShapeDtypeStruct((1, D_MODEL), F32),
            jax.ShapeDtypeStruct((1, D_MODEL), F32),
        ],
        scratch_shapes=[pltpu.VMEM((D_MODEL, D_MODEL), BF16), pltpu.VMEM((FF_BLOCKS, D_MODEL, FF_BLOCK), BF16),
                        pltpu.VMEM((D_FF, D_MODEL), BF16), pltpu.SemaphoreType.DMA((3,))],
        compiler_params=_params(),
        name="post_fused",
    )(x, oa, ob, tgt, g2, gf, wout, wff1, wff2)


WGRAD_TOKENS = 1024


def wgrad(a, b, name, bm=None, bn=None, col_blocked=False):
    T, M = a.shape
    N = b.shape[1]
    bm = M if bm is None else bm
    bn = N if bn is None else bn
    tk = min(WGRAD_TOKENS, T)

    def body(a_ref, b_ref, o_ref):
        part = _dot_tn(a_ref[...].astype(BF16), b_ref[...].astype(BF16))
        out = o_ref.at[0] if col_blocked else o_ref

        @pl.when(pl.program_id(2) == 0)
        def _():
            out[...] = part

        @pl.when(pl.program_id(2) > 0)
        def _():
            out[...] += part

    if col_blocked:
        assert bm == M
        out_spec = pl.BlockSpec((1, M, bn), lambda i, j, k: (j, 0, 0))
        out_shape = jax.ShapeDtypeStruct((N // bn, M, bn), F32)
    else:
        out_spec = pl.BlockSpec((bm, bn), lambda i, j, k: (i, j))
        out_shape = jax.ShapeDtypeStruct((M, N), F32)
    return pl.pallas_call(
        body,
        grid=(M // bm, N // bn, T // tk),
        in_specs=[pl.BlockSpec((tk, bm), lambda i, j, k: (k, i)), pl.BlockSpec((tk, bn), lambda i, j, k: (k, j))],
        out_specs=out_spec,
        out_shape=out_shape,
        compiler_params=_params(("arbitrary", "arbitrary", "arbitrary")),
        name=name,
    )(a, b)


def dx_final(x, dh1, g1, da, dq, dk, dv, wp):
    T = x.shape[0]
    tm = 256

    def body(x_ref, dh1_ref, g_ref, da_ref, dq_ref, dk_ref, dv_ref, w_hbm, dx_ref, dg_ref, w_vmem, sem):
        @pl.when(pl.program_id(0) == 0)
        def _():
            _load_once(w_hbm, w_vmem, sem)
            dg_ref[...] = jnp.zeros_like(dg_ref)

        dnx = (_dot_nt(da_ref[...], w_vmem[:, 0:P_A]) + _dot_nt(dq_ref[...], w_vmem[:, P_DQ:P_DQ + DSA_WIDTH])
               + _dot_nt(dk_ref[...], w_vmem[:, P_DK:P_DK + DSA_WIDTH]) + _dot_nt(dv_ref[...], w_vmem[:, P_DV:P_DV + DSA_WIDTH]))
        xf = x_ref[...]
        rs = _rstd(xf)
        hn = xf * rs
        dg_ref[...] += jnp.sum(dnx * hn, axis=0, keepdims=True)
        dhn = dnx * g_ref[...]
        dx_ref[...] = dh1_ref[...] + rs * (dhn - hn * jnp.mean(dhn * hn, axis=-1, keepdims=True))

    row = lambda w: pl.BlockSpec((tm, w), lambda i: (i, 0))
    vec = pl.BlockSpec((1, D_MODEL), lambda i: (0, 0))
    return pl.pallas_call(
        body,
        grid=(T // tm,),
        in_specs=[row(D_MODEL), row(D_MODEL), vec, row(P_A), row(DSA_WIDTH), row(DSA_WIDTH), row(DSA_WIDTH), ANY],
        out_specs=[row(D_MODEL), vec],
        out_shape=[jax.ShapeDtypeStruct((T, D_MODEL), F32), jax.ShapeDtypeStruct((1, D_MODEL), F32)],
        scratch_shapes=[pltpu.VMEM((D_MODEL, P_ALL), BF16), pltpu.SemaphoreType.DMA],
        compiler_params=_params(),
        name="dx_final",
    )(x, dh1, g1, da, dq, dk, dv, wp)


def adamw(w, g, m, v, name):
    R, C = w.shape
    br = 256 if R % 256 == 0 else R

    def body(w_ref, g_ref, m_ref, v_ref, d_ref, nm_ref, nv_ref):
        g = g_ref[...]
        m_new = ADAM_B1 * m_ref[...] + (1.0 - ADAM_B1) * g
        v_new = ADAM_B2 * v_ref[...] + (1.0 - ADAM_B2) * (g * g)
        m_hat = m_new / (1.0 - ADAM_B1 ** ADAM_STEP)
        v_hat = v_new / (1.0 - ADAM_B2 ** ADAM_STEP)
        d_ref[...] = -ADAM_LR * (m_hat / (jnp.sqrt(v_hat) + ADAM_EPS) + ADAM_WD * w_ref[...])
        nm_ref[...] = m_new
        nv_ref[...] = v_new

    spec = pl.BlockSpec((br, C), lambda i: (i, 0))
    return pl.pallas_call(
        body,
        grid=(R // br,),
        in_specs=[spec] * 4,
        out_specs=[spec] * 3,
        out_shape=[jax.ShapeDtypeStruct((R, C), F32)] * 3,
        compiler_params=_params(),
        name=name,
    )(w, g, m, v)


def _place():
    return lax.axis_index("x"), lax.axis_index("y"), lax.axis_index("c")


def _other_chips(x, y):
    return [(1 - x, y), (x, 1 - y), (1 - x, 1 - y)]


def gather_weights(shards):
    n = len(shards)

    def body(*refs):
        ins, outs = refs[:n], refs[n:2 * n]
        send_sems, recv_sems, local_sems = refs[2 * n:]
        x, y, c = _place()
        me, sib = (x, y, c), (x, y, 1 - c)
        mine = 2 * x + y
        chips = _other_chips(x, y)

        def region(a, owner, half):
            h = ins[a].shape[0] // 2
            return outs[a].at[owner, pl.ds(half * h, h)]

        def copy(a, k, owner, half, to, src=None):
            return pltpu.make_async_remote_copy(
                src_ref=region(a, owner, half) if src is None else src, dst_ref=region(a, owner, half),
                send_sem=send_sems.at[a, k], recv_sem=recv_sems.at[a, k], device_id=to, device_id_type=MESH)

        own = [pltpu.make_async_copy(ins[a], outs[a].at[mine], local_sems.at[a]) for a in range(n)]
        for cp in own:
            cp.start()
        first = []
        for a in range(n):
            h = ins[a].shape[0] // 2
            for t, (cx, cy) in enumerate(chips):
                first.append(copy(a, t, mine, c, (cx, cy, c), src=ins[a].at[pl.ds(c * h, h)]))
        for cp in first:
            cp.start()
        passed = []
        for a in range(n):
            for t, (cx, cy) in enumerate(chips):
                copy(a, t, 2 * cx + cy, c, me).wait_recv()
                fw = copy(a, 3 + t, 2 * cx + cy, c, sib)
                fw.start()
                passed.append(fw)
        for a in range(n):
            for t, (cx, cy) in enumerate(chips):
                copy(a, 3 + t, 2 * cx + cy, 1 - c, me).wait_recv()
        for cp in first + passed:
            cp.wait_send()
        for cp in own:
            cp.wait()

    return pl.pallas_call(
        body,
        in_specs=[ANY] * n,
        out_specs=[ANY] * n,
        out_shape=[jax.ShapeDtypeStruct((4,) + s.shape, s.dtype) for s in shards],
        scratch_shapes=[pltpu.SemaphoreType.DMA((n, 6)), pltpu.SemaphoreType.DMA((n, 6)), pltpu.SemaphoreType.DMA((n,))],
        name="gather_weights",
    )(*shards)


def swap_halves(grads):
    n = len(grads)

    def body(*refs):
        ins, outs = refs[:n], refs[n:2 * n]
        send_sems, recv_sems = refs[2 * n:]
        x, y, c = _place()
        cps = []
        for a in range(n):
            h = ins[a].shape[1] // 2
            cps.append(pltpu.make_async_remote_copy(
                src_ref=ins[a].at[:, pl.ds((1 - c) * h, h)], dst_ref=outs[a],
                send_sem=send_sems.at[a], recv_sem=recv_sems.at[a], device_id=(x, y, 1 - c), device_id_type=MESH))
        for cp in cps:
            cp.start()
        for cp in cps:
            cp.wait()

    return pl.pallas_call(
        body,
        in_specs=[ANY] * n,
        out_specs=[ANY] * n,
        out_shape=[jax.ShapeDtypeStruct((4, g.shape[1] // 2, g.shape[2]), g.dtype) for g in grads],
        scratch_shapes=[pltpu.SemaphoreType.DMA((n,)), pltpu.SemaphoreType.DMA((n,))],
        name="swap_halves",
    )(*grads)


def add_halves(g, got, core, name):
    _, R, C = g.shape
    h = R // 2
    br = 128
    nb = h // br

    def body(core_ref, g_ref, got_ref, o_ref):
        o_ref[...] = (g_ref[...] + got_ref[...]).astype(BF16)

    return pl.pallas_call(
        body,
        grid_spec=pltpu.PrefetchScalarGridSpec(
            num_scalar_prefetch=1,
            grid=(4, nb),
            in_specs=[pl.BlockSpec((1, br, C), lambda s, i, core: (s, core[0] * nb + i, 0)),
                      pl.BlockSpec((1, br, C), lambda s, i, core: (s, i, 0))],
            out_specs=pl.BlockSpec((1, br, C), lambda s, i, core: (s, i, 0)),
        ),
        out_shape=jax.ShapeDtypeStruct((4, h, C), BF16),
        compiler_params=_params(("arbitrary", "arbitrary")),
        name=name,
    )(core, g, got)


def scatter_chips(sums):
    n = len(sums)

    def body(*refs):
        ins, outs = refs[:n], refs[n:2 * n]
        send_sems, recv_sems, local_sems = refs[2 * n:]
        x, y, c = _place()
        mine = 2 * x + y
        chips = _other_chips(x, y)
        own = [pltpu.make_async_copy(ins[a].at[mine], outs[a].at[mine], local_sems.at[a]) for a in range(n)]
        for cp in own:
            cp.start()
        cps = []
        for a in range(n):
            for t, (cx, cy) in enumerate(chips):
                cps.append(pltpu.make_async_remote_copy(
                    src_ref=ins[a].at[2 * cx + cy], dst_ref=outs[a].at[mine],
                    send_sem=send_sems.at[a, t], recv_sem=recv_sems.at[a, t], device_id=(cx, cy, c), device_id_type=MESH))
        for cp in cps:
            cp.start()
        for a in range(n):
            for t, (cx, cy) in enumerate(chips):
                pltpu.make_async_remote_copy(
                    src_ref=ins[a].at[mine], dst_ref=outs[a].at[2 * cx + cy],
                    send_sem=send_sems.at[a, t], recv_sem=recv_sems.at[a, t], device_id=(x, y, c), device_id_type=MESH).wait_recv()
        for cp in cps:
            cp.wait_send()
        for cp in own:
            cp.wait()

    return pl.pallas_call(
        body,
        in_specs=[ANY] * n,
        out_specs=[ANY] * n,
        out_shape=[jax.ShapeDtypeStruct(s.shape, s.dtype) for s in sums],
        scratch_shapes=[pltpu.SemaphoreType.DMA((n, 3)), pltpu.SemaphoreType.DMA((n, 3)), pltpu.SemaphoreType.DMA((n,))],
        name="scatter_chips",
    )(*sums)


def sum_slots(parts, name):
    S, R, C = parts.shape
    br = 128 if R % 128 == 0 else R

    def body(p_ref, o_ref):
        acc = p_ref[0].astype(F32)
        for s in range(1, S):
            acc = acc + p_ref[s].astype(F32)
        o_ref[...] = acc

    return pl.pallas_call(
        body,
        grid=(R // br,),
        in_specs=[pl.BlockSpec((S, br, C), lambda i: (0, i, 0))],
        out_specs=pl.BlockSpec((br, C), lambda i: (i, 0)),
        out_shape=jax.ShapeDtypeStruct((R, C), F32),
        compiler_params=_params(),
        name=name,
    )(parts)


def join_halves(halves):
    n = len(halves)

    def body(*refs):
        ins, outs = refs[:n], refs[n:2 * n]
        send_sems, recv_sems, local_sems = refs[2 * n:]
        x, y, c = _place()
        own, cps = [], []
        for a in range(n):
            h = ins[a].shape[0]
            dst = outs[a].at[pl.ds(c * h, h)]
            own.append(pltpu.make_async_copy(ins[a], dst, local_sems.at[a]))
            cps.append(pltpu.make_async_remote_copy(
                src_ref=ins[a], dst_ref=dst, send_sem=send_sems.at[a], recv_sem=recv_sems.at[a],
                device_id=(x, y, 1 - c), device_id_type=MESH))
        for cp in own + cps:
            cp.start()
        for a in range(n):
            h = ins[a].shape[0]
            pltpu.make_async_remote_copy(
                src_ref=ins[a], dst_ref=outs[a].at[pl.ds((1 - c) * h, h)], send_sem=send_sems.at[a], recv_sem=recv_sems.at[a],
                device_id=(x, y, c), device_id_type=MESH).wait_recv()
        for cp in cps:
            cp.wait_send()
        for cp in own:
            cp.wait()

    return pl.pallas_call(
        body,
        in_specs=[ANY] * n,
        out_specs=[ANY] * n,
        out_shape=[jax.ShapeDtypeStruct((2 * s.shape[0], s.shape[1]), s.dtype) for s in halves],
        scratch_shapes=[pltpu.SemaphoreType.DMA((n,)), pltpu.SemaphoreType.DMA((n,)), pltpu.SemaphoreType.DMA((n,))],
        name="join_halves",
    )(*halves)


SMALL_ROWS = 64


def gather_small(vec):
    def body(v_ref, o_ref, send_sems, recv_sems, local_sem):
        x, y, c = _place()
        flips = [(fx, fy, fc) for fx in (0, 1) for fy in (0, 1) for fc in (0, 1)][1:]

        def peer(f):
            return (1 - x if f[0] else x, 1 - y if f[1] else y, 1 - c if f[2] else c)

        slot = lambda p: 4 * p[0] + 2 * p[1] + p[2]
        own = pltpu.make_async_copy(v_ref, o_ref.at[slot((x, y, c))], local_sem)
        own.start()
        cps = [pltpu.make_async_remote_copy(
            src_ref=v_ref, dst_ref=o_ref.at[slot((x, y, c))], send_sem=send_sems.at[k], recv_sem=recv_sems.at[k],
            device_id=peer(f), device_id_type=MESH) for k, f in enumerate(flips)]
        for cp in cps:
            cp.start()
        for k, f in enumerate(flips):
            pltpu.make_async_remote_copy(
                src_ref=v_ref, dst_ref=o_ref.at[slot(peer(f))], send_sem=send_sems.at[k], recv_sem=recv_sems.at[k],
                device_id=(x, y, c), device_id_type=MESH).wait_recv()
        for cp in cps:
            cp.wait_send()
        own.wait()

    return pl.pallas_call(
        body,
        in_specs=[ANY],
        out_specs=ANY,
        out_shape=jax.ShapeDtypeStruct((8,) + vec.shape, vec.dtype),
        scratch_shapes=[pltpu.SemaphoreType.DMA((7,)), pltpu.SemaphoreType.DMA((7,)), pltpu.SemaphoreType.DMA],
        name="gather_small",
    )(vec)


GLOW_PAD = LANE - GLA_RANK


def kernel(x, attn_norm_g, w_in, gla_gate_w2, gla_gate_b, gla_norm_g, rel_bias, w_out, mlp_norm_g, w_ff1, w_ff2, final_norm_g, loss_target, m_attn_norm_g, m_w_in, m_gla_gate_w2, m_gla_gate_b, m_gla_norm_g, m_rel_bias, m_w_out, m_mlp_norm_g, m_w_ff1, m_w_ff2, m_final_norm_g, v_attn_norm_g, v_w_in, v_gla_gate_w2, v_gla_gate_b, v_gla_norm_g, v_rel_bias, v_w_out, v_mlp_norm_g, v_w_ff1, v_w_ff2, v_final_norm_g):
    xs, tgt = x[0], loss_target[0]
    T = xs.shape[0]
    cx, cy, cc = _place()
    chip = 2 * cx + cy
    gf = final_norm_g.reshape(1, D_MODEL)

    win_g, wout_g, wff1, wff2_g, w2_g = gather_weights(
        [w_in[0].astype(BF16), w_out[0].astype(BF16), w_ff1[0].astype(BF16), w_ff2[0].astype(BF16), gla_gate_w2[0]])
    win = jnp.transpose(win_g, (1, 0, 2)).reshape(D_MODEL, D_IN)
    n_glow = R_GLOW + GLA_RANK
    wp = jnp.concatenate([win[:, :n_glow], jnp.zeros((D_MODEL, GLOW_PAD), BF16), win[:, n_glow:]], axis=1)
    wout = wout_g.reshape(D_MODEL, D_MODEL)
    wff2 = wff2_g.reshape(D_FF, D_MODEL)
    w2 = jnp.transpose(w2_g, (1, 0, 2)).reshape(GLA_RANK, GLA_QK)
    w2p = jnp.concatenate([w2, jnp.zeros((GLOW_PAD, GLA_QK), F32)], axis=0)

    proj, nx = inproj(xs, attn_norm_g, wp)
    oa, opre, sprev = gla_fwd(proj, w2p, gla_gate_b, gla_norm_g)
    tab = bias_tables(rel_bias)
    ob, lse = dsa_fwd(proj, tab)
    mixed, nm, act, dpre, dh2, dh1, dmixed, loss, dgf, dg2 = post_fused(xs, oa, ob, tgt, mlp_norm_g, gf, wout, wff1, wff2)
    da, dw2p, dbg, dgn = gla_bwd(proj, w2p, gla_gate_b, gla_norm_g, opre, sprev, dmixed)
    dq, dk, dv, dtab = dsa_bwd(proj, tab, ob, lse, dmixed)
    drel = bias_tables_bwd(dtab)
    dxs, dg1 = dx_final(xs, dh1, attn_norm_g, da, dq, dk, dv, wp)

    dwff2 = wgrad(act, dh2, "wgrad_ff2", bm=FF_BLOCK)
    dwff1 = wgrad(nm, dpre, "wgrad_ff1", bn=FF_BLOCK, col_blocked=True)
    dwout = wgrad(mixed, dh1, "wgrad_out")
    dwa = wgrad(nx, da, "wgrad_in_gla")
    dwq = wgrad(nx, dq, "wgrad_in_q")
    dwk = wgrad(nx, dk, "wgrad_in_k")
    dwv = wgrad(nx, dv, "wgrad_in_v")
    dwin = jnp.concatenate([dwa[:, :n_glow], dwq, dwk, dwv], axis=1)
    big = [
        jnp.transpose(dwin.reshape(D_MODEL, 4, D_IN // 4), (1, 0, 2)),
        dwout.reshape(4, D_MODEL // 4, D_MODEL),
        dwff1,
        dwff2.reshape(4, FF_BLOCK, D_MODEL),
    ]

    names = ["in", "out", "ff1", "ff2"]
    got = swap_halves(big)
    core = cc.astype(jnp.int32).reshape(1)
    sums = [add_halves(g, r, core, "add_halves_" + s) for g, r, s in zip(big, got, names)]
    slots = scatter_chips(sums)
    halves = [sum_slots(p, "sum_chips_" + s) for p, s in zip(slots, names)]
    g_win, g_wout, g_wff1, g_wff2 = join_halves(halves)

    small = jnp.concatenate([dg1.reshape(-1), dbg.reshape(-1), dgn.reshape(-1), drel.reshape(-1), dg2.reshape(-1),
                             dgf.reshape(-1), dw2p[:GLA_RANK].reshape(-1)]).reshape(SMALL_ROWS, LANE)
    tot = sum_slots(gather_small(small), "sum_small").reshape(-1)
    sizes = [D_MODEL, GLA_QK, GLA_WIDTH, REL_BUCKETS * DSA_HEADS, D_MODEL, D_MODEL, GLA_RANK * GLA_QK]
    offs = np.concatenate([[0], np.cumsum(sizes)])
    piece = lambda i: tot[int(offs[i]):int(offs[i + 1])]
    g_g1 = piece(0).reshape(1, D_MODEL)
    g_bg = piece(1).reshape(1, GLA_QK)
    g_gn = piece(2).reshape(1, GLA_WIDTH)
    g_rel = piece(3).reshape(REL_BUCKETS, DSA_HEADS)
    g_g2 = piece(4).reshape(1, D_MODEL)
    g_gf = piece(5).reshape(1, D_MODEL)
    g_w2 = lax.dynamic_slice_in_dim(piece(6).reshape(GLA_RANK, GLA_QK), chip * (GLA_QK // 4), GLA_QK // 4, axis=1)

    loss_all = lax.psum(loss[0, 0], ("x", "y", "c"))

    upd = [
        ("attn_norm_g", attn_norm_g, g_g1, m_attn_norm_g, v_attn_norm_g),
        ("w_in", w_in[0], g_win, m_w_in[0], v_w_in[0]),
        ("gla_gate_w2", gla_gate_w2[0], g_w2, m_gla_gate_w2[0], v_gla_gate_w2[0]),
        ("gla_gate_b", gla_gate_b, g_bg, m_gla_gate_b, v_gla_gate_b),
        ("gla_norm_g", gla_norm_g, g_gn, m_gla_norm_g, v_gla_norm_g),
        ("rel_bias", rel_bias, g_rel, m_rel_bias, v_rel_bias),
        ("w_out", w_out[0], g_wout, m_w_out[0], v_w_out[0]),
        ("mlp_norm_g", mlp_norm_g, g_g2, m_mlp_norm_g, v_mlp_norm_g),
        ("w_ff1", w_ff1[0], g_wff1, m_w_ff1[0], v_w_ff1[0]),
        ("w_ff2", w_ff2[0], g_wff2, m_w_ff2[0], v_w_ff2[0]),
        ("final_norm_g", gf, g_gf, m_final_norm_g.reshape(1, D_MODEL), v_final_norm_g.reshape(1, D_MODEL)),
    ]
    shapes = [attn_norm_g.shape, w_in.shape, gla_gate_w2.shape, gla_gate_b.shape, gla_norm_g.shape, rel_bias.shape,
              w_out.shape, mlp_norm_g.shape, w_ff1.shape, w_ff2.shape, final_norm_g.shape]
    grads, deltas, new_m, new_v = [], [], [], []
    for (name, w, g, m, v), shape in zip(upd, shapes):
        d, nm_, nv_ = adamw(w, g, m, v, "adamw_" + name)
        grads.append(g.reshape(shape))
        deltas.append(d.reshape(shape))
        new_m.append(nm_.reshape(shape))
        new_v.append(nv_.reshape(shape))
    return (loss_all, dxs.reshape(1, T, D_MODEL), *grads, *deltas, *new_m, *new_v)
```

```python
import functools
import math

import jax
import jax.numpy as jnp
import numpy as np
from jax import lax
from jax.experimental import pallas as pl
from jax.experimental.pallas import tpu as pltpu

F32 = jnp.float32
BF16 = jnp.bfloat16
MESH = pl.DeviceIdType.MESH

D_MODEL = 1024
GLA_WIDTH = 512
GLA_HEADS = 4
GLA_DK = 64
GLA_DV = 128
GLA_QK = GLA_HEADS * GLA_DK
GLA_RANK = 16
GLA_TAU = 16.0
GLA_CHUNK = 64
DSA_WIDTH = 512
DSA_HEADS = 8
DSA_DH = 64
DSA_DILATIONS = (1, 4, 16)
DSA_SPAN = 128
DSA_BLOCK = 128
DSA_SUPER = DSA_BLOCK * DSA_DILATIONS[-1]
REL_BUCKETS = 32
REL_MAX_DIST = 2048
D_FF = 4096
D_IN = 3088
EPS = 1e-6
NEG = -1e30
QK_SCALE = 0.125

ADAM_LR = 0.001
ADAM_B1 = 0.9
ADAM_B2 = 0.999
ADAM_EPS = 1e-08
ADAM_WD = 0.01
ADAM_STEP = 10

LANE = 128
P_GQ, P_GK, P_GV, P_GR = 0, 256, 512, 1024
P_GLOW = 1536
P_A = 1664
P_DQ, P_DK, P_DV = 1664, 2176, 2688
P_ALL = 3200
R_GQ, R_GK, R_GV, R_GR, R_GLOW, R_DQ, R_DK, R_DV = 0, 256, 512, 1024, 1536, 1552, 2064, 2576

VMEM_LIMIT = 56 * 1024 * 1024


def _params(sem=("arbitrary",), vmem=VMEM_LIMIT):
    return pltpu.CompilerParams(dimension_semantics=sem, vmem_limit_bytes=vmem)


def _dot(a, b):
    return jnp.dot(a, b, preferred_element_type=F32)


def _dot_nt(a, b):
    return lax.dot_general(a, b, (((1,), (1,)), ((), ())), preferred_element_type=F32)


def _dot_tn(a, b):
    return lax.dot_general(a, b, (((0,), (0,)), ((), ())), preferred_element_type=F32)


def _split3(x):
    x1 = x.astype(BF16)
    r1 = x - x1.astype(F32)
    x2 = r1.astype(BF16)
    x3 = (r1 - x2.astype(F32)).astype(BF16)
    return x1, x2, x3


def _dot_exact_lhs(m_bf16, x):
    x1, x2, x3 = _split3(x)
    return _dot(m_bf16, x1) + _dot(m_bf16, x2) + _dot(m_bf16, x3)


def _rstd(xf):
    return lax.rsqrt(jnp.mean(xf * xf, axis=-1, keepdims=True) + EPS)


def _load_once(hbm_ref, vmem_ref, sem):
    cp = pltpu.make_async_copy(hbm_ref, vmem_ref, sem)
    cp.start()
    cp.wait()


ANY = pl.BlockSpec(memory_space=pl.ANY)


def inproj(x, g1, wp):
    T = x.shape[0]
    tm = 256

    def body(x_ref, g_ref, w_hbm, proj_ref, nx_ref, w_vmem, sem):
        @pl.when(pl.program_id(0) == 0)
        def _():
            _load_once(w_hbm, w_vmem, sem)

        xf = x_ref[...]
        nx = ((xf * _rstd(xf)) * g_ref[...]).astype(BF16)
        nx_ref[...] = nx
        proj_ref[...] = _dot(nx, w_vmem[...])

    return pl.pallas_call(
        body,
        grid=(T // tm,),
        in_specs=[pl.BlockSpec((tm, D_MODEL), lambda i: (i, 0)), pl.BlockSpec((1, D_MODEL), lambda i: (0, 0)), ANY],
        out_specs=[pl.BlockSpec((tm, P_ALL), lambda i: (i, 0)), pl.BlockSpec((tm, D_MODEL), lambda i: (i, 0))],
        out_shape=[jax.ShapeDtypeStruct((T, P_ALL), F32), jax.ShapeDtypeStruct((T, D_MODEL), BF16)],
        scratch_shapes=[pltpu.VMEM((D_MODEL, P_ALL), BF16), pltpu.SemaphoreType.DMA],
        compiler_params=_params(),
        name="inproj",
    )(x, g1, wp)


GLA_CHUNKS_PER_STEP = 8
GLA_ROWS = GLA_CHUNK * GLA_CHUNKS_PER_STEP
GLA_UNROLL = 4


def _gla_masks():
    lane = lax.broadcasted_iota(jnp.int32, (1, GLA_QK), 1)
    return [(lane >= h * GLA_DK) & (lane < (h + 1) * GLA_DK) for h in range(GLA_HEADS)]


def _log_sigmoid(x):
    return jnp.minimum(x, 0.0) - jnp.log(1.0 + jnp.exp(-jnp.abs(x)))


def _sigmoid(x):
    return 1.0 / (1.0 + jnp.exp(-x))


def _phased_loop(n, unroll, load, compute, store, init):
    def body(i, carry):
        steps = [i * unroll + u for u in range(unroll)]
        loaded = [load(s) for s in steps]
        results = []
        for vals in loaded:
            carry, res = compute(vals, carry)
            results.append(res)
        for s, res in zip(steps, results):
            store(s, res)
        return carry

    return lax.fori_loop(0, n // unroll, body, init)


def _gla_chunk_rows(c):
    return pl.ds(pl.multiple_of(c * GLA_CHUNK, GLA_CHUNK), GLA_CHUNK)


def _gla_chunk_common(q, k, glow, w2, bg, tri):
    gpre = _dot(glow, w2) + bg
    glog = _log_sigmoid(gpre) / GLA_TAU
    b = _dot_exact_lhs(tri, glog)
    bl = b[GLA_CHUNK - 1:GLA_CHUNK, :]
    eb = jnp.exp(b)
    enb = jnp.exp(-b)
    eke = jnp.exp(bl - b)
    qd = (q * QK_SCALE) * eb
    ki = k * enb
    ke = k * eke
    return gpre, eb, enb, eke, jnp.exp(bl), qd, ki, ke


def _head_cols(h):
    return slice(h * GLA_DV, (h + 1) * GLA_DV)


def gla_fwd(proj, w2p, bg, gn):
    T = proj.shape[0]
    n_steps = T // GLA_ROWS
    n_chunks = T // GLA_CHUNK

    def body(proj_ref, w2_ref, bg_ref, gn_ref, oa_ref, opre_ref, sprev_ref, st_ref):
        @pl.when(pl.program_id(0) == 0)
        def _():
            st_ref[...] = jnp.zeros_like(st_ref)

        masks = _gla_masks()
        ri = lax.broadcasted_iota(jnp.int32, (GLA_CHUNK, GLA_CHUNK), 0)
        ci = lax.broadcasted_iota(jnp.int32, (GLA_CHUNK, GLA_CHUNK), 1)
        causal = ri >= ci
        tri = causal.astype(BF16)
        w2 = w2_ref[...].astype(BF16)
        bg = bg_ref[...]

        gn = gn_ref[...]

        def load(c):
            rows = _gla_chunk_rows(c)
            return (proj_ref[rows, P_GQ:P_GQ + GLA_QK], proj_ref[rows, P_GK:P_GK + GLA_QK],
                    proj_ref[rows, P_GV:P_GV + GLA_WIDTH], proj_ref[rows, P_GR:P_GR + GLA_WIDTH],
                    proj_ref[rows, P_GLOW:P_GLOW + LANE])

        def compute(vals, st):
            q, k, v, r, glow = vals
            _, _, _, _, ebl, qd, ki, ke = _gla_chunk_common(q, k, glow.astype(BF16), w2, bg, tri)
            st_b = st.astype(BF16)
            ki_b = ki.astype(BF16)
            inc = jnp.zeros_like(st)
            o_heads = []
            for h in range(GLA_HEADS):
                v_h = v[:, _head_cols(h)].astype(BF16)
                qd_h = jnp.where(masks[h], qd, 0.0).astype(BF16)
                ke_h = jnp.where(masks[h], ke, 0.0).astype(BF16)
                att = jnp.where(causal, _dot_nt(qd_h, ki_b), 0.0)
                o_heads.append(_dot(att.astype(BF16), v_h) + _dot_nt(qd_h, st_b))
                inc = inc + _dot_tn(v_h, ke_h)
            o = jnp.concatenate(o_heads, axis=1)
            on = jnp.concatenate([o_h * _rstd(o_h) for o_h in o_heads], axis=1)
            oa = ((on * gn) * (r * _sigmoid(r))).astype(BF16)
            return st * ebl + inc, (st, o, oa)

        def store(c, res):
            rows = _gla_chunk_rows(c)
            sprev_ref[c], opre_ref[rows, :], oa_ref[rows, :] = res

        st_ref[...] = _phased_loop(GLA_CHUNKS_PER_STEP, GLA_UNROLL, load, compute, store, st_ref[...])

    return pl.pallas_call(
        body,
        grid=(n_steps,),
        in_specs=[
            pl.BlockSpec((GLA_ROWS, P_ALL), lambda i: (i, 0)),
            pl.BlockSpec((LANE, GLA_QK), lambda i: (0, 0)),
            pl.BlockSpec((1, GLA_QK), lambda i: (0, 0)),
            pl.BlockSpec((1, GLA_WIDTH), lambda i: (0, 0)),
        ],
        out_specs=[
            pl.BlockSpec((GLA_ROWS, GLA_WIDTH), lambda i: (i, 0)),
            pl.BlockSpec((GLA_ROWS, GLA_WIDTH), lambda i: (i, 0)),
            pl.BlockSpec((GLA_CHUNKS_PER_STEP, GLA_DV, GLA_QK), lambda i: (i, 0, 0)),
        ],
        out_shape=[
            jax.ShapeDtypeStruct((T, GLA_WIDTH), BF16),
            jax.ShapeDtypeStruct((T, GLA_WIDTH), F32),
            jax.ShapeDtypeStruct((n_chunks, GLA_DV, GLA_QK), F32),
        ],
        scratch_shapes=[pltpu.VMEM((GLA_DV, GLA_QK), F32)],
        compiler_params=_params(),
        name="gla_fwd",
    )(proj, w2p, bg, gn)


def gla_bwd(proj, w2p, bg, gn, opre, sprev, dmixed):
    T = proj.shape[0]
    n_steps = T // GLA_ROWS

    def body(proj_ref, w2_ref, bg_ref, gn_ref, opre_ref, sprev_ref, doa_ref, da_ref, dw2_ref, dbg_ref, dgn_ref, dst_ref):
        @pl.when(pl.program_id(0) == 0)
        def _():
            dst_ref[...] = jnp.zeros_like(dst_ref)
            dw2_ref[...] = jnp.zeros_like(dw2_ref)
            dbg_ref[...] = jnp.zeros_like(dbg_ref)
            dgn_ref[...] = jnp.zeros_like(dgn_ref)

        masks = _gla_masks()
        ri = lax.broadcasted_iota(jnp.int32, (GLA_CHUNK, GLA_CHUNK), 0)
        ci = lax.broadcasted_iota(jnp.int32, (GLA_CHUNK, GLA_CHUNK), 1)
        causal = ri >= ci
        tri = causal.astype(BF16)
        tri_t = (ri <= ci).astype(BF16)
        last_row = lax.broadcasted_iota(jnp.int32, (GLA_CHUNK, GLA_QK), 0) == GLA_CHUNK - 1
        w2 = w2_ref[...].astype(BF16)
        bg = bg_ref[...]

        gn = gn_ref[...]

        def load(j):
            c = GLA_CHUNKS_PER_STEP - 1 - j
            rows = _gla_chunk_rows(c)
            return (proj_ref[rows, P_GQ:P_GQ + GLA_QK], proj_ref[rows, P_GK:P_GK + GLA_QK],
                    proj_ref[rows, P_GV:P_GV + GLA_WIDTH], proj_ref[rows, P_GR:P_GR + GLA_WIDTH],
                    proj_ref[rows, P_GLOW:P_GLOW + LANE], opre_ref[rows, :], doa_ref[rows, :], sprev_ref[c])

        def compute(vals, carry):
            dst, dw2, dbg, dgn = carry
            q, k, v, r, glow, o, doa, st = vals
            glow = glow.astype(BF16)
            gpre, eb, enb, eke, ebl, qd, ki, ke = _gla_chunk_common(q, k, glow, w2, bg, tri)
            sig = _sigmoid(r)
            rs = jnp.concatenate([jnp.broadcast_to(_rstd(o[:, _head_cols(h)]), (GLA_CHUNK, GLA_DV))
                                  for h in range(GLA_HEADS)], axis=1)
            on = o * rs
            d_ong = doa * (r * sig)
            dr = doa * (on * gn) * (sig * (1.0 + r * (1.0 - sig)))
            dgn = dgn + jnp.sum(d_ong * on, axis=0, keepdims=True)
            d_on = d_ong * gn
            t = d_on * on
            mean_t = jnp.concatenate([jnp.broadcast_to(jnp.mean(t[:, _head_cols(h)], axis=-1, keepdims=True), (GLA_CHUNK, GLA_DV))
                                      for h in range(GLA_HEADS)], axis=1)
            do = rs * (d_on - on * mean_t)
            st_b = st.astype(BF16)
            dst_b = dst.astype(BF16)
            ki_b = ki.astype(BF16)
            dqd = jnp.zeros_like(qd)
            dki = jnp.zeros_like(qd)
            dke = jnp.zeros_like(qd)
            dst_add = jnp.zeros_like(st)
            dv_heads = []
            for h in range(GLA_HEADS):
                do_b = do[:, _head_cols(h)].astype(BF16)
                v_h = v[:, _head_cols(h)].astype(BF16)
                qd_h = jnp.where(masks[h], qd, 0.0).astype(BF16)
                ke_h = jnp.where(masks[h], ke, 0.0).astype(BF16)
                att = jnp.where(causal, _dot_nt(qd_h, ki_b), 0.0).astype(BF16)
                d_att = jnp.where(causal, _dot_nt(do_b, v_h), 0.0).astype(BF16)
                dv_heads.append(_dot_tn(att, do_b) + _dot_nt(ke_h, dst_b))
                dqd = dqd + jnp.where(masks[h], _dot(d_att, ki_b) + _dot(do_b, st_b), 0.0)
                dki = dki + _dot_tn(d_att, qd_h)
                dke = dke + jnp.where(masks[h], _dot(v_h, dst_b), 0.0)
                dst_add = dst_add + _dot_tn(do_b, qd_h)
            debl = jnp.sum(dst * st, axis=0, keepdims=True)
            dkk = dke * ke
            db = dqd * qd - dki * ki - dkk
            dbl = jnp.sum(dkk, axis=0, keepdims=True) + debl * ebl
            db = db + jnp.where(last_row, dbl, 0.0)
            dglog = _dot_exact_lhs(tri_t, db)
            dgpre = (dglog / GLA_TAU) * _sigmoid(-gpre)
            dgpre_b = dgpre.astype(BF16)
            da = jnp.concatenate([dqd * eb * QK_SCALE, dki * enb + dke * eke] + dv_heads + [dr, _dot_nt(dgpre_b, w2)],
                                 axis=1).astype(BF16)
            carry = (dst * ebl + dst_add, dw2 + _dot_tn(glow, dgpre_b), dbg + jnp.sum(dgpre, axis=0, keepdims=True), dgn)
            return carry, da

        def store(j, da):
            da_ref[_gla_chunk_rows(GLA_CHUNKS_PER_STEP - 1 - j), :] = da

        carry = (dst_ref[...], jnp.zeros((LANE, GLA_QK), F32), jnp.zeros((1, GLA_QK), F32), jnp.zeros((1, GLA_WIDTH), F32))
        dst, dw2, dbg, dgn = _phased_loop(GLA_CHUNKS_PER_STEP, GLA_UNROLL, load, compute, store, carry)
        dst_ref[...] = dst
        dw2_ref[...] += dw2
        dbg_ref[...] += dbg
        dgn_ref[...] += dgn

    rev = lambda i: (n_steps - 1 - i, 0)
    return pl.pallas_call(
        body,
        grid=(n_steps,),
        in_specs=[
            pl.BlockSpec((GLA_ROWS, P_ALL), rev),
            pl.BlockSpec((LANE, GLA_QK), lambda i: (0, 0)),
            pl.BlockSpec((1, GLA_QK), lambda i: (0, 0)),
            pl.BlockSpec((1, GLA_WIDTH), lambda i: (0, 0)),
            pl.BlockSpec((GLA_ROWS, GLA_WIDTH), rev),
            pl.BlockSpec((GLA_CHUNKS_PER_STEP, GLA_DV, GLA_QK), lambda i: (n_steps - 1 - i, 0, 0)),
            pl.BlockSpec((GLA_ROWS, GLA_WIDTH), rev),
        ],
        out_specs=[
            pl.BlockSpec((GLA_ROWS, P_A), rev),
            pl.BlockSpec((LANE, GLA_QK), lambda i: (0, 0)),
            pl.BlockSpec((1, GLA_QK), lambda i: (0, 0)),
            pl.BlockSpec((1, GLA_WIDTH), lambda i: (0, 0)),
        ],
        out_shape=[
            jax.ShapeDtypeStruct((T, P_A), BF16),
            jax.ShapeDtypeStruct((LANE, GLA_QK), F32),
            jax.ShapeDtypeStruct((1, GLA_QK), F32),
            jax.ShapeDtypeStruct((1, GLA_WIDTH), F32),
        ],
        scratch_shapes=[pltpu.VMEM((GLA_DV, GLA_QK), F32)],
        compiler_params=_params(),
        name="gla_bwd",
    )(proj, w2p, bg, gn, opre, sprev, dmixed)


def _t5_bucket(dist):
    max_exact = REL_BUCKETS // 2
    n = np.maximum(dist, 0)
    large = max_exact + (np.log(np.maximum(n, 1) / max_exact) / math.log(REL_MAX_DIST / max_exact)
                         * (REL_BUCKETS - max_exact)).astype(np.int32)
    large = np.minimum(large, REL_BUCKETS - 1)
    return np.where(n < max_exact, n, large).astype(np.int32)


def _bucket_ids():
    L = DSA_BLOCK
    steps = L + np.arange(L)[:, None] - np.arange(2 * L)[None, :]
    in_band = (steps >= 0) & (steps <= DSA_SPAN)
    return np.stack([np.where(in_band, _t5_bucket(steps * d), -1) for d in DSA_DILATIONS]).astype(np.int32)


def bias_tables(rel_bias):
    ids = jnp.asarray(_bucket_ids())
    nd = len(DSA_DILATIONS)

    def body(rel_ref, ids_ref, tab_ref):
        h = pl.program_id(1)
        idt = ids_ref[0]
        acc = jnp.where(idt < 0, NEG, 0.0).astype(F32)
        for b in range(REL_BUCKETS):
            acc = jnp.where(idt == b, rel_ref[b, h], acc)
        tab_ref[0, 0] = acc

    return pl.pallas_call(
        body,
        grid=(nd, DSA_HEADS),
        in_specs=[pl.BlockSpec(memory_space=pltpu.SMEM), pl.BlockSpec((1, DSA_BLOCK, 2 * DSA_BLOCK), lambda d, h: (d, 0, 0))],
        out_specs=pl.BlockSpec((1, 1, DSA_BLOCK, 2 * DSA_BLOCK), lambda d, h: (d, h, 0, 0)),
        out_shape=jax.ShapeDtypeStruct((nd, DSA_HEADS, DSA_BLOCK, 2 * DSA_BLOCK), F32),
        compiler_params=_params(("arbitrary", "arbitrary")),
        name="bias_tables",
    )(rel_bias, ids)


def bias_tables_bwd(dtab):
    ids = jnp.asarray(_bucket_ids())
    nd = len(DSA_DILATIONS)

    def body(dtab_ref, ids_ref, drel_ref):
        @pl.when((pl.program_id(0) == 0) & (pl.program_id(1) == 0))
        def _():
            for b in range(REL_BUCKETS):
                for h in range(DSA_HEADS):
                    drel_ref[b, h] = 0.0

        h = pl.program_id(1)
        idt = ids_ref[0]
        g = dtab_ref[0, 0]
        for b in range(REL_BUCKETS):
            drel_ref[b, h] += jnp.sum(jnp.where(idt == b, g, 0.0))

    return pl.pallas_call(
        body,
        grid=(nd, DSA_HEADS),
        in_specs=[pl.BlockSpec((1, 1, DSA_BLOCK, 2 * DSA_BLOCK), lambda d, h: (d, h, 0, 0)),
                  pl.BlockSpec((1, DSA_BLOCK, 2 * DSA_BLOCK), lambda d, h: (d, 0, 0))],
        out_specs=pl.BlockSpec(memory_space=pltpu.SMEM),
        out_shape=jax.ShapeDtypeStruct((REL_BUCKETS, DSA_HEADS), F32),
        compiler_params=_params(("arbitrary", "arbitrary")),
        name="bias_tables_bwd",
    )(dtab, ids)


DSA_PAIRS = DSA_HEADS // 2
DSA_UNROLL = 8
DSA_COMBINE_ROWS = 256


def _dsa_units(d):
    return d, DSA_SUPER // (DSA_BLOCK * d)


def _dsa_specs(T):
    nsb = T // DSA_SUPER
    qcol, kcol, vcol = P_DQ // LANE, P_DK // LANE, P_DV // LANE
    return nsb, qcol, kcol, vcol


def _head_lane_mask():
    return lax.broadcasted_iota(jnp.int32, (1, LANE), 1) < DSA_DH


def _first_block_penalty(first):
    col = lax.broadcasted_iota(jnp.int32, (2 * DSA_BLOCK, 2 * DSA_BLOCK), 1)
    return jnp.where(first & (col < DSA_BLOCK), NEG, 0.0).astype(F32)


def _pair_tiles(tab):
    return tab.reshape(len(DSA_DILATIONS), DSA_PAIRS, 2 * DSA_BLOCK, 2 * DSA_BLOCK)


def _stack_heads(t, head0):
    return jnp.concatenate([jnp.where(head0, t, 0.0), jnp.where(head0, 0.0, t)], axis=0)


def dsa_fwd(proj, tab):
    T = proj.shape[0]
    nsb, qcol, kcol, vcol = _dsa_specs(T)
    S = DSA_SUPER

    def body(q_ref, kp_ref, kc_ref, vp_ref, vc_ref, tab_ref, out_ref, lse_ref, kk, vv, ob, lb):
        sb = pl.program_id(1)
        kk[0:S, :] = kp_ref[...]
        kk[S:2 * S, :] = kc_ref[...]
        vv[0:S, :] = vp_ref[...]
        vv[S:2 * S, :] = vc_ref[...]
        head0 = _head_lane_mask()

        for di, d in enumerate(DSA_DILATIONS):
            n_res, n_blk = _dsa_units(d)

            def unit(u, carry, di=di, d=d, n_blk=n_blk):
                r = u // n_blk
                c = u % n_blk
                q0 = r + d * DSA_BLOCK * c
                qrows = pl.ds(q0, DSA_BLOCK, stride=d) if d > 1 else pl.ds(q0, DSA_BLOCK)
                krows = pl.ds(S + q0 - d * DSA_BLOCK, 2 * DSA_BLOCK, stride=d) if d > 1 else pl.ds(S + q0 - DSA_BLOCK, 2 * DSA_BLOCK)
                q2 = q_ref[qrows, :] * QK_SCALE
                k2 = kk[krows, :].astype(BF16)
                v2 = vv[krows, :].astype(BF16)
                qs = _stack_heads(q2, head0).astype(BF16)
                s = _dot_nt(qs, k2) + (tab_ref[di, 0] + _first_block_penalty((sb == 0) & (c == 0)))
                m = jnp.max(s, axis=-1, keepdims=True)
                p = jnp.exp(s - m)
                den = jnp.sum(p, axis=-1, keepdims=True)
                o = _dot(p.astype(BF16), v2) / den
                l = jnp.broadcast_to(m + jnp.log(den), (2 * DSA_BLOCK, LANE))
                ob[di, qrows, :] = jnp.where(head0, o[:DSA_BLOCK], o[DSA_BLOCK:])
                lb[di, qrows, :] = jnp.where(head0, l[:DSA_BLOCK], l[DSA_BLOCK:])
                return carry

            lax.fori_loop(0, n_res * n_blk, unit, 0, unroll=DSA_UNROLL)

        def combine(i, carry):
            rows = pl.ds(pl.multiple_of(i * DSA_COMBINE_ROWS, DSA_COMBINE_ROWS), DSA_COMBINE_ROWS)
            l0, l1, l2 = lb[0, rows, :], lb[1, rows, :], lb[2, rows, :]
            mx = jnp.maximum(jnp.maximum(l0, l1), l2)
            e0, e1, e2 = jnp.exp(l0 - mx), jnp.exp(l1 - mx), jnp.exp(l2 - mx)
            den = e0 + e1 + e2
            out_ref[rows, :] = (e0 * ob[0, rows, :] + e1 * ob[1, rows, :] + e2 * ob[2, rows, :]) / den
            lse_ref[rows, :] = mx + jnp.log(den)
            return carry

        lax.fori_loop(0, S // DSA_COMBINE_ROWS, combine, 0)

    prev = lambda col: (lambda hp, sb: (jnp.maximum(sb - 1, 0), col + hp))
    cur = lambda col: (lambda hp, sb: (sb, col + hp))
    blk = lambda f: pl.BlockSpec((S, LANE), f)
    return pl.pallas_call(
        body,
        grid=(DSA_PAIRS, nsb),
        in_specs=[blk(cur(qcol)), blk(prev(kcol)), blk(cur(kcol)), blk(prev(vcol)), blk(cur(vcol)),
                  pl.BlockSpec((len(DSA_DILATIONS), 1, 2 * DSA_BLOCK, 2 * DSA_BLOCK), lambda hp, sb: (0, hp, 0, 0))],
        out_specs=[blk(lambda hp, sb: (sb, hp)), blk(lambda hp, sb: (sb, hp))],
        out_shape=[jax.ShapeDtypeStruct((T, DSA_WIDTH), F32), jax.ShapeDtypeStruct((T, DSA_WIDTH), F32)],
        scratch_shapes=[pltpu.VMEM((2 * S, LANE), F32), pltpu.VMEM((2 * S, LANE), F32),
                        pltpu.VMEM((len(DSA_DILATIONS), S, LANE), F32), pltpu.VMEM((len(DSA_DILATIONS), S, LANE), F32)],
        compiler_params=_params(("arbitrary", "arbitrary")),
        name="dsa_fwd",
    )(proj, proj, proj, proj, proj, _pair_tiles(tab))


def dsa_bwd(proj, tab, ob_out, lse, dmixed):
    T = proj.shape[0]
    nsb, qcol, kcol, vcol = _dsa_specs(T)
    S = DSA_SUPER
    nd = len(DSA_DILATIONS)
    ocol = GLA_WIDTH // LANE

    def body(q_ref, kp_ref, kc_ref, vp_ref, vc_ref, tab_ref, o_ref, lse_ref, do_ref,
             dq_ref, dk_ref, dv_ref, dtab_ref, kk, vv, dqa, dkk, dvv):
        j = pl.program_id(1)
        sb = nsb - 1 - j
        kk[0:S, :] = kp_ref[...]
        kk[S:2 * S, :] = kc_ref[...]
        vv[0:S, :] = vp_ref[...]
        vv[S:2 * S, :] = vc_ref[...]
        head0 = _head_lane_mask()

        @pl.when(j == 0)
        def _():
            dtab_ref[...] = jnp.zeros_like(dtab_ref)
            dkk[S:2 * S, :] = jnp.zeros((S, LANE), F32)
            dvv[S:2 * S, :] = jnp.zeros((S, LANE), F32)

        @pl.when(j > 0)
        def _():
            dkk[S:2 * S, :] = dkk[0:S, :]
            dvv[S:2 * S, :] = dvv[0:S, :]

        dkk[0:S, :] = jnp.zeros((S, LANE), F32)
        dvv[0:S, :] = jnp.zeros((S, LANE), F32)
        dqa[...] = jnp.zeros_like(dqa)

        for di, d in enumerate(DSA_DILATIONS):
            n_res, n_blk = _dsa_units(d)

            def unit(u, carry, di=di, d=d, n_blk=n_blk):
                r = u // n_blk
                c = u % n_blk
                q0 = r + d * DSA_BLOCK * c
                qrows = pl.ds(q0, DSA_BLOCK, stride=d) if d > 1 else pl.ds(q0, DSA_BLOCK)
                krows = pl.ds(S + q0 - d * DSA_BLOCK, 2 * DSA_BLOCK, stride=d) if d > 1 else pl.ds(S + q0 - DSA_BLOCK, 2 * DSA_BLOCK)
                q2 = q_ref[qrows, :] * QK_SCALE
                k2 = kk[krows, :].astype(BF16)
                v2 = vv[krows, :].astype(BF16)
                do2 = do_ref[qrows, :]
                o2 = o_ref[qrows, :]
                l2 = lse_ref[qrows, :]
                qs = _stack_heads(q2, head0).astype(BF16)
                dos = _stack_heads(do2, head0)
                dos_b = dos.astype(BF16)
                delta = jnp.sum(dos * jnp.concatenate([o2, o2], axis=0), axis=-1, keepdims=True)
                lse = jnp.concatenate([jnp.max(jnp.where(head0, l2, -jnp.inf), axis=-1, keepdims=True),
                                       jnp.max(jnp.where(head0, -jnp.inf, l2), axis=-1, keepdims=True)], axis=0)
                s = _dot_nt(qs, k2) + (tab_ref[di, 0] + _first_block_penalty((sb == 0) & (c == 0)))
                p = jnp.exp(s - lse)
                ds = p * (_dot_nt(dos_b, v2) - delta)
                dtab_ref[di, 0] += ds
                ds_b = ds.astype(BF16)
                dq = _dot(ds_b, k2)
                dqa[qrows, :] += jnp.where(head0, dq[:DSA_BLOCK], dq[DSA_BLOCK:]) * QK_SCALE
                dkk[krows, :] += _dot_tn(ds_b, qs)
                dvv[krows, :] += _dot_tn(p.astype(BF16), dos_b)
                return carry

            lax.fori_loop(0, n_res * n_blk, unit, 0, unroll=DSA_UNROLL)

        dq_ref[...] = dqa[...].astype(BF16)
        dk_ref[...] = dkk[S:2 * S, :].astype(BF16)
        dv_ref[...] = dvv[S:2 * S, :].astype(BF16)

    prev = lambda col: (lambda hp, j: (jnp.maximum(nsb - 2 - j, 0), col + hp))
    cur = lambda col: (lambda hp, j: (nsb - 1 - j, col + hp))
    blk = lambda f: pl.BlockSpec((S, LANE), f)
    out_blk = blk(lambda hp, j: (nsb - 1 - j, hp))
    tab_blk = pl.BlockSpec((nd, 1, 2 * DSA_BLOCK, 2 * DSA_BLOCK), lambda hp, j: (0, hp, 0, 0))
    dq, dk, dv, dtab = pl.pallas_call(
        body,
        grid=(DSA_PAIRS, nsb),
        in_specs=[blk(cur(qcol)), blk(prev(kcol)), blk(cur(kcol)), blk(prev(vcol)), blk(cur(vcol)), tab_blk,
                  out_blk, out_blk, blk(cur(ocol))],
        out_specs=[out_blk, out_blk, out_blk, tab_blk],
        out_shape=[jax.ShapeDtypeStruct((T, DSA_WIDTH), BF16)] * 3
        + [jax.ShapeDtypeStruct((nd, DSA_PAIRS, 2 * DSA_BLOCK, 2 * DSA_BLOCK), F32)],
        scratch_shapes=[pltpu.VMEM((2 * S, LANE), F32), pltpu.VMEM((2 * S, LANE), F32), pltpu.VMEM((S, LANE), F32),
                        pltpu.VMEM((2 * S, LANE), F32), pltpu.VMEM((2 * S, LANE), F32)],
        compiler_params=_params(("arbitrary", "arbitrary")),
        name="dsa_bwd",
    )(proj, proj, proj, proj, proj, _pair_tiles(tab), ob_out, lse, dmixed)
    return dq, dk, dv, dtab.reshape(nd, DSA_HEADS, DSA_BLOCK, 2 * DSA_BLOCK)


FF_BLOCKS = 4
FF_BLOCK = D_FF // FF_BLOCKS


def post_fused(x, oa, ob, tgt, g2, gf, wout, wff1, wff2):
    T = x.shape[0]
    tm = 256
    inv_d = 1.0 / D_MODEL

    def body(x_ref, oa_ref, ob_ref, tgt_ref, g2_ref, gf_ref, wout_hbm, wff1_hbm, wff2_hbm,
             mixed_ref, nm_ref, a_ref, dpre_ref, dh2_ref, dh1_ref, dmixed_ref, loss_ref, dgf_ref, dg2_ref,
             wout_v, wff1_v, wff2_v, sems):
        @pl.when(pl.program_id(0) == 0)
        def _():
            cps = [pltpu.make_async_copy(s, d, sems.at[i])
                   for i, (s, d) in enumerate([(wout_hbm, wout_v), (wff1_hbm, wff1_v), (wff2_hbm, wff2_v)])]
            for cp in cps:
                cp.start()
            for cp in cps:
                cp.wait()
            loss_ref[...] = jnp.zeros_like(loss_ref)
            dgf_ref[...] = jnp.zeros_like(dgf_ref)
            dg2_ref[...] = jnp.zeros_like(dg2_ref)

        mixed = jnp.concatenate([oa_ref[...], ob_ref[...].astype(BF16)], axis=1)
        mixed_ref[...] = mixed
        h1 = x_ref[...] + _dot(mixed, wout_v[...])
        rs1 = _rstd(h1)
        hn1 = h1 * rs1
        g2 = g2_ref[...]
        nm = (hn1 * g2).astype(BF16)
        nm_ref[...] = nm
        relu = []
        mlp = jnp.zeros((tm, D_MODEL), F32)
        for j in range(FF_BLOCKS):
            cols = slice(j * FF_BLOCK, (j + 1) * FF_BLOCK)
            r_j = jnp.maximum(_dot(nm, wff1_v[j]), 0.0)
            a_j = (r_j * r_j).astype(BF16)
            a_ref[:, cols] = a_j
            relu.append(r_j)
            mlp = mlp + _dot(a_j, wff2_v[cols, :])
        h2 = h1 + mlp
        rsf = _rstd(h2)
        hnf = h2 * rsf
        gf = gf_ref[...]
        diff = hnf * gf - tgt_ref[...]
        loss_ref[...] += 0.5 * jnp.sum(jnp.sum(diff * diff, axis=-1, keepdims=True) * inv_d, axis=0, keepdims=True)
        dy = diff * inv_d
        dgf_ref[...] += jnp.sum(dy * hnf, axis=0, keepdims=True)
        dhnf = dy * gf
        dh2 = rsf * (dhnf - hnf * jnp.mean(dhnf * hnf, axis=-1, keepdims=True))
        dh2_b = dh2.astype(BF16)
        dh2_ref[...] = dh2_b
        dnm = jnp.zeros((tm, D_MODEL), F32)
        for j in range(FF_BLOCKS):
            cols = slice(j * FF_BLOCK, (j + 1) * FF_BLOCK)
            dpre_j = (_dot_nt(dh2_b, wff2_v[cols, :]) * (2.0 * relu[j])).astype(BF16)
            dpre_ref[:, cols] = dpre_j
            dnm = dnm + _dot_nt(dpre_j, wff1_v[j])
        dg2_ref[...] += jnp.sum(dnm * hn1, axis=0, keepdims=True)
        dhn1 = dnm * g2
        dh1 = dh2 + rs1 * (dhn1 - hn1 * jnp.mean(dhn1 * hn1, axis=-1, keepdims=True))
        dh1_ref[...] = dh1
        dmixed_ref[...] = _dot_nt(dh1.astype(BF16), wout_v[...])

    row = lambda w: pl.BlockSpec((tm, w), lambda i: (i, 0))
    vec = lambda w: pl.BlockSpec((1, w), lambda i: (0, 0))
    return pl.pallas_call(
        body,
        grid=(T // tm,),
        in_specs=[row(D_MODEL), row(GLA_WIDTH), row(DSA_WIDTH), row(D_MODEL), vec(D_MODEL), vec(D_MODEL), ANY, ANY, ANY],
        out_specs=[row(D_MODEL), row(D_MODEL), row(D_FF), row(D_FF), row(D_MODEL), row(D_MODEL), row(D_MODEL),
                   vec(1), vec(D_MODEL), vec(D_MODEL)],
        out_shape=[
            jax.ShapeDtypeStruct((T, D_MODEL), BF16),
            jax.ShapeDtypeStruct((T, D_MODEL), BF16),
            jax.ShapeDtypeStruct((T, D_FF), BF16),
            jax.ShapeDtypeStruct((T, D_FF), BF16),
            jax.ShapeDtypeStruct((T, D_MODEL), BF16),
            jax.ShapeDtypeStruct((T, D_MODEL), F32),
            jax.ShapeDtypeStruct((T, D_MODEL), F32),
            jax.ShapeDtypeStruct((1, 1), F32),
            jax.ShapeDtypeStruct((1, D_MODEL), F32),
            jax.ShapeDtypeStruct((1, D_MODEL), F32),
        ],
        scratch_shapes=[pltpu.VMEM((D_MODEL, D_MODEL), BF16), pltpu.VMEM((FF_BLOCKS, D_MODEL, FF_BLOCK), BF16),
                        pltpu.VMEM((D_FF, D_MODEL), BF16), pltpu.SemaphoreType.DMA((3,))],
        compiler_params=_params(),
        name="post_fused",
    )(x, oa, ob, tgt, g2, gf, wout, wff1, wff2)


WGRAD_TOKENS = 1024


def wgrad(a, b, name, bm=None, bn=None, col_blocked=False):
    T, M = a.shape
    N = b.shape[1]
    bm = M if bm is None else bm
    bn = N if bn is None else bn
    tk = min(WGRAD_TOKENS, T)

    def body(a_ref, b_ref, o_ref):
        part = _dot_tn(a_ref[...].astype(BF16), b_ref[...].astype(BF16))
        out = o_ref.at[0] if col_blocked else o_ref

        @pl.when(pl.program_id(2) == 0)
        def _():
            out[...] = part

        @pl.when(pl.program_id(2) > 0)
        def _():
            out[...] += part

    if col_blocked:
        assert bm == M
        out_spec = pl.BlockSpec((1, M, bn), lambda i, j, k: (j, 0, 0))
        out_shape = jax.ShapeDtypeStruct((N // bn, M, bn), F32)
    else:
        out_spec = pl.BlockSpec((bm, bn), lambda i, j, k: (i, j))
        out_shape = jax.ShapeDtypeStruct((M, N), F32)
    return pl.pallas_call(
        body,
        grid=(M // bm, N // bn, T // tk),
        in_specs=[pl.BlockSpec((tk, bm), lambda i, j, k: (k, i)), pl.BlockSpec((tk, bn), lambda i, j, k: (k, j))],
        out_specs=out_spec,
        out_shape=out_shape,
        compiler_params=_params(("arbitrary", "arbitrary", "arbitrary")),
        name=name,
    )(a, b)


def dx_final(x, dh1, g1, da, dq, dk, dv, wp):
    T = x.shape[0]
    tm = 256

    def body(x_ref, dh1_ref, g_ref, da_ref, dq_ref, dk_ref, dv_ref, w_hbm, dx_ref, dg_ref, w_vmem, sem):
        @pl.when(pl.program_id(0) == 0)
        def _():
            _load_once(w_hbm, w_vmem, sem)
            dg_ref[...] = jnp.zeros_like(dg_ref)

        dnx = (_dot_nt(da_ref[...], w_vmem[:, 0:P_A]) + _dot_nt(dq_ref[...], w_vmem[:, P_DQ:P_DQ + DSA_WIDTH])
               + _dot_nt(dk_ref[...], w_vmem[:, P_DK:P_DK + DSA_WIDTH]) + _dot_nt(dv_ref[...], w_vmem[:, P_DV:P_DV + DSA_WIDTH]))
        xf = x_ref[...]
        rs = _rstd(xf)
        hn = xf * rs
        dg_ref[...] += jnp.sum(dnx * hn, axis=0, keepdims=True)
        dhn = dnx * g_ref[...]
        dx_ref[...] = dh1_ref[...] + rs * (dhn - hn * jnp.mean(dhn * hn, axis=-1, keepdims=True))

    row = lambda w: pl.BlockSpec((tm, w), lambda i: (i, 0))
    vec = pl.BlockSpec((1, D_MODEL), lambda i: (0, 0))
    return pl.pallas_call(
        body,
        grid=(T // tm,),
        in_specs=[row(D_MODEL), row(D_MODEL), vec, row(P_A), row(DSA_WIDTH), row(DSA_WIDTH), row(DSA_WIDTH), ANY],
        out_specs=[row(D_MODEL), vec],
        out_shape=[jax.ShapeDtypeStruct((T, D_MODEL), F32), jax.ShapeDtypeStruct((1, D_MODEL), F32)],
        scratch_shapes=[pltpu.VMEM((D_MODEL, P_ALL), BF16), pltpu.SemaphoreType.DMA],
        compiler_params=_params(),
        name="dx_final",
    )(x, dh1, g1, da, dq, dk, dv, wp)


def adamw(w, g, m, v, name):
    R, C = w.shape
    br = 256 if R % 256 == 0 else R

    def body(w_ref, g_ref, m_ref, v_ref, d_ref, nm_ref, nv_ref):
        d_ref[...], nm_ref[...], nv_ref[...] = _adamw_math(w_ref[...], g_ref[...], m_ref[...], v_ref[...])

    spec = pl.BlockSpec((br, C), lambda i: (i, 0))
    return pl.pallas_call(
        body,
        grid=(R // br,),
        in_specs=[spec] * 4,
        out_specs=[spec] * 3,
        out_shape=[jax.ShapeDtypeStruct((R, C), F32)] * 3,
        compiler_params=_params(),
        name=name,
    )(w, g, m, v)


def _place():
    return lax.axis_index("x"), lax.axis_index("y"), lax.axis_index("c")


def _other_chips(x, y):
    return [(1 - x, y), (x, 1 - y), (1 - x, 1 - y)]


def gather_weights(shards):
    n = len(shards)

    def body(*refs):
        ins, outs = refs[:n], refs[n:2 * n]
        send_sems, recv_sems = refs[2 * n:]
        x, y, c = _place()
        me, sib = (x, y, c), (x, y, 1 - c)
        mine = 2 * x + y
        chips = _other_chips(x, y)

        def region(a, owner, half):
            h = ins[a].shape[0] // 2
            return outs[a].at[owner, pl.ds(half * h, h)]

        def copy(a, k, owner, half, to, src=None):
            return pltpu.make_async_remote_copy(
                src_ref=region(a, owner, half) if src is None else src, dst_ref=region(a, owner, half),
                send_sem=send_sems.at[a, k], recv_sem=recv_sems.at[a, k], device_id=to, device_id_type=MESH)

        first = []
        for a in range(n):
            h = ins[a].shape[0] // 2
            for t, (cx, cy) in enumerate(chips):
                first.append(copy(a, t, mine, c, (cx, cy, c), src=ins[a].at[pl.ds(c * h, h)]))
        for cp in first:
            cp.start()
        passed = []
        for a in range(n):
            for t, (cx, cy) in enumerate(chips):
                copy(a, t, 2 * cx + cy, c, me).wait_recv()
                fw = copy(a, 3 + t, 2 * cx + cy, c, sib)
                fw.start()
                passed.append(fw)
        for a in range(n):
            for t, (cx, cy) in enumerate(chips):
                copy(a, 3 + t, 2 * cx + cy, 1 - c, me).wait_recv()
        for cp in first + passed:
            cp.wait_send()

    outs = pl.pallas_call(
        body,
        in_specs=[ANY] * n,
        out_specs=[ANY] * n,
        out_shape=[jax.ShapeDtypeStruct((4,) + s.shape, s.dtype) for s in shards],
        scratch_shapes=[pltpu.SemaphoreType.DMA((n, 6)), pltpu.SemaphoreType.DMA((n, 6))],
        name="gather_weights",
    )(*shards)
    x, y, _ = _place()
    return [lax.dynamic_update_index_in_dim(o, s, 2 * x + y, 0) for o, s in zip(outs, shards)]


def swap_halves(grads):
    n = len(grads)

    def body(*refs):
        ins, outs = refs[:n], refs[n:2 * n]
        send_sems, recv_sems = refs[2 * n:]
        x, y, c = _place()
        cps = []
        for a in range(n):
            h = ins[a].shape[1] // 2
            cps.append(pltpu.make_async_remote_copy(
                src_ref=ins[a].at[:, pl.ds((1 - c) * h, h)], dst_ref=outs[a],
                send_sem=send_sems.at[a], recv_sem=recv_sems.at[a], device_id=(x, y, 1 - c), device_id_type=MESH))
        for cp in cps:
            cp.start()
        for cp in cps:
            cp.wait()

    return pl.pallas_call(
        body,
        in_specs=[ANY] * n,
        out_specs=[ANY] * n,
        out_shape=[jax.ShapeDtypeStruct((4, g.shape[1] // 2, g.shape[2]), g.dtype) for g in grads],
        scratch_shapes=[pltpu.SemaphoreType.DMA((n,)), pltpu.SemaphoreType.DMA((n,))],
        name="swap_halves",
    )(*grads)


def add_halves(g, got, core, name):
    _, R, C = g.shape
    h = R // 2
    br = 128
    nb = h // br

    def body(core_ref, g_ref, got_ref, o_ref):
        o_ref[...] = (g_ref[...] + got_ref[...]).astype(BF16)

    return pl.pallas_call(
        body,
        grid_spec=pltpu.PrefetchScalarGridSpec(
            num_scalar_prefetch=1,
            grid=(4, nb),
            in_specs=[pl.BlockSpec((1, br, C), lambda s, i, core: (s, core[0] * nb + i, 0)),
                      pl.BlockSpec((1, br, C), lambda s, i, core: (s, i, 0))],
            out_specs=pl.BlockSpec((1, br, C), lambda s, i, core: (s, i, 0)),
        ),
        out_shape=jax.ShapeDtypeStruct((4, h, C), BF16),
        compiler_params=_params(("arbitrary", "arbitrary")),
        name=name,
    )(core, g, got)


def scatter_chips(sums):
    n = len(sums)

    def body(*refs):
        ins, outs = refs[:n], refs[n:2 * n]
        send_sems, recv_sems = refs[2 * n:]
        x, y, c = _place()
        me, sib = (x, y, c), (x, y, 1 - c)
        mine = 2 * x + y
        chips = _other_chips(x, y)

        def region(a, sender, half):
            h = ins[a].shape[1]
            return outs[a].at[sender, pl.ds(half * h, h)]

        def copy(a, k, sender, half, to, src=None):
            return pltpu.make_async_remote_copy(
                src_ref=region(a, sender, half) if src is None else src, dst_ref=region(a, sender, half),
                send_sem=send_sems.at[a, k], recv_sem=recv_sems.at[a, k], device_id=to, device_id_type=MESH)

        first = []
        for a in range(n):
            for t, (cx, cy) in enumerate(chips):
                first.append(copy(a, t, mine, c, (cx, cy, c), src=ins[a].at[2 * cx + cy]))
            first.append(copy(a, 6, mine, c, sib, src=ins[a].at[mine]))
        for cp in first:
            cp.start()
        passed = []
        for a in range(n):
            for t, (cx, cy) in enumerate(chips):
                copy(a, t, 2 * cx + cy, c, me).wait_recv()
                fw = copy(a, 3 + t, 2 * cx + cy, c, sib)
                fw.start()
                passed.append(fw)
        for a in range(n):
            for t, (cx, cy) in enumerate(chips):
                copy(a, 3 + t, 2 * cx + cy, 1 - c, me).wait_recv()
            copy(a, 6, mine, 1 - c, me).wait_recv()
        for cp in first + passed:
            cp.wait_send()

    outs = pl.pallas_call(
        body,
        in_specs=[ANY] * n,
        out_specs=[ANY] * n,
        out_shape=[jax.ShapeDtypeStruct((4, 2 * s.shape[1], s.shape[2]), s.dtype) for s in sums],
        scratch_shapes=[pltpu.SemaphoreType.DMA((n, 7)), pltpu.SemaphoreType.DMA((n, 7))],
        name="scatter_chips",
    )(*sums)
    x, y, c = _place()
    filled = []
    for o, s in zip(outs, sums):
        h = s.shape[1]
        own = lax.dynamic_index_in_dim(s, 2 * x + y, 0, keepdims=True)
        filled.append(lax.dynamic_update_slice(o, own, (2 * x + y, c * h, 0)))
    return filled


def sum_slots(parts, name):
    S, R, C = parts.shape
    br = 128 if R % 128 == 0 else R

    def body(p_ref, o_ref):
        acc = p_ref[0].astype(F32)
        for s in range(1, S):
            acc = acc + p_ref[s].astype(F32)
        o_ref[...] = acc

    return pl.pallas_call(
        body,
        grid=(R // br,),
        in_specs=[pl.BlockSpec((S, br, C), lambda i: (0, i, 0))],
        out_specs=pl.BlockSpec((br, C), lambda i: (i, 0)),
        out_shape=jax.ShapeDtypeStruct((R, C), F32),
        compiler_params=_params(),
        name=name,
    )(parts)


def _adamw_math(w, g, m, v):
    m_new = ADAM_B1 * m + (1.0 - ADAM_B1) * g
    v_new = ADAM_B2 * v + (1.0 - ADAM_B2) * (g * g)
    m_hat = m_new / (1.0 - ADAM_B1 ** ADAM_STEP)
    v_hat = v_new / (1.0 - ADAM_B2 ** ADAM_STEP)
    return -ADAM_LR * (m_hat / (jnp.sqrt(v_hat) + ADAM_EPS) + ADAM_WD * w), m_new, v_new


def reduce_adamw(slots, w, m, v, name):
    S, R, C = slots.shape
    br = 128

    def body(p_ref, w_ref, m_ref, v_ref, g_ref, d_ref, nm_ref, nv_ref):
        g = p_ref[0].astype(F32)
        for s in range(1, S):
            g = g + p_ref[s].astype(F32)
        g_ref[...] = g
        d_ref[...], nm_ref[...], nv_ref[...] = _adamw_math(w_ref[...], g, m_ref[...], v_ref[...])

    spec = pl.BlockSpec((br, C), lambda i: (i, 0))
    return pl.pallas_call(
        body,
        grid=(R // br,),
        in_specs=[pl.BlockSpec((S, br, C), lambda i: (0, i, 0)), spec, spec, spec],
        out_specs=[spec] * 4,
        out_shape=[jax.ShapeDtypeStruct((R, C), F32)] * 4,
        compiler_params=_params(),
        name=name,
    )(slots, w, m, v)


SMALL_ROWS = 64


def gather_small(vec):
    def body(v_ref, o_ref, send_sems, recv_sems, local_sem):
        x, y, c = _place()
        flips = [(fx, fy, fc) for fx in (0, 1) for fy in (0, 1) for fc in (0, 1)][1:]

        def peer(f):
            return (1 - x if f[0] else x, 1 - y if f[1] else y, 1 - c if f[2] else c)

        slot = lambda p: 4 * p[0] + 2 * p[1] + p[2]
        own = pltpu.make_async_copy(v_ref, o_ref.at[slot((x, y, c))], local_sem)
        own.start()
        cps = [pltpu.make_async_remote_copy(
            src_ref=v_ref, dst_ref=o_ref.at[slot((x, y, c))], send_sem=send_sems.at[k], recv_sem=recv_sems.at[k],
            device_id=peer(f), device_id_type=MESH) for k, f in enumerate(flips)]
        for cp in cps:
            cp.start()
        for k, f in enumerate(flips):
            pltpu.make_async_remote_copy(
                src_ref=v_ref, dst_ref=o_ref.at[slot(peer(f))], send_sem=send_sems.at[k], recv_sem=recv_sems.at[k],
                device_id=(x, y, c), device_id_type=MESH).wait_recv()
        for cp in cps:
            cp.wait_send()
        own.wait()

    return pl.pallas_call(
        body,
        in_specs=[ANY],
        out_specs=ANY,
        out_shape=jax.ShapeDtypeStruct((8,) + vec.shape, vec.dtype),
        scratch_shapes=[pltpu.SemaphoreType.DMA((7,)), pltpu.SemaphoreType.DMA((7,)), pltpu.SemaphoreType.DMA],
        name="gather_small",
    )(vec)


GLOW_PAD = LANE - GLA_RANK


def kernel(x, attn_norm_g, w_in, gla_gate_w2, gla_gate_b, gla_norm_g, rel_bias, w_out, mlp_norm_g, w_ff1, w_ff2, final_norm_g, loss_target, m_attn_norm_g, m_w_in, m_gla_gate_w2, m_gla_gate_b, m_gla_norm_g, m_rel_bias, m_w_out, m_mlp_norm_g, m_w_ff1, m_w_ff2, m_final_norm_g, v_attn_norm_g, v_w_in, v_gla_gate_w2, v_gla_gate_b, v_gla_norm_g, v_rel_bias, v_w_out, v_mlp_norm_g, v_w_ff1, v_w_ff2, v_final_norm_g):
    xs, tgt = x[0], loss_target[0]
    T = xs.shape[0]
    cx, cy, cc = _place()
    chip = 2 * cx + cy
    gf = final_norm_g.reshape(1, D_MODEL)

    win_g, wout_g, wff1, wff2_g, w2_g = gather_weights(
        [w_in[0].astype(BF16), w_out[0].astype(BF16), w_ff1[0].astype(BF16), w_ff2[0].astype(BF16), gla_gate_w2[0]])
    win = jnp.transpose(win_g, (1, 0, 2)).reshape(D_MODEL, D_IN)
    n_glow = R_GLOW + GLA_RANK
    wp = jnp.concatenate([win[:, :n_glow], jnp.zeros((D_MODEL, GLOW_PAD), BF16), win[:, n_glow:]], axis=1)
    wout = wout_g.reshape(D_MODEL, D_MODEL)
    wff2 = wff2_g.reshape(D_FF, D_MODEL)
    w2 = jnp.transpose(w2_g, (1, 0, 2)).reshape(GLA_RANK, GLA_QK)
    w2p = jnp.concatenate([w2, jnp.zeros((GLOW_PAD, GLA_QK), F32)], axis=0)

    proj, nx = inproj(xs, attn_norm_g, wp)
    oa, opre, sprev = gla_fwd(proj, w2p, gla_gate_b, gla_norm_g)
    tab = bias_tables(rel_bias)
    ob, lse = dsa_fwd(proj, tab)
    mixed, nm, act, dpre, dh2, dh1, dmixed, loss, dgf, dg2 = post_fused(xs, oa, ob, tgt, mlp_norm_g, gf, wout, wff1, wff2)
    da, dw2p, dbg, dgn = gla_bwd(proj, w2p, gla_gate_b, gla_norm_g, opre, sprev, dmixed)
    dq, dk, dv, dtab = dsa_bwd(proj, tab, ob, lse, dmixed)
    drel = bias_tables_bwd(dtab)
    dxs, dg1 = dx_final(xs, dh1, attn_norm_g, da, dq, dk, dv, wp)

    dwff2 = wgrad(act, dh2, "wgrad_ff2", bm=FF_BLOCK)
    dwff1 = wgrad(nm, dpre, "wgrad_ff1", bn=FF_BLOCK, col_blocked=True)
    dwout = wgrad(mixed, dh1, "wgrad_out")
    dwa = wgrad(nx, da, "wgrad_in_gla")
    dwq = wgrad(nx, dq, "wgrad_in_q")
    dwk = wgrad(nx, dk, "wgrad_in_k")
    dwv = wgrad(nx, dv, "wgrad_in_v")
    dwin = jnp.concatenate([dwa[:, :n_glow], dwq, dwk, dwv], axis=1)
    big = [
        jnp.transpose(dwin.reshape(D_MODEL, 4, D_IN // 4), (1, 0, 2)),
        dwout.reshape(4, D_MODEL // 4, D_MODEL),
        dwff1,
        dwff2.reshape(4, FF_BLOCK, D_MODEL),
    ]

    names = ["in", "out", "ff1", "ff2"]
    got = swap_halves(big)
    core = cc.astype(jnp.int32).reshape(1)
    sums = [add_halves(g, r, core, "add_halves_" + s) for g, r, s in zip(big, got, names)]
    slots = dict(zip(["w_in", "w_out", "w_ff1", "w_ff2"], scatter_chips(sums)))

    small = jnp.concatenate([dg1.reshape(-1), dbg.reshape(-1), dgn.reshape(-1), drel.reshape(-1), dg2.reshape(-1),
                             dgf.reshape(-1), dw2p[:GLA_RANK].reshape(-1)]).reshape(SMALL_ROWS, LANE)
    tot = sum_slots(gather_small(small), "sum_small").reshape(-1)
    sizes = [D_MODEL, GLA_QK, GLA_WIDTH, REL_BUCKETS * DSA_HEADS, D_MODEL, D_MODEL, GLA_RANK * GLA_QK]
    offs = np.concatenate([[0], np.cumsum(sizes)])
    piece = lambda i: tot[int(offs[i]):int(offs[i + 1])]
    g_g1 = piece(0).reshape(1, D_MODEL)
    g_bg = piece(1).reshape(1, GLA_QK)
    g_gn = piece(2).reshape(1, GLA_WIDTH)
    g_rel = piece(3).reshape(REL_BUCKETS, DSA_HEADS)
    g_g2 = piece(4).reshape(1, D_MODEL)
    g_gf = piece(5).reshape(1, D_MODEL)
    g_w2 = lax.dynamic_slice_in_dim(piece(6).reshape(GLA_RANK, GLA_QK), chip * (GLA_QK // 4), GLA_QK // 4, axis=1)

    loss_all = lax.psum(loss[0, 0], ("x", "y", "c"))

    upd = [
        ("attn_norm_g", attn_norm_g, g_g1, m_attn_norm_g, v_attn_norm_g),
        ("w_in", w_in[0], None, m_w_in[0], v_w_in[0]),
        ("gla_gate_w2", gla_gate_w2[0], g_w2, m_gla_gate_w2[0], v_gla_gate_w2[0]),
        ("gla_gate_b", gla_gate_b, g_bg, m_gla_gate_b, v_gla_gate_b),
        ("gla_norm_g", gla_norm_g, g_gn, m_gla_norm_g, v_gla_norm_g),
        ("rel_bias", rel_bias, g_rel, m_rel_bias, v_rel_bias),
        ("w_out", w_out[0], None, m_w_out[0], v_w_out[0]),
        ("mlp_norm_g", mlp_norm_g, g_g2, m_mlp_norm_g, v_mlp_norm_g),
        ("w_ff1", w_ff1[0], None, m_w_ff1[0], v_w_ff1[0]),
        ("w_ff2", w_ff2[0], None, m_w_ff2[0], v_w_ff2[0]),
        ("final_norm_g", gf, g_gf, m_final_norm_g.reshape(1, D_MODEL), v_final_norm_g.reshape(1, D_MODEL)),
    ]
    shapes = [attn_norm_g.shape, w_in.shape, gla_gate_w2.shape, gla_gate_b.shape, gla_norm_g.shape, rel_bias.shape,
              w_out.shape, mlp_norm_g.shape, w_ff1.shape, w_ff2.shape, final_norm_g.shape]
    grads, deltas, new_m, new_v = [], [], [], []
    for (name, w, g, m, v), shape in zip(upd, shapes):
        if name in slots:
            g, d, nm_, nv_ = reduce_adamw(slots[name], w, m, v, "reduce_adamw_" + name)
        else:
            d, nm_, nv_ = adamw(w, g, m, v, "adamw_" + name)
        grads.append(g.reshape(shape))
        deltas.append(d.reshape(shape))
        new_m.append(nm_.reshape(shape))
        new_v.append(nv_.reshape(shape))
    return (loss_all, dxs.reshape(1, T, D_MODEL), *grads, *deltas, *new_m, *new_v)
```

```python
import functools
import math

import jax
import jax.numpy as jnp
import numpy as np
from jax import lax
from jax.experimental import pallas as pl
from jax.experimental.pallas import tpu as pltpu

F32 = jnp.float32
BF16 = jnp.bfloat16
MESH = pl.DeviceIdType.MESH

D_MODEL = 1024
GLA_WIDTH = 512
GLA_HEADS = 4
GLA_DK = 64
GLA_DV = 128
GLA_QK = GLA_HEADS * GLA_DK
GLA_RANK = 16
GLA_TAU = 16.0
GLA_CHUNK = 64
DSA_WIDTH = 512
DSA_HEADS = 8
DSA_DH = 64
DSA_DILATIONS = (1, 4, 16)
DSA_SPAN = 128
DSA_BLOCK = 128
DSA_SUPER = DSA_BLOCK * DSA_DILATIONS[-1]
REL_BUCKETS = 32
REL_MAX_DIST = 2048
D_FF = 4096
D_IN = 3088
EPS = 1e-6
NEG = -1e30
QK_SCALE = 0.125

ADAM_LR = 0.001
ADAM_B1 = 0.9
ADAM_B2 = 0.999
ADAM_EPS = 1e-08
ADAM_WD = 0.01
ADAM_STEP = 10

LANE = 128
P_GQ, P_GK, P_GV, P_GR = 0, 256, 512, 1024
P_GLOW = 1536
P_A = 1664
P_DQ, P_DK, P_DV = 1664, 2176, 2688
P_ALL = 3200
R_GQ, R_GK, R_GV, R_GR, R_GLOW, R_DQ, R_DK, R_DV = 0, 256, 512, 1024, 1536, 1552, 2064, 2576

VMEM_LIMIT = 56 * 1024 * 1024


def _params(sem=("arbitrary",), vmem=VMEM_LIMIT):
    return pltpu.CompilerParams(dimension_semantics=sem, vmem_limit_bytes=vmem)


def _dot(a, b):
    return jnp.dot(a, b, preferred_element_type=F32)


def _dot_nt(a, b):
    return lax.dot_general(a, b, (((1,), (1,)), ((), ())), preferred_element_type=F32)


def _dot_tn(a, b):
    return lax.dot_general(a, b, (((0,), (0,)), ((), ())), preferred_element_type=F32)


def _split3(x):
    x1 = x.astype(BF16)
    r1 = x - x1.astype(F32)
    x2 = r1.astype(BF16)
    x3 = (r1 - x2.astype(F32)).astype(BF16)
    return x1, x2, x3


def _dot_exact_lhs(m_bf16, x):
    x1, x2, x3 = _split3(x)
    return _dot(m_bf16, x1) + _dot(m_bf16, x2) + _dot(m_bf16, x3)


def _rstd(xf):
    return lax.rsqrt(jnp.mean(xf * xf, axis=-1, keepdims=True) + EPS)


def _load_once(hbm_ref, vmem_ref, sem):
    cp = pltpu.make_async_copy(hbm_ref, vmem_ref, sem)
    cp.start()
    cp.wait()


ANY = pl.BlockSpec(memory_space=pl.ANY)


def inproj(x, g1, wp):
    T = x.shape[0]
    tm = 256

    def body(x_ref, g_ref, w_hbm, proj_ref, nx_ref, w_vmem, sem):
        @pl.when(pl.program_id(0) == 0)
        def _():
            _load_once(w_hbm, w_vmem, sem)

        xf = x_ref[...]
        nx = ((xf * _rstd(xf)) * g_ref[...]).astype(BF16)
        nx_ref[...] = nx
        proj_ref[...] = _dot(nx, w_vmem[...])

    return pl.pallas_call(
        body,
        grid=(T // tm,),
        in_specs=[pl.BlockSpec((tm, D_MODEL), lambda i: (i, 0)), pl.BlockSpec((1, D_MODEL), lambda i: (0, 0)), ANY],
        out_specs=[pl.BlockSpec((tm, P_ALL), lambda i: (i, 0)), pl.BlockSpec((tm, D_MODEL), lambda i: (i, 0))],
        out_shape=[jax.ShapeDtypeStruct((T, P_ALL), F32), jax.ShapeDtypeStruct((T, D_MODEL), BF16)],
        scratch_shapes=[pltpu.VMEM((D_MODEL, P_ALL), BF16), pltpu.SemaphoreType.DMA],
        compiler_params=_params(),
        name="inproj",
    )(x, g1, wp)


GLA_CHUNKS_PER_STEP = 8
GLA_ROWS = GLA_CHUNK * GLA_CHUNKS_PER_STEP
GLA_UNROLL = 4


def _gla_masks():
    lane = lax.broadcasted_iota(jnp.int32, (1, GLA_QK), 1)
    return [(lane >= h * GLA_DK) & (lane < (h + 1) * GLA_DK) for h in range(GLA_HEADS)]


def _log_sigmoid(x):
    return jnp.minimum(x, 0.0) - jnp.log(1.0 + jnp.exp(-jnp.abs(x)))


def _sigmoid(x):
    return 1.0 / (1.0 + jnp.exp(-x))


def _phased_loop(n, unroll, load, compute, store, init):
    def body(i, carry):
        steps = [i * unroll + u for u in range(unroll)]
        loaded = [load(s) for s in steps]
        results = []
        for vals in loaded:
            carry, res = compute(vals, carry)
            results.append(res)
        for s, res in zip(steps, results):
            store(s, res)
        return carry

    return lax.fori_loop(0, n // unroll, body, init)


def _gla_chunk_rows(c):
    return pl.ds(pl.multiple_of(c * GLA_CHUNK, GLA_CHUNK), GLA_CHUNK)


def _gla_chunk_common(q, k, glow, w2, bg, tri):
    gpre = _dot(glow, w2) + bg
    glog = _log_sigmoid(gpre) / GLA_TAU
    b = _dot_exact_lhs(tri, glog)
    bl = b[GLA_CHUNK - 1:GLA_CHUNK, :]
    eb = jnp.exp(b)
    enb = jnp.exp(-b)
    eke = jnp.exp(bl - b)
    qd = (q * QK_SCALE) * eb
    ki = k * enb
    ke = k * eke
    return gpre, eb, enb, eke, jnp.exp(bl), qd, ki, ke


def _head_cols(h):
    return slice(h * GLA_DV, (h + 1) * GLA_DV)


def gla_fwd(proj, w2p, bg, gn):
    T = proj.shape[0]
    n_steps = T // GLA_ROWS
    n_chunks = T // GLA_CHUNK

    def body(proj_ref, w2_ref, bg_ref, gn_ref, oa_ref, opre_ref, sprev_ref, st_ref):
        @pl.when(pl.program_id(0) == 0)
        def _():
            st_ref[...] = jnp.zeros_like(st_ref)

        masks = _gla_masks()
        ri = lax.broadcasted_iota(jnp.int32, (GLA_CHUNK, GLA_CHUNK), 0)
        ci = lax.broadcasted_iota(jnp.int32, (GLA_CHUNK, GLA_CHUNK), 1)
        causal = ri >= ci
        tri = causal.astype(BF16)
        w2 = w2_ref[...].astype(BF16)
        bg = bg_ref[...]

        gn = gn_ref[...]

        def load(c):
            rows = _gla_chunk_rows(c)
            return (proj_ref[rows, P_GQ:P_GQ + GLA_QK], proj_ref[rows, P_GK:P_GK + GLA_QK],
                    proj_ref[rows, P_GV:P_GV + GLA_WIDTH], proj_ref[rows, P_GR:P_GR + GLA_WIDTH],
                    proj_ref[rows, P_GLOW:P_GLOW + LANE])

        def compute(vals, st):
            q, k, v, r, glow = vals
            _, _, _, _, ebl, qd, ki, ke = _gla_chunk_common(q, k, glow.astype(BF16), w2, bg, tri)
            st_b = st.astype(BF16)
            ki_b = ki.astype(BF16)
            inc = jnp.zeros_like(st)
            o_heads = []
            for h in range(GLA_HEADS):
                v_h = v[:, _head_cols(h)].astype(BF16)
                qd_h = jnp.where(masks[h], qd, 0.0).astype(BF16)
                ke_h = jnp.where(masks[h], ke, 0.0).astype(BF16)
                att = jnp.where(causal, _dot_nt(qd_h, ki_b), 0.0)
                o_heads.append(_dot(att.astype(BF16), v_h) + _dot_nt(qd_h, st_b))
                inc = inc + _dot_tn(v_h, ke_h)
            o = jnp.concatenate(o_heads, axis=1)
            on = jnp.concatenate([o_h * _rstd(o_h) for o_h in o_heads], axis=1)
            oa = ((on * gn) * (r * _sigmoid(r))).astype(BF16)
            return st * ebl + inc, (st, o, oa)

        def store(c, res):
            rows = _gla_chunk_rows(c)
            sprev_ref[c], opre_ref[rows, :], oa_ref[rows, :] = res

        st_ref[...] = _phased_loop(GLA_CHUNKS_PER_STEP, GLA_UNROLL, load, compute, store, st_ref[...])

    return pl.pallas_call(
        body,
        grid=(n_steps,),
        in_specs=[
            pl.BlockSpec((GLA_ROWS, P_ALL), lambda i: (i, 0)),
            pl.BlockSpec((LANE, GLA_QK), lambda i: (0, 0)),
            pl.BlockSpec((1, GLA_QK), lambda i: (0, 0)),
            pl.BlockSpec((1, GLA_WIDTH), lambda i: (0, 0)),
        ],
        out_specs=[
            pl.BlockSpec((GLA_ROWS, GLA_WIDTH), lambda i: (i, 0)),
            pl.BlockSpec((GLA_ROWS, GLA_WIDTH), lambda i: (i, 0)),
            pl.BlockSpec((GLA_CHUNKS_PER_STEP, GLA_DV, GLA_QK), lambda i: (i, 0, 0)),
        ],
        out_shape=[
            jax.ShapeDtypeStruct((T, GLA_WIDTH), BF16),
            jax.ShapeDtypeStruct((T, GLA_WIDTH), F32),
            jax.ShapeDtypeStruct((n_chunks, GLA_DV, GLA_QK), F32),
        ],
        scratch_shapes=[pltpu.VMEM((GLA_DV, GLA_QK), F32)],
        compiler_params=_params(),
        name="gla_fwd",
    )(proj, w2p, bg, gn)


def gla_bwd(proj, w2p, bg, gn, opre, sprev, dmixed, exchange=None):
    T = proj.shape[0]
    n_steps = T // GLA_ROWS

    def body(*refs):
        refs = _host_exchange(exchange, refs, 7, 4, pl.program_id(0), n_steps)
        proj_ref, w2_ref, bg_ref, gn_ref, opre_ref, sprev_ref, doa_ref, da_ref, dw2_ref, dbg_ref, dgn_ref, dst_ref = refs

        @pl.when(pl.program_id(0) == 0)
        def _():
            dst_ref[...] = jnp.zeros_like(dst_ref)
            dw2_ref[...] = jnp.zeros_like(dw2_ref)
            dbg_ref[...] = jnp.zeros_like(dbg_ref)
            dgn_ref[...] = jnp.zeros_like(dgn_ref)

        masks = _gla_masks()
        ri = lax.broadcasted_iota(jnp.int32, (GLA_CHUNK, GLA_CHUNK), 0)
        ci = lax.broadcasted_iota(jnp.int32, (GLA_CHUNK, GLA_CHUNK), 1)
        causal = ri >= ci
        tri = causal.astype(BF16)
        tri_t = (ri <= ci).astype(BF16)
        last_row = lax.broadcasted_iota(jnp.int32, (GLA_CHUNK, GLA_QK), 0) == GLA_CHUNK - 1
        w2 = w2_ref[...].astype(BF16)
        bg = bg_ref[...]

        gn = gn_ref[...]

        def load(j):
            c = GLA_CHUNKS_PER_STEP - 1 - j
            rows = _gla_chunk_rows(c)
            return (proj_ref[rows, P_GQ:P_GQ + GLA_QK], proj_ref[rows, P_GK:P_GK + GLA_QK],
                    proj_ref[rows, P_GV:P_GV + GLA_WIDTH], proj_ref[rows, P_GR:P_GR + GLA_WIDTH],
                    proj_ref[rows, P_GLOW:P_GLOW + LANE], opre_ref[rows, :], doa_ref[rows, :], sprev_ref[c])

        def compute(vals, carry):
            dst, dw2, dbg, dgn = carry
            q, k, v, r, glow, o, doa, st = vals
            glow = glow.astype(BF16)
            gpre, eb, enb, eke, ebl, qd, ki, ke = _gla_chunk_common(q, k, glow, w2, bg, tri)
            sig = _sigmoid(r)
            rs = jnp.concatenate([jnp.broadcast_to(_rstd(o[:, _head_cols(h)]), (GLA_CHUNK, GLA_DV))
                                  for h in range(GLA_HEADS)], axis=1)
            on = o * rs
            d_ong = doa * (r * sig)
            dr = doa * (on * gn) * (sig * (1.0 + r * (1.0 - sig)))
            dgn = dgn + jnp.sum(d_ong * on, axis=0, keepdims=True)
            d_on = d_ong * gn
            t = d_on * on
            mean_t = jnp.concatenate([jnp.broadcast_to(jnp.mean(t[:, _head_cols(h)], axis=-1, keepdims=True), (GLA_CHUNK, GLA_DV))
                                      for h in range(GLA_HEADS)], axis=1)
            do = rs * (d_on - on * mean_t)
            st_b = st.astype(BF16)
            dst_b = dst.astype(BF16)
            ki_b = ki.astype(BF16)
            dqd = jnp.zeros_like(qd)
            dki = jnp.zeros_like(qd)
            dke = jnp.zeros_like(qd)
            dst_add = jnp.zeros_like(st)
            dv_heads = []
            for h in range(GLA_HEADS):
                do_b = do[:, _head_cols(h)].astype(BF16)
                v_h = v[:, _head_cols(h)].astype(BF16)
                qd_h = jnp.where(masks[h], qd, 0.0).astype(BF16)
                ke_h = jnp.where(masks[h], ke, 0.0).astype(BF16)
                att = jnp.where(causal, _dot_nt(qd_h, ki_b), 0.0).astype(BF16)
                d_att = jnp.where(causal, _dot_nt(do_b, v_h), 0.0).astype(BF16)
                dv_heads.append(_dot_tn(att, do_b) + _dot_nt(ke_h, dst_b))
                dqd = dqd + jnp.where(masks[h], _dot(d_att, ki_b) + _dot(do_b, st_b), 0.0)
                dki = dki + _dot_tn(d_att, qd_h)
                dke = dke + jnp.where(masks[h], _dot(v_h, dst_b), 0.0)
                dst_add = dst_add + _dot_tn(do_b, qd_h)
            debl = jnp.sum(dst * st, axis=0, keepdims=True)
            dkk = dke * ke
            db = dqd * qd - dki * ki - dkk
            dbl = jnp.sum(dkk, axis=0, keepdims=True) + debl * ebl
            db = db + jnp.where(last_row, dbl, 0.0)
            dglog = _dot_exact_lhs(tri_t, db)
            dgpre = (dglog / GLA_TAU) * _sigmoid(-gpre)
            dgpre_b = dgpre.astype(BF16)
            da = jnp.concatenate([dqd * eb * QK_SCALE, dki * enb + dke * eke] + dv_heads + [dr, _dot_nt(dgpre_b, w2)],
                                 axis=1).astype(BF16)
            carry = (dst * ebl + dst_add, dw2 + _dot_tn(glow, dgpre_b), dbg + jnp.sum(dgpre, axis=0, keepdims=True), dgn)
            return carry, da

        def store(j, da):
            da_ref[_gla_chunk_rows(GLA_CHUNKS_PER_STEP - 1 - j), :] = da

        carry = (dst_ref[...], jnp.zeros((LANE, GLA_QK), F32), jnp.zeros((1, GLA_QK), F32), jnp.zeros((1, GLA_WIDTH), F32))
        dst, dw2, dbg, dgn = _phased_loop(GLA_CHUNKS_PER_STEP, GLA_UNROLL, load, compute, store, carry)
        dst_ref[...] = dst
        dw2_ref[...] += dw2
        dbg_ref[...] += dbg
        dgn_ref[...] += dgn

    rev = lambda i: (n_steps - 1 - i, 0)
    return _hosted_call(
        exchange,
        body,
        grid=(n_steps,),
        in_specs=[
            pl.BlockSpec((GLA_ROWS, P_ALL), rev),
            pl.BlockSpec((LANE, GLA_QK), lambda i: (0, 0)),
            pl.BlockSpec((1, GLA_QK), lambda i: (0, 0)),
            pl.BlockSpec((1, GLA_WIDTH), lambda i: (0, 0)),
            pl.BlockSpec((GLA_ROWS, GLA_WIDTH), rev),
            pl.BlockSpec((GLA_CHUNKS_PER_STEP, GLA_DV, GLA_QK), lambda i: (n_steps - 1 - i, 0, 0)),
            pl.BlockSpec((GLA_ROWS, GLA_WIDTH), rev),
        ],
        out_specs=[
            pl.BlockSpec((GLA_ROWS, P_A), rev),
            pl.BlockSpec((LANE, GLA_QK), lambda i: (0, 0)),
            pl.BlockSpec((1, GLA_QK), lambda i: (0, 0)),
            pl.BlockSpec((1, GLA_WIDTH), lambda i: (0, 0)),
        ],
        out_shape=[
            jax.ShapeDtypeStruct((T, P_A), BF16),
            jax.ShapeDtypeStruct((LANE, GLA_QK), F32),
            jax.ShapeDtypeStruct((1, GLA_QK), F32),
            jax.ShapeDtypeStruct((1, GLA_WIDTH), F32),
        ],
        scratch_shapes=[pltpu.VMEM((GLA_DV, GLA_QK), F32)],
        compiler_params=_params(),
        name="gla_bwd",
        args=(proj, w2p, bg, gn, opre, sprev, dmixed),
    )


def _t5_bucket(dist):
    max_exact = REL_BUCKETS // 2
    n = np.maximum(dist, 0)
    large = max_exact + (np.log(np.maximum(n, 1) / max_exact) / math.log(REL_MAX_DIST / max_exact)
                         * (REL_BUCKETS - max_exact)).astype(np.int32)
    large = np.minimum(large, REL_BUCKETS - 1)
    return np.where(n < max_exact, n, large).astype(np.int32)


def _bucket_ids():
    L = DSA_BLOCK
    steps = L + np.arange(L)[:, None] - np.arange(2 * L)[None, :]
    in_band = (steps >= 0) & (steps <= DSA_SPAN)
    return np.stack([np.where(in_band, _t5_bucket(steps * d), -1) for d in DSA_DILATIONS]).astype(np.int32)


def bias_tables(rel_bias):
    ids = jnp.asarray(_bucket_ids())
    nd = len(DSA_DILATIONS)

    def body(rel_ref, ids_ref, tab_ref):
        h = pl.program_id(1)
        idt = ids_ref[0]
        acc = jnp.where(idt < 0, NEG, 0.0).astype(F32)
        for b in range(REL_BUCKETS):
            acc = jnp.where(idt == b, rel_ref[b, h], acc)
        tab_ref[0, 0] = acc

    return pl.pallas_call(
        body,
        grid=(nd, DSA_HEADS),
        in_specs=[pl.BlockSpec(memory_space=pltpu.SMEM), pl.BlockSpec((1, DSA_BLOCK, 2 * DSA_BLOCK), lambda d, h: (d, 0, 0))],
        out_specs=pl.BlockSpec((1, 1, DSA_BLOCK, 2 * DSA_BLOCK), lambda d, h: (d, h, 0, 0)),
        out_shape=jax.ShapeDtypeStruct((nd, DSA_HEADS, DSA_BLOCK, 2 * DSA_BLOCK), F32),
        compiler_params=_params(("arbitrary", "arbitrary")),
        name="bias_tables",
    )(rel_bias, ids)


def bias_tables_bwd(dtab):
    ids = jnp.asarray(_bucket_ids())
    nd = len(DSA_DILATIONS)

    def body(dtab_ref, ids_ref, drel_ref):
        @pl.when((pl.program_id(0) == 0) & (pl.program_id(1) == 0))
        def _():
            for b in range(REL_BUCKETS):
                for h in range(DSA_HEADS):
                    drel_ref[b, h] = 0.0

        h = pl.program_id(1)
        idt = ids_ref[0]
        g = dtab_ref[0, 0]
        for b in range(REL_BUCKETS):
            drel_ref[b, h] += jnp.sum(jnp.where(idt == b, g, 0.0))

    return pl.pallas_call(
        body,
        grid=(nd, DSA_HEADS),
        in_specs=[pl.BlockSpec((1, 1, DSA_BLOCK, 2 * DSA_BLOCK), lambda d, h: (d, h, 0, 0)),
                  pl.BlockSpec((1, DSA_BLOCK, 2 * DSA_BLOCK), lambda d, h: (d, 0, 0))],
        out_specs=pl.BlockSpec(memory_space=pltpu.SMEM),
        out_shape=jax.ShapeDtypeStruct((REL_BUCKETS, DSA_HEADS), F32),
        compiler_params=_params(("arbitrary", "arbitrary")),
        name="bias_tables_bwd",
    )(dtab, ids)


DSA_PAIRS = DSA_HEADS // 2
DSA_UNROLL = 8
DSA_COMBINE_ROWS = 256


def _dsa_units(d):
    return d, DSA_SUPER // (DSA_BLOCK * d)


def _dsa_specs(T):
    nsb = T // DSA_SUPER
    qcol, kcol, vcol = P_DQ // LANE, P_DK // LANE, P_DV // LANE
    return nsb, qcol, kcol, vcol


def _head_lane_mask():
    return lax.broadcasted_iota(jnp.int32, (1, LANE), 1) < DSA_DH


def _first_block_penalty(first):
    col = lax.broadcasted_iota(jnp.int32, (2 * DSA_BLOCK, 2 * DSA_BLOCK), 1)
    return jnp.where(first & (col < DSA_BLOCK), NEG, 0.0).astype(F32)


def _pair_tiles(tab):
    return tab.reshape(len(DSA_DILATIONS), DSA_PAIRS, 2 * DSA_BLOCK, 2 * DSA_BLOCK)


def _stack_heads(t, head0):
    return jnp.concatenate([jnp.where(head0, t, 0.0), jnp.where(head0, 0.0, t)], axis=0)


def dsa_fwd(proj, tab, exchange=None):
    T = proj.shape[0]
    nsb, qcol, kcol, vcol = _dsa_specs(T)
    S = DSA_SUPER

    def body(*refs):
        refs = _host_exchange(exchange, refs, 6, 2, pl.program_id(0) * nsb + pl.program_id(1), DSA_PAIRS * nsb)
        q_ref, kp_ref, kc_ref, vp_ref, vc_ref, tab_ref, out_ref, lse_ref, kk, vv, ob, lb = refs
        sb = pl.program_id(1)
        kk[0:S, :] = kp_ref[...]
        kk[S:2 * S, :] = kc_ref[...]
        vv[0:S, :] = vp_ref[...]
        vv[S:2 * S, :] = vc_ref[...]
        head0 = _head_lane_mask()

        for di, d in enumerate(DSA_DILATIONS):
            n_res, n_blk = _dsa_units(d)

            def unit(u, carry, di=di, d=d, n_blk=n_blk):
                r = u // n_blk
                c = u % n_blk
                q0 = r + d * DSA_BLOCK * c
                qrows = pl.ds(q0, DSA_BLOCK, stride=d) if d > 1 else pl.ds(q0, DSA_BLOCK)
                krows = pl.ds(S + q0 - d * DSA_BLOCK, 2 * DSA_BLOCK, stride=d) if d > 1 else pl.ds(S + q0 - DSA_BLOCK, 2 * DSA_BLOCK)
                q2 = q_ref[qrows, :] * QK_SCALE
                k2 = kk[krows, :].astype(BF16)
                v2 = vv[krows, :].astype(BF16)
                qs = _stack_heads(q2, head0).astype(BF16)
                s = _dot_nt(qs, k2) + (tab_ref[di, 0] + _first_block_penalty((sb == 0) & (c == 0)))
                m = jnp.max(s, axis=-1, keepdims=True)
                p = jnp.exp(s - m)
                den = jnp.sum(p, axis=-1, keepdims=True)
                o = _dot(p.astype(BF16), v2) / den
                l = jnp.broadcast_to(m + jnp.log(den), (2 * DSA_BLOCK, LANE))
                ob[di, qrows, :] = jnp.where(head0, o[:DSA_BLOCK], o[DSA_BLOCK:])
                lb[di, qrows, :] = jnp.where(head0, l[:DSA_BLOCK], l[DSA_BLOCK:])
                return carry

            lax.fori_loop(0, n_res * n_blk, unit, 0, unroll=DSA_UNROLL)

        def combine(i, carry):
            rows = pl.ds(pl.multiple_of(i * DSA_COMBINE_ROWS, DSA_COMBINE_ROWS), DSA_COMBINE_ROWS)
            l0, l1, l2 = lb[0, rows, :], lb[1, rows, :], lb[2, rows, :]
            mx = jnp.maximum(jnp.maximum(l0, l1), l2)
            e0, e1, e2 = jnp.exp(l0 - mx), jnp.exp(l1 - mx), jnp.exp(l2 - mx)
            den = e0 + e1 + e2
            out_ref[rows, :] = (e0 * ob[0, rows, :] + e1 * ob[1, rows, :] + e2 * ob[2, rows, :]) / den
            lse_ref[rows, :] = mx + jnp.log(den)
            return carry

        lax.fori_loop(0, S // DSA_COMBINE_ROWS, combine, 0)

    prev = lambda col: (lambda hp, sb: (jnp.maximum(sb - 1, 0), col + hp))
    cur = lambda col: (lambda hp, sb: (sb, col + hp))
    blk = lambda f: pl.BlockSpec((S, LANE), f)
    return _hosted_call(
        exchange,
        body,
        grid=(DSA_PAIRS, nsb),
        in_specs=[blk(cur(qcol)), blk(prev(kcol)), blk(cur(kcol)), blk(prev(vcol)), blk(cur(vcol)),
                  pl.BlockSpec((len(DSA_DILATIONS), 1, 2 * DSA_BLOCK, 2 * DSA_BLOCK), lambda hp, sb: (0, hp, 0, 0))],
        out_specs=[blk(lambda hp, sb: (sb, hp)), blk(lambda hp, sb: (sb, hp))],
        out_shape=[jax.ShapeDtypeStruct((T, DSA_WIDTH), F32), jax.ShapeDtypeStruct((T, DSA_WIDTH), F32)],
        scratch_shapes=[pltpu.VMEM((2 * S, LANE), F32), pltpu.VMEM((2 * S, LANE), F32),
                        pltpu.VMEM((len(DSA_DILATIONS), S, LANE), F32), pltpu.VMEM((len(DSA_DILATIONS), S, LANE), F32)],
        compiler_params=_params(("arbitrary", "arbitrary")),
        name="dsa_fwd",
        args=(proj, proj, proj, proj, proj, _pair_tiles(tab)),
    )


def dsa_bwd(proj, tab, ob_out, lse, dmixed, exchange=None):
    T = proj.shape[0]
    nsb, qcol, kcol, vcol = _dsa_specs(T)
    S = DSA_SUPER
    nd = len(DSA_DILATIONS)
    ocol = GLA_WIDTH // LANE

    def body(*refs):
        refs = _host_exchange(exchange, refs, 9, 4, pl.program_id(0) * nsb + pl.program_id(1), DSA_PAIRS * nsb)
        (q_ref, kp_ref, kc_ref, vp_ref, vc_ref, tab_ref, o_ref, lse_ref, do_ref,
         dq_ref, dk_ref, dv_ref, dtab_ref, kk, vv, dqa, dkk, dvv) = refs
        j = pl.program_id(1)
        sb = nsb - 1 - j
        kk[0:S, :] = kp_ref[...]
        kk[S:2 * S, :] = kc_ref[...]
        vv[0:S, :] = vp_ref[...]
        vv[S:2 * S, :] = vc_ref[...]
        head0 = _head_lane_mask()

        @pl.when(j == 0)
        def _():
            dtab_ref[...] = jnp.zeros_like(dtab_ref)
            dkk[S:2 * S, :] = jnp.zeros((S, LANE), F32)
            dvv[S:2 * S, :] = jnp.zeros((S, LANE), F32)

        @pl.when(j > 0)
        def _():
            dkk[S:2 * S, :] = dkk[0:S, :]
            dvv[S:2 * S, :] = dvv[0:S, :]

        dkk[0:S, :] = jnp.zeros((S, LANE), F32)
        dvv[0:S, :] = jnp.zeros((S, LANE), F32)
        dqa[...] = jnp.zeros_like(dqa)

        for di, d in enumerate(DSA_DILATIONS):
            n_res, n_blk = _dsa_units(d)

            def unit(u, carry, di=di, d=d, n_blk=n_blk):
                r = u // n_blk
                c = u % n_blk
                q0 = r + d * DSA_BLOCK * c
                qrows = pl.ds(q0, DSA_BLOCK, stride=d) if d > 1 else pl.ds(q0, DSA_BLOCK)
                krows = pl.ds(S + q0 - d * DSA_BLOCK, 2 * DSA_BLOCK, stride=d) if d > 1 else pl.ds(S + q0 - DSA_BLOCK, 2 * DSA_BLOCK)
                q2 = q_ref[qrows, :] * QK_SCALE
                k2 = kk[krows, :].astype(BF16)
                v2 = vv[krows, :].astype(BF16)
                do2 = do_ref[qrows, :]
                o2 = o_ref[qrows, :]
                l2 = lse_ref[qrows, :]
                qs = _stack_heads(q2, head0).astype(BF16)
                dos = _stack_heads(do2, head0)
                dos_b = dos.astype(BF16)
                delta = jnp.sum(dos * jnp.concatenate([o2, o2], axis=0), axis=-1, keepdims=True)
                lse = jnp.concatenate([jnp.max(jnp.where(head0, l2, -jnp.inf), axis=-1, keepdims=True),
                                       jnp.max(jnp.where(head0, -jnp.inf, l2), axis=-1, keepdims=True)], axis=0)
                s = _dot_nt(qs, k2) + (tab_ref[di, 0] + _first_block_penalty((sb == 0) & (c == 0)))
                p = jnp.exp(s - lse)
                ds = p * (_dot_nt(dos_b, v2) - delta)
                dtab_ref[di, 0] += ds
                ds_b = ds.astype(BF16)
                dq = _dot(ds_b, k2)
                dqa[qrows, :] += jnp.where(head0, dq[:DSA_BLOCK], dq[DSA_BLOCK:]) * QK_SCALE
                dkk[krows, :] += _dot_tn(ds_b, qs)
                dvv[krows, :] += _dot_tn(p.astype(BF16), dos_b)
                return carry

            lax.fori_loop(0, n_res * n_blk, unit, 0, unroll=DSA_UNROLL)

        dq_ref[...] = dqa[...].astype(BF16)
        dk_ref[...] = dkk[S:2 * S, :].astype(BF16)
        dv_ref[...] = dvv[S:2 * S, :].astype(BF16)

    prev = lambda col: (lambda hp, j: (jnp.maximum(nsb - 2 - j, 0), col + hp))
    cur = lambda col: (lambda hp, j: (nsb - 1 - j, col + hp))
    blk = lambda f: pl.BlockSpec((S, LANE), f)
    out_blk = blk(lambda hp, j: (nsb - 1 - j, hp))
    tab_blk = pl.BlockSpec((nd, 1, 2 * DSA_BLOCK, 2 * DSA_BLOCK), lambda hp, j: (0, hp, 0, 0))
    dq, dk, dv, dtab, *carried = _hosted_call(
        exchange,
        body,
        grid=(DSA_PAIRS, nsb),
        in_specs=[blk(cur(qcol)), blk(prev(kcol)), blk(cur(kcol)), blk(prev(vcol)), blk(cur(vcol)), tab_blk,
                  out_blk, out_blk, blk(cur(ocol))],
        out_specs=[out_blk, out_blk, out_blk, tab_blk],
        out_shape=[jax.ShapeDtypeStruct((T, DSA_WIDTH), BF16)] * 3
        + [jax.ShapeDtypeStruct((nd, DSA_PAIRS, 2 * DSA_BLOCK, 2 * DSA_BLOCK), F32)],
        scratch_shapes=[pltpu.VMEM((2 * S, LANE), F32), pltpu.VMEM((2 * S, LANE), F32), pltpu.VMEM((S, LANE), F32),
                        pltpu.VMEM((2 * S, LANE), F32), pltpu.VMEM((2 * S, LANE), F32)],
        compiler_params=_params(("arbitrary", "arbitrary")),
        name="dsa_bwd",
        args=(proj, proj, proj, proj, proj, _pair_tiles(tab), ob_out, lse, dmixed),
    )
    return (dq, dk, dv, dtab.reshape(nd, DSA_HEADS, DSA_BLOCK, 2 * DSA_BLOCK), *carried)


FF_BLOCKS = 4
FF_BLOCK = D_FF // FF_BLOCKS


def post_fused(x, oa, ob, tgt, g2, gf, wout, wff1, wff2):
    T = x.shape[0]
    tm = 256
    inv_d = 1.0 / D_MODEL

    def body(x_ref, oa_ref, ob_ref, tgt_ref, g2_ref, gf_ref, wout_hbm, wff1_hbm, wff2_hbm,
             mixed_ref, nm_ref, a_ref, dpre_ref, dh2_ref, dh1_ref, dmixed_ref, loss_ref, dgf_ref, dg2_ref,
             wout_v, wff1_v, wff2_v, sems):
        @pl.when(pl.program_id(0) == 0)
        def _():
            cps = [pltpu.make_async_copy(s, d, sems.at[i])
                   for i, (s, d) in enumerate([(wout_hbm, wout_v), (wff1_hbm, wff1_v), (wff2_hbm, wff2_v)])]
            for cp in cps:
                cp.start()
            for cp in cps:
                cp.wait()
            loss_ref[...] = jnp.zeros_like(loss_ref)
            dgf_ref[...] = jnp.zeros_like(dgf_ref)
            dg2_ref[...] = jnp.zeros_like(dg2_ref)

        mixed = jnp.concatenate([oa_ref[...], ob_ref[...].astype(BF16)], axis=1)
        mixed_ref[...] = mixed
        h1 = x_ref[...] + _dot(mixed, wout_v[...])
        rs1 = _rstd(h1)
        hn1 = h1 * rs1
        g2 = g2_ref[...]
        nm = (hn1 * g2).astype(BF16)
        nm_ref[...] = nm
        relu = []
        mlp = jnp.zeros((tm, D_MODEL), F32)
        for j in range(FF_BLOCKS):
            cols = slice(j * FF_BLOCK, (j + 1) * FF_BLOCK)
            r_j = jnp.maximum(_dot(nm, wff1_v[j]), 0.0)
            a_j = (r_j * r_j).astype(BF16)
            a_ref[:, cols] = a_j
            relu.append(r_j)
            mlp = mlp + _dot(a_j, wff2_v[cols, :])
        h2 = h1 + mlp
        rsf = _rstd(h2)
        hnf = h2 * rsf
        gf = gf_ref[...]
        diff = hnf * gf - tgt_ref[...]
        loss_ref[...] += 0.5 * jnp.sum(jnp.sum(diff * diff, axis=-1, keepdims=True) * inv_d, axis=0, keepdims=True)
        dy = diff * inv_d
        dgf_ref[...] += jnp.sum(dy * hnf, axis=0, keepdims=True)
        dhnf = dy * gf
        dh2 = rsf * (dhnf - hnf * jnp.mean(dhnf * hnf, axis=-1, keepdims=True))
        dh2_b = dh2.astype(BF16)
        dh2_ref[...] = dh2_b
        dnm = jnp.zeros((tm, D_MODEL), F32)
        for j in range(FF_BLOCKS):
            cols = slice(j * FF_BLOCK, (j + 1) * FF_BLOCK)
            dpre_j = (_dot_nt(dh2_b, wff2_v[cols, :]) * (2.0 * relu[j])).astype(BF16)
            dpre_ref[:, cols] = dpre_j
            dnm = dnm + _dot_nt(dpre_j, wff1_v[j])
        dg2_ref[...] += jnp.sum(dnm * hn1, axis=0, keepdims=True)
        dhn1 = dnm * g2
        dh1 = dh2 + rs1 * (dhn1 - hn1 * jnp.mean(dhn1 * hn1, axis=-1, keepdims=True))
        dh1_ref[...] = dh1
        dmixed_ref[...] = _dot_nt(dh1.astype(BF16), wout_v[...])

    row = lambda w: pl.BlockSpec((tm, w), lambda i: (i, 0))
    vec = lambda w: pl.BlockSpec((1, w), lambda i: (0, 0))
    return pl.pallas_call(
        body,
        grid=(T // tm,),
        in_specs=[row(D_MODEL), row(GLA_WIDTH), row(DSA_WIDTH), row(D_MODEL), vec(D_MODEL), vec(D_MODEL), ANY, ANY, ANY],
        out_specs=[row(D_MODEL), row(D_MODEL), row(D_FF), row(D_FF), row(D_MODEL), row(D_MODEL), row(D_MODEL),
                   vec(1), vec(D_MODEL), vec(D_MODEL)],
        out_shape=[
            jax.ShapeDtypeStruct((T, D_MODEL), BF16),
            jax.ShapeDtypeStruct((T, D_MODEL), BF16),
            jax.ShapeDtypeStruct((T, D_FF), BF16),
            jax.ShapeDtypeStruct((T, D_FF), BF16),
            jax.ShapeDtypeStruct((T, D_MODEL), BF16),
            jax.ShapeDtypeStruct((T, D_MODEL), F32),
            jax.ShapeDtypeStruct((T, D_MODEL), F32),
            jax.ShapeDtypeStruct((1, 1), F32),
            jax.ShapeDtypeStruct((1, D_MODEL), F32),
            jax.ShapeDtypeStruct((1, D_MODEL), F32),
        ],
        scratch_shapes=[pltpu.VMEM((D_MODEL, D_MODEL), BF16), pltpu.VMEM((FF_BLOCKS, D_MODEL, FF_BLOCK), BF16),
                        pltpu.VMEM((D_FF, D_MODEL), BF16), pltpu.SemaphoreType.DMA((3,))],
        compiler_params=_params(),
        name="post_fused",
    )(x, oa, ob, tgt, g2, gf, wout, wff1, wff2)


WGRAD_TOKENS = 1024


def wgrad(a, b, name, bm=None, bn=None, col_blocked=False):
    T, M = a.shape
    N = b.shape[1]
    bm = M if bm is None else bm
    bn = N if bn is None else bn
    tk = min(WGRAD_TOKENS, T)

    def body(a_ref, b_ref, o_ref):
        part = _dot_tn(a_ref[...].astype(BF16), b_ref[...].astype(BF16))
        out = o_ref.at[0] if col_blocked else o_ref

        @pl.when(pl.program_id(2) == 0)
        def _():
            out[...] = part

        @pl.when(pl.program_id(2) > 0)
        def _():
            out[...] += part

    if col_blocked:
        assert bm == M
        out_spec = pl.BlockSpec((1, M, bn), lambda i, j, k: (j, 0, 0))
        out_shape = jax.ShapeDtypeStruct((N // bn, M, bn), F32)
    else:
        out_spec = pl.BlockSpec((bm, bn), lambda i, j, k: (i, j))
        out_shape = jax.ShapeDtypeStruct((M, N), F32)
    return pl.pallas_call(
        body,
        grid=(M // bm, N // bn, T // tk),
        in_specs=[pl.BlockSpec((tk, bm), lambda i, j, k: (k, i)), pl.BlockSpec((tk, bn), lambda i, j, k: (k, j))],
        out_specs=out_spec,
        out_shape=out_shape,
        compiler_params=_params(("arbitrary", "arbitrary", "arbitrary")),
        name=name,
    )(a, b)


def dx_final(x, dh1, g1, da, dq, dk, dv, wp):
    T = x.shape[0]
    tm = 256

    def body(x_ref, dh1_ref, g_ref, da_ref, dq_ref, dk_ref, dv_ref, w_hbm, dx_ref, dg_ref, w_vmem, sem):
        @pl.when(pl.program_id(0) == 0)
        def _():
            _load_once(w_hbm, w_vmem, sem)
            dg_ref[...] = jnp.zeros_like(dg_ref)

        dnx = (_dot_nt(da_ref[...], w_vmem[:, 0:P_A]) + _dot_nt(dq_ref[...], w_vmem[:, P_DQ:P_DQ + DSA_WIDTH])
               + _dot_nt(dk_ref[...], w_vmem[:, P_DK:P_DK + DSA_WIDTH]) + _dot_nt(dv_ref[...], w_vmem[:, P_DV:P_DV + DSA_WIDTH]))
        xf = x_ref[...]
        rs = _rstd(xf)
        hn = xf * rs
        dg_ref[...] += jnp.sum(dnx * hn, axis=0, keepdims=True)
        dhn = dnx * g_ref[...]
        dx_ref[...] = dh1_ref[...] + rs * (dhn - hn * jnp.mean(dhn * hn, axis=-1, keepdims=True))

    row = lambda w: pl.BlockSpec((tm, w), lambda i: (i, 0))
    vec = pl.BlockSpec((1, D_MODEL), lambda i: (0, 0))
    return pl.pallas_call(
        body,
        grid=(T // tm,),
        in_specs=[row(D_MODEL), row(D_MODEL), vec, row(P_A), row(DSA_WIDTH), row(DSA_WIDTH), row(DSA_WIDTH), ANY],
        out_specs=[row(D_MODEL), vec],
        out_shape=[jax.ShapeDtypeStruct((T, D_MODEL), F32), jax.ShapeDtypeStruct((1, D_MODEL), F32)],
        scratch_shapes=[pltpu.VMEM((D_MODEL, P_ALL), BF16), pltpu.SemaphoreType.DMA],
        compiler_params=_params(),
        name="dx_final",
    )(x, dh1, g1, da, dq, dk, dv, wp)


def adamw(w, g, m, v, name):
    R, C = w.shape
    br = 256 if R % 256 == 0 else R

    def body(w_ref, g_ref, m_ref, v_ref, d_ref, nm_ref, nv_ref):
        d_ref[...], nm_ref[...], nv_ref[...] = _adamw_math(w_ref[...], g_ref[...], m_ref[...], v_ref[...])

    spec = pl.BlockSpec((br, C), lambda i: (i, 0))
    return pl.pallas_call(
        body,
        grid=(R // br,),
        in_specs=[spec] * 4,
        out_specs=[spec] * 3,
        out_shape=[jax.ShapeDtypeStruct((R, C), F32)] * 3,
        compiler_params=_params(),
        name=name,
    )(w, g, m, v)


def _place():
    return lax.axis_index("x"), lax.axis_index("y"), lax.axis_index("c")


def _other_chips(x, y):
    return [(1 - x, y), (x, 1 - y), (1 - x, 1 - y)]


class Exchange:
    def __init__(self, kind, arrays):
        self.kind, self.arrays, self.n = kind, arrays, len(arrays)

    def out_shapes(self):
        if self.kind == "gather":
            return [jax.ShapeDtypeStruct((4,) + s.shape, s.dtype) for s in self.arrays]
        if self.kind == "scatter":
            return [jax.ShapeDtypeStruct((4, 2 * s.shape[1], s.shape[2]), s.dtype) for s in self.arrays]
        return [jax.ShapeDtypeStruct((4, s.shape[1] // 2, s.shape[2]), s.dtype) for s in self.arrays]

    def sems(self):
        return [pltpu.SemaphoreType.DMA((self.n, 7)), pltpu.SemaphoreType.DMA((self.n, 7))]

    def phases(self, ins, outs, send_sems, recv_sems):
        n, kind = self.n, self.kind
        x, y, c = _place()
        me, sib = (x, y, c), (x, y, 1 - c)
        mine = 2 * x + y
        chips = _other_chips(x, y)

        def region(a, owner, half):
            h = outs[a].shape[1] // 2
            return outs[a].at[owner, pl.ds(half * h, h)]

        def copy(a, k, owner, half, to, src=None):
            return pltpu.make_async_remote_copy(
                src_ref=region(a, owner, half) if src is None else src, dst_ref=region(a, owner, half),
                send_sem=send_sems.at[a, k], recv_sem=recv_sems.at[a, k], device_id=to, device_id_type=MESH)

        def swap_copy(a, to):
            h = ins[a].shape[1] // 2
            return pltpu.make_async_remote_copy(
                src_ref=ins[a].at[:, pl.ds((1 - c) * h, h)], dst_ref=outs[a],
                send_sem=send_sems.at[a, 0], recv_sem=recv_sems.at[a, 0], device_id=to, device_id_type=MESH)

        def first_copies():
            if kind == "swap":
                return [swap_copy(a, sib) for a in range(n)]
            cps = []
            for a in range(n):
                h = outs[a].shape[1] // 2
                for t, (cx, cy) in enumerate(chips):
                    src = ins[a].at[2 * cx + cy] if kind == "scatter" else ins[a].at[pl.ds(c * h, h)]
                    cps.append(copy(a, t, mine, c, (cx, cy, c), src=src))
                if kind == "scatter":
                    cps.append(copy(a, 6, mine, c, sib, src=ins[a].at[mine]))
            return cps

        def forward_copies():
            if kind == "swap":
                return []
            return [copy(a, 3 + t, 2 * cx + cy, c, sib) for a in range(n) for t, (cx, cy) in enumerate(chips)]

        def start():
            for cp in first_copies():
                cp.start()

        def forward():
            if kind == "swap":
                return
            fws = forward_copies()
            for a in range(n):
                for t, (cx, cy) in enumerate(chips):
                    copy(a, t, 2 * cx + cy, c, me).wait_recv()
                    fws[3 * a + t].start()

        def finish():
            for a in range(n):
                if kind == "swap":
                    swap_copy(a, me).wait_recv()
                    continue
                for t, (cx, cy) in enumerate(chips):
                    copy(a, 3 + t, 2 * cx + cy, 1 - c, me).wait_recv()
                if kind == "scatter":
                    copy(a, 6, mine, 1 - c, me).wait_recv()
            for cp in first_copies() + forward_copies():
                cp.wait_send()

        return start, forward, finish

    def fill_own(self, outs):
        if self.kind == "swap":
            return list(outs)
        x, y, c = _place()
        if self.kind == "gather":
            return [lax.dynamic_update_index_in_dim(o, s, 2 * x + y, 0) for o, s in zip(outs, self.arrays)]
        filled = []
        for o, s in zip(outs, self.arrays):
            own = lax.dynamic_index_in_dim(s, 2 * x + y, 0, keepdims=True)
            filled.append(lax.dynamic_update_slice(o, own, (2 * x + y, c * s.shape[1], 0)))
        return filled

    def run(self, name):
        n = self.n

        def body(*refs):
            start, forward, finish = self.phases(refs[:n], refs[n:2 * n], *refs[2 * n:])
            start()
            forward()
            finish()

        outs = pl.pallas_call(
            body, in_specs=[ANY] * n, out_specs=[ANY] * n, out_shape=self.out_shapes(), scratch_shapes=self.sems(), name=name,
        )(*self.arrays)
        return self.fill_own(outs)


def _host_exchange(exchange, refs, n_in, n_out, step, n_steps):
    if exchange is None:
        return refs
    n = exchange.n
    own_in, ex_in = refs[:n_in], refs[n_in:n_in + n]
    own_out, ex_out = refs[n_in + n:n_in + n + n_out], refs[n_in + n + n_out:n_in + 2 * n + n_out]
    rest = refs[n_in + 2 * n + n_out:]
    start, forward, finish = exchange.phases(ex_in, ex_out, rest[-2], rest[-1])
    pl.when(step == 0)(start)
    pl.when(step == (2 * n_steps) // 3)(forward)
    pl.when(step == n_steps - 1)(finish)
    return own_in + own_out + rest[:-2]


def _hosted_call(exchange, body, *, grid, in_specs, out_specs, out_shape, scratch_shapes, compiler_params, name, args):
    if exchange is None:
        return pl.pallas_call(body, grid=grid, in_specs=in_specs, out_specs=out_specs, out_shape=out_shape,
                              scratch_shapes=scratch_shapes, compiler_params=compiler_params, name=name)(*args)
    n = exchange.n
    res = pl.pallas_call(
        body, grid=grid, in_specs=list(in_specs) + [ANY] * n, out_specs=list(out_specs) + [ANY] * n,
        out_shape=list(out_shape) + exchange.out_shapes(), scratch_shapes=list(scratch_shapes) + exchange.sems(),
        compiler_params=compiler_params, name=name)(*args, *exchange.arrays)
    return list(res[:len(out_shape)]) + [exchange.fill_own(res[len(out_shape):])]


def add_halves(g, got, core, name):
    _, R, C = g.shape
    h = R // 2
    br = 128
    nb = h // br

    def body(core_ref, g_ref, got_ref, o_ref):
        o_ref[...] = (g_ref[...] + got_ref[...]).astype(BF16)

    return pl.pallas_call(
        body,
        grid_spec=pltpu.PrefetchScalarGridSpec(
            num_scalar_prefetch=1,
            grid=(4, nb),
            in_specs=[pl.BlockSpec((1, br, C), lambda s, i, core: (s, core[0] * nb + i, 0)),
                      pl.BlockSpec((1, br, C), lambda s, i, core: (s, i, 0))],
            out_specs=pl.BlockSpec((1, br, C), lambda s, i, core: (s, i, 0)),
        ),
        out_shape=jax.ShapeDtypeStruct((4, h, C), BF16),
        compiler_params=_params(("arbitrary", "arbitrary")),
        name=name,
    )(core, g, got)


def sum_slots(parts, name):
    S, R, C = parts.shape
    br = 128 if R % 128 == 0 else R

    def body(p_ref, o_ref):
        acc = p_ref[0].astype(F32)
        for s in range(1, S):
            acc = acc + p_ref[s].astype(F32)
        o_ref[...] = acc

    return pl.pallas_call(
        body,
        grid=(R // br,),
        in_specs=[pl.BlockSpec((S, br, C), lambda i: (0, i, 0))],
        out_specs=pl.BlockSpec((br, C), lambda i: (i, 0)),
        out_shape=jax.ShapeDtypeStruct((R, C), F32),
        compiler_params=_params(),
        name=name,
    )(parts)


def _adamw_math(w, g, m, v):
    m_new = ADAM_B1 * m + (1.0 - ADAM_B1) * g
    v_new = ADAM_B2 * v + (1.0 - ADAM_B2) * (g * g)
    m_hat = m_new / (1.0 - ADAM_B1 ** ADAM_STEP)
    v_hat = v_new / (1.0 - ADAM_B2 ** ADAM_STEP)
    return -ADAM_LR * (m_hat / (jnp.sqrt(v_hat) + ADAM_EPS) + ADAM_WD * w), m_new, v_new


def reduce_adamw(slots, w, m, v, name):
    S, R, C = slots.shape
    br = 128

    def body(p_ref, w_ref, m_ref, v_ref, g_ref, d_ref, nm_ref, nv_ref):
        g = p_ref[0].astype(F32)
        for s in range(1, S):
            g = g + p_ref[s].astype(F32)
        g_ref[...] = g
        d_ref[...], nm_ref[...], nv_ref[...] = _adamw_math(w_ref[...], g, m_ref[...], v_ref[...])

    spec = pl.BlockSpec((br, C), lambda i: (i, 0))
    return pl.pallas_call(
        body,
        grid=(R // br,),
        in_specs=[pl.BlockSpec((S, br, C), lambda i: (0, i, 0)), spec, spec, spec],
        out_specs=[spec] * 4,
        out_shape=[jax.ShapeDtypeStruct((R, C), F32)] * 4,
        compiler_params=_params(),
        name=name,
    )(slots, w, m, v)


SMALL_ROWS = 64


def gather_small(vec):
    def body(v_ref, o_ref, send_sems, recv_sems, local_sem):
        x, y, c = _place()
        flips = [(fx, fy, fc) for fx in (0, 1) for fy in (0, 1) for fc in (0, 1)][1:]

        def peer(f):
            return (1 - x if f[0] else x, 1 - y if f[1] else y, 1 - c if f[2] else c)

        slot = lambda p: 4 * p[0] + 2 * p[1] + p[2]
        own = pltpu.make_async_copy(v_ref, o_ref.at[slot((x, y, c))], local_sem)
        own.start()
        cps = [pltpu.make_async_remote_copy(
            src_ref=v_ref, dst_ref=o_ref.at[slot((x, y, c))], send_sem=send_sems.at[k], recv_sem=recv_sems.at[k],
            device_id=peer(f), device_id_type=MESH) for k, f in enumerate(flips)]
        for cp in cps:
            cp.start()
        for k, f in enumerate(flips):
            pltpu.make_async_remote_copy(
                src_ref=v_ref, dst_ref=o_ref.at[slot(peer(f))], send_sem=send_sems.at[k], recv_sem=recv_sems.at[k],
                device_id=(x, y, c), device_id_type=MESH).wait_recv()
        for cp in cps:
            cp.wait_send()
        own.wait()

    return pl.pallas_call(
        body,
        in_specs=[ANY],
        out_specs=ANY,
        out_shape=jax.ShapeDtypeStruct((8,) + vec.shape, vec.dtype),
        scratch_shapes=[pltpu.SemaphoreType.DMA((7,)), pltpu.SemaphoreType.DMA((7,)), pltpu.SemaphoreType.DMA],
        name="gather_small",
    )(vec)


GLOW_PAD = LANE - GLA_RANK


def kernel(x, attn_norm_g, w_in, gla_gate_w2, gla_gate_b, gla_norm_g, rel_bias, w_out, mlp_norm_g, w_ff1, w_ff2, final_norm_g, loss_target, m_attn_norm_g, m_w_in, m_gla_gate_w2, m_gla_gate_b, m_gla_norm_g, m_rel_bias, m_w_out, m_mlp_norm_g, m_w_ff1, m_w_ff2, m_final_norm_g, v_attn_norm_g, v_w_in, v_gla_gate_w2, v_gla_gate_b, v_gla_norm_g, v_rel_bias, v_w_out, v_mlp_norm_g, v_w_ff1, v_w_ff2, v_final_norm_g):
    xs, tgt = x[0], loss_target[0]
    T = xs.shape[0]
    cx, cy, cc = _place()
    chip = 2 * cx + cy
    gf = final_norm_g.reshape(1, D_MODEL)

    win_g, w2_g = Exchange("gather", [w_in[0].astype(BF16), gla_gate_w2[0]]).run("gather_w_in")
    win = jnp.transpose(win_g, (1, 0, 2)).reshape(D_MODEL, D_IN)
    n_glow = R_GLOW + GLA_RANK
    wp = jnp.concatenate([win[:, :n_glow], jnp.zeros((D_MODEL, GLOW_PAD), BF16), win[:, n_glow:]], axis=1)
    w2 = jnp.transpose(w2_g, (1, 0, 2)).reshape(GLA_RANK, GLA_QK)
    w2p = jnp.concatenate([w2, jnp.zeros((GLOW_PAD, GLA_QK), F32)], axis=0)

    proj, nx = inproj(xs, attn_norm_g, wp)
    tab = bias_tables(rel_bias)
    ob, lse, (wout_g, wff1, wff2_g) = dsa_fwd(
        proj, tab, Exchange("gather", [w_out[0].astype(BF16), w_ff1[0].astype(BF16), w_ff2[0].astype(BF16)]))
    wout = wout_g.reshape(D_MODEL, D_MODEL)
    wff2 = wff2_g.reshape(D_FF, D_MODEL)
    oa, opre, sprev = gla_fwd(proj, w2p, gla_gate_b, gla_norm_g)
    mixed, nm, act, dpre, dh2, dh1, dmixed, loss, dgf, dg2 = post_fused(xs, oa, ob, tgt, mlp_norm_g, gf, wout, wff1, wff2)

    core = cc.astype(jnp.int32).reshape(1)
    late = [
        wgrad(mixed, dh1, "wgrad_out").reshape(4, D_MODEL // 4, D_MODEL),
        wgrad(nm, dpre, "wgrad_ff1", bn=FF_BLOCK, col_blocked=True),
        wgrad(act, dh2, "wgrad_ff2", bm=FF_BLOCK).reshape(4, FF_BLOCK, D_MODEL),
    ]
    late_names = ["w_out", "w_ff1", "w_ff2"]
    da, dw2p, dbg, dgn, got = gla_bwd(proj, w2p, gla_gate_b, gla_norm_g, opre, sprev, dmixed, Exchange("swap", late))
    sums = [add_halves(g, r, core, "add_halves_" + s) for g, r, s in zip(late, got, late_names)]
    dq, dk, dv, dtab, late_slots = dsa_bwd(proj, tab, ob, lse, dmixed, Exchange("scatter", sums))
    slots = dict(zip(late_names, late_slots))
    drel = bias_tables_bwd(dtab)
    dxs, dg1 = dx_final(xs, dh1, attn_norm_g, da, dq, dk, dv, wp)

    dwa = wgrad(nx, da, "wgrad_in_gla")
    dwq = wgrad(nx, dq, "wgrad_in_q")
    dwk = wgrad(nx, dk, "wgrad_in_k")
    dwv = wgrad(nx, dv, "wgrad_in_v")
    dwin = jnp.concatenate([dwa[:, :n_glow], dwq, dwk, dwv], axis=1)
    dwin = [jnp.transpose(dwin.reshape(D_MODEL, 4, D_IN // 4), (1, 0, 2))]
    got = Exchange("swap", dwin).run("swap_w_in")
    sums = [add_halves(dwin[0], got[0], core, "add_halves_w_in")]
    slots["w_in"] = Exchange("scatter", sums).run("scatter_w_in")[0]

    small = jnp.concatenate([dg1.reshape(-1), dbg.reshape(-1), dgn.reshape(-1), drel.reshape(-1), dg2.reshape(-1),
                             dgf.reshape(-1), dw2p[:GLA_RANK].reshape(-1)]).reshape(SMALL_ROWS, LANE)
    tot = sum_slots(gather_small(small), "sum_small").reshape(-1)
    sizes = [D_MODEL, GLA_QK, GLA_WIDTH, REL_BUCKETS * DSA_HEADS, D_MODEL, D_MODEL, GLA_RANK * GLA_QK]
    offs = np.concatenate([[0], np.cumsum(sizes)])
    piece = lambda i: tot[int(offs[i]):int(offs[i + 1])]
    g_g1 = piece(0).reshape(1, D_MODEL)
    g_bg = piece(1).reshape(1, GLA_QK)
    g_gn = piece(2).reshape(1, GLA_WIDTH)
    g_rel = piece(3).reshape(REL_BUCKETS, DSA_HEADS)
    g_g2 = piece(4).reshape(1, D_MODEL)
    g_gf = piece(5).reshape(1, D_MODEL)
    g_w2 = lax.dynamic_slice_in_dim(piece(6).reshape(GLA_RANK, GLA_QK), chip * (GLA_QK // 4), GLA_QK // 4, axis=1)

    loss_all = lax.psum(loss[0, 0], ("x", "y", "c"))

    upd = [
        ("attn_norm_g", attn_norm_g, g_g1, m_attn_norm_g, v_attn_norm_g),
        ("w_in", w_in[0], None, m_w_in[0], v_w_in[0]),
        ("gla_gate_w2", gla_gate_w2[0], g_w2, m_gla_gate_w2[0], v_gla_gate_w2[0]),
        ("gla_gate_b", gla_gate_b, g_bg, m_gla_gate_b, v_gla_gate_b),
        ("gla_norm_g", gla_norm_g, g_gn, m_gla_norm_g, v_gla_norm_g),
        ("rel_bias", rel_bias, g_rel, m_rel_bias, v_rel_bias),
        ("w_out", w_out[0], None, m_w_out[0], v_w_out[0]),
        ("mlp_norm_g", mlp_norm_g, g_g2, m_mlp_norm_g, v_mlp_norm_g),
        ("w_ff1", w_ff1[0], None, m_w_ff1[0], v_w_ff1[0]),
        ("w_ff2", w_ff2[0], None, m_w_ff2[0], v_w_ff2[0]),
        ("final_norm_g", gf, g_gf, m_final_norm_g.reshape(1, D_MODEL), v_final_norm_g.reshape(1, D_MODEL)),
    ]
    shapes = [attn_norm_g.shape, w_in.shape, gla_gate_w2.shape, gla_gate_b.shape, gla_norm_g.shape, rel_bias.shape,
              w_out.shape, mlp_norm_g.shape, w_ff1.shape, w_ff2.shape, final_norm_g.shape]
    grads, deltas, new_m, new_v = [], [], [], []
    for (name, w, g, m, v), shape in zip(upd, shapes):
        if name in slots:
            g, d, nm_, nv_ = reduce_adamw(slots[name], w, m, v, "reduce_adamw_" + name)
        else:
            d, nm_, nv_ = adamw(w, g, m, v, "adamw_" + name)
        grads.append(g.reshape(shape))
        deltas.append(d.reshape(shape))
        new_m.append(nm_.reshape(shape))
        new_v.append(nv_.reshape(shape))
    return (loss_all, dxs.reshape(1, T, D_MODEL), *grads, *deltas, *new_m, *new_v)
```

```python
import functools
import math

import jax
import jax.numpy as jnp
import numpy as np
from jax import lax
from jax.experimental import pallas as pl
from jax.experimental.pallas import tpu as pltpu

F32 = jnp.float32
BF16 = jnp.bfloat16
MESH = pl.DeviceIdType.MESH

D_MODEL = 1024
GLA_WIDTH = 512
GLA_HEADS = 4
GLA_DK = 64
GLA_DV = 128
GLA_QK = GLA_HEADS * GLA_DK
GLA_RANK = 16
GLA_TAU = 16.0
GLA_CHUNK = 64
DSA_WIDTH = 512
DSA_HEADS = 8
DSA_DH = 64
DSA_DILATIONS = (1, 4, 16)
DSA_SPAN = 128
DSA_BLOCK = 128
DSA_SUPER = DSA_BLOCK * DSA_DILATIONS[-1]
REL_BUCKETS = 32
REL_MAX_DIST = 2048
D_FF = 4096
D_IN = 3088
EPS = 1e-6
NEG = -1e30
QK_SCALE = 0.125

ADAM_LR = 0.001
ADAM_B1 = 0.9
ADAM_B2 = 0.999
ADAM_EPS = 1e-08
ADAM_WD = 0.01
ADAM_STEP = 10

LANE = 128
P_GQ, P_GK, P_GV, P_GR = 0, 256, 512, 1024
P_GLOW = 1536
P_A = 1664
P_DQ, P_DK, P_DV = 1664, 2176, 2688
P_ALL = 3200
R_GQ, R_GK, R_GV, R_GR, R_GLOW, R_DQ, R_DK, R_DV = 0, 256, 512, 1024, 1536, 1552, 2064, 2576

VMEM_LIMIT = 56 * 1024 * 1024


def _params(sem=("arbitrary",), vmem=VMEM_LIMIT):
    return pltpu.CompilerParams(dimension_semantics=sem, vmem_limit_bytes=vmem)


def _dot(a, b):
    return jnp.dot(a, b, preferred_element_type=F32)


def _dot_nt(a, b):
    return lax.dot_general(a, b, (((1,), (1,)), ((), ())), preferred_element_type=F32)


def _dot_tn(a, b):
    return lax.dot_general(a, b, (((0,), (0,)), ((), ())), preferred_element_type=F32)


def _split3(x):
    x1 = x.astype(BF16)
    r1 = x - x1.astype(F32)
    x2 = r1.astype(BF16)
    x3 = (r1 - x2.astype(F32)).astype(BF16)
    return x1, x2, x3


def _dot_exact_lhs(m_bf16, x):
    x1, x2, x3 = _split3(x)
    return _dot(m_bf16, x1) + _dot(m_bf16, x2) + _dot(m_bf16, x3)


def _rstd(xf):
    return lax.rsqrt(jnp.mean(xf * xf, axis=-1, keepdims=True) + EPS)


def _load_once(hbm_ref, vmem_ref, sem):
    cp = pltpu.make_async_copy(hbm_ref, vmem_ref, sem)
    cp.start()
    cp.wait()


ANY = pl.BlockSpec(memory_space=pl.ANY)


def inproj(x, g1, wp):
    T = x.shape[0]
    tm = 256

    def body(x_ref, g_ref, w_hbm, proj_ref, nx_ref, w_vmem, sem):
        @pl.when(pl.program_id(0) == 0)
        def _():
            _load_once(w_hbm, w_vmem, sem)

        xf = x_ref[...]
        nx = ((xf * _rstd(xf)) * g_ref[...]).astype(BF16)
        nx_ref[...] = nx
        proj_ref[...] = _dot(nx, w_vmem[...])

    return pl.pallas_call(
        body,
        grid=(T // tm,),
        in_specs=[pl.BlockSpec((tm, D_MODEL), lambda i: (i, 0)), pl.BlockSpec((1, D_MODEL), lambda i: (0, 0)), ANY],
        out_specs=[pl.BlockSpec((tm, P_ALL), lambda i: (i, 0)), pl.BlockSpec((tm, D_MODEL), lambda i: (i, 0))],
        out_shape=[jax.ShapeDtypeStruct((T, P_ALL), F32), jax.ShapeDtypeStruct((T, D_MODEL), BF16)],
        scratch_shapes=[pltpu.VMEM((D_MODEL, P_ALL), BF16), pltpu.SemaphoreType.DMA],
        compiler_params=_params(),
        name="inproj",
    )(x, g1, wp)


GLA_CHUNKS_PER_STEP = 8
GLA_ROWS = GLA_CHUNK * GLA_CHUNKS_PER_STEP


def _gla_masks():
    lane = lax.broadcasted_iota(jnp.int32, (1, GLA_QK), 1)
    return [(lane >= h * GLA_DK) & (lane < (h + 1) * GLA_DK) for h in range(GLA_HEADS)]


def _log_sigmoid(x):
    return jnp.minimum(x, 0.0) - jnp.log(1.0 + jnp.exp(-jnp.abs(x)))


def _sigmoid(x):
    return 1.0 / (1.0 + jnp.exp(-x))


def _head_cols(h):
    return slice(h * GLA_DV, (h + 1) * GLA_DV)


def _gla_step_constants():
    ri = lax.broadcasted_iota(jnp.int32, (GLA_ROWS, GLA_ROWS), 0)
    ci = lax.broadcasted_iota(jnp.int32, (GLA_ROWS, GLA_ROWS), 1)
    shift = GLA_CHUNK.bit_length() - 1
    same = lax.shift_right_logical(ri, shift) == lax.shift_right_logical(ci, shift)
    return same & (ri >= ci), same & (ri <= ci), _gla_masks()


def _per_chunk(x):
    return x.reshape(GLA_CHUNKS_PER_STEP, GLA_CHUNK, x.shape[-1])


def _chunk_rows_of(x, c):
    return x[c * GLA_CHUNK:(c + 1) * GLA_CHUNK]


def _stack_masked(x, masks):
    return jnp.concatenate([jnp.where(m, x, 0.0) for m in masks], axis=0)


def _stack_head_cols(x):
    return jnp.concatenate([x[:, _head_cols(h)] for h in range(GLA_HEADS)], axis=0)


def _diag_blocks(full, masks):
    out = jnp.where(masks[0], full[:GLA_DV], 0.0)
    for h in range(1, GLA_HEADS):
        out = out + jnp.where(masks[h], full[h * GLA_DV:(h + 1) * GLA_DV], 0.0)
    return out


def _row_blocks_masked(full, masks):
    out = jnp.where(masks[0], full[:GLA_CHUNK], 0.0)
    for h in range(1, GLA_HEADS):
        out = out + jnp.where(masks[h], full[h * GLA_CHUNK:(h + 1) * GLA_CHUNK], 0.0)
    return out


def _gla_step_common(q, k, glow_b, w2, bg, tri):
    gpre = _dot(glow_b, w2) + bg
    glog = _log_sigmoid(gpre) / GLA_TAU
    b = _dot_exact_lhs(tri, glog)
    bl = jnp.sum(_per_chunk(glog), axis=1, keepdims=True)
    eb = jnp.exp(b)
    enb = jnp.exp(-b)
    eke = jnp.exp(jnp.broadcast_to(bl, (GLA_CHUNKS_PER_STEP, GLA_CHUNK, GLA_QK)).reshape(GLA_ROWS, GLA_QK) - b)
    return gpre, eb, enb, eke, jnp.exp(bl), (q * QK_SCALE) * eb, k * enb, k * eke


def gla_fwd(proj, w2p, bg, gn):
    T = proj.shape[0]
    n_steps = T // GLA_ROWS
    n_chunks = T // GLA_CHUNK

    def body(proj_ref, w2_ref, bg_ref, gn_ref, oa_ref, opre_ref, sprev_ref, st_ref):
        @pl.when(pl.program_id(0) == 0)
        def _():
            st_ref[...] = jnp.zeros_like(st_ref)

        causal, _, masks = _gla_step_constants()
        q = proj_ref[:, P_GQ:P_GQ + GLA_QK]
        k = proj_ref[:, P_GK:P_GK + GLA_QK]
        v = proj_ref[:, P_GV:P_GV + GLA_WIDTH]
        r = proj_ref[:, P_GR:P_GR + GLA_WIDTH]
        glow = proj_ref[:, P_GLOW:P_GLOW + LANE].astype(BF16)
        _, _, _, _, ebl, qd, ki, ke = _gla_step_common(q, k, glow, w2_ref[...].astype(BF16), bg_ref[...], causal.astype(BF16))
        ki_b = ki.astype(BF16)
        v_b = v.astype(BF16)
        o_heads = []
        for h in range(GLA_HEADS):
            att = jnp.where(causal, _dot_nt(jnp.where(masks[h], qd, 0.0).astype(BF16), ki_b), 0.0)
            o_heads.append(_dot(att.astype(BF16), v_b[:, _head_cols(h)]))
        st = st_ref[...]
        states = []
        for c in range(GLA_CHUNKS_PER_STEP):
            states.append(st)
            sprev_ref[c] = st
            inc = _diag_blocks(_dot_tn(_chunk_rows_of(v_b, c), _chunk_rows_of(ke, c).astype(BF16)), masks)
            st = st * ebl[c] + inc
        st_ref[...] = st
        inter = []
        for c in range(GLA_CHUNKS_PER_STEP):
            qd_c = _stack_masked(_chunk_rows_of(qd, c), masks).astype(BF16)
            got = _dot_nt(qd_c, states[c].astype(BF16))
            inter.append(jnp.concatenate([got[h * GLA_CHUNK:(h + 1) * GLA_CHUNK] for h in range(GLA_HEADS)], axis=1))
        o = jnp.concatenate(o_heads, axis=1) + jnp.concatenate(inter, axis=0)
        opre_ref[...] = o
        on = jnp.concatenate([o[:, _head_cols(h)] * _rstd(o[:, _head_cols(h)]) for h in range(GLA_HEADS)], axis=1)
        oa_ref[...] = ((on * gn_ref[...]) * (r * _sigmoid(r))).astype(BF16)

    return pl.pallas_call(
        body,
        grid=(n_steps,),
        in_specs=[
            pl.BlockSpec((GLA_ROWS, P_ALL), lambda i: (i, 0)),
            pl.BlockSpec((LANE, GLA_QK), lambda i: (0, 0)),
            pl.BlockSpec((1, GLA_QK), lambda i: (0, 0)),
            pl.BlockSpec((1, GLA_WIDTH), lambda i: (0, 0)),
        ],
        out_specs=[
            pl.BlockSpec((GLA_ROWS, GLA_WIDTH), lambda i: (i, 0)),
            pl.BlockSpec((GLA_ROWS, GLA_WIDTH), lambda i: (i, 0)),
            pl.BlockSpec((GLA_CHUNKS_PER_STEP, GLA_DV, GLA_QK), lambda i: (i, 0, 0)),
        ],
        out_shape=[
            jax.ShapeDtypeStruct((T, GLA_WIDTH), BF16),
            jax.ShapeDtypeStruct((T, GLA_WIDTH), F32),
            jax.ShapeDtypeStruct((n_chunks, GLA_DV, GLA_QK), F32),
        ],
        scratch_shapes=[pltpu.VMEM((GLA_DV, GLA_QK), F32)],
        compiler_params=_params(),
        name="gla_fwd",
    )(proj, w2p, bg, gn)


def gla_bwd(proj, w2p, bg, gn, opre, sprev, dmixed, exchange=None):
    T = proj.shape[0]
    n_steps = T // GLA_ROWS

    def body(*refs):
        refs = _host_exchange(exchange, refs, 7, 4, pl.program_id(0), n_steps)
        proj_ref, w2_ref, bg_ref, gn_ref, opre_ref, sprev_ref, doa_ref, da_ref, dw2_ref, dbg_ref, dgn_ref, dst_ref = refs

        @pl.when(pl.program_id(0) == 0)
        def _():
            dst_ref[...] = jnp.zeros_like(dst_ref)
            dw2_ref[...] = jnp.zeros_like(dw2_ref)
            dbg_ref[...] = jnp.zeros_like(dbg_ref)
            dgn_ref[...] = jnp.zeros_like(dgn_ref)

        causal, causal_t, masks = _gla_step_constants()
        w2 = w2_ref[...].astype(BF16)
        gn = gn_ref[...]
        q = proj_ref[:, P_GQ:P_GQ + GLA_QK]
        k = proj_ref[:, P_GK:P_GK + GLA_QK]
        v_b = proj_ref[:, P_GV:P_GV + GLA_WIDTH].astype(BF16)
        r = proj_ref[:, P_GR:P_GR + GLA_WIDTH]
        glow = proj_ref[:, P_GLOW:P_GLOW + LANE].astype(BF16)
        o = opre_ref[...]
        doa = doa_ref[...]
        gpre, eb, enb, eke, ebl, qd, ki, ke = _gla_step_common(q, k, glow, w2, bg_ref[...], causal.astype(BF16))
        sig = _sigmoid(r)
        rs = jnp.concatenate([jnp.broadcast_to(_rstd(o[:, _head_cols(h)]), (GLA_ROWS, GLA_DV)) for h in range(GLA_HEADS)], axis=1)
        on = o * rs
        d_ong = doa * (r * sig)
        dr = doa * (on * gn) * (sig * (1.0 + r * (1.0 - sig)))
        dgn_ref[...] += jnp.sum(d_ong * on, axis=0, keepdims=True)
        d_on = d_ong * gn
        t = d_on * on
        mean_t = jnp.concatenate([jnp.broadcast_to(jnp.mean(t[:, _head_cols(h)], axis=-1, keepdims=True), (GLA_ROWS, GLA_DV))
                                  for h in range(GLA_HEADS)], axis=1)
        do_b = (rs * (d_on - on * mean_t)).astype(BF16)
        ki_b = ki.astype(BF16)
        ke_b = ke.astype(BF16)
        dqd = jnp.zeros_like(qd)
        dki = jnp.zeros_like(qd)
        dv_heads = []
        for h in range(GLA_HEADS):
            qd_h = jnp.where(masks[h], qd, 0.0).astype(BF16)
            do_h = do_b[:, _head_cols(h)]
            att = jnp.where(causal, _dot_nt(qd_h, ki_b), 0.0).astype(BF16)
            d_att = jnp.where(causal, _dot_nt(do_h, v_b[:, _head_cols(h)]), 0.0).astype(BF16)
            dv_heads.append(_dot_tn(att, do_h))
            dqd = dqd + jnp.where(masks[h], _dot(d_att, ki_b), 0.0)
            dki = dki + _dot_tn(d_att, qd_h)
        states = [sprev_ref[c] for c in range(GLA_CHUNKS_PER_STEP)]
        dqd_inter, dst_adds = [], []
        for c in range(GLA_CHUNKS_PER_STEP):
            do_c = _stack_head_cols(_chunk_rows_of(do_b, c))
            dqd_inter.append(_row_blocks_masked(_dot(do_c, states[c].astype(BF16)), masks))
            dst_adds.append(_diag_blocks(_dot_tn(_chunk_rows_of(do_b, c), _chunk_rows_of(qd, c).astype(BF16)), masks))
        dst = dst_ref[...]
        dsts, debls = [None] * GLA_CHUNKS_PER_STEP, [None] * GLA_CHUNKS_PER_STEP
        for c in reversed(range(GLA_CHUNKS_PER_STEP)):
            dsts[c] = dst
            debls[c] = jnp.sum(dst * states[c], axis=0, keepdims=True)
            dst = dst * ebl[c] + dst_adds[c]
        dst_ref[...] = dst
        dv_inter, dke = [], []
        for c in range(GLA_CHUNKS_PER_STEP):
            dst_b = dsts[c].astype(BF16)
            got = _dot_nt(_stack_masked(_chunk_rows_of(ke, c), masks).astype(BF16), dst_b)
            dv_inter.append(jnp.concatenate([got[h * GLA_CHUNK:(h + 1) * GLA_CHUNK] for h in range(GLA_HEADS)], axis=1))
            dke.append(_row_blocks_masked(_dot(_stack_head_cols(_chunk_rows_of(v_b, c)), dst_b), masks))
        dqd = dqd + jnp.concatenate(dqd_inter, axis=0)
        dke = jnp.concatenate(dke, axis=0)
        dv = jnp.concatenate(dv_heads, axis=1) + jnp.concatenate(dv_inter, axis=0)
        dkk = dke * ke
        dbl = jnp.sum(_per_chunk(dkk), axis=1, keepdims=True) + jnp.stack(debls) * ebl
        last_row = lax.broadcasted_iota(jnp.int32, (GLA_CHUNKS_PER_STEP, GLA_CHUNK, GLA_QK), 1) == GLA_CHUNK - 1
        db = dqd * qd - dki * ki - dkk + jnp.where(last_row, dbl, 0.0).reshape(GLA_ROWS, GLA_QK)
        dglog = _dot_exact_lhs(causal_t.astype(BF16), db)
        dgpre = (dglog / GLA_TAU) * _sigmoid(-gpre)
        dgpre_b = dgpre.astype(BF16)
        da_ref[...] = jnp.concatenate(
            [dqd * eb * QK_SCALE, dki * enb + dke * eke, dv, dr, _dot_nt(dgpre_b, w2)], axis=1).astype(BF16)
        dw2_ref[...] += _dot_tn(glow, dgpre_b)
        dbg_ref[...] += jnp.sum(dgpre, axis=0, keepdims=True)

    rev = lambda i: (n_steps - 1 - i, 0)
    return _hosted_call(
        exchange,
        body,
        grid=(n_steps,),
        in_specs=[
            pl.BlockSpec((GLA_ROWS, P_ALL), rev),
            pl.BlockSpec((LANE, GLA_QK), lambda i: (0, 0)),
            pl.BlockSpec((1, GLA_QK), lambda i: (0, 0)),
            pl.BlockSpec((1, GLA_WIDTH), lambda i: (0, 0)),
            pl.BlockSpec((GLA_ROWS, GLA_WIDTH), rev),
            pl.BlockSpec((GLA_CHUNKS_PER_STEP, GLA_DV, GLA_QK), lambda i: (n_steps - 1 - i, 0, 0)),
            pl.BlockSpec((GLA_ROWS, GLA_WIDTH), rev),
        ],
        out_specs=[
            pl.BlockSpec((GLA_ROWS, P_A), rev),
            pl.BlockSpec((LANE, GLA_QK), lambda i: (0, 0)),
            pl.BlockSpec((1, GLA_QK), lambda i: (0, 0)),
            pl.BlockSpec((1, GLA_WIDTH), lambda i: (0, 0)),
        ],
        out_shape=[
            jax.ShapeDtypeStruct((T, P_A), BF16),
            jax.ShapeDtypeStruct((LANE, GLA_QK), F32),
            jax.ShapeDtypeStruct((1, GLA_QK), F32),
            jax.ShapeDtypeStruct((1, GLA_WIDTH), F32),
        ],
        scratch_shapes=[pltpu.VMEM((GLA_DV, GLA_QK), F32)],
        compiler_params=_params(),
        name="gla_bwd",
        args=(proj, w2p, bg, gn, opre, sprev, dmixed),
    )


def _t5_bucket(dist):
    max_exact = REL_BUCKETS // 2
    n = np.maximum(dist, 0)
    large = max_exact + (np.log(np.maximum(n, 1) / max_exact) / math.log(REL_MAX_DIST / max_exact)
                         * (REL_BUCKETS - max_exact)).astype(np.int32)
    large = np.minimum(large, REL_BUCKETS - 1)
    return np.where(n < max_exact, n, large).astype(np.int32)


def _bucket_ids():
    L = DSA_BLOCK
    steps = L + np.arange(L)[:, None] - np.arange(2 * L)[None, :]
    in_band = (steps >= 0) & (steps <= DSA_SPAN)
    return np.stack([np.where(in_band, _t5_bucket(steps * d), -1) for d in DSA_DILATIONS]).astype(np.int32)


def bias_tables(rel_bias):
    ids = jnp.asarray(_bucket_ids())
    nd = len(DSA_DILATIONS)

    def body(rel_ref, ids_ref, tab_ref):
        h = pl.program_id(1)
        idt = ids_ref[0]
        acc = jnp.where(idt < 0, NEG, 0.0).astype(F32)
        for b in range(REL_BUCKETS):
            acc = jnp.where(idt == b, rel_ref[b, h], acc)
        tab_ref[0, 0] = acc

    return pl.pallas_call(
        body,
        grid=(nd, DSA_HEADS),
        in_specs=[pl.BlockSpec(memory_space=pltpu.SMEM), pl.BlockSpec((1, DSA_BLOCK, 2 * DSA_BLOCK), lambda d, h: (d, 0, 0))],
        out_specs=pl.BlockSpec((1, 1, DSA_BLOCK, 2 * DSA_BLOCK), lambda d, h: (d, h, 0, 0)),
        out_shape=jax.ShapeDtypeStruct((nd, DSA_HEADS, DSA_BLOCK, 2 * DSA_BLOCK), F32),
        compiler_params=_params(("arbitrary", "arbitrary")),
        name="bias_tables",
    )(rel_bias, ids)


def bias_tables_bwd(dtab):
    ids = jnp.asarray(_bucket_ids())
    nd = len(DSA_DILATIONS)

    def body(dtab_ref, ids_ref, drel_ref):
        @pl.when((pl.program_id(0) == 0) & (pl.program_id(1) == 0))
        def _():
            for b in range(REL_BUCKETS):
                for h in range(DSA_HEADS):
                    drel_ref[b, h] = 0.0

        h = pl.program_id(1)
        idt = ids_ref[0]
        g = dtab_ref[0, 0]
        for b in range(REL_BUCKETS):
            drel_ref[b, h] += jnp.sum(jnp.where(idt == b, g, 0.0))

    return pl.pallas_call(
        body,
        grid=(nd, DSA_HEADS),
        in_specs=[pl.BlockSpec((1, 1, DSA_BLOCK, 2 * DSA_BLOCK), lambda d, h: (d, h, 0, 0)),
                  pl.BlockSpec((1, DSA_BLOCK, 2 * DSA_BLOCK), lambda d, h: (d, 0, 0))],
        out_specs=pl.BlockSpec(memory_space=pltpu.SMEM),
        out_shape=jax.ShapeDtypeStruct((REL_BUCKETS, DSA_HEADS), F32),
        compiler_params=_params(("arbitrary", "arbitrary")),
        name="bias_tables_bwd",
    )(dtab, ids)


DSA_PAIRS = DSA_HEADS // 2
DSA_UNROLL = 8
DSA_COMBINE_ROWS = 256


def _dsa_units(d):
    return d, DSA_SUPER // (DSA_BLOCK * d)


def _dsa_specs(T):
    nsb = T // DSA_SUPER
    qcol, kcol, vcol = P_DQ // LANE, P_DK // LANE, P_DV // LANE
    return nsb, qcol, kcol, vcol


def _head_lane_mask():
    return lax.broadcasted_iota(jnp.int32, (1, LANE), 1) < DSA_DH


def _first_block_penalty(first):
    col = lax.broadcasted_iota(jnp.int32, (2 * DSA_BLOCK, 2 * DSA_BLOCK), 1)
    return jnp.where(first & (col < DSA_BLOCK), NEG, 0.0).astype(F32)


def _pair_tiles(tab):
    return tab.reshape(len(DSA_DILATIONS), DSA_PAIRS, 2 * DSA_BLOCK, 2 * DSA_BLOCK)


def _stack_heads(t, head0):
    return jnp.concatenate([jnp.where(head0, t, 0.0), jnp.where(head0, 0.0, t)], axis=0)


def dsa_fwd(proj, tab, exchange=None):
    T = proj.shape[0]
    nsb, qcol, kcol, vcol = _dsa_specs(T)
    S = DSA_SUPER

    def body(*refs):
        refs = _host_exchange(exchange, refs, 6, 2, pl.program_id(0) * nsb + pl.program_id(1), DSA_PAIRS * nsb)
        q_ref, kp_ref, kc_ref, vp_ref, vc_ref, tab_ref, out_ref, lse_ref, kk, vv, ob, lb = refs
        sb = pl.program_id(1)
        kk[0:S, :] = kp_ref[...]
        kk[S:2 * S, :] = kc_ref[...]
        vv[0:S, :] = vp_ref[...]
        vv[S:2 * S, :] = vc_ref[...]
        head0 = _head_lane_mask()

        for di, d in enumerate(DSA_DILATIONS):
            n_res, n_blk = _dsa_units(d)

            def unit(u, carry, di=di, d=d, n_blk=n_blk):
                r = u // n_blk
                c = u % n_blk
                q0 = r + d * DSA_BLOCK * c
                qrows = pl.ds(q0, DSA_BLOCK, stride=d) if d > 1 else pl.ds(q0, DSA_BLOCK)
                krows = pl.ds(S + q0 - d * DSA_BLOCK, 2 * DSA_BLOCK, stride=d) if d > 1 else pl.ds(S + q0 - DSA_BLOCK, 2 * DSA_BLOCK)
                q2 = q_ref[qrows, :] * QK_SCALE
                k2 = kk[krows, :].astype(BF16)
                v2 = vv[krows, :].astype(BF16)
                qs = _stack_heads(q2, head0).astype(BF16)
                s = _dot_nt(qs, k2) + (tab_ref[di, 0] + _first_block_penalty((sb == 0) & (c == 0)))
                m = jnp.max(s, axis=-1, keepdims=True)
                p = jnp.exp(s - m)
                den = jnp.sum(p, axis=-1, keepdims=True)
                o = _dot(p.astype(BF16), v2) / den
                l = jnp.broadcast_to(m + jnp.log(den), (2 * DSA_BLOCK, LANE))
                ob[di, qrows, :] = jnp.where(head0, o[:DSA_BLOCK], o[DSA_BLOCK:])
                lb[di, qrows, :] = jnp.where(head0, l[:DSA_BLOCK], l[DSA_BLOCK:])
                return carry

            lax.fori_loop(0, n_res * n_blk, unit, 0, unroll=DSA_UNROLL)

        def combine(i, carry):
            rows = pl.ds(pl.multiple_of(i * DSA_COMBINE_ROWS, DSA_COMBINE_ROWS), DSA_COMBINE_ROWS)
            l0, l1, l2 = lb[0, rows, :], lb[1, rows, :], lb[2, rows, :]
            mx = jnp.maximum(jnp.maximum(l0, l1), l2)
            e0, e1, e2 = jnp.exp(l0 - mx), jnp.exp(l1 - mx), jnp.exp(l2 - mx)
            den = e0 + e1 + e2
            out_ref[rows, :] = (e0 * ob[0, rows, :] + e1 * ob[1, rows, :] + e2 * ob[2, rows, :]) / den
            lse_ref[rows, :] = mx + jnp.log(den)
            return carry

        lax.fori_loop(0, S // DSA_COMBINE_ROWS, combine, 0)

    prev = lambda col: (lambda hp, sb: (jnp.maximum(sb - 1, 0), col + hp))
    cur = lambda col: (lambda hp, sb: (sb, col + hp))
    blk = lambda f: pl.BlockSpec((S, LANE), f)
    return _hosted_call(
        exchange,
        body,
        grid=(DSA_PAIRS, nsb),
        in_specs=[blk(cur(qcol)), blk(prev(kcol)), blk(cur(kcol)), blk(prev(vcol)), blk(cur(vcol)),
                  pl.BlockSpec((len(DSA_DILATIONS), 1, 2 * DSA_BLOCK, 2 * DSA_BLOCK), lambda hp, sb: (0, hp, 0, 0))],
        out_specs=[blk(lambda hp, sb: (sb, hp)), blk(lambda hp, sb: (sb, hp))],
        out_shape=[jax.ShapeDtypeStruct((T, DSA_WIDTH), F32), jax.ShapeDtypeStruct((T, DSA_WIDTH), F32)],
        scratch_shapes=[pltpu.VMEM((2 * S, LANE), F32), pltpu.VMEM((2 * S, LANE), F32),
                        pltpu.VMEM((len(DSA_DILATIONS), S, LANE), F32), pltpu.VMEM((len(DSA_DILATIONS), S, LANE), F32)],
        compiler_params=_params(("arbitrary", "arbitrary")),
        name="dsa_fwd",
        args=(proj, proj, proj, proj, proj, _pair_tiles(tab)),
    )


def dsa_bwd(proj, tab, ob_out, lse, dmixed, exchange=None):
    T = proj.shape[0]
    nsb, qcol, kcol, vcol = _dsa_specs(T)
    S = DSA_SUPER
    nd = len(DSA_DILATIONS)
    ocol = GLA_WIDTH // LANE

    def body(*refs):
        refs = _host_exchange(exchange, refs, 9, 4, pl.program_id(0) * nsb + pl.program_id(1), DSA_PAIRS * nsb)
        (q_ref, kp_ref, kc_ref, vp_ref, vc_ref, tab_ref, o_ref, lse_ref, do_ref,
         dq_ref, dk_ref, dv_ref, dtab_ref, kk, vv, dqa, dkk, dvv) = refs
        j = pl.program_id(1)
        sb = nsb - 1 - j
        kk[0:S, :] = kp_ref[...]
        kk[S:2 * S, :] = kc_ref[...]
        vv[0:S, :] = vp_ref[...]
        vv[S:2 * S, :] = vc_ref[...]
        head0 = _head_lane_mask()

        @pl.when(j == 0)
        def _():
            dtab_ref[...] = jnp.zeros_like(dtab_ref)
            dkk[S:2 * S, :] = jnp.zeros((S, LANE), F32)
            dvv[S:2 * S, :] = jnp.zeros((S, LANE), F32)

        @pl.when(j > 0)
        def _():
            dkk[S:2 * S, :] = dkk[0:S, :]
            dvv[S:2 * S, :] = dvv[0:S, :]

        dkk[0:S, :] = jnp.zeros((S, LANE), F32)
        dvv[0:S, :] = jnp.zeros((S, LANE), F32)
        dqa[...] = jnp.zeros_like(dqa)

        for di, d in enumerate(DSA_DILATIONS):
            n_res, n_blk = _dsa_units(d)

            def unit(u, carry, di=di, d=d, n_blk=n_blk):
                r = u // n_blk
                c = u % n_blk
                q0 = r + d * DSA_BLOCK * c
                qrows = pl.ds(q0, DSA_BLOCK, stride=d) if d > 1 else pl.ds(q0, DSA_BLOCK)
                krows = pl.ds(S + q0 - d * DSA_BLOCK, 2 * DSA_BLOCK, stride=d) if d > 1 else pl.ds(S + q0 - DSA_BLOCK, 2 * DSA_BLOCK)
                q2 = q_ref[qrows, :] * QK_SCALE
                k2 = kk[krows, :].astype(BF16)
                v2 = vv[krows, :].astype(BF16)
                do2 = do_ref[qrows, :]
                o2 = o_ref[qrows, :]
                l2 = lse_ref[qrows, :]
                qs = _stack_heads(q2, head0).astype(BF16)
                dos = _stack_heads(do2, head0)
                dos_b = dos.astype(BF16)
                delta = jnp.sum(dos * jnp.concatenate([o2, o2], axis=0), axis=-1, keepdims=True)
                lse = jnp.concatenate([jnp.max(jnp.where(head0, l2, -jnp.inf), axis=-1, keepdims=True),
                                       jnp.max(jnp.where(head0, -jnp.inf, l2), axis=-1, keepdims=True)], axis=0)
                s = _dot_nt(qs, k2) + (tab_ref[di, 0] + _first_block_penalty((sb == 0) & (c == 0)))
                p = jnp.exp(s - lse)
                ds = p * (_dot_nt(dos_b, v2) - delta)
                dtab_ref[di, 0] += ds
                ds_b = ds.astype(BF16)
                dq = _dot(ds_b, k2)
                dqa[qrows, :] += jnp.where(head0, dq[:DSA_BLOCK], dq[DSA_BLOCK:]) * QK_SCALE
                dkk[krows, :] += _dot_tn(ds_b, qs)
                dvv[krows, :] += _dot_tn(p.astype(BF16), dos_b)
                return carry

            lax.fori_loop(0, n_res * n_blk, unit, 0, unroll=DSA_UNROLL)

        dq_ref[...] = dqa[...].astype(BF16)
        dk_ref[...] = dkk[S:2 * S, :].astype(BF16)
        dv_ref[...] = dvv[S:2 * S, :].astype(BF16)

    prev = lambda col: (lambda hp, j: (jnp.maximum(nsb - 2 - j, 0), col + hp))
    cur = lambda col: (lambda hp, j: (nsb - 1 - j, col + hp))
    blk = lambda f: pl.BlockSpec((S, LANE), f)
    out_blk = blk(lambda hp, j: (nsb - 1 - j, hp))
    tab_blk = pl.BlockSpec((nd, 1, 2 * DSA_BLOCK, 2 * DSA_BLOCK), lambda hp, j: (0, hp, 0, 0))
    dq, dk, dv, dtab, *carried = _hosted_call(
        exchange,
        body,
        grid=(DSA_PAIRS, nsb),
        in_specs=[blk(cur(qcol)), blk(prev(kcol)), blk(cur(kcol)), blk(prev(vcol)), blk(cur(vcol)), tab_blk,
                  out_blk, out_blk, blk(cur(ocol))],
        out_specs=[out_blk, out_blk, out_blk, tab_blk],
        out_shape=[jax.ShapeDtypeStruct((T, DSA_WIDTH), BF16)] * 3
        + [jax.ShapeDtypeStruct((nd, DSA_PAIRS, 2 * DSA_BLOCK, 2 * DSA_BLOCK), F32)],
        scratch_shapes=[pltpu.VMEM((2 * S, LANE), F32), pltpu.VMEM((2 * S, LANE), F32), pltpu.VMEM((S, LANE), F32),
                        pltpu.VMEM((2 * S, LANE), F32), pltpu.VMEM((2 * S, LANE), F32)],
        compiler_params=_params(("arbitrary", "arbitrary")),
        name="dsa_bwd",
        args=(proj, proj, proj, proj, proj, _pair_tiles(tab), ob_out, lse, dmixed),
    )
    return (dq, dk, dv, dtab.reshape(nd, DSA_HEADS, DSA_BLOCK, 2 * DSA_BLOCK), *carried)


FF_BLOCKS = 4
FF_BLOCK = D_FF // FF_BLOCKS


def post_fused(x, oa, ob, tgt, g2, gf, wout, wff1, wff2):
    T = x.shape[0]
    tm = 256
    inv_d = 1.0 / D_MODEL

    def body(x_ref, oa_ref, ob_ref, tgt_ref, g2_ref, gf_ref, wout_hbm, wff1_hbm, wff2_hbm,
             mixed_ref, nm_ref, a_ref, dpre_ref, dh2_ref, dh1_ref, dmixed_ref, loss_ref, dgf_ref, dg2_ref,
             wout_v, wff1_v, wff2_v, sems):
        @pl.when(pl.program_id(0) == 0)
        def _():
            cps = [pltpu.make_async_copy(s, d, sems.at[i])
                   for i, (s, d) in enumerate([(wout_hbm, wout_v), (wff1_hbm, wff1_v), (wff2_hbm, wff2_v)])]
            for cp in cps:
                cp.start()
            for cp in cps:
                cp.wait()
            loss_ref[...] = jnp.zeros_like(loss_ref)
            dgf_ref[...] = jnp.zeros_like(dgf_ref)
            dg2_ref[...] = jnp.zeros_like(dg2_ref)

        mixed = jnp.concatenate([oa_ref[...], ob_ref[...].astype(BF16)], axis=1)
        mixed_ref[...] = mixed
        h1 = x_ref[...] + _dot(mixed, wout_v[...])
        rs1 = _rstd(h1)
        hn1 = h1 * rs1
        g2 = g2_ref[...]
        nm = (hn1 * g2).astype(BF16)
        nm_ref[...] = nm
        relu = []
        mlp = jnp.zeros((tm, D_MODEL), F32)
        for j in range(FF_BLOCKS):
            cols = slice(j * FF_BLOCK, (j + 1) * FF_BLOCK)
            r_j = jnp.maximum(_dot(nm, wff1_v[j]), 0.0)
            a_j = (r_j * r_j).astype(BF16)
            a_ref[:, cols] = a_j
            relu.append(r_j)
            mlp = mlp + _dot(a_j, wff2_v[cols, :])
        h2 = h1 + mlp
        rsf = _rstd(h2)
        hnf = h2 * rsf
        gf = gf_ref[...]
        diff = hnf * gf - tgt_ref[...]
        loss_ref[...] += 0.5 * jnp.sum(jnp.sum(diff * diff, axis=-1, keepdims=True) * inv_d, axis=0, keepdims=True)
        dy = diff * inv_d
        dgf_ref[...] += jnp.sum(dy * hnf, axis=0, keepdims=True)
        dhnf = dy * gf
        dh2 = rsf * (dhnf - hnf * jnp.mean(dhnf * hnf, axis=-1, keepdims=True))
        dh2_b = dh2.astype(BF16)
        dh2_ref[...] = dh2_b
        dnm = jnp.zeros((tm, D_MODEL), F32)
        for j in range(FF_BLOCKS):
            cols = slice(j * FF_BLOCK, (j + 1) * FF_BLOCK)
            dpre_j = (_dot_nt(dh2_b, wff2_v[cols, :]) * (2.0 * relu[j])).astype(BF16)
            dpre_ref[:, cols] = dpre_j
            dnm = dnm + _dot_nt(dpre_j, wff1_v[j])
        dg2_ref[...] += jnp.sum(dnm * hn1, axis=0, keepdims=True)
        dhn1 = dnm * g2
        dh1 = dh2 + rs1 * (dhn1 - hn1 * jnp.mean(dhn1 * hn1, axis=-1, keepdims=True))
        dh1_ref[...] = dh1
        dmixed_ref[...] = _dot_nt(dh1.astype(BF16), wout_v[...])

    row = lambda w: pl.BlockSpec((tm, w), lambda i: (i, 0))
    vec = lambda w: pl.BlockSpec((1, w), lambda i: (0, 0))
    return pl.pallas_call(
        body,
        grid=(T // tm,),
        in_specs=[row(D_MODEL), row(GLA_WIDTH), row(DSA_WIDTH), row(D_MODEL), vec(D_MODEL), vec(D_MODEL), ANY, ANY, ANY],
        out_specs=[row(D_MODEL), row(D_MODEL), row(D_FF), row(D_FF), row(D_MODEL), row(D_MODEL), row(D_MODEL),
                   vec(1), vec(D_MODEL), vec(D_MODEL)],
        out_shape=[
            jax.ShapeDtypeStruct((T, D_MODEL), BF16),
            jax.ShapeDtypeStruct((T, D_MODEL), BF16),
            jax.ShapeDtypeStruct((T, D_FF), BF16),
            jax.ShapeDtypeStruct((T, D_FF), BF16),
            jax.ShapeDtypeStruct((T, D_MODEL), BF16),
            jax.ShapeDtypeStruct((T, D_MODEL), F32),
            jax.ShapeDtypeStruct((T, D_MODEL), F32),
            jax.ShapeDtypeStruct((1, 1), F32),
            jax.ShapeDtypeStruct((1, D_MODEL), F32),
            jax.ShapeDtypeStruct((1, D_MODEL), F32),
        ],
        scratch_shapes=[pltpu.VMEM((D_MODEL, D_MODEL), BF16), pltpu.VMEM((FF_BLOCKS, D_MODEL, FF_BLOCK), BF16),
                        pltpu.VMEM((D_FF, D_MODEL), BF16), pltpu.SemaphoreType.DMA((3,))],
        compiler_params=_params(),
        name="post_fused",
    )(x, oa, ob, tgt, g2, gf, wout, wff1, wff2)


WGRAD_TOKENS = 1024


def wgrad(a, b, name, bm=None, bn=None, col_blocked=False):
    T, M = a.shape
    N = b.shape[1]
    bm = M if bm is None else bm
    bn = N if bn is None else bn
    tk = min(WGRAD_TOKENS, T)

    def body(a_ref, b_ref, o_ref):
        part = _dot_tn(a_ref[...].astype(BF16), b_ref[...].astype(BF16))
        out = o_ref.at[0] if col_blocked else o_ref

        @pl.when(pl.program_id(2) == 0)
        def _():
            out[...] = part

        @pl.when(pl.program_id(2) > 0)
        def _():
            out[...] += part

    if col_blocked:
        assert bm == M
        out_spec = pl.BlockSpec((1, M, bn), lambda i, j, k: (j, 0, 0))
        out_shape = jax.ShapeDtypeStruct((N // bn, M, bn), F32)
    else:
        out_spec = pl.BlockSpec((bm, bn), lambda i, j, k: (i, j))
        out_shape = jax.ShapeDtypeStruct((M, N), F32)
    return pl.pallas_call(
        body,
        grid=(M // bm, N // bn, T // tk),
        in_specs=[pl.BlockSpec((tk, bm), lambda i, j, k: (k, i)), pl.BlockSpec((tk, bn), lambda i, j, k: (k, j))],
        out_specs=out_spec,
        out_shape=out_shape,
        compiler_params=_params(("arbitrary", "arbitrary", "arbitrary")),
        name=name,
    )(a, b)


def dx_final(x, dh1, g1, da, dq, dk, dv, wp):
    T = x.shape[0]
    tm = 256

    def body(x_ref, dh1_ref, g_ref, da_ref, dq_ref, dk_ref, dv_ref, w_hbm, dx_ref, dg_ref, w_vmem, sem):
        @pl.when(pl.program_id(0) == 0)
        def _():
            _load_once(w_hbm, w_vmem, sem)
            dg_ref[...] = jnp.zeros_like(dg_ref)

        dnx = (_dot_nt(da_ref[...], w_vmem[:, 0:P_A]) + _dot_nt(dq_ref[...], w_vmem[:, P_DQ:P_DQ + DSA_WIDTH])
               + _dot_nt(dk_ref[...], w_vmem[:, P_DK:P_DK + DSA_WIDTH]) + _dot_nt(dv_ref[...], w_vmem[:, P_DV:P_DV + DSA_WIDTH]))
        xf = x_ref[...]
        rs = _rstd(xf)
        hn = xf * rs
        dg_ref[...] += jnp.sum(dnx * hn, axis=0, keepdims=True)
        dhn = dnx * g_ref[...]
        dx_ref[...] = dh1_ref[...] + rs * (dhn - hn * jnp.mean(dhn * hn, axis=-1, keepdims=True))

    row = lambda w: pl.BlockSpec((tm, w), lambda i: (i, 0))
    vec = pl.BlockSpec((1, D_MODEL), lambda i: (0, 0))
    return pl.pallas_call(
        body,
        grid=(T // tm,),
        in_specs=[row(D_MODEL), row(D_MODEL), vec, row(P_A), row(DSA_WIDTH), row(DSA_WIDTH), row(DSA_WIDTH), ANY],
        out_specs=[row(D_MODEL), vec],
        out_shape=[jax.ShapeDtypeStruct((T, D_MODEL), F32), jax.ShapeDtypeStruct((1, D_MODEL), F32)],
        scratch_shapes=[pltpu.VMEM((D_MODEL, P_ALL), BF16), pltpu.SemaphoreType.DMA],
        compiler_params=_params(),
        name="dx_final",
    )(x, dh1, g1, da, dq, dk, dv, wp)


def adamw(w, g, m, v, name):
    R, C = w.shape
    br = 256 if R % 256 == 0 else R

    def body(w_ref, g_ref, m_ref, v_ref, d_ref, nm_ref, nv_ref):
        d_ref[...], nm_ref[...], nv_ref[...] = _adamw_math(w_ref[...], g_ref[...], m_ref[...], v_ref[...])

    spec = pl.BlockSpec((br, C), lambda i: (i, 0))
    return pl.pallas_call(
        body,
        grid=(R // br,),
        in_specs=[spec] * 4,
        out_specs=[spec] * 3,
        out_shape=[jax.ShapeDtypeStruct((R, C), F32)] * 3,
        compiler_params=_params(),
        name=name,
    )(w, g, m, v)


def _place():
    return lax.axis_index("x"), lax.axis_index("y"), lax.axis_index("c")


def _other_chips(x, y):
    return [(1 - x, y), (x, 1 - y), (1 - x, 1 - y)]


class Exchange:
    def __init__(self, kind, arrays):
        self.kind, self.arrays, self.n = kind, arrays, len(arrays)

    def out_shapes(self):
        if self.kind == "gather":
            return [jax.ShapeDtypeStruct((4,) + s.shape, s.dtype) for s in self.arrays]
        if self.kind == "scatter":
            return [jax.ShapeDtypeStruct((4, 2 * s.shape[1], s.shape[2]), s.dtype) for s in self.arrays]
        return [jax.ShapeDtypeStruct((4, s.shape[1] // 2, s.shape[2]), s.dtype) for s in self.arrays]

    def sems(self):
        return [pltpu.SemaphoreType.DMA((self.n, 7)), pltpu.SemaphoreType.DMA((self.n, 7))]

    def phases(self, ins, outs, send_sems, recv_sems):
        n, kind = self.n, self.kind
        x, y, c = _place()
        me, sib = (x, y, c), (x, y, 1 - c)
        mine = 2 * x + y
        chips = _other_chips(x, y)

        def region(a, owner, half):
            h = outs[a].shape[1] // 2
            return outs[a].at[owner, pl.ds(half * h, h)]

        def copy(a, k, owner, half, to, src=None):
            return pltpu.make_async_remote_copy(
                src_ref=region(a, owner, half) if src is None else src, dst_ref=region(a, owner, half),
                send_sem=send_sems.at[a, k], recv_sem=recv_sems.at[a, k], device_id=to, device_id_type=MESH)

        def swap_copy(a, to):
            h = ins[a].shape[1] // 2
            return pltpu.make_async_remote_copy(
                src_ref=ins[a].at[:, pl.ds((1 - c) * h, h)], dst_ref=outs[a],
                send_sem=send_sems.at[a, 0], recv_sem=recv_sems.at[a, 0], device_id=to, device_id_type=MESH)

        def first_copies():
            if kind == "swap":
                return [swap_copy(a, sib) for a in range(n)]
            cps = []
            for a in range(n):
                h = outs[a].shape[1] // 2
                for t, (cx, cy) in enumerate(chips):
                    src = ins[a].at[2 * cx + cy] if kind == "scatter" else ins[a].at[pl.ds(c * h, h)]
                    cps.append(copy(a, t, mine, c, (cx, cy, c), src=src))
                if kind == "scatter":
                    cps.append(copy(a, 6, mine, c, sib, src=ins[a].at[mine]))
            return cps

        def forward_copies():
            if kind == "swap":
                return []
            return [copy(a, 3 + t, 2 * cx + cy, c, sib) for a in range(n) for t, (cx, cy) in enumerate(chips)]

        def start():
            for cp in first_copies():
                cp.start()

        def forward():
            if kind == "swap":
                return
            fws = forward_copies()
            for a in range(n):
                for t, (cx, cy) in enumerate(chips):
                    copy(a, t, 2 * cx + cy, c, me).wait_recv()
                    fws[3 * a + t].start()

        def finish():
            for a in range(n):
                if kind == "swap":
                    swap_copy(a, me).wait_recv()
                    continue
                for t, (cx, cy) in enumerate(chips):
                    copy(a, 3 + t, 2 * cx + cy, 1 - c, me).wait_recv()
                if kind == "scatter":
                    copy(a, 6, mine, 1 - c, me).wait_recv()
            for cp in first_copies() + forward_copies():
                cp.wait_send()

        return start, forward, finish

    def fill_own(self, outs):
        if self.kind == "swap":
            return list(outs)
        x, y, c = _place()
        if self.kind == "gather":
            return [lax.dynamic_update_index_in_dim(o, s, 2 * x + y, 0) for o, s in zip(outs, self.arrays)]
        filled = []
        for o, s in zip(outs, self.arrays):
            own = lax.dynamic_index_in_dim(s, 2 * x + y, 0, keepdims=True)
            filled.append(lax.dynamic_update_slice(o, own, (2 * x + y, c * s.shape[1], 0)))
        return filled

    def run(self, name):
        n = self.n

        def body(*refs):
            start, forward, finish = self.phases(refs[:n], refs[n:2 * n], *refs[2 * n:])
            start()
            forward()
            finish()

        outs = pl.pallas_call(
            body, in_specs=[ANY] * n, out_specs=[ANY] * n, out_shape=self.out_shapes(), scratch_shapes=self.sems(), name=name,
        )(*self.arrays)
        return self.fill_own(outs)


def _host_exchange(exchange, refs, n_in, n_out, step, n_steps):
    if exchange is None:
        return refs
    n = exchange.n
    own_in, ex_in = refs[:n_in], refs[n_in:n_in + n]
    own_out, ex_out = refs[n_in + n:n_in + n + n_out], refs[n_in + n + n_out:n_in + 2 * n + n_out]
    rest = refs[n_in + 2 * n + n_out:]
    start, forward, finish = exchange.phases(ex_in, ex_out, rest[-2], rest[-1])
    pl.when(step == 0)(start)
    pl.when(step == (2 * n_steps) // 3)(forward)
    pl.when(step == n_steps - 1)(finish)
    return own_in + own_out + rest[:-2]


def _hosted_call(exchange, body, *, grid, in_specs, out_specs, out_shape, scratch_shapes, compiler_params, name, args):
    if exchange is None:
        return pl.pallas_call(body, grid=grid, in_specs=in_specs, out_specs=out_specs, out_shape=out_shape,
                              scratch_shapes=scratch_shapes, compiler_params=compiler_params, name=name)(*args)
    n = exchange.n
    res = pl.pallas_call(
        body, grid=grid, in_specs=list(in_specs) + [ANY] * n, out_specs=list(out_specs) + [ANY] * n,
        out_shape=list(out_shape) + exchange.out_shapes(), scratch_shapes=list(scratch_shapes) + exchange.sems(),
        compiler_params=compiler_params, name=name)(*args, *exchange.arrays)
    return list(res[:len(out_shape)]) + [exchange.fill_own(res[len(out_shape):])]


def add_halves(g, got, core, name):
    _, R, C = g.shape
    h = R // 2
    br = 128
    nb = h // br

    def body(core_ref, g_ref, got_ref, o_ref):
        o_ref[...] = (g_ref[...] + got_ref[...]).astype(BF16)

    return pl.pallas_call(
        body,
        grid_spec=pltpu.PrefetchScalarGridSpec(
            num_scalar_prefetch=1,
            grid=(4, nb),
            in_specs=[pl.BlockSpec((1, br, C), lambda s, i, core: (s, core[0] * nb + i, 0)),
                      pl.BlockSpec((1, br, C), lambda s, i, core: (s, i, 0))],
            out_specs=pl.BlockSpec((1, br, C), lambda s, i, core: (s, i, 0)),
        ),
        out_shape=jax.ShapeDtypeStruct((4, h, C), BF16),
        compiler_params=_params(("arbitrary", "arbitrary")),
        name=name,
    )(core, g, got)


def sum_slots(parts, name):
    S, R, C = parts.shape
    br = 128 if R % 128 == 0 else R

    def body(p_ref, o_ref):
        acc = p_ref[0].astype(F32)
        for s in range(1, S):
            acc = acc + p_ref[s].astype(F32)
        o_ref[...] = acc

    return pl.pallas_call(
        body,
        grid=(R // br,),
        in_specs=[pl.BlockSpec((S, br, C), lambda i: (0, i, 0))],
        out_specs=pl.BlockSpec((br, C), lambda i: (i, 0)),
        out_shape=jax.ShapeDtypeStruct((R, C), F32),
        compiler_params=_params(),
        name=name,
    )(parts)


def _adamw_math(w, g, m, v):
    m_new = ADAM_B1 * m + (1.0 - ADAM_B1) * g
    v_new = ADAM_B2 * v + (1.0 - ADAM_B2) * (g * g)
    m_hat = m_new / (1.0 - ADAM_B1 ** ADAM_STEP)
    v_hat = v_new / (1.0 - ADAM_B2 ** ADAM_STEP)
    return -ADAM_LR * (m_hat / (jnp.sqrt(v_hat) + ADAM_EPS) + ADAM_WD * w), m_new, v_new


def reduce_adamw(slots, w, m, v, name):
    S, R, C = slots.shape
    br = 128

    def body(p_ref, w_ref, m_ref, v_ref, g_ref, d_ref, nm_ref, nv_ref):
        g = p_ref[0].astype(F32)
        for s in range(1, S):
            g = g + p_ref[s].astype(F32)
        g_ref[...] = g
        d_ref[...], nm_ref[...], nv_ref[...] = _adamw_math(w_ref[...], g, m_ref[...], v_ref[...])

    spec = pl.BlockSpec((br, C), lambda i: (i, 0))
    return pl.pallas_call(
        body,
        grid=(R // br,),
        in_specs=[pl.BlockSpec((S, br, C), lambda i: (0, i, 0)), spec, spec, spec],
        out_specs=[spec] * 4,
        out_shape=[jax.ShapeDtypeStruct((R, C), F32)] * 4,
        compiler_params=_params(),
        name=name,
    )(slots, w, m, v)


SMALL_ROWS = 64


def gather_small(vec):
    def body(v_ref, o_ref, send_sems, recv_sems, local_sem):
        x, y, c = _place()
        flips = [(fx, fy, fc) for fx in (0, 1) for fy in (0, 1) for fc in (0, 1)][1:]

        def peer(f):
            return (1 - x if f[0] else x, 1 - y if f[1] else y, 1 - c if f[2] else c)

        slot = lambda p: 4 * p[0] + 2 * p[1] + p[2]
        own = pltpu.make_async_copy(v_ref, o_ref.at[slot((x, y, c))], local_sem)
        own.start()
        cps = [pltpu.make_async_remote_copy(
            src_ref=v_ref, dst_ref=o_ref.at[slot((x, y, c))], send_sem=send_sems.at[k], recv_sem=recv_sems.at[k],
            device_id=peer(f), device_id_type=MESH) for k, f in enumerate(flips)]
        for cp in cps:
            cp.start()
        for k, f in enumerate(flips):
            pltpu.make_async_remote_copy(
                src_ref=v_ref, dst_ref=o_ref.at[slot(peer(f))], send_sem=send_sems.at[k], recv_sem=recv_sems.at[k],
                device_id=(x, y, c), device_id_type=MESH).wait_recv()
        for cp in cps:
            cp.wait_send()
        own.wait()

    return pl.pallas_call(
        body,
        in_specs=[ANY],
        out_specs=ANY,
        out_shape=jax.ShapeDtypeStruct((8,) + vec.shape, vec.dtype),
        scratch_shapes=[pltpu.SemaphoreType.DMA((7,)), pltpu.SemaphoreType.DMA((7,)), pltpu.SemaphoreType.DMA],
        name="gather_small",
    )(vec)


GLOW_PAD = LANE - GLA_RANK


def kernel(x, attn_norm_g, w_in, gla_gate_w2, gla_gate_b, gla_norm_g, rel_bias, w_out, mlp_norm_g, w_ff1, w_ff2, final_norm_g, loss_target, m_attn_norm_g, m_w_in, m_gla_gate_w2, m_gla_gate_b, m_gla_norm_g, m_rel_bias, m_w_out, m_mlp_norm_g, m_w_ff1, m_w_ff2, m_final_norm_g, v_attn_norm_g, v_w_in, v_gla_gate_w2, v_gla_gate_b, v_gla_norm_g, v_rel_bias, v_w_out, v_mlp_norm_g, v_w_ff1, v_w_ff2, v_final_norm_g):
    xs, tgt = x[0], loss_target[0]
    T = xs.shape[0]
    cx, cy, cc = _place()
    chip = 2 * cx + cy
    gf = final_norm_g.reshape(1, D_MODEL)

    win_g, w2_g = Exchange("gather", [w_in[0].astype(BF16), gla_gate_w2[0]]).run("gather_w_in")
    win = jnp.transpose(win_g, (1, 0, 2)).reshape(D_MODEL, D_IN)
    n_glow = R_GLOW + GLA_RANK
    wp = jnp.concatenate([win[:, :n_glow], jnp.zeros((D_MODEL, GLOW_PAD), BF16), win[:, n_glow:]], axis=1)
    w2 = jnp.transpose(w2_g, (1, 0, 2)).reshape(GLA_RANK, GLA_QK)
    w2p = jnp.concatenate([w2, jnp.zeros((GLOW_PAD, GLA_QK), F32)], axis=0)

    proj, nx = inproj(xs, attn_norm_g, wp)
    tab = bias_tables(rel_bias)
    ob, lse, (wout_g, wff1, wff2_g) = dsa_fwd(
        proj, tab, Exchange("gather", [w_out[0].astype(BF16), w_ff1[0].astype(BF16), w_ff2[0].astype(BF16)]))
    wout = wout_g.reshape(D_MODEL, D_MODEL)
    wff2 = wff2_g.reshape(D_FF, D_MODEL)
    oa, opre, sprev = gla_fwd(proj, w2p, gla_gate_b, gla_norm_g)
    mixed, nm, act, dpre, dh2, dh1, dmixed, loss, dgf, dg2 = post_fused(xs, oa, ob, tgt, mlp_norm_g, gf, wout, wff1, wff2)

    core = cc.astype(jnp.int32).reshape(1)
    late = [
        wgrad(mixed, dh1, "wgrad_out").reshape(4, D_MODEL // 4, D_MODEL),
        wgrad(nm, dpre, "wgrad_ff1", bn=FF_BLOCK, col_blocked=True),
        wgrad(act, dh2, "wgrad_ff2", bm=FF_BLOCK).reshape(4, FF_BLOCK, D_MODEL),
    ]
    late_names = ["w_out", "w_ff1", "w_ff2"]
    da, dw2p, dbg, dgn, got = gla_bwd(proj, w2p, gla_gate_b, gla_norm_g, opre, sprev, dmixed, Exchange("swap", late))
    sums = [add_halves(g, r, core, "add_halves_" + s) for g, r, s in zip(late, got, late_names)]
    dq, dk, dv, dtab, late_slots = dsa_bwd(proj, tab, ob, lse, dmixed, Exchange("scatter", sums))
    slots = dict(zip(late_names, late_slots))
    drel = bias_tables_bwd(dtab)
    dxs, dg1 = dx_final(xs, dh1, attn_norm_g, da, dq, dk, dv, wp)

    dwa = wgrad(nx, da, "wgrad_in_gla")
    dwq = wgrad(nx, dq, "wgrad_in_q")
    dwk = wgrad(nx, dk, "wgrad_in_k")
    dwv = wgrad(nx, dv, "wgrad_in_v")
    dwin = jnp.concatenate([dwa[:, :n_glow], dwq, dwk, dwv], axis=1)
    dwin = [jnp.transpose(dwin.reshape(D_MODEL, 4, D_IN // 4), (1, 0, 2))]
    got = Exchange("swap", dwin).run("swap_w_in")
    sums = [add_halves(dwin[0], got[0], core, "add_halves_w_in")]
    slots["w_in"] = Exchange("scatter", sums).run("scatter_w_in")[0]

    small = jnp.concatenate([dg1.reshape(-1), dbg.reshape(-1), dgn.reshape(-1), drel.reshape(-1), dg2.reshape(-1),
                             dgf.reshape(-1), dw2p[:GLA_RANK].reshape(-1)]).reshape(SMALL_ROWS, LANE)
    tot = sum_slots(gather_small(small), "sum_small").reshape(-1)
    sizes = [D_MODEL, GLA_QK, GLA_WIDTH, REL_BUCKETS * DSA_HEADS, D_MODEL, D_MODEL, GLA_RANK * GLA_QK]
    offs = np.concatenate([[0], np.cumsum(sizes)])
    piece = lambda i: tot[int(offs[i]):int(offs[i + 1])]
    g_g1 = piece(0).reshape(1, D_MODEL)
    g_bg = piece(1).reshape(1, GLA_QK)
    g_gn = piece(2).reshape(1, GLA_WIDTH)
    g_rel = piece(3).reshape(REL_BUCKETS, DSA_HEADS)
    g_g2 = piece(4).reshape(1, D_MODEL)
    g_gf = piece(5).reshape(1, D_MODEL)
    g_w2 = lax.dynamic_slice_in_dim(piece(6).reshape(GLA_RANK, GLA_QK), chip * (GLA_QK // 4), GLA_QK // 4, axis=1)

    loss_all = lax.psum(loss[0, 0], ("x", "y", "c"))

    upd = [
        ("attn_norm_g", attn_norm_g, g_g1, m_attn_norm_g, v_attn_norm_g),
        ("w_in", w_in[0], None, m_w_in[0], v_w_in[0]),
        ("gla_gate_w2", gla_gate_w2[0], g_w2, m_gla_gate_w2[0], v_gla_gate_w2[0]),
        ("gla_gate_b", gla_gate_b, g_bg, m_gla_gate_b, v_gla_gate_b),
        ("gla_norm_g", gla_norm_g, g_gn, m_gla_norm_g, v_gla_norm_g),
        ("rel_bias", rel_bias, g_rel, m_rel_bias, v_rel_bias),
        ("w_out", w_out[0], None, m_w_out[0], v_w_out[0]),
        ("mlp_norm_g", mlp_norm_g, g_g2, m_mlp_norm_g, v_mlp_norm_g),
        ("w_ff1", w_ff1[0], None, m_w_ff1[0], v_w_ff1[0]),
        ("w_ff2", w_ff2[0], None, m_w_ff2[0], v_w_ff2[0]),
        ("final_norm_g", gf, g_gf, m_final_norm_g.reshape(1, D_MODEL), v_final_norm_g.reshape(1, D_MODEL)),
    ]
    shapes = [attn_norm_g.shape, w_in.shape, gla_gate_w2.shape, gla_gate_b.shape, gla_norm_g.shape, rel_bias.shape,
              w_out.shape, mlp_norm_g.shape, w_ff1.shape, w_ff2.shape, final_norm_g.shape]
    grads, deltas, new_m, new_v = [], [], [], []
    for (name, w, g, m, v), shape in zip(upd, shapes):
        if name in slots:
            g, d, nm_, nv_ = reduce_adamw(slots[name], w, m, v, "reduce_adamw_" + name)
        else:
            d, nm_, nv_ = adamw(w, g, m, v, "adamw_" + name)
        grads.append(g.reshape(shape))
        deltas.append(d.reshape(shape))
        new_m.append(nm_.reshape(shape))
        new_v.append(nv_.reshape(shape))
    return (loss_all, dxs.reshape(1, T, D_MODEL), *grads, *deltas, *new_m, *new_v)
```

```python
import functools
import math

import jax
import jax.numpy as jnp
import numpy as np
from jax import lax
from jax.experimental import pallas as pl
from jax.experimental.pallas import tpu as pltpu

F32 = jnp.float32
BF16 = jnp.bfloat16
MESH = pl.DeviceIdType.MESH

D_MODEL = 1024
GLA_WIDTH = 512
GLA_HEADS = 4
GLA_DK = 64
GLA_DV = 128
GLA_QK = GLA_HEADS * GLA_DK
GLA_RANK = 16
GLA_TAU = 16.0
GLA_CHUNK = 64
DSA_WIDTH = 512
DSA_HEADS = 8
DSA_DH = 64
DSA_DILATIONS = (1, 4, 16)
DSA_SPAN = 128
DSA_BLOCK = 128
DSA_SUPER = DSA_BLOCK * DSA_DILATIONS[-1]
REL_BUCKETS = 32
REL_MAX_DIST = 2048
D_FF = 4096
D_IN = 3088
EPS = 1e-6
NEG = -1e30
QK_SCALE = 0.125

ADAM_LR = 0.001
ADAM_B1 = 0.9
ADAM_B2 = 0.999
ADAM_EPS = 1e-08
ADAM_WD = 0.01
ADAM_STEP = 10

LANE = 128
P_GQ, P_GK, P_GV, P_GR = 0, 256, 512, 1024
P_GLOW = 1536
P_A = 1664
P_DQ, P_DK, P_DV = 1664, 2176, 2688
P_ALL = 3200
R_GQ, R_GK, R_GV, R_GR, R_GLOW, R_DQ, R_DK, R_DV = 0, 256, 512, 1024, 1536, 1552, 2064, 2576

VMEM_LIMIT = 56 * 1024 * 1024


def _params(sem=("arbitrary",), vmem=VMEM_LIMIT):
    return pltpu.CompilerParams(dimension_semantics=sem, vmem_limit_bytes=vmem)


def _dot(a, b):
    return jnp.dot(a, b, preferred_element_type=F32)


def _dot_nt(a, b):
    return lax.dot_general(a, b, (((1,), (1,)), ((), ())), preferred_element_type=F32)


def _dot_tn(a, b):
    return lax.dot_general(a, b, (((0,), (0,)), ((), ())), preferred_element_type=F32)


def _split3(x):
    x1 = x.astype(BF16)
    r1 = x - x1.astype(F32)
    x2 = r1.astype(BF16)
    x3 = (r1 - x2.astype(F32)).astype(BF16)
    return x1, x2, x3


def _dot_exact_lhs(m_bf16, x):
    x1, x2, x3 = _split3(x)
    return _dot(m_bf16, x1) + _dot(m_bf16, x2) + _dot(m_bf16, x3)


def _rstd(xf):
    return lax.rsqrt(jnp.mean(xf * xf, axis=-1, keepdims=True) + EPS)


def _load_once(hbm_ref, vmem_ref, sem):
    cp = pltpu.make_async_copy(hbm_ref, vmem_ref, sem)
    cp.start()
    cp.wait()


ANY = pl.BlockSpec(memory_space=pl.ANY)


def inproj(x, g1, wp):
    T = x.shape[0]
    tm = 256

    def body(x_ref, g_ref, w_hbm, proj_ref, nx_ref, w_vmem, sem):
        @pl.when(pl.program_id(0) == 0)
        def _():
            _load_once(w_hbm, w_vmem, sem)

        xf = x_ref[...]
        nx = ((xf * _rstd(xf)) * g_ref[...]).astype(BF16)
        nx_ref[...] = nx
        proj_ref[...] = _dot(nx, w_vmem[...])

    return pl.pallas_call(
        body,
        grid=(T // tm,),
        in_specs=[pl.BlockSpec((tm, D_MODEL), lambda i: (i, 0)), pl.BlockSpec((1, D_MODEL), lambda i: (0, 0)), ANY],
        out_specs=[pl.BlockSpec((tm, P_ALL), lambda i: (i, 0)), pl.BlockSpec((tm, D_MODEL), lambda i: (i, 0))],
        out_shape=[jax.ShapeDtypeStruct((T, P_ALL), F32), jax.ShapeDtypeStruct((T, D_MODEL), BF16)],
        scratch_shapes=[pltpu.VMEM((D_MODEL, P_ALL), BF16), pltpu.SemaphoreType.DMA],
        compiler_params=_params(),
        name="inproj",
    )(x, g1, wp)


GLA_CHUNKS_PER_STEP = 8
GLA_ROWS = GLA_CHUNK * GLA_CHUNKS_PER_STEP


def _gla_masks():
    lane = lax.broadcasted_iota(jnp.int32, (1, GLA_QK), 1)
    return [(lane >= h * GLA_DK) & (lane < (h + 1) * GLA_DK) for h in range(GLA_HEADS)]


def _log_sigmoid(x):
    return jnp.minimum(x, 0.0) - jnp.log(1.0 + jnp.exp(-jnp.abs(x)))


def _sigmoid(x):
    return 1.0 / (1.0 + jnp.exp(-x))


def _head_cols(h):
    return slice(h * GLA_DV, (h + 1) * GLA_DV)


GLA_GROUP = 256


def _gla_step_constants():
    ri = lax.broadcasted_iota(jnp.int32, (GLA_GROUP, GLA_GROUP), 0)
    ci = lax.broadcasted_iota(jnp.int32, (GLA_GROUP, GLA_GROUP), 1)
    shift = GLA_CHUNK.bit_length() - 1
    same = lax.shift_right_logical(ri, shift) == lax.shift_right_logical(ci, shift)
    return same & (ri >= ci), same & (ri <= ci), _gla_masks()


def _by_group(fn, *arrays):
    outs = [fn(*[a[g * GLA_GROUP:(g + 1) * GLA_GROUP] for a in arrays]) for g in range(GLA_ROWS // GLA_GROUP)]
    if isinstance(outs[0], tuple):
        return tuple(jnp.concatenate(parts, axis=0) for parts in zip(*outs))
    return jnp.concatenate(outs, axis=0)


def _per_chunk(x):
    return x.reshape(GLA_CHUNKS_PER_STEP, GLA_CHUNK, x.shape[-1])


def _chunk_rows_of(x, c):
    return x[c * GLA_CHUNK:(c + 1) * GLA_CHUNK]


def _stack_masked(x, masks):
    return jnp.concatenate([jnp.where(m, x, 0.0) for m in masks], axis=0)


def _stack_head_cols(x):
    return jnp.concatenate([x[:, _head_cols(h)] for h in range(GLA_HEADS)], axis=0)


def _diag_blocks(full, masks):
    out = jnp.where(masks[0], full[:GLA_DV], 0.0)
    for h in range(1, GLA_HEADS):
        out = out + jnp.where(masks[h], full[h * GLA_DV:(h + 1) * GLA_DV], 0.0)
    return out


def _row_blocks_masked(full, masks):
    out = jnp.where(masks[0], full[:GLA_CHUNK], 0.0)
    for h in range(1, GLA_HEADS):
        out = out + jnp.where(masks[h], full[h * GLA_CHUNK:(h + 1) * GLA_CHUNK], 0.0)
    return out


def _gla_step_common(q, k, glow_b, w2, bg, tri):
    gpre = _dot(glow_b, w2) + bg
    glog = _log_sigmoid(gpre) / GLA_TAU
    b = _by_group(lambda g: _dot_exact_lhs(tri, g), glog)
    bl = jnp.sum(_per_chunk(glog), axis=1, keepdims=True)
    eb = jnp.exp(b)
    enb = jnp.exp(-b)
    eke = jnp.exp(jnp.broadcast_to(bl, (GLA_CHUNKS_PER_STEP, GLA_CHUNK, GLA_QK)).reshape(GLA_ROWS, GLA_QK) - b)
    return gpre, eb, enb, eke, jnp.exp(bl), (q * QK_SCALE) * eb, k * enb, k * eke


def gla_fwd(proj, w2p, bg, gn):
    T = proj.shape[0]
    n_steps = T // GLA_ROWS
    n_chunks = T // GLA_CHUNK

    def body(proj_ref, w2_ref, bg_ref, gn_ref, oa_ref, opre_ref, sprev_ref, st_ref):
        @pl.when(pl.program_id(0) == 0)
        def _():
            st_ref[...] = jnp.zeros_like(st_ref)

        causal, _, masks = _gla_step_constants()
        q = proj_ref[:, P_GQ:P_GQ + GLA_QK]
        k = proj_ref[:, P_GK:P_GK + GLA_QK]
        v = proj_ref[:, P_GV:P_GV + GLA_WIDTH]
        r = proj_ref[:, P_GR:P_GR + GLA_WIDTH]
        glow = proj_ref[:, P_GLOW:P_GLOW + LANE].astype(BF16)
        _, _, _, _, ebl, qd, ki, ke = _gla_step_common(q, k, glow, w2_ref[...].astype(BF16), bg_ref[...], causal.astype(BF16))
        ki_b = ki.astype(BF16)
        v_b = v.astype(BF16)
        o_heads = []
        for h in range(GLA_HEADS):
            def intra(qd_g, ki_g, v_g):
                att = jnp.where(causal, _dot_nt(qd_g, ki_g), 0.0)
                return _dot(att.astype(BF16), v_g)

            o_heads.append(_by_group(intra, jnp.where(masks[h], qd, 0.0).astype(BF16), ki_b, v_b[:, _head_cols(h)]))
        st = st_ref[...]
        states = []
        for c in range(GLA_CHUNKS_PER_STEP):
            states.append(st)
            sprev_ref[c] = st
            inc = _diag_blocks(_dot_tn(_chunk_rows_of(v_b, c), _chunk_rows_of(ke, c).astype(BF16)), masks)
            st = st * ebl[c] + inc
        st_ref[...] = st
        inter = []
        for c in range(GLA_CHUNKS_PER_STEP):
            qd_c = _stack_masked(_chunk_rows_of(qd, c), masks).astype(BF16)
            got = _dot_nt(qd_c, states[c].astype(BF16))
            inter.append(jnp.concatenate([got[h * GLA_CHUNK:(h + 1) * GLA_CHUNK] for h in range(GLA_HEADS)], axis=1))
        o = jnp.concatenate(o_heads, axis=1) + jnp.concatenate(inter, axis=0)
        opre_ref[...] = o
        on = jnp.concatenate([o[:, _head_cols(h)] * _rstd(o[:, _head_cols(h)]) for h in range(GLA_HEADS)], axis=1)
        oa_ref[...] = ((on * gn_ref[...]) * (r * _sigmoid(r))).astype(BF16)

    return pl.pallas_call(
        body,
        grid=(n_steps,),
        in_specs=[
            pl.BlockSpec((GLA_ROWS, P_ALL), lambda i: (i, 0)),
            pl.BlockSpec((LANE, GLA_QK), lambda i: (0, 0)),
            pl.BlockSpec((1, GLA_QK), lambda i: (0, 0)),
            pl.BlockSpec((1, GLA_WIDTH), lambda i: (0, 0)),
        ],
        out_specs=[
            pl.BlockSpec((GLA_ROWS, GLA_WIDTH), lambda i: (i, 0)),
            pl.BlockSpec((GLA_ROWS, GLA_WIDTH), lambda i: (i, 0)),
            pl.BlockSpec((GLA_CHUNKS_PER_STEP, GLA_DV, GLA_QK), lambda i: (i, 0, 0)),
        ],
        out_shape=[
            jax.ShapeDtypeStruct((T, GLA_WIDTH), BF16),
            jax.ShapeDtypeStruct((T, GLA_WIDTH), F32),
            jax.ShapeDtypeStruct((n_chunks, GLA_DV, GLA_QK), F32),
        ],
        scratch_shapes=[pltpu.VMEM((GLA_DV, GLA_QK), F32)],
        compiler_params=_params(),
        name="gla_fwd",
    )(proj, w2p, bg, gn)


def gla_bwd(proj, w2p, bg, gn, opre, sprev, dmixed, exchange=None):
    T = proj.shape[0]
    n_steps = T // GLA_ROWS

    def body(*refs):
        refs = _host_exchange(exchange, refs, 7, 4, pl.program_id(0), n_steps)
        proj_ref, w2_ref, bg_ref, gn_ref, opre_ref, sprev_ref, doa_ref, da_ref, dw2_ref, dbg_ref, dgn_ref, dst_ref = refs

        @pl.when(pl.program_id(0) == 0)
        def _():
            dst_ref[...] = jnp.zeros_like(dst_ref)
            dw2_ref[...] = jnp.zeros_like(dw2_ref)
            dbg_ref[...] = jnp.zeros_like(dbg_ref)
            dgn_ref[...] = jnp.zeros_like(dgn_ref)

        causal, causal_t, masks = _gla_step_constants()
        w2 = w2_ref[...].astype(BF16)
        gn = gn_ref[...]
        q = proj_ref[:, P_GQ:P_GQ + GLA_QK]
        k = proj_ref[:, P_GK:P_GK + GLA_QK]
        v_b = proj_ref[:, P_GV:P_GV + GLA_WIDTH].astype(BF16)
        r = proj_ref[:, P_GR:P_GR + GLA_WIDTH]
        glow = proj_ref[:, P_GLOW:P_GLOW + LANE].astype(BF16)
        o = opre_ref[...]
        doa = doa_ref[...]
        gpre, eb, enb, eke, ebl, qd, ki, ke = _gla_step_common(q, k, glow, w2, bg_ref[...], causal.astype(BF16))
        sig = _sigmoid(r)
        rs = jnp.concatenate([jnp.broadcast_to(_rstd(o[:, _head_cols(h)]), (GLA_ROWS, GLA_DV)) for h in range(GLA_HEADS)], axis=1)
        on = o * rs
        d_ong = doa * (r * sig)
        dr = doa * (on * gn) * (sig * (1.0 + r * (1.0 - sig)))
        dgn_ref[...] += jnp.sum(d_ong * on, axis=0, keepdims=True)
        d_on = d_ong * gn
        t = d_on * on
        mean_t = jnp.concatenate([jnp.broadcast_to(jnp.mean(t[:, _head_cols(h)], axis=-1, keepdims=True), (GLA_ROWS, GLA_DV))
                                  for h in range(GLA_HEADS)], axis=1)
        do_b = (rs * (d_on - on * mean_t)).astype(BF16)
        ki_b = ki.astype(BF16)
        ke_b = ke.astype(BF16)
        dqd = jnp.zeros_like(qd)
        dki = jnp.zeros_like(qd)
        dv_heads = []
        for h in range(GLA_HEADS):
            qd_h = jnp.where(masks[h], qd, 0.0).astype(BF16)
            do_h = do_b[:, _head_cols(h)]

            def intra(qd_g, ki_g, v_g, do_g):
                att = jnp.where(causal, _dot_nt(qd_g, ki_g), 0.0).astype(BF16)
                d_att = jnp.where(causal, _dot_nt(do_g, v_g), 0.0).astype(BF16)
                return _dot_tn(att, do_g), _dot(d_att, ki_g), _dot_tn(d_att, qd_g)

            dv_h, dqd_h, dki_h = _by_group(intra, qd_h, ki_b, v_b[:, _head_cols(h)], do_h)
            dv_heads.append(dv_h)
            dqd = dqd + jnp.where(masks[h], dqd_h, 0.0)
            dki = dki + dki_h
        states = [sprev_ref[c] for c in range(GLA_CHUNKS_PER_STEP)]
        dqd_inter, dst_adds = [], []
        for c in range(GLA_CHUNKS_PER_STEP):
            do_c = _stack_head_cols(_chunk_rows_of(do_b, c))
            dqd_inter.append(_row_blocks_masked(_dot(do_c, states[c].astype(BF16)), masks))
            dst_adds.append(_diag_blocks(_dot_tn(_chunk_rows_of(do_b, c), _chunk_rows_of(qd, c).astype(BF16)), masks))
        dst = dst_ref[...]
        dsts, debls = [None] * GLA_CHUNKS_PER_STEP, [None] * GLA_CHUNKS_PER_STEP
        for c in reversed(range(GLA_CHUNKS_PER_STEP)):
            dsts[c] = dst
            debls[c] = jnp.sum(dst * states[c], axis=0, keepdims=True)
            dst = dst * ebl[c] + dst_adds[c]
        dst_ref[...] = dst
        dv_inter, dke = [], []
        for c in range(GLA_CHUNKS_PER_STEP):
            dst_b = dsts[c].astype(BF16)
            got = _dot_nt(_stack_masked(_chunk_rows_of(ke, c), masks).astype(BF16), dst_b)
            dv_inter.append(jnp.concatenate([got[h * GLA_CHUNK:(h + 1) * GLA_CHUNK] for h in range(GLA_HEADS)], axis=1))
            dke.append(_row_blocks_masked(_dot(_stack_head_cols(_chunk_rows_of(v_b, c)), dst_b), masks))
        dqd = dqd + jnp.concatenate(dqd_inter, axis=0)
        dke = jnp.concatenate(dke, axis=0)
        dv = jnp.concatenate(dv_heads, axis=1) + jnp.concatenate(dv_inter, axis=0)
        dkk = dke * ke
        dbl = jnp.sum(_per_chunk(dkk), axis=1, keepdims=True) + jnp.stack(debls) * ebl
        last_row = lax.broadcasted_iota(jnp.int32, (GLA_CHUNKS_PER_STEP, GLA_CHUNK, GLA_QK), 1) == GLA_CHUNK - 1
        db = dqd * qd - dki * ki - dkk + jnp.where(last_row, dbl, 0.0).reshape(GLA_ROWS, GLA_QK)
        tri_t = causal_t.astype(BF16)
        dglog = _by_group(lambda g: _dot_exact_lhs(tri_t, g), db)
        dgpre = (dglog / GLA_TAU) * _sigmoid(-gpre)
        dgpre_b = dgpre.astype(BF16)
        da_ref[...] = jnp.concatenate(
            [dqd * eb * QK_SCALE, dki * enb + dke * eke, dv, dr, _dot_nt(dgpre_b, w2)], axis=1).astype(BF16)
        dw2_ref[...] += _dot_tn(glow, dgpre_b)
        dbg_ref[...] += jnp.sum(dgpre, axis=0, keepdims=True)

    rev = lambda i: (n_steps - 1 - i, 0)
    return _hosted_call(
        exchange,
        body,
        grid=(n_steps,),
        in_specs=[
            pl.BlockSpec((GLA_ROWS, P_ALL), rev),
            pl.BlockSpec((LANE, GLA_QK), lambda i: (0, 0)),
            pl.BlockSpec((1, GLA_QK), lambda i: (0, 0)),
            pl.BlockSpec((1, GLA_WIDTH), lambda i: (0, 0)),
            pl.BlockSpec((GLA_ROWS, GLA_WIDTH), rev),
            pl.BlockSpec((GLA_CHUNKS_PER_STEP, GLA_DV, GLA_QK), lambda i: (n_steps - 1 - i, 0, 0)),
            pl.BlockSpec((GLA_ROWS, GLA_WIDTH), rev),
        ],
        out_specs=[
            pl.BlockSpec((GLA_ROWS, P_A), rev),
            pl.BlockSpec((LANE, GLA_QK), lambda i: (0, 0)),
            pl.BlockSpec((1, GLA_QK), lambda i: (0, 0)),
            pl.BlockSpec((1, GLA_WIDTH), lambda i: (0, 0)),
        ],
        out_shape=[
            jax.ShapeDtypeStruct((T, P_A), BF16),
            jax.ShapeDtypeStruct((LANE, GLA_QK), F32),
            jax.ShapeDtypeStruct((1, GLA_QK), F32),
            jax.ShapeDtypeStruct((1, GLA_WIDTH), F32),
        ],
        scratch_shapes=[pltpu.VMEM((GLA_DV, GLA_QK), F32)],
        compiler_params=_params(),
        name="gla_bwd",
        args=(proj, w2p, bg, gn, opre, sprev, dmixed),
    )


def _t5_bucket(dist):
    max_exact = REL_BUCKETS // 2
    n = np.maximum(dist, 0)
    large = max_exact + (np.log(np.maximum(n, 1) / max_exact) / math.log(REL_MAX_DIST / max_exact)
                         * (REL_BUCKETS - max_exact)).astype(np.int32)
    large = np.minimum(large, REL_BUCKETS - 1)
    return np.where(n < max_exact, n, large).astype(np.int32)


def _bucket_ids():
    L = DSA_BLOCK
    steps = L + np.arange(L)[:, None] - np.arange(2 * L)[None, :]
    in_band = (steps >= 0) & (steps <= DSA_SPAN)
    return np.stack([np.where(in_band, _t5_bucket(steps * d), -1) for d in DSA_DILATIONS]).astype(np.int32)


def bias_tables(rel_bias):
    ids = jnp.asarray(_bucket_ids())
    nd = len(DSA_DILATIONS)

    def body(rel_ref, ids_ref, tab_ref):
        h = pl.program_id(1)
        idt = ids_ref[0]
        acc = jnp.where(idt < 0, NEG, 0.0).astype(F32)
        for b in range(REL_BUCKETS):
            acc = jnp.where(idt == b, rel_ref[b, h], acc)
        tab_ref[0, 0] = acc

    return pl.pallas_call(
        body,
        grid=(nd, DSA_HEADS),
        in_specs=[pl.BlockSpec(memory_space=pltpu.SMEM), pl.BlockSpec((1, DSA_BLOCK, 2 * DSA_BLOCK), lambda d, h: (d, 0, 0))],
        out_specs=pl.BlockSpec((1, 1, DSA_BLOCK, 2 * DSA_BLOCK), lambda d, h: (d, h, 0, 0)),
        out_shape=jax.ShapeDtypeStruct((nd, DSA_HEADS, DSA_BLOCK, 2 * DSA_BLOCK), F32),
        compiler_params=_params(("arbitrary", "arbitrary")),
        name="bias_tables",
    )(rel_bias, ids)


def bias_tables_bwd(dtab):
    ids = jnp.asarray(_bucket_ids())
    nd = len(DSA_DILATIONS)

    def body(dtab_ref, ids_ref, drel_ref):
        @pl.when((pl.program_id(0) == 0) & (pl.program_id(1) == 0))
        def _():
            for b in range(REL_BUCKETS):
                for h in range(DSA_HEADS):
                    drel_ref[b, h] = 0.0

        h = pl.program_id(1)
        idt = ids_ref[0]
        g = dtab_ref[0, 0]
        for b in range(REL_BUCKETS):
            drel_ref[b, h] += jnp.sum(jnp.where(idt == b, g, 0.0))

    return pl.pallas_call(
        body,
        grid=(nd, DSA_HEADS),
        in_specs=[pl.BlockSpec((1, 1, DSA_BLOCK, 2 * DSA_BLOCK), lambda d, h: (d, h, 0, 0)),
                  pl.BlockSpec((1, DSA_BLOCK, 2 * DSA_BLOCK), lambda d, h: (d, 0, 0))],
        out_specs=pl.BlockSpec(memory_space=pltpu.SMEM),
        out_shape=jax.ShapeDtypeStruct((REL_BUCKETS, DSA_HEADS), F32),
        compiler_params=_params(("arbitrary", "arbitrary")),
        name="bias_tables_bwd",
    )(dtab, ids)


DSA_PAIRS = DSA_HEADS // 2
DSA_UNROLL = 8
DSA_COMBINE_ROWS = 256


def _dsa_units(d):
    return d, DSA_SUPER // (DSA_BLOCK * d)


def _dsa_specs(T):
    nsb = T // DSA_SUPER
    qcol, kcol, vcol = P_DQ // LANE, P_DK // LANE, P_DV // LANE
    return nsb, qcol, kcol, vcol


def _head_lane_mask():
    return lax.broadcasted_iota(jnp.int32, (1, LANE), 1) < DSA_DH


def _first_block_penalty(first):
    col = lax.broadcasted_iota(jnp.int32, (2 * DSA_BLOCK, 2 * DSA_BLOCK), 1)
    return jnp.where(first & (col < DSA_BLOCK), NEG, 0.0).astype(F32)


def _pair_tiles(tab):
    return tab.reshape(len(DSA_DILATIONS), DSA_PAIRS, 2 * DSA_BLOCK, 2 * DSA_BLOCK)


def _stack_heads(t, head0):
    return jnp.concatenate([jnp.where(head0, t, 0.0), jnp.where(head0, 0.0, t)], axis=0)


def dsa_fwd(proj, tab, exchange=None):
    T = proj.shape[0]
    nsb, qcol, kcol, vcol = _dsa_specs(T)
    S = DSA_SUPER

    def body(*refs):
        refs = _host_exchange(exchange, refs, 6, 2, pl.program_id(0) * nsb + pl.program_id(1), DSA_PAIRS * nsb)
        q_ref, kp_ref, kc_ref, vp_ref, vc_ref, tab_ref, out_ref, lse_ref, kk, vv, ob, lb = refs
        sb = pl.program_id(1)
        kk[0:S, :] = kp_ref[...]
        kk[S:2 * S, :] = kc_ref[...]
        vv[0:S, :] = vp_ref[...]
        vv[S:2 * S, :] = vc_ref[...]
        head0 = _head_lane_mask()

        for di, d in enumerate(DSA_DILATIONS):
            n_res, n_blk = _dsa_units(d)

            def unit(u, carry, di=di, d=d, n_blk=n_blk):
                r = u // n_blk
                c = u % n_blk
                q0 = r + d * DSA_BLOCK * c
                qrows = pl.ds(q0, DSA_BLOCK, stride=d) if d > 1 else pl.ds(q0, DSA_BLOCK)
                krows = pl.ds(S + q0 - d * DSA_BLOCK, 2 * DSA_BLOCK, stride=d) if d > 1 else pl.ds(S + q0 - DSA_BLOCK, 2 * DSA_BLOCK)
                q2 = q_ref[qrows, :] * QK_SCALE
                k2 = kk[krows, :].astype(BF16)
                v2 = vv[krows, :].astype(BF16)
                qs = _stack_heads(q2, head0).astype(BF16)
                s = _dot_nt(qs, k2) + (tab_ref[di, 0] + _first_block_penalty((sb == 0) & (c == 0)))
                m = jnp.max(s, axis=-1, keepdims=True)
                p = jnp.exp(s - m)
                den = jnp.sum(p, axis=-1, keepdims=True)
                o = _dot(p.astype(BF16), v2) / den
                l = jnp.broadcast_to(m + jnp.log(den), (2 * DSA_BLOCK, LANE))
                ob[di, qrows, :] = jnp.where(head0, o[:DSA_BLOCK], o[DSA_BLOCK:])
                lb[di, qrows, :] = jnp.where(head0, l[:DSA_BLOCK], l[DSA_BLOCK:])
                return carry

            lax.fori_loop(0, n_res * n_blk, unit, 0, unroll=DSA_UNROLL)

        def combine(i, carry):
            rows = pl.ds(pl.multiple_of(i * DSA_COMBINE_ROWS, DSA_COMBINE_ROWS), DSA_COMBINE_ROWS)
            l0, l1, l2 = lb[0, rows, :], lb[1, rows, :], lb[2, rows, :]
            mx = jnp.maximum(jnp.maximum(l0, l1), l2)
            e0, e1, e2 = jnp.exp(l0 - mx), jnp.exp(l1 - mx), jnp.exp(l2 - mx)
            den = e0 + e1 + e2
            out_ref[rows, :] = (e0 * ob[0, rows, :] + e1 * ob[1, rows, :] + e2 * ob[2, rows, :]) / den
            lse_ref[rows, :] = mx + jnp.log(den)
            return carry

        lax.fori_loop(0, S // DSA_COMBINE_ROWS, combine, 0)

    prev = lambda col: (lambda hp, sb: (jnp.maximum(sb - 1, 0), col + hp))
    cur = lambda col: (lambda hp, sb: (sb, col + hp))
    blk = lambda f: pl.BlockSpec((S, LANE), f)
    return _hosted_call(
        exchange,
        body,
        grid=(DSA_PAIRS, nsb),
        in_specs=[blk(cur(qcol)), blk(prev(kcol)), blk(cur(kcol)), blk(prev(vcol)), blk(cur(vcol)),
                  pl.BlockSpec((len(DSA_DILATIONS), 1, 2 * DSA_BLOCK, 2 * DSA_BLOCK), lambda hp, sb: (0, hp, 0, 0))],
        out_specs=[blk(lambda hp, sb: (sb, hp)), blk(lambda hp, sb: (sb, hp))],
        out_shape=[jax.ShapeDtypeStruct((T, DSA_WIDTH), F32), jax.ShapeDtypeStruct((T, DSA_WIDTH), F32)],
        scratch_shapes=[pltpu.VMEM((2 * S, LANE), F32), pltpu.VMEM((2 * S, LANE), F32),
                        pltpu.VMEM((len(DSA_DILATIONS), S, LANE), F32), pltpu.VMEM((len(DSA_DILATIONS), S, LANE), F32)],
        compiler_params=_params(("arbitrary", "arbitrary")),
        name="dsa_fwd",
        args=(proj, proj, proj, proj, proj, _pair_tiles(tab)),
    )


def dsa_bwd(proj, tab, ob_out, lse, dmixed, exchange=None):
    T = proj.shape[0]
    nsb, qcol, kcol, vcol = _dsa_specs(T)
    S = DSA_SUPER
    nd = len(DSA_DILATIONS)
    ocol = GLA_WIDTH // LANE

    def body(*refs):
        refs = _host_exchange(exchange, refs, 9, 4, pl.program_id(0) * nsb + pl.program_id(1), DSA_PAIRS * nsb)
        (q_ref, kp_ref, kc_ref, vp_ref, vc_ref, tab_ref, o_ref, lse_ref, do_ref,
         dq_ref, dk_ref, dv_ref, dtab_ref, kk, vv, dqa, dkk, dvv) = refs
        j = pl.program_id(1)
        sb = nsb - 1 - j
        kk[0:S, :] = kp_ref[...]
        kk[S:2 * S, :] = kc_ref[...]
        vv[0:S, :] = vp_ref[...]
        vv[S:2 * S, :] = vc_ref[...]
        head0 = _head_lane_mask()

        @pl.when(j == 0)
        def _():
            dtab_ref[...] = jnp.zeros_like(dtab_ref)
            dkk[S:2 * S, :] = jnp.zeros((S, LANE), F32)
            dvv[S:2 * S, :] = jnp.zeros((S, LANE), F32)

        @pl.when(j > 0)
        def _():
            dkk[S:2 * S, :] = dkk[0:S, :]
            dvv[S:2 * S, :] = dvv[0:S, :]

        dkk[0:S, :] = jnp.zeros((S, LANE), F32)
        dvv[0:S, :] = jnp.zeros((S, LANE), F32)
        dqa[...] = jnp.zeros_like(dqa)

        for di, d in enumerate(DSA_DILATIONS):
            n_res, n_blk = _dsa_units(d)

            def unit(u, carry, di=di, d=d, n_blk=n_blk):
                r = u // n_blk
                c = u % n_blk
                q0 = r + d * DSA_BLOCK * c
                qrows = pl.ds(q0, DSA_BLOCK, stride=d) if d > 1 else pl.ds(q0, DSA_BLOCK)
                krows = pl.ds(S + q0 - d * DSA_BLOCK, 2 * DSA_BLOCK, stride=d) if d > 1 else pl.ds(S + q0 - DSA_BLOCK, 2 * DSA_BLOCK)
                q2 = q_ref[qrows, :] * QK_SCALE
                k2 = kk[krows, :].astype(BF16)
                v2 = vv[krows, :].astype(BF16)
                do2 = do_ref[qrows, :]
                o2 = o_ref[qrows, :]
                l2 = lse_ref[qrows, :]
                qs = _stack_heads(q2, head0).astype(BF16)
                dos = _stack_heads(do2, head0)
                dos_b = dos.astype(BF16)
                delta = jnp.sum(dos * jnp.concatenate([o2, o2], axis=0), axis=-1, keepdims=True)
                lse = jnp.concatenate([jnp.max(jnp.where(head0, l2, -jnp.inf), axis=-1, keepdims=True),
                                       jnp.max(jnp.where(head0, -jnp.inf, l2), axis=-1, keepdims=True)], axis=0)
                s = _dot_nt(qs, k2) + (tab_ref[di, 0] + _first_block_penalty((sb == 0) & (c == 0)))
                p = jnp.exp(s - lse)
                ds = p * (_dot_nt(dos_b, v2) - delta)
                dtab_ref[di, 0] += ds
                ds_b = ds.astype(BF16)
                dq = _dot(ds_b, k2)
                dqa[qrows, :] += jnp.where(head0, dq[:DSA_BLOCK], dq[DSA_BLOCK:]) * QK_SCALE
                dkk[krows, :] += _dot_tn(ds_b, qs)
                dvv[krows, :] += _dot_tn(p.astype(BF16), dos_b)
                return carry

            lax.fori_loop(0, n_res * n_blk, unit, 0, unroll=DSA_UNROLL)

        dq_ref[...] = dqa[...].astype(BF16)
        dk_ref[...] = dkk[S:2 * S, :].astype(BF16)
        dv_ref[...] = dvv[S:2 * S, :].astype(BF16)

    prev = lambda col: (lambda hp, j: (jnp.maximum(nsb - 2 - j, 0), col + hp))
    cur = lambda col: (lambda hp, j: (nsb - 1 - j, col + hp))
    blk = lambda f: pl.BlockSpec((S, LANE), f)
    out_blk = blk(lambda hp, j: (nsb - 1 - j, hp))
    tab_blk = pl.BlockSpec((nd, 1, 2 * DSA_BLOCK, 2 * DSA_BLOCK), lambda hp, j: (0, hp, 0, 0))
    dq, dk, dv, dtab, *carried = _hosted_call(
        exchange,
        body,
        grid=(DSA_PAIRS, nsb),
        in_specs=[blk(cur(qcol)), blk(prev(kcol)), blk(cur(kcol)), blk(prev(vcol)), blk(cur(vcol)), tab_blk,
                  out_blk, out_blk, blk(cur(ocol))],
        out_specs=[out_blk, out_blk, out_blk, tab_blk],
        out_shape=[jax.ShapeDtypeStruct((T, DSA_WIDTH), BF16)] * 3
        + [jax.ShapeDtypeStruct((nd, DSA_PAIRS, 2 * DSA_BLOCK, 2 * DSA_BLOCK), F32)],
        scratch_shapes=[pltpu.VMEM((2 * S, LANE), F32), pltpu.VMEM((2 * S, LANE), F32), pltpu.VMEM((S, LANE), F32),
                        pltpu.VMEM((2 * S, LANE), F32), pltpu.VMEM((2 * S, LANE), F32)],
        compiler_params=_params(("arbitrary", "arbitrary")),
        name="dsa_bwd",
        args=(proj, proj, proj, proj, proj, _pair_tiles(tab), ob_out, lse, dmixed),
    )
    return (dq, dk, dv, dtab.reshape(nd, DSA_HEADS, DSA_BLOCK, 2 * DSA_BLOCK), *carried)


FF_BLOCKS = 4
FF_BLOCK = D_FF // FF_BLOCKS


def post_fused(x, oa, ob, tgt, g2, gf, wout, wff1, wff2):
    T = x.shape[0]
    tm = 256
    inv_d = 1.0 / D_MODEL

    def body(x_ref, oa_ref, ob_ref, tgt_ref, g2_ref, gf_ref, wout_hbm, wff1_hbm, wff2_hbm,
             mixed_ref, nm_ref, a_ref, dpre_ref, dh2_ref, dh1_ref, dmixed_ref, loss_ref, dgf_ref, dg2_ref,
             wout_v, wff1_v, wff2_v, sems):
        @pl.when(pl.program_id(0) == 0)
        def _():
            cps = [pltpu.make_async_copy(s, d, sems.at[i])
                   for i, (s, d) in enumerate([(wout_hbm, wout_v), (wff1_hbm, wff1_v), (wff2_hbm, wff2_v)])]
            for cp in cps:
                cp.start()
            for cp in cps:
                cp.wait()
            loss_ref[...] = jnp.zeros_like(loss_ref)
            dgf_ref[...] = jnp.zeros_like(dgf_ref)
            dg2_ref[...] = jnp.zeros_like(dg2_ref)

        mixed = jnp.concatenate([oa_ref[...], ob_ref[...].astype(BF16)], axis=1)
        mixed_ref[...] = mixed
        h1 = x_ref[...] + _dot(mixed, wout_v[...])
        rs1 = _rstd(h1)
        hn1 = h1 * rs1
        g2 = g2_ref[...]
        nm = (hn1 * g2).astype(BF16)
        nm_ref[...] = nm
        relu = []
        mlp = jnp.zeros((tm, D_MODEL), F32)
        for j in range(FF_BLOCKS):
            cols = slice(j * FF_BLOCK, (j + 1) * FF_BLOCK)
            r_j = jnp.maximum(_dot(nm, wff1_v[j]), 0.0)
            a_j = (r_j * r_j).astype(BF16)
            a_ref[:, cols] = a_j
            relu.append(r_j)
            mlp = mlp + _dot(a_j, wff2_v[cols, :])
        h2 = h1 + mlp
        rsf = _rstd(h2)
        hnf = h2 * rsf
        gf = gf_ref[...]
        diff = hnf * gf - tgt_ref[...]
        loss_ref[...] += 0.5 * jnp.sum(jnp.sum(diff * diff, axis=-1, keepdims=True) * inv_d, axis=0, keepdims=True)
        dy = diff * inv_d
        dgf_ref[...] += jnp.sum(dy * hnf, axis=0, keepdims=True)
        dhnf = dy * gf
        dh2 = rsf * (dhnf - hnf * jnp.mean(dhnf * hnf, axis=-1, keepdims=True))
        dh2_b = dh2.astype(BF16)
        dh2_ref[...] = dh2_b
        dnm = jnp.zeros((tm, D_MODEL), F32)
        for j in range(FF_BLOCKS):
            cols = slice(j * FF_BLOCK, (j + 1) * FF_BLOCK)
            dpre_j = (_dot_nt(dh2_b, wff2_v[cols, :]) * (2.0 * relu[j])).astype(BF16)
            dpre_ref[:, cols] = dpre_j
            dnm = dnm + _dot_nt(dpre_j, wff1_v[j])
        dg2_ref[...] += jnp.sum(dnm * hn1, axis=0, keepdims=True)
        dhn1 = dnm * g2
        dh1 = dh2 + rs1 * (dhn1 - hn1 * jnp.mean(dhn1 * hn1, axis=-1, keepdims=True))
        dh1_ref[...] = dh1
        dmixed_ref[...] = _dot_nt(dh1.astype(BF16), wout_v[...])

    row = lambda w: pl.BlockSpec((tm, w), lambda i: (i, 0))
    vec = lambda w: pl.BlockSpec((1, w), lambda i: (0, 0))
    return pl.pallas_call(
        body,
        grid=(T // tm,),
        in_specs=[row(D_MODEL), row(GLA_WIDTH), row(DSA_WIDTH), row(D_MODEL), vec(D_MODEL), vec(D_MODEL), ANY, ANY, ANY],
        out_specs=[row(D_MODEL), row(D_MODEL), row(D_FF), row(D_FF), row(D_MODEL), row(D_MODEL), row(D_MODEL),
                   vec(1), vec(D_MODEL), vec(D_MODEL)],
        out_shape=[
            jax.ShapeDtypeStruct((T, D_MODEL), BF16),
            jax.ShapeDtypeStruct((T, D_MODEL), BF16),
            jax.ShapeDtypeStruct((T, D_FF), BF16),
            jax.ShapeDtypeStruct((T, D_FF), BF16),
            jax.ShapeDtypeStruct((T, D_MODEL), BF16),
            jax.ShapeDtypeStruct((T, D_MODEL), F32),
            jax.ShapeDtypeStruct((T, D_MODEL), F32),
            jax.ShapeDtypeStruct((1, 1), F32),
            jax.ShapeDtypeStruct((1, D_MODEL), F32),
            jax.ShapeDtypeStruct((1, D_MODEL), F32),
        ],
        scratch_shapes=[pltpu.VMEM((D_MODEL, D_MODEL), BF16), pltpu.VMEM((FF_BLOCKS, D_MODEL, FF_BLOCK), BF16),
                        pltpu.VMEM((D_FF, D_MODEL), BF16), pltpu.SemaphoreType.DMA((3,))],
        compiler_params=_params(),
        name="post_fused",
    )(x, oa, ob, tgt, g2, gf, wout, wff1, wff2)


WGRAD_TOKENS = 2048


def wgrad(a, b, name, bm=None, bn=None, col_blocked=False):
    T, M = a.shape
    N = b.shape[1]
    bm = M if bm is None else bm
    bn = N if bn is None else bn
    tk = min(WGRAD_TOKENS, T)

    def body(a_ref, b_ref, o_ref):
        part = _dot_tn(a_ref[...].astype(BF16), b_ref[...].astype(BF16))
        out = o_ref.at[0] if col_blocked else o_ref

        @pl.when(pl.program_id(2) == 0)
        def _():
            out[...] = part

        @pl.when(pl.program_id(2) > 0)
        def _():
            out[...] += part

    if col_blocked:
        assert bm == M
        out_spec = pl.BlockSpec((1, M, bn), lambda i, j, k: (j, 0, 0))
        out_shape = jax.ShapeDtypeStruct((N // bn, M, bn), F32)
    else:
        out_spec = pl.BlockSpec((bm, bn), lambda i, j, k: (i, j))
        out_shape = jax.ShapeDtypeStruct((M, N), F32)
    return pl.pallas_call(
        body,
        grid=(M // bm, N // bn, T // tk),
        in_specs=[pl.BlockSpec((tk, bm), lambda i, j, k: (k, i)), pl.BlockSpec((tk, bn), lambda i, j, k: (k, j))],
        out_specs=out_spec,
        out_shape=out_shape,
        compiler_params=_params(("arbitrary", "arbitrary", "arbitrary")),
        name=name,
    )(a, b)


def dx_final(x, dh1, g1, da, dq, dk, dv, wp, exchange=None):
    T = x.shape[0]
    tm = 256

    def body(*refs):
        refs = _host_exchange(exchange, refs, 8, 2, pl.program_id(0), T // tm)
        x_ref, dh1_ref, g_ref, da_ref, dq_ref, dk_ref, dv_ref, w_hbm, dx_ref, dg_ref, w_vmem, sem = refs

        @pl.when(pl.program_id(0) == 0)
        def _():
            _load_once(w_hbm, w_vmem, sem)
            dg_ref[...] = jnp.zeros_like(dg_ref)

        dnx = (_dot_nt(da_ref[...], w_vmem[:, 0:P_A]) + _dot_nt(dq_ref[...], w_vmem[:, P_DQ:P_DQ + DSA_WIDTH])
               + _dot_nt(dk_ref[...], w_vmem[:, P_DK:P_DK + DSA_WIDTH]) + _dot_nt(dv_ref[...], w_vmem[:, P_DV:P_DV + DSA_WIDTH]))
        xf = x_ref[...]
        rs = _rstd(xf)
        hn = xf * rs
        dg_ref[...] += jnp.sum(dnx * hn, axis=0, keepdims=True)
        dhn = dnx * g_ref[...]
        dx_ref[...] = dh1_ref[...] + rs * (dhn - hn * jnp.mean(dhn * hn, axis=-1, keepdims=True))

    row = lambda w: pl.BlockSpec((tm, w), lambda i: (i, 0))
    vec = pl.BlockSpec((1, D_MODEL), lambda i: (0, 0))
    return _hosted_call(
        exchange,
        body,
        grid=(T // tm,),
        in_specs=[row(D_MODEL), row(D_MODEL), vec, row(P_A), row(DSA_WIDTH), row(DSA_WIDTH), row(DSA_WIDTH), ANY],
        out_specs=[row(D_MODEL), vec],
        out_shape=[jax.ShapeDtypeStruct((T, D_MODEL), F32), jax.ShapeDtypeStruct((1, D_MODEL), F32)],
        scratch_shapes=[pltpu.VMEM((D_MODEL, P_ALL), BF16), pltpu.SemaphoreType.DMA],
        compiler_params=_params(),
        name="dx_final",
        args=(x, dh1, g1, da, dq, dk, dv, wp),
    )


def adamw(w, g, m, v, name):
    R, C = w.shape
    br = 256 if R % 256 == 0 else R

    def body(w_ref, g_ref, m_ref, v_ref, d_ref, nm_ref, nv_ref):
        d_ref[...], nm_ref[...], nv_ref[...] = _adamw_math(w_ref[...], g_ref[...], m_ref[...], v_ref[...])

    spec = pl.BlockSpec((br, C), lambda i: (i, 0))
    return pl.pallas_call(
        body,
        grid=(R // br,),
        in_specs=[spec] * 4,
        out_specs=[spec] * 3,
        out_shape=[jax.ShapeDtypeStruct((R, C), F32)] * 3,
        compiler_params=_params(),
        name=name,
    )(w, g, m, v)


def _place():
    return lax.axis_index("x"), lax.axis_index("y"), lax.axis_index("c")


def _other_chips(x, y):
    return [(1 - x, y), (x, 1 - y), (1 - x, 1 - y)]


class Exchange:
    def __init__(self, kind, arrays):
        self.kind, self.arrays, self.n = kind, arrays, len(arrays)

    def out_shapes(self):
        if self.kind == "gather":
            return [jax.ShapeDtypeStruct((4,) + s.shape, s.dtype) for s in self.arrays]
        if self.kind == "scatter":
            return [jax.ShapeDtypeStruct((4, 2 * s.shape[1], s.shape[2]), s.dtype) for s in self.arrays]
        return [jax.ShapeDtypeStruct((4, s.shape[1] // 2, s.shape[2]), s.dtype) for s in self.arrays]

    def sems(self):
        return [pltpu.SemaphoreType.DMA((self.n, 7)), pltpu.SemaphoreType.DMA((self.n, 7))]

    def phases(self, ins, outs, send_sems, recv_sems):
        n, kind = self.n, self.kind
        x, y, c = _place()
        me, sib = (x, y, c), (x, y, 1 - c)
        mine = 2 * x + y
        chips = _other_chips(x, y)

        def region(a, owner, half):
            h = outs[a].shape[1] // 2
            return outs[a].at[owner, pl.ds(half * h, h)]

        def copy(a, k, owner, half, to, src=None):
            return pltpu.make_async_remote_copy(
                src_ref=region(a, owner, half) if src is None else src, dst_ref=region(a, owner, half),
                send_sem=send_sems.at[a, k], recv_sem=recv_sems.at[a, k], device_id=to, device_id_type=MESH)

        def swap_copy(a, to):
            h = ins[a].shape[1] // 2
            return pltpu.make_async_remote_copy(
                src_ref=ins[a].at[:, pl.ds((1 - c) * h, h)], dst_ref=outs[a],
                send_sem=send_sems.at[a, 0], recv_sem=recv_sems.at[a, 0], device_id=to, device_id_type=MESH)

        def first_copies():
            if kind == "swap":
                return [swap_copy(a, sib) for a in range(n)]
            cps = []
            for a in range(n):
                h = outs[a].shape[1] // 2
                for t, (cx, cy) in enumerate(chips):
                    src = ins[a].at[2 * cx + cy] if kind == "scatter" else ins[a].at[pl.ds(c * h, h)]
                    cps.append(copy(a, t, mine, c, (cx, cy, c), src=src))
                if kind == "scatter":
                    cps.append(copy(a, 6, mine, c, sib, src=ins[a].at[mine]))
            return cps

        def forward_copies():
            if kind == "swap":
                return []
            return [copy(a, 3 + t, 2 * cx + cy, c, sib) for a in range(n) for t, (cx, cy) in enumerate(chips)]

        def start():
            for cp in first_copies():
                cp.start()

        def forward():
            if kind == "swap":
                return
            fws = forward_copies()
            for a in range(n):
                for t, (cx, cy) in enumerate(chips):
                    copy(a, t, 2 * cx + cy, c, me).wait_recv()
                    fws[3 * a + t].start()

        def finish():
            for a in range(n):
                if kind == "swap":
                    swap_copy(a, me).wait_recv()
                    continue
                for t, (cx, cy) in enumerate(chips):
                    copy(a, 3 + t, 2 * cx + cy, 1 - c, me).wait_recv()
                if kind == "scatter":
                    copy(a, 6, mine, 1 - c, me).wait_recv()
            for cp in first_copies() + forward_copies():
                cp.wait_send()

        return start, forward, finish

    def fill_own(self, outs):
        if self.kind == "swap":
            return list(outs)
        x, y, c = _place()
        if self.kind == "gather":
            return [lax.dynamic_update_index_in_dim(o, s, 2 * x + y, 0) for o, s in zip(outs, self.arrays)]
        filled = []
        for o, s in zip(outs, self.arrays):
            own = lax.dynamic_index_in_dim(s, 2 * x + y, 0, keepdims=True)
            filled.append(lax.dynamic_update_slice(o, own, (2 * x + y, c * s.shape[1], 0)))
        return filled

    def run(self, name):
        n = self.n

        def body(*refs):
            start, forward, finish = self.phases(refs[:n], refs[n:2 * n], *refs[2 * n:])
            start()
            forward()
            finish()

        outs = pl.pallas_call(
            body, in_specs=[ANY] * n, out_specs=[ANY] * n, out_shape=self.out_shapes(), scratch_shapes=self.sems(), name=name,
        )(*self.arrays)
        return self.fill_own(outs)


def _host_exchange(exchange, refs, n_in, n_out, step, n_steps):
    if exchange is None:
        return refs
    n = exchange.n
    own_in, ex_in = refs[:n_in], refs[n_in:n_in + n]
    own_out, ex_out = refs[n_in + n:n_in + n + n_out], refs[n_in + n + n_out:n_in + 2 * n + n_out]
    rest = refs[n_in + 2 * n + n_out:]
    start, forward, finish = exchange.phases(ex_in, ex_out, rest[-2], rest[-1])
    pl.when(step == 0)(start)
    pl.when(step == (2 * n_steps) // 3)(forward)
    pl.when(step == n_steps - 1)(finish)
    return own_in + own_out + rest[:-2]


def _hosted_call(exchange, body, *, grid, in_specs, out_specs, out_shape, scratch_shapes, compiler_params, name, args):
    if exchange is None:
        return pl.pallas_call(body, grid=grid, in_specs=in_specs, out_specs=out_specs, out_shape=out_shape,
                              scratch_shapes=scratch_shapes, compiler_params=compiler_params, name=name)(*args)
    n = exchange.n
    res = pl.pallas_call(
        body, grid=grid, in_specs=list(in_specs) + [ANY] * n, out_specs=list(out_specs) + [ANY] * n,
        out_shape=list(out_shape) + exchange.out_shapes(), scratch_shapes=list(scratch_shapes) + exchange.sems(),
        compiler_params=compiler_params, name=name)(*args, *exchange.arrays)
    return list(res[:len(out_shape)]) + [exchange.fill_own(res[len(out_shape):])]


def add_halves(g, got, core, name):
    _, R, C = g.shape
    h = R // 2
    br = 128
    nb = h // br

    def body(core_ref, g_ref, got_ref, o_ref):
        o_ref[...] = (g_ref[...] + got_ref[...]).astype(BF16)

    return pl.pallas_call(
        body,
        grid_spec=pltpu.PrefetchScalarGridSpec(
            num_scalar_prefetch=1,
            grid=(4, nb),
            in_specs=[pl.BlockSpec((1, br, C), lambda s, i, core: (s, core[0] * nb + i, 0)),
                      pl.BlockSpec((1, br, C), lambda s, i, core: (s, i, 0))],
            out_specs=pl.BlockSpec((1, br, C), lambda s, i, core: (s, i, 0)),
        ),
        out_shape=jax.ShapeDtypeStruct((4, h, C), BF16),
        compiler_params=_params(("arbitrary", "arbitrary")),
        name=name,
    )(core, g, got)


def sum_slots(parts, name):
    S, R, C = parts.shape
    br = 128 if R % 128 == 0 else R

    def body(p_ref, o_ref):
        acc = p_ref[0].astype(F32)
        for s in range(1, S):
            acc = acc + p_ref[s].astype(F32)
        o_ref[...] = acc

    return pl.pallas_call(
        body,
        grid=(R // br,),
        in_specs=[pl.BlockSpec((S, br, C), lambda i: (0, i, 0))],
        out_specs=pl.BlockSpec((br, C), lambda i: (i, 0)),
        out_shape=jax.ShapeDtypeStruct((R, C), F32),
        compiler_params=_params(),
        name=name,
    )(parts)


def _adamw_math(w, g, m, v):
    m_new = ADAM_B1 * m + (1.0 - ADAM_B1) * g
    v_new = ADAM_B2 * v + (1.0 - ADAM_B2) * (g * g)
    m_hat = m_new / (1.0 - ADAM_B1 ** ADAM_STEP)
    v_hat = v_new / (1.0 - ADAM_B2 ** ADAM_STEP)
    return -ADAM_LR * (m_hat / (jnp.sqrt(v_hat) + ADAM_EPS) + ADAM_WD * w), m_new, v_new


def reduce_adamw(slots, w, m, v, name):
    S, R, C = slots.shape
    br = 128

    def body(p_ref, w_ref, m_ref, v_ref, g_ref, d_ref, nm_ref, nv_ref):
        g = p_ref[0].astype(F32)
        for s in range(1, S):
            g = g + p_ref[s].astype(F32)
        g_ref[...] = g
        d_ref[...], nm_ref[...], nv_ref[...] = _adamw_math(w_ref[...], g, m_ref[...], v_ref[...])

    spec = pl.BlockSpec((br, C), lambda i: (i, 0))
    return pl.pallas_call(
        body,
        grid=(R // br,),
        in_specs=[pl.BlockSpec((S, br, C), lambda i: (0, i, 0)), spec, spec, spec],
        out_specs=[spec] * 4,
        out_shape=[jax.ShapeDtypeStruct((R, C), F32)] * 4,
        compiler_params=_params(),
        name=name,
    )(slots, w, m, v)


SMALL_ROWS = 64


def gather_small(vec):
    def body(v_ref, o_ref, send_sems, recv_sems, local_sem):
        x, y, c = _place()
        flips = [(fx, fy, fc) for fx in (0, 1) for fy in (0, 1) for fc in (0, 1)][1:]

        def peer(f):
            return (1 - x if f[0] else x, 1 - y if f[1] else y, 1 - c if f[2] else c)

        slot = lambda p: 4 * p[0] + 2 * p[1] + p[2]
        own = pltpu.make_async_copy(v_ref, o_ref.at[slot((x, y, c))], local_sem)
        own.start()
        cps = [pltpu.make_async_remote_copy(
            src_ref=v_ref, dst_ref=o_ref.at[slot((x, y, c))], send_sem=send_sems.at[k], recv_sem=recv_sems.at[k],
            device_id=peer(f), device_id_type=MESH) for k, f in enumerate(flips)]
        for cp in cps:
            cp.start()
        for k, f in enumerate(flips):
            pltpu.make_async_remote_copy(
                src_ref=v_ref, dst_ref=o_ref.at[slot(peer(f))], send_sem=send_sems.at[k], recv_sem=recv_sems.at[k],
                device_id=(x, y, c), device_id_type=MESH).wait_recv()
        for cp in cps:
            cp.wait_send()
        own.wait()

    return pl.pallas_call(
        body,
        in_specs=[ANY],
        out_specs=ANY,
        out_shape=jax.ShapeDtypeStruct((8,) + vec.shape, vec.dtype),
        scratch_shapes=[pltpu.SemaphoreType.DMA((7,)), pltpu.SemaphoreType.DMA((7,)), pltpu.SemaphoreType.DMA],
        name="gather_small",
    )(vec)


GLOW_PAD = LANE - GLA_RANK


def kernel(x, attn_norm_g, w_in, gla_gate_w2, gla_gate_b, gla_norm_g, rel_bias, w_out, mlp_norm_g, w_ff1, w_ff2, final_norm_g, loss_target, m_attn_norm_g, m_w_in, m_gla_gate_w2, m_gla_gate_b, m_gla_norm_g, m_rel_bias, m_w_out, m_mlp_norm_g, m_w_ff1, m_w_ff2, m_final_norm_g, v_attn_norm_g, v_w_in, v_gla_gate_w2, v_gla_gate_b, v_gla_norm_g, v_rel_bias, v_w_out, v_mlp_norm_g, v_w_ff1, v_w_ff2, v_final_norm_g):
    xs, tgt = x[0], loss_target[0]
    T = xs.shape[0]
    cx, cy, cc = _place()
    chip = 2 * cx + cy
    gf = final_norm_g.reshape(1, D_MODEL)

    win_g, w2_g = Exchange("gather", [w_in[0].astype(BF16), gla_gate_w2[0]]).run("gather_w_in")
    win = jnp.transpose(win_g, (1, 0, 2)).reshape(D_MODEL, D_IN)
    n_glow = R_GLOW + GLA_RANK
    wp = jnp.concatenate([win[:, :n_glow], jnp.zeros((D_MODEL, GLOW_PAD), BF16), win[:, n_glow:]], axis=1)
    w2 = jnp.transpose(w2_g, (1, 0, 2)).reshape(GLA_RANK, GLA_QK)
    w2p = jnp.concatenate([w2, jnp.zeros((GLOW_PAD, GLA_QK), F32)], axis=0)

    proj, nx = inproj(xs, attn_norm_g, wp)
    tab = bias_tables(rel_bias)
    ob, lse, (wout_g, wff1, wff2_g) = dsa_fwd(
        proj, tab, Exchange("gather", [w_out[0].astype(BF16), w_ff1[0].astype(BF16), w_ff2[0].astype(BF16)]))
    wout = wout_g.reshape(D_MODEL, D_MODEL)
    wff2 = wff2_g.reshape(D_FF, D_MODEL)
    oa, opre, sprev = gla_fwd(proj, w2p, gla_gate_b, gla_norm_g)
    mixed, nm, act, dpre, dh2, dh1, dmixed, loss, dgf, dg2 = post_fused(xs, oa, ob, tgt, mlp_norm_g, gf, wout, wff1, wff2)

    core = cc.astype(jnp.int32).reshape(1)
    late = [
        wgrad(mixed, dh1, "wgrad_out").reshape(4, D_MODEL // 4, D_MODEL),
        wgrad(nm, dpre, "wgrad_ff1", bn=FF_BLOCK, col_blocked=True),
        wgrad(act, dh2, "wgrad_ff2", bm=FF_BLOCK).reshape(4, FF_BLOCK, D_MODEL),
    ]
    late_names = ["w_out", "w_ff1", "w_ff2"]
    da, dw2p, dbg, dgn, got = gla_bwd(proj, w2p, gla_gate_b, gla_norm_g, opre, sprev, dmixed, Exchange("swap", late))
    sums = [add_halves(g, r, core, "add_halves_" + s) for g, r, s in zip(late, got, late_names)]
    dq, dk, dv, dtab, late_slots = dsa_bwd(proj, tab, ob, lse, dmixed, Exchange("scatter", sums))
    slots = dict(zip(late_names, late_slots))
    drel = bias_tables_bwd(dtab)

    dwa = wgrad(nx, da, "wgrad_in_gla")
    dwq = wgrad(nx, dq, "wgrad_in_q")
    dwk = wgrad(nx, dk, "wgrad_in_k")
    dwv = wgrad(nx, dv, "wgrad_in_v")
    dwin = jnp.concatenate([dwa[:, :n_glow], dwq, dwk, dwv], axis=1)
    dwin = [jnp.transpose(dwin.reshape(D_MODEL, 4, D_IN // 4), (1, 0, 2))]
    got = Exchange("swap", dwin).run("swap_w_in")
    sums = [add_halves(dwin[0], got[0], core, "add_halves_w_in")]
    dxs, dg1, (slots["w_in"],) = dx_final(xs, dh1, attn_norm_g, da, dq, dk, dv, wp, Exchange("scatter", sums))

    small = jnp.concatenate([dg1.reshape(-1), dbg.reshape(-1), dgn.reshape(-1), drel.reshape(-1), dg2.reshape(-1),
                             dgf.reshape(-1), dw2p[:GLA_RANK].reshape(-1)]).reshape(SMALL_ROWS, LANE)
    tot = sum_slots(gather_small(small), "sum_small").reshape(-1)
    sizes = [D_MODEL, GLA_QK, GLA_WIDTH, REL_BUCKETS * DSA_HEADS, D_MODEL, D_MODEL, GLA_RANK * GLA_QK]
    offs = np.concatenate([[0], np.cumsum(sizes)])
    piece = lambda i: tot[int(offs[i]):int(offs[i + 1])]
    g_g1 = piece(0).reshape(1, D_MODEL)
    g_bg = piece(1).reshape(1, GLA_QK)
    g_gn = piece(2).reshape(1, GLA_WIDTH)
    g_rel = piece(3).reshape(REL_BUCKETS, DSA_HEADS)
    g_g2 = piece(4).reshape(1, D_MODEL)
    g_gf = piece(5).reshape(1, D_MODEL)
    g_w2 = lax.dynamic_slice_in_dim(piece(6).reshape(GLA_RANK, GLA_QK), chip * (GLA_QK // 4), GLA_QK // 4, axis=1)

    loss_all = lax.psum(loss[0, 0], ("x", "y", "c"))

    upd = [
        ("attn_norm_g", attn_norm_g, g_g1, m_attn_norm_g, v_attn_norm_g),
        ("w_in", w_in[0], None, m_w_in[0], v_w_in[0]),
        ("gla_gate_w2", gla_gate_w2[0], g_w2, m_gla_gate_w2[0], v_gla_gate_w2[0]),
        ("gla_gate_b", gla_gate_b, g_bg, m_gla_gate_b, v_gla_gate_b),
        ("gla_norm_g", gla_norm_g, g_gn, m_gla_norm_g, v_gla_norm_g),
        ("rel_bias", rel_bias, g_rel, m_rel_bias, v_rel_bias),
        ("w_out", w_out[0], None, m_w_out[0], v_w_out[0]),
        ("mlp_norm_g", mlp_norm_g, g_g2, m_mlp_norm_g, v_mlp_norm_g),
        ("w_ff1", w_ff1[0], None, m_w_ff1[0], v_w_ff1[0]),
        ("w_ff2", w_ff2[0], None, m_w_ff2[0], v_w_ff2[0]),
        ("final_norm_g", gf, g_gf, m_final_norm_g.reshape(1, D_MODEL), v_final_norm_g.reshape(1, D_MODEL)),
    ]
    shapes = [attn_norm_g.shape, w_in.shape, gla_gate_w2.shape, gla_gate_b.shape, gla_norm_g.shape, rel_bias.shape,
              w_out.shape, mlp_norm_g.shape, w_ff1.shape, w_ff2.shape, final_norm_g.shape]
    grads, deltas, new_m, new_v = [], [], [], []
    for (name, w, g, m, v), shape in zip(upd, shapes):
        if name in slots:
            g, d, nm_, nv_ = reduce_adamw(slots[name], w, m, v, "reduce_adamw_" + name)
        else:
            d, nm_, nv_ = adamw(w, g, m, v, "adamw_" + name)
        grads.append(g.reshape(shape))
        deltas.append(d.reshape(shape))
        new_m.append(nm_.reshape(shape))
        new_v.append(nv_.reshape(shape))
    return (loss_all, dxs.reshape(1, T, D_MODEL), *grads, *deltas, *new_m, *new_v)
```

```python
import functools
import math

import jax
import jax.numpy as jnp
import numpy as np
from jax import lax
from jax.experimental import pallas as pl
from jax.experimental.pallas import tpu as pltpu

F32 = jnp.float32
BF16 = jnp.bfloat16
MESH = pl.DeviceIdType.MESH

D_MODEL = 1024
GLA_WIDTH = 512
GLA_HEADS = 4
GLA_DK = 64
GLA_DV = 128
GLA_QK = GLA_HEADS * GLA_DK
GLA_RANK = 16
GLA_TAU = 16.0
GLA_CHUNK = 64
DSA_WIDTH = 512
DSA_HEADS = 8
DSA_DH = 64
DSA_DILATIONS = (1, 4, 16)
DSA_SPAN = 128
DSA_BLOCK = 128
DSA_SUPER = DSA_BLOCK * DSA_DILATIONS[-1]
REL_BUCKETS = 32
REL_MAX_DIST = 2048
D_FF = 4096
D_IN = 3088
EPS = 1e-6
NEG = -1e30
QK_SCALE = 0.125

ADAM_LR = 0.001
ADAM_B1 = 0.9
ADAM_B2 = 0.999
ADAM_EPS = 1e-08
ADAM_WD = 0.01
ADAM_STEP = 10

LANE = 128
P_GQ, P_GK, P_GV, P_GR = 0, 256, 512, 1024
P_GLOW = 1536
P_A = 1664
P_DQ, P_DK, P_DV = 1664, 2176, 2688
P_ALL = 3200
R_GQ, R_GK, R_GV, R_GR, R_GLOW, R_DQ, R_DK, R_DV = 0, 256, 512, 1024, 1536, 1552, 2064, 2576

VMEM_LIMIT = 56 * 1024 * 1024


def _params(sem=("arbitrary",), vmem=VMEM_LIMIT):
    return pltpu.CompilerParams(dimension_semantics=sem, vmem_limit_bytes=vmem)


def _dot(a, b):
    return jnp.dot(a, b, preferred_element_type=F32)


def _dot_nt(a, b):
    return lax.dot_general(a, b, (((1,), (1,)), ((), ())), preferred_element_type=F32)


def _dot_tn(a, b):
    return lax.dot_general(a, b, (((0,), (0,)), ((), ())), preferred_element_type=F32)


def _split3(x):
    x1 = x.astype(BF16)
    r1 = x - x1.astype(F32)
    x2 = r1.astype(BF16)
    x3 = (r1 - x2.astype(F32)).astype(BF16)
    return x1, x2, x3


def _dot_exact_lhs(m_bf16, x):
    x1, x2, x3 = _split3(x)
    return _dot(m_bf16, x1) + _dot(m_bf16, x2) + _dot(m_bf16, x3)


def _rstd(xf):
    return lax.rsqrt(jnp.mean(xf * xf, axis=-1, keepdims=True) + EPS)


def _load_once(hbm_ref, vmem_ref, sem):
    cp = pltpu.make_async_copy(hbm_ref, vmem_ref, sem)
    cp.start()
    cp.wait()


ANY = pl.BlockSpec(memory_space=pl.ANY)


def inproj(x, g1, wp):
    T = x.shape[0]
    tm = 256

    def body(x_ref, g_ref, w_hbm, proj_ref, nx_ref, w_vmem, sem):
        @pl.when(pl.program_id(0) == 0)
        def _():
            _load_once(w_hbm, w_vmem, sem)

        xf = x_ref[...]
        nx = ((xf * _rstd(xf)) * g_ref[...]).astype(BF16)
        nx_ref[...] = nx
        proj_ref[...] = _dot(nx, w_vmem[...])

    return pl.pallas_call(
        body,
        grid=(T // tm,),
        in_specs=[pl.BlockSpec((tm, D_MODEL), lambda i: (i, 0)), pl.BlockSpec((1, D_MODEL), lambda i: (0, 0)), ANY],
        out_specs=[pl.BlockSpec((tm, P_ALL), lambda i: (i, 0)), pl.BlockSpec((tm, D_MODEL), lambda i: (i, 0))],
        out_shape=[jax.ShapeDtypeStruct((T, P_ALL), F32), jax.ShapeDtypeStruct((T, D_MODEL), BF16)],
        scratch_shapes=[pltpu.VMEM((D_MODEL, P_ALL), BF16), pltpu.SemaphoreType.DMA],
        compiler_params=_params(),
        name="inproj",
    )(x, g1, wp)


GLA_CHUNKS_PER_STEP = 8
GLA_ROWS = GLA_CHUNK * GLA_CHUNKS_PER_STEP


def _gla_masks():
    lane = lax.broadcasted_iota(jnp.int32, (1, GLA_QK), 1)
    return [(lane >= h * GLA_DK) & (lane < (h + 1) * GLA_DK) for h in range(GLA_HEADS)]


def _log_sigmoid(x):
    return jnp.minimum(x, 0.0) - jnp.log(1.0 + jnp.exp(-jnp.abs(x)))


def _sigmoid(x):
    return 1.0 / (1.0 + jnp.exp(-x))


def _head_cols(h):
    return slice(h * GLA_DV, (h + 1) * GLA_DV)


GLA_GROUP = 256


def _gla_step_constants():
    ri = lax.broadcasted_iota(jnp.int32, (GLA_GROUP, GLA_GROUP), 0)
    ci = lax.broadcasted_iota(jnp.int32, (GLA_GROUP, GLA_GROUP), 1)
    shift = GLA_CHUNK.bit_length() - 1
    same = lax.shift_right_logical(ri, shift) == lax.shift_right_logical(ci, shift)
    return same & (ri >= ci), same & (ri <= ci), _gla_masks()


def _by_group(fn, *arrays):
    outs = [fn(*[a[g * GLA_GROUP:(g + 1) * GLA_GROUP] for a in arrays]) for g in range(GLA_ROWS // GLA_GROUP)]
    if isinstance(outs[0], tuple):
        return tuple(jnp.concatenate(parts, axis=0) for parts in zip(*outs))
    return jnp.concatenate(outs, axis=0)


def _per_chunk(x):
    return x.reshape(GLA_CHUNKS_PER_STEP, GLA_CHUNK, x.shape[-1])


def _chunk_rows_of(x, c):
    return x[c * GLA_CHUNK:(c + 1) * GLA_CHUNK]


def _stack_masked(x, masks):
    return jnp.concatenate([jnp.where(m, x, 0.0) for m in masks], axis=0)


def _stack_head_cols(x):
    return jnp.concatenate([x[:, _head_cols(h)] for h in range(GLA_HEADS)], axis=0)


def _diag_blocks(full, masks):
    out = jnp.where(masks[0], full[:GLA_DV], 0.0)
    for h in range(1, GLA_HEADS):
        out = out + jnp.where(masks[h], full[h * GLA_DV:(h + 1) * GLA_DV], 0.0)
    return out


def _row_blocks_masked(full, masks):
    out = jnp.where(masks[0], full[:GLA_CHUNK], 0.0)
    for h in range(1, GLA_HEADS):
        out = out + jnp.where(masks[h], full[h * GLA_CHUNK:(h + 1) * GLA_CHUNK], 0.0)
    return out


def _gla_step_common(q, k, glow_b, w2, bg, tri):
    gpre = _dot(glow_b, w2) + bg
    glog = _log_sigmoid(gpre) / GLA_TAU
    b = _by_group(lambda g: _dot_exact_lhs(tri, g), glog)
    bl = jnp.sum(_per_chunk(glog), axis=1, keepdims=True)
    eb = jnp.exp(b)
    enb = jnp.exp(-b)
    eke = jnp.exp(jnp.broadcast_to(bl, (GLA_CHUNKS_PER_STEP, GLA_CHUNK, GLA_QK)).reshape(GLA_ROWS, GLA_QK) - b)
    return gpre, eb, enb, eke, jnp.exp(bl), (q * QK_SCALE) * eb, k * enb, k * eke


def gla_fwd(proj, w2p, bg, gn):
    T = proj.shape[0]
    n_steps = T // GLA_ROWS
    n_chunks = T // GLA_CHUNK

    def body(proj_ref, w2_ref, bg_ref, gn_ref, oa_ref, opre_ref, sprev_ref, st_ref):
        @pl.when(pl.program_id(0) == 0)
        def _():
            st_ref[...] = jnp.zeros_like(st_ref)

        causal, _, masks = _gla_step_constants()
        q = proj_ref[:, P_GQ:P_GQ + GLA_QK]
        k = proj_ref[:, P_GK:P_GK + GLA_QK]
        v = proj_ref[:, P_GV:P_GV + GLA_WIDTH]
        r = proj_ref[:, P_GR:P_GR + GLA_WIDTH]
        glow = proj_ref[:, P_GLOW:P_GLOW + LANE].astype(BF16)
        _, _, _, _, ebl, qd, ki, ke = _gla_step_common(q, k, glow, w2_ref[...].astype(BF16), bg_ref[...], causal.astype(BF16))
        ki_b = ki.astype(BF16)
        v_b = v.astype(BF16)
        o_heads = []
        for h in range(GLA_HEADS):
            def intra(qd_g, ki_g, v_g):
                att = jnp.where(causal, _dot_nt(qd_g, ki_g), 0.0)
                return _dot(att.astype(BF16), v_g)

            o_heads.append(_by_group(intra, jnp.where(masks[h], qd, 0.0).astype(BF16), ki_b, v_b[:, _head_cols(h)]))
        st = st_ref[...]
        states = []
        for c in range(GLA_CHUNKS_PER_STEP):
            states.append(st)
            sprev_ref[c] = st
            inc = _diag_blocks(_dot_tn(_chunk_rows_of(v_b, c), _chunk_rows_of(ke, c).astype(BF16)), masks)
            st = st * ebl[c] + inc
        st_ref[...] = st
        inter = []
        for c in range(GLA_CHUNKS_PER_STEP):
            qd_c = _stack_masked(_chunk_rows_of(qd, c), masks).astype(BF16)
            got = _dot_nt(qd_c, states[c].astype(BF16))
            inter.append(jnp.concatenate([got[h * GLA_CHUNK:(h + 1) * GLA_CHUNK] for h in range(GLA_HEADS)], axis=1))
        o = jnp.concatenate(o_heads, axis=1) + jnp.concatenate(inter, axis=0)
        opre_ref[...] = o
        on = jnp.concatenate([o[:, _head_cols(h)] * _rstd(o[:, _head_cols(h)]) for h in range(GLA_HEADS)], axis=1)
        oa_ref[...] = ((on * gn_ref[...]) * (r * _sigmoid(r))).astype(BF16)

    return pl.pallas_call(
        body,
        grid=(n_steps,),
        in_specs=[
            pl.BlockSpec((GLA_ROWS, P_ALL), lambda i: (i, 0)),
            pl.BlockSpec((LANE, GLA_QK), lambda i: (0, 0)),
            pl.BlockSpec((1, GLA_QK), lambda i: (0, 0)),
            pl.BlockSpec((1, GLA_WIDTH), lambda i: (0, 0)),
        ],
        out_specs=[
            pl.BlockSpec((GLA_ROWS, GLA_WIDTH), lambda i: (i, 0)),
            pl.BlockSpec((GLA_ROWS, GLA_WIDTH), lambda i: (i, 0)),
            pl.BlockSpec((GLA_CHUNKS_PER_STEP, GLA_DV, GLA_QK), lambda i: (i, 0, 0)),
        ],
        out_shape=[
            jax.ShapeDtypeStruct((T, GLA_WIDTH), BF16),
            jax.ShapeDtypeStruct((T, GLA_WIDTH), F32),
            jax.ShapeDtypeStruct((n_chunks, GLA_DV, GLA_QK), F32),
        ],
        scratch_shapes=[pltpu.VMEM((GLA_DV, GLA_QK), F32)],
        compiler_params=_params(),
        name="gla_fwd",
    )(proj, w2p, bg, gn)


def gla_bwd(proj, w2p, bg, gn, opre, sprev, dmixed, exchange=None):
    T = proj.shape[0]
    n_steps = T // GLA_ROWS

    def body(*refs):
        refs = _host_exchange(exchange, refs, 7, 4, pl.program_id(0), n_steps)
        proj_ref, w2_ref, bg_ref, gn_ref, opre_ref, sprev_ref, doa_ref, da_ref, dw2_ref, dbg_ref, dgn_ref, dst_ref = refs

        @pl.when(pl.program_id(0) == 0)
        def _():
            dst_ref[...] = jnp.zeros_like(dst_ref)
            dw2_ref[...] = jnp.zeros_like(dw2_ref)
            dbg_ref[...] = jnp.zeros_like(dbg_ref)
            dgn_ref[...] = jnp.zeros_like(dgn_ref)

        causal, causal_t, masks = _gla_step_constants()
        w2 = w2_ref[...].astype(BF16)
        gn = gn_ref[...]
        q = proj_ref[:, P_GQ:P_GQ + GLA_QK]
        k = proj_ref[:, P_GK:P_GK + GLA_QK]
        v_b = proj_ref[:, P_GV:P_GV + GLA_WIDTH].astype(BF16)
        r = proj_ref[:, P_GR:P_GR + GLA_WIDTH]
        glow = proj_ref[:, P_GLOW:P_GLOW + LANE].astype(BF16)
        o = opre_ref[...]
        doa = doa_ref[...]
        gpre, eb, enb, eke, ebl, qd, ki, ke = _gla_step_common(q, k, glow, w2, bg_ref[...], causal.astype(BF16))
        sig = _sigmoid(r)
        rs = jnp.concatenate([jnp.broadcast_to(_rstd(o[:, _head_cols(h)]), (GLA_ROWS, GLA_DV)) for h in range(GLA_HEADS)], axis=1)
        on = o * rs
        d_ong = doa * (r * sig)
        dr = doa * (on * gn) * (sig * (1.0 + r * (1.0 - sig)))
        dgn_ref[...] += jnp.sum(d_ong * on, axis=0, keepdims=True)
        d_on = d_ong * gn
        t = d_on * on
        mean_t = jnp.concatenate([jnp.broadcast_to(jnp.mean(t[:, _head_cols(h)], axis=-1, keepdims=True), (GLA_ROWS, GLA_DV))
                                  for h in range(GLA_HEADS)], axis=1)
        do_b = (rs * (d_on - on * mean_t)).astype(BF16)
        ki_b = ki.astype(BF16)
        ke_b = ke.astype(BF16)
        dqd = jnp.zeros_like(qd)
        dki = jnp.zeros_like(qd)
        dv_heads = []
        for h in range(GLA_HEADS):
            qd_h = jnp.where(masks[h], qd, 0.0).astype(BF16)
            do_h = do_b[:, _head_cols(h)]

            def intra(qd_g, ki_g, v_g, do_g):
                att = jnp.where(causal, _dot_nt(qd_g, ki_g), 0.0).astype(BF16)
                d_att = jnp.where(causal, _dot_nt(do_g, v_g), 0.0).astype(BF16)
                return _dot_tn(att, do_g), _dot(d_att, ki_g), _dot_tn(d_att, qd_g)

            dv_h, dqd_h, dki_h = _by_group(intra, qd_h, ki_b, v_b[:, _head_cols(h)], do_h)
            dv_heads.append(dv_h)
            dqd = dqd + jnp.where(masks[h], dqd_h, 0.0)
            dki = dki + dki_h
        states = [sprev_ref[c] for c in range(GLA_CHUNKS_PER_STEP)]
        dqd_inter, dst_adds = [], []
        for c in range(GLA_CHUNKS_PER_STEP):
            do_c = _stack_head_cols(_chunk_rows_of(do_b, c))
            dqd_inter.append(_row_blocks_masked(_dot(do_c, states[c].astype(BF16)), masks))
            dst_adds.append(_diag_blocks(_dot_tn(_chunk_rows_of(do_b, c), _chunk_rows_of(qd, c).astype(BF16)), masks))
        dst = dst_ref[...]
        dsts, debls = [None] * GLA_CHUNKS_PER_STEP, [None] * GLA_CHUNKS_PER_STEP
        for c in reversed(range(GLA_CHUNKS_PER_STEP)):
            dsts[c] = dst
            debls[c] = jnp.sum(dst * states[c], axis=0, keepdims=True)
            dst = dst * ebl[c] + dst_adds[c]
        dst_ref[...] = dst
        dv_inter, dke = [], []
        for c in range(GLA_CHUNKS_PER_STEP):
            dst_b = dsts[c].astype(BF16)
            got = _dot_nt(_stack_masked(_chunk_rows_of(ke, c), masks).astype(BF16), dst_b)
            dv_inter.append(jnp.concatenate([got[h * GLA_CHUNK:(h + 1) * GLA_CHUNK] for h in range(GLA_HEADS)], axis=1))
            dke.append(_row_blocks_masked(_dot(_stack_head_cols(_chunk_rows_of(v_b, c)), dst_b), masks))
        dqd = dqd + jnp.concatenate(dqd_inter, axis=0)
        dke = jnp.concatenate(dke, axis=0)
        dv = jnp.concatenate(dv_heads, axis=1) + jnp.concatenate(dv_inter, axis=0)
        dkk = dke * ke
        dbl = jnp.sum(_per_chunk(dkk), axis=1, keepdims=True) + jnp.stack(debls) * ebl
        last_row = lax.broadcasted_iota(jnp.int32, (GLA_CHUNKS_PER_STEP, GLA_CHUNK, GLA_QK), 1) == GLA_CHUNK - 1
        db = dqd * qd - dki * ki - dkk + jnp.where(last_row, dbl, 0.0).reshape(GLA_ROWS, GLA_QK)
        tri_t = causal_t.astype(BF16)
        dglog = _by_group(lambda g: _dot_exact_lhs(tri_t, g), db)
        dgpre = (dglog / GLA_TAU) * _sigmoid(-gpre)
        dgpre_b = dgpre.astype(BF16)
        da_ref[...] = jnp.concatenate(
            [dqd * eb * QK_SCALE, dki * enb + dke * eke, dv, dr, _dot_nt(dgpre_b, w2)], axis=1).astype(BF16)
        dw2_ref[...] += _dot_tn(glow, dgpre_b)
        dbg_ref[...] += jnp.sum(dgpre, axis=0, keepdims=True)

    rev = lambda i: (n_steps - 1 - i, 0)
    return _hosted_call(
        exchange,
        body,
        grid=(n_steps,),
        in_specs=[
            pl.BlockSpec((GLA_ROWS, P_ALL), rev),
            pl.BlockSpec((LANE, GLA_QK), lambda i: (0, 0)),
            pl.BlockSpec((1, GLA_QK), lambda i: (0, 0)),
            pl.BlockSpec((1, GLA_WIDTH), lambda i: (0, 0)),
            pl.BlockSpec((GLA_ROWS, GLA_WIDTH), rev),
            pl.BlockSpec((GLA_CHUNKS_PER_STEP, GLA_DV, GLA_QK), lambda i: (n_steps - 1 - i, 0, 0)),
            pl.BlockSpec((GLA_ROWS, GLA_WIDTH), rev),
        ],
        out_specs=[
            pl.BlockSpec((GLA_ROWS, P_A), rev),
            pl.BlockSpec((LANE, GLA_QK), lambda i: (0, 0)),
            pl.BlockSpec((1, GLA_QK), lambda i: (0, 0)),
            pl.BlockSpec((1, GLA_WIDTH), lambda i: (0, 0)),
        ],
        out_shape=[
            jax.ShapeDtypeStruct((T, P_A), BF16),
            jax.ShapeDtypeStruct((LANE, GLA_QK), F32),
            jax.ShapeDtypeStruct((1, GLA_QK), F32),
            jax.ShapeDtypeStruct((1, GLA_WIDTH), F32),
        ],
        scratch_shapes=[pltpu.VMEM((GLA_DV, GLA_QK), F32)],
        compiler_params=_params(),
        name="gla_bwd",
        args=(proj, w2p, bg, gn, opre, sprev, dmixed),
    )


def _t5_bucket(dist):
    max_exact = REL_BUCKETS // 2
    n = np.maximum(dist, 0)
    large = max_exact + (np.log(np.maximum(n, 1) / max_exact) / math.log(REL_MAX_DIST / max_exact)
                         * (REL_BUCKETS - max_exact)).astype(np.int32)
    large = np.minimum(large, REL_BUCKETS - 1)
    return np.where(n < max_exact, n, large).astype(np.int32)


SUBLANES = 8


def _bucket_rows():
    steps = DSA_BLOCK - np.arange(2 * DSA_BLOCK)
    in_band = (steps >= 0) & (steps <= DSA_SPAN)
    rows = np.stack([np.where(in_band, _t5_bucket(steps * d), -1) for d in DSA_DILATIONS]).astype(np.int32)
    return np.broadcast_to(rows[:, None, :], (len(DSA_DILATIONS), SUBLANES, 2 * DSA_BLOCK)).copy()


def bias_tables(rel_bias):
    ids = jnp.asarray(_bucket_rows())
    nd = len(DSA_DILATIONS)

    def body(rel_ref, ids_ref, tab_ref):
        h = pl.program_id(1)
        idt = ids_ref[0]
        row = jnp.where(idt < 0, NEG, 0.0).astype(F32)
        for b in range(REL_BUCKETS):
            row = jnp.where(idt == b, rel_ref[b, h], row)
        full = jnp.broadcast_to(row[0:1], (DSA_BLOCK, 2 * DSA_BLOCK))
        tab_ref[0, 0] = pltpu.roll(full, 0, 1, stride=1, stride_axis=0)

    return pl.pallas_call(
        body,
        grid=(nd, DSA_HEADS),
        in_specs=[pl.BlockSpec(memory_space=pltpu.SMEM), pl.BlockSpec((1, SUBLANES, 2 * DSA_BLOCK), lambda d, h: (d, 0, 0))],
        out_specs=pl.BlockSpec((1, 1, DSA_BLOCK, 2 * DSA_BLOCK), lambda d, h: (d, h, 0, 0)),
        out_shape=jax.ShapeDtypeStruct((nd, DSA_HEADS, DSA_BLOCK, 2 * DSA_BLOCK), F32),
        compiler_params=_params(("arbitrary", "arbitrary")),
        name="bias_tables",
    )(rel_bias, ids)


def _bucket_ids():
    L = DSA_BLOCK
    steps = L + np.arange(L)[:, None] - np.arange(2 * L)[None, :]
    in_band = (steps >= 0) & (steps <= DSA_SPAN)
    return np.stack([np.where(in_band, _t5_bucket(steps * d), -1) for d in DSA_DILATIONS]).astype(np.int32)


def bias_tables_bwd(dtab):
    ids = jnp.asarray(_bucket_ids())
    nd = len(DSA_DILATIONS)

    def body(dtab_ref, ids_ref, drel_ref):
        @pl.when((pl.program_id(0) == 0) & (pl.program_id(1) == 0))
        def _():
            for b in range(REL_BUCKETS):
                for h in range(DSA_HEADS):
                    drel_ref[b, h] = 0.0

        h = pl.program_id(1)
        idt = ids_ref[0]
        g = dtab_ref[0, 0]
        for b in range(REL_BUCKETS):
            drel_ref[b, h] += jnp.sum(jnp.where(idt == b, g, 0.0))

    return pl.pallas_call(
        body,
        grid=(nd, DSA_HEADS),
        in_specs=[pl.BlockSpec((1, 1, DSA_BLOCK, 2 * DSA_BLOCK), lambda d, h: (d, h, 0, 0)),
                  pl.BlockSpec((1, DSA_BLOCK, 2 * DSA_BLOCK), lambda d, h: (d, 0, 0))],
        out_specs=pl.BlockSpec(memory_space=pltpu.SMEM),
        out_shape=jax.ShapeDtypeStruct((REL_BUCKETS, DSA_HEADS), F32),
        compiler_params=_params(("arbitrary", "arbitrary")),
        name="bias_tables_bwd",
    )(dtab, ids)


DSA_PAIRS = DSA_HEADS // 2
DSA_UNROLL = 8
DSA_COMBINE_ROWS = 256


def _dsa_units(d):
    return d, DSA_SUPER // (DSA_BLOCK * d)


def _dsa_specs(T):
    nsb = T // DSA_SUPER
    qcol, kcol, vcol = P_DQ // LANE, P_DK // LANE, P_DV // LANE
    return nsb, qcol, kcol, vcol


def _head_lane_mask():
    return lax.broadcasted_iota(jnp.int32, (1, LANE), 1) < DSA_DH


def _first_block_penalty(first):
    col = lax.broadcasted_iota(jnp.int32, (2 * DSA_BLOCK, 2 * DSA_BLOCK), 1)
    return jnp.where(first & (col < DSA_BLOCK), NEG, 0.0).astype(F32)


def _pair_tiles(tab):
    return tab.reshape(len(DSA_DILATIONS), DSA_PAIRS, 2 * DSA_BLOCK, 2 * DSA_BLOCK)


def _stack_heads(t, head0):
    return jnp.concatenate([jnp.where(head0, t, 0.0), jnp.where(head0, 0.0, t)], axis=0)


def dsa_fwd(proj, tab, exchange=None):
    T = proj.shape[0]
    nsb, qcol, kcol, vcol = _dsa_specs(T)
    S = DSA_SUPER

    def body(*refs):
        refs = _host_exchange(exchange, refs, 6, 2, pl.program_id(0) * nsb + pl.program_id(1), DSA_PAIRS * nsb)
        q_ref, kp_ref, kc_ref, vp_ref, vc_ref, tab_ref, out_ref, lse_ref, kk, vv, ob, lb = refs
        sb = pl.program_id(1)
        kk[0:S, :] = kp_ref[...]
        kk[S:2 * S, :] = kc_ref[...]
        vv[0:S, :] = vp_ref[...]
        vv[S:2 * S, :] = vc_ref[...]
        head0 = _head_lane_mask()

        for di, d in enumerate(DSA_DILATIONS):
            n_res, n_blk = _dsa_units(d)

            def unit(u, carry, di=di, d=d, n_blk=n_blk):
                r = u // n_blk
                c = u % n_blk
                q0 = r + d * DSA_BLOCK * c
                qrows = pl.ds(q0, DSA_BLOCK, stride=d) if d > 1 else pl.ds(q0, DSA_BLOCK)
                krows = pl.ds(S + q0 - d * DSA_BLOCK, 2 * DSA_BLOCK, stride=d) if d > 1 else pl.ds(S + q0 - DSA_BLOCK, 2 * DSA_BLOCK)
                q2 = q_ref[qrows, :] * QK_SCALE
                k2 = kk[krows, :].astype(BF16)
                v2 = vv[krows, :].astype(BF16)
                qs = _stack_heads(q2, head0).astype(BF16)
                s = _dot_nt(qs, k2) + (tab_ref[di, 0] + _first_block_penalty((sb == 0) & (c == 0)))
                m = jnp.max(s, axis=-1, keepdims=True)
                p = jnp.exp(s - m)
                den = jnp.sum(p, axis=-1, keepdims=True)
                o = _dot(p.astype(BF16), v2) / den
                l = jnp.broadcast_to(m + jnp.log(den), (2 * DSA_BLOCK, LANE))
                ob[di, qrows, :] = jnp.where(head0, o[:DSA_BLOCK], o[DSA_BLOCK:])
                lb[di, qrows, :] = jnp.where(head0, l[:DSA_BLOCK], l[DSA_BLOCK:])
                return carry

            lax.fori_loop(0, n_res * n_blk, unit, 0, unroll=DSA_UNROLL)

        def combine(i, carry):
            rows = pl.ds(pl.multiple_of(i * DSA_COMBINE_ROWS, DSA_COMBINE_ROWS), DSA_COMBINE_ROWS)
            l0, l1, l2 = lb[0, rows, :], lb[1, rows, :], lb[2, rows, :]
            mx = jnp.maximum(jnp.maximum(l0, l1), l2)
            e0, e1, e2 = jnp.exp(l0 - mx), jnp.exp(l1 - mx), jnp.exp(l2 - mx)
            den = e0 + e1 + e2
            out_ref[rows, :] = (e0 * ob[0, rows, :] + e1 * ob[1, rows, :] + e2 * ob[2, rows, :]) / den
            lse_ref[rows, :] = mx + jnp.log(den)
            return carry

        lax.fori_loop(0, S // DSA_COMBINE_ROWS, combine, 0)

    prev = lambda col: (lambda hp, sb: (jnp.maximum(sb - 1, 0), col + hp))
    cur = lambda col: (lambda hp, sb: (sb, col + hp))
    blk = lambda f: pl.BlockSpec((S, LANE), f)
    return _hosted_call(
        exchange,
        body,
        grid=(DSA_PAIRS, nsb),
        in_specs=[blk(cur(qcol)), blk(prev(kcol)), blk(cur(kcol)), blk(prev(vcol)), blk(cur(vcol)),
                  pl.BlockSpec((len(DSA_DILATIONS), 1, 2 * DSA_BLOCK, 2 * DSA_BLOCK), lambda hp, sb: (0, hp, 0, 0))],
        out_specs=[blk(lambda hp, sb: (sb, hp)), blk(lambda hp, sb: (sb, hp))],
        out_shape=[jax.ShapeDtypeStruct((T, DSA_WIDTH), F32), jax.ShapeDtypeStruct((T, DSA_WIDTH), F32)],
        scratch_shapes=[pltpu.VMEM((2 * S, LANE), F32), pltpu.VMEM((2 * S, LANE), F32),
                        pltpu.VMEM((len(DSA_DILATIONS), S, LANE), F32), pltpu.VMEM((len(DSA_DILATIONS), S, LANE), F32)],
        compiler_params=_params(("arbitrary", "arbitrary")),
        name="dsa_fwd",
        args=(proj, proj, proj, proj, proj, _pair_tiles(tab)),
    )


def dsa_bwd(proj, tab, ob_out, lse, dmixed, exchange=None):
    T = proj.shape[0]
    nsb, qcol, kcol, vcol = _dsa_specs(T)
    S = DSA_SUPER
    nd = len(DSA_DILATIONS)
    ocol = GLA_WIDTH // LANE

    def body(*refs):
        refs = _host_exchange(exchange, refs, 9, 4, pl.program_id(0) * nsb + pl.program_id(1), DSA_PAIRS * nsb)
        (q_ref, kp_ref, kc_ref, vp_ref, vc_ref, tab_ref, o_ref, lse_ref, do_ref,
         dq_ref, dk_ref, dv_ref, dtab_ref, kk, vv, dqa, dkk, dvv) = refs
        j = pl.program_id(1)
        sb = nsb - 1 - j
        kk[0:S, :] = kp_ref[...]
        kk[S:2 * S, :] = kc_ref[...]
        vv[0:S, :] = vp_ref[...]
        vv[S:2 * S, :] = vc_ref[...]
        head0 = _head_lane_mask()

        @pl.when(j == 0)
        def _():
            dtab_ref[...] = jnp.zeros_like(dtab_ref)
            dkk[S:2 * S, :] = jnp.zeros((S, LANE), F32)
            dvv[S:2 * S, :] = jnp.zeros((S, LANE), F32)

        @pl.when(j > 0)
        def _():
            dkk[S:2 * S, :] = dkk[0:S, :]
            dvv[S:2 * S, :] = dvv[0:S, :]

        dkk[0:S, :] = jnp.zeros((S, LANE), F32)
        dvv[0:S, :] = jnp.zeros((S, LANE), F32)
        dqa[...] = jnp.zeros_like(dqa)

        for di, d in enumerate(DSA_DILATIONS):
            n_res, n_blk = _dsa_units(d)

            def unit(u, carry, di=di, d=d, n_blk=n_blk):
                r = u // n_blk
                c = u % n_blk
                q0 = r + d * DSA_BLOCK * c
                qrows = pl.ds(q0, DSA_BLOCK, stride=d) if d > 1 else pl.ds(q0, DSA_BLOCK)
                krows = pl.ds(S + q0 - d * DSA_BLOCK, 2 * DSA_BLOCK, stride=d) if d > 1 else pl.ds(S + q0 - DSA_BLOCK, 2 * DSA_BLOCK)
                q2 = q_ref[qrows, :] * QK_SCALE
                k2 = kk[krows, :].astype(BF16)
                v2 = vv[krows, :].astype(BF16)
                do2 = do_ref[qrows, :]
                o2 = o_ref[qrows, :]
                l2 = lse_ref[qrows, :]
                qs = _stack_heads(q2, head0).astype(BF16)
                dos = _stack_heads(do2, head0)
                dos_b = dos.astype(BF16)
                delta = jnp.sum(dos * jnp.concatenate([o2, o2], axis=0), axis=-1, keepdims=True)
                lse = jnp.concatenate([jnp.max(jnp.where(head0, l2, -jnp.inf), axis=-1, keepdims=True),
                                       jnp.max(jnp.where(head0, -jnp.inf, l2), axis=-1, keepdims=True)], axis=0)
                s = _dot_nt(qs, k2) + (tab_ref[di, 0] + _first_block_penalty((sb == 0) & (c == 0)))
                p = jnp.exp(s - lse)
                ds = p * (_dot_nt(dos_b, v2) - delta)
                dtab_ref[di, 0] += ds
                ds_b = ds.astype(BF16)
                dq = _dot(ds_b, k2)
                dqa[qrows, :] += jnp.where(head0, dq[:DSA_BLOCK], dq[DSA_BLOCK:]) * QK_SCALE
                dkk[krows, :] += _dot_tn(ds_b, qs)
                dvv[krows, :] += _dot_tn(p.astype(BF16), dos_b)
                return carry

            lax.fori_loop(0, n_res * n_blk, unit, 0, unroll=DSA_UNROLL)

        dq_ref[...] = dqa[...].astype(BF16)
        dk_ref[...] = dkk[S:2 * S, :].astype(BF16)
        dv_ref[...] = dvv[S:2 * S, :].astype(BF16)

    prev = lambda col: (lambda hp, j: (jnp.maximum(nsb - 2 - j, 0), col + hp))
    cur = lambda col: (lambda hp, j: (nsb - 1 - j, col + hp))
    blk = lambda f: pl.BlockSpec((S, LANE), f)
    out_blk = blk(lambda hp, j: (nsb - 1 - j, hp))
    tab_blk = pl.BlockSpec((nd, 1, 2 * DSA_BLOCK, 2 * DSA_BLOCK), lambda hp, j: (0, hp, 0, 0))
    dq, dk, dv, dtab, *carried = _hosted_call(
        exchange,
        body,
        grid=(DSA_PAIRS, nsb),
        in_specs=[blk(cur(qcol)), blk(prev(kcol)), blk(cur(kcol)), blk(prev(vcol)), blk(cur(vcol)), tab_blk,
                  out_blk, out_blk, blk(cur(ocol))],
        out_specs=[out_blk, out_blk, out_blk, tab_blk],
        out_shape=[jax.ShapeDtypeStruct((T, DSA_WIDTH), BF16)] * 3
        + [jax.ShapeDtypeStruct((nd, DSA_PAIRS, 2 * DSA_BLOCK, 2 * DSA_BLOCK), F32)],
        scratch_shapes=[pltpu.VMEM((2 * S, LANE), F32), pltpu.VMEM((2 * S, LANE), F32), pltpu.VMEM((S, LANE), F32),
                        pltpu.VMEM((2 * S, LANE), F32), pltpu.VMEM((2 * S, LANE), F32)],
        compiler_params=_params(("arbitrary", "arbitrary")),
        name="dsa_bwd",
        args=(proj, proj, proj, proj, proj, _pair_tiles(tab), ob_out, lse, dmixed),
    )
    return (dq, dk, dv, dtab.reshape(nd, DSA_HEADS, DSA_BLOCK, 2 * DSA_BLOCK), *carried)


FF_BLOCKS = 4
FF_BLOCK = D_FF // FF_BLOCKS


def post_fused(x, oa, ob, tgt, g2, gf, wout, wff1, wff2):
    T = x.shape[0]
    tm = 256
    inv_d = 1.0 / D_MODEL

    def body(x_ref, oa_ref, ob_ref, tgt_ref, g2_ref, gf_ref, wout_hbm, wff1_hbm, wff2_hbm,
             mixed_ref, nm_ref, a_ref, dpre_ref, dh2_ref, dh1_ref, dmixed_ref, loss_ref, dgf_ref, dg2_ref,
             wout_v, wff1_v, wff2_v, sems):
        @pl.when(pl.program_id(0) == 0)
        def _():
            cps = [pltpu.make_async_copy(s, d, sems.at[i])
                   for i, (s, d) in enumerate([(wout_hbm, wout_v), (wff1_hbm, wff1_v), (wff2_hbm, wff2_v)])]
            for cp in cps:
                cp.start()
            for cp in cps:
                cp.wait()
            loss_ref[...] = jnp.zeros_like(loss_ref)
            dgf_ref[...] = jnp.zeros_like(dgf_ref)
            dg2_ref[...] = jnp.zeros_like(dg2_ref)

        mixed = jnp.concatenate([oa_ref[...], ob_ref[...].astype(BF16)], axis=1)
        mixed_ref[...] = mixed
        h1 = x_ref[...] + _dot(mixed, wout_v[...])
        rs1 = _rstd(h1)
        hn1 = h1 * rs1
        g2 = g2_ref[...]
        nm = (hn1 * g2).astype(BF16)
        nm_ref[...] = nm
        relu = []
        mlp = jnp.zeros((tm, D_MODEL), F32)
        for j in range(FF_BLOCKS):
            cols = slice(j * FF_BLOCK, (j + 1) * FF_BLOCK)
            r_j = jnp.maximum(_dot(nm, wff1_v[j]), 0.0)
            a_j = (r_j * r_j).astype(BF16)
            a_ref[:, cols] = a_j
            relu.append(r_j)
            mlp = mlp + _dot(a_j, wff2_v[cols, :])
        h2 = h1 + mlp
        rsf = _rstd(h2)
        hnf = h2 * rsf
        gf = gf_ref[...]
        diff = hnf * gf - tgt_ref[...]
        loss_ref[...] += 0.5 * jnp.sum(jnp.sum(diff * diff, axis=-1, keepdims=True) * inv_d, axis=0, keepdims=True)
        dy = diff * inv_d
        dgf_ref[...] += jnp.sum(dy * hnf, axis=0, keepdims=True)
        dhnf = dy * gf
        dh2 = rsf * (dhnf - hnf * jnp.mean(dhnf * hnf, axis=-1, keepdims=True))
        dh2_b = dh2.astype(BF16)
        dh2_ref[...] = dh2_b
        dnm = jnp.zeros((tm, D_MODEL), F32)
        for j in range(FF_BLOCKS):
            cols = slice(j * FF_BLOCK, (j + 1) * FF_BLOCK)
            dpre_j = (_dot_nt(dh2_b, wff2_v[cols, :]) * (2.0 * relu[j])).astype(BF16)
            dpre_ref[:, cols] = dpre_j
            dnm = dnm + _dot_nt(dpre_j, wff1_v[j])
        dg2_ref[...] += jnp.sum(dnm * hn1, axis=0, keepdims=True)
        dhn1 = dnm * g2
        dh1 = dh2 + rs1 * (dhn1 - hn1 * jnp.mean(dhn1 * hn1, axis=-1, keepdims=True))
        dh1_ref[...] = dh1
        dmixed_ref[...] = _dot_nt(dh1.astype(BF16), wout_v[...])

    row = lambda w: pl.BlockSpec((tm, w), lambda i: (i, 0))
    vec = lambda w: pl.BlockSpec((1, w), lambda i: (0, 0))
    return pl.pallas_call(
        body,
        grid=(T // tm,),
        in_specs=[row(D_MODEL), row(GLA_WIDTH), row(DSA_WIDTH), row(D_MODEL), vec(D_MODEL), vec(D_MODEL), ANY, ANY, ANY],
        out_specs=[row(D_MODEL), row(D_MODEL), row(D_FF), row(D_FF), row(D_MODEL), row(D_MODEL), row(D_MODEL),
                   vec(1), vec(D_MODEL), vec(D_MODEL)],
        out_shape=[
            jax.ShapeDtypeStruct((T, D_MODEL), BF16),
            jax.ShapeDtypeStruct((T, D_MODEL), BF16),
            jax.ShapeDtypeStruct((T, D_FF), BF16),
            jax.ShapeDtypeStruct((T, D_FF), BF16),
            jax.ShapeDtypeStruct((T, D_MODEL), BF16),
            jax.ShapeDtypeStruct((T, D_MODEL), F32),
            jax.ShapeDtypeStruct((T, D_MODEL), F32),
            jax.ShapeDtypeStruct((1, 1), F32),
            jax.ShapeDtypeStruct((1, D_MODEL), F32),
            jax.ShapeDtypeStruct((1, D_MODEL), F32),
        ],
        scratch_shapes=[pltpu.VMEM((D_MODEL, D_MODEL), BF16), pltpu.VMEM((FF_BLOCKS, D_MODEL, FF_BLOCK), BF16),
                        pltpu.VMEM((D_FF, D_MODEL), BF16), pltpu.SemaphoreType.DMA((3,))],
        compiler_params=_params(),
        name="post_fused",
    )(x, oa, ob, tgt, g2, gf, wout, wff1, wff2)


WGRAD_TOKENS = 2048


def wgrad(a, b, name, bm=None, bn=None, col_blocked=False):
    T, M = a.shape
    N = b.shape[1]
    bm = M if bm is None else bm
    bn = N if bn is None else bn
    tk = min(WGRAD_TOKENS, T)

    def body(a_ref, b_ref, o_ref):
        part = _dot_tn(a_ref[...].astype(BF16), b_ref[...].astype(BF16))
        out = o_ref.at[0] if col_blocked else o_ref

        @pl.when(pl.program_id(2) == 0)
        def _():
            out[...] = part

        @pl.when(pl.program_id(2) > 0)
        def _():
            out[...] += part

    if col_blocked:
        assert bm == M
        out_spec = pl.BlockSpec((1, M, bn), lambda i, j, k: (j, 0, 0))
        out_shape = jax.ShapeDtypeStruct((N // bn, M, bn), F32)
    else:
        out_spec = pl.BlockSpec((bm, bn), lambda i, j, k: (i, j))
        out_shape = jax.ShapeDtypeStruct((M, N), F32)
    return pl.pallas_call(
        body,
        grid=(M // bm, N // bn, T // tk),
        in_specs=[pl.BlockSpec((tk, bm), lambda i, j, k: (k, i)), pl.BlockSpec((tk, bn), lambda i, j, k: (k, j))],
        out_specs=out_spec,
        out_shape=out_shape,
        compiler_params=_params(("arbitrary", "arbitrary", "arbitrary")),
        name=name,
    )(a, b)


def dx_final(x, dh1, g1, da, dq, dk, dv, wp, exchange=None):
    T = x.shape[0]
    tm = 256

    def body(*refs):
        refs = _host_exchange(exchange, refs, 8, 2, pl.program_id(0), T // tm)
        x_ref, dh1_ref, g_ref, da_ref, dq_ref, dk_ref, dv_ref, w_hbm, dx_ref, dg_ref, w_vmem, sem = refs

        @pl.when(pl.program_id(0) == 0)
        def _():
            _load_once(w_hbm, w_vmem, sem)
            dg_ref[...] = jnp.zeros_like(dg_ref)

        dnx = (_dot_nt(da_ref[...], w_vmem[:, 0:P_A]) + _dot_nt(dq_ref[...], w_vmem[:, P_DQ:P_DQ + DSA_WIDTH])
               + _dot_nt(dk_ref[...], w_vmem[:, P_DK:P_DK + DSA_WIDTH]) + _dot_nt(dv_ref[...], w_vmem[:, P_DV:P_DV + DSA_WIDTH]))
        xf = x_ref[...]
        rs = _rstd(xf)
        hn = xf * rs
        dg_ref[...] += jnp.sum(dnx * hn, axis=0, keepdims=True)
        dhn = dnx * g_ref[...]
        dx_ref[...] = dh1_ref[...] + rs * (dhn - hn * jnp.mean(dhn * hn, axis=-1, keepdims=True))

    row = lambda w: pl.BlockSpec((tm, w), lambda i: (i, 0))
    vec = pl.BlockSpec((1, D_MODEL), lambda i: (0, 0))
    return _hosted_call(
        exchange,
        body,
        grid=(T // tm,),
        in_specs=[row(D_MODEL), row(D_MODEL), vec, row(P_A), row(DSA_WIDTH), row(DSA_WIDTH), row(DSA_WIDTH), ANY],
        out_specs=[row(D_MODEL), vec],
        out_shape=[jax.ShapeDtypeStruct((T, D_MODEL), F32), jax.ShapeDtypeStruct((1, D_MODEL), F32)],
        scratch_shapes=[pltpu.VMEM((D_MODEL, P_ALL), BF16), pltpu.SemaphoreType.DMA],
        compiler_params=_params(),
        name="dx_final",
        args=(x, dh1, g1, da, dq, dk, dv, wp),
    )


def adamw(w, g, m, v, name):
    R, C = w.shape
    br = 256 if R % 256 == 0 else R

    def body(w_ref, g_ref, m_ref, v_ref, d_ref, nm_ref, nv_ref):
        d_ref[...], nm_ref[...], nv_ref[...] = _adamw_math(w_ref[...], g_ref[...], m_ref[...], v_ref[...])

    spec = pl.BlockSpec((br, C), lambda i: (i, 0))
    return pl.pallas_call(
        body,
        grid=(R // br,),
        in_specs=[spec] * 4,
        out_specs=[spec] * 3,
        out_shape=[jax.ShapeDtypeStruct((R, C), F32)] * 3,
        compiler_params=_params(),
        name=name,
    )(w, g, m, v)


def _place():
    return lax.axis_index("x"), lax.axis_index("y"), lax.axis_index("c")


def _other_chips(x, y):
    return [(1 - x, y), (x, 1 - y), (1 - x, 1 - y)]


class Exchange:
    def __init__(self, kind, arrays):
        self.kind, self.arrays, self.n = kind, arrays, len(arrays)

    def out_shapes(self):
        if self.kind == "gather":
            return [jax.ShapeDtypeStruct((4,) + s.shape, s.dtype) for s in self.arrays]
        if self.kind == "scatter":
            return [jax.ShapeDtypeStruct((4, 2 * s.shape[1], s.shape[2]), s.dtype) for s in self.arrays]
        return [jax.ShapeDtypeStruct((4, s.shape[1] // 2, s.shape[2]), s.dtype) for s in self.arrays]

    def sems(self):
        return [pltpu.SemaphoreType.DMA((self.n, 7)), pltpu.SemaphoreType.DMA((self.n, 7))]

    def phases(self, ins, outs, send_sems, recv_sems):
        n, kind = self.n, self.kind
        x, y, c = _place()
        me, sib = (x, y, c), (x, y, 1 - c)
        mine = 2 * x + y
        chips = _other_chips(x, y)

        def region(a, owner, half):
            h = outs[a].shape[1] // 2
            return outs[a].at[owner, pl.ds(half * h, h)]

        def copy(a, k, owner, half, to, src=None):
            return pltpu.make_async_remote_copy(
                src_ref=region(a, owner, half) if src is None else src, dst_ref=region(a, owner, half),
                send_sem=send_sems.at[a, k], recv_sem=recv_sems.at[a, k], device_id=to, device_id_type=MESH)

        def swap_copy(a, to):
            h = ins[a].shape[1] // 2
            return pltpu.make_async_remote_copy(
                src_ref=ins[a].at[:, pl.ds((1 - c) * h, h)], dst_ref=outs[a],
                send_sem=send_sems.at[a, 0], recv_sem=recv_sems.at[a, 0], device_id=to, device_id_type=MESH)

        def first_copies():
            if kind == "swap":
                return [swap_copy(a, sib) for a in range(n)]
            cps = []
            for a in range(n):
                h = outs[a].shape[1] // 2
                for t, (cx, cy) in enumerate(chips):
                    src = ins[a].at[2 * cx + cy] if kind == "scatter" else ins[a].at[pl.ds(c * h, h)]
                    cps.append(copy(a, t, mine, c, (cx, cy, c), src=src))
                if kind == "scatter":
                    cps.append(copy(a, 6, mine, c, sib, src=ins[a].at[mine]))
            return cps

        def forward_copies():
            if kind == "swap":
                return []
            return [copy(a, 3 + t, 2 * cx + cy, c, sib) for a in range(n) for t, (cx, cy) in enumerate(chips)]

        def start():
            for cp in first_copies():
                cp.start()

        def forward():
            if kind == "swap":
                return
            fws = forward_copies()
            for a in range(n):
                for t, (cx, cy) in enumerate(chips):
                    copy(a, t, 2 * cx + cy, c, me).wait_recv()
                    fws[3 * a + t].start()

        def finish():
            for a in range(n):
                if kind == "swap":
                    swap_copy(a, me).wait_recv()
                    continue
                for t, (cx, cy) in enumerate(chips):
                    copy(a, 3 + t, 2 * cx + cy, 1 - c, me).wait_recv()
                if kind == "scatter":
                    copy(a, 6, mine, 1 - c, me).wait_recv()
            for cp in first_copies() + forward_copies():
                cp.wait_send()

        return start, forward, finish

    def fill_own(self, outs):
        if self.kind == "swap":
            return list(outs)
        x, y, c = _place()
        if self.kind == "gather":
            return [lax.dynamic_update_index_in_dim(o, s, 2 * x + y, 0) for o, s in zip(outs, self.arrays)]
        filled = []
        for o, s in zip(outs, self.arrays):
            own = lax.dynamic_index_in_dim(s, 2 * x + y, 0, keepdims=True)
            filled.append(lax.dynamic_update_slice(o, own, (2 * x + y, c * s.shape[1], 0)))
        return filled

    def run(self, name):
        n = self.n

        def body(*refs):
            start, forward, finish = self.phases(refs[:n], refs[n:2 * n], *refs[2 * n:])
            start()
            forward()
            finish()

        outs = pl.pallas_call(
            body, in_specs=[ANY] * n, out_specs=[ANY] * n, out_shape=self.out_shapes(), scratch_shapes=self.sems(), name=name,
        )(*self.arrays)
        return self.fill_own(outs)


def _host_exchange(exchange, refs, n_in, n_out, step, n_steps):
    if exchange is None:
        return refs
    n = exchange.n
    own_in, ex_in = refs[:n_in], refs[n_in:n_in + n]
    own_out, ex_out = refs[n_in + n:n_in + n + n_out], refs[n_in + n + n_out:n_in + 2 * n + n_out]
    rest = refs[n_in + 2 * n + n_out:]
    start, forward, finish = exchange.phases(ex_in, ex_out, rest[-2], rest[-1])
    pl.when(step == 0)(start)
    pl.when(step == (2 * n_steps) // 3)(forward)
    pl.when(step == n_steps - 1)(finish)
    return own_in + own_out + rest[:-2]


def _hosted_call(exchange, body, *, grid, in_specs, out_specs, out_shape, scratch_shapes, compiler_params, name, args):
    if exchange is None:
        return pl.pallas_call(body, grid=grid, in_specs=in_specs, out_specs=out_specs, out_shape=out_shape,
                              scratch_shapes=scratch_shapes, compiler_params=compiler_params, name=name)(*args)
    n = exchange.n
    res = pl.pallas_call(
        body, grid=grid, in_specs=list(in_specs) + [ANY] * n, out_specs=list(out_specs) + [ANY] * n,
        out_shape=list(out_shape) + exchange.out_shapes(), scratch_shapes=list(scratch_shapes) + exchange.sems(),
        compiler_params=compiler_params, name=name)(*args, *exchange.arrays)
    return list(res[:len(out_shape)]) + [exchange.fill_own(res[len(out_shape):])]


def add_halves(g, got, core, name):
    _, R, C = g.shape
    h = R // 2
    br = 128
    nb = h // br

    def body(core_ref, g_ref, got_ref, o_ref):
        o_ref[...] = (g_ref[...] + got_ref[...]).astype(BF16)

    return pl.pallas_call(
        body,
        grid_spec=pltpu.PrefetchScalarGridSpec(
            num_scalar_prefetch=1,
            grid=(4, nb),
            in_specs=[pl.BlockSpec((1, br, C), lambda s, i, core: (s, core[0] * nb + i, 0)),
                      pl.BlockSpec((1, br, C), lambda s, i, core: (s, i, 0))],
            out_specs=pl.BlockSpec((1, br, C), lambda s, i, core: (s, i, 0)),
        ),
        out_shape=jax.ShapeDtypeStruct((4, h, C), BF16),
        compiler_params=_params(("arbitrary", "arbitrary")),
        name=name,
    )(core, g, got)


def sum_slots(parts, name):
    S, R, C = parts.shape
    br = 128 if R % 128 == 0 else R

    def body(p_ref, o_ref):
        acc = p_ref[0].astype(F32)
        for s in range(1, S):
            acc = acc + p_ref[s].astype(F32)
        o_ref[...] = acc

    return pl.pallas_call(
        body,
        grid=(R // br,),
        in_specs=[pl.BlockSpec((S, br, C), lambda i: (0, i, 0))],
        out_specs=pl.BlockSpec((br, C), lambda i: (i, 0)),
        out_shape=jax.ShapeDtypeStruct((R, C), F32),
        compiler_params=_params(),
        name=name,
    )(parts)


def _adamw_math(w, g, m, v):
    m_new = ADAM_B1 * m + (1.0 - ADAM_B1) * g
    v_new = ADAM_B2 * v + (1.0 - ADAM_B2) * (g * g)
    m_hat = m_new / (1.0 - ADAM_B1 ** ADAM_STEP)
    v_hat = v_new / (1.0 - ADAM_B2 ** ADAM_STEP)
    return -ADAM_LR * (m_hat / (jnp.sqrt(v_hat) + ADAM_EPS) + ADAM_WD * w), m_new, v_new


def reduce_adamw(slots, w, m, v, name):
    S, R, C = slots.shape
    br = 128

    def body(p_ref, w_ref, m_ref, v_ref, g_ref, d_ref, nm_ref, nv_ref):
        g = p_ref[0].astype(F32)
        for s in range(1, S):
            g = g + p_ref[s].astype(F32)
        g_ref[...] = g
        d_ref[...], nm_ref[...], nv_ref[...] = _adamw_math(w_ref[...], g, m_ref[...], v_ref[...])

    spec = pl.BlockSpec((br, C), lambda i: (i, 0))
    return pl.pallas_call(
        body,
        grid=(R // br,),
        in_specs=[pl.BlockSpec((S, br, C), lambda i: (0, i, 0)), spec, spec, spec],
        out_specs=[spec] * 4,
        out_shape=[jax.ShapeDtypeStruct((R, C), F32)] * 4,
        compiler_params=_params(),
        name=name,
    )(slots, w, m, v)


SMALL_ROWS = 72


def gather_small(vec):
    def body(v_ref, o_ref, send_sems, recv_sems, local_sem):
        x, y, c = _place()
        flips = [(fx, fy, fc) for fx in (0, 1) for fy in (0, 1) for fc in (0, 1)][1:]

        def peer(f):
            return (1 - x if f[0] else x, 1 - y if f[1] else y, 1 - c if f[2] else c)

        slot = lambda p: 4 * p[0] + 2 * p[1] + p[2]
        own = pltpu.make_async_copy(v_ref, o_ref.at[slot((x, y, c))], local_sem)
        own.start()
        cps = [pltpu.make_async_remote_copy(
            src_ref=v_ref, dst_ref=o_ref.at[slot((x, y, c))], send_sem=send_sems.at[k], recv_sem=recv_sems.at[k],
            device_id=peer(f), device_id_type=MESH) for k, f in enumerate(flips)]
        for cp in cps:
            cp.start()
        for k, f in enumerate(flips):
            pltpu.make_async_remote_copy(
                src_ref=v_ref, dst_ref=o_ref.at[slot(peer(f))], send_sem=send_sems.at[k], recv_sem=recv_sems.at[k],
                device_id=(x, y, c), device_id_type=MESH).wait_recv()
        for cp in cps:
            cp.wait_send()
        own.wait()

    return pl.pallas_call(
        body,
        in_specs=[ANY],
        out_specs=ANY,
        out_shape=jax.ShapeDtypeStruct((8,) + vec.shape, vec.dtype),
        scratch_shapes=[pltpu.SemaphoreType.DMA((7,)), pltpu.SemaphoreType.DMA((7,)), pltpu.SemaphoreType.DMA],
        name="gather_small",
    )(vec)


GLOW_PAD = LANE - GLA_RANK


def kernel(x, attn_norm_g, w_in, gla_gate_w2, gla_gate_b, gla_norm_g, rel_bias, w_out, mlp_norm_g, w_ff1, w_ff2, final_norm_g, loss_target, m_attn_norm_g, m_w_in, m_gla_gate_w2, m_gla_gate_b, m_gla_norm_g, m_rel_bias, m_w_out, m_mlp_norm_g, m_w_ff1, m_w_ff2, m_final_norm_g, v_attn_norm_g, v_w_in, v_gla_gate_w2, v_gla_gate_b, v_gla_norm_g, v_rel_bias, v_w_out, v_mlp_norm_g, v_w_ff1, v_w_ff2, v_final_norm_g):
    xs, tgt = x[0], loss_target[0]
    T = xs.shape[0]
    cx, cy, cc = _place()
    chip = 2 * cx + cy
    gf = final_norm_g.reshape(1, D_MODEL)

    win_g, w2_g = Exchange("gather", [w_in[0].astype(BF16), gla_gate_w2[0]]).run("gather_w_in")
    win = jnp.transpose(win_g, (1, 0, 2)).reshape(D_MODEL, D_IN)
    n_glow = R_GLOW + GLA_RANK
    wp = jnp.concatenate([win[:, :n_glow], jnp.zeros((D_MODEL, GLOW_PAD), BF16), win[:, n_glow:]], axis=1)
    w2 = jnp.transpose(w2_g, (1, 0, 2)).reshape(GLA_RANK, GLA_QK)
    w2p = jnp.concatenate([w2, jnp.zeros((GLOW_PAD, GLA_QK), F32)], axis=0)

    proj, nx = inproj(xs, attn_norm_g, wp)
    tab = bias_tables(rel_bias)
    ob, lse, (wout_g, wff1, wff2_g) = dsa_fwd(
        proj, tab, Exchange("gather", [w_out[0].astype(BF16), w_ff1[0].astype(BF16), w_ff2[0].astype(BF16)]))
    wout = wout_g.reshape(D_MODEL, D_MODEL)
    wff2 = wff2_g.reshape(D_FF, D_MODEL)
    oa, opre, sprev = gla_fwd(proj, w2p, gla_gate_b, gla_norm_g)
    mixed, nm, act, dpre, dh2, dh1, dmixed, loss, dgf, dg2 = post_fused(xs, oa, ob, tgt, mlp_norm_g, gf, wout, wff1, wff2)

    core = cc.astype(jnp.int32).reshape(1)
    late = [
        wgrad(mixed, dh1, "wgrad_out").reshape(4, D_MODEL // 4, D_MODEL),
        wgrad(nm, dpre, "wgrad_ff1", bn=FF_BLOCK, col_blocked=True),
        wgrad(act, dh2, "wgrad_ff2", bm=FF_BLOCK).reshape(4, FF_BLOCK, D_MODEL),
    ]
    late_names = ["w_out", "w_ff1", "w_ff2"]
    da, dw2p, dbg, dgn, got = gla_bwd(proj, w2p, gla_gate_b, gla_norm_g, opre, sprev, dmixed, Exchange("swap", late))
    sums = [add_halves(g, r, core, "add_halves_" + s) for g, r, s in zip(late, got, late_names)]
    dq, dk, dv, dtab, late_slots = dsa_bwd(proj, tab, ob, lse, dmixed, Exchange("scatter", sums))
    slots = dict(zip(late_names, late_slots))
    drel = bias_tables_bwd(dtab)

    dwa = wgrad(nx, da, "wgrad_in_gla")
    dwq = wgrad(nx, dq, "wgrad_in_q")
    dwk = wgrad(nx, dk, "wgrad_in_k")
    dwv = wgrad(nx, dv, "wgrad_in_v")
    dwin = jnp.concatenate([dwa[:, :n_glow], dwq, dwk, dwv], axis=1)
    dwin = [jnp.transpose(dwin.reshape(D_MODEL, 4, D_IN // 4), (1, 0, 2))]
    got = Exchange("swap", dwin).run("swap_w_in")
    sums = [add_halves(dwin[0], got[0], core, "add_halves_w_in")]
    dxs, dg1, (slots["w_in"],) = dx_final(xs, dh1, attn_norm_g, da, dq, dk, dv, wp, Exchange("scatter", sums))

    sizes = [D_MODEL, GLA_QK, GLA_WIDTH, REL_BUCKETS * DSA_HEADS, D_MODEL, D_MODEL, GLA_RANK * GLA_QK, 1]
    small = jnp.concatenate([dg1.reshape(-1), dbg.reshape(-1), dgn.reshape(-1), drel.reshape(-1), dg2.reshape(-1),
                             dgf.reshape(-1), dw2p[:GLA_RANK].reshape(-1), loss.reshape(-1),
                             jnp.zeros((SMALL_ROWS * LANE - sum(sizes),), F32)]).reshape(SMALL_ROWS, LANE)
    tot = sum_slots(gather_small(small), "sum_small").reshape(-1)
    offs = np.concatenate([[0], np.cumsum(sizes)])
    piece = lambda i: tot[int(offs[i]):int(offs[i + 1])]
    g_g1 = piece(0).reshape(1, D_MODEL)
    g_bg = piece(1).reshape(1, GLA_QK)
    g_gn = piece(2).reshape(1, GLA_WIDTH)
    g_rel = piece(3).reshape(REL_BUCKETS, DSA_HEADS)
    g_g2 = piece(4).reshape(1, D_MODEL)
    g_gf = piece(5).reshape(1, D_MODEL)
    g_w2 = lax.dynamic_slice_in_dim(piece(6).reshape(GLA_RANK, GLA_QK), chip * (GLA_QK // 4), GLA_QK // 4, axis=1)

    loss_all = piece(7)[0]

    upd = [
        ("attn_norm_g", attn_norm_g, g_g1, m_attn_norm_g, v_attn_norm_g),
        ("w_in", w_in[0], None, m_w_in[0], v_w_in[0]),
        ("gla_gate_w2", gla_gate_w2[0], g_w2, m_gla_gate_w2[0], v_gla_gate_w2[0]),
        ("gla_gate_b", gla_gate_b, g_bg, m_gla_gate_b, v_gla_gate_b),
        ("gla_norm_g", gla_norm_g, g_gn, m_gla_norm_g, v_gla_norm_g),
        ("rel_bias", rel_bias, g_rel, m_rel_bias, v_rel_bias),
        ("w_out", w_out[0], None, m_w_out[0], v_w_out[0]),
        ("mlp_norm_g", mlp_norm_g, g_g2, m_mlp_norm_g, v_mlp_norm_g),
        ("w_ff1", w_ff1[0], None, m_w_ff1[0], v_w_ff1[0]),
        ("w_ff2", w_ff2[0], None, m_w_ff2[0], v_w_ff2[0]),
        ("final_norm_g", gf, g_gf, m_final_norm_g.reshape(1, D_MODEL), v_final_norm_g.reshape(1, D_MODEL)),
    ]
    shapes = [attn_norm_g.shape, w_in.shape, gla_gate_w2.shape, gla_gate_b.shape, gla_norm_g.shape, rel_bias.shape,
              w_out.shape, mlp_norm_g.shape, w_ff1.shape, w_ff2.shape, final_norm_g.shape]
    grads, deltas, new_m, new_v = [], [], [], []
    for (name, w, g, m, v), shape in zip(upd, shapes):
        if name in slots:
            g, d, nm_, nv_ = reduce_adamw(slots[name], w, m, v, "reduce_adamw_" + name)
        else:
            d, nm_, nv_ = adamw(w, g, m, v, "adamw_" + name)
        grads.append(g.reshape(shape))
        deltas.append(d.reshape(shape))
        new_m.append(nm_.reshape(shape))
        new_v.append(nv_.reshape(shape))
    return (loss_all, dxs.reshape(1, T, D_MODEL), *grads, *deltas, *new_m, *new_v)
```

```python
import functools
import math

import jax
import jax.numpy as jnp
import numpy as np
from jax import lax
from jax.experimental import pallas as pl
from jax.experimental.pallas import tpu as pltpu

F32 = jnp.float32
BF16 = jnp.bfloat16
MESH = pl.DeviceIdType.MESH

D_MODEL = 1024
GLA_WIDTH = 512
GLA_HEADS = 4
GLA_DK = 64
GLA_DV = 128
GLA_QK = GLA_HEADS * GLA_DK
GLA_RANK = 16
GLA_TAU = 16.0
GLA_CHUNK = 64
DSA_WIDTH = 512
DSA_HEADS = 8
DSA_DH = 64
DSA_DILATIONS = (1, 4, 16)
DSA_SPAN = 128
DSA_BLOCK = 128
DSA_SUPER = DSA_BLOCK * DSA_DILATIONS[-1]
REL_BUCKETS = 32
REL_MAX_DIST = 2048
D_FF = 4096
D_IN = 3088
EPS = 1e-6
NEG = -1e30
QK_SCALE = 0.125

ADAM_LR = 0.001
ADAM_B1 = 0.9
ADAM_B2 = 0.999
ADAM_EPS = 1e-08
ADAM_WD = 0.01
ADAM_STEP = 10

LANE = 128
P_GQ, P_GK, P_GV, P_GR = 0, 256, 512, 1024
P_GLOW = 1536
P_A = 1664
P_DQ, P_DK, P_DV = 1664, 2176, 2688
P_ALL = 3200
R_GQ, R_GK, R_GV, R_GR, R_GLOW, R_DQ, R_DK, R_DV = 0, 256, 512, 1024, 1536, 1552, 2064, 2576

VMEM_LIMIT = 56 * 1024 * 1024


def _params(sem=("arbitrary",), vmem=VMEM_LIMIT):
    return pltpu.CompilerParams(dimension_semantics=sem, vmem_limit_bytes=vmem)


def _dot(a, b):
    return jnp.dot(a, b, preferred_element_type=F32)


def _dot_nt(a, b):
    return lax.dot_general(a, b, (((1,), (1,)), ((), ())), preferred_element_type=F32)


def _dot_tn(a, b):
    return lax.dot_general(a, b, (((0,), (0,)), ((), ())), preferred_element_type=F32)


def _split3(x):
    x1 = x.astype(BF16)
    r1 = x - x1.astype(F32)
    x2 = r1.astype(BF16)
    x3 = (r1 - x2.astype(F32)).astype(BF16)
    return x1, x2, x3


def _dot_exact_lhs(m_bf16, x):
    x1, x2, x3 = _split3(x)
    return _dot(m_bf16, x1) + _dot(m_bf16, x2) + _dot(m_bf16, x3)


def _rstd(xf):
    return lax.rsqrt(jnp.mean(xf * xf, axis=-1, keepdims=True) + EPS)


def _load_once(hbm_ref, vmem_ref, sem):
    cp = pltpu.make_async_copy(hbm_ref, vmem_ref, sem)
    cp.start()
    cp.wait()


ANY = pl.BlockSpec(memory_space=pl.ANY)


def inproj(x, g1, wp):
    T = x.shape[0]
    tm = 256

    def body(x_ref, g_ref, w_hbm, proj_ref, nx_ref, w_vmem, sem):
        @pl.when(pl.program_id(0) == 0)
        def _():
            _load_once(w_hbm, w_vmem, sem)

        xf = x_ref[...]
        nx = ((xf * _rstd(xf)) * g_ref[...]).astype(BF16)
        nx_ref[...] = nx
        proj_ref[...] = _dot(nx, w_vmem[...])

    return pl.pallas_call(
        body,
        grid=(T // tm,),
        in_specs=[pl.BlockSpec((tm, D_MODEL), lambda i: (i, 0)), pl.BlockSpec((1, D_MODEL), lambda i: (0, 0)), ANY],
        out_specs=[pl.BlockSpec((tm, P_ALL), lambda i: (i, 0)), pl.BlockSpec((tm, D_MODEL), lambda i: (i, 0))],
        out_shape=[jax.ShapeDtypeStruct((T, P_ALL), F32), jax.ShapeDtypeStruct((T, D_MODEL), BF16)],
        scratch_shapes=[pltpu.VMEM((D_MODEL, P_ALL), BF16), pltpu.SemaphoreType.DMA],
        compiler_params=_params(),
        name="inproj",
    )(x, g1, wp)


GLA_CHUNKS_PER_STEP = 8
GLA_ROWS = GLA_CHUNK * GLA_CHUNKS_PER_STEP


def _gla_masks():
    lane = lax.broadcasted_iota(jnp.int32, (1, GLA_QK), 1)
    return [(lane >= h * GLA_DK) & (lane < (h + 1) * GLA_DK) for h in range(GLA_HEADS)]


def _log_sigmoid(x):
    return jnp.minimum(x, 0.0) - jnp.log(1.0 + jnp.exp(-jnp.abs(x)))


def _sigmoid(x):
    return 1.0 / (1.0 + jnp.exp(-x))


def _head_cols(h):
    return slice(h * GLA_DV, (h + 1) * GLA_DV)


GLA_GROUP = 256


def _gla_step_constants():
    ri = lax.broadcasted_iota(jnp.int32, (GLA_GROUP, GLA_GROUP), 0)
    ci = lax.broadcasted_iota(jnp.int32, (GLA_GROUP, GLA_GROUP), 1)
    shift = GLA_CHUNK.bit_length() - 1
    same = lax.shift_right_logical(ri, shift) == lax.shift_right_logical(ci, shift)
    return same & (ri >= ci), same & (ri <= ci), _gla_masks()


def _by_group(fn, *arrays):
    outs = [fn(*[a[g * GLA_GROUP:(g + 1) * GLA_GROUP] for a in arrays]) for g in range(GLA_ROWS // GLA_GROUP)]
    if isinstance(outs[0], tuple):
        return tuple(jnp.concatenate(parts, axis=0) for parts in zip(*outs))
    return jnp.concatenate(outs, axis=0)


def _per_chunk(x):
    return x.reshape(GLA_CHUNKS_PER_STEP, GLA_CHUNK, x.shape[-1])


def _chunk_rows_of(x, c):
    return x[c * GLA_CHUNK:(c + 1) * GLA_CHUNK]


def _stack_masked(x, masks):
    return jnp.concatenate([jnp.where(m, x, 0.0) for m in masks], axis=0)


def _stack_head_cols(x):
    return jnp.concatenate([x[:, _head_cols(h)] for h in range(GLA_HEADS)], axis=0)


def _diag_blocks(full, masks):
    out = jnp.where(masks[0], full[:GLA_DV], 0.0)
    for h in range(1, GLA_HEADS):
        out = out + jnp.where(masks[h], full[h * GLA_DV:(h + 1) * GLA_DV], 0.0)
    return out


def _row_blocks_masked(full, masks):
    out = jnp.where(masks[0], full[:GLA_CHUNK], 0.0)
    for h in range(1, GLA_HEADS):
        out = out + jnp.where(masks[h], full[h * GLA_CHUNK:(h + 1) * GLA_CHUNK], 0.0)
    return out


def _gla_step_common(q, k, glow_b, w2, bg, tri):
    gpre = _dot(glow_b, w2) + bg
    glog = _log_sigmoid(gpre) / GLA_TAU
    b = _by_group(lambda g: _dot_exact_lhs(tri, g), glog)
    bl = jnp.sum(_per_chunk(glog), axis=1, keepdims=True)
    eb = jnp.exp(b)
    enb = jnp.exp(-b)
    eke = jnp.exp(jnp.broadcast_to(bl, (GLA_CHUNKS_PER_STEP, GLA_CHUNK, GLA_QK)).reshape(GLA_ROWS, GLA_QK) - b)
    return gpre, eb, enb, eke, jnp.exp(bl), (q * QK_SCALE) * eb, k * enb, k * eke


def gla_fwd(proj, w2p, bg, gn):
    T = proj.shape[0]
    n_steps = T // GLA_ROWS
    n_chunks = T // GLA_CHUNK

    def body(proj_ref, w2_ref, bg_ref, gn_ref, oa_ref, opre_ref, sprev_ref, st_ref):
        @pl.when(pl.program_id(0) == 0)
        def _():
            st_ref[...] = jnp.zeros_like(st_ref)

        causal, _, masks = _gla_step_constants()
        q = proj_ref[:, P_GQ:P_GQ + GLA_QK]
        k = proj_ref[:, P_GK:P_GK + GLA_QK]
        v = proj_ref[:, P_GV:P_GV + GLA_WIDTH]
        r = proj_ref[:, P_GR:P_GR + GLA_WIDTH]
        glow = proj_ref[:, P_GLOW:P_GLOW + LANE].astype(BF16)
        _, _, _, _, ebl, qd, ki, ke = _gla_step_common(q, k, glow, w2_ref[...].astype(BF16), bg_ref[...], causal.astype(BF16))
        ki_b = ki.astype(BF16)
        v_b = v.astype(BF16)
        o_heads = []
        for h in range(GLA_HEADS):
            def intra(qd_g, ki_g, v_g):
                att = jnp.where(causal, _dot_nt(qd_g, ki_g), 0.0)
                return _dot(att.astype(BF16), v_g)

            o_heads.append(_by_group(intra, jnp.where(masks[h], qd, 0.0).astype(BF16), ki_b, v_b[:, _head_cols(h)]))
        st = st_ref[...]
        states = []
        for c in range(GLA_CHUNKS_PER_STEP):
            states.append(st)
            sprev_ref[c] = st
            inc = _diag_blocks(_dot_tn(_chunk_rows_of(v_b, c), _chunk_rows_of(ke, c).astype(BF16)), masks)
            st = st * ebl[c] + inc
        st_ref[...] = st
        inter = []
        for c in range(GLA_CHUNKS_PER_STEP):
            qd_c = _stack_masked(_chunk_rows_of(qd, c), masks).astype(BF16)
            got = _dot_nt(qd_c, states[c].astype(BF16))
            inter.append(jnp.concatenate([got[h * GLA_CHUNK:(h + 1) * GLA_CHUNK] for h in range(GLA_HEADS)], axis=1))
        o = jnp.concatenate(o_heads, axis=1) + jnp.concatenate(inter, axis=0)
        opre_ref[...] = o
        on = jnp.concatenate([o[:, _head_cols(h)] * _rstd(o[:, _head_cols(h)]) for h in range(GLA_HEADS)], axis=1)
        oa_ref[...] = ((on * gn_ref[...]) * (r * _sigmoid(r))).astype(BF16)

    return pl.pallas_call(
        body,
        grid=(n_steps,),
        in_specs=[
            pl.BlockSpec((GLA_ROWS, P_ALL), lambda i: (i, 0)),
            pl.BlockSpec((LANE, GLA_QK), lambda i: (0, 0)),
            pl.BlockSpec((1, GLA_QK), lambda i: (0, 0)),
            pl.BlockSpec((1, GLA_WIDTH), lambda i: (0, 0)),
        ],
        out_specs=[
            pl.BlockSpec((GLA_ROWS, GLA_WIDTH), lambda i: (i, 0)),
            pl.BlockSpec((GLA_ROWS, GLA_WIDTH), lambda i: (i, 0)),
            pl.BlockSpec((GLA_CHUNKS_PER_STEP, GLA_DV, GLA_QK), lambda i: (i, 0, 0)),
        ],
        out_shape=[
            jax.ShapeDtypeStruct((T, GLA_WIDTH), BF16),
            jax.ShapeDtypeStruct((T, GLA_WIDTH), F32),
            jax.ShapeDtypeStruct((n_chunks, GLA_DV, GLA_QK), F32),
        ],
        scratch_shapes=[pltpu.VMEM((GLA_DV, GLA_QK), F32)],
        compiler_params=_params(),
        name="gla_fwd",
    )(proj, w2p, bg, gn)


def gla_bwd(proj, w2p, bg, gn, opre, sprev, dmixed, exchange=None):
    T = proj.shape[0]
    n_steps = T // GLA_ROWS

    def body(*refs):
        refs = _host_exchange(exchange, refs, 7, 4, pl.program_id(0), n_steps)
        proj_ref, w2_ref, bg_ref, gn_ref, opre_ref, sprev_ref, doa_ref, da_ref, dw2_ref, dbg_ref, dgn_ref, dst_ref = refs

        @pl.when(pl.program_id(0) == 0)
        def _():
            dst_ref[...] = jnp.zeros_like(dst_ref)
            dw2_ref[...] = jnp.zeros_like(dw2_ref)
            dbg_ref[...] = jnp.zeros_like(dbg_ref)
            dgn_ref[...] = jnp.zeros_like(dgn_ref)

        causal, causal_t, masks = _gla_step_constants()
        w2 = w2_ref[...].astype(BF16)
        gn = gn_ref[...]
        q = proj_ref[:, P_GQ:P_GQ + GLA_QK]
        k = proj_ref[:, P_GK:P_GK + GLA_QK]
        v_b = proj_ref[:, P_GV:P_GV + GLA_WIDTH].astype(BF16)
        r = proj_ref[:, P_GR:P_GR + GLA_WIDTH]
        glow = proj_ref[:, P_GLOW:P_GLOW + LANE].astype(BF16)
        o = opre_ref[...]
        doa = doa_ref[...]
        gpre, eb, enb, eke, ebl, qd, ki, ke = _gla_step_common(q, k, glow, w2, bg_ref[...], causal.astype(BF16))
        sig = _sigmoid(r)
        rs = jnp.concatenate([jnp.broadcast_to(_rstd(o[:, _head_cols(h)]), (GLA_ROWS, GLA_DV)) for h in range(GLA_HEADS)], axis=1)
        on = o * rs
        d_ong = doa * (r * sig)
        dr = doa * (on * gn) * (sig * (1.0 + r * (1.0 - sig)))
        dgn_ref[...] += jnp.sum(d_ong * on, axis=0, keepdims=True)
        d_on = d_ong * gn
        t = d_on * on
        mean_t = jnp.concatenate([jnp.broadcast_to(jnp.mean(t[:, _head_cols(h)], axis=-1, keepdims=True), (GLA_ROWS, GLA_DV))
                                  for h in range(GLA_HEADS)], axis=1)
        do_b = (rs * (d_on - on * mean_t)).astype(BF16)
        ki_b = ki.astype(BF16)
        ke_b = ke.astype(BF16)
        dqd = jnp.zeros_like(qd)
        dki = jnp.zeros_like(qd)
        dv_heads = []
        for h in range(GLA_HEADS):
            qd_h = jnp.where(masks[h], qd, 0.0).astype(BF16)
            do_h = do_b[:, _head_cols(h)]

            def intra(qd_g, ki_g, v_g, do_g):
                att = jnp.where(causal, _dot_nt(qd_g, ki_g), 0.0).astype(BF16)
                d_att = jnp.where(causal, _dot_nt(do_g, v_g), 0.0).astype(BF16)
                return _dot_tn(att, do_g), _dot(d_att, ki_g), _dot_tn(d_att, qd_g)

            dv_h, dqd_h, dki_h = _by_group(intra, qd_h, ki_b, v_b[:, _head_cols(h)], do_h)
            dv_heads.append(dv_h)
            dqd = dqd + jnp.where(masks[h], dqd_h, 0.0)
            dki = dki + dki_h
        states = [sprev_ref[c] for c in range(GLA_CHUNKS_PER_STEP)]
        dqd_inter, dst_adds = [], []
        for c in range(GLA_CHUNKS_PER_STEP):
            do_c = _stack_head_cols(_chunk_rows_of(do_b, c))
            dqd_inter.append(_row_blocks_masked(_dot(do_c, states[c].astype(BF16)), masks))
            dst_adds.append(_diag_blocks(_dot_tn(_chunk_rows_of(do_b, c), _chunk_rows_of(qd, c).astype(BF16)), masks))
        dst = dst_ref[...]
        dsts, debls = [None] * GLA_CHUNKS_PER_STEP, [None] * GLA_CHUNKS_PER_STEP
        for c in reversed(range(GLA_CHUNKS_PER_STEP)):
            dsts[c] = dst
            debls[c] = jnp.sum(dst * states[c], axis=0, keepdims=True)
            dst = dst * ebl[c] + dst_adds[c]
        dst_ref[...] = dst
        dv_inter, dke = [], []
        for c in range(GLA_CHUNKS_PER_STEP):
            dst_b = dsts[c].astype(BF16)
            got = _dot_nt(_stack_masked(_chunk_rows_of(ke, c), masks).astype(BF16), dst_b)
            dv_inter.append(jnp.concatenate([got[h * GLA_CHUNK:(h + 1) * GLA_CHUNK] for h in range(GLA_HEADS)], axis=1))
            dke.append(_row_blocks_masked(_dot(_stack_head_cols(_chunk_rows_of(v_b, c)), dst_b), masks))
        dqd = dqd + jnp.concatenate(dqd_inter, axis=0)
        dke = jnp.concatenate(dke, axis=0)
        dv = jnp.concatenate(dv_heads, axis=1) + jnp.concatenate(dv_inter, axis=0)
        dkk = dke * ke
        dbl = jnp.sum(_per_chunk(dkk), axis=1, keepdims=True) + jnp.stack(debls) * ebl
        last_row = lax.broadcasted_iota(jnp.int32, (GLA_CHUNKS_PER_STEP, GLA_CHUNK, GLA_QK), 1) == GLA_CHUNK - 1
        db = dqd * qd - dki * ki - dkk + jnp.where(last_row, dbl, 0.0).reshape(GLA_ROWS, GLA_QK)
        tri_t = causal_t.astype(BF16)
        dglog = _by_group(lambda g: _dot_exact_lhs(tri_t, g), db)
        dgpre = (dglog / GLA_TAU) * _sigmoid(-gpre)
        dgpre_b = dgpre.astype(BF16)
        da_ref[...] = jnp.concatenate(
            [dqd * eb * QK_SCALE, dki * enb + dke * eke, dv, dr, _dot_nt(dgpre_b, w2)], axis=1).astype(BF16)
        dw2_ref[...] += _dot_tn(glow, dgpre_b)
        dbg_ref[...] += jnp.sum(dgpre, axis=0, keepdims=True)

    rev = lambda i: (n_steps - 1 - i, 0)
    return _hosted_call(
        exchange,
        body,
        grid=(n_steps,),
        in_specs=[
            pl.BlockSpec((GLA_ROWS, P_ALL), rev),
            pl.BlockSpec((LANE, GLA_QK), lambda i: (0, 0)),
            pl.BlockSpec((1, GLA_QK), lambda i: (0, 0)),
            pl.BlockSpec((1, GLA_WIDTH), lambda i: (0, 0)),
            pl.BlockSpec((GLA_ROWS, GLA_WIDTH), rev),
            pl.BlockSpec((GLA_CHUNKS_PER_STEP, GLA_DV, GLA_QK), lambda i: (n_steps - 1 - i, 0, 0)),
            pl.BlockSpec((GLA_ROWS, GLA_WIDTH), rev),
        ],
        out_specs=[
            pl.BlockSpec((GLA_ROWS, P_A), rev),
            pl.BlockSpec((LANE, GLA_QK), lambda i: (0, 0)),
            pl.BlockSpec((1, GLA_QK), lambda i: (0, 0)),
            pl.BlockSpec((1, GLA_WIDTH), lambda i: (0, 0)),
        ],
        out_shape=[
            jax.ShapeDtypeStruct((T, P_A), BF16),
            jax.ShapeDtypeStruct((LANE, GLA_QK), F32),
            jax.ShapeDtypeStruct((1, GLA_QK), F32),
            jax.ShapeDtypeStruct((1, GLA_WIDTH), F32),
        ],
        scratch_shapes=[pltpu.VMEM((GLA_DV, GLA_QK), F32)],
        compiler_params=_params(),
        name="gla_bwd",
        args=(proj, w2p, bg, gn, opre, sprev, dmixed),
    )


def _t5_bucket(dist):
    max_exact = REL_BUCKETS // 2
    n = np.maximum(dist, 0)
    large = max_exact + (np.log(np.maximum(n, 1) / max_exact) / math.log(REL_MAX_DIST / max_exact)
                         * (REL_BUCKETS - max_exact)).astype(np.int32)
    large = np.minimum(large, REL_BUCKETS - 1)
    return np.where(n < max_exact, n, large).astype(np.int32)


SUBLANES = 8


def _bucket_rows():
    steps = DSA_BLOCK - np.arange(2 * DSA_BLOCK)
    in_band = (steps >= 0) & (steps <= DSA_SPAN)
    rows = np.stack([np.where(in_band, _t5_bucket(steps * d), -1) for d in DSA_DILATIONS]).astype(np.int32)
    return np.broadcast_to(rows[:, None, :], (len(DSA_DILATIONS), SUBLANES, 2 * DSA_BLOCK)).copy()


def bias_tables(rel_bias):
    ids = jnp.asarray(_bucket_rows())
    nd = len(DSA_DILATIONS)

    def body(rel_ref, ids_ref, tab_ref):
        h = pl.program_id(1)
        idt = ids_ref[0]
        row = jnp.where(idt < 0, NEG, 0.0).astype(F32)
        for b in range(REL_BUCKETS):
            row = jnp.where(idt == b, rel_ref[b, h], row)
        full = jnp.broadcast_to(row[0:1], (DSA_BLOCK, 2 * DSA_BLOCK))
        tab_ref[0, 0] = pltpu.roll(full, 0, 1, stride=1, stride_axis=0)

    return pl.pallas_call(
        body,
        grid=(nd, DSA_HEADS),
        in_specs=[pl.BlockSpec(memory_space=pltpu.SMEM), pl.BlockSpec((1, SUBLANES, 2 * DSA_BLOCK), lambda d, h: (d, 0, 0))],
        out_specs=pl.BlockSpec((1, 1, DSA_BLOCK, 2 * DSA_BLOCK), lambda d, h: (d, h, 0, 0)),
        out_shape=jax.ShapeDtypeStruct((nd, DSA_HEADS, DSA_BLOCK, 2 * DSA_BLOCK), F32),
        compiler_params=_params(("arbitrary", "arbitrary")),
        name="bias_tables",
    )(rel_bias, ids)


def _bucket_ids():
    L = DSA_BLOCK
    steps = L + np.arange(L)[:, None] - np.arange(2 * L)[None, :]
    in_band = (steps >= 0) & (steps <= DSA_SPAN)
    return np.stack([np.where(in_band, _t5_bucket(steps * d), -1) for d in DSA_DILATIONS]).astype(np.int32)


def bias_tables_bwd(dtab):
    ids = jnp.asarray(_bucket_ids())
    nd = len(DSA_DILATIONS)

    def body(dtab_ref, ids_ref, drel_ref):
        @pl.when((pl.program_id(0) == 0) & (pl.program_id(1) == 0))
        def _():
            for b in range(REL_BUCKETS):
                for h in range(DSA_HEADS):
                    drel_ref[b, h] = 0.0

        h = pl.program_id(1)
        idt = ids_ref[0]
        g = dtab_ref[0, 0]
        for b in range(REL_BUCKETS):
            drel_ref[b, h] += jnp.sum(jnp.where(idt == b, g, 0.0))

    return pl.pallas_call(
        body,
        grid=(nd, DSA_HEADS),
        in_specs=[pl.BlockSpec((1, 1, DSA_BLOCK, 2 * DSA_BLOCK), lambda d, h: (d, h, 0, 0)),
                  pl.BlockSpec((1, DSA_BLOCK, 2 * DSA_BLOCK), lambda d, h: (d, 0, 0))],
        out_specs=pl.BlockSpec(memory_space=pltpu.SMEM),
        out_shape=jax.ShapeDtypeStruct((REL_BUCKETS, DSA_HEADS), F32),
        compiler_params=_params(("arbitrary", "arbitrary")),
        name="bias_tables_bwd",
    )(dtab, ids)


DSA_PAIRS = DSA_HEADS // 2
DSA_UNROLL = 16
DSA_COMBINE_ROWS = 256


def _dsa_units(d):
    return d, DSA_SUPER // (DSA_BLOCK * d)


def _dsa_specs(T):
    nsb = T // DSA_SUPER
    qcol, kcol, vcol = P_DQ // LANE, P_DK // LANE, P_DV // LANE
    return nsb, qcol, kcol, vcol


def _head_lane_mask():
    return lax.broadcasted_iota(jnp.int32, (1, LANE), 1) < DSA_DH


def _fill_tile_variants(tab_ref, variants):
    col = lax.broadcasted_iota(jnp.int32, (2 * DSA_BLOCK, 2 * DSA_BLOCK), 1)
    for di in range(len(DSA_DILATIONS)):
        tile = tab_ref[di, 0]
        variants[di, 0] = tile
        variants[di, 1] = jnp.where(col < DSA_BLOCK, NEG, tile)


def _tile_variants_scratch():
    return pltpu.VMEM((len(DSA_DILATIONS), 2, 2 * DSA_BLOCK, 2 * DSA_BLOCK), F32)


def _pair_tiles(tab):
    return tab.reshape(len(DSA_DILATIONS), DSA_PAIRS, 2 * DSA_BLOCK, 2 * DSA_BLOCK)


def _stack_heads(t, head0):
    return jnp.concatenate([jnp.where(head0, t, 0.0), jnp.where(head0, 0.0, t)], axis=0)


def dsa_fwd(proj, tab, exchange=None):
    T = proj.shape[0]
    nsb, qcol, kcol, vcol = _dsa_specs(T)
    S = DSA_SUPER

    def body(*refs):
        refs = _host_exchange(exchange, refs, 6, 2, pl.program_id(0) * nsb + pl.program_id(1), DSA_PAIRS * nsb)
        q_ref, kp_ref, kc_ref, vp_ref, vc_ref, tab_ref, out_ref, lse_ref, kk, vv, ob, lb, tiles = refs
        sb = pl.program_id(1)
        kk[0:S, :] = kp_ref[...]
        kk[S:2 * S, :] = kc_ref[...]
        vv[0:S, :] = vp_ref[...]
        vv[S:2 * S, :] = vc_ref[...]
        head0 = _head_lane_mask()
        pl.when(sb == 0)(functools.partial(_fill_tile_variants, tab_ref, tiles))

        for di, d in enumerate(DSA_DILATIONS):
            n_res, n_blk = _dsa_units(d)

            def unit(u, carry, di=di, d=d, n_blk=n_blk):
                r = u // n_blk
                c = u % n_blk
                q0 = r + d * DSA_BLOCK * c
                qrows = pl.ds(q0, DSA_BLOCK, stride=d) if d > 1 else pl.ds(q0, DSA_BLOCK)
                krows = pl.ds(S + q0 - d * DSA_BLOCK, 2 * DSA_BLOCK, stride=d) if d > 1 else pl.ds(S + q0 - DSA_BLOCK, 2 * DSA_BLOCK)
                q2 = q_ref[qrows, :] * QK_SCALE
                k2 = kk[krows, :].astype(BF16)
                v2 = vv[krows, :].astype(BF16)
                qs = _stack_heads(q2, head0).astype(BF16)
                s = _dot_nt(qs, k2) + tiles[di, ((sb == 0) & (c == 0)).astype(jnp.int32)]
                m = jnp.max(s, axis=-1, keepdims=True)
                p = jnp.exp(s - m)
                den = jnp.sum(p, axis=-1, keepdims=True)
                o = _dot(p.astype(BF16), v2) / den
                l = jnp.broadcast_to(m + jnp.log(den), (2 * DSA_BLOCK, LANE))
                ob[di, qrows, :] = jnp.where(head0, o[:DSA_BLOCK], o[DSA_BLOCK:])
                lb[di, qrows, :] = jnp.where(head0, l[:DSA_BLOCK], l[DSA_BLOCK:])
                return carry

            lax.fori_loop(0, n_res * n_blk, unit, 0, unroll=DSA_UNROLL)

        def combine(i, carry):
            rows = pl.ds(pl.multiple_of(i * DSA_COMBINE_ROWS, DSA_COMBINE_ROWS), DSA_COMBINE_ROWS)
            l0, l1, l2 = lb[0, rows, :], lb[1, rows, :], lb[2, rows, :]
            mx = jnp.maximum(jnp.maximum(l0, l1), l2)
            e0, e1, e2 = jnp.exp(l0 - mx), jnp.exp(l1 - mx), jnp.exp(l2 - mx)
            den = e0 + e1 + e2
            out_ref[rows, :] = (e0 * ob[0, rows, :] + e1 * ob[1, rows, :] + e2 * ob[2, rows, :]) / den
            lse_ref[rows, :] = mx + jnp.log(den)
            return carry

        lax.fori_loop(0, S // DSA_COMBINE_ROWS, combine, 0)

    prev = lambda col: (lambda hp, sb: (jnp.maximum(sb - 1, 0), col + hp))
    cur = lambda col: (lambda hp, sb: (sb, col + hp))
    blk = lambda f: pl.BlockSpec((S, LANE), f)
    return _hosted_call(
        exchange,
        body,
        grid=(DSA_PAIRS, nsb),
        in_specs=[blk(cur(qcol)), blk(prev(kcol)), blk(cur(kcol)), blk(prev(vcol)), blk(cur(vcol)),
                  pl.BlockSpec((len(DSA_DILATIONS), 1, 2 * DSA_BLOCK, 2 * DSA_BLOCK), lambda hp, sb: (0, hp, 0, 0))],
        out_specs=[blk(lambda hp, sb: (sb, hp)), blk(lambda hp, sb: (sb, hp))],
        out_shape=[jax.ShapeDtypeStruct((T, DSA_WIDTH), F32), jax.ShapeDtypeStruct((T, DSA_WIDTH), F32)],
        scratch_shapes=[pltpu.VMEM((2 * S, LANE), F32), pltpu.VMEM((2 * S, LANE), F32),
                        pltpu.VMEM((len(DSA_DILATIONS), S, LANE), F32), pltpu.VMEM((len(DSA_DILATIONS), S, LANE), F32),
                        _tile_variants_scratch()],
        compiler_params=_params(("arbitrary", "arbitrary")),
        name="dsa_fwd",
        args=(proj, proj, proj, proj, proj, _pair_tiles(tab)),
    )


def dsa_bwd(proj, tab, ob_out, lse, dmixed, exchange=None):
    T = proj.shape[0]
    nsb, qcol, kcol, vcol = _dsa_specs(T)
    S = DSA_SUPER
    nd = len(DSA_DILATIONS)
    ocol = GLA_WIDTH // LANE

    def body(*refs):
        refs = _host_exchange(exchange, refs, 9, 4, pl.program_id(0) * nsb + pl.program_id(1), DSA_PAIRS * nsb)
        (q_ref, kp_ref, kc_ref, vp_ref, vc_ref, tab_ref, o_ref, lse_ref, do_ref,
         dq_ref, dk_ref, dv_ref, dtab_ref, kk, vv, dqa, dkk, dvv, tiles) = refs
        j = pl.program_id(1)
        sb = nsb - 1 - j
        kk[0:S, :] = kp_ref[...]
        kk[S:2 * S, :] = kc_ref[...]
        vv[0:S, :] = vp_ref[...]
        vv[S:2 * S, :] = vc_ref[...]
        head0 = _head_lane_mask()
        pl.when(j == 0)(functools.partial(_fill_tile_variants, tab_ref, tiles))

        @pl.when(j == 0)
        def _():
            dtab_ref[...] = jnp.zeros_like(dtab_ref)
            dkk[S:2 * S, :] = jnp.zeros((S, LANE), F32)
            dvv[S:2 * S, :] = jnp.zeros((S, LANE), F32)

        @pl.when(j > 0)
        def _():
            dkk[S:2 * S, :] = dkk[0:S, :]
            dvv[S:2 * S, :] = dvv[0:S, :]

        dkk[0:S, :] = jnp.zeros((S, LANE), F32)
        dvv[0:S, :] = jnp.zeros((S, LANE), F32)
        dqa[...] = jnp.zeros_like(dqa)

        for di, d in enumerate(DSA_DILATIONS):
            n_res, n_blk = _dsa_units(d)

            def unit(u, carry, di=di, d=d, n_blk=n_blk):
                r = u // n_blk
                c = u % n_blk
                q0 = r + d * DSA_BLOCK * c
                qrows = pl.ds(q0, DSA_BLOCK, stride=d) if d > 1 else pl.ds(q0, DSA_BLOCK)
                krows = pl.ds(S + q0 - d * DSA_BLOCK, 2 * DSA_BLOCK, stride=d) if d > 1 else pl.ds(S + q0 - DSA_BLOCK, 2 * DSA_BLOCK)
                q2 = q_ref[qrows, :] * QK_SCALE
                k2 = kk[krows, :].astype(BF16)
                v2 = vv[krows, :].astype(BF16)
                do2 = do_ref[qrows, :]
                o2 = o_ref[qrows, :]
                l2 = lse_ref[qrows, :]
                qs = _stack_heads(q2, head0).astype(BF16)
                dos = _stack_heads(do2, head0)
                dos_b = dos.astype(BF16)
                delta = jnp.sum(dos * jnp.concatenate([o2, o2], axis=0), axis=-1, keepdims=True)
                lse = jnp.concatenate([jnp.max(jnp.where(head0, l2, -jnp.inf), axis=-1, keepdims=True),
                                       jnp.max(jnp.where(head0, -jnp.inf, l2), axis=-1, keepdims=True)], axis=0)
                s = _dot_nt(qs, k2) + tiles[di, ((sb == 0) & (c == 0)).astype(jnp.int32)]
                p = jnp.exp(s - lse)
                ds = p * (_dot_nt(dos_b, v2) - delta)
                dtab_ref[di, 0] += ds
                ds_b = ds.astype(BF16)
                dq = _dot(ds_b, k2)
                dqa[qrows, :] += jnp.where(head0, dq[:DSA_BLOCK], dq[DSA_BLOCK:]) * QK_SCALE
                dkk[krows, :] += _dot_tn(ds_b, qs)
                dvv[krows, :] += _dot_tn(p.astype(BF16), dos_b)
                return carry

            lax.fori_loop(0, n_res * n_blk, unit, 0, unroll=DSA_UNROLL)

        dq_ref[...] = dqa[...].astype(BF16)
        dk_ref[...] = dkk[S:2 * S, :].astype(BF16)
        dv_ref[...] = dvv[S:2 * S, :].astype(BF16)

    prev = lambda col: (lambda hp, j: (jnp.maximum(nsb - 2 - j, 0), col + hp))
    cur = lambda col: (lambda hp, j: (nsb - 1 - j, col + hp))
    blk = lambda f: pl.BlockSpec((S, LANE), f)
    out_blk = blk(lambda hp, j: (nsb - 1 - j, hp))
    tab_blk = pl.BlockSpec((nd, 1, 2 * DSA_BLOCK, 2 * DSA_BLOCK), lambda hp, j: (0, hp, 0, 0))
    dq, dk, dv, dtab, *carried = _hosted_call(
        exchange,
        body,
        grid=(DSA_PAIRS, nsb),
        in_specs=[blk(cur(qcol)), blk(prev(kcol)), blk(cur(kcol)), blk(prev(vcol)), blk(cur(vcol)), tab_blk,
                  out_blk, out_blk, blk(cur(ocol))],
        out_specs=[out_blk, out_blk, out_blk, tab_blk],
        out_shape=[jax.ShapeDtypeStruct((T, DSA_WIDTH), BF16)] * 3
        + [jax.ShapeDtypeStruct((nd, DSA_PAIRS, 2 * DSA_BLOCK, 2 * DSA_BLOCK), F32)],
        scratch_shapes=[pltpu.VMEM((2 * S, LANE), F32), pltpu.VMEM((2 * S, LANE), F32), pltpu.VMEM((S, LANE), F32),
                        pltpu.VMEM((2 * S, LANE), F32), pltpu.VMEM((2 * S, LANE), F32), _tile_variants_scratch()],
        compiler_params=_params(("arbitrary", "arbitrary")),
        name="dsa_bwd",
        args=(proj, proj, proj, proj, proj, _pair_tiles(tab), ob_out, lse, dmixed),
    )
    return (dq, dk, dv, dtab.reshape(nd, DSA_HEADS, DSA_BLOCK, 2 * DSA_BLOCK), *carried)


FF_BLOCKS = 4
FF_BLOCK = D_FF // FF_BLOCKS


def post_fused(x, oa, ob, tgt, g2, gf, wout, wff1, wff2):
    T = x.shape[0]
    tm = 256
    inv_d = 1.0 / D_MODEL

    def body(x_ref, oa_ref, ob_ref, tgt_ref, g2_ref, gf_ref, wout_hbm, wff1_hbm, wff2_hbm,
             mixed_ref, nm_ref, a_ref, dpre_ref, dh2_ref, dh1_ref, dmixed_ref, loss_ref, dgf_ref, dg2_ref,
             wout_v, wff1_v, wff2_v, sems):
        @pl.when(pl.program_id(0) == 0)
        def _():
            cps = [pltpu.make_async_copy(s, d, sems.at[i])
                   for i, (s, d) in enumerate([(wout_hbm, wout_v), (wff1_hbm, wff1_v), (wff2_hbm, wff2_v)])]
            for cp in cps:
                cp.start()
            for cp in cps:
                cp.wait()
            loss_ref[...] = jnp.zeros_like(loss_ref)
            dgf_ref[...] = jnp.zeros_like(dgf_ref)
            dg2_ref[...] = jnp.zeros_like(dg2_ref)

        mixed = jnp.concatenate([oa_ref[...], ob_ref[...].astype(BF16)], axis=1)
        mixed_ref[...] = mixed
        h1 = x_ref[...] + _dot(mixed, wout_v[...])
        rs1 = _rstd(h1)
        hn1 = h1 * rs1
        g2 = g2_ref[...]
        nm = (hn1 * g2).astype(BF16)
        nm_ref[...] = nm
        relu = []
        mlp = jnp.zeros((tm, D_MODEL), F32)
        for j in range(FF_BLOCKS):
            cols = slice(j * FF_BLOCK, (j + 1) * FF_BLOCK)
            r_j = jnp.maximum(_dot(nm, wff1_v[j]), 0.0)
            a_j = (r_j * r_j).astype(BF16)
            a_ref[:, cols] = a_j
            relu.append(r_j)
            mlp = mlp + _dot(a_j, wff2_v[cols, :])
        h2 = h1 + mlp
        rsf = _rstd(h2)
        hnf = h2 * rsf
        gf = gf_ref[...]
        diff = hnf * gf - tgt_ref[...]
        loss_ref[...] += 0.5 * jnp.sum(jnp.sum(diff * diff, axis=-1, keepdims=True) * inv_d, axis=0, keepdims=True)
        dy = diff * inv_d
        dgf_ref[...] += jnp.sum(dy * hnf, axis=0, keepdims=True)
        dhnf = dy * gf
        dh2 = rsf * (dhnf - hnf * jnp.mean(dhnf * hnf, axis=-1, keepdims=True))
        dh2_b = dh2.astype(BF16)
        dh2_ref[...] = dh2_b
        dnm = jnp.zeros((tm, D_MODEL), F32)
        for j in range(FF_BLOCKS):
            cols = slice(j * FF_BLOCK, (j + 1) * FF_BLOCK)
            dpre_j = (_dot_nt(dh2_b, wff2_v[cols, :]) * (2.0 * relu[j])).astype(BF16)
            dpre_ref[:, cols] = dpre_j
            dnm = dnm + _dot_nt(dpre_j, wff1_v[j])
        dg2_ref[...] += jnp.sum(dnm * hn1, axis=0, keepdims=True)
        dhn1 = dnm * g2
        dh1 = dh2 + rs1 * (dhn1 - hn1 * jnp.mean(dhn1 * hn1, axis=-1, keepdims=True))
        dh1_ref[...] = dh1
        dmixed_ref[...] = _dot_nt(dh1.astype(BF16), wout_v[...])

    row = lambda w: pl.BlockSpec((tm, w), lambda i: (i, 0))
    vec = lambda w: pl.BlockSpec((1, w), lambda i: (0, 0))
    return pl.pallas_call(
        body,
        grid=(T // tm,),
        in_specs=[row(D_MODEL), row(GLA_WIDTH), row(DSA_WIDTH), row(D_MODEL), vec(D_MODEL), vec(D_MODEL), ANY, ANY, ANY],
        out_specs=[row(D_MODEL), row(D_MODEL), row(D_FF), row(D_FF), row(D_MODEL), row(D_MODEL), row(D_MODEL),
                   vec(1), vec(D_MODEL), vec(D_MODEL)],
        out_shape=[
            jax.ShapeDtypeStruct((T, D_MODEL), BF16),
            jax.ShapeDtypeStruct((T, D_MODEL), BF16),
            jax.ShapeDtypeStruct((T, D_FF), BF16),
            jax.ShapeDtypeStruct((T, D_FF), BF16),
            jax.ShapeDtypeStruct((T, D_MODEL), BF16),
            jax.ShapeDtypeStruct((T, D_MODEL), F32),
            jax.ShapeDtypeStruct((T, D_MODEL), F32),
            jax.ShapeDtypeStruct((1, 1), F32),
            jax.ShapeDtypeStruct((1, D_MODEL), F32),
            jax.ShapeDtypeStruct((1, D_MODEL), F32),
        ],
        scratch_shapes=[pltpu.VMEM((D_MODEL, D_MODEL), BF16), pltpu.VMEM((FF_BLOCKS, D_MODEL, FF_BLOCK), BF16),
                        pltpu.VMEM((D_FF, D_MODEL), BF16), pltpu.SemaphoreType.DMA((3,))],
        compiler_params=_params(),
        name="post_fused",
    )(x, oa, ob, tgt, g2, gf, wout, wff1, wff2)


WGRAD_TOKENS = 2048


def wgrad(a, b, name, bm=None, bn=None, col_blocked=False):
    T, M = a.shape
    N = b.shape[1]
    bm = M if bm is None else bm
    bn = N if bn is None else bn
    tk = min(WGRAD_TOKENS, T)
    n_k = T // tk

    def body(a_ref, b_ref, o_ref, acc_ref):
        part = _dot_tn(a_ref[...].astype(BF16), b_ref[...].astype(BF16))
        out = o_ref.at[0] if col_blocked else o_ref
        k = pl.program_id(2)
        if n_k == 1:
            out[...] = part.astype(BF16)
            return

        @pl.when(k == 0)
        def _():
            acc_ref[...] = part

        @pl.when((k > 0) & (k < n_k - 1))
        def _():
            acc_ref[...] += part

        @pl.when(k == n_k - 1)
        def _():
            out[...] = (acc_ref[...] + part).astype(BF16)

    if col_blocked:
        assert bm == M
        out_spec = pl.BlockSpec((1, M, bn), lambda i, j, k: (j, 0, 0))
        out_shape = jax.ShapeDtypeStruct((N // bn, M, bn), BF16)
    else:
        out_spec = pl.BlockSpec((bm, bn), lambda i, j, k: (i, j))
        out_shape = jax.ShapeDtypeStruct((M, N), BF16)
    return pl.pallas_call(
        body,
        grid=(M // bm, N // bn, n_k),
        in_specs=[pl.BlockSpec((tk, bm), lambda i, j, k: (k, i)), pl.BlockSpec((tk, bn), lambda i, j, k: (k, j))],
        out_specs=out_spec,
        out_shape=out_shape,
        scratch_shapes=[pltpu.VMEM((bm, bn), F32)],
        compiler_params=_params(("arbitrary", "arbitrary", "arbitrary")),
        name=name,
    )(a, b)


def dx_final(x, dh1, g1, da, dq, dk, dv, wp, exchange=None):
    T = x.shape[0]
    tm = 256

    def body(*refs):
        refs = _host_exchange(exchange, refs, 8, 2, pl.program_id(0), T // tm)
        x_ref, dh1_ref, g_ref, da_ref, dq_ref, dk_ref, dv_ref, w_hbm, dx_ref, dg_ref, w_vmem, sem = refs

        @pl.when(pl.program_id(0) == 0)
        def _():
            _load_once(w_hbm, w_vmem, sem)
            dg_ref[...] = jnp.zeros_like(dg_ref)

        dnx = (_dot_nt(da_ref[...], w_vmem[:, 0:P_A]) + _dot_nt(dq_ref[...], w_vmem[:, P_DQ:P_DQ + DSA_WIDTH])
               + _dot_nt(dk_ref[...], w_vmem[:, P_DK:P_DK + DSA_WIDTH]) + _dot_nt(dv_ref[...], w_vmem[:, P_DV:P_DV + DSA_WIDTH]))
        xf = x_ref[...]
        rs = _rstd(xf)
        hn = xf * rs
        dg_ref[...] += jnp.sum(dnx * hn, axis=0, keepdims=True)
        dhn = dnx * g_ref[...]
        dx_ref[...] = dh1_ref[...] + rs * (dhn - hn * jnp.mean(dhn * hn, axis=-1, keepdims=True))

    row = lambda w: pl.BlockSpec((tm, w), lambda i: (i, 0))
    vec = pl.BlockSpec((1, D_MODEL), lambda i: (0, 0))
    return _hosted_call(
        exchange,
        body,
        grid=(T // tm,),
        in_specs=[row(D_MODEL), row(D_MODEL), vec, row(P_A), row(DSA_WIDTH), row(DSA_WIDTH), row(DSA_WIDTH), ANY],
        out_specs=[row(D_MODEL), vec],
        out_shape=[jax.ShapeDtypeStruct((T, D_MODEL), F32), jax.ShapeDtypeStruct((1, D_MODEL), F32)],
        scratch_shapes=[pltpu.VMEM((D_MODEL, P_ALL), BF16), pltpu.SemaphoreType.DMA],
        compiler_params=_params(),
        name="dx_final",
        args=(x, dh1, g1, da, dq, dk, dv, wp),
    )


def adamw(w, g, m, v, name):
    R, C = w.shape
    br = 256 if R % 256 == 0 else R

    def body(w_ref, g_ref, m_ref, v_ref, d_ref, nm_ref, nv_ref):
        d_ref[...], nm_ref[...], nv_ref[...] = _adamw_math(w_ref[...], g_ref[...], m_ref[...], v_ref[...])

    spec = pl.BlockSpec((br, C), lambda i: (i, 0))
    return pl.pallas_call(
        body,
        grid=(R // br,),
        in_specs=[spec] * 4,
        out_specs=[spec] * 3,
        out_shape=[jax.ShapeDtypeStruct((R, C), F32)] * 3,
        compiler_params=_params(),
        name=name,
    )(w, g, m, v)


def _place():
    return lax.axis_index("x"), lax.axis_index("y"), lax.axis_index("c")


def _other_chips(x, y):
    return [(1 - x, y), (x, 1 - y), (1 - x, 1 - y)]


class Exchange:
    def __init__(self, kind, arrays):
        self.kind, self.arrays, self.n = kind, arrays, len(arrays)
        self.slots = 4 if kind == "gather" else 8

    def out_shapes(self):
        if self.kind == "gather":
            return [jax.ShapeDtypeStruct((4,) + s.shape, s.dtype) for s in self.arrays]
        return [jax.ShapeDtypeStruct((8,) + s.shape[1:], s.dtype) for s in self.arrays]

    def sems(self):
        return [pltpu.SemaphoreType.DMA((self.n, 19)), pltpu.SemaphoreType.DMA((self.n, 19))]

    def phases(self, ins, outs, send_sems, recv_sems):
        n, scatter = self.n, self.kind == "scatter"
        x, y, c = _place()
        me, sib = (x, y, c), (x, y, 1 - c)
        mine = 2 * x + y
        chips = _other_chips(x, y)
        own_pair = 18

        def region(a, slot, half):
            h = outs[a].shape[1] // 2
            return outs[a].at[slot, pl.ds(half * h, h)]

        def copy(a, k, slot, half, to, src=None):
            return pltpu.make_async_remote_copy(
                src_ref=region(a, slot, half) if src is None else src, dst_ref=region(a, slot, half),
                send_sem=send_sems.at[a, k], recv_sem=recv_sems.at[a, k], device_id=to, device_id_type=MESH)

        def over_ici(t, to_core, from_core):
            return 4 * t + 2 * to_core + from_core

        def passed_on(t, from_core):
            return 12 + 2 * t + from_core

        senders = [(t, cc) for t in range(3) for cc in ((0, 1) if scatter else (c,))]

        def slot_of(t, cc):
            cx, cy = chips[t]
            return 2 * (2 * cx + cy) + cc if scatter else 2 * cx + cy

        def first_copies():
            cps = []
            for a in range(n):
                h = outs[a].shape[1] // 2
                for t, (cx, cy) in enumerate(chips):
                    if scatter:
                        for half in (0, 1):
                            cps.append(copy(a, over_ici(t, half, c), 2 * mine + c, half, (cx, cy, half),
                                            src=ins[a].at[2 * cx + cy, pl.ds(half * h, h)]))
                    else:
                        cps.append(copy(a, over_ici(t, c, c), mine, c, (cx, cy, c), src=ins[a].at[pl.ds(c * h, h)]))
                if scatter:
                    cps.append(pltpu.make_async_remote_copy(
                        src_ref=ins[a].at[mine], dst_ref=outs[a].at[2 * mine + c], send_sem=send_sems.at[a, own_pair],
                        recv_sem=recv_sems.at[a, own_pair], device_id=sib, device_id_type=MESH))
            return cps

        def forward_copies():
            return [copy(a, passed_on(t, cc), slot_of(t, cc), c, sib) for a in range(n) for t, cc in senders]

        def start():
            for cp in first_copies():
                cp.start()

        def forward():
            fws = iter(forward_copies())
            for a in range(n):
                for t, cc in senders:
                    copy(a, over_ici(t, c, cc), slot_of(t, cc), c, me).wait_recv()
                    next(fws).start()

        def finish():
            for a in range(n):
                for t, cc in senders:
                    from_core = cc if scatter else 1 - c
                    copy(a, passed_on(t, from_core), slot_of(t, from_core), 1 - c, me).wait_recv()
                if scatter:
                    pltpu.make_async_remote_copy(
                        src_ref=ins[a].at[mine], dst_ref=outs[a].at[2 * mine + 1 - c], send_sem=send_sems.at[a, own_pair],
                        recv_sem=recv_sems.at[a, own_pair], device_id=me, device_id_type=MESH).wait_recv()
            for cp in first_copies() + forward_copies():
                cp.wait_send()

        return start, forward, finish

    def fill_own(self, outs):
        x, y, c = _place()
        if self.kind == "gather":
            return [lax.dynamic_update_index_in_dim(o, s, 2 * x + y, 0) for o, s in zip(outs, self.arrays)]
        return [lax.dynamic_update_index_in_dim(o, lax.dynamic_index_in_dim(s, 2 * x + y, 0, keepdims=False), 2 * (2 * x + y) + c, 0)
                for o, s in zip(outs, self.arrays)]

    def run(self, name):
        n = self.n

        def body(*refs):
            start, forward, finish = self.phases(refs[:n], refs[n:2 * n], *refs[2 * n:])
            start()
            forward()
            finish()

        outs = pl.pallas_call(
            body, in_specs=[ANY] * n, out_specs=[ANY] * n, out_shape=self.out_shapes(), scratch_shapes=self.sems(), name=name,
        )(*self.arrays)
        return self.fill_own(outs)


def _host_exchange(exchange, refs, n_in, n_out, step, n_steps):
    if exchange is None:
        return refs
    n = exchange.n
    own_in, ex_in = refs[:n_in], refs[n_in:n_in + n]
    own_out, ex_out = refs[n_in + n:n_in + n + n_out], refs[n_in + n + n_out:n_in + 2 * n + n_out]
    rest = refs[n_in + 2 * n + n_out:]
    start, forward, finish = exchange.phases(ex_in, ex_out, rest[-2], rest[-1])
    pl.when(step == 0)(start)
    pl.when(step == (2 * n_steps) // 3)(forward)
    pl.when(step == n_steps - 1)(finish)
    return own_in + own_out + rest[:-2]


def _hosted_call(exchange, body, *, grid, in_specs, out_specs, out_shape, scratch_shapes, compiler_params, name, args):
    if exchange is None:
        return pl.pallas_call(body, grid=grid, in_specs=in_specs, out_specs=out_specs, out_shape=out_shape,
                              scratch_shapes=scratch_shapes, compiler_params=compiler_params, name=name)(*args)
    n = exchange.n
    res = pl.pallas_call(
        body, grid=grid, in_specs=list(in_specs) + [ANY] * n, out_specs=list(out_specs) + [ANY] * n,
        out_shape=list(out_shape) + exchange.out_shapes(), scratch_shapes=list(scratch_shapes) + exchange.sems(),
        compiler_params=compiler_params, name=name)(*args, *exchange.arrays)
    return list(res[:len(out_shape)]) + [exchange.fill_own(res[len(out_shape):])]


def sum_slots(parts, name):
    S, R, C = parts.shape
    br = 128 if R % 128 == 0 else R

    def body(p_ref, o_ref):
        acc = p_ref[0].astype(F32)
        for s in range(1, S):
            acc = acc + p_ref[s].astype(F32)
        o_ref[...] = acc

    return pl.pallas_call(
        body,
        grid=(R // br,),
        in_specs=[pl.BlockSpec((S, br, C), lambda i: (0, i, 0))],
        out_specs=pl.BlockSpec((br, C), lambda i: (i, 0)),
        out_shape=jax.ShapeDtypeStruct((R, C), F32),
        compiler_params=_params(),
        name=name,
    )(parts)


def _adamw_math(w, g, m, v):
    m_new = ADAM_B1 * m + (1.0 - ADAM_B1) * g
    v_new = ADAM_B2 * v + (1.0 - ADAM_B2) * (g * g)
    m_hat = m_new / (1.0 - ADAM_B1 ** ADAM_STEP)
    v_hat = v_new / (1.0 - ADAM_B2 ** ADAM_STEP)
    return -ADAM_LR * (m_hat / (jnp.sqrt(v_hat) + ADAM_EPS) + ADAM_WD * w), m_new, v_new


def reduce_adamw(slots, w, m, v, name):
    S, R, C = slots.shape
    br = 128

    def body(p_ref, w_ref, m_ref, v_ref, g_ref, d_ref, nm_ref, nv_ref):
        g = p_ref[0].astype(F32)
        for s in range(1, S):
            g = g + p_ref[s].astype(F32)
        g_ref[...] = g
        d_ref[...], nm_ref[...], nv_ref[...] = _adamw_math(w_ref[...], g, m_ref[...], v_ref[...])

    spec = pl.BlockSpec((br, C), lambda i: (i, 0))
    return pl.pallas_call(
        body,
        grid=(R // br,),
        in_specs=[pl.BlockSpec((S, br, C), lambda i: (0, i, 0)), spec, spec, spec],
        out_specs=[spec] * 4,
        out_shape=[jax.ShapeDtypeStruct((R, C), F32)] * 4,
        compiler_params=_params(),
        name=name,
    )(slots, w, m, v)


SMALL_ROWS = 72


def gather_small(vec):
    def body(v_ref, o_ref, send_sems, recv_sems, local_sem):
        x, y, c = _place()
        flips = [(fx, fy, fc) for fx in (0, 1) for fy in (0, 1) for fc in (0, 1)][1:]

        def peer(f):
            return (1 - x if f[0] else x, 1 - y if f[1] else y, 1 - c if f[2] else c)

        slot = lambda p: 4 * p[0] + 2 * p[1] + p[2]
        own = pltpu.make_async_copy(v_ref, o_ref.at[slot((x, y, c))], local_sem)
        own.start()
        cps = [pltpu.make_async_remote_copy(
            src_ref=v_ref, dst_ref=o_ref.at[slot((x, y, c))], send_sem=send_sems.at[k], recv_sem=recv_sems.at[k],
            device_id=peer(f), device_id_type=MESH) for k, f in enumerate(flips)]
        for cp in cps:
            cp.start()
        for k, f in enumerate(flips):
            pltpu.make_async_remote_copy(
                src_ref=v_ref, dst_ref=o_ref.at[slot(peer(f))], send_sem=send_sems.at[k], recv_sem=recv_sems.at[k],
                device_id=(x, y, c), device_id_type=MESH).wait_recv()
        for cp in cps:
            cp.wait_send()
        own.wait()

    return pl.pallas_call(
        body,
        in_specs=[ANY],
        out_specs=ANY,
        out_shape=jax.ShapeDtypeStruct((8,) + vec.shape, vec.dtype),
        scratch_shapes=[pltpu.SemaphoreType.DMA((7,)), pltpu.SemaphoreType.DMA((7,)), pltpu.SemaphoreType.DMA],
        name="gather_small",
    )(vec)


GLOW_PAD = LANE - GLA_RANK


def kernel(x, attn_norm_g, w_in, gla_gate_w2, gla_gate_b, gla_norm_g, rel_bias, w_out, mlp_norm_g, w_ff1, w_ff2, final_norm_g, loss_target, m_attn_norm_g, m_w_in, m_gla_gate_w2, m_gla_gate_b, m_gla_norm_g, m_rel_bias, m_w_out, m_mlp_norm_g, m_w_ff1, m_w_ff2, m_final_norm_g, v_attn_norm_g, v_w_in, v_gla_gate_w2, v_gla_gate_b, v_gla_norm_g, v_rel_bias, v_w_out, v_mlp_norm_g, v_w_ff1, v_w_ff2, v_final_norm_g):
    xs, tgt = x[0], loss_target[0]
    T = xs.shape[0]
    cx, cy, cc = _place()
    chip = 2 * cx + cy
    gf = final_norm_g.reshape(1, D_MODEL)

    win_g, w2_g = Exchange("gather", [w_in[0].astype(BF16), gla_gate_w2[0]]).run("gather_w_in")
    win = jnp.transpose(win_g, (1, 0, 2)).reshape(D_MODEL, D_IN)
    n_glow = R_GLOW + GLA_RANK
    wp = jnp.concatenate([win[:, :n_glow], jnp.zeros((D_MODEL, GLOW_PAD), BF16), win[:, n_glow:]], axis=1)
    w2 = jnp.transpose(w2_g, (1, 0, 2)).reshape(GLA_RANK, GLA_QK)
    w2p = jnp.concatenate([w2, jnp.zeros((GLOW_PAD, GLA_QK), F32)], axis=0)

    proj, nx = inproj(xs, attn_norm_g, wp)
    tab = bias_tables(rel_bias)
    ob, lse, (wout_g, wff1, wff2_g) = dsa_fwd(
        proj, tab, Exchange("gather", [w_out[0].astype(BF16), w_ff1[0].astype(BF16), w_ff2[0].astype(BF16)]))
    wout = wout_g.reshape(D_MODEL, D_MODEL)
    wff2 = wff2_g.reshape(D_FF, D_MODEL)
    oa, opre, sprev = gla_fwd(proj, w2p, gla_gate_b, gla_norm_g)
    mixed, nm, act, dpre, dh2, dh1, dmixed, loss, dgf, dg2 = post_fused(xs, oa, ob, tgt, mlp_norm_g, gf, wout, wff1, wff2)

    late = [
        wgrad(mixed, dh1, "wgrad_out").reshape(4, D_MODEL // 4, D_MODEL),
        wgrad(nm, dpre, "wgrad_ff1", bn=FF_BLOCK, col_blocked=True),
        wgrad(act, dh2, "wgrad_ff2", bm=FF_BLOCK).reshape(4, FF_BLOCK, D_MODEL),
    ]
    da, dw2p, dbg, dgn = gla_bwd(proj, w2p, gla_gate_b, gla_norm_g, opre, sprev, dmixed)
    dq, dk, dv, dtab, late_slots = dsa_bwd(proj, tab, ob, lse, dmixed, Exchange("scatter", late))
    slots = dict(zip(["w_out", "w_ff1", "w_ff2"], late_slots))
    drel = bias_tables_bwd(dtab)

    dwa = wgrad(nx, da, "wgrad_in_gla")
    dwq = wgrad(nx, dq, "wgrad_in_q")
    dwk = wgrad(nx, dk, "wgrad_in_k")
    dwv = wgrad(nx, dv, "wgrad_in_v")
    dwin = jnp.concatenate([dwa[:, :n_glow], dwq, dwk, dwv], axis=1)
    dwin = [jnp.transpose(dwin.reshape(D_MODEL, 4, D_IN // 4), (1, 0, 2))]
    dxs, dg1, (slots["w_in"],) = dx_final(xs, dh1, attn_norm_g, da, dq, dk, dv, wp, Exchange("scatter", dwin))

    sizes = [D_MODEL, GLA_QK, GLA_WIDTH, REL_BUCKETS * DSA_HEADS, D_MODEL, D_MODEL, GLA_RANK * GLA_QK, 1]
    small = jnp.concatenate([dg1.reshape(-1), dbg.reshape(-1), dgn.reshape(-1), drel.reshape(-1), dg2.reshape(-1),
                             dgf.reshape(-1), dw2p[:GLA_RANK].reshape(-1), loss.reshape(-1),
                             jnp.zeros((SMALL_ROWS * LANE - sum(sizes),), F32)]).reshape(SMALL_ROWS, LANE)
    tot = sum_slots(gather_small(small), "sum_small").reshape(-1)
    offs = np.concatenate([[0], np.cumsum(sizes)])
    piece = lambda i: tot[int(offs[i]):int(offs[i + 1])]
    g_g1 = piece(0).reshape(1, D_MODEL)
    g_bg = piece(1).reshape(1, GLA_QK)
    g_gn = piece(2).reshape(1, GLA_WIDTH)
    g_rel = piece(3).reshape(REL_BUCKETS, DSA_HEADS)
    g_g2 = piece(4).reshape(1, D_MODEL)
    g_gf = piece(5).reshape(1, D_MODEL)
    g_w2 = lax.dynamic_slice_in_dim(piece(6).reshape(GLA_RANK, GLA_QK), chip * (GLA_QK // 4), GLA_QK // 4, axis=1)

    loss_all = piece(7)[0]

    upd = [
        ("attn_norm_g", attn_norm_g, g_g1, m_attn_norm_g, v_attn_norm_g),
        ("w_in", w_in[0], None, m_w_in[0], v_w_in[0]),
        ("gla_gate_w2", gla_gate_w2[0], g_w2, m_gla_gate_w2[0], v_gla_gate_w2[0]),
        ("gla_gate_b", gla_gate_b, g_bg, m_gla_gate_b, v_gla_gate_b),
        ("gla_norm_g", gla_norm_g, g_gn, m_gla_norm_g, v_gla_norm_g),
        ("rel_bias", rel_bias, g_rel, m_rel_bias, v_rel_bias),
        ("w_out", w_out[0], None, m_w_out[0], v_w_out[0]),
        ("mlp_norm_g", mlp_norm_g, g_g2, m_mlp_norm_g, v_mlp_norm_g),
        ("w_ff1", w_ff1[0], None, m_w_ff1[0], v_w_ff1[0]),
        ("w_ff2", w_ff2[0], None, m_w_ff2[0], v_w_ff2[0]),
        ("final_norm_g", gf, g_gf, m_final_norm_g.reshape(1, D_MODEL), v_final_norm_g.reshape(1, D_MODEL)),
    ]
    shapes = [attn_norm_g.shape, w_in.shape, gla_gate_w2.shape, gla_gate_b.shape, gla_norm_g.shape, rel_bias.shape,
              w_out.shape, mlp_norm_g.shape, w_ff1.shape, w_ff2.shape, final_norm_g.shape]
    grads, deltas, new_m, new_v = [], [], [], []
    for (name, w, g, m, v), shape in zip(upd, shapes):
        if name in slots:
            g, d, nm_, nv_ = reduce_adamw(slots[name], w, m, v, "reduce_adamw_" + name)
        else:
            d, nm_, nv_ = adamw(w, g, m, v, "adamw_" + name)
        grads.append(g.reshape(shape))
        deltas.append(d.reshape(shape))
        new_m.append(nm_.reshape(shape))
        new_v.append(nv_.reshape(shape))
    return (loss_all, dxs.reshape(1, T, D_MODEL), *grads, *deltas, *new_m, *new_v)
```

```python
import functools
import math

import jax
import jax.numpy as jnp
import numpy as np
from jax import lax
from jax.experimental import pallas as pl
from jax.experimental.pallas import tpu as pltpu

F32 = jnp.float32
BF16 = jnp.bfloat16
MESH = pl.DeviceIdType.MESH

D_MODEL = 1024
GLA_WIDTH = 512
GLA_HEADS = 4
GLA_DK = 64
GLA_DV = 128
GLA_QK = GLA_HEADS * GLA_DK
GLA_RANK = 16
GLA_TAU = 16.0
GLA_CHUNK = 64
DSA_WIDTH = 512
DSA_HEADS = 8
DSA_DH = 64
DSA_DILATIONS = (1, 4, 16)
DSA_SPAN = 128
DSA_BLOCK = 128
DSA_SUPER = DSA_BLOCK * DSA_DILATIONS[-1]
REL_BUCKETS = 32
REL_MAX_DIST = 2048
D_FF = 4096
D_IN = 3088
EPS = 1e-6
NEG = -1e30
QK_SCALE = 0.125

ADAM_LR = 0.001
ADAM_B1 = 0.9
ADAM_B2 = 0.999
ADAM_EPS = 1e-08
ADAM_WD = 0.01
ADAM_STEP = 10

LANE = 128
P_GQ, P_GK, P_GV, P_GR = 0, 256, 512, 1024
P_GLOW = 1536
P_A = 1664
P_DQ, P_DK, P_DV = 1664, 2176, 2688
P_ALL = 3200
R_GLOW = 1536

VMEM_LIMIT = 56 * 1024 * 1024


def _params(sem=("arbitrary",), vmem=VMEM_LIMIT):
    return pltpu.CompilerParams(dimension_semantics=sem, vmem_limit_bytes=vmem)


def _dot(a, b):
    return jnp.dot(a, b, preferred_element_type=F32)


def _dot_nt(a, b):
    return lax.dot_general(a, b, (((1,), (1,)), ((), ())), preferred_element_type=F32)


def _dot_tn(a, b):
    return lax.dot_general(a, b, (((0,), (0,)), ((), ())), preferred_element_type=F32)


def _split3(x):
    x1 = x.astype(BF16)
    r1 = x - x1.astype(F32)
    x2 = r1.astype(BF16)
    x3 = (r1 - x2.astype(F32)).astype(BF16)
    return x1, x2, x3


def _dot_exact_lhs(m_bf16, x):
    x1, x2, x3 = _split3(x)
    return _dot(m_bf16, x1) + _dot(m_bf16, x2) + _dot(m_bf16, x3)


def _rstd(xf):
    return lax.rsqrt(jnp.mean(xf * xf, axis=-1, keepdims=True) + EPS)


def _load_once(hbm_ref, vmem_ref, sem):
    cp = pltpu.make_async_copy(hbm_ref, vmem_ref, sem)
    cp.start()
    cp.wait()


ANY = pl.BlockSpec(memory_space=pl.ANY)


def inproj(x, g1, wp):
    T = x.shape[0]
    tm = 256

    def body(x_ref, g_ref, w_hbm, proj_ref, nx_ref, w_vmem, sem):
        @pl.when(pl.program_id(0) == 0)
        def _():
            _load_once(w_hbm, w_vmem, sem)

        xf = x_ref[...]
        nx = ((xf * _rstd(xf)) * g_ref[...]).astype(BF16)
        nx_ref[...] = nx
        proj_ref[...] = _dot(nx, w_vmem[...])

    return pl.pallas_call(
        body,
        grid=(T // tm,),
        in_specs=[pl.BlockSpec((tm, D_MODEL), lambda i: (i, 0)), pl.BlockSpec((1, D_MODEL), lambda i: (0, 0)), ANY],
        out_specs=[pl.BlockSpec((tm, P_ALL), lambda i: (i, 0)), pl.BlockSpec((tm, D_MODEL), lambda i: (i, 0))],
        out_shape=[jax.ShapeDtypeStruct((T, P_ALL), F32), jax.ShapeDtypeStruct((T, D_MODEL), BF16)],
        scratch_shapes=[pltpu.VMEM((D_MODEL, P_ALL), BF16), pltpu.SemaphoreType.DMA],
        compiler_params=_params(),
        name="inproj",
    )(x, g1, wp)


GLA_CHUNKS_PER_STEP = 8
GLA_ROWS = GLA_CHUNK * GLA_CHUNKS_PER_STEP


def _gla_masks():
    lane = lax.broadcasted_iota(jnp.int32, (1, GLA_QK), 1)
    return [(lane >= h * GLA_DK) & (lane < (h + 1) * GLA_DK) for h in range(GLA_HEADS)]


def _log_sigmoid(x):
    return jnp.minimum(x, 0.0) - jnp.log(1.0 + jnp.exp(-jnp.abs(x)))


def _sigmoid(x):
    return 1.0 / (1.0 + jnp.exp(-x))


def _head_cols(h):
    return slice(h * GLA_DV, (h + 1) * GLA_DV)


GLA_GROUP = 256


def _gla_step_constants():
    ri = lax.broadcasted_iota(jnp.int32, (GLA_GROUP, GLA_GROUP), 0)
    ci = lax.broadcasted_iota(jnp.int32, (GLA_GROUP, GLA_GROUP), 1)
    shift = GLA_CHUNK.bit_length() - 1
    same = lax.shift_right_logical(ri, shift) == lax.shift_right_logical(ci, shift)
    return same & (ri >= ci), same & (ri <= ci), _gla_masks()


def _by_group(fn, *arrays):
    outs = [fn(*[a[g * GLA_GROUP:(g + 1) * GLA_GROUP] for a in arrays]) for g in range(GLA_ROWS // GLA_GROUP)]
    if isinstance(outs[0], tuple):
        return tuple(jnp.concatenate(parts, axis=0) for parts in zip(*outs))
    return jnp.concatenate(outs, axis=0)


def _per_chunk(x):
    return x.reshape(GLA_CHUNKS_PER_STEP, GLA_CHUNK, x.shape[-1])


def _chunk_rows_of(x, c):
    return x[c * GLA_CHUNK:(c + 1) * GLA_CHUNK]


def _stack_masked(x, masks):
    return jnp.concatenate([jnp.where(m, x, 0.0) for m in masks], axis=0)


def _stack_head_cols(x):
    return jnp.concatenate([x[:, _head_cols(h)] for h in range(GLA_HEADS)], axis=0)


def _diag_blocks(full, masks):
    out = jnp.where(masks[0], full[:GLA_DV], 0.0)
    for h in range(1, GLA_HEADS):
        out = out + jnp.where(masks[h], full[h * GLA_DV:(h + 1) * GLA_DV], 0.0)
    return out


def _row_blocks_masked(full, masks):
    out = jnp.where(masks[0], full[:GLA_CHUNK], 0.0)
    for h in range(1, GLA_HEADS):
        out = out + jnp.where(masks[h], full[h * GLA_CHUNK:(h + 1) * GLA_CHUNK], 0.0)
    return out


def _gla_step_common(q, k, glow_b, w2, bg, tri):
    gpre = _dot(glow_b, w2) + bg
    glog = _log_sigmoid(gpre) / GLA_TAU
    b = _by_group(lambda g: _dot_exact_lhs(tri, g), glog)
    bl = jnp.sum(_per_chunk(glog), axis=1, keepdims=True)
    eb = jnp.exp(b)
    enb = jnp.exp(-b)
    eke = jnp.exp(jnp.broadcast_to(bl, (GLA_CHUNKS_PER_STEP, GLA_CHUNK, GLA_QK)).reshape(GLA_ROWS, GLA_QK) - b)
    return gpre, eb, enb, eke, jnp.exp(bl), (q * QK_SCALE) * eb, k * enb, k * eke


def gla_fwd(proj, w2p, bg, gn):
    T = proj.shape[0]
    n_steps = T // GLA_ROWS
    n_chunks = T // GLA_CHUNK

    def body(proj_ref, w2_ref, bg_ref, gn_ref, oa_ref, opre_ref, sprev_ref, st_ref):
        @pl.when(pl.program_id(0) == 0)
        def _():
            st_ref[...] = jnp.zeros_like(st_ref)

        causal, _, masks = _gla_step_constants()
        q = proj_ref[:, P_GQ:P_GQ + GLA_QK]
        k = proj_ref[:, P_GK:P_GK + GLA_QK]
        v = proj_ref[:, P_GV:P_GV + GLA_WIDTH]
        r = proj_ref[:, P_GR:P_GR + GLA_WIDTH]
        glow = proj_ref[:, P_GLOW:P_GLOW + LANE].astype(BF16)
        _, _, _, _, ebl, qd, ki, ke = _gla_step_common(q, k, glow, w2_ref[...].astype(BF16), bg_ref[...], causal.astype(BF16))
        ki_b = ki.astype(BF16)
        v_b = v.astype(BF16)
        o_heads = []
        for h in range(GLA_HEADS):
            def intra(qd_g, ki_g, v_g):
                att = jnp.where(causal, _dot_nt(qd_g, ki_g), 0.0)
                return _dot(att.astype(BF16), v_g)

            o_heads.append(_by_group(intra, jnp.where(masks[h], qd, 0.0).astype(BF16), ki_b, v_b[:, _head_cols(h)]))
        st = st_ref[...]
        states = []
        for c in range(GLA_CHUNKS_PER_STEP):
            states.append(st)
            sprev_ref[c] = st
            inc = _diag_blocks(_dot_tn(_chunk_rows_of(v_b, c), _chunk_rows_of(ke, c).astype(BF16)), masks)
            st = st * ebl[c] + inc
        st_ref[...] = st
        inter = []
        for c in range(GLA_CHUNKS_PER_STEP):
            qd_c = _stack_masked(_chunk_rows_of(qd, c), masks).astype(BF16)
            got = _dot_nt(qd_c, states[c].astype(BF16))
            inter.append(jnp.concatenate([got[h * GLA_CHUNK:(h + 1) * GLA_CHUNK] for h in range(GLA_HEADS)], axis=1))
        o = jnp.concatenate(o_heads, axis=1) + jnp.concatenate(inter, axis=0)
        opre_ref[...] = o
        on = jnp.concatenate([o[:, _head_cols(h)] * _rstd(o[:, _head_cols(h)]) for h in range(GLA_HEADS)], axis=1)
        oa_ref[...] = ((on * gn_ref[...]) * (r * _sigmoid(r))).astype(BF16)

    return pl.pallas_call(
        body,
        grid=(n_steps,),
        in_specs=[
            pl.BlockSpec((GLA_ROWS, P_A), lambda i: (i, 0)),
            pl.BlockSpec((LANE, GLA_QK), lambda i: (0, 0)),
            pl.BlockSpec((1, GLA_QK), lambda i: (0, 0)),
            pl.BlockSpec((1, GLA_WIDTH), lambda i: (0, 0)),
        ],
        out_specs=[
            pl.BlockSpec((GLA_ROWS, GLA_WIDTH), lambda i: (i, 0)),
            pl.BlockSpec((GLA_ROWS, GLA_WIDTH), lambda i: (i, 0)),
            pl.BlockSpec((GLA_CHUNKS_PER_STEP, GLA_DV, GLA_QK), lambda i: (i, 0, 0)),
        ],
        out_shape=[
            jax.ShapeDtypeStruct((T, GLA_WIDTH), BF16),
            jax.ShapeDtypeStruct((T, GLA_WIDTH), F32),
            jax.ShapeDtypeStruct((n_chunks, GLA_DV, GLA_QK), F32),
        ],
        scratch_shapes=[pltpu.VMEM((GLA_DV, GLA_QK), F32)],
        compiler_params=_params(),
        name="gla_fwd",
    )(proj, w2p, bg, gn)


def gla_bwd(proj, w2p, bg, gn, opre, sprev, dmixed, exchange=None):
    T = proj.shape[0]
    n_steps = T // GLA_ROWS

    def body(*refs):
        refs = _host_exchange(exchange, refs, 7, 4, pl.program_id(0), n_steps)
        proj_ref, w2_ref, bg_ref, gn_ref, opre_ref, sprev_ref, doa_ref, da_ref, dw2_ref, dbg_ref, dgn_ref, dst_ref = refs

        @pl.when(pl.program_id(0) == 0)
        def _():
            dst_ref[...] = jnp.zeros_like(dst_ref)
            dw2_ref[...] = jnp.zeros_like(dw2_ref)
            dbg_ref[...] = jnp.zeros_like(dbg_ref)
            dgn_ref[...] = jnp.zeros_like(dgn_ref)

        causal, causal_t, masks = _gla_step_constants()
        w2 = w2_ref[...].astype(BF16)
        gn = gn_ref[...]
        q = proj_ref[:, P_GQ:P_GQ + GLA_QK]
        k = proj_ref[:, P_GK:P_GK + GLA_QK]
        v_b = proj_ref[:, P_GV:P_GV + GLA_WIDTH].astype(BF16)
        r = proj_ref[:, P_GR:P_GR + GLA_WIDTH]
        glow = proj_ref[:, P_GLOW:P_GLOW + LANE].astype(BF16)
        o = opre_ref[...]
        doa = doa_ref[...]
        gpre, eb, enb, eke, ebl, qd, ki, ke = _gla_step_common(q, k, glow, w2, bg_ref[...], causal.astype(BF16))
        sig = _sigmoid(r)
        rs = jnp.concatenate([jnp.broadcast_to(_rstd(o[:, _head_cols(h)]), (GLA_ROWS, GLA_DV)) for h in range(GLA_HEADS)], axis=1)
        on = o * rs
        d_ong = doa * (r * sig)
        dr = doa * (on * gn) * (sig * (1.0 + r * (1.0 - sig)))
        dgn_ref[...] += jnp.sum(d_ong * on, axis=0, keepdims=True)
        d_on = d_ong * gn
        t = d_on * on
        mean_t = jnp.concatenate([jnp.broadcast_to(jnp.mean(t[:, _head_cols(h)], axis=-1, keepdims=True), (GLA_ROWS, GLA_DV))
                                  for h in range(GLA_HEADS)], axis=1)
        do_b = (rs * (d_on - on * mean_t)).astype(BF16)
        ki_b = ki.astype(BF16)
        ke_b = ke.astype(BF16)
        dqd = jnp.zeros_like(qd)
        dki = jnp.zeros_like(qd)
        dv_heads = []
        for h in range(GLA_HEADS):
            qd_h = jnp.where(masks[h], qd, 0.0).astype(BF16)
            do_h = do_b[:, _head_cols(h)]

            def intra(qd_g, ki_g, v_g, do_g):
                att = jnp.where(causal, _dot_nt(qd_g, ki_g), 0.0).astype(BF16)
                d_att = jnp.where(causal, _dot_nt(do_g, v_g), 0.0).astype(BF16)
                return _dot_tn(att, do_g), _dot(d_att, ki_g), _dot_tn(d_att, qd_g)

            dv_h, dqd_h, dki_h = _by_group(intra, qd_h, ki_b, v_b[:, _head_cols(h)], do_h)
            dv_heads.append(dv_h)
            dqd = dqd + jnp.where(masks[h], dqd_h, 0.0)
            dki = dki + dki_h
        states = [sprev_ref[c] for c in range(GLA_CHUNKS_PER_STEP)]
        dqd_inter, dst_adds = [], []
        for c in range(GLA_CHUNKS_PER_STEP):
            do_c = _stack_head_cols(_chunk_rows_of(do_b, c))
            dqd_inter.append(_row_blocks_masked(_dot(do_c, states[c].astype(BF16)), masks))
            dst_adds.append(_diag_blocks(_dot_tn(_chunk_rows_of(do_b, c), _chunk_rows_of(qd, c).astype(BF16)), masks))
        dst = dst_ref[...]
        dsts, debls = [None] * GLA_CHUNKS_PER_STEP, [None] * GLA_CHUNKS_PER_STEP
        for c in reversed(range(GLA_CHUNKS_PER_STEP)):
            dsts[c] = dst
            debls[c] = jnp.sum(dst * states[c], axis=0, keepdims=True)
            dst = dst * ebl[c] + dst_adds[c]
        dst_ref[...] = dst
        dv_inter, dke = [], []
        for c in range(GLA_CHUNKS_PER_STEP):
            dst_b = dsts[c].astype(BF16)
            got = _dot_nt(_stack_masked(_chunk_rows_of(ke, c), masks).astype(BF16), dst_b)
            dv_inter.append(jnp.concatenate([got[h * GLA_CHUNK:(h + 1) * GLA_CHUNK] for h in range(GLA_HEADS)], axis=1))
            dke.append(_row_blocks_masked(_dot(_stack_head_cols(_chunk_rows_of(v_b, c)), dst_b), masks))
        dqd = dqd + jnp.concatenate(dqd_inter, axis=0)
        dke = jnp.concatenate(dke, axis=0)
        dv = jnp.concatenate(dv_heads, axis=1) + jnp.concatenate(dv_inter, axis=0)
        dkk = dke * ke
        dbl = jnp.sum(_per_chunk(dkk), axis=1, keepdims=True) + jnp.stack(debls) * ebl
        last_row = lax.broadcasted_iota(jnp.int32, (GLA_CHUNKS_PER_STEP, GLA_CHUNK, GLA_QK), 1) == GLA_CHUNK - 1
        db = dqd * qd - dki * ki - dkk + jnp.where(last_row, dbl, 0.0).reshape(GLA_ROWS, GLA_QK)
        tri_t = causal_t.astype(BF16)
        dglog = _by_group(lambda g: _dot_exact_lhs(tri_t, g), db)
        dgpre = (dglog / GLA_TAU) * _sigmoid(-gpre)
        dgpre_b = dgpre.astype(BF16)
        da_ref[...] = jnp.concatenate(
            [dqd * eb * QK_SCALE, dki * enb + dke * eke, dv, dr, _dot_nt(dgpre_b, w2)], axis=1).astype(BF16)
        dw2_ref[...] += _dot_tn(glow, dgpre_b)
        dbg_ref[...] += jnp.sum(dgpre, axis=0, keepdims=True)

    rev = lambda i: (n_steps - 1 - i, 0)
    return _hosted_call(
        exchange,
        body,
        grid=(n_steps,),
        in_specs=[
            pl.BlockSpec((GLA_ROWS, P_A), rev),
            pl.BlockSpec((LANE, GLA_QK), lambda i: (0, 0)),
            pl.BlockSpec((1, GLA_QK), lambda i: (0, 0)),
            pl.BlockSpec((1, GLA_WIDTH), lambda i: (0, 0)),
            pl.BlockSpec((GLA_ROWS, GLA_WIDTH), rev),
            pl.BlockSpec((GLA_CHUNKS_PER_STEP, GLA_DV, GLA_QK), lambda i: (n_steps - 1 - i, 0, 0)),
            pl.BlockSpec((GLA_ROWS, GLA_WIDTH), rev),
        ],
        out_specs=[
            pl.BlockSpec((GLA_ROWS, P_A), rev),
            pl.BlockSpec((LANE, GLA_QK), lambda i: (0, 0)),
            pl.BlockSpec((1, GLA_QK), lambda i: (0, 0)),
            pl.BlockSpec((1, GLA_WIDTH), lambda i: (0, 0)),
        ],
        out_shape=[
            jax.ShapeDtypeStruct((T, P_A), BF16),
            jax.ShapeDtypeStruct((LANE, GLA_QK), F32),
            jax.ShapeDtypeStruct((1, GLA_QK), F32),
            jax.ShapeDtypeStruct((1, GLA_WIDTH), F32),
        ],
        scratch_shapes=[pltpu.VMEM((GLA_DV, GLA_QK), F32)],
        compiler_params=_params(),
        name="gla_bwd",
        args=(proj, w2p, bg, gn, opre, sprev, dmixed),
    )


def _t5_bucket(dist):
    max_exact = REL_BUCKETS // 2
    n = np.maximum(dist, 0)
    large = max_exact + (np.log(np.maximum(n, 1) / max_exact) / math.log(REL_MAX_DIST / max_exact)
                         * (REL_BUCKETS - max_exact)).astype(np.int32)
    large = np.minimum(large, REL_BUCKETS - 1)
    return np.where(n < max_exact, n, large).astype(np.int32)


SUBLANES = 8


def _bucket_rows():
    steps = DSA_BLOCK - np.arange(2 * DSA_BLOCK)
    in_band = (steps >= 0) & (steps <= DSA_SPAN)
    rows = np.stack([np.where(in_band, _t5_bucket(steps * d), -1) for d in DSA_DILATIONS]).astype(np.int32)
    return np.broadcast_to(rows[:, None, :], (len(DSA_DILATIONS), SUBLANES, 2 * DSA_BLOCK)).copy()


def bias_tables(rel_bias):
    ids = jnp.asarray(_bucket_rows())
    nd = len(DSA_DILATIONS)

    def body(rel_ref, ids_ref, tab_ref):
        h = pl.program_id(1)
        idt = ids_ref[0]
        row = jnp.where(idt < 0, NEG, 0.0).astype(F32)
        for b in range(REL_BUCKETS):
            row = jnp.where(idt == b, rel_ref[b, h], row)
        full = jnp.broadcast_to(row[0:1], (DSA_BLOCK, 2 * DSA_BLOCK))
        tab_ref[0, 0] = pltpu.roll(full, 0, 1, stride=1, stride_axis=0)

    return pl.pallas_call(
        body,
        grid=(nd, DSA_HEADS),
        in_specs=[pl.BlockSpec(memory_space=pltpu.SMEM), pl.BlockSpec((1, SUBLANES, 2 * DSA_BLOCK), lambda d, h: (d, 0, 0))],
        out_specs=pl.BlockSpec((1, 1, DSA_BLOCK, 2 * DSA_BLOCK), lambda d, h: (d, h, 0, 0)),
        out_shape=jax.ShapeDtypeStruct((nd, DSA_HEADS, DSA_BLOCK, 2 * DSA_BLOCK), F32),
        compiler_params=_params(("arbitrary", "arbitrary")),
        name="bias_tables",
    )(rel_bias, ids)


def _bucket_ids():
    L = DSA_BLOCK
    steps = L + np.arange(L)[:, None] - np.arange(2 * L)[None, :]
    in_band = (steps >= 0) & (steps <= DSA_SPAN)
    return np.stack([np.where(in_band, _t5_bucket(steps * d), -1) for d in DSA_DILATIONS]).astype(np.int32)


def bias_tables_bwd(dtab):
    ids = jnp.asarray(_bucket_ids())
    nd = len(DSA_DILATIONS)

    def body(dtab_ref, ids_ref, drel_ref):
        @pl.when((pl.program_id(0) == 0) & (pl.program_id(1) == 0))
        def _():
            for b in range(REL_BUCKETS):
                for h in range(DSA_HEADS):
                    drel_ref[b, h] = 0.0

        h = pl.program_id(1)
        idt = ids_ref[0]
        g = dtab_ref[0, 0]
        for b in range(REL_BUCKETS):
            drel_ref[b, h] += jnp.sum(jnp.where(idt == b, g, 0.0))

    return pl.pallas_call(
        body,
        grid=(nd, DSA_HEADS),
        in_specs=[pl.BlockSpec((1, 1, DSA_BLOCK, 2 * DSA_BLOCK), lambda d, h: (d, h, 0, 0)),
                  pl.BlockSpec((1, DSA_BLOCK, 2 * DSA_BLOCK), lambda d, h: (d, 0, 0))],
        out_specs=pl.BlockSpec(memory_space=pltpu.SMEM),
        out_shape=jax.ShapeDtypeStruct((REL_BUCKETS, DSA_HEADS), F32),
        compiler_params=_params(("arbitrary", "arbitrary")),
        name="bias_tables_bwd",
    )(dtab, ids)


DSA_PAIRS = DSA_HEADS // 2
DSA_UNROLL = 16
DSA_COMBINE_ROWS = 256


def _dsa_units(d):
    return d, DSA_SUPER // (DSA_BLOCK * d)


def _dsa_specs(T):
    nsb = T // DSA_SUPER
    qcol, kcol, vcol = P_DQ // LANE, P_DK // LANE, P_DV // LANE
    return nsb, qcol, kcol, vcol


def _head_lane_mask():
    return lax.broadcasted_iota(jnp.int32, (1, LANE), 1) < DSA_DH


def _fill_tile_variants(tab_ref, variants):
    col = lax.broadcasted_iota(jnp.int32, (2 * DSA_BLOCK, 2 * DSA_BLOCK), 1)
    for di in range(len(DSA_DILATIONS)):
        tile = tab_ref[di, 0]
        variants[di, 0] = tile
        variants[di, 1] = jnp.where(col < DSA_BLOCK, NEG, tile)


def _tile_variants_scratch():
    return pltpu.VMEM((len(DSA_DILATIONS), 2, 2 * DSA_BLOCK, 2 * DSA_BLOCK), F32)


def _pair_tiles(tab):
    return tab.reshape(len(DSA_DILATIONS), DSA_PAIRS, 2 * DSA_BLOCK, 2 * DSA_BLOCK)


def _stack_heads(t, head0):
    return jnp.concatenate([jnp.where(head0, t, 0.0), jnp.where(head0, 0.0, t)], axis=0)


def dsa_fwd(proj, tab, exchange=None):
    T = proj.shape[0]
    nsb, qcol, kcol, vcol = _dsa_specs(T)
    S = DSA_SUPER

    def body(*refs):
        refs = _host_exchange(exchange, refs, 6, 2, pl.program_id(0) * nsb + pl.program_id(1), DSA_PAIRS * nsb)
        q_ref, kp_ref, kc_ref, vp_ref, vc_ref, tab_ref, out_ref, lse_ref, kk, vv, ob, lb, tiles = refs
        sb = pl.program_id(1)
        kk[0:S, :] = kp_ref[...]
        kk[S:2 * S, :] = kc_ref[...]
        vv[0:S, :] = vp_ref[...]
        vv[S:2 * S, :] = vc_ref[...]
        head0 = _head_lane_mask()
        pl.when(sb == 0)(functools.partial(_fill_tile_variants, tab_ref, tiles))

        for di, d in enumerate(DSA_DILATIONS):
            n_res, n_blk = _dsa_units(d)

            def unit(u, carry, di=di, d=d, n_blk=n_blk):
                r = u // n_blk
                c = u % n_blk
                q0 = r + d * DSA_BLOCK * c
                qrows = pl.ds(q0, DSA_BLOCK, stride=d) if d > 1 else pl.ds(q0, DSA_BLOCK)
                krows = pl.ds(S + q0 - d * DSA_BLOCK, 2 * DSA_BLOCK, stride=d) if d > 1 else pl.ds(S + q0 - DSA_BLOCK, 2 * DSA_BLOCK)
                q2 = q_ref[qrows, :] * QK_SCALE
                k2 = kk[krows, :].astype(BF16)
                v2 = vv[krows, :].astype(BF16)
                qs = _stack_heads(q2, head0).astype(BF16)
                s = _dot_nt(qs, k2) + tiles[di, ((sb == 0) & (c == 0)).astype(jnp.int32)]
                m = jnp.max(s, axis=-1, keepdims=True)
                p = jnp.exp(s - m)
                den = jnp.sum(p, axis=-1, keepdims=True)
                o = _dot(p.astype(BF16), v2) / den
                l = jnp.broadcast_to(m + jnp.log(den), (2 * DSA_BLOCK, LANE))
                ob[di, qrows, :] = jnp.where(head0, o[:DSA_BLOCK], o[DSA_BLOCK:])
                lb[di, qrows, :] = jnp.where(head0, l[:DSA_BLOCK], l[DSA_BLOCK:])
                return carry

            lax.fori_loop(0, n_res * n_blk, unit, 0, unroll=DSA_UNROLL)

        def combine(i, carry):
            rows = pl.ds(pl.multiple_of(i * DSA_COMBINE_ROWS, DSA_COMBINE_ROWS), DSA_COMBINE_ROWS)
            l0, l1, l2 = lb[0, rows, :], lb[1, rows, :], lb[2, rows, :]
            mx = jnp.maximum(jnp.maximum(l0, l1), l2)
            e0, e1, e2 = jnp.exp(l0 - mx), jnp.exp(l1 - mx), jnp.exp(l2 - mx)
            den = e0 + e1 + e2
            out_ref[rows, :] = (e0 * ob[0, rows, :] + e1 * ob[1, rows, :] + e2 * ob[2, rows, :]) / den
            lse_ref[rows, :] = mx + jnp.log(den)
            return carry

        lax.fori_loop(0, S // DSA_COMBINE_ROWS, combine, 0)

    prev = lambda col: (lambda hp, sb: (jnp.maximum(sb - 1, 0), col + hp))
    cur = lambda col: (lambda hp, sb: (sb, col + hp))
    blk = lambda f: pl.BlockSpec((S, LANE), f)
    return _hosted_call(
        exchange,
        body,
        grid=(DSA_PAIRS, nsb),
        in_specs=[blk(cur(qcol)), blk(prev(kcol)), blk(cur(kcol)), blk(prev(vcol)), blk(cur(vcol)),
                  pl.BlockSpec((len(DSA_DILATIONS), 1, 2 * DSA_BLOCK, 2 * DSA_BLOCK), lambda hp, sb: (0, hp, 0, 0))],
        out_specs=[blk(lambda hp, sb: (sb, hp)), blk(lambda hp, sb: (sb, hp))],
        out_shape=[jax.ShapeDtypeStruct((T, DSA_WIDTH), F32), jax.ShapeDtypeStruct((T, DSA_WIDTH), F32)],
        scratch_shapes=[pltpu.VMEM((2 * S, LANE), F32), pltpu.VMEM((2 * S, LANE), F32),
                        pltpu.VMEM((len(DSA_DILATIONS), S, LANE), F32), pltpu.VMEM((len(DSA_DILATIONS), S, LANE), F32),
                        _tile_variants_scratch()],
        compiler_params=_params(("arbitrary", "arbitrary")),
        name="dsa_fwd",
        args=(proj, proj, proj, proj, proj, _pair_tiles(tab)),
    )


def dsa_bwd(proj, tab, ob_out, lse, dmixed, exchange=None):
    T = proj.shape[0]
    nsb, qcol, kcol, vcol = _dsa_specs(T)
    S = DSA_SUPER
    nd = len(DSA_DILATIONS)
    ocol = GLA_WIDTH // LANE

    def body(*refs):
        refs = _host_exchange(exchange, refs, 9, 4, pl.program_id(0) * nsb + pl.program_id(1), DSA_PAIRS * nsb)
        (q_ref, kp_ref, kc_ref, vp_ref, vc_ref, tab_ref, o_ref, lse_ref, do_ref,
         dq_ref, dk_ref, dv_ref, dtab_ref, kk, vv, dqa, dkk, dvv, tiles) = refs
        j = pl.program_id(1)
        sb = nsb - 1 - j
        kk[0:S, :] = kp_ref[...]
        kk[S:2 * S, :] = kc_ref[...]
        vv[0:S, :] = vp_ref[...]
        vv[S:2 * S, :] = vc_ref[...]
        head0 = _head_lane_mask()
        pl.when(j == 0)(functools.partial(_fill_tile_variants, tab_ref, tiles))

        @pl.when(j == 0)
        def _():
            dtab_ref[...] = jnp.zeros_like(dtab_ref)
            dkk[S:2 * S, :] = jnp.zeros((S, LANE), F32)
            dvv[S:2 * S, :] = jnp.zeros((S, LANE), F32)

        @pl.when(j > 0)
        def _():
            dkk[S:2 * S, :] = dkk[0:S, :]
            dvv[S:2 * S, :] = dvv[0:S, :]

        dkk[0:S, :] = jnp.zeros((S, LANE), F32)
        dvv[0:S, :] = jnp.zeros((S, LANE), F32)
        dqa[...] = jnp.zeros_like(dqa)

        for di, d in enumerate(DSA_DILATIONS):
            n_res, n_blk = _dsa_units(d)

            def unit(u, carry, di=di, d=d, n_blk=n_blk):
                r = u // n_blk
                c = u % n_blk
                q0 = r + d * DSA_BLOCK * c
                qrows = pl.ds(q0, DSA_BLOCK, stride=d) if d > 1 else pl.ds(q0, DSA_BLOCK)
                krows = pl.ds(S + q0 - d * DSA_BLOCK, 2 * DSA_BLOCK, stride=d) if d > 1 else pl.ds(S + q0 - DSA_BLOCK, 2 * DSA_BLOCK)
                q2 = q_ref[qrows, :] * QK_SCALE
                k2 = kk[krows, :].astype(BF16)
                v2 = vv[krows, :].astype(BF16)
                do2 = do_ref[qrows, :]
                o2 = o_ref[qrows, :]
                l2 = lse_ref[qrows, :]
                qs = _stack_heads(q2, head0).astype(BF16)
                dos = _stack_heads(do2, head0)
                dos_b = dos.astype(BF16)
                delta = jnp.sum(dos * jnp.concatenate([o2, o2], axis=0), axis=-1, keepdims=True)
                lse = jnp.concatenate([jnp.max(jnp.where(head0, l2, -jnp.inf), axis=-1, keepdims=True),
                                       jnp.max(jnp.where(head0, -jnp.inf, l2), axis=-1, keepdims=True)], axis=0)
                s = _dot_nt(qs, k2) + tiles[di, ((sb == 0) & (c == 0)).astype(jnp.int32)]
                p = jnp.exp(s - lse)
                ds = p * (_dot_nt(dos_b, v2) - delta)
                dtab_ref[di, 0] += ds
                ds_b = ds.astype(BF16)
                dq = _dot(ds_b, k2)
                dqa[qrows, :] += jnp.where(head0, dq[:DSA_BLOCK], dq[DSA_BLOCK:]) * QK_SCALE
                dkk[krows, :] += _dot_tn(ds_b, qs)
                dvv[krows, :] += _dot_tn(p.astype(BF16), dos_b)
                return carry

            lax.fori_loop(0, n_res * n_blk, unit, 0, unroll=DSA_UNROLL)

        dq_ref[...] = dqa[...].astype(BF16)
        dk_ref[...] = dkk[S:2 * S, :].astype(BF16)
        dv_ref[...] = dvv[S:2 * S, :].astype(BF16)

    prev = lambda col: (lambda hp, j: (jnp.maximum(nsb - 2 - j, 0), col + hp))
    cur = lambda col: (lambda hp, j: (nsb - 1 - j, col + hp))
    blk = lambda f: pl.BlockSpec((S, LANE), f)
    out_blk = blk(lambda hp, j: (nsb - 1 - j, hp))
    tab_blk = pl.BlockSpec((nd, 1, 2 * DSA_BLOCK, 2 * DSA_BLOCK), lambda hp, j: (0, hp, 0, 0))
    dq, dk, dv, dtab, *carried = _hosted_call(
        exchange,
        body,
        grid=(DSA_PAIRS, nsb),
        in_specs=[blk(cur(qcol)), blk(prev(kcol)), blk(cur(kcol)), blk(prev(vcol)), blk(cur(vcol)), tab_blk,
                  out_blk, out_blk, blk(cur(ocol))],
        out_specs=[out_blk, out_blk, out_blk, tab_blk],
        out_shape=[jax.ShapeDtypeStruct((T, DSA_WIDTH), BF16)] * 3
        + [jax.ShapeDtypeStruct((nd, DSA_PAIRS, 2 * DSA_BLOCK, 2 * DSA_BLOCK), F32)],
        scratch_shapes=[pltpu.VMEM((2 * S, LANE), F32), pltpu.VMEM((2 * S, LANE), F32), pltpu.VMEM((S, LANE), F32),
                        pltpu.VMEM((2 * S, LANE), F32), pltpu.VMEM((2 * S, LANE), F32), _tile_variants_scratch()],
        compiler_params=_params(("arbitrary", "arbitrary")),
        name="dsa_bwd",
        args=(proj, proj, proj, proj, proj, _pair_tiles(tab), ob_out, lse, dmixed),
    )
    return (dq, dk, dv, dtab.reshape(nd, DSA_HEADS, DSA_BLOCK, 2 * DSA_BLOCK), *carried)


FF_BLOCKS = 4
FF_BLOCK = D_FF // FF_BLOCKS


def post_fused(x, oa, ob, tgt, g2, gf, wout, wff1, wff2):
    T = x.shape[0]
    tm = 256
    inv_d = 1.0 / D_MODEL

    def body(x_ref, oa_ref, ob_ref, tgt_ref, g2_ref, gf_ref, wout_hbm, wff1_hbm, wff2_hbm,
             mixed_ref, nm_ref, a_ref, dpre_ref, dh2_ref, dh1_ref, dmixed_ref, loss_ref, dgf_ref, dg2_ref,
             wout_v, wff1_v, wff2_v, sems):
        @pl.when(pl.program_id(0) == 0)
        def _():
            cps = [pltpu.make_async_copy(s, d, sems.at[i])
                   for i, (s, d) in enumerate([(wout_hbm, wout_v), (wff1_hbm, wff1_v), (wff2_hbm, wff2_v)])]
            for cp in cps:
                cp.start()
            for cp in cps:
                cp.wait()
            loss_ref[...] = jnp.zeros_like(loss_ref)
            dgf_ref[...] = jnp.zeros_like(dgf_ref)
            dg2_ref[...] = jnp.zeros_like(dg2_ref)

        mixed = jnp.concatenate([oa_ref[...], ob_ref[...].astype(BF16)], axis=1)
        mixed_ref[...] = mixed
        h1 = x_ref[...] + _dot(mixed, wout_v[...])
        rs1 = _rstd(h1)
        hn1 = h1 * rs1
        g2 = g2_ref[...]
        nm = (hn1 * g2).astype(BF16)
        nm_ref[...] = nm
        relu = []
        mlp = jnp.zeros((tm, D_MODEL), F32)
        for j in range(FF_BLOCKS):
            cols = slice(j * FF_BLOCK, (j + 1) * FF_BLOCK)
            r_j = jnp.maximum(_dot(nm, wff1_v[j]), 0.0)
            a_j = (r_j * r_j).astype(BF16)
            a_ref[:, cols] = a_j
            relu.append(r_j)
            mlp = mlp + _dot(a_j, wff2_v[cols, :])
        h2 = h1 + mlp
        rsf = _rstd(h2)
        hnf = h2 * rsf
        gf = gf_ref[...]
        diff = hnf * gf - tgt_ref[...]
        loss_ref[...] += 0.5 * jnp.sum(jnp.sum(diff * diff, axis=-1, keepdims=True) * inv_d, axis=0, keepdims=True)
        dy = diff * inv_d
        dgf_ref[...] += jnp.sum(dy * hnf, axis=0, keepdims=True)
        dhnf = dy * gf
        dh2 = rsf * (dhnf - hnf * jnp.mean(dhnf * hnf, axis=-1, keepdims=True))
        dh2_b = dh2.astype(BF16)
        dh2_ref[...] = dh2_b
        dnm = jnp.zeros((tm, D_MODEL), F32)
        for j in range(FF_BLOCKS):
            cols = slice(j * FF_BLOCK, (j + 1) * FF_BLOCK)
            dpre_j = (_dot_nt(dh2_b, wff2_v[cols, :]) * (2.0 * relu[j])).astype(BF16)
            dpre_ref[:, cols] = dpre_j
            dnm = dnm + _dot_nt(dpre_j, wff1_v[j])
        dg2_ref[...] += jnp.sum(dnm * hn1, axis=0, keepdims=True)
        dhn1 = dnm * g2
        dh1 = dh2 + rs1 * (dhn1 - hn1 * jnp.mean(dhn1 * hn1, axis=-1, keepdims=True))
        dh1_ref[...] = dh1
        dmixed_ref[...] = _dot_nt(dh1.astype(BF16), wout_v[...])

    row = lambda w: pl.BlockSpec((tm, w), lambda i: (i, 0))
    vec = lambda w: pl.BlockSpec((1, w), lambda i: (0, 0))
    return pl.pallas_call(
        body,
        grid=(T // tm,),
        in_specs=[row(D_MODEL), row(GLA_WIDTH), row(DSA_WIDTH), row(D_MODEL), vec(D_MODEL), vec(D_MODEL), ANY, ANY, ANY],
        out_specs=[row(D_MODEL), row(D_MODEL), row(D_FF), row(D_FF), row(D_MODEL), row(D_MODEL), row(D_MODEL),
                   vec(1), vec(D_MODEL), vec(D_MODEL)],
        out_shape=[
            jax.ShapeDtypeStruct((T, D_MODEL), BF16),
            jax.ShapeDtypeStruct((T, D_MODEL), BF16),
            jax.ShapeDtypeStruct((T, D_FF), BF16),
            jax.ShapeDtypeStruct((T, D_FF), BF16),
            jax.ShapeDtypeStruct((T, D_MODEL), BF16),
            jax.ShapeDtypeStruct((T, D_MODEL), F32),
            jax.ShapeDtypeStruct((T, D_MODEL), F32),
            jax.ShapeDtypeStruct((1, 1), F32),
            jax.ShapeDtypeStruct((1, D_MODEL), F32),
            jax.ShapeDtypeStruct((1, D_MODEL), F32),
        ],
        scratch_shapes=[pltpu.VMEM((D_MODEL, D_MODEL), BF16), pltpu.VMEM((FF_BLOCKS, D_MODEL, FF_BLOCK), BF16),
                        pltpu.VMEM((D_FF, D_MODEL), BF16), pltpu.SemaphoreType.DMA((3,))],
        compiler_params=_params(),
        name="post_fused",
    )(x, oa, ob, tgt, g2, gf, wout, wff1, wff2)


WGRAD_TOKENS = 2048


def wgrad(a, b, name, bm=None, bn=None, col_blocked=False):
    T, M = a.shape
    N = b.shape[1]
    bm = M if bm is None else bm
    bn = N if bn is None else bn
    tk = min(WGRAD_TOKENS, T)
    n_k = T // tk

    def body(a_ref, b_ref, o_ref, acc_ref):
        part = _dot_tn(a_ref[...].astype(BF16), b_ref[...].astype(BF16))
        out = o_ref.at[0] if col_blocked else o_ref
        k = pl.program_id(2)
        if n_k == 1:
            out[...] = part.astype(BF16)
            return

        @pl.when(k == 0)
        def _():
            acc_ref[...] = part

        @pl.when((k > 0) & (k < n_k - 1))
        def _():
            acc_ref[...] += part

        @pl.when(k == n_k - 1)
        def _():
            out[...] = (acc_ref[...] + part).astype(BF16)

    if col_blocked:
        assert bm == M
        out_spec = pl.BlockSpec((1, M, bn), lambda i, j, k: (j, 0, 0))
        out_shape = jax.ShapeDtypeStruct((N // bn, M, bn), BF16)
    else:
        out_spec = pl.BlockSpec((bm, bn), lambda i, j, k: (i, j))
        out_shape = jax.ShapeDtypeStruct((M, N), BF16)
    return pl.pallas_call(
        body,
        grid=(M // bm, N // bn, n_k),
        in_specs=[pl.BlockSpec((tk, bm), lambda i, j, k: (k, i)), pl.BlockSpec((tk, bn), lambda i, j, k: (k, j))],
        out_specs=out_spec,
        out_shape=out_shape,
        scratch_shapes=[pltpu.VMEM((bm, bn), F32)],
        compiler_params=_params(("arbitrary", "arbitrary", "arbitrary")),
        name=name,
    )(a, b)


def dx_final(x, dh1, g1, da, dq, dk, dv, wp, exchange=None):
    T = x.shape[0]
    tm = 256

    def body(*refs):
        refs = _host_exchange(exchange, refs, 8, 2, pl.program_id(0), T // tm)
        x_ref, dh1_ref, g_ref, da_ref, dq_ref, dk_ref, dv_ref, w_hbm, dx_ref, dg_ref, w_vmem, sem = refs

        @pl.when(pl.program_id(0) == 0)
        def _():
            _load_once(w_hbm, w_vmem, sem)
            dg_ref[...] = jnp.zeros_like(dg_ref)

        dnx = (_dot_nt(da_ref[...], w_vmem[:, 0:P_A]) + _dot_nt(dq_ref[...], w_vmem[:, P_DQ:P_DQ + DSA_WIDTH])
               + _dot_nt(dk_ref[...], w_vmem[:, P_DK:P_DK + DSA_WIDTH]) + _dot_nt(dv_ref[...], w_vmem[:, P_DV:P_DV + DSA_WIDTH]))
        xf = x_ref[...]
        rs = _rstd(xf)
        hn = xf * rs
        dg_ref[...] += jnp.sum(dnx * hn, axis=0, keepdims=True)
        dhn = dnx * g_ref[...]
        dx_ref[...] = dh1_ref[...] + rs * (dhn - hn * jnp.mean(dhn * hn, axis=-1, keepdims=True))

    row = lambda w: pl.BlockSpec((tm, w), lambda i: (i, 0))
    vec = pl.BlockSpec((1, D_MODEL), lambda i: (0, 0))
    return _hosted_call(
        exchange,
        body,
        grid=(T // tm,),
        in_specs=[row(D_MODEL), row(D_MODEL), vec, row(P_A), row(DSA_WIDTH), row(DSA_WIDTH), row(DSA_WIDTH), ANY],
        out_specs=[row(D_MODEL), vec],
        out_shape=[jax.ShapeDtypeStruct((T, D_MODEL), F32), jax.ShapeDtypeStruct((1, D_MODEL), F32)],
        scratch_shapes=[pltpu.VMEM((D_MODEL, P_ALL), BF16), pltpu.SemaphoreType.DMA],
        compiler_params=_params(),
        name="dx_final",
        args=(x, dh1, g1, da, dq, dk, dv, wp),
    )


def adamw(w, g, m, v, name):
    R, C = w.shape
    br = 256 if R % 256 == 0 else R

    def body(w_ref, g_ref, m_ref, v_ref, d_ref, nm_ref, nv_ref):
        d_ref[...], nm_ref[...], nv_ref[...] = _adamw_math(w_ref[...], g_ref[...], m_ref[...], v_ref[...])

    spec = pl.BlockSpec((br, C), lambda i: (i, 0))
    return pl.pallas_call(
        body,
        grid=(R // br,),
        in_specs=[spec] * 4,
        out_specs=[spec] * 3,
        out_shape=[jax.ShapeDtypeStruct((R, C), F32)] * 3,
        compiler_params=_params(),
        name=name,
    )(w, g, m, v)


def _place():
    return lax.axis_index("x"), lax.axis_index("y"), lax.axis_index("c")


def _other_chips(x, y):
    return [(1 - x, y), (x, 1 - y), (1 - x, 1 - y)]


class Exchange:
    def __init__(self, kind, arrays):
        self.kind, self.arrays, self.n = kind, arrays, len(arrays)
        self.slots = 4 if kind == "gather" else 8

    def out_shapes(self):
        if self.kind == "gather":
            return [jax.ShapeDtypeStruct((4,) + s.shape, s.dtype) for s in self.arrays]
        return [jax.ShapeDtypeStruct((8,) + s.shape[1:], s.dtype) for s in self.arrays]

    def sems(self):
        return [pltpu.SemaphoreType.DMA((self.n, 19)), pltpu.SemaphoreType.DMA((self.n, 19))]

    def phases(self, ins, outs, send_sems, recv_sems):
        n, scatter = self.n, self.kind == "scatter"
        x, y, c = _place()
        me, sib = (x, y, c), (x, y, 1 - c)
        mine = 2 * x + y
        chips = _other_chips(x, y)
        own_pair = 18

        def region(a, slot, half):
            h = outs[a].shape[1] // 2
            return outs[a].at[slot, pl.ds(half * h, h)]

        def copy(a, k, slot, half, to, src=None):
            return pltpu.make_async_remote_copy(
                src_ref=region(a, slot, half) if src is None else src, dst_ref=region(a, slot, half),
                send_sem=send_sems.at[a, k], recv_sem=recv_sems.at[a, k], device_id=to, device_id_type=MESH)

        def over_ici(t, to_core, from_core):
            return 4 * t + 2 * to_core + from_core

        def passed_on(t, from_core):
            return 12 + 2 * t + from_core

        senders = [(t, cc) for t in range(3) for cc in ((0, 1) if scatter else (c,))]

        def slot_of(t, cc):
            cx, cy = chips[t]
            return 2 * (2 * cx + cy) + cc if scatter else 2 * cx + cy

        def first_copies():
            cps = []
            for a in range(n):
                h = outs[a].shape[1] // 2
                for t, (cx, cy) in enumerate(chips):
                    if scatter:
                        for half in (0, 1):
                            cps.append(copy(a, over_ici(t, half, c), 2 * mine + c, half, (cx, cy, half),
                                            src=ins[a].at[2 * cx + cy, pl.ds(half * h, h)]))
                    else:
                        cps.append(copy(a, over_ici(t, c, c), mine, c, (cx, cy, c), src=ins[a].at[pl.ds(c * h, h)]))
                if scatter:
                    cps.append(pltpu.make_async_remote_copy(
                        src_ref=ins[a].at[mine], dst_ref=outs[a].at[2 * mine + c], send_sem=send_sems.at[a, own_pair],
                        recv_sem=recv_sems.at[a, own_pair], device_id=sib, device_id_type=MESH))
            return cps

        def forward_copies():
            return [copy(a, passed_on(t, cc), slot_of(t, cc), c, sib) for a in range(n) for t, cc in senders]

        def start():
            for cp in first_copies():
                cp.start()

        def forward():
            fws = iter(forward_copies())
            for a in range(n):
                for t, cc in senders:
                    copy(a, over_ici(t, c, cc), slot_of(t, cc), c, me).wait_recv()
                    next(fws).start()

        def finish():
            for a in range(n):
                for t, cc in senders:
                    from_core = cc if scatter else 1 - c
                    copy(a, passed_on(t, from_core), slot_of(t, from_core), 1 - c, me).wait_recv()
                if scatter:
                    pltpu.make_async_remote_copy(
                        src_ref=ins[a].at[mine], dst_ref=outs[a].at[2 * mine + 1 - c], send_sem=send_sems.at[a, own_pair],
                        recv_sem=recv_sems.at[a, own_pair], device_id=me, device_id_type=MESH).wait_recv()
            for cp in first_copies() + forward_copies():
                cp.wait_send()

        return start, forward, finish

    def fill_own(self, outs):
        x, y, c = _place()
        if self.kind == "gather":
            return [lax.dynamic_update_index_in_dim(o, s, 2 * x + y, 0) for o, s in zip(outs, self.arrays)]
        return [lax.dynamic_update_index_in_dim(o, lax.dynamic_index_in_dim(s, 2 * x + y, 0, keepdims=False), 2 * (2 * x + y) + c, 0)
                for o, s in zip(outs, self.arrays)]

    def run(self, name):
        n = self.n

        def body(*refs):
            start, forward, finish = self.phases(refs[:n], refs[n:2 * n], *refs[2 * n:])
            start()
            forward()
            finish()

        outs = pl.pallas_call(
            body, in_specs=[ANY] * n, out_specs=[ANY] * n, out_shape=self.out_shapes(), scratch_shapes=self.sems(), name=name,
        )(*self.arrays)
        return self.fill_own(outs)


def _host_exchange(exchange, refs, n_in, n_out, step, n_steps):
    if exchange is None:
        return refs
    n = exchange.n
    own_in, ex_in = refs[:n_in], refs[n_in:n_in + n]
    own_out, ex_out = refs[n_in + n:n_in + n + n_out], refs[n_in + n + n_out:n_in + 2 * n + n_out]
    rest = refs[n_in + 2 * n + n_out:]
    start, forward, finish = exchange.phases(ex_in, ex_out, rest[-2], rest[-1])
    pl.when(step == 0)(start)
    pl.when(step == (2 * n_steps) // 3)(forward)
    pl.when(step == n_steps - 1)(finish)
    return own_in + own_out + rest[:-2]


def _hosted_call(exchange, body, *, grid, in_specs, out_specs, out_shape, scratch_shapes, compiler_params, name, args):
    if exchange is None:
        return pl.pallas_call(body, grid=grid, in_specs=in_specs, out_specs=out_specs, out_shape=out_shape,
                              scratch_shapes=scratch_shapes, compiler_params=compiler_params, name=name)(*args)
    n = exchange.n
    res = pl.pallas_call(
        body, grid=grid, in_specs=list(in_specs) + [ANY] * n, out_specs=list(out_specs) + [ANY] * n,
        out_shape=list(out_shape) + exchange.out_shapes(), scratch_shapes=list(scratch_shapes) + exchange.sems(),
        compiler_params=compiler_params, name=name)(*args, *exchange.arrays)
    return list(res[:len(out_shape)]) + [exchange.fill_own(res[len(out_shape):])]


def sum_slots(parts, name):
    S, R, C = parts.shape
    br = 128 if R % 128 == 0 else R

    def body(p_ref, o_ref):
        acc = p_ref[0].astype(F32)
        for s in range(1, S):
            acc = acc + p_ref[s].astype(F32)
        o_ref[...] = acc

    return pl.pallas_call(
        body,
        grid=(R // br,),
        in_specs=[pl.BlockSpec((S, br, C), lambda i: (0, i, 0))],
        out_specs=pl.BlockSpec((br, C), lambda i: (i, 0)),
        out_shape=jax.ShapeDtypeStruct((R, C), F32),
        compiler_params=_params(),
        name=name,
    )(parts)


def _adamw_math(w, g, m, v):
    m_new = ADAM_B1 * m + (1.0 - ADAM_B1) * g
    v_new = ADAM_B2 * v + (1.0 - ADAM_B2) * (g * g)
    m_hat = m_new / (1.0 - ADAM_B1 ** ADAM_STEP)
    v_hat = v_new / (1.0 - ADAM_B2 ** ADAM_STEP)
    return -ADAM_LR * (m_hat / (jnp.sqrt(v_hat) + ADAM_EPS) + ADAM_WD * w), m_new, v_new


def reduce_adamw(slots, w, m, v, name):
    S, R, C = slots.shape
    br = 128

    def body(p_ref, w_ref, m_ref, v_ref, g_ref, d_ref, nm_ref, nv_ref):
        g = p_ref[0].astype(F32)
        for s in range(1, S):
            g = g + p_ref[s].astype(F32)
        g_ref[...] = g
        d_ref[...], nm_ref[...], nv_ref[...] = _adamw_math(w_ref[...], g, m_ref[...], v_ref[...])

    spec = pl.BlockSpec((br, C), lambda i: (i, 0))
    return pl.pallas_call(
        body,
        grid=(R // br,),
        in_specs=[pl.BlockSpec((S, br, C), lambda i: (0, i, 0)), spec, spec, spec],
        out_specs=[spec] * 4,
        out_shape=[jax.ShapeDtypeStruct((R, C), F32)] * 4,
        compiler_params=_params(),
        name=name,
    )(slots, w, m, v)


SMALL_ROWS = 72


def gather_small(vec):
    def body(v_ref, o_ref, send_sems, recv_sems, local_sem):
        x, y, c = _place()
        flips = [(fx, fy, fc) for fx in (0, 1) for fy in (0, 1) for fc in (0, 1)][1:]

        def peer(f):
            return (1 - x if f[0] else x, 1 - y if f[1] else y, 1 - c if f[2] else c)

        slot = lambda p: 4 * p[0] + 2 * p[1] + p[2]
        own = pltpu.make_async_copy(v_ref, o_ref.at[slot((x, y, c))], local_sem)
        own.start()
        cps = [pltpu.make_async_remote_copy(
            src_ref=v_ref, dst_ref=o_ref.at[slot((x, y, c))], send_sem=send_sems.at[k], recv_sem=recv_sems.at[k],
            device_id=peer(f), device_id_type=MESH) for k, f in enumerate(flips)]
        for cp in cps:
            cp.start()
        for k, f in enumerate(flips):
            pltpu.make_async_remote_copy(
                src_ref=v_ref, dst_ref=o_ref.at[slot(peer(f))], send_sem=send_sems.at[k], recv_sem=recv_sems.at[k],
                device_id=(x, y, c), device_id_type=MESH).wait_recv()
        for cp in cps:
            cp.wait_send()
        own.wait()

    return pl.pallas_call(
        body,
        in_specs=[ANY],
        out_specs=ANY,
        out_shape=jax.ShapeDtypeStruct((8,) + vec.shape, vec.dtype),
        scratch_shapes=[pltpu.SemaphoreType.DMA((7,)), pltpu.SemaphoreType.DMA((7,)), pltpu.SemaphoreType.DMA],
        name="gather_small",
    )(vec)


GLOW_PAD = LANE - GLA_RANK


def kernel(x, attn_norm_g, w_in, gla_gate_w2, gla_gate_b, gla_norm_g, rel_bias, w_out, mlp_norm_g, w_ff1, w_ff2, final_norm_g, loss_target, m_attn_norm_g, m_w_in, m_gla_gate_w2, m_gla_gate_b, m_gla_norm_g, m_rel_bias, m_w_out, m_mlp_norm_g, m_w_ff1, m_w_ff2, m_final_norm_g, v_attn_norm_g, v_w_in, v_gla_gate_w2, v_gla_gate_b, v_gla_norm_g, v_rel_bias, v_w_out, v_mlp_norm_g, v_w_ff1, v_w_ff2, v_final_norm_g):
    xs, tgt = x[0], loss_target[0]
    T = xs.shape[0]
    cx, cy, _ = _place()
    chip = 2 * cx + cy
    gf = final_norm_g.reshape(1, D_MODEL)

    win_g, w2_g = Exchange("gather", [w_in[0].astype(BF16), gla_gate_w2[0]]).run("gather_w_in")
    win = jnp.transpose(win_g, (1, 0, 2)).reshape(D_MODEL, D_IN)
    n_glow = R_GLOW + GLA_RANK
    wp = jnp.concatenate([win[:, :n_glow], jnp.zeros((D_MODEL, GLOW_PAD), BF16), win[:, n_glow:]], axis=1)
    w2 = jnp.transpose(w2_g, (1, 0, 2)).reshape(GLA_RANK, GLA_QK)
    w2p = jnp.concatenate([w2, jnp.zeros((GLOW_PAD, GLA_QK), F32)], axis=0)

    proj, nx = inproj(xs, attn_norm_g, wp)
    tab = bias_tables(rel_bias)
    ob, lse, (wout_g, wff1, wff2_g) = dsa_fwd(
        proj, tab, Exchange("gather", [w_out[0].astype(BF16), w_ff1[0].astype(BF16), w_ff2[0].astype(BF16)]))
    wout = wout_g.reshape(D_MODEL, D_MODEL)
    wff2 = wff2_g.reshape(D_FF, D_MODEL)
    oa, opre, sprev = gla_fwd(proj, w2p, gla_gate_b, gla_norm_g)
    mixed, nm, act, dpre, dh2, dh1, dmixed, loss, dgf, dg2 = post_fused(xs, oa, ob, tgt, mlp_norm_g, gf, wout, wff1, wff2)

    late = [
        wgrad(mixed, dh1, "wgrad_out").reshape(4, D_MODEL // 4, D_MODEL),
        wgrad(nm, dpre, "wgrad_ff1", bn=FF_BLOCK, col_blocked=True),
        wgrad(act, dh2, "wgrad_ff2", bm=FF_BLOCK).reshape(4, FF_BLOCK, D_MODEL),
    ]
    da, dw2p, dbg, dgn = gla_bwd(proj, w2p, gla_gate_b, gla_norm_g, opre, sprev, dmixed)
    dq, dk, dv, dtab, late_slots = dsa_bwd(proj, tab, ob, lse, dmixed, Exchange("scatter", late))
    slots = dict(zip(["w_out", "w_ff1", "w_ff2"], late_slots))
    drel = bias_tables_bwd(dtab)

    dwa = wgrad(nx, da, "wgrad_in_gla")
    dwq = wgrad(nx, dq, "wgrad_in_q")
    dwk = wgrad(nx, dk, "wgrad_in_k")
    dwv = wgrad(nx, dv, "wgrad_in_v")
    dwin = jnp.concatenate([dwa[:, :n_glow], dwq, dwk, dwv], axis=1)
    dwin = [jnp.transpose(dwin.reshape(D_MODEL, 4, D_IN // 4), (1, 0, 2))]
    dxs, dg1, (slots["w_in"],) = dx_final(xs, dh1, attn_norm_g, da, dq, dk, dv, wp, Exchange("scatter", dwin))

    sizes = [D_MODEL, GLA_QK, GLA_WIDTH, REL_BUCKETS * DSA_HEADS, D_MODEL, D_MODEL, GLA_RANK * GLA_QK, 1]
    small = jnp.concatenate([dg1.reshape(-1), dbg.reshape(-1), dgn.reshape(-1), drel.reshape(-1), dg2.reshape(-1),
                             dgf.reshape(-1), dw2p[:GLA_RANK].reshape(-1), loss.reshape(-1),
                             jnp.zeros((SMALL_ROWS * LANE - sum(sizes),), F32)]).reshape(SMALL_ROWS, LANE)
    tot = sum_slots(gather_small(small), "sum_small").reshape(-1)
    offs = np.concatenate([[0], np.cumsum(sizes)])
    piece = lambda i: tot[int(offs[i]):int(offs[i + 1])]
    g_g1 = piece(0).reshape(1, D_MODEL)
    g_bg = piece(1).reshape(1, GLA_QK)
    g_gn = piece(2).reshape(1, GLA_WIDTH)
    g_rel = piece(3).reshape(REL_BUCKETS, DSA_HEADS)
    g_g2 = piece(4).reshape(1, D_MODEL)
    g_gf = piece(5).reshape(1, D_MODEL)
    g_w2 = lax.dynamic_slice_in_dim(piece(6).reshape(GLA_RANK, GLA_QK), chip * (GLA_QK // 4), GLA_QK // 4, axis=1)

    loss_all = piece(7)[0]

    upd = [
        ("attn_norm_g", attn_norm_g, g_g1, m_attn_norm_g, v_attn_norm_g),
        ("w_in", w_in[0], None, m_w_in[0], v_w_in[0]),
        ("gla_gate_w2", gla_gate_w2[0], g_w2, m_gla_gate_w2[0], v_gla_gate_w2[0]),
        ("gla_gate_b", gla_gate_b, g_bg, m_gla_gate_b, v_gla_gate_b),
        ("gla_norm_g", gla_norm_g, g_gn, m_gla_norm_g, v_gla_norm_g),
        ("rel_bias", rel_bias, g_rel, m_rel_bias, v_rel_bias),
        ("w_out", w_out[0], None, m_w_out[0], v_w_out[0]),
        ("mlp_norm_g", mlp_norm_g, g_g2, m_mlp_norm_g, v_mlp_norm_g),
        ("w_ff1", w_ff1[0], None, m_w_ff1[0], v_w_ff1[0]),
        ("w_ff2", w_ff2[0], None, m_w_ff2[0], v_w_ff2[0]),
        ("final_norm_g", gf, g_gf, m_final_norm_g.reshape(1, D_MODEL), v_final_norm_g.reshape(1, D_MODEL)),
    ]
    shapes = [attn_norm_g.shape, w_in.shape, gla_gate_w2.shape, gla_gate_b.shape, gla_norm_g.shape, rel_bias.shape,
              w_out.shape, mlp_norm_g.shape, w_ff1.shape, w_ff2.shape, final_norm_g.shape]
    grads, deltas, new_m, new_v = [], [], [], []
    for (name, w, g, m, v), shape in zip(upd, shapes):
        if name in slots:
            g, d, nm_, nv_ = reduce_adamw(slots[name], w, m, v, "reduce_adamw_" + name)
        else:
            d, nm_, nv_ = adamw(w, g, m, v, "adamw_" + name)
        grads.append(g.reshape(shape))
        deltas.append(d.reshape(shape))
        new_m.append(nm_.reshape(shape))
        new_v.append(nv_.reshape(shape))
    return (loss_all, dxs.reshape(1, T, D_MODEL), *grads, *deltas, *new_m, *new_v)
```

```python
import functools
import math

import jax
import jax.numpy as jnp
import numpy as np
from jax import lax
from jax.experimental import pallas as pl
from jax.experimental.pallas import tpu as pltpu

F32 = jnp.float32
BF16 = jnp.bfloat16
MESH = pl.DeviceIdType.MESH

D_MODEL = 1024
GLA_WIDTH = 512
GLA_HEADS = 4
GLA_DK = 64
GLA_DV = 128
GLA_QK = GLA_HEADS * GLA_DK
GLA_RANK = 16
GLA_TAU = 16.0
GLA_CHUNK = 64
DSA_WIDTH = 512
DSA_HEADS = 8
DSA_DH = 64
DSA_DILATIONS = (1, 4, 16)
DSA_SPAN = 128
DSA_BLOCK = 128
DSA_SUPER = DSA_BLOCK * DSA_DILATIONS[-1]
REL_BUCKETS = 32
REL_MAX_DIST = 2048
D_FF = 4096
D_IN = 3088
EPS = 1e-6
NEG = -1e30
QK_SCALE = 0.125

ADAM_LR = 0.001
ADAM_B1 = 0.9
ADAM_B2 = 0.999
ADAM_EPS = 1e-08
ADAM_WD = 0.01
ADAM_STEP = 10

LANE = 128
P_GQ, P_GK, P_GV, P_GR = 0, 256, 512, 1024
P_GLOW = 1536
P_A = 1664
P_DQ, P_DK, P_DV = 1664, 2176, 2688
P_ALL = 3200
R_GLOW = 1536

VMEM_LIMIT = 56 * 1024 * 1024


def _params(sem=("arbitrary",), vmem=VMEM_LIMIT):
    return pltpu.CompilerParams(dimension_semantics=sem, vmem_limit_bytes=vmem)


def _dot(a, b):
    return jnp.dot(a, b, preferred_element_type=F32)


def _dot_nt(a, b):
    return lax.dot_general(a, b, (((1,), (1,)), ((), ())), preferred_element_type=F32)


def _dot_tn(a, b):
    return lax.dot_general(a, b, (((0,), (0,)), ((), ())), preferred_element_type=F32)


def _split3(x):
    x1 = x.astype(BF16)
    r1 = x - x1.astype(F32)
    x2 = r1.astype(BF16)
    x3 = (r1 - x2.astype(F32)).astype(BF16)
    return x1, x2, x3


def _dot_exact_lhs(m_bf16, x):
    x1, x2, x3 = _split3(x)
    return _dot(m_bf16, x1) + _dot(m_bf16, x2) + _dot(m_bf16, x3)


def _rstd(xf):
    return lax.rsqrt(jnp.mean(xf * xf, axis=-1, keepdims=True) + EPS)


def _load_once(hbm_ref, vmem_ref, sem):
    cp = pltpu.make_async_copy(hbm_ref, vmem_ref, sem)
    cp.start()
    cp.wait()


ANY = pl.BlockSpec(memory_space=pl.ANY)


def inproj(x, g1, wp):
    T = x.shape[0]
    tm = 256

    def body(x_ref, g_ref, w_hbm, proj_ref, nx_ref, w_vmem, sem):
        @pl.when(pl.program_id(0) == 0)
        def _():
            _load_once(w_hbm, w_vmem, sem)

        xf = x_ref[...]
        nx = ((xf * _rstd(xf)) * g_ref[...]).astype(BF16)
        nx_ref[...] = nx
        proj_ref[...] = _dot(nx, w_vmem[...])

    return pl.pallas_call(
        body,
        grid=(T // tm,),
        in_specs=[pl.BlockSpec((tm, D_MODEL), lambda i: (i, 0)), pl.BlockSpec((1, D_MODEL), lambda i: (0, 0)), ANY],
        out_specs=[pl.BlockSpec((tm, P_ALL), lambda i: (i, 0)), pl.BlockSpec((tm, D_MODEL), lambda i: (i, 0))],
        out_shape=[jax.ShapeDtypeStruct((T, P_ALL), F32), jax.ShapeDtypeStruct((T, D_MODEL), BF16)],
        scratch_shapes=[pltpu.VMEM((D_MODEL, P_ALL), BF16), pltpu.SemaphoreType.DMA],
        compiler_params=_params(),
        name="inproj",
    )(x, g1, wp)


GLA_CHUNKS_PER_STEP = 16
GLA_ROWS = GLA_CHUNK * GLA_CHUNKS_PER_STEP


def _gla_masks():
    lane = lax.broadcasted_iota(jnp.int32, (1, GLA_QK), 1)
    return [(lane >= h * GLA_DK) & (lane < (h + 1) * GLA_DK) for h in range(GLA_HEADS)]


def _log_sigmoid(x):
    return jnp.minimum(x, 0.0) - jnp.log(1.0 + jnp.exp(-jnp.abs(x)))


def _sigmoid(x):
    return 1.0 / (1.0 + jnp.exp(-x))


def _head_cols(h):
    return slice(h * GLA_DV, (h + 1) * GLA_DV)


GLA_GROUP = 256


def _gla_step_constants():
    ri = lax.broadcasted_iota(jnp.int32, (GLA_GROUP, GLA_GROUP), 0)
    ci = lax.broadcasted_iota(jnp.int32, (GLA_GROUP, GLA_GROUP), 1)
    shift = GLA_CHUNK.bit_length() - 1
    same = lax.shift_right_logical(ri, shift) == lax.shift_right_logical(ci, shift)
    return same & (ri >= ci), same & (ri <= ci), _gla_masks()


def _by_group(fn, *arrays):
    outs = [fn(*[a[g * GLA_GROUP:(g + 1) * GLA_GROUP] for a in arrays]) for g in range(GLA_ROWS // GLA_GROUP)]
    if isinstance(outs[0], tuple):
        return tuple(jnp.concatenate(parts, axis=0) for parts in zip(*outs))
    return jnp.concatenate(outs, axis=0)


def _per_chunk(x):
    return x.reshape(GLA_CHUNKS_PER_STEP, GLA_CHUNK, x.shape[-1])


def _chunk_rows_of(x, c):
    return x[c * GLA_CHUNK:(c + 1) * GLA_CHUNK]


def _stack_masked(x, masks):
    return jnp.concatenate([jnp.where(m, x, 0.0) for m in masks], axis=0)


def _stack_head_cols(x):
    return jnp.concatenate([x[:, _head_cols(h)] for h in range(GLA_HEADS)], axis=0)


def _diag_blocks(full, masks):
    out = jnp.where(masks[0], full[:GLA_DV], 0.0)
    for h in range(1, GLA_HEADS):
        out = out + jnp.where(masks[h], full[h * GLA_DV:(h + 1) * GLA_DV], 0.0)
    return out


def _row_blocks_masked(full, masks):
    out = jnp.where(masks[0], full[:GLA_CHUNK], 0.0)
    for h in range(1, GLA_HEADS):
        out = out + jnp.where(masks[h], full[h * GLA_CHUNK:(h + 1) * GLA_CHUNK], 0.0)
    return out


def _gla_step_common(q, k, glow_b, w2, bg, tri):
    gpre = _dot(glow_b, w2) + bg
    glog = _log_sigmoid(gpre) / GLA_TAU
    b = _by_group(lambda g: _dot_exact_lhs(tri, g), glog)
    bl = jnp.sum(_per_chunk(glog), axis=1, keepdims=True)
    eb = jnp.exp(b)
    enb = jnp.exp(-b)
    eke = jnp.exp(jnp.broadcast_to(bl, (GLA_CHUNKS_PER_STEP, GLA_CHUNK, GLA_QK)).reshape(GLA_ROWS, GLA_QK) - b)
    return gpre, eb, enb, eke, jnp.exp(bl), (q * QK_SCALE) * eb, k * enb, k * eke


def gla_fwd(proj, w2p, bg, gn):
    T = proj.shape[0]
    n_steps = T // GLA_ROWS
    n_chunks = T // GLA_CHUNK

    def body(proj_ref, w2_ref, bg_ref, gn_ref, oa_ref, opre_ref, sprev_ref, st_ref):
        @pl.when(pl.program_id(0) == 0)
        def _():
            st_ref[...] = jnp.zeros_like(st_ref)

        causal, _, masks = _gla_step_constants()
        q = proj_ref[:, P_GQ:P_GQ + GLA_QK]
        k = proj_ref[:, P_GK:P_GK + GLA_QK]
        v = proj_ref[:, P_GV:P_GV + GLA_WIDTH]
        r = proj_ref[:, P_GR:P_GR + GLA_WIDTH]
        glow = proj_ref[:, P_GLOW:P_GLOW + LANE].astype(BF16)
        _, _, _, _, ebl, qd, ki, ke = _gla_step_common(q, k, glow, w2_ref[...].astype(BF16), bg_ref[...], causal.astype(BF16))
        ki_b = ki.astype(BF16)
        v_b = v.astype(BF16)
        o_heads = []
        for h in range(GLA_HEADS):
            def intra(qd_g, ki_g, v_g):
                att = jnp.where(causal, _dot_nt(qd_g, ki_g), 0.0)
                return _dot(att.astype(BF16), v_g)

            o_heads.append(_by_group(intra, jnp.where(masks[h], qd, 0.0).astype(BF16), ki_b, v_b[:, _head_cols(h)]))
        st = st_ref[...]
        states = []
        for c in range(GLA_CHUNKS_PER_STEP):
            states.append(st)
            sprev_ref[c] = st
            inc = _diag_blocks(_dot_tn(_chunk_rows_of(v_b, c), _chunk_rows_of(ke, c).astype(BF16)), masks)
            st = st * ebl[c] + inc
        st_ref[...] = st
        inter = []
        for c in range(GLA_CHUNKS_PER_STEP):
            qd_c = _stack_masked(_chunk_rows_of(qd, c), masks).astype(BF16)
            got = _dot_nt(qd_c, states[c].astype(BF16))
            inter.append(jnp.concatenate([got[h * GLA_CHUNK:(h + 1) * GLA_CHUNK] for h in range(GLA_HEADS)], axis=1))
        o = jnp.concatenate(o_heads, axis=1) + jnp.concatenate(inter, axis=0)
        opre_ref[...] = o
        on = jnp.concatenate([o[:, _head_cols(h)] * _rstd(o[:, _head_cols(h)]) for h in range(GLA_HEADS)], axis=1)
        oa_ref[...] = ((on * gn_ref[...]) * (r * _sigmoid(r))).astype(BF16)

    return pl.pallas_call(
        body,
        grid=(n_steps,),
        in_specs=[
            pl.BlockSpec((GLA_ROWS, P_A), lambda i: (i, 0)),
            pl.BlockSpec((LANE, GLA_QK), lambda i: (0, 0)),
            pl.BlockSpec((1, GLA_QK), lambda i: (0, 0)),
            pl.BlockSpec((1, GLA_WIDTH), lambda i: (0, 0)),
        ],
        out_specs=[
            pl.BlockSpec((GLA_ROWS, GLA_WIDTH), lambda i: (i, 0)),
            pl.BlockSpec((GLA_ROWS, GLA_WIDTH), lambda i: (i, 0)),
            pl.BlockSpec((GLA_CHUNKS_PER_STEP, GLA_DV, GLA_QK), lambda i: (i, 0, 0)),
        ],
        out_shape=[
            jax.ShapeDtypeStruct((T, GLA_WIDTH), BF16),
            jax.ShapeDtypeStruct((T, GLA_WIDTH), F32),
            jax.ShapeDtypeStruct((n_chunks, GLA_DV, GLA_QK), F32),
        ],
        scratch_shapes=[pltpu.VMEM((GLA_DV, GLA_QK), F32)],
        compiler_params=_params(),
        name="gla_fwd",
    )(proj, w2p, bg, gn)


def gla_bwd(proj, w2p, bg, gn, opre, sprev, dmixed, exchange=None):
    T = proj.shape[0]
    n_steps = T // GLA_ROWS

    def body(*refs):
        refs = _host_exchange(exchange, refs, 7, 4, pl.program_id(0), n_steps)
        proj_ref, w2_ref, bg_ref, gn_ref, opre_ref, sprev_ref, doa_ref, da_ref, dw2_ref, dbg_ref, dgn_ref, dst_ref = refs

        @pl.when(pl.program_id(0) == 0)
        def _():
            dst_ref[...] = jnp.zeros_like(dst_ref)
            dw2_ref[...] = jnp.zeros_like(dw2_ref)
            dbg_ref[...] = jnp.zeros_like(dbg_ref)
            dgn_ref[...] = jnp.zeros_like(dgn_ref)

        causal, causal_t, masks = _gla_step_constants()
        w2 = w2_ref[...].astype(BF16)
        gn = gn_ref[...]
        q = proj_ref[:, P_GQ:P_GQ + GLA_QK]
        k = proj_ref[:, P_GK:P_GK + GLA_QK]
        v_b = proj_ref[:, P_GV:P_GV + GLA_WIDTH].astype(BF16)
        r = proj_ref[:, P_GR:P_GR + GLA_WIDTH]
        glow = proj_ref[:, P_GLOW:P_GLOW + LANE].astype(BF16)
        o = opre_ref[...]
        doa = doa_ref[...]
        gpre, eb, enb, eke, ebl, qd, ki, ke = _gla_step_common(q, k, glow, w2, bg_ref[...], causal.astype(BF16))
        sig = _sigmoid(r)
        rs = jnp.concatenate([jnp.broadcast_to(_rstd(o[:, _head_cols(h)]), (GLA_ROWS, GLA_DV)) for h in range(GLA_HEADS)], axis=1)
        on = o * rs
        d_ong = doa * (r * sig)
        dr = doa * (on * gn) * (sig * (1.0 + r * (1.0 - sig)))
        dgn_ref[...] += jnp.sum(d_ong * on, axis=0, keepdims=True)
        d_on = d_ong * gn
        t = d_on * on
        mean_t = jnp.concatenate([jnp.broadcast_to(jnp.mean(t[:, _head_cols(h)], axis=-1, keepdims=True), (GLA_ROWS, GLA_DV))
                                  for h in range(GLA_HEADS)], axis=1)
        do_b = (rs * (d_on - on * mean_t)).astype(BF16)
        ki_b = ki.astype(BF16)
        ke_b = ke.astype(BF16)
        dqd = jnp.zeros_like(qd)
        dki = jnp.zeros_like(qd)
        dv_heads = []
        for h in range(GLA_HEADS):
            qd_h = jnp.where(masks[h], qd, 0.0).astype(BF16)
            do_h = do_b[:, _head_cols(h)]

            def intra(qd_g, ki_g, v_g, do_g):
                att = jnp.where(causal, _dot_nt(qd_g, ki_g), 0.0).astype(BF16)
                d_att = jnp.where(causal, _dot_nt(do_g, v_g), 0.0).astype(BF16)
                return _dot_tn(att, do_g), _dot(d_att, ki_g), _dot_tn(d_att, qd_g)

            dv_h, dqd_h, dki_h = _by_group(intra, qd_h, ki_b, v_b[:, _head_cols(h)], do_h)
            dv_heads.append(dv_h)
            dqd = dqd + jnp.where(masks[h], dqd_h, 0.0)
            dki = dki + dki_h
        states = [sprev_ref[c] for c in range(GLA_CHUNKS_PER_STEP)]
        dqd_inter, dst_adds = [], []
        for c in range(GLA_CHUNKS_PER_STEP):
            do_c = _stack_head_cols(_chunk_rows_of(do_b, c))
            dqd_inter.append(_row_blocks_masked(_dot(do_c, states[c].astype(BF16)), masks))
            dst_adds.append(_diag_blocks(_dot_tn(_chunk_rows_of(do_b, c), _chunk_rows_of(qd, c).astype(BF16)), masks))
        dst = dst_ref[...]
        dsts, debls = [None] * GLA_CHUNKS_PER_STEP, [None] * GLA_CHUNKS_PER_STEP
        for c in reversed(range(GLA_CHUNKS_PER_STEP)):
            dsts[c] = dst
            debls[c] = jnp.sum(dst * states[c], axis=0, keepdims=True)
            dst = dst * ebl[c] + dst_adds[c]
        dst_ref[...] = dst
        dv_inter, dke = [], []
        for c in range(GLA_CHUNKS_PER_STEP):
            dst_b = dsts[c].astype(BF16)
            got = _dot_nt(_stack_masked(_chunk_rows_of(ke, c), masks).astype(BF16), dst_b)
            dv_inter.append(jnp.concatenate([got[h * GLA_CHUNK:(h + 1) * GLA_CHUNK] for h in range(GLA_HEADS)], axis=1))
            dke.append(_row_blocks_masked(_dot(_stack_head_cols(_chunk_rows_of(v_b, c)), dst_b), masks))
        dqd = dqd + jnp.concatenate(dqd_inter, axis=0)
        dke = jnp.concatenate(dke, axis=0)
        dv = jnp.concatenate(dv_heads, axis=1) + jnp.concatenate(dv_inter, axis=0)
        dkk = dke * ke
        dbl = jnp.sum(_per_chunk(dkk), axis=1, keepdims=True) + jnp.stack(debls) * ebl
        last_row = lax.broadcasted_iota(jnp.int32, (GLA_CHUNKS_PER_STEP, GLA_CHUNK, GLA_QK), 1) == GLA_CHUNK - 1
        db = dqd * qd - dki * ki - dkk + jnp.where(last_row, dbl, 0.0).reshape(GLA_ROWS, GLA_QK)
        tri_t = causal_t.astype(BF16)
        dglog = _by_group(lambda g: _dot_exact_lhs(tri_t, g), db)
        dgpre = (dglog / GLA_TAU) * _sigmoid(-gpre)
        dgpre_b = dgpre.astype(BF16)
        da_ref[...] = jnp.concatenate(
            [dqd * eb * QK_SCALE, dki * enb + dke * eke, dv, dr, _dot_nt(dgpre_b, w2)], axis=1).astype(BF16)
        dw2_ref[...] += _dot_tn(glow, dgpre_b)
        dbg_ref[...] += jnp.sum(dgpre, axis=0, keepdims=True)

    rev = lambda i: (n_steps - 1 - i, 0)
    return _hosted_call(
        exchange,
        body,
        grid=(n_steps,),
        in_specs=[
            pl.BlockSpec((GLA_ROWS, P_A), rev),
            pl.BlockSpec((LANE, GLA_QK), lambda i: (0, 0)),
            pl.BlockSpec((1, GLA_QK), lambda i: (0, 0)),
            pl.BlockSpec((1, GLA_WIDTH), lambda i: (0, 0)),
            pl.BlockSpec((GLA_ROWS, GLA_WIDTH), rev),
            pl.BlockSpec((GLA_CHUNKS_PER_STEP, GLA_DV, GLA_QK), lambda i: (n_steps - 1 - i, 0, 0)),
            pl.BlockSpec((GLA_ROWS, GLA_WIDTH), rev),
        ],
        out_specs=[
            pl.BlockSpec((GLA_ROWS, P_A), rev),
            pl.BlockSpec((LANE, GLA_QK), lambda i: (0, 0)),
            pl.BlockSpec((1, GLA_QK), lambda i: (0, 0)),
            pl.BlockSpec((1, GLA_WIDTH), lambda i: (0, 0)),
        ],
        out_shape=[
            jax.ShapeDtypeStruct((T, P_A), BF16),
            jax.ShapeDtypeStruct((LANE, GLA_QK), F32),
            jax.ShapeDtypeStruct((1, GLA_QK), F32),
            jax.ShapeDtypeStruct((1, GLA_WIDTH), F32),
        ],
        scratch_shapes=[pltpu.VMEM((GLA_DV, GLA_QK), F32)],
        compiler_params=_params(),
        name="gla_bwd",
        args=(proj, w2p, bg, gn, opre, sprev, dmixed),
    )


def _t5_bucket(dist):
    max_exact = REL_BUCKETS // 2
    n = np.maximum(dist, 0)
    large = max_exact + (np.log(np.maximum(n, 1) / max_exact) / math.log(REL_MAX_DIST / max_exact)
                         * (REL_BUCKETS - max_exact)).astype(np.int32)
    large = np.minimum(large, REL_BUCKETS - 1)
    return np.where(n < max_exact, n, large).astype(np.int32)


SUBLANES = 8


def _bucket_rows():
    steps = DSA_BLOCK - np.arange(2 * DSA_BLOCK)
    in_band = (steps >= 0) & (steps <= DSA_SPAN)
    rows = np.stack([np.where(in_band, _t5_bucket(steps * d), -1) for d in DSA_DILATIONS]).astype(np.int32)
    return np.broadcast_to(rows[:, None, :], (len(DSA_DILATIONS), SUBLANES, 2 * DSA_BLOCK)).copy()


def bias_tables(rel_bias):
    ids = jnp.asarray(_bucket_rows())
    nd = len(DSA_DILATIONS)

    def body(rel_ref, ids_ref, tab_ref):
        h = pl.program_id(1)
        idt = ids_ref[0]
        row = jnp.where(idt < 0, NEG, 0.0).astype(F32)
        for b in range(REL_BUCKETS):
            row = jnp.where(idt == b, rel_ref[b, h], row)
        full = jnp.broadcast_to(row[0:1], (DSA_BLOCK, 2 * DSA_BLOCK))
        tab_ref[0, 0] = pltpu.roll(full, 0, 1, stride=1, stride_axis=0)

    return pl.pallas_call(
        body,
        grid=(nd, DSA_HEADS),
        in_specs=[pl.BlockSpec(memory_space=pltpu.SMEM), pl.BlockSpec((1, SUBLANES, 2 * DSA_BLOCK), lambda d, h: (d, 0, 0))],
        out_specs=pl.BlockSpec((1, 1, DSA_BLOCK, 2 * DSA_BLOCK), lambda d, h: (d, h, 0, 0)),
        out_shape=jax.ShapeDtypeStruct((nd, DSA_HEADS, DSA_BLOCK, 2 * DSA_BLOCK), F32),
        compiler_params=_params(("arbitrary", "arbitrary")),
        name="bias_tables",
    )(rel_bias, ids)


def _bucket_ids():
    L = DSA_BLOCK
    steps = L + np.arange(L)[:, None] - np.arange(2 * L)[None, :]
    in_band = (steps >= 0) & (steps <= DSA_SPAN)
    return np.stack([np.where(in_band, _t5_bucket(steps * d), -1) for d in DSA_DILATIONS]).astype(np.int32)


def bias_tables_bwd(dtab):
    ids = jnp.asarray(_bucket_ids())
    nd = len(DSA_DILATIONS)

    def body(dtab_ref, ids_ref, drel_ref):
        @pl.when((pl.program_id(0) == 0) & (pl.program_id(1) == 0))
        def _():
            for b in range(REL_BUCKETS):
                for h in range(DSA_HEADS):
                    drel_ref[b, h] = 0.0

        h = pl.program_id(1)
        idt = ids_ref[0]
        g = dtab_ref[0, 0]
        for b in range(REL_BUCKETS):
            drel_ref[b, h] += jnp.sum(jnp.where(idt == b, g, 0.0))

    return pl.pallas_call(
        body,
        grid=(nd, DSA_HEADS),
        in_specs=[pl.BlockSpec((1, 1, DSA_BLOCK, 2 * DSA_BLOCK), lambda d, h: (d, h, 0, 0)),
                  pl.BlockSpec((1, DSA_BLOCK, 2 * DSA_BLOCK), lambda d, h: (d, 0, 0))],
        out_specs=pl.BlockSpec(memory_space=pltpu.SMEM),
        out_shape=jax.ShapeDtypeStruct((REL_BUCKETS, DSA_HEADS), F32),
        compiler_params=_params(("arbitrary", "arbitrary")),
        name="bias_tables_bwd",
    )(dtab, ids)


DSA_PAIRS = DSA_HEADS // 2
DSA_UNROLL = 16
DSA_COMBINE_ROWS = 256


def _dsa_units(d):
    return d, DSA_SUPER // (DSA_BLOCK * d)


def _dsa_specs(T):
    nsb = T // DSA_SUPER
    qcol, kcol, vcol = P_DQ // LANE, P_DK // LANE, P_DV // LANE
    return nsb, qcol, kcol, vcol


def _head_lane_mask():
    return lax.broadcasted_iota(jnp.int32, (1, LANE), 1) < DSA_DH


def _fill_tile_variants(tab_ref, variants):
    col = lax.broadcasted_iota(jnp.int32, (2 * DSA_BLOCK, 2 * DSA_BLOCK), 1)
    for di in range(len(DSA_DILATIONS)):
        tile = tab_ref[di, 0]
        variants[di, 0] = tile
        variants[di, 1] = jnp.where(col < DSA_BLOCK, NEG, tile)


def _tile_variants_scratch():
    return pltpu.VMEM((len(DSA_DILATIONS), 2, 2 * DSA_BLOCK, 2 * DSA_BLOCK), F32)


def _pair_tiles(tab):
    return tab.reshape(len(DSA_DILATIONS), DSA_PAIRS, 2 * DSA_BLOCK, 2 * DSA_BLOCK)


def _stack_heads(t, head0):
    return jnp.concatenate([jnp.where(head0, t, 0.0), jnp.where(head0, 0.0, t)], axis=0)


def dsa_fwd(proj, tab, exchange=None):
    T = proj.shape[0]
    nsb, qcol, kcol, vcol = _dsa_specs(T)
    S = DSA_SUPER

    def body(*refs):
        refs = _host_exchange(exchange, refs, 6, 2, pl.program_id(0) * nsb + pl.program_id(1), DSA_PAIRS * nsb)
        q_ref, kp_ref, kc_ref, vp_ref, vc_ref, tab_ref, out_ref, lse_ref, kk, vv, ob, lb, tiles = refs
        sb = pl.program_id(1)
        kk[0:S, :] = kp_ref[...]
        kk[S:2 * S, :] = kc_ref[...]
        vv[0:S, :] = vp_ref[...]
        vv[S:2 * S, :] = vc_ref[...]
        head0 = _head_lane_mask()
        pl.when(sb == 0)(functools.partial(_fill_tile_variants, tab_ref, tiles))

        for di, d in enumerate(DSA_DILATIONS):
            n_res, n_blk = _dsa_units(d)

            def unit(u, carry, di=di, d=d, n_blk=n_blk):
                r = u // n_blk
                c = u % n_blk
                q0 = r + d * DSA_BLOCK * c
                qrows = pl.ds(q0, DSA_BLOCK, stride=d) if d > 1 else pl.ds(q0, DSA_BLOCK)
                krows = pl.ds(S + q0 - d * DSA_BLOCK, 2 * DSA_BLOCK, stride=d) if d > 1 else pl.ds(S + q0 - DSA_BLOCK, 2 * DSA_BLOCK)
                q2 = q_ref[qrows, :] * QK_SCALE
                k2 = kk[krows, :].astype(BF16)
                v2 = vv[krows, :].astype(BF16)
                qs = _stack_heads(q2, head0).astype(BF16)
                s = _dot_nt(qs, k2) + tiles[di, ((sb == 0) & (c == 0)).astype(jnp.int32)]
                m = jnp.max(s, axis=-1, keepdims=True)
                p = jnp.exp(s - m)
                den = jnp.sum(p, axis=-1, keepdims=True)
                o = _dot(p.astype(BF16), v2) / den
                l = jnp.broadcast_to(m + jnp.log(den), (2 * DSA_BLOCK, LANE))
                ob[di, qrows, :] = jnp.where(head0, o[:DSA_BLOCK], o[DSA_BLOCK:])
                lb[di, qrows, :] = jnp.where(head0, l[:DSA_BLOCK], l[DSA_BLOCK:])
                return carry

            lax.fori_loop(0, n_res * n_blk, unit, 0, unroll=DSA_UNROLL)

        def combine(i, carry):
            rows = pl.ds(pl.multiple_of(i * DSA_COMBINE_ROWS, DSA_COMBINE_ROWS), DSA_COMBINE_ROWS)
            l0, l1, l2 = lb[0, rows, :], lb[1, rows, :], lb[2, rows, :]
            mx = jnp.maximum(jnp.maximum(l0, l1), l2)
            e0, e1, e2 = jnp.exp(l0 - mx), jnp.exp(l1 - mx), jnp.exp(l2 - mx)
            den = e0 + e1 + e2
            out_ref[rows, :] = (e0 * ob[0, rows, :] + e1 * ob[1, rows, :] + e2 * ob[2, rows, :]) / den
            lse_ref[rows, :] = mx + jnp.log(den)
            return carry

        lax.fori_loop(0, S // DSA_COMBINE_ROWS, combine, 0)

    prev = lambda col: (lambda hp, sb: (jnp.maximum(sb - 1, 0), col + hp))
    cur = lambda col: (lambda hp, sb: (sb, col + hp))
    blk = lambda f: pl.BlockSpec((S, LANE), f)
    return _hosted_call(
        exchange,
        body,
        grid=(DSA_PAIRS, nsb),
        in_specs=[blk(cur(qcol)), blk(prev(kcol)), blk(cur(kcol)), blk(prev(vcol)), blk(cur(vcol)),
                  pl.BlockSpec((len(DSA_DILATIONS), 1, 2 * DSA_BLOCK, 2 * DSA_BLOCK), lambda hp, sb: (0, hp, 0, 0))],
        out_specs=[blk(lambda hp, sb: (sb, hp)), blk(lambda hp, sb: (sb, hp))],
        out_shape=[jax.ShapeDtypeStruct((T, DSA_WIDTH), F32), jax.ShapeDtypeStruct((T, DSA_WIDTH), F32)],
        scratch_shapes=[pltpu.VMEM((2 * S, LANE), F32), pltpu.VMEM((2 * S, LANE), F32),
                        pltpu.VMEM((len(DSA_DILATIONS), S, LANE), F32), pltpu.VMEM((len(DSA_DILATIONS), S, LANE), F32),
                        _tile_variants_scratch()],
        compiler_params=_params(("arbitrary", "arbitrary")),
        name="dsa_fwd",
        args=(proj, proj, proj, proj, proj, _pair_tiles(tab)),
    )


def dsa_bwd(proj, tab, ob_out, lse, dmixed, exchange=None):
    T = proj.shape[0]
    nsb, qcol, kcol, vcol = _dsa_specs(T)
    S = DSA_SUPER
    nd = len(DSA_DILATIONS)
    ocol = GLA_WIDTH // LANE

    def body(*refs):
        refs = _host_exchange(exchange, refs, 9, 4, pl.program_id(0) * nsb + pl.program_id(1), DSA_PAIRS * nsb)
        (q_ref, kp_ref, kc_ref, vp_ref, vc_ref, tab_ref, o_ref, lse_ref, do_ref,
         dq_ref, dk_ref, dv_ref, dtab_ref, kk, vv, dqa, dkk, dvv, tiles) = refs
        j = pl.program_id(1)
        sb = nsb - 1 - j
        kk[0:S, :] = kp_ref[...]
        kk[S:2 * S, :] = kc_ref[...]
        vv[0:S, :] = vp_ref[...]
        vv[S:2 * S, :] = vc_ref[...]
        head0 = _head_lane_mask()
        pl.when(j == 0)(functools.partial(_fill_tile_variants, tab_ref, tiles))

        @pl.when(j == 0)
        def _():
            dtab_ref[...] = jnp.zeros_like(dtab_ref)
            dkk[S:2 * S, :] = jnp.zeros((S, LANE), F32)
            dvv[S:2 * S, :] = jnp.zeros((S, LANE), F32)

        @pl.when(j > 0)
        def _():
            dkk[S:2 * S, :] = dkk[0:S, :]
            dvv[S:2 * S, :] = dvv[0:S, :]

        dkk[0:S, :] = jnp.zeros((S, LANE), F32)
        dvv[0:S, :] = jnp.zeros((S, LANE), F32)
        dqa[...] = jnp.zeros_like(dqa)

        for di, d in enumerate(DSA_DILATIONS):
            n_res, n_blk = _dsa_units(d)

            def unit(u, carry, di=di, d=d, n_blk=n_blk):
                r = u // n_blk
                c = u % n_blk
                q0 = r + d * DSA_BLOCK * c
                qrows = pl.ds(q0, DSA_BLOCK, stride=d) if d > 1 else pl.ds(q0, DSA_BLOCK)
                krows = pl.ds(S + q0 - d * DSA_BLOCK, 2 * DSA_BLOCK, stride=d) if d > 1 else pl.ds(S + q0 - DSA_BLOCK, 2 * DSA_BLOCK)
                q2 = q_ref[qrows, :] * QK_SCALE
                k2 = kk[krows, :].astype(BF16)
                v2 = vv[krows, :].astype(BF16)
                do2 = do_ref[qrows, :]
                o2 = o_ref[qrows, :]
                l2 = lse_ref[qrows, :]
                qs = _stack_heads(q2, head0).astype(BF16)
                dos = _stack_heads(do2, head0)
                dos_b = dos.astype(BF16)
                delta = jnp.sum(dos * jnp.concatenate([o2, o2], axis=0), axis=-1, keepdims=True)
                lse = jnp.concatenate([jnp.max(jnp.where(head0, l2, -jnp.inf), axis=-1, keepdims=True),
                                       jnp.max(jnp.where(head0, -jnp.inf, l2), axis=-1, keepdims=True)], axis=0)
                s = _dot_nt(qs, k2) + tiles[di, ((sb == 0) & (c == 0)).astype(jnp.int32)]
                p = jnp.exp(s - lse)
                ds = p * (_dot_nt(dos_b, v2) - delta)
                dtab_ref[di, 0] += ds
                ds_b = ds.astype(BF16)
                dq = _dot(ds_b, k2)
                dqa[qrows, :] += jnp.where(head0, dq[:DSA_BLOCK], dq[DSA_BLOCK:]) * QK_SCALE
                dkk[krows, :] += _dot_tn(ds_b, qs)
                dvv[krows, :] += _dot_tn(p.astype(BF16), dos_b)
                return carry

            lax.fori_loop(0, n_res * n_blk, unit, 0, unroll=DSA_UNROLL)

        dq_ref[...] = dqa[...].astype(BF16)
        dk_ref[...] = dkk[S:2 * S, :].astype(BF16)
        dv_ref[...] = dvv[S:2 * S, :].astype(BF16)

    prev = lambda col: (lambda hp, j: (jnp.maximum(nsb - 2 - j, 0), col + hp))
    cur = lambda col: (lambda hp, j: (nsb - 1 - j, col + hp))
    blk = lambda f: pl.BlockSpec((S, LANE), f)
    out_blk = blk(lambda hp, j: (nsb - 1 - j, hp))
    tab_blk = pl.BlockSpec((nd, 1, 2 * DSA_BLOCK, 2 * DSA_BLOCK), lambda hp, j: (0, hp, 0, 0))
    dq, dk, dv, dtab, *carried = _hosted_call(
        exchange,
        body,
        grid=(DSA_PAIRS, nsb),
        in_specs=[blk(cur(qcol)), blk(prev(kcol)), blk(cur(kcol)), blk(prev(vcol)), blk(cur(vcol)), tab_blk,
                  out_blk, out_blk, blk(cur(ocol))],
        out_specs=[out_blk, out_blk, out_blk, tab_blk],
        out_shape=[jax.ShapeDtypeStruct((T, DSA_WIDTH), BF16)] * 3
        + [jax.ShapeDtypeStruct((nd, DSA_PAIRS, 2 * DSA_BLOCK, 2 * DSA_BLOCK), F32)],
        scratch_shapes=[pltpu.VMEM((2 * S, LANE), F32), pltpu.VMEM((2 * S, LANE), F32), pltpu.VMEM((S, LANE), F32),
                        pltpu.VMEM((2 * S, LANE), F32), pltpu.VMEM((2 * S, LANE), F32), _tile_variants_scratch()],
        compiler_params=_params(("arbitrary", "arbitrary")),
        name="dsa_bwd",
        args=(proj, proj, proj, proj, proj, _pair_tiles(tab), ob_out, lse, dmixed),
    )
    return (dq, dk, dv, dtab.reshape(nd, DSA_HEADS, DSA_BLOCK, 2 * DSA_BLOCK), *carried)


FF_BLOCKS = 4
FF_BLOCK = D_FF // FF_BLOCKS


def post_fused(x, oa, ob, tgt, g2, gf, wout, wff1, wff2):
    T = x.shape[0]
    tm = 256
    inv_d = 1.0 / D_MODEL

    def body(x_ref, oa_ref, ob_ref, tgt_ref, g2_ref, gf_ref, wout_hbm, wff1_hbm, wff2_hbm,
             mixed_ref, nm_ref, a_ref, dpre_ref, dh2_ref, dh1_ref, dmixed_ref, loss_ref, dgf_ref, dg2_ref,
             wout_v, wff1_v, wff2_v, sems):
        @pl.when(pl.program_id(0) == 0)
        def _():
            cps = [pltpu.make_async_copy(s, d, sems.at[i])
                   for i, (s, d) in enumerate([(wout_hbm, wout_v), (wff1_hbm, wff1_v), (wff2_hbm, wff2_v)])]
            for cp in cps:
                cp.start()
            for cp in cps:
                cp.wait()
            loss_ref[...] = jnp.zeros_like(loss_ref)
            dgf_ref[...] = jnp.zeros_like(dgf_ref)
            dg2_ref[...] = jnp.zeros_like(dg2_ref)

        mixed = jnp.concatenate([oa_ref[...], ob_ref[...].astype(BF16)], axis=1)
        mixed_ref[...] = mixed
        h1 = x_ref[...] + _dot(mixed, wout_v[...])
        rs1 = _rstd(h1)
        hn1 = h1 * rs1
        g2 = g2_ref[...]
        nm = (hn1 * g2).astype(BF16)
        nm_ref[...] = nm
        relu = []
        mlp = jnp.zeros((tm, D_MODEL), F32)
        for j in range(FF_BLOCKS):
            cols = slice(j * FF_BLOCK, (j + 1) * FF_BLOCK)
            r_j = jnp.maximum(_dot(nm, wff1_v[j]), 0.0)
            a_j = (r_j * r_j).astype(BF16)
            a_ref[:, cols] = a_j
            relu.append(r_j)
            mlp = mlp + _dot(a_j, wff2_v[cols, :])
        h2 = h1 + mlp
        rsf = _rstd(h2)
        hnf = h2 * rsf
        gf = gf_ref[...]
        diff = hnf * gf - tgt_ref[...]
        loss_ref[...] += 0.5 * jnp.sum(jnp.sum(diff * diff, axis=-1, keepdims=True) * inv_d, axis=0, keepdims=True)
        dy = diff * inv_d
        dgf_ref[...] += jnp.sum(dy * hnf, axis=0, keepdims=True)
        dhnf = dy * gf
        dh2 = rsf * (dhnf - hnf * jnp.mean(dhnf * hnf, axis=-1, keepdims=True))
        dh2_b = dh2.astype(BF16)
        dh2_ref[...] = dh2_b
        dnm = jnp.zeros((tm, D_MODEL), F32)
        for j in range(FF_BLOCKS):
            cols = slice(j * FF_BLOCK, (j + 1) * FF_BLOCK)
            dpre_j = (_dot_nt(dh2_b, wff2_v[cols, :]) * (2.0 * relu[j])).astype(BF16)
            dpre_ref[:, cols] = dpre_j
            dnm = dnm + _dot_nt(dpre_j, wff1_v[j])
        dg2_ref[...] += jnp.sum(dnm * hn1, axis=0, keepdims=True)
        dhn1 = dnm * g2
        dh1 = dh2 + rs1 * (dhn1 - hn1 * jnp.mean(dhn1 * hn1, axis=-1, keepdims=True))
        dh1_ref[...] = dh1
        dmixed_ref[...] = _dot_nt(dh1.astype(BF16), wout_v[...])

    row = lambda w: pl.BlockSpec((tm, w), lambda i: (i, 0))
    vec = lambda w: pl.BlockSpec((1, w), lambda i: (0, 0))
    return pl.pallas_call(
        body,
        grid=(T // tm,),
        in_specs=[row(D_MODEL), row(GLA_WIDTH), row(DSA_WIDTH), row(D_MODEL), vec(D_MODEL), vec(D_MODEL), ANY, ANY, ANY],
        out_specs=[row(D_MODEL), row(D_MODEL), row(D_FF), row(D_FF), row(D_MODEL), row(D_MODEL), row(D_MODEL),
                   vec(1), vec(D_MODEL), vec(D_MODEL)],
        out_shape=[
            jax.ShapeDtypeStruct((T, D_MODEL), BF16),
            jax.ShapeDtypeStruct((T, D_MODEL), BF16),
            jax.ShapeDtypeStruct((T, D_FF), BF16),
            jax.ShapeDtypeStruct((T, D_FF), BF16),
            jax.ShapeDtypeStruct((T, D_MODEL), BF16),
            jax.ShapeDtypeStruct((T, D_MODEL), F32),
            jax.ShapeDtypeStruct((T, D_MODEL), F32),
            jax.ShapeDtypeStruct((1, 1), F32),
            jax.ShapeDtypeStruct((1, D_MODEL), F32),
            jax.ShapeDtypeStruct((1, D_MODEL), F32),
        ],
        scratch_shapes=[pltpu.VMEM((D_MODEL, D_MODEL), BF16), pltpu.VMEM((FF_BLOCKS, D_MODEL, FF_BLOCK), BF16),
                        pltpu.VMEM((D_FF, D_MODEL), BF16), pltpu.SemaphoreType.DMA((3,))],
        compiler_params=_params(),
        name="post_fused",
    )(x, oa, ob, tgt, g2, gf, wout, wff1, wff2)


WGRAD_TOKENS = 2048


def wgrad(a, b, name, bm=None, bn=None, col_blocked=False):
    T, M = a.shape
    N = b.shape[1]
    bm = M if bm is None else bm
    bn = N if bn is None else bn
    tk = min(WGRAD_TOKENS, T)
    n_k = T // tk

    def body(a_ref, b_ref, o_ref, acc_ref):
        part = _dot_tn(a_ref[...].astype(BF16), b_ref[...].astype(BF16))
        out = o_ref.at[0] if col_blocked else o_ref
        k = pl.program_id(2)
        if n_k == 1:
            out[...] = part.astype(BF16)
            return

        @pl.when(k == 0)
        def _():
            acc_ref[...] = part

        @pl.when((k > 0) & (k < n_k - 1))
        def _():
            acc_ref[...] += part

        @pl.when(k == n_k - 1)
        def _():
            out[...] = (acc_ref[...] + part).astype(BF16)

    if col_blocked:
        assert bm == M
        out_spec = pl.BlockSpec((1, M, bn), lambda i, j, k: (j, 0, 0))
        out_shape = jax.ShapeDtypeStruct((N // bn, M, bn), BF16)
    else:
        out_spec = pl.BlockSpec((bm, bn), lambda i, j, k: (i, j))
        out_shape = jax.ShapeDtypeStruct((M, N), BF16)
    return pl.pallas_call(
        body,
        grid=(M // bm, N // bn, n_k),
        in_specs=[pl.BlockSpec((tk, bm), lambda i, j, k: (k, i)), pl.BlockSpec((tk, bn), lambda i, j, k: (k, j))],
        out_specs=out_spec,
        out_shape=out_shape,
        scratch_shapes=[pltpu.VMEM((bm, bn), F32)],
        compiler_params=_params(("arbitrary", "arbitrary", "arbitrary")),
        name=name,
    )(a, b)


def dx_final(x, dh1, g1, da, dq, dk, dv, wp, exchange=None):
    T = x.shape[0]
    tm = 256

    def body(*refs):
        refs = _host_exchange(exchange, refs, 8, 2, pl.program_id(0), T // tm)
        x_ref, dh1_ref, g_ref, da_ref, dq_ref, dk_ref, dv_ref, w_hbm, dx_ref, dg_ref, w_vmem, sem = refs

        @pl.when(pl.program_id(0) == 0)
        def _():
            _load_once(w_hbm, w_vmem, sem)
            dg_ref[...] = jnp.zeros_like(dg_ref)

        dnx = (_dot_nt(da_ref[...], w_vmem[:, 0:P_A]) + _dot_nt(dq_ref[...], w_vmem[:, P_DQ:P_DQ + DSA_WIDTH])
               + _dot_nt(dk_ref[...], w_vmem[:, P_DK:P_DK + DSA_WIDTH]) + _dot_nt(dv_ref[...], w_vmem[:, P_DV:P_DV + DSA_WIDTH]))
        xf = x_ref[...]
        rs = _rstd(xf)
        hn = xf * rs
        dg_ref[...] += jnp.sum(dnx * hn, axis=0, keepdims=True)
        dhn = dnx * g_ref[...]
        dx_ref[...] = dh1_ref[...] + rs * (dhn - hn * jnp.mean(dhn * hn, axis=-1, keepdims=True))

    row = lambda w: pl.BlockSpec((tm, w), lambda i: (i, 0))
    vec = pl.BlockSpec((1, D_MODEL), lambda i: (0, 0))
    return _hosted_call(
        exchange,
        body,
        grid=(T // tm,),
        in_specs=[row(D_MODEL), row(D_MODEL), vec, row(P_A), row(DSA_WIDTH), row(DSA_WIDTH), row(DSA_WIDTH), ANY],
        out_specs=[row(D_MODEL), vec],
        out_shape=[jax.ShapeDtypeStruct((T, D_MODEL), F32), jax.ShapeDtypeStruct((1, D_MODEL), F32)],
        scratch_shapes=[pltpu.VMEM((D_MODEL, P_ALL), BF16), pltpu.SemaphoreType.DMA],
        compiler_params=_params(),
        name="dx_final",
        args=(x, dh1, g1, da, dq, dk, dv, wp),
    )


def adamw(w, g, m, v, name):
    R, C = w.shape
    br = 256 if R % 256 == 0 else R

    def body(w_ref, g_ref, m_ref, v_ref, d_ref, nm_ref, nv_ref):
        d_ref[...], nm_ref[...], nv_ref[...] = _adamw_math(w_ref[...], g_ref[...], m_ref[...], v_ref[...])

    spec = pl.BlockSpec((br, C), lambda i: (i, 0))
    return pl.pallas_call(
        body,
        grid=(R // br,),
        in_specs=[spec] * 4,
        out_specs=[spec] * 3,
        out_shape=[jax.ShapeDtypeStruct((R, C), F32)] * 3,
        compiler_params=_params(),
        name=name,
    )(w, g, m, v)


def _place():
    return lax.axis_index("x"), lax.axis_index("y"), lax.axis_index("c")


def _other_chips(x, y):
    return [(1 - x, y), (x, 1 - y), (1 - x, 1 - y)]


class Exchange:
    def __init__(self, kind, arrays):
        self.kind, self.arrays, self.n = kind, arrays, len(arrays)
        self.slots = 4 if kind == "gather" else 8

    def out_shapes(self):
        if self.kind == "gather":
            return [jax.ShapeDtypeStruct((4,) + s.shape, s.dtype) for s in self.arrays]
        return [jax.ShapeDtypeStruct((8,) + s.shape[1:], s.dtype) for s in self.arrays]

    def sems(self):
        return [pltpu.SemaphoreType.DMA((self.n, 19)), pltpu.SemaphoreType.DMA((self.n, 19))]

    def phases(self, ins, outs, send_sems, recv_sems):
        n, scatter = self.n, self.kind == "scatter"
        x, y, c = _place()
        me, sib = (x, y, c), (x, y, 1 - c)
        mine = 2 * x + y
        chips = _other_chips(x, y)
        own_pair = 18

        def region(a, slot, half):
            h = outs[a].shape[1] // 2
            return outs[a].at[slot, pl.ds(half * h, h)]

        def copy(a, k, slot, half, to, src=None):
            return pltpu.make_async_remote_copy(
                src_ref=region(a, slot, half) if src is None else src, dst_ref=region(a, slot, half),
                send_sem=send_sems.at[a, k], recv_sem=recv_sems.at[a, k], device_id=to, device_id_type=MESH)

        def over_ici(t, to_core, from_core):
            return 4 * t + 2 * to_core + from_core

        def passed_on(t, from_core):
            return 12 + 2 * t + from_core

        senders = [(t, cc) for t in range(3) for cc in ((0, 1) if scatter else (c,))]

        def slot_of(t, cc):
            cx, cy = chips[t]
            return 2 * (2 * cx + cy) + cc if scatter else 2 * cx + cy

        def first_copies():
            cps = []
            for a in range(n):
                h = outs[a].shape[1] // 2
                for t, (cx, cy) in enumerate(chips):
                    if scatter:
                        for half in (0, 1):
                            cps.append(copy(a, over_ici(t, half, c), 2 * mine + c, half, (cx, cy, half),
                                            src=ins[a].at[2 * cx + cy, pl.ds(half * h, h)]))
                    else:
                        cps.append(copy(a, over_ici(t, c, c), mine, c, (cx, cy, c), src=ins[a].at[pl.ds(c * h, h)]))
                if scatter:
                    cps.append(pltpu.make_async_remote_copy(
                        src_ref=ins[a].at[mine], dst_ref=outs[a].at[2 * mine + c], send_sem=send_sems.at[a, own_pair],
                        recv_sem=recv_sems.at[a, own_pair], device_id=sib, device_id_type=MESH))
            return cps

        def forward_copies():
            return [copy(a, passed_on(t, cc), slot_of(t, cc), c, sib) for a in range(n) for t, cc in senders]

        def start():
            for cp in first_copies():
                cp.start()

        def forward():
            fws = iter(forward_copies())
            for a in range(n):
                for t, cc in senders:
                    copy(a, over_ici(t, c, cc), slot_of(t, cc), c, me).wait_recv()
                    next(fws).start()

        def finish():
            for a in range(n):
                for t, cc in senders:
                    from_core = cc if scatter else 1 - c
                    copy(a, passed_on(t, from_core), slot_of(t, from_core), 1 - c, me).wait_recv()
                if scatter:
                    pltpu.make_async_remote_copy(
                        src_ref=ins[a].at[mine], dst_ref=outs[a].at[2 * mine + 1 - c], send_sem=send_sems.at[a, own_pair],
                        recv_sem=recv_sems.at[a, own_pair], device_id=me, device_id_type=MESH).wait_recv()
            for cp in first_copies() + forward_copies():
                cp.wait_send()

        return start, forward, finish

    def fill_own(self, outs):
        x, y, c = _place()
        if self.kind == "gather":
            return [lax.dynamic_update_index_in_dim(o, s, 2 * x + y, 0) for o, s in zip(outs, self.arrays)]
        return [lax.dynamic_update_index_in_dim(o, lax.dynamic_index_in_dim(s, 2 * x + y, 0, keepdims=False), 2 * (2 * x + y) + c, 0)
                for o, s in zip(outs, self.arrays)]

    def run(self, name):
        n = self.n

        def body(*refs):
            start, forward, finish = self.phases(refs[:n], refs[n:2 * n], *refs[2 * n:])
            start()
            forward()
            finish()

        outs = pl.pallas_call(
            body, in_specs=[ANY] * n, out_specs=[ANY] * n, out_shape=self.out_shapes(), scratch_shapes=self.sems(), name=name,
        )(*self.arrays)
        return self.fill_own(outs)


def _host_exchange(exchange, refs, n_in, n_out, step, n_steps):
    if exchange is None:
        return refs
    n = exchange.n
    own_in, ex_in = refs[:n_in], refs[n_in:n_in + n]
    own_out, ex_out = refs[n_in + n:n_in + n + n_out], refs[n_in + n + n_out:n_in + 2 * n + n_out]
    rest = refs[n_in + 2 * n + n_out:]
    start, forward, finish = exchange.phases(ex_in, ex_out, rest[-2], rest[-1])
    pl.when(step == 0)(start)
    pl.when(step == (2 * n_steps) // 3)(forward)
    pl.when(step == n_steps - 1)(finish)
    return own_in + own_out + rest[:-2]


def _hosted_call(exchange, body, *, grid, in_specs, out_specs, out_shape, scratch_shapes, compiler_params, name, args):
    if exchange is None:
        return pl.pallas_call(body, grid=grid, in_specs=in_specs, out_specs=out_specs, out_shape=out_shape,
                              scratch_shapes=scratch_shapes, compiler_params=compiler_params, name=name)(*args)
    n = exchange.n
    res = pl.pallas_call(
        body, grid=grid, in_specs=list(in_specs) + [ANY] * n, out_specs=list(out_specs) + [ANY] * n,
        out_shape=list(out_shape) + exchange.out_shapes(), scratch_shapes=list(scratch_shapes) + exchange.sems(),
        compiler_params=compiler_params, name=name)(*args, *exchange.arrays)
    return list(res[:len(out_shape)]) + [exchange.fill_own(res[len(out_shape):])]


def sum_slots(parts, name):
    S, R, C = parts.shape
    br = 128 if R % 128 == 0 else R

    def body(p_ref, o_ref):
        acc = p_ref[0].astype(F32)
        for s in range(1, S):
            acc = acc + p_ref[s].astype(F32)
        o_ref[...] = acc

    return pl.pallas_call(
        body,
        grid=(R // br,),
        in_specs=[pl.BlockSpec((S, br, C), lambda i: (0, i, 0))],
        out_specs=pl.BlockSpec((br, C), lambda i: (i, 0)),
        out_shape=jax.ShapeDtypeStruct((R, C), F32),
        compiler_params=_params(),
        name=name,
    )(parts)


def _adamw_math(w, g, m, v):
    m_new = ADAM_B1 * m + (1.0 - ADAM_B1) * g
    v_new = ADAM_B2 * v + (1.0 - ADAM_B2) * (g * g)
    m_hat = m_new / (1.0 - ADAM_B1 ** ADAM_STEP)
    v_hat = v_new / (1.0 - ADAM_B2 ** ADAM_STEP)
    return -ADAM_LR * (m_hat / (jnp.sqrt(v_hat) + ADAM_EPS) + ADAM_WD * w), m_new, v_new


def reduce_adamw(slots, w, m, v, name):
    S, R, C = slots.shape
    br = 128

    def body(p_ref, w_ref, m_ref, v_ref, g_ref, d_ref, nm_ref, nv_ref):
        g = p_ref[0].astype(F32)
        for s in range(1, S):
            g = g + p_ref[s].astype(F32)
        g_ref[...] = g
        d_ref[...], nm_ref[...], nv_ref[...] = _adamw_math(w_ref[...], g, m_ref[...], v_ref[...])

    spec = pl.BlockSpec((br, C), lambda i: (i, 0))
    return pl.pallas_call(
        body,
        grid=(R // br,),
        in_specs=[pl.BlockSpec((S, br, C), lambda i: (0, i, 0)), spec, spec, spec],
        out_specs=[spec] * 4,
        out_shape=[jax.ShapeDtypeStruct((R, C), F32)] * 4,
        compiler_params=_params(),
        name=name,
    )(slots, w, m, v)


SMALL_ROWS = 72


def gather_small(vec):
    def body(v_ref, o_ref, send_sems, recv_sems, local_sem):
        x, y, c = _place()
        flips = [(fx, fy, fc) for fx in (0, 1) for fy in (0, 1) for fc in (0, 1)][1:]

        def peer(f):
            return (1 - x if f[0] else x, 1 - y if f[1] else y, 1 - c if f[2] else c)

        slot = lambda p: 4 * p[0] + 2 * p[1] + p[2]
        own = pltpu.make_async_copy(v_ref, o_ref.at[slot((x, y, c))], local_sem)
        own.start()
        cps = [pltpu.make_async_remote_copy(
            src_ref=v_ref, dst_ref=o_ref.at[slot((x, y, c))], send_sem=send_sems.at[k], recv_sem=recv_sems.at[k],
            device_id=peer(f), device_id_type=MESH) for k, f in enumerate(flips)]
        for cp in cps:
            cp.start()
        for k, f in enumerate(flips):
            pltpu.make_async_remote_copy(
                src_ref=v_ref, dst_ref=o_ref.at[slot(peer(f))], send_sem=send_sems.at[k], recv_sem=recv_sems.at[k],
                device_id=(x, y, c), device_id_type=MESH).wait_recv()
        for cp in cps:
            cp.wait_send()
        own.wait()

    return pl.pallas_call(
        body,
        in_specs=[ANY],
        out_specs=ANY,
        out_shape=jax.ShapeDtypeStruct((8,) + vec.shape, vec.dtype),
        scratch_shapes=[pltpu.SemaphoreType.DMA((7,)), pltpu.SemaphoreType.DMA((7,)), pltpu.SemaphoreType.DMA],
        name="gather_small",
    )(vec)


GLOW_PAD = LANE - GLA_RANK


def kernel(x, attn_norm_g, w_in, gla_gate_w2, gla_gate_b, gla_norm_g, rel_bias, w_out, mlp_norm_g, w_ff1, w_ff2, final_norm_g, loss_target, m_attn_norm_g, m_w_in, m_gla_gate_w2, m_gla_gate_b, m_gla_norm_g, m_rel_bias, m_w_out, m_mlp_norm_g, m_w_ff1, m_w_ff2, m_final_norm_g, v_attn_norm_g, v_w_in, v_gla_gate_w2, v_gla_gate_b, v_gla_norm_g, v_rel_bias, v_w_out, v_mlp_norm_g, v_w_ff1, v_w_ff2, v_final_norm_g):
    xs, tgt = x[0], loss_target[0]
    T = xs.shape[0]
    cx, cy, _ = _place()
    chip = 2 * cx + cy
    gf = final_norm_g.reshape(1, D_MODEL)

    win_g, w2_g = Exchange("gather", [w_in[0].astype(BF16), gla_gate_w2[0]]).run("gather_w_in")
    win = jnp.transpose(win_g, (1, 0, 2)).reshape(D_MODEL, D_IN)
    n_glow = R_GLOW + GLA_RANK
    wp = jnp.concatenate([win[:, :n_glow], jnp.zeros((D_MODEL, GLOW_PAD), BF16), win[:, n_glow:]], axis=1)
    w2 = jnp.transpose(w2_g, (1, 0, 2)).reshape(GLA_RANK, GLA_QK)
    w2p = jnp.concatenate([w2, jnp.zeros((GLOW_PAD, GLA_QK), F32)], axis=0)

    proj, nx = inproj(xs, attn_norm_g, wp)
    tab = bias_tables(rel_bias)
    ob, lse, (wout_g, wff1, wff2_g) = dsa_fwd(
        proj, tab, Exchange("gather", [w_out[0].astype(BF16), w_ff1[0].astype(BF16), w_ff2[0].astype(BF16)]))
    wout = wout_g.reshape(D_MODEL, D_MODEL)
    wff2 = wff2_g.reshape(D_FF, D_MODEL)
    oa, opre, sprev = gla_fwd(proj, w2p, gla_gate_b, gla_norm_g)
    mixed, nm, act, dpre, dh2, dh1, dmixed, loss, dgf, dg2 = post_fused(xs, oa, ob, tgt, mlp_norm_g, gf, wout, wff1, wff2)

    late = [
        wgrad(mixed, dh1, "wgrad_out").reshape(4, D_MODEL // 4, D_MODEL),
        wgrad(nm, dpre, "wgrad_ff1", bn=FF_BLOCK, col_blocked=True),
        wgrad(act, dh2, "wgrad_ff2", bm=FF_BLOCK).reshape(4, FF_BLOCK, D_MODEL),
    ]
    da, dw2p, dbg, dgn = gla_bwd(proj, w2p, gla_gate_b, gla_norm_g, opre, sprev, dmixed)
    dq, dk, dv, dtab, late_slots = dsa_bwd(proj, tab, ob, lse, dmixed, Exchange("scatter", late))
    slots = dict(zip(["w_out", "w_ff1", "w_ff2"], late_slots))
    drel = bias_tables_bwd(dtab)

    dwa = wgrad(nx, da, "wgrad_in_gla")
    dwq = wgrad(nx, dq, "wgrad_in_q")
    dwk = wgrad(nx, dk, "wgrad_in_k")
    dwv = wgrad(nx, dv, "wgrad_in_v")
    dwin = jnp.concatenate([dwa[:, :n_glow], dwq, dwk, dwv], axis=1)
    dwin = [jnp.transpose(dwin.reshape(D_MODEL, 4, D_IN // 4), (1, 0, 2))]
    dxs, dg1, (slots["w_in"],) = dx_final(xs, dh1, attn_norm_g, da, dq, dk, dv, wp, Exchange("scatter", dwin))

    sizes = [D_MODEL, GLA_QK, GLA_WIDTH, REL_BUCKETS * DSA_HEADS, D_MODEL, D_MODEL, GLA_RANK * GLA_QK, 1]
    small = jnp.concatenate([dg1.reshape(-1), dbg.reshape(-1), dgn.reshape(-1), drel.reshape(-1), dg2.reshape(-1),
                             dgf.reshape(-1), dw2p[:GLA_RANK].reshape(-1), loss.reshape(-1),
                             jnp.zeros((SMALL_ROWS * LANE - sum(sizes),), F32)]).reshape(SMALL_ROWS, LANE)
    tot = sum_slots(gather_small(small), "sum_small").reshape(-1)
    offs = np.concatenate([[0], np.cumsum(sizes)])
    piece = lambda i: tot[int(offs[i]):int(offs[i + 1])]
    g_g1 = piece(0).reshape(1, D_MODEL)
    g_bg = piece(1).reshape(1, GLA_QK)
    g_gn = piece(2).reshape(1, GLA_WIDTH)
    g_rel = piece(3).reshape(REL_BUCKETS, DSA_HEADS)
    g_g2 = piece(4).reshape(1, D_MODEL)
    g_gf = piece(5).reshape(1, D_MODEL)
    g_w2 = lax.dynamic_slice_in_dim(piece(6).reshape(GLA_RANK, GLA_QK), chip * (GLA_QK // 4), GLA_QK // 4, axis=1)

    loss_all = piece(7)[0]

    upd = [
        ("attn_norm_g", attn_norm_g, g_g1, m_attn_norm_g, v_attn_norm_g),
        ("w_in", w_in[0], None, m_w_in[0], v_w_in[0]),
        ("gla_gate_w2", gla_gate_w2[0], g_w2, m_gla_gate_w2[0], v_gla_gate_w2[0]),
        ("gla_gate_b", gla_gate_b, g_bg, m_gla_gate_b, v_gla_gate_b),
        ("gla_norm_g", gla_norm_g, g_gn, m_gla_norm_g, v_gla_norm_g),
        ("rel_bias", rel_bias, g_rel, m_rel_bias, v_rel_bias),
        ("w_out", w_out[0], None, m_w_out[0], v_w_out[0]),
        ("mlp_norm_g", mlp_norm_g, g_g2, m_mlp_norm_g, v_mlp_norm_g),
        ("w_ff1", w_ff1[0], None, m_w_ff1[0], v_w_ff1[0]),
        ("w_ff2", w_ff2[0], None, m_w_ff2[0], v_w_ff2[0]),
        ("final_norm_g", gf, g_gf, m_final_norm_g.reshape(1, D_MODEL), v_final_norm_g.reshape(1, D_MODEL)),
    ]
    shapes = [attn_norm_g.shape, w_in.shape, gla_gate_w2.shape, gla_gate_b.shape, gla_norm_g.shape, rel_bias.shape,
              w_out.shape, mlp_norm_g.shape, w_ff1.shape, w_ff2.shape, final_norm_g.shape]
    grads, deltas, new_m, new_v = [], [], [], []
    for (name, w, g, m, v), shape in zip(upd, shapes):
        if name in slots:
            g, d, nm_, nv_ = reduce_adamw(slots[name], w, m, v, "reduce_adamw_" + name)
        else:
            d, nm_, nv_ = adamw(w, g, m, v, "adamw_" + name)
        grads.append(g.reshape(shape))
        deltas.append(d.reshape(shape))
        new_m.append(nm_.reshape(shape))
        new_v.append(nv_.reshape(shape))
    return (loss_all, dxs.reshape(1, T, D_MODEL), *grads, *deltas, *new_m, *new_v)
```

```python
import functools
import math

import jax
import jax.numpy as jnp
import numpy as np
from jax import lax
from jax.experimental import pallas as pl
from jax.experimental.pallas import tpu as pltpu

F32 = jnp.float32
BF16 = jnp.bfloat16
MESH = pl.DeviceIdType.MESH

D_MODEL = 1024
GLA_WIDTH = 512
GLA_HEADS = 4
GLA_DK = 64
GLA_DV = 128
GLA_QK = GLA_HEADS * GLA_DK
GLA_RANK = 16
GLA_TAU = 16.0
GLA_CHUNK = 64
DSA_WIDTH = 512
DSA_HEADS = 8
DSA_DH = 64
DSA_DILATIONS = (1, 4, 16)
DSA_SPAN = 128
DSA_BLOCK = 128
DSA_SUPER = DSA_BLOCK * DSA_DILATIONS[-1]
REL_BUCKETS = 32
REL_MAX_DIST = 2048
D_FF = 4096
D_IN = 3088
EPS = 1e-6
NEG = -1e30
QK_SCALE = 0.125

ADAM_LR = 0.001
ADAM_B1 = 0.9
ADAM_B2 = 0.999
ADAM_EPS = 1e-08
ADAM_WD = 0.01
ADAM_STEP = 10

LANE = 128
P_GQ, P_GK, P_GV, P_GR = 0, 256, 512, 1024
P_GLOW = 1536
P_A = 1664
P_DQ, P_DK, P_DV = 1664, 2176, 2688
P_ALL = 3200
R_GLOW = 1536

VMEM_LIMIT = 56 * 1024 * 1024


def _params(sem=("arbitrary",), vmem=VMEM_LIMIT):
    return pltpu.CompilerParams(dimension_semantics=sem, vmem_limit_bytes=vmem)


def _dot(a, b):
    return jnp.dot(a, b, preferred_element_type=F32)


def _dot_nt(a, b):
    return lax.dot_general(a, b, (((1,), (1,)), ((), ())), preferred_element_type=F32)


def _dot_tn(a, b):
    return lax.dot_general(a, b, (((0,), (0,)), ((), ())), preferred_element_type=F32)


def _split3(x):
    x1 = x.astype(BF16)
    r1 = x - x1.astype(F32)
    x2 = r1.astype(BF16)
    x3 = (r1 - x2.astype(F32)).astype(BF16)
    return x1, x2, x3


def _dot_exact_lhs(m_bf16, x):
    x1, x2, x3 = _split3(x)
    return _dot(m_bf16, x1) + _dot(m_bf16, x2) + _dot(m_bf16, x3)


def _rstd(xf):
    return lax.rsqrt(jnp.mean(xf * xf, axis=-1, keepdims=True) + EPS)


def _load_once(hbm_ref, vmem_ref, sem):
    cp = pltpu.make_async_copy(hbm_ref, vmem_ref, sem)
    cp.start()
    cp.wait()


ANY = pl.BlockSpec(memory_space=pl.ANY)


def inproj(x, g1, wp):
    T = x.shape[0]
    tm = 256

    def body(x_ref, g_ref, w_hbm, proj_ref, nx_ref, w_vmem, sem):
        @pl.when(pl.program_id(0) == 0)
        def _():
            _load_once(w_hbm, w_vmem, sem)

        xf = x_ref[...]
        nx = ((xf * _rstd(xf)) * g_ref[...]).astype(BF16)
        nx_ref[...] = nx
        proj_ref[...] = _dot(nx, w_vmem[...])

    return pl.pallas_call(
        body,
        grid=(T // tm,),
        in_specs=[pl.BlockSpec((tm, D_MODEL), lambda i: (i, 0)), pl.BlockSpec((1, D_MODEL), lambda i: (0, 0)), ANY],
        out_specs=[pl.BlockSpec((tm, P_ALL), lambda i: (i, 0)), pl.BlockSpec((tm, D_MODEL), lambda i: (i, 0))],
        out_shape=[jax.ShapeDtypeStruct((T, P_ALL), F32), jax.ShapeDtypeStruct((T, D_MODEL), BF16)],
        scratch_shapes=[pltpu.VMEM((D_MODEL, P_ALL), BF16), pltpu.SemaphoreType.DMA],
        compiler_params=_params(),
        name="inproj",
    )(x, g1, wp)


GLA_CHUNKS_PER_STEP = 16
GLA_ROWS = GLA_CHUNK * GLA_CHUNKS_PER_STEP


def _gla_masks():
    lane = lax.broadcasted_iota(jnp.int32, (1, GLA_QK), 1)
    return [(lane >= h * GLA_DK) & (lane < (h + 1) * GLA_DK) for h in range(GLA_HEADS)]


def _log_sigmoid(x):
    return jnp.minimum(x, 0.0) - jnp.log(1.0 + jnp.exp(-jnp.abs(x)))


def _sigmoid(x):
    return 1.0 / (1.0 + jnp.exp(-x))


def _head_cols(h):
    return slice(h * GLA_DV, (h + 1) * GLA_DV)


GLA_GROUP = 256


def _gla_step_constants():
    ri = lax.broadcasted_iota(jnp.int32, (GLA_GROUP, GLA_GROUP), 0)
    ci = lax.broadcasted_iota(jnp.int32, (GLA_GROUP, GLA_GROUP), 1)
    shift = GLA_CHUNK.bit_length() - 1
    same = lax.shift_right_logical(ri, shift) == lax.shift_right_logical(ci, shift)
    return same & (ri >= ci), same & (ri <= ci), _gla_masks()


def _by_group(fn, *arrays):
    outs = [fn(*[a[g * GLA_GROUP:(g + 1) * GLA_GROUP] for a in arrays]) for g in range(GLA_ROWS // GLA_GROUP)]
    if isinstance(outs[0], tuple):
        return tuple(jnp.concatenate(parts, axis=0) for parts in zip(*outs))
    return jnp.concatenate(outs, axis=0)


def _per_chunk(x):
    return x.reshape(GLA_CHUNKS_PER_STEP, GLA_CHUNK, x.shape[-1])


def _chunk_rows_of(x, c):
    return x[c * GLA_CHUNK:(c + 1) * GLA_CHUNK]


def _stack_masked(x, masks):
    return jnp.concatenate([jnp.where(m, x, 0.0) for m in masks], axis=0)


def _stack_head_cols(x):
    return jnp.concatenate([x[:, _head_cols(h)] for h in range(GLA_HEADS)], axis=0)


def _diag_blocks(full, masks):
    out = jnp.where(masks[0], full[:GLA_DV], 0.0)
    for h in range(1, GLA_HEADS):
        out = out + jnp.where(masks[h], full[h * GLA_DV:(h + 1) * GLA_DV], 0.0)
    return out


def _row_blocks_masked(full, masks):
    out = jnp.where(masks[0], full[:GLA_CHUNK], 0.0)
    for h in range(1, GLA_HEADS):
        out = out + jnp.where(masks[h], full[h * GLA_CHUNK:(h + 1) * GLA_CHUNK], 0.0)
    return out


def _gla_step_common(q, k, glow_b, w2, bg, tri):
    gpre = _dot(glow_b, w2) + bg
    glog = _log_sigmoid(gpre) / GLA_TAU
    b = _by_group(lambda g: _dot_exact_lhs(tri, g), glog)
    bl = jnp.sum(_per_chunk(glog), axis=1, keepdims=True)
    eb = jnp.exp(b)
    enb = jnp.exp(-b)
    eke = jnp.exp(jnp.broadcast_to(bl, (GLA_CHUNKS_PER_STEP, GLA_CHUNK, GLA_QK)).reshape(GLA_ROWS, GLA_QK) - b)
    return gpre, eb, enb, eke, jnp.exp(bl), (q * QK_SCALE) * eb, k * enb, k * eke


def gla_fwd(proj, w2p, bg, gn):
    T = proj.shape[0]
    n_steps = T // GLA_ROWS
    n_chunks = T // GLA_CHUNK

    def body(proj_ref, w2_ref, bg_ref, gn_ref, oa_ref, opre_ref, sprev_ref, st_ref):
        @pl.when(pl.program_id(0) == 0)
        def _():
            st_ref[...] = jnp.zeros_like(st_ref)

        causal, _, masks = _gla_step_constants()
        q = proj_ref[:, P_GQ:P_GQ + GLA_QK]
        k = proj_ref[:, P_GK:P_GK + GLA_QK]
        v = proj_ref[:, P_GV:P_GV + GLA_WIDTH]
        r = proj_ref[:, P_GR:P_GR + GLA_WIDTH]
        glow = proj_ref[:, P_GLOW:P_GLOW + LANE].astype(BF16)
        _, _, _, _, ebl, qd, ki, ke = _gla_step_common(q, k, glow, w2_ref[...].astype(BF16), bg_ref[...], causal.astype(BF16))
        ki_b = ki.astype(BF16)
        v_b = v.astype(BF16)
        o_heads = []
        for h in range(GLA_HEADS):
            def intra(qd_g, ki_g, v_g):
                att = jnp.where(causal, _dot_nt(qd_g, ki_g), 0.0)
                return _dot(att.astype(BF16), v_g)

            o_heads.append(_by_group(intra, jnp.where(masks[h], qd, 0.0).astype(BF16), ki_b, v_b[:, _head_cols(h)]))
        st = st_ref[...]
        states = []
        for c in range(GLA_CHUNKS_PER_STEP):
            states.append(st)
            sprev_ref[c] = st
            inc = _diag_blocks(_dot_tn(_chunk_rows_of(v_b, c), _chunk_rows_of(ke, c).astype(BF16)), masks)
            st = st * ebl[c] + inc
        st_ref[...] = st
        inter = []
        for c in range(GLA_CHUNKS_PER_STEP):
            qd_c = _stack_masked(_chunk_rows_of(qd, c), masks).astype(BF16)
            got = _dot_nt(qd_c, states[c].astype(BF16))
            inter.append(jnp.concatenate([got[h * GLA_CHUNK:(h + 1) * GLA_CHUNK] for h in range(GLA_HEADS)], axis=1))
        o = jnp.concatenate(o_heads, axis=1) + jnp.concatenate(inter, axis=0)
        opre_ref[...] = o
        on = jnp.concatenate([o[:, _head_cols(h)] * _rstd(o[:, _head_cols(h)]) for h in range(GLA_HEADS)], axis=1)
        oa_ref[...] = ((on * gn_ref[...]) * (r * _sigmoid(r))).astype(BF16)

    return pl.pallas_call(
        body,
        grid=(n_steps,),
        in_specs=[
            pl.BlockSpec((GLA_ROWS, P_A), lambda i: (i, 0)),
            pl.BlockSpec((LANE, GLA_QK), lambda i: (0, 0)),
            pl.BlockSpec((1, GLA_QK), lambda i: (0, 0)),
            pl.BlockSpec((1, GLA_WIDTH), lambda i: (0, 0)),
        ],
        out_specs=[
            pl.BlockSpec((GLA_ROWS, GLA_WIDTH), lambda i: (i, 0)),
            pl.BlockSpec((GLA_ROWS, GLA_WIDTH), lambda i: (i, 0)),
            pl.BlockSpec((GLA_CHUNKS_PER_STEP, GLA_DV, GLA_QK), lambda i: (i, 0, 0)),
        ],
        out_shape=[
            jax.ShapeDtypeStruct((T, GLA_WIDTH), BF16),
            jax.ShapeDtypeStruct((T, GLA_WIDTH), F32),
            jax.ShapeDtypeStruct((n_chunks, GLA_DV, GLA_QK), F32),
        ],
        scratch_shapes=[pltpu.VMEM((GLA_DV, GLA_QK), F32)],
        compiler_params=_params(),
        name="gla_fwd",
    )(proj, w2p, bg, gn)


def gla_bwd(proj, w2p, bg, gn, opre, sprev, dmixed, exchange=None):
    T = proj.shape[0]
    n_steps = T // GLA_ROWS

    def body(*refs):
        refs = _host_exchange(exchange, refs, 7, 4, pl.program_id(0), n_steps)
        proj_ref, w2_ref, bg_ref, gn_ref, opre_ref, sprev_ref, doa_ref, da_ref, dw2_ref, dbg_ref, dgn_ref, dst_ref = refs

        @pl.when(pl.program_id(0) == 0)
        def _():
            dst_ref[...] = jnp.zeros_like(dst_ref)
            dw2_ref[...] = jnp.zeros_like(dw2_ref)
            dbg_ref[...] = jnp.zeros_like(dbg_ref)
            dgn_ref[...] = jnp.zeros_like(dgn_ref)

        causal, causal_t, masks = _gla_step_constants()
        w2 = w2_ref[...].astype(BF16)
        gn = gn_ref[...]
        q = proj_ref[:, P_GQ:P_GQ + GLA_QK]
        k = proj_ref[:, P_GK:P_GK + GLA_QK]
        v_b = proj_ref[:, P_GV:P_GV + GLA_WIDTH].astype(BF16)
        r = proj_ref[:, P_GR:P_GR + GLA_WIDTH]
        glow = proj_ref[:, P_GLOW:P_GLOW + LANE].astype(BF16)
        o = opre_ref[...]
        doa = doa_ref[...]
        gpre, eb, enb, eke, ebl, qd, ki, ke = _gla_step_common(q, k, glow, w2, bg_ref[...], causal.astype(BF16))
        sig = _sigmoid(r)
        rs = jnp.concatenate([jnp.broadcast_to(_rstd(o[:, _head_cols(h)]), (GLA_ROWS, GLA_DV)) for h in range(GLA_HEADS)], axis=1)
        on = o * rs
        d_ong = doa * (r * sig)
        dr = doa * (on * gn) * (sig * (1.0 + r * (1.0 - sig)))
        dgn_ref[...] += jnp.sum(d_ong * on, axis=0, keepdims=True)
        d_on = d_ong * gn
        t = d_on * on
        mean_t = jnp.concatenate([jnp.broadcast_to(jnp.mean(t[:, _head_cols(h)], axis=-1, keepdims=True), (GLA_ROWS, GLA_DV))
                                  for h in range(GLA_HEADS)], axis=1)
        do_b = (rs * (d_on - on * mean_t)).astype(BF16)
        ki_b = ki.astype(BF16)
        ke_b = ke.astype(BF16)
        dqd = jnp.zeros_like(qd)
        dki = jnp.zeros_like(qd)
        dv_heads = []
        for h in range(GLA_HEADS):
            qd_h = jnp.where(masks[h], qd, 0.0).astype(BF16)
            do_h = do_b[:, _head_cols(h)]

            def intra(qd_g, ki_g, v_g, do_g):
                att = jnp.where(causal, _dot_nt(qd_g, ki_g), 0.0).astype(BF16)
                d_att = jnp.where(causal, _dot_nt(do_g, v_g), 0.0).astype(BF16)
                return _dot_tn(att, do_g), _dot(d_att, ki_g), _dot_tn(d_att, qd_g)

            dv_h, dqd_h, dki_h = _by_group(intra, qd_h, ki_b, v_b[:, _head_cols(h)], do_h)
            dv_heads.append(dv_h)
            dqd = dqd + jnp.where(masks[h], dqd_h, 0.0)
            dki = dki + dki_h
        states = [sprev_ref[c] for c in range(GLA_CHUNKS_PER_STEP)]
        dqd_inter, dst_adds = [], []
        for c in range(GLA_CHUNKS_PER_STEP):
            do_c = _stack_head_cols(_chunk_rows_of(do_b, c))
            dqd_inter.append(_row_blocks_masked(_dot(do_c, states[c].astype(BF16)), masks))
            dst_adds.append(_diag_blocks(_dot_tn(_chunk_rows_of(do_b, c), _chunk_rows_of(qd, c).astype(BF16)), masks))
        dst = dst_ref[...]
        dsts, debls = [None] * GLA_CHUNKS_PER_STEP, [None] * GLA_CHUNKS_PER_STEP
        for c in reversed(range(GLA_CHUNKS_PER_STEP)):
            dsts[c] = dst
            debls[c] = jnp.sum(dst * states[c], axis=0, keepdims=True)
            dst = dst * ebl[c] + dst_adds[c]
        dst_ref[...] = dst
        dv_inter, dke = [], []
        for c in range(GLA_CHUNKS_PER_STEP):
            dst_b = dsts[c].astype(BF16)
            got = _dot_nt(_stack_masked(_chunk_rows_of(ke, c), masks).astype(BF16), dst_b)
            dv_inter.append(jnp.concatenate([got[h * GLA_CHUNK:(h + 1) * GLA_CHUNK] for h in range(GLA_HEADS)], axis=1))
            dke.append(_row_blocks_masked(_dot(_stack_head_cols(_chunk_rows_of(v_b, c)), dst_b), masks))
        dqd = dqd + jnp.concatenate(dqd_inter, axis=0)
        dke = jnp.concatenate(dke, axis=0)
        dv = jnp.concatenate(dv_heads, axis=1) + jnp.concatenate(dv_inter, axis=0)
        dkk = dke * ke
        dbl = jnp.sum(_per_chunk(dkk), axis=1, keepdims=True) + jnp.stack(debls) * ebl
        last_row = lax.broadcasted_iota(jnp.int32, (GLA_CHUNKS_PER_STEP, GLA_CHUNK, GLA_QK), 1) == GLA_CHUNK - 1
        db = dqd * qd - dki * ki - dkk + jnp.where(last_row, dbl, 0.0).reshape(GLA_ROWS, GLA_QK)
        tri_t = causal_t.astype(BF16)
        dglog = _by_group(lambda g: _dot_exact_lhs(tri_t, g), db)
        dgpre = (dglog / GLA_TAU) * _sigmoid(-gpre)
        dgpre_b = dgpre.astype(BF16)
        da_ref[...] = jnp.concatenate(
            [dqd * eb * QK_SCALE, dki * enb + dke * eke, dv, dr, _dot_nt(dgpre_b, w2)], axis=1).astype(BF16)
        dw2_ref[...] += _dot_tn(glow, dgpre_b)
        dbg_ref[...] += jnp.sum(dgpre, axis=0, keepdims=True)

    rev = lambda i: (n_steps - 1 - i, 0)
    return _hosted_call(
        exchange,
        body,
        grid=(n_steps,),
        in_specs=[
            pl.BlockSpec((GLA_ROWS, P_A), rev),
            pl.BlockSpec((LANE, GLA_QK), lambda i: (0, 0)),
            pl.BlockSpec((1, GLA_QK), lambda i: (0, 0)),
            pl.BlockSpec((1, GLA_WIDTH), lambda i: (0, 0)),
            pl.BlockSpec((GLA_ROWS, GLA_WIDTH), rev),
            pl.BlockSpec((GLA_CHUNKS_PER_STEP, GLA_DV, GLA_QK), lambda i: (n_steps - 1 - i, 0, 0)),
            pl.BlockSpec((GLA_ROWS, GLA_WIDTH), rev),
        ],
        out_specs=[
            pl.BlockSpec((GLA_ROWS, P_A), rev),
            pl.BlockSpec((LANE, GLA_QK), lambda i: (0, 0)),
            pl.BlockSpec((1, GLA_QK), lambda i: (0, 0)),
            pl.BlockSpec((1, GLA_WIDTH), lambda i: (0, 0)),
        ],
        out_shape=[
            jax.ShapeDtypeStruct((T, P_A), BF16),
            jax.ShapeDtypeStruct((LANE, GLA_QK), F32),
            jax.ShapeDtypeStruct((1, GLA_QK), F32),
            jax.ShapeDtypeStruct((1, GLA_WIDTH), F32),
        ],
        scratch_shapes=[pltpu.VMEM((GLA_DV, GLA_QK), F32)],
        compiler_params=_params(),
        name="gla_bwd",
        args=(proj, w2p, bg, gn, opre, sprev, dmixed),
    )


def _t5_bucket(dist):
    max_exact = REL_BUCKETS // 2
    n = np.maximum(dist, 0)
    large = max_exact + (np.log(np.maximum(n, 1) / max_exact) / math.log(REL_MAX_DIST / max_exact)
                         * (REL_BUCKETS - max_exact)).astype(np.int32)
    large = np.minimum(large, REL_BUCKETS - 1)
    return np.where(n < max_exact, n, large).astype(np.int32)


SUBLANES = 8


def _bucket_rows():
    steps = DSA_BLOCK - np.arange(2 * DSA_BLOCK)
    in_band = (steps >= 0) & (steps <= DSA_SPAN)
    rows = np.stack([np.where(in_band, _t5_bucket(steps * d), -1) for d in DSA_DILATIONS]).astype(np.int32)
    return np.broadcast_to(rows[:, None, :], (len(DSA_DILATIONS), SUBLANES, 2 * DSA_BLOCK)).copy()


def bias_tables(rel_bias):
    ids = jnp.asarray(_bucket_rows())
    nd = len(DSA_DILATIONS)

    def body(rel_ref, ids_ref, tab_ref):
        h = pl.program_id(1)
        idt = ids_ref[0]
        row = jnp.where(idt < 0, NEG, 0.0).astype(F32)
        for b in range(REL_BUCKETS):
            row = jnp.where(idt == b, rel_ref[b, h], row)
        full = jnp.broadcast_to(row[0:1], (DSA_BLOCK, 2 * DSA_BLOCK))
        tab_ref[0, 0] = pltpu.roll(full, 0, 1, stride=1, stride_axis=0)

    return pl.pallas_call(
        body,
        grid=(nd, DSA_HEADS),
        in_specs=[pl.BlockSpec(memory_space=pltpu.SMEM), pl.BlockSpec((1, SUBLANES, 2 * DSA_BLOCK), lambda d, h: (d, 0, 0))],
        out_specs=pl.BlockSpec((1, 1, DSA_BLOCK, 2 * DSA_BLOCK), lambda d, h: (d, h, 0, 0)),
        out_shape=jax.ShapeDtypeStruct((nd, DSA_HEADS, DSA_BLOCK, 2 * DSA_BLOCK), F32),
        compiler_params=_params(("arbitrary", "arbitrary")),
        name="bias_tables",
    )(rel_bias, ids)


def _bucket_ids():
    L = DSA_BLOCK
    steps = L + np.arange(L)[:, None] - np.arange(2 * L)[None, :]
    in_band = (steps >= 0) & (steps <= DSA_SPAN)
    return np.stack([np.where(in_band, _t5_bucket(steps * d), -1) for d in DSA_DILATIONS]).astype(np.int32)


def bias_tables_bwd(dtab):
    ids = jnp.asarray(_bucket_ids())
    nd = len(DSA_DILATIONS)

    def body(dtab_ref, ids_ref, drel_ref):
        @pl.when((pl.program_id(0) == 0) & (pl.program_id(1) == 0))
        def _():
            for b in range(REL_BUCKETS):
                for h in range(DSA_HEADS):
                    drel_ref[b, h] = 0.0

        h = pl.program_id(1)
        idt = ids_ref[0]
        g = dtab_ref[0, 0]
        for b in range(REL_BUCKETS):
            drel_ref[b, h] += jnp.sum(jnp.where(idt == b, g, 0.0))

    return pl.pallas_call(
        body,
        grid=(nd, DSA_HEADS),
        in_specs=[pl.BlockSpec((1, 1, DSA_BLOCK, 2 * DSA_BLOCK), lambda d, h: (d, h, 0, 0)),
                  pl.BlockSpec((1, DSA_BLOCK, 2 * DSA_BLOCK), lambda d, h: (d, 0, 0))],
        out_specs=pl.BlockSpec(memory_space=pltpu.SMEM),
        out_shape=jax.ShapeDtypeStruct((REL_BUCKETS, DSA_HEADS), F32),
        compiler_params=_params(("arbitrary", "arbitrary")),
        name="bias_tables_bwd",
    )(dtab, ids)


DSA_PAIRS = DSA_HEADS // 2
DSA_UNROLL = 16
DSA_COMBINE_ROWS = 256


def _dsa_units(d):
    return d, DSA_SUPER // (DSA_BLOCK * d)


def _dsa_specs(T):
    nsb = T // DSA_SUPER
    qcol, kcol, vcol = P_DQ // LANE, P_DK // LANE, P_DV // LANE
    return nsb, qcol, kcol, vcol


def _head_lane_mask():
    return lax.broadcasted_iota(jnp.int32, (1, LANE), 1) < DSA_DH


def _fill_tile_variants(tab_ref, variants):
    col = lax.broadcasted_iota(jnp.int32, (2 * DSA_BLOCK, 2 * DSA_BLOCK), 1)
    for di in range(len(DSA_DILATIONS)):
        tile = tab_ref[di, 0]
        variants[di, 0] = tile
        variants[di, 1] = jnp.where(col < DSA_BLOCK, NEG, tile)


def _tile_variants_scratch():
    return pltpu.VMEM((len(DSA_DILATIONS), 2, 2 * DSA_BLOCK, 2 * DSA_BLOCK), F32)


def _pair_tiles(tab):
    return tab.reshape(len(DSA_DILATIONS), DSA_PAIRS, 2 * DSA_BLOCK, 2 * DSA_BLOCK)


def _stack_heads(t, head0):
    return jnp.concatenate([jnp.where(head0, t, 0.0), jnp.where(head0, 0.0, t)], axis=0)


def dsa_fwd(proj, tab, exchange=None):
    T = proj.shape[0]
    nsb, qcol, kcol, vcol = _dsa_specs(T)
    S = DSA_SUPER

    def body(*refs):
        refs = _host_exchange(exchange, refs, 6, 2, pl.program_id(0) * nsb + pl.program_id(1), DSA_PAIRS * nsb)
        q_ref, kp_ref, kc_ref, vp_ref, vc_ref, tab_ref, out_ref, lse_ref, kk, vv, ob, lb, tiles = refs
        sb = pl.program_id(1)
        kk[0:S, :] = kp_ref[...]
        kk[S:2 * S, :] = kc_ref[...]
        vv[0:S, :] = vp_ref[...]
        vv[S:2 * S, :] = vc_ref[...]
        head0 = _head_lane_mask()
        pl.when(sb == 0)(functools.partial(_fill_tile_variants, tab_ref, tiles))

        for di, d in enumerate(DSA_DILATIONS):
            n_res, n_blk = _dsa_units(d)

            def unit(u, carry, di=di, d=d, n_blk=n_blk):
                r = u // n_blk
                c = u % n_blk
                q0 = r + d * DSA_BLOCK * c
                qrows = pl.ds(q0, DSA_BLOCK, stride=d) if d > 1 else pl.ds(q0, DSA_BLOCK)
                krows = pl.ds(S + q0 - d * DSA_BLOCK, 2 * DSA_BLOCK, stride=d) if d > 1 else pl.ds(S + q0 - DSA_BLOCK, 2 * DSA_BLOCK)
                q2 = q_ref[qrows, :] * QK_SCALE
                k2 = kk[krows, :].astype(BF16)
                v2 = vv[krows, :].astype(BF16)
                qs = _stack_heads(q2, head0).astype(BF16)
                s = _dot_nt(qs, k2) + tiles[di, ((sb == 0) & (c == 0)).astype(jnp.int32)]
                m = jnp.max(s, axis=-1, keepdims=True)
                p = jnp.exp(s - m)
                den = jnp.sum(p, axis=-1, keepdims=True)
                o = _dot(p.astype(BF16), v2) / den
                l = jnp.broadcast_to(m + jnp.log(den), (2 * DSA_BLOCK, LANE))
                ob[di, qrows, :] = jnp.where(head0, o[:DSA_BLOCK], o[DSA_BLOCK:])
                lb[di, qrows, :] = jnp.where(head0, l[:DSA_BLOCK], l[DSA_BLOCK:])
                return carry

            lax.fori_loop(0, n_res * n_blk, unit, 0, unroll=DSA_UNROLL)

        def combine(i, carry):
            rows = pl.ds(pl.multiple_of(i * DSA_COMBINE_ROWS, DSA_COMBINE_ROWS), DSA_COMBINE_ROWS)
            l0, l1, l2 = lb[0, rows, :], lb[1, rows, :], lb[2, rows, :]
            mx = jnp.maximum(jnp.maximum(l0, l1), l2)
            e0, e1, e2 = jnp.exp(l0 - mx), jnp.exp(l1 - mx), jnp.exp(l2 - mx)
            den = e0 + e1 + e2
            out_ref[rows, :] = (e0 * ob[0, rows, :] + e1 * ob[1, rows, :] + e2 * ob[2, rows, :]) / den
            lse_ref[rows, :] = mx + jnp.log(den)
            return carry

        lax.fori_loop(0, S // DSA_COMBINE_ROWS, combine, 0)

    prev = lambda col: (lambda hp, sb: (jnp.maximum(sb - 1, 0), col + hp))
    cur = lambda col: (lambda hp, sb: (sb, col + hp))
    blk = lambda f: pl.BlockSpec((S, LANE), f)
    return _hosted_call(
        exchange,
        body,
        grid=(DSA_PAIRS, nsb),
        in_specs=[blk(cur(qcol)), blk(prev(kcol)), blk(cur(kcol)), blk(prev(vcol)), blk(cur(vcol)),
                  pl.BlockSpec((len(DSA_DILATIONS), 1, 2 * DSA_BLOCK, 2 * DSA_BLOCK), lambda hp, sb: (0, hp, 0, 0))],
        out_specs=[blk(lambda hp, sb: (sb, hp)), blk(lambda hp, sb: (sb, hp))],
        out_shape=[jax.ShapeDtypeStruct((T, DSA_WIDTH), F32), jax.ShapeDtypeStruct((T, DSA_WIDTH), F32)],
        scratch_shapes=[pltpu.VMEM((2 * S, LANE), F32), pltpu.VMEM((2 * S, LANE), F32),
                        pltpu.VMEM((len(DSA_DILATIONS), S, LANE), F32), pltpu.VMEM((len(DSA_DILATIONS), S, LANE), F32),
                        _tile_variants_scratch()],
        compiler_params=_params(("arbitrary", "arbitrary")),
        name="dsa_fwd",
        args=(proj, proj, proj, proj, proj, _pair_tiles(tab)),
    )


def dsa_bwd(proj, tab, ob_out, lse, dmixed, exchange=None):
    T = proj.shape[0]
    nsb, qcol, kcol, vcol = _dsa_specs(T)
    S = DSA_SUPER
    nd = len(DSA_DILATIONS)
    ocol = GLA_WIDTH // LANE

    def body(*refs):
        refs = _host_exchange(exchange, refs, 9, 4, pl.program_id(0) * nsb + pl.program_id(1), DSA_PAIRS * nsb)
        (q_ref, kp_ref, kc_ref, vp_ref, vc_ref, tab_ref, o_ref, lse_ref, do_ref,
         dq_ref, dk_ref, dv_ref, dtab_ref, kk, vv, dqa, dkk, dvv, tiles) = refs
        j = pl.program_id(1)
        sb = nsb - 1 - j
        kk[0:S, :] = kp_ref[...]
        kk[S:2 * S, :] = kc_ref[...]
        vv[0:S, :] = vp_ref[...]
        vv[S:2 * S, :] = vc_ref[...]
        head0 = _head_lane_mask()
        pl.when(j == 0)(functools.partial(_fill_tile_variants, tab_ref, tiles))

        @pl.when(j == 0)
        def _():
            dtab_ref[...] = jnp.zeros_like(dtab_ref)
            dkk[S:2 * S, :] = jnp.zeros((S, LANE), F32)
            dvv[S:2 * S, :] = jnp.zeros((S, LANE), F32)

        @pl.when(j > 0)
        def _():
            dkk[S:2 * S, :] = dkk[0:S, :]
            dvv[S:2 * S, :] = dvv[0:S, :]

        dkk[0:S, :] = jnp.zeros((S, LANE), F32)
        dvv[0:S, :] = jnp.zeros((S, LANE), F32)
        dqa[...] = jnp.zeros_like(dqa)

        for di, d in enumerate(DSA_DILATIONS):
            n_res, n_blk = _dsa_units(d)

            def unit(u, carry, di=di, d=d, n_blk=n_blk):
                r = u // n_blk
                c = u % n_blk
                q0 = r + d * DSA_BLOCK * c
                qrows = pl.ds(q0, DSA_BLOCK, stride=d) if d > 1 else pl.ds(q0, DSA_BLOCK)
                krows = pl.ds(S + q0 - d * DSA_BLOCK, 2 * DSA_BLOCK, stride=d) if d > 1 else pl.ds(S + q0 - DSA_BLOCK, 2 * DSA_BLOCK)
                q2 = q_ref[qrows, :] * QK_SCALE
                k2 = kk[krows, :].astype(BF16)
                v2 = vv[krows, :].astype(BF16)
                do2 = do_ref[qrows, :]
                o2 = o_ref[qrows, :]
                l2 = lse_ref[qrows, :]
                qs = _stack_heads(q2, head0).astype(BF16)
                dos = _stack_heads(do2, head0)
                dos_b = dos.astype(BF16)
                delta = jnp.sum(dos * jnp.concatenate([o2, o2], axis=0), axis=-1, keepdims=True)
                lse = jnp.concatenate([jnp.max(jnp.where(head0, l2, -jnp.inf), axis=-1, keepdims=True),
                                       jnp.max(jnp.where(head0, -jnp.inf, l2), axis=-1, keepdims=True)], axis=0)
                s = _dot_nt(qs, k2) + tiles[di, ((sb == 0) & (c == 0)).astype(jnp.int32)]
                p = jnp.exp(s - lse)
                ds = p * (_dot_nt(dos_b, v2) - delta)
                dtab_ref[di, 0] += ds
                ds_b = ds.astype(BF16)
                dq = _dot(ds_b, k2)
                dqa[qrows, :] += jnp.where(head0, dq[:DSA_BLOCK], dq[DSA_BLOCK:]) * QK_SCALE
                dkk[krows, :] += _dot_tn(ds_b, qs)
                dvv[krows, :] += _dot_tn(p.astype(BF16), dos_b)
                return carry

            lax.fori_loop(0, n_res * n_blk, unit, 0, unroll=DSA_UNROLL)

        dq_ref[...] = dqa[...].astype(BF16)
        dk_ref[...] = dkk[S:2 * S, :].astype(BF16)
        dv_ref[...] = dvv[S:2 * S, :].astype(BF16)

    prev = lambda col: (lambda hp, j: (jnp.maximum(nsb - 2 - j, 0), col + hp))
    cur = lambda col: (lambda hp, j: (nsb - 1 - j, col + hp))
    blk = lambda f: pl.BlockSpec((S, LANE), f)
    out_blk = blk(lambda hp, j: (nsb - 1 - j, hp))
    tab_blk = pl.BlockSpec((nd, 1, 2 * DSA_BLOCK, 2 * DSA_BLOCK), lambda hp, j: (0, hp, 0, 0))
    dq, dk, dv, dtab, *carried = _hosted_call(
        exchange,
        body,
        grid=(DSA_PAIRS, nsb),
        in_specs=[blk(cur(qcol)), blk(prev(kcol)), blk(cur(kcol)), blk(prev(vcol)), blk(cur(vcol)), tab_blk,
                  out_blk, out_blk, blk(cur(ocol))],
        out_specs=[out_blk, out_blk, out_blk, tab_blk],
        out_shape=[jax.ShapeDtypeStruct((T, DSA_WIDTH), BF16)] * 3
        + [jax.ShapeDtypeStruct((nd, DSA_PAIRS, 2 * DSA_BLOCK, 2 * DSA_BLOCK), F32)],
        scratch_shapes=[pltpu.VMEM((2 * S, LANE), F32), pltpu.VMEM((2 * S, LANE), F32), pltpu.VMEM((S, LANE), F32),
                        pltpu.VMEM((2 * S, LANE), F32), pltpu.VMEM((2 * S, LANE), F32), _tile_variants_scratch()],
        compiler_params=_params(("arbitrary", "arbitrary")),
        name="dsa_bwd",
        args=(proj, proj, proj, proj, proj, _pair_tiles(tab), ob_out, lse, dmixed),
    )
    return (dq, dk, dv, dtab.reshape(nd, DSA_HEADS, DSA_BLOCK, 2 * DSA_BLOCK), *carried)


FF_BLOCKS = 4
FF_BLOCK = D_FF // FF_BLOCKS


def post_fused(x, oa, ob, tgt, g2, gf, wout, wff1, wff2):
    T = x.shape[0]
    tm = 256
    inv_d = 1.0 / D_MODEL

    def body(x_ref, oa_ref, ob_ref, tgt_ref, g2_ref, gf_ref, wout_hbm, wff1_hbm, wff2_hbm,
             mixed_ref, nm_ref, a_ref, dpre_ref, dh2_ref, dh1_ref, dmixed_ref, loss_ref, dgf_ref, dg2_ref,
             wout_v, wff1_v, wff2_v, sems):
        @pl.when(pl.program_id(0) == 0)
        def _():
            cps = [pltpu.make_async_copy(s, d, sems.at[i])
                   for i, (s, d) in enumerate([(wout_hbm, wout_v), (wff1_hbm, wff1_v), (wff2_hbm, wff2_v)])]
            for cp in cps:
                cp.start()
            for cp in cps:
                cp.wait()
            loss_ref[...] = jnp.zeros_like(loss_ref)
            dgf_ref[...] = jnp.zeros_like(dgf_ref)
            dg2_ref[...] = jnp.zeros_like(dg2_ref)

        mixed = jnp.concatenate([oa_ref[...], ob_ref[...].astype(BF16)], axis=1)
        mixed_ref[...] = mixed
        h1 = x_ref[...] + _dot(mixed, wout_v[...])
        rs1 = _rstd(h1)
        hn1 = h1 * rs1
        g2 = g2_ref[...]
        nm = (hn1 * g2).astype(BF16)
        nm_ref[...] = nm
        relu = []
        mlp = jnp.zeros((tm, D_MODEL), F32)
        for j in range(FF_BLOCKS):
            cols = slice(j * FF_BLOCK, (j + 1) * FF_BLOCK)
            r_j = jnp.maximum(_dot(nm, wff1_v[j]), 0.0)
            a_j = (r_j * r_j).astype(BF16)
            a_ref[:, cols] = a_j
            relu.append(r_j)
            mlp = mlp + _dot(a_j, wff2_v[cols, :])
        h2 = h1 + mlp
        rsf = _rstd(h2)
        hnf = h2 * rsf
        gf = gf_ref[...]
        diff = hnf * gf - tgt_ref[...]
        loss_ref[...] += 0.5 * jnp.sum(jnp.sum(diff * diff, axis=-1, keepdims=True) * inv_d, axis=0, keepdims=True)
        dy = diff * inv_d
        dgf_ref[...] += jnp.sum(dy * hnf, axis=0, keepdims=True)
        dhnf = dy * gf
        dh2 = rsf * (dhnf - hnf * jnp.mean(dhnf * hnf, axis=-1, keepdims=True))
        dh2_b = dh2.astype(BF16)
        dh2_ref[...] = dh2_b
        dnm = jnp.zeros((tm, D_MODEL), F32)
        for j in range(FF_BLOCKS):
            cols = slice(j * FF_BLOCK, (j + 1) * FF_BLOCK)
            dpre_j = (_dot_nt(dh2_b, wff2_v[cols, :]) * (2.0 * relu[j])).astype(BF16)
            dpre_ref[:, cols] = dpre_j
            dnm = dnm + _dot_nt(dpre_j, wff1_v[j])
        dg2_ref[...] += jnp.sum(dnm * hn1, axis=0, keepdims=True)
        dhn1 = dnm * g2
        dh1 = dh2 + rs1 * (dhn1 - hn1 * jnp.mean(dhn1 * hn1, axis=-1, keepdims=True))
        dh1_ref[...] = dh1
        dmixed_ref[...] = _dot_nt(dh1.astype(BF16), wout_v[...])

    row = lambda w: pl.BlockSpec((tm, w), lambda i: (i, 0))
    vec = lambda w: pl.BlockSpec((1, w), lambda i: (0, 0))
    return pl.pallas_call(
        body,
        grid=(T // tm,),
        in_specs=[row(D_MODEL), row(GLA_WIDTH), row(DSA_WIDTH), row(D_MODEL), vec(D_MODEL), vec(D_MODEL), ANY, ANY, ANY],
        out_specs=[row(D_MODEL), row(D_MODEL), row(D_FF), row(D_FF), row(D_MODEL), row(D_MODEL), row(D_MODEL),
                   vec(1), vec(D_MODEL), vec(D_MODEL)],
        out_shape=[
            jax.ShapeDtypeStruct((T, D_MODEL), BF16),
            jax.ShapeDtypeStruct((T, D_MODEL), BF16),
            jax.ShapeDtypeStruct((T, D_FF), BF16),
            jax.ShapeDtypeStruct((T, D_FF), BF16),
            jax.ShapeDtypeStruct((T, D_MODEL), BF16),
            jax.ShapeDtypeStruct((T, D_MODEL), F32),
            jax.ShapeDtypeStruct((T, D_MODEL), F32),
            jax.ShapeDtypeStruct((1, 1), F32),
            jax.ShapeDtypeStruct((1, D_MODEL), F32),
            jax.ShapeDtypeStruct((1, D_MODEL), F32),
        ],
        scratch_shapes=[pltpu.VMEM((D_MODEL, D_MODEL), BF16), pltpu.VMEM((FF_BLOCKS, D_MODEL, FF_BLOCK), BF16),
                        pltpu.VMEM((D_FF, D_MODEL), BF16), pltpu.SemaphoreType.DMA((3,))],
        compiler_params=_params(),
        name="post_fused",
    )(x, oa, ob, tgt, g2, gf, wout, wff1, wff2)


WGRAD_TOKENS = 2048


def wgrad(a, b, name, bm=None, bn=None, col_blocked=False):
    T, M = a.shape
    N = b.shape[1]
    bm = M if bm is None else bm
    bn = N if bn is None else bn
    tk = min(WGRAD_TOKENS, T)
    n_k = T // tk

    def body(a_ref, b_ref, o_ref, acc_ref):
        part = _dot_tn(a_ref[...].astype(BF16), b_ref[...].astype(BF16))
        out = o_ref.at[0] if col_blocked else o_ref
        k = pl.program_id(2)
        if n_k == 1:
            out[...] = part.astype(BF16)
            return

        @pl.when(k == 0)
        def _():
            acc_ref[...] = part

        @pl.when((k > 0) & (k < n_k - 1))
        def _():
            acc_ref[...] += part

        @pl.when(k == n_k - 1)
        def _():
            out[...] = (acc_ref[...] + part).astype(BF16)

    if col_blocked:
        assert bm == M
        out_spec = pl.BlockSpec((1, M, bn), lambda i, j, k: (j, 0, 0))
        out_shape = jax.ShapeDtypeStruct((N // bn, M, bn), BF16)
    else:
        out_spec = pl.BlockSpec((bm, bn), lambda i, j, k: (i, j))
        out_shape = jax.ShapeDtypeStruct((M, N), BF16)
    return pl.pallas_call(
        body,
        grid=(M // bm, N // bn, n_k),
        in_specs=[pl.BlockSpec((tk, bm), lambda i, j, k: (k, i)), pl.BlockSpec((tk, bn), lambda i, j, k: (k, j))],
        out_specs=out_spec,
        out_shape=out_shape,
        scratch_shapes=[pltpu.VMEM((bm, bn), F32)],
        compiler_params=_params(("arbitrary", "arbitrary", "arbitrary")),
        name=name,
    )(a, b)


def wgrad_cat(a, bs, name):
    T, M = a.shape
    widths = [b.shape[1] for b in bs]
    starts = [sum(widths[:i]) for i in range(len(bs))]
    N = sum(widths)
    tk = min(WGRAD_TOKENS // 2, T)
    n_k = T // tk

    def body(a_ref, *rest):
        b_refs, o_ref, acc_ref = rest[:len(bs)], rest[len(bs)], rest[len(bs) + 1]
        k = pl.program_id(0)

        @pl.when(k == 0)
        def _():
            acc_ref[...] = jnp.zeros_like(acc_ref)

        a_t = a_ref[...]
        for b_ref, start, width in zip(b_refs, starts, widths):
            acc_ref[:, start:start + width] += _dot_tn(a_t, b_ref[...])

        @pl.when(k == n_k - 1)
        def _():
            o_ref[...] = acc_ref[...].astype(BF16)

    return pl.pallas_call(
        body,
        grid=(n_k,),
        in_specs=[pl.BlockSpec((tk, M), lambda k: (k, 0))] + [pl.BlockSpec((tk, w), lambda k: (k, 0)) for w in widths],
        out_specs=pl.BlockSpec((M, N), lambda k: (0, 0)),
        out_shape=jax.ShapeDtypeStruct((M, N), BF16),
        scratch_shapes=[pltpu.VMEM((M, N), F32)],
        compiler_params=_params(),
        name=name,
    )(a, *bs)


def dx_final(x, dh1, g1, da, dq, dk, dv, wp, exchange=None):
    T = x.shape[0]
    tm = 256

    def body(*refs):
        refs = _host_exchange(exchange, refs, 8, 2, pl.program_id(0), T // tm)
        x_ref, dh1_ref, g_ref, da_ref, dq_ref, dk_ref, dv_ref, w_hbm, dx_ref, dg_ref, w_vmem, sem = refs

        @pl.when(pl.program_id(0) == 0)
        def _():
            _load_once(w_hbm, w_vmem, sem)
            dg_ref[...] = jnp.zeros_like(dg_ref)

        dnx = (_dot_nt(da_ref[...], w_vmem[:, 0:P_A]) + _dot_nt(dq_ref[...], w_vmem[:, P_DQ:P_DQ + DSA_WIDTH])
               + _dot_nt(dk_ref[...], w_vmem[:, P_DK:P_DK + DSA_WIDTH]) + _dot_nt(dv_ref[...], w_vmem[:, P_DV:P_DV + DSA_WIDTH]))
        xf = x_ref[...]
        rs = _rstd(xf)
        hn = xf * rs
        dg_ref[...] += jnp.sum(dnx * hn, axis=0, keepdims=True)
        dhn = dnx * g_ref[...]
        dx_ref[...] = dh1_ref[...] + rs * (dhn - hn * jnp.mean(dhn * hn, axis=-1, keepdims=True))

    row = lambda w: pl.BlockSpec((tm, w), lambda i: (i, 0))
    vec = pl.BlockSpec((1, D_MODEL), lambda i: (0, 0))
    return _hosted_call(
        exchange,
        body,
        grid=(T // tm,),
        in_specs=[row(D_MODEL), row(D_MODEL), vec, row(P_A), row(DSA_WIDTH), row(DSA_WIDTH), row(DSA_WIDTH), ANY],
        out_specs=[row(D_MODEL), vec],
        out_shape=[jax.ShapeDtypeStruct((T, D_MODEL), F32), jax.ShapeDtypeStruct((1, D_MODEL), F32)],
        scratch_shapes=[pltpu.VMEM((D_MODEL, P_ALL), BF16), pltpu.SemaphoreType.DMA],
        compiler_params=_params(),
        name="dx_final",
        args=(x, dh1, g1, da, dq, dk, dv, wp),
    )


def adamw(w, g, m, v, name):
    R, C = w.shape
    br = 256 if R % 256 == 0 else R

    def body(w_ref, g_ref, m_ref, v_ref, d_ref, nm_ref, nv_ref):
        d_ref[...], nm_ref[...], nv_ref[...] = _adamw_math(w_ref[...], g_ref[...], m_ref[...], v_ref[...])

    spec = pl.BlockSpec((br, C), lambda i: (i, 0))
    return pl.pallas_call(
        body,
        grid=(R // br,),
        in_specs=[spec] * 4,
        out_specs=[spec] * 3,
        out_shape=[jax.ShapeDtypeStruct((R, C), F32)] * 3,
        compiler_params=_params(),
        name=name,
    )(w, g, m, v)


def _place():
    return lax.axis_index("x"), lax.axis_index("y"), lax.axis_index("c")


def _other_chips(x, y):
    return [(1 - x, y), (x, 1 - y), (1 - x, 1 - y)]


class Exchange:
    def __init__(self, kind, arrays):
        self.kind, self.arrays, self.n = kind, arrays, len(arrays)
        self.slots = 4 if kind == "gather" else 8

    def out_shapes(self):
        if self.kind == "gather":
            return [jax.ShapeDtypeStruct((4,) + s.shape, s.dtype) for s in self.arrays]
        return [jax.ShapeDtypeStruct((8,) + s.shape[1:], s.dtype) for s in self.arrays]

    def sems(self):
        return [pltpu.SemaphoreType.DMA((self.n, 19)), pltpu.SemaphoreType.DMA((self.n, 19))]

    def phases(self, ins, outs, send_sems, recv_sems):
        n, scatter = self.n, self.kind == "scatter"
        x, y, c = _place()
        me, sib = (x, y, c), (x, y, 1 - c)
        mine = 2 * x + y
        chips = _other_chips(x, y)
        own_pair = 18

        def region(a, slot, half):
            h = outs[a].shape[1] // 2
            return outs[a].at[slot, pl.ds(half * h, h)]

        def copy(a, k, slot, half, to, src=None):
            return pltpu.make_async_remote_copy(
                src_ref=region(a, slot, half) if src is None else src, dst_ref=region(a, slot, half),
                send_sem=send_sems.at[a, k], recv_sem=recv_sems.at[a, k], device_id=to, device_id_type=MESH)

        def over_ici(t, to_core, from_core):
            return 4 * t + 2 * to_core + from_core

        def passed_on(t, from_core):
            return 12 + 2 * t + from_core

        senders = [(t, cc) for t in range(3) for cc in ((0, 1) if scatter else (c,))]

        def slot_of(t, cc):
            cx, cy = chips[t]
            return 2 * (2 * cx + cy) + cc if scatter else 2 * cx + cy

        def first_copies():
            cps = []
            for a in range(n):
                h = outs[a].shape[1] // 2
                for t, (cx, cy) in enumerate(chips):
                    if scatter:
                        for half in (0, 1):
                            cps.append(copy(a, over_ici(t, half, c), 2 * mine + c, half, (cx, cy, half),
                                            src=ins[a].at[2 * cx + cy, pl.ds(half * h, h)]))
                    else:
                        cps.append(copy(a, over_ici(t, c, c), mine, c, (cx, cy, c), src=ins[a].at[pl.ds(c * h, h)]))
                if scatter:
                    cps.append(pltpu.make_async_remote_copy(
                        src_ref=ins[a].at[mine], dst_ref=outs[a].at[2 * mine + c], send_sem=send_sems.at[a, own_pair],
                        recv_sem=recv_sems.at[a, own_pair], device_id=sib, device_id_type=MESH))
            return cps

        def forward_copies():
            return [copy(a, passed_on(t, cc), slot_of(t, cc), c, sib) for a in range(n) for t, cc in senders]

        def start():
            for cp in first_copies():
                cp.start()

        def forward():
            fws = iter(forward_copies())
            for a in range(n):
                for t, cc in senders:
                    copy(a, over_ici(t, c, cc), slot_of(t, cc), c, me).wait_recv()
                    next(fws).start()

        def finish():
            for a in range(n):
                for t, cc in senders:
                    from_core = cc if scatter else 1 - c
                    copy(a, passed_on(t, from_core), slot_of(t, from_core), 1 - c, me).wait_recv()
                if scatter:
                    pltpu.make_async_remote_copy(
                        src_ref=ins[a].at[mine], dst_ref=outs[a].at[2 * mine + 1 - c], send_sem=send_sems.at[a, own_pair],
                        recv_sem=recv_sems.at[a, own_pair], device_id=me, device_id_type=MESH).wait_recv()
            for cp in first_copies() + forward_copies():
                cp.wait_send()

        return start, forward, finish

    def fill_own(self, outs):
        x, y, c = _place()
        if self.kind == "gather":
            return [lax.dynamic_update_index_in_dim(o, s, 2 * x + y, 0) for o, s in zip(outs, self.arrays)]
        return [lax.dynamic_update_index_in_dim(o, lax.dynamic_index_in_dim(s, 2 * x + y, 0, keepdims=False), 2 * (2 * x + y) + c, 0)
                for o, s in zip(outs, self.arrays)]

    def run(self, name):
        n = self.n

        def body(*refs):
            start, forward, finish = self.phases(refs[:n], refs[n:2 * n], *refs[2 * n:])
            start()
            forward()
            finish()

        outs = pl.pallas_call(
            body, in_specs=[ANY] * n, out_specs=[ANY] * n, out_shape=self.out_shapes(), scratch_shapes=self.sems(), name=name,
        )(*self.arrays)
        return self.fill_own(outs)


def _host_exchange(exchange, refs, n_in, n_out, step, n_steps):
    if exchange is None:
        return refs
    n = exchange.n
    own_in, ex_in = refs[:n_in], refs[n_in:n_in + n]
    own_out, ex_out = refs[n_in + n:n_in + n + n_out], refs[n_in + n + n_out:n_in + 2 * n + n_out]
    rest = refs[n_in + 2 * n + n_out:]
    start, forward, finish = exchange.phases(ex_in, ex_out, rest[-2], rest[-1])
    pl.when(step == 0)(start)
    pl.when(step == (2 * n_steps) // 3)(forward)
    pl.when(step == n_steps - 1)(finish)
    return own_in + own_out + rest[:-2]


def _hosted_call(exchange, body, *, grid, in_specs, out_specs, out_shape, scratch_shapes, compiler_params, name, args):
    if exchange is None:
        return pl.pallas_call(body, grid=grid, in_specs=in_specs, out_specs=out_specs, out_shape=out_shape,
                              scratch_shapes=scratch_shapes, compiler_params=compiler_params, name=name)(*args)
    n = exchange.n
    res = pl.pallas_call(
        body, grid=grid, in_specs=list(in_specs) + [ANY] * n, out_specs=list(out_specs) + [ANY] * n,
        out_shape=list(out_shape) + exchange.out_shapes(), scratch_shapes=list(scratch_shapes) + exchange.sems(),
        compiler_params=compiler_params, name=name)(*args, *exchange.arrays)
    return list(res[:len(out_shape)]) + [exchange.fill_own(res[len(out_shape):])]


def sum_slots(parts, name):
    S, R, C = parts.shape
    br = 128 if R % 128 == 0 else R

    def body(p_ref, o_ref):
        acc = p_ref[0].astype(F32)
        for s in range(1, S):
            acc = acc + p_ref[s].astype(F32)
        o_ref[...] = acc

    return pl.pallas_call(
        body,
        grid=(R // br,),
        in_specs=[pl.BlockSpec((S, br, C), lambda i: (0, i, 0))],
        out_specs=pl.BlockSpec((br, C), lambda i: (i, 0)),
        out_shape=jax.ShapeDtypeStruct((R, C), F32),
        compiler_params=_params(),
        name=name,
    )(parts)


def _adamw_math(w, g, m, v):
    m_new = ADAM_B1 * m + (1.0 - ADAM_B1) * g
    v_new = ADAM_B2 * v + (1.0 - ADAM_B2) * (g * g)
    m_hat = m_new / (1.0 - ADAM_B1 ** ADAM_STEP)
    v_hat = v_new / (1.0 - ADAM_B2 ** ADAM_STEP)
    return -ADAM_LR * (m_hat / (jnp.sqrt(v_hat) + ADAM_EPS) + ADAM_WD * w), m_new, v_new


def reduce_adamw(slots, w, m, v, name):
    S, R, C = slots.shape
    br = 128

    def body(p_ref, w_ref, m_ref, v_ref, g_ref, d_ref, nm_ref, nv_ref):
        g = p_ref[0].astype(F32)
        for s in range(1, S):
            g = g + p_ref[s].astype(F32)
        g_ref[...] = g
        d_ref[...], nm_ref[...], nv_ref[...] = _adamw_math(w_ref[...], g, m_ref[...], v_ref[...])

    spec = pl.BlockSpec((br, C), lambda i: (i, 0))
    return pl.pallas_call(
        body,
        grid=(R // br,),
        in_specs=[pl.BlockSpec((S, br, C), lambda i: (0, i, 0)), spec, spec, spec],
        out_specs=[spec] * 4,
        out_shape=[jax.ShapeDtypeStruct((R, C), F32)] * 4,
        compiler_params=_params(),
        name=name,
    )(slots, w, m, v)


SMALL_ROWS = 72


def gather_small(vec):
    def body(v_ref, o_ref, send_sems, recv_sems, local_sem):
        x, y, c = _place()
        flips = [(fx, fy, fc) for fx in (0, 1) for fy in (0, 1) for fc in (0, 1)][1:]

        def peer(f):
            return (1 - x if f[0] else x, 1 - y if f[1] else y, 1 - c if f[2] else c)

        slot = lambda p: 4 * p[0] + 2 * p[1] + p[2]
        own = pltpu.make_async_copy(v_ref, o_ref.at[slot((x, y, c))], local_sem)
        own.start()
        cps = [pltpu.make_async_remote_copy(
            src_ref=v_ref, dst_ref=o_ref.at[slot((x, y, c))], send_sem=send_sems.at[k], recv_sem=recv_sems.at[k],
            device_id=peer(f), device_id_type=MESH) for k, f in enumerate(flips)]
        for cp in cps:
            cp.start()
        for k, f in enumerate(flips):
            pltpu.make_async_remote_copy(
                src_ref=v_ref, dst_ref=o_ref.at[slot(peer(f))], send_sem=send_sems.at[k], recv_sem=recv_sems.at[k],
                device_id=(x, y, c), device_id_type=MESH).wait_recv()
        for cp in cps:
            cp.wait_send()
        own.wait()

    return pl.pallas_call(
        body,
        in_specs=[ANY],
        out_specs=ANY,
        out_shape=jax.ShapeDtypeStruct((8,) + vec.shape, vec.dtype),
        scratch_shapes=[pltpu.SemaphoreType.DMA((7,)), pltpu.SemaphoreType.DMA((7,)), pltpu.SemaphoreType.DMA],
        name="gather_small",
    )(vec)


GLOW_PAD = LANE - GLA_RANK


def kernel(x, attn_norm_g, w_in, gla_gate_w2, gla_gate_b, gla_norm_g, rel_bias, w_out, mlp_norm_g, w_ff1, w_ff2, final_norm_g, loss_target, m_attn_norm_g, m_w_in, m_gla_gate_w2, m_gla_gate_b, m_gla_norm_g, m_rel_bias, m_w_out, m_mlp_norm_g, m_w_ff1, m_w_ff2, m_final_norm_g, v_attn_norm_g, v_w_in, v_gla_gate_w2, v_gla_gate_b, v_gla_norm_g, v_rel_bias, v_w_out, v_mlp_norm_g, v_w_ff1, v_w_ff2, v_final_norm_g):
    xs, tgt = x[0], loss_target[0]
    T = xs.shape[0]
    cx, cy, _ = _place()
    chip = 2 * cx + cy
    gf = final_norm_g.reshape(1, D_MODEL)

    win_g, w2_g = Exchange("gather", [w_in[0].astype(BF16), gla_gate_w2[0]]).run("gather_w_in")
    win = jnp.transpose(win_g, (1, 0, 2)).reshape(D_MODEL, D_IN)
    n_glow = R_GLOW + GLA_RANK
    wp = jnp.concatenate([win[:, :n_glow], jnp.zeros((D_MODEL, GLOW_PAD), BF16), win[:, n_glow:]], axis=1)
    w2 = jnp.transpose(w2_g, (1, 0, 2)).reshape(GLA_RANK, GLA_QK)
    w2p = jnp.concatenate([w2, jnp.zeros((GLOW_PAD, GLA_QK), F32)], axis=0)

    proj, nx = inproj(xs, attn_norm_g, wp)
    tab = bias_tables(rel_bias)
    ob, lse, (wout_g, wff1, wff2_g) = dsa_fwd(
        proj, tab, Exchange("gather", [w_out[0].astype(BF16), w_ff1[0].astype(BF16), w_ff2[0].astype(BF16)]))
    wout = wout_g.reshape(D_MODEL, D_MODEL)
    wff2 = wff2_g.reshape(D_FF, D_MODEL)
    oa, opre, sprev = gla_fwd(proj, w2p, gla_gate_b, gla_norm_g)
    mixed, nm, act, dpre, dh2, dh1, dmixed, loss, dgf, dg2 = post_fused(xs, oa, ob, tgt, mlp_norm_g, gf, wout, wff1, wff2)

    late = [
        wgrad(mixed, dh1, "wgrad_out").reshape(4, D_MODEL // 4, D_MODEL),
        wgrad(nm, dpre, "wgrad_ff1", bn=FF_BLOCK, col_blocked=True),
        wgrad(act, dh2, "wgrad_ff2", bm=FF_BLOCK).reshape(4, FF_BLOCK, D_MODEL),
    ]
    da, dw2p, dbg, dgn = gla_bwd(proj, w2p, gla_gate_b, gla_norm_g, opre, sprev, dmixed)
    dq, dk, dv, dtab, late_slots = dsa_bwd(proj, tab, ob, lse, dmixed, Exchange("scatter", late))
    slots = dict(zip(["w_out", "w_ff1", "w_ff2"], late_slots))
    drel = bias_tables_bwd(dtab)

    dwp = wgrad_cat(nx, [da, dq, dk, dv], "wgrad_in")
    dwin = jnp.concatenate([dwp[:, :n_glow], dwp[:, P_A:]], axis=1)
    dwin = [jnp.transpose(dwin.reshape(D_MODEL, 4, D_IN // 4), (1, 0, 2))]
    dxs, dg1, (slots["w_in"],) = dx_final(xs, dh1, attn_norm_g, da, dq, dk, dv, wp, Exchange("scatter", dwin))

    sizes = [D_MODEL, GLA_QK, GLA_WIDTH, REL_BUCKETS * DSA_HEADS, D_MODEL, D_MODEL, GLA_RANK * GLA_QK, 1]
    small = jnp.concatenate([dg1.reshape(-1), dbg.reshape(-1), dgn.reshape(-1), drel.reshape(-1), dg2.reshape(-1),
                             dgf.reshape(-1), dw2p[:GLA_RANK].reshape(-1), loss.reshape(-1),
                             jnp.zeros((SMALL_ROWS * LANE - sum(sizes),), F32)]).reshape(SMALL_ROWS, LANE)
    tot = sum_slots(gather_small(small), "sum_small").reshape(-1)
    offs = np.concatenate([[0], np.cumsum(sizes)])
    piece = lambda i: tot[int(offs[i]):int(offs[i + 1])]
    g_g1 = piece(0).reshape(1, D_MODEL)
    g_bg = piece(1).reshape(1, GLA_QK)
    g_gn = piece(2).reshape(1, GLA_WIDTH)
    g_rel = piece(3).reshape(REL_BUCKETS, DSA_HEADS)
    g_g2 = piece(4).reshape(1, D_MODEL)
    g_gf = piece(5).reshape(1, D_MODEL)
    g_w2 = lax.dynamic_slice_in_dim(piece(6).reshape(GLA_RANK, GLA_QK), chip * (GLA_QK // 4), GLA_QK // 4, axis=1)

    loss_all = piece(7)[0]

    upd = [
        ("attn_norm_g", attn_norm_g, g_g1, m_attn_norm_g, v_attn_norm_g),
        ("w_in", w_in[0], None, m_w_in[0], v_w_in[0]),
        ("gla_gate_w2", gla_gate_w2[0], g_w2, m_gla_gate_w2[0], v_gla_gate_w2[0]),
        ("gla_gate_b", gla_gate_b, g_bg, m_gla_gate_b, v_gla_gate_b),
        ("gla_norm_g", gla_norm_g, g_gn, m_gla_norm_g, v_gla_norm_g),
        ("rel_bias", rel_bias, g_rel, m_rel_bias, v_rel_bias),
        ("w_out", w_out[0], None, m_w_out[0], v_w_out[0]),
        ("mlp_norm_g", mlp_norm_g, g_g2, m_mlp_norm_g, v_mlp_norm_g),
        ("w_ff1", w_ff1[0], None, m_w_ff1[0], v_w_ff1[0]),
        ("w_ff2", w_ff2[0], None, m_w_ff2[0], v_w_ff2[0]),
        ("final_norm_g", gf, g_gf, m_final_norm_g.reshape(1, D_MODEL), v_final_norm_g.reshape(1, D_MODEL)),
    ]
    shapes = [attn_norm_g.shape, w_in.shape, gla_gate_w2.shape, gla_gate_b.shape, gla_norm_g.shape, rel_bias.shape,
              w_out.shape, mlp_norm_g.shape, w_ff1.shape, w_ff2.shape, final_norm_g.shape]
    grads, deltas, new_m, new_v = [], [], [], []
    for (name, w, g, m, v), shape in zip(upd, shapes):
        if name in slots:
            g, d, nm_, nv_ = reduce_adamw(slots[name], w, m, v, "reduce_adamw_" + name)
        else:
            d, nm_, nv_ = adamw(w, g, m, v, "adamw_" + name)
        grads.append(g.reshape(shape))
        deltas.append(d.reshape(shape))
        new_m.append(nm_.reshape(shape))
        new_v.append(nv_.reshape(shape))
    return (loss_all, dxs.reshape(1, T, D_MODEL), *grads, *deltas, *new_m, *new_v)
```

```python
import functools
import math

import jax
import jax.numpy as jnp
import numpy as np
from jax import lax
from jax.experimental import pallas as pl
from jax.experimental.pallas import tpu as pltpu

F32 = jnp.float32
BF16 = jnp.bfloat16
MESH = pl.DeviceIdType.MESH

D_MODEL = 1024
GLA_WIDTH = 512
GLA_HEADS = 4
GLA_DK = 64
GLA_DV = 128
GLA_QK = GLA_HEADS * GLA_DK
GLA_RANK = 16
GLA_TAU = 16.0
GLA_CHUNK = 64
DSA_WIDTH = 512
DSA_HEADS = 8
DSA_DH = 64
DSA_DILATIONS = (1, 4, 16)
DSA_SPAN = 128
DSA_BLOCK = 128
DSA_SUPER = DSA_BLOCK * DSA_DILATIONS[-1]
REL_BUCKETS = 32
REL_MAX_DIST = 2048
D_FF = 4096
D_IN = 3088
EPS = 1e-6
NEG = -1e30
QK_SCALE = 0.125

ADAM_LR = 0.001
ADAM_B1 = 0.9
ADAM_B2 = 0.999
ADAM_EPS = 1e-08
ADAM_WD = 0.01
ADAM_STEP = 10

LANE = 128
P_GQ, P_GK, P_GV, P_GR = 0, 256, 512, 1024
P_GLOW = 1536
P_A = 1664
P_DQ, P_DK, P_DV = 1664, 2176, 2688
P_ALL = 3200
R_GLOW = 1536

VMEM_LIMIT = 56 * 1024 * 1024


def _params(sem=("arbitrary",), vmem=VMEM_LIMIT):
    return pltpu.CompilerParams(dimension_semantics=sem, vmem_limit_bytes=vmem)


def _dot(a, b):
    return jnp.dot(a, b, preferred_element_type=F32)


def _dot_nt(a, b):
    return lax.dot_general(a, b, (((1,), (1,)), ((), ())), preferred_element_type=F32)


def _dot_tn(a, b):
    return lax.dot_general(a, b, (((0,), (0,)), ((), ())), preferred_element_type=F32)


def _split3(x):
    x1 = x.astype(BF16)
    r1 = x - x1.astype(F32)
    x2 = r1.astype(BF16)
    x3 = (r1 - x2.astype(F32)).astype(BF16)
    return x1, x2, x3


def _dot_exact_lhs(m_bf16, x):
    x1, x2, x3 = _split3(x)
    return _dot(m_bf16, x1) + _dot(m_bf16, x2) + _dot(m_bf16, x3)


def _rstd(xf):
    return lax.rsqrt(jnp.mean(xf * xf, axis=-1, keepdims=True) + EPS)


def _load_once(hbm_ref, vmem_ref, sem):
    cp = pltpu.make_async_copy(hbm_ref, vmem_ref, sem)
    cp.start()
    cp.wait()


ANY = pl.BlockSpec(memory_space=pl.ANY)


def inproj(x, g1, wp):
    T = x.shape[0]
    tm = 256

    def body(x_ref, g_ref, w_hbm, proj_ref, nx_ref, w_vmem, sem):
        @pl.when(pl.program_id(0) == 0)
        def _():
            _load_once(w_hbm, w_vmem, sem)

        xf = x_ref[...]
        nx = ((xf * _rstd(xf)) * g_ref[...]).astype(BF16)
        nx_ref[...] = nx
        proj_ref[...] = _dot(nx, w_vmem[...])

    return pl.pallas_call(
        body,
        grid=(T // tm,),
        in_specs=[pl.BlockSpec((tm, D_MODEL), lambda i: (i, 0)), pl.BlockSpec((1, D_MODEL), lambda i: (0, 0)), ANY],
        out_specs=[pl.BlockSpec((tm, P_ALL), lambda i: (i, 0)), pl.BlockSpec((tm, D_MODEL), lambda i: (i, 0))],
        out_shape=[jax.ShapeDtypeStruct((T, P_ALL), F32), jax.ShapeDtypeStruct((T, D_MODEL), BF16)],
        scratch_shapes=[pltpu.VMEM((D_MODEL, P_ALL), BF16), pltpu.SemaphoreType.DMA],
        compiler_params=_params(),
        name="inproj",
    )(x, g1, wp)


GLA_CHUNKS_PER_STEP = 16
GLA_ROWS = GLA_CHUNK * GLA_CHUNKS_PER_STEP


def _gla_masks():
    lane = lax.broadcasted_iota(jnp.int32, (1, GLA_QK), 1)
    return [(lane >= h * GLA_DK) & (lane < (h + 1) * GLA_DK) for h in range(GLA_HEADS)]


def _log_sigmoid(x):
    return jnp.minimum(x, 0.0) - jnp.log(1.0 + jnp.exp(-jnp.abs(x)))


def _sigmoid(x):
    return 1.0 / (1.0 + jnp.exp(-x))


def _head_cols(h):
    return slice(h * GLA_DV, (h + 1) * GLA_DV)


GLA_GROUP = 256


def _gla_step_constants():
    ri = lax.broadcasted_iota(jnp.int32, (GLA_GROUP, GLA_GROUP), 0)
    ci = lax.broadcasted_iota(jnp.int32, (GLA_GROUP, GLA_GROUP), 1)
    shift = GLA_CHUNK.bit_length() - 1
    same = lax.shift_right_logical(ri, shift) == lax.shift_right_logical(ci, shift)
    return same & (ri >= ci), same & (ri <= ci), _gla_masks()


def _by_group(fn, *arrays):
    outs = [fn(*[a[g * GLA_GROUP:(g + 1) * GLA_GROUP] for a in arrays]) for g in range(GLA_ROWS // GLA_GROUP)]
    if isinstance(outs[0], tuple):
        return tuple(jnp.concatenate(parts, axis=0) for parts in zip(*outs))
    return jnp.concatenate(outs, axis=0)


def _per_chunk(x):
    return x.reshape(GLA_CHUNKS_PER_STEP, GLA_CHUNK, x.shape[-1])


def _chunk_rows_of(x, c):
    return x[c * GLA_CHUNK:(c + 1) * GLA_CHUNK]


def _stack_masked(x, masks):
    return jnp.concatenate([jnp.where(m, x, 0.0) for m in masks], axis=0)


def _stack_head_cols(x):
    return jnp.concatenate([x[:, _head_cols(h)] for h in range(GLA_HEADS)], axis=0)


def _diag_blocks(full, masks):
    out = jnp.where(masks[0], full[:GLA_DV], 0.0)
    for h in range(1, GLA_HEADS):
        out = out + jnp.where(masks[h], full[h * GLA_DV:(h + 1) * GLA_DV], 0.0)
    return out


def _row_blocks_masked(full, masks):
    out = jnp.where(masks[0], full[:GLA_CHUNK], 0.0)
    for h in range(1, GLA_HEADS):
        out = out + jnp.where(masks[h], full[h * GLA_CHUNK:(h + 1) * GLA_CHUNK], 0.0)
    return out


def _gla_step_common(q, k, glow_b, w2, bg, tri):
    gpre = _dot(glow_b, w2) + bg
    glog = _log_sigmoid(gpre) / GLA_TAU
    b = _by_group(lambda g: _dot_exact_lhs(tri, g), glog)
    bl = jnp.sum(_per_chunk(glog), axis=1, keepdims=True)
    eb = jnp.exp(b)
    enb = jnp.exp(-b)
    eke = jnp.exp(jnp.broadcast_to(bl, (GLA_CHUNKS_PER_STEP, GLA_CHUNK, GLA_QK)).reshape(GLA_ROWS, GLA_QK) - b)
    return gpre, eb, enb, eke, jnp.exp(bl), (q * QK_SCALE) * eb, k * enb, k * eke


def gla_fwd(proj, w2p, bg, gn):
    T = proj.shape[0]
    n_steps = T // GLA_ROWS
    n_chunks = T // GLA_CHUNK

    def body(proj_ref, w2_ref, bg_ref, gn_ref, oa_ref, opre_ref, sprev_ref, st_ref):
        @pl.when(pl.program_id(0) == 0)
        def _():
            st_ref[...] = jnp.zeros_like(st_ref)

        causal, _, masks = _gla_step_constants()
        q = proj_ref[:, P_GQ:P_GQ + GLA_QK]
        k = proj_ref[:, P_GK:P_GK + GLA_QK]
        v = proj_ref[:, P_GV:P_GV + GLA_WIDTH]
        r = proj_ref[:, P_GR:P_GR + GLA_WIDTH]
        glow = proj_ref[:, P_GLOW:P_GLOW + LANE].astype(BF16)
        _, _, _, _, ebl, qd, ki, ke = _gla_step_common(q, k, glow, w2_ref[...].astype(BF16), bg_ref[...], causal.astype(BF16))
        ki_b = ki.astype(BF16)
        v_b = v.astype(BF16)
        o_heads = []
        for h in range(GLA_HEADS):
            def intra(qd_g, ki_g, v_g):
                att = jnp.where(causal, _dot_nt(qd_g, ki_g), 0.0)
                return _dot(att.astype(BF16), v_g)

            o_heads.append(_by_group(intra, jnp.where(masks[h], qd, 0.0).astype(BF16), ki_b, v_b[:, _head_cols(h)]))
        st = st_ref[...]
        states = []
        for c in range(GLA_CHUNKS_PER_STEP):
            states.append(st)
            sprev_ref[c] = st
            inc = _diag_blocks(_dot_tn(_chunk_rows_of(v_b, c), _chunk_rows_of(ke, c).astype(BF16)), masks)
            st = st * ebl[c] + inc
        st_ref[...] = st
        inter = []
        for c in range(GLA_CHUNKS_PER_STEP):
            qd_c = _stack_masked(_chunk_rows_of(qd, c), masks).astype(BF16)
            got = _dot_nt(qd_c, states[c].astype(BF16))
            inter.append(jnp.concatenate([got[h * GLA_CHUNK:(h + 1) * GLA_CHUNK] for h in range(GLA_HEADS)], axis=1))
        o = jnp.concatenate(o_heads, axis=1) + jnp.concatenate(inter, axis=0)
        opre_ref[...] = o
        on = jnp.concatenate([o[:, _head_cols(h)] * _rstd(o[:, _head_cols(h)]) for h in range(GLA_HEADS)], axis=1)
        oa_ref[...] = ((on * gn_ref[...]) * (r * _sigmoid(r))).astype(BF16)

    return pl.pallas_call(
        body,
        grid=(n_steps,),
        in_specs=[
            pl.BlockSpec((GLA_ROWS, P_A), lambda i: (i, 0)),
            pl.BlockSpec((LANE, GLA_QK), lambda i: (0, 0)),
            pl.BlockSpec((1, GLA_QK), lambda i: (0, 0)),
            pl.BlockSpec((1, GLA_WIDTH), lambda i: (0, 0)),
        ],
        out_specs=[
            pl.BlockSpec((GLA_ROWS, GLA_WIDTH), lambda i: (i, 0)),
            pl.BlockSpec((GLA_ROWS, GLA_WIDTH), lambda i: (i, 0)),
            pl.BlockSpec((GLA_CHUNKS_PER_STEP, GLA_DV, GLA_QK), lambda i: (i, 0, 0)),
        ],
        out_shape=[
            jax.ShapeDtypeStruct((T, GLA_WIDTH), BF16),
            jax.ShapeDtypeStruct((T, GLA_WIDTH), F32),
            jax.ShapeDtypeStruct((n_chunks, GLA_DV, GLA_QK), F32),
        ],
        scratch_shapes=[pltpu.VMEM((GLA_DV, GLA_QK), F32)],
        compiler_params=_params(),
        name="gla_fwd",
    )(proj, w2p, bg, gn)


def gla_bwd(proj, w2p, bg, gn, opre, sprev, dmixed, exchange=None):
    T = proj.shape[0]
    n_steps = T // GLA_ROWS

    def body(*refs):
        refs = _host_exchange(exchange, refs, 7, 4, pl.program_id(0), n_steps)
        proj_ref, w2_ref, bg_ref, gn_ref, opre_ref, sprev_ref, doa_ref, da_ref, dw2_ref, dbg_ref, dgn_ref, dst_ref = refs

        @pl.when(pl.program_id(0) == 0)
        def _():
            dst_ref[...] = jnp.zeros_like(dst_ref)
            dw2_ref[...] = jnp.zeros_like(dw2_ref)
            dbg_ref[...] = jnp.zeros_like(dbg_ref)
            dgn_ref[...] = jnp.zeros_like(dgn_ref)

        causal, causal_t, masks = _gla_step_constants()
        w2 = w2_ref[...].astype(BF16)
        gn = gn_ref[...]
        q = proj_ref[:, P_GQ:P_GQ + GLA_QK]
        k = proj_ref[:, P_GK:P_GK + GLA_QK]
        v_b = proj_ref[:, P_GV:P_GV + GLA_WIDTH].astype(BF16)
        r = proj_ref[:, P_GR:P_GR + GLA_WIDTH]
        glow = proj_ref[:, P_GLOW:P_GLOW + LANE].astype(BF16)
        o = opre_ref[...]
        doa = doa_ref[...]
        gpre, eb, enb, eke, ebl, qd, ki, ke = _gla_step_common(q, k, glow, w2, bg_ref[...], causal.astype(BF16))
        sig = _sigmoid(r)
        rs = jnp.concatenate([jnp.broadcast_to(_rstd(o[:, _head_cols(h)]), (GLA_ROWS, GLA_DV)) for h in range(GLA_HEADS)], axis=1)
        on = o * rs
        d_ong = doa * (r * sig)
        dr = doa * (on * gn) * (sig * (1.0 + r * (1.0 - sig)))
        dgn_ref[...] += jnp.sum(d_ong * on, axis=0, keepdims=True)
        d_on = d_ong * gn
        t = d_on * on
        mean_t = jnp.concatenate([jnp.broadcast_to(jnp.mean(t[:, _head_cols(h)], axis=-1, keepdims=True), (GLA_ROWS, GLA_DV))
                                  for h in range(GLA_HEADS)], axis=1)
        do_b = (rs * (d_on - on * mean_t)).astype(BF16)
        ki_b = ki.astype(BF16)
        ke_b = ke.astype(BF16)
        dqd = jnp.zeros_like(qd)
        dki = jnp.zeros_like(qd)
        dv_heads = []
        for h in range(GLA_HEADS):
            qd_h = jnp.where(masks[h], qd, 0.0).astype(BF16)
            do_h = do_b[:, _head_cols(h)]

            def intra(qd_g, ki_g, v_g, do_g):
                att = jnp.where(causal, _dot_nt(qd_g, ki_g), 0.0).astype(BF16)
                d_att = jnp.where(causal, _dot_nt(do_g, v_g), 0.0).astype(BF16)
                return _dot_tn(att, do_g), _dot(d_att, ki_g), _dot_tn(d_att, qd_g)

            dv_h, dqd_h, dki_h = _by_group(intra, qd_h, ki_b, v_b[:, _head_cols(h)], do_h)
            dv_heads.append(dv_h)
            dqd = dqd + jnp.where(masks[h], dqd_h, 0.0)
            dki = dki + dki_h
        states = [sprev_ref[c] for c in range(GLA_CHUNKS_PER_STEP)]
        dqd_inter, dst_adds = [], []
        for c in range(GLA_CHUNKS_PER_STEP):
            do_c = _stack_head_cols(_chunk_rows_of(do_b, c))
            dqd_inter.append(_row_blocks_masked(_dot(do_c, states[c].astype(BF16)), masks))
            dst_adds.append(_diag_blocks(_dot_tn(_chunk_rows_of(do_b, c), _chunk_rows_of(qd, c).astype(BF16)), masks))
        dst = dst_ref[...]
        dsts, debls = [None] * GLA_CHUNKS_PER_STEP, [None] * GLA_CHUNKS_PER_STEP
        for c in reversed(range(GLA_CHUNKS_PER_STEP)):
            dsts[c] = dst
            debls[c] = jnp.sum(dst * states[c], axis=0, keepdims=True)
            dst = dst * ebl[c] + dst_adds[c]
        dst_ref[...] = dst
        dv_inter, dke = [], []
        for c in range(GLA_CHUNKS_PER_STEP):
            dst_b = dsts[c].astype(BF16)
            got = _dot_nt(_stack_masked(_chunk_rows_of(ke, c), masks).astype(BF16), dst_b)
            dv_inter.append(jnp.concatenate([got[h * GLA_CHUNK:(h + 1) * GLA_CHUNK] for h in range(GLA_HEADS)], axis=1))
            dke.append(_row_blocks_masked(_dot(_stack_head_cols(_chunk_rows_of(v_b, c)), dst_b), masks))
        dqd = dqd + jnp.concatenate(dqd_inter, axis=0)
        dke = jnp.concatenate(dke, axis=0)
        dv = jnp.concatenate(dv_heads, axis=1) + jnp.concatenate(dv_inter, axis=0)
        dkk = dke * ke
        dbl = jnp.sum(_per_chunk(dkk), axis=1, keepdims=True) + jnp.stack(debls) * ebl
        last_row = lax.broadcasted_iota(jnp.int32, (GLA_CHUNKS_PER_STEP, GLA_CHUNK, GLA_QK), 1) == GLA_CHUNK - 1
        db = dqd * qd - dki * ki - dkk + jnp.where(last_row, dbl, 0.0).reshape(GLA_ROWS, GLA_QK)
        tri_t = causal_t.astype(BF16)
        dglog = _by_group(lambda g: _dot_exact_lhs(tri_t, g), db)
        dgpre = (dglog / GLA_TAU) * _sigmoid(-gpre)
        dgpre_b = dgpre.astype(BF16)
        da_ref[...] = jnp.concatenate(
            [dqd * eb * QK_SCALE, dki * enb + dke * eke, dv, dr, _dot_nt(dgpre_b, w2)], axis=1).astype(BF16)
        dw2_ref[...] += _dot_tn(glow, dgpre_b)
        dbg_ref[...] += jnp.sum(dgpre, axis=0, keepdims=True)

    rev = lambda i: (n_steps - 1 - i, 0)
    return _hosted_call(
        exchange,
        body,
        grid=(n_steps,),
        in_specs=[
            pl.BlockSpec((GLA_ROWS, P_A), rev),
            pl.BlockSpec((LANE, GLA_QK), lambda i: (0, 0)),
            pl.BlockSpec((1, GLA_QK), lambda i: (0, 0)),
            pl.BlockSpec((1, GLA_WIDTH), lambda i: (0, 0)),
            pl.BlockSpec((GLA_ROWS, GLA_WIDTH), rev),
            pl.BlockSpec((GLA_CHUNKS_PER_STEP, GLA_DV, GLA_QK), lambda i: (n_steps - 1 - i, 0, 0)),
            pl.BlockSpec((GLA_ROWS, GLA_WIDTH), rev),
        ],
        out_specs=[
            pl.BlockSpec((GLA_ROWS, P_A), rev),
            pl.BlockSpec((LANE, GLA_QK), lambda i: (0, 0)),
            pl.BlockSpec((1, GLA_QK), lambda i: (0, 0)),
            pl.BlockSpec((1, GLA_WIDTH), lambda i: (0, 0)),
        ],
        out_shape=[
            jax.ShapeDtypeStruct((T, P_A), BF16),
            jax.ShapeDtypeStruct((LANE, GLA_QK), F32),
            jax.ShapeDtypeStruct((1, GLA_QK), F32),
            jax.ShapeDtypeStruct((1, GLA_WIDTH), F32),
        ],
        scratch_shapes=[pltpu.VMEM((GLA_DV, GLA_QK), F32)],
        compiler_params=_params(),
        name="gla_bwd",
        args=(proj, w2p, bg, gn, opre, sprev, dmixed),
    )


def _t5_bucket(dist):
    max_exact = REL_BUCKETS // 2
    n = np.maximum(dist, 0)
    large = max_exact + (np.log(np.maximum(n, 1) / max_exact) / math.log(REL_MAX_DIST / max_exact)
                         * (REL_BUCKETS - max_exact)).astype(np.int32)
    large = np.minimum(large, REL_BUCKETS - 1)
    return np.where(n < max_exact, n, large).astype(np.int32)


SUBLANES = 8


def _bucket_rows():
    steps = DSA_BLOCK - np.arange(2 * DSA_BLOCK)
    in_band = (steps >= 0) & (steps <= DSA_SPAN)
    rows = np.stack([np.where(in_band, _t5_bucket(steps * d), -1) for d in DSA_DILATIONS]).astype(np.int32)
    return np.broadcast_to(rows[:, None, :], (len(DSA_DILATIONS), SUBLANES, 2 * DSA_BLOCK)).copy()


def bias_tables(rel_bias):
    ids = jnp.asarray(_bucket_rows())
    nd = len(DSA_DILATIONS)

    def body(rel_ref, ids_ref, tab_ref):
        h = pl.program_id(1)
        idt = ids_ref[0]
        row = jnp.where(idt < 0, NEG, 0.0).astype(F32)
        for b in range(REL_BUCKETS):
            row = jnp.where(idt == b, rel_ref[b, h], row)
        full = jnp.broadcast_to(row[0:1], (DSA_BLOCK, 2 * DSA_BLOCK))
        tab_ref[0, 0] = pltpu.roll(full, 0, 1, stride=1, stride_axis=0)

    return pl.pallas_call(
        body,
        grid=(nd, DSA_HEADS),
        in_specs=[pl.BlockSpec(memory_space=pltpu.SMEM), pl.BlockSpec((1, SUBLANES, 2 * DSA_BLOCK), lambda d, h: (d, 0, 0))],
        out_specs=pl.BlockSpec((1, 1, DSA_BLOCK, 2 * DSA_BLOCK), lambda d, h: (d, h, 0, 0)),
        out_shape=jax.ShapeDtypeStruct((nd, DSA_HEADS, DSA_BLOCK, 2 * DSA_BLOCK), F32),
        compiler_params=_params(("arbitrary", "arbitrary")),
        name="bias_tables",
    )(rel_bias, ids)


def _bucket_ids():
    L = DSA_BLOCK
    steps = L + np.arange(L)[:, None] - np.arange(2 * L)[None, :]
    in_band = (steps >= 0) & (steps <= DSA_SPAN)
    return np.stack([np.where(in_band, _t5_bucket(steps * d), -1) for d in DSA_DILATIONS]).astype(np.int32)


def bias_tables_bwd(dtab):
    ids = jnp.asarray(_bucket_ids())
    nd = len(DSA_DILATIONS)

    def body(dtab_ref, ids_ref, drel_ref):
        @pl.when(pl.program_id(0) == 0)
        def _():
            for b in range(REL_BUCKETS):
                for h in range(DSA_HEADS):
                    drel_ref[b, h] = 0.0

        idt = ids_ref[0]
        for b in range(REL_BUCKETS):
            in_bucket = idt == b
            for h in range(DSA_HEADS):
                drel_ref[b, h] += jnp.sum(jnp.where(in_bucket, dtab_ref[0, h], 0.0))

    return pl.pallas_call(
        body,
        grid=(nd,),
        in_specs=[pl.BlockSpec((1, DSA_HEADS, DSA_BLOCK, 2 * DSA_BLOCK), lambda d: (d, 0, 0, 0)),
                  pl.BlockSpec((1, DSA_BLOCK, 2 * DSA_BLOCK), lambda d: (d, 0, 0))],
        out_specs=pl.BlockSpec(memory_space=pltpu.SMEM),
        out_shape=jax.ShapeDtypeStruct((REL_BUCKETS, DSA_HEADS), F32),
        compiler_params=_params(),
        name="bias_tables_bwd",
    )(dtab, ids)


DSA_PAIRS = DSA_HEADS // 2
DSA_UNROLL = 16
DSA_COMBINE_ROWS = 256


def _dsa_units(d):
    return d, DSA_SUPER // (DSA_BLOCK * d)


def _dsa_specs(T):
    nsb = T // DSA_SUPER
    qcol, kcol, vcol = P_DQ // LANE, P_DK // LANE, P_DV // LANE
    return nsb, qcol, kcol, vcol


def _head_lane_mask():
    return lax.broadcasted_iota(jnp.int32, (1, LANE), 1) < DSA_DH


def _fill_tile_variants(tab_ref, variants):
    col = lax.broadcasted_iota(jnp.int32, (2 * DSA_BLOCK, 2 * DSA_BLOCK), 1)
    for di in range(len(DSA_DILATIONS)):
        tile = tab_ref[di, 0]
        variants[di, 0] = tile
        variants[di, 1] = jnp.where(col < DSA_BLOCK, NEG, tile)


def _tile_variants_scratch():
    return pltpu.VMEM((len(DSA_DILATIONS), 2, 2 * DSA_BLOCK, 2 * DSA_BLOCK), F32)


def _pair_tiles(tab):
    return tab.reshape(len(DSA_DILATIONS), DSA_PAIRS, 2 * DSA_BLOCK, 2 * DSA_BLOCK)


def _stack_heads(t, head0):
    return jnp.concatenate([jnp.where(head0, t, 0.0), jnp.where(head0, 0.0, t)], axis=0)


def dsa_fwd(proj, tab, exchange=None):
    T = proj.shape[0]
    nsb, qcol, kcol, vcol = _dsa_specs(T)
    S = DSA_SUPER

    def body(*refs):
        refs = _host_exchange(exchange, refs, 6, 2, pl.program_id(0) * nsb + pl.program_id(1), DSA_PAIRS * nsb)
        q_ref, kp_ref, kc_ref, vp_ref, vc_ref, tab_ref, out_ref, lse_ref, kk, vv, ob, lb, tiles = refs
        sb = pl.program_id(1)
        kk[0:S, :] = kp_ref[...]
        kk[S:2 * S, :] = kc_ref[...]
        vv[0:S, :] = vp_ref[...]
        vv[S:2 * S, :] = vc_ref[...]
        head0 = _head_lane_mask()
        pl.when(sb == 0)(functools.partial(_fill_tile_variants, tab_ref, tiles))

        for di, d in enumerate(DSA_DILATIONS):
            n_res, n_blk = _dsa_units(d)

            def unit(u, carry, di=di, d=d, n_blk=n_blk):
                r = u // n_blk
                c = u % n_blk
                q0 = r + d * DSA_BLOCK * c
                qrows = pl.ds(q0, DSA_BLOCK, stride=d) if d > 1 else pl.ds(q0, DSA_BLOCK)
                krows = pl.ds(S + q0 - d * DSA_BLOCK, 2 * DSA_BLOCK, stride=d) if d > 1 else pl.ds(S + q0 - DSA_BLOCK, 2 * DSA_BLOCK)
                q2 = q_ref[qrows, :] * QK_SCALE
                k2 = kk[krows, :].astype(BF16)
                v2 = vv[krows, :].astype(BF16)
                qs = _stack_heads(q2, head0).astype(BF16)
                s = _dot_nt(qs, k2) + tiles[di, ((sb == 0) & (c == 0)).astype(jnp.int32)]
                m = jnp.max(s, axis=-1, keepdims=True)
                p = jnp.exp(s - m)
                den = jnp.sum(p, axis=-1, keepdims=True)
                o = _dot(p.astype(BF16), v2) / den
                l = jnp.broadcast_to(m + jnp.log(den), (2 * DSA_BLOCK, LANE))
                ob[di, qrows, :] = jnp.where(head0, o[:DSA_BLOCK], o[DSA_BLOCK:])
                lb[di, qrows, :] = jnp.where(head0, l[:DSA_BLOCK], l[DSA_BLOCK:])
                return carry

            lax.fori_loop(0, n_res * n_blk, unit, 0, unroll=DSA_UNROLL)

        def combine(i, carry):
            rows = pl.ds(pl.multiple_of(i * DSA_COMBINE_ROWS, DSA_COMBINE_ROWS), DSA_COMBINE_ROWS)
            l0, l1, l2 = lb[0, rows, :], lb[1, rows, :], lb[2, rows, :]
            mx = jnp.maximum(jnp.maximum(l0, l1), l2)
            e0, e1, e2 = jnp.exp(l0 - mx), jnp.exp(l1 - mx), jnp.exp(l2 - mx)
            den = e0 + e1 + e2
            out_ref[rows, :] = (e0 * ob[0, rows, :] + e1 * ob[1, rows, :] + e2 * ob[2, rows, :]) / den
            lse_ref[rows, :] = mx + jnp.log(den)
            return carry

        lax.fori_loop(0, S // DSA_COMBINE_ROWS, combine, 0)

    prev = lambda col: (lambda hp, sb: (jnp.maximum(sb - 1, 0), col + hp))
    cur = lambda col: (lambda hp, sb: (sb, col + hp))
    blk = lambda f: pl.BlockSpec((S, LANE), f)
    return _hosted_call(
        exchange,
        body,
        grid=(DSA_PAIRS, nsb),
        in_specs=[blk(cur(qcol)), blk(prev(kcol)), blk(cur(kcol)), blk(prev(vcol)), blk(cur(vcol)),
                  pl.BlockSpec((len(DSA_DILATIONS), 1, 2 * DSA_BLOCK, 2 * DSA_BLOCK), lambda hp, sb: (0, hp, 0, 0))],
        out_specs=[blk(lambda hp, sb: (sb, hp)), blk(lambda hp, sb: (sb, hp))],
        out_shape=[jax.ShapeDtypeStruct((T, DSA_WIDTH), F32), jax.ShapeDtypeStruct((T, DSA_WIDTH), F32)],
        scratch_shapes=[pltpu.VMEM((2 * S, LANE), F32), pltpu.VMEM((2 * S, LANE), F32),
                        pltpu.VMEM((len(DSA_DILATIONS), S, LANE), F32), pltpu.VMEM((len(DSA_DILATIONS), S, LANE), F32),
                        _tile_variants_scratch()],
        compiler_params=_params(("arbitrary", "arbitrary")),
        name="dsa_fwd",
        args=(proj, proj, proj, proj, proj, _pair_tiles(tab)),
    )


def dsa_bwd(proj, tab, ob_out, lse, dmixed, exchange=None):
    T = proj.shape[0]
    nsb, qcol, kcol, vcol = _dsa_specs(T)
    S = DSA_SUPER
    nd = len(DSA_DILATIONS)
    ocol = GLA_WIDTH // LANE

    def body(*refs):
        refs = _host_exchange(exchange, refs, 9, 4, pl.program_id(0) * nsb + pl.program_id(1), DSA_PAIRS * nsb)
        (q_ref, kp_ref, kc_ref, vp_ref, vc_ref, tab_ref, o_ref, lse_ref, do_ref,
         dq_ref, dk_ref, dv_ref, dtab_ref, kk, vv, dqa, dkk, dvv, tiles) = refs
        j = pl.program_id(1)
        sb = nsb - 1 - j
        kk[0:S, :] = kp_ref[...]
        kk[S:2 * S, :] = kc_ref[...]
        vv[0:S, :] = vp_ref[...]
        vv[S:2 * S, :] = vc_ref[...]
        head0 = _head_lane_mask()
        pl.when(j == 0)(functools.partial(_fill_tile_variants, tab_ref, tiles))

        @pl.when(j == 0)
        def _():
            dtab_ref[...] = jnp.zeros_like(dtab_ref)
            dkk[S:2 * S, :] = jnp.zeros((S, LANE), F32)
            dvv[S:2 * S, :] = jnp.zeros((S, LANE), F32)

        @pl.when(j > 0)
        def _():
            dkk[S:2 * S, :] = dkk[0:S, :]
            dvv[S:2 * S, :] = dvv[0:S, :]

        dkk[0:S, :] = jnp.zeros((S, LANE), F32)
        dvv[0:S, :] = jnp.zeros((S, LANE), F32)
        dqa[...] = jnp.zeros_like(dqa)

        for di, d in enumerate(DSA_DILATIONS):
            n_res, n_blk = _dsa_units(d)

            def unit(u, carry, di=di, d=d, n_blk=n_blk):
                r = u // n_blk
                c = u % n_blk
                q0 = r + d * DSA_BLOCK * c
                qrows = pl.ds(q0, DSA_BLOCK, stride=d) if d > 1 else pl.ds(q0, DSA_BLOCK)
                krows = pl.ds(S + q0 - d * DSA_BLOCK, 2 * DSA_BLOCK, stride=d) if d > 1 else pl.ds(S + q0 - DSA_BLOCK, 2 * DSA_BLOCK)
                q2 = q_ref[qrows, :] * QK_SCALE
                k2 = kk[krows, :].astype(BF16)
                v2 = vv[krows, :].astype(BF16)
                do2 = do_ref[qrows, :]
                o2 = o_ref[qrows, :]
                l2 = lse_ref[qrows, :]
                qs = _stack_heads(q2, head0).astype(BF16)
                dos = _stack_heads(do2, head0)
                dos_b = dos.astype(BF16)
                delta = jnp.sum(dos * jnp.concatenate([o2, o2], axis=0), axis=-1, keepdims=True)
                lse = jnp.concatenate([jnp.max(jnp.where(head0, l2, -jnp.inf), axis=-1, keepdims=True),
                                       jnp.max(jnp.where(head0, -jnp.inf, l2), axis=-1, keepdims=True)], axis=0)
                s = _dot_nt(qs, k2) + tiles[di, ((sb == 0) & (c == 0)).astype(jnp.int32)]
                p = jnp.exp(s - lse)
                ds = p * (_dot_nt(dos_b, v2) - delta)
                dtab_ref[di, 0] += ds
                ds_b = ds.astype(BF16)
                dq = _dot(ds_b, k2)
                dqa[qrows, :] += jnp.where(head0, dq[:DSA_BLOCK], dq[DSA_BLOCK:]) * QK_SCALE
                dkk[krows, :] += _dot_tn(ds_b, qs)
                dvv[krows, :] += _dot_tn(p.astype(BF16), dos_b)
                return carry

            lax.fori_loop(0, n_res * n_blk, unit, 0, unroll=DSA_UNROLL)

        dq_ref[...] = dqa[...].astype(BF16)
        dk_ref[...] = dkk[S:2 * S, :].astype(BF16)
        dv_ref[...] = dvv[S:2 * S, :].astype(BF16)

    prev = lambda col: (lambda hp, j: (jnp.maximum(nsb - 2 - j, 0), col + hp))
    cur = lambda col: (lambda hp, j: (nsb - 1 - j, col + hp))
    blk = lambda f: pl.BlockSpec((S, LANE), f)
    out_blk = blk(lambda hp, j: (nsb - 1 - j, hp))
    tab_blk = pl.BlockSpec((nd, 1, 2 * DSA_BLOCK, 2 * DSA_BLOCK), lambda hp, j: (0, hp, 0, 0))
    dq, dk, dv, dtab, *carried = _hosted_call(
        exchange,
        body,
        grid=(DSA_PAIRS, nsb),
        in_specs=[blk(cur(qcol)), blk(prev(kcol)), blk(cur(kcol)), blk(prev(vcol)), blk(cur(vcol)), tab_blk,
                  out_blk, out_blk, blk(cur(ocol))],
        out_specs=[out_blk, out_blk, out_blk, tab_blk],
        out_shape=[jax.ShapeDtypeStruct((T, DSA_WIDTH), BF16)] * 3
        + [jax.ShapeDtypeStruct((nd, DSA_PAIRS, 2 * DSA_BLOCK, 2 * DSA_BLOCK), F32)],
        scratch_shapes=[pltpu.VMEM((2 * S, LANE), F32), pltpu.VMEM((2 * S, LANE), F32), pltpu.VMEM((S, LANE), F32),
                        pltpu.VMEM((2 * S, LANE), F32), pltpu.VMEM((2 * S, LANE), F32), _tile_variants_scratch()],
        compiler_params=_params(("arbitrary", "arbitrary")),
        name="dsa_bwd",
        args=(proj, proj, proj, proj, proj, _pair_tiles(tab), ob_out, lse, dmixed),
    )
    return (dq, dk, dv, dtab.reshape(nd, DSA_HEADS, DSA_BLOCK, 2 * DSA_BLOCK), *carried)


FF_BLOCKS = 4
FF_BLOCK = D_FF // FF_BLOCKS


def post_fused(x, oa, ob, tgt, g2, gf, wout, wff1, wff2):
    T = x.shape[0]
    tm = 256
    inv_d = 1.0 / D_MODEL

    def body(x_ref, oa_ref, ob_ref, tgt_ref, g2_ref, gf_ref, wout_hbm, wff1_hbm, wff2_hbm,
             mixed_ref, nm_ref, a_ref, dpre_ref, dh2_ref, dh1_ref, dmixed_ref, loss_ref, dgf_ref, dg2_ref,
             wout_v, wff1_v, wff2_v, sems):
        @pl.when(pl.program_id(0) == 0)
        def _():
            cps = [pltpu.make_async_copy(s, d, sems.at[i])
                   for i, (s, d) in enumerate([(wout_hbm, wout_v), (wff1_hbm, wff1_v), (wff2_hbm, wff2_v)])]
            for cp in cps:
                cp.start()
            for cp in cps:
                cp.wait()
            loss_ref[...] = jnp.zeros_like(loss_ref)
            dgf_ref[...] = jnp.zeros_like(dgf_ref)
            dg2_ref[...] = jnp.zeros_like(dg2_ref)

        mixed = jnp.concatenate([oa_ref[...], ob_ref[...].astype(BF16)], axis=1)
        mixed_ref[...] = mixed
        h1 = x_ref[...] + _dot(mixed, wout_v[...])
        rs1 = _rstd(h1)
        hn1 = h1 * rs1
        g2 = g2_ref[...]
        nm = (hn1 * g2).astype(BF16)
        nm_ref[...] = nm
        relu = []
        mlp = jnp.zeros((tm, D_MODEL), F32)
        for j in range(FF_BLOCKS):
            cols = slice(j * FF_BLOCK, (j + 1) * FF_BLOCK)
            r_j = jnp.maximum(_dot(nm, wff1_v[j]), 0.0)
            a_j = (r_j * r_j).astype(BF16)
            a_ref[:, cols] = a_j
            relu.append(r_j)
            mlp = mlp + _dot(a_j, wff2_v[cols, :])
        h2 = h1 + mlp
        rsf = _rstd(h2)
        hnf = h2 * rsf
        gf = gf_ref[...]
        diff = hnf * gf - tgt_ref[...]
        loss_ref[...] += 0.5 * jnp.sum(jnp.sum(diff * diff, axis=-1, keepdims=True) * inv_d, axis=0, keepdims=True)
        dy = diff * inv_d
        dgf_ref[...] += jnp.sum(dy * hnf, axis=0, keepdims=True)
        dhnf = dy * gf
        dh2 = rsf * (dhnf - hnf * jnp.mean(dhnf * hnf, axis=-1, keepdims=True))
        dh2_b = dh2.astype(BF16)
        dh2_ref[...] = dh2_b
        dnm = jnp.zeros((tm, D_MODEL), F32)
        for j in range(FF_BLOCKS):
            cols = slice(j * FF_BLOCK, (j + 1) * FF_BLOCK)
            dpre_j = (_dot_nt(dh2_b, wff2_v[cols, :]) * (2.0 * relu[j])).astype(BF16)
            dpre_ref[:, cols] = dpre_j
            dnm = dnm + _dot_nt(dpre_j, wff1_v[j])
        dg2_ref[...] += jnp.sum(dnm * hn1, axis=0, keepdims=True)
        dhn1 = dnm * g2
        dh1 = dh2 + rs1 * (dhn1 - hn1 * jnp.mean(dhn1 * hn1, axis=-1, keepdims=True))
        dh1_ref[...] = dh1
        dmixed_ref[...] = _dot_nt(dh1.astype(BF16), wout_v[...])

    row = lambda w: pl.BlockSpec((tm, w), lambda i: (i, 0))
    vec = lambda w: pl.BlockSpec((1, w), lambda i: (0, 0))
    return pl.pallas_call(
        body,
        grid=(T // tm,),
        in_specs=[row(D_MODEL), row(GLA_WIDTH), row(DSA_WIDTH), row(D_MODEL), vec(D_MODEL), vec(D_MODEL), ANY, ANY, ANY],
        out_specs=[row(D_MODEL), row(D_MODEL), row(D_FF), row(D_FF), row(D_MODEL), row(D_MODEL), row(D_MODEL),
                   vec(1), vec(D_MODEL), vec(D_MODEL)],
        out_shape=[
            jax.ShapeDtypeStruct((T, D_MODEL), BF16),
            jax.ShapeDtypeStruct((T, D_MODEL), BF16),
            jax.ShapeDtypeStruct((T, D_FF), BF16),
            jax.ShapeDtypeStruct((T, D_FF), BF16),
            jax.ShapeDtypeStruct((T, D_MODEL), BF16),
            jax.ShapeDtypeStruct((T, D_MODEL), F32),
            jax.ShapeDtypeStruct((T, D_MODEL), F32),
            jax.ShapeDtypeStruct((1, 1), F32),
            jax.ShapeDtypeStruct((1, D_MODEL), F32),
            jax.ShapeDtypeStruct((1, D_MODEL), F32),
        ],
        scratch_shapes=[pltpu.VMEM((D_MODEL, D_MODEL), BF16), pltpu.VMEM((FF_BLOCKS, D_MODEL, FF_BLOCK), BF16),
                        pltpu.VMEM((D_FF, D_MODEL), BF16), pltpu.SemaphoreType.DMA((3,))],
        compiler_params=_params(),
        name="post_fused",
    )(x, oa, ob, tgt, g2, gf, wout, wff1, wff2)


WGRAD_TOKENS = 2048


def wgrad(a, b, name, bm=None, bn=None, col_blocked=False):
    T, M = a.shape
    N = b.shape[1]
    bm = M if bm is None else bm
    bn = N if bn is None else bn
    tk = min(WGRAD_TOKENS, T)
    n_k = T // tk

    def body(a_ref, b_ref, o_ref, acc_ref):
        part = _dot_tn(a_ref[...].astype(BF16), b_ref[...].astype(BF16))
        out = o_ref.at[0] if col_blocked else o_ref
        k = pl.program_id(2)
        if n_k == 1:
            out[...] = part.astype(BF16)
            return

        @pl.when(k == 0)
        def _():
            acc_ref[...] = part

        @pl.when((k > 0) & (k < n_k - 1))
        def _():
            acc_ref[...] += part

        @pl.when(k == n_k - 1)
        def _():
            out[...] = (acc_ref[...] + part).astype(BF16)

    if col_blocked:
        assert bm == M
        out_spec = pl.BlockSpec((1, M, bn), lambda i, j, k: (j, 0, 0))
        out_shape = jax.ShapeDtypeStruct((N // bn, M, bn), BF16)
    else:
        out_spec = pl.BlockSpec((bm, bn), lambda i, j, k: (i, j))
        out_shape = jax.ShapeDtypeStruct((M, N), BF16)
    return pl.pallas_call(
        body,
        grid=(M // bm, N // bn, n_k),
        in_specs=[pl.BlockSpec((tk, bm), lambda i, j, k: (k, i)), pl.BlockSpec((tk, bn), lambda i, j, k: (k, j))],
        out_specs=out_spec,
        out_shape=out_shape,
        scratch_shapes=[pltpu.VMEM((bm, bn), F32)],
        compiler_params=_params(("arbitrary", "arbitrary", "arbitrary")),
        name=name,
    )(a, b)


def wgrad_cat(a, bs, name):
    T, M = a.shape
    widths = [b.shape[1] for b in bs]
    starts = [sum(widths[:i]) for i in range(len(bs))]
    N = sum(widths)
    tk = min(WGRAD_TOKENS // 2, T)
    n_k = T // tk

    def body(a_ref, *rest):
        b_refs, o_ref, acc_ref = rest[:len(bs)], rest[len(bs)], rest[len(bs) + 1]
        k = pl.program_id(0)

        @pl.when(k == 0)
        def _():
            acc_ref[...] = jnp.zeros_like(acc_ref)

        a_t = a_ref[...]
        for b_ref, start, width in zip(b_refs, starts, widths):
            acc_ref[:, start:start + width] += _dot_tn(a_t, b_ref[...])

        @pl.when(k == n_k - 1)
        def _():
            o_ref[...] = acc_ref[...].astype(BF16)

    return pl.pallas_call(
        body,
        grid=(n_k,),
        in_specs=[pl.BlockSpec((tk, M), lambda k: (k, 0))] + [pl.BlockSpec((tk, w), lambda k: (k, 0)) for w in widths],
        out_specs=pl.BlockSpec((M, N), lambda k: (0, 0)),
        out_shape=jax.ShapeDtypeStruct((M, N), BF16),
        scratch_shapes=[pltpu.VMEM((M, N), F32)],
        compiler_params=_params(),
        name=name,
    )(a, *bs)


def dx_final(x, dh1, g1, da, dq, dk, dv, wp, exchange=None):
    T = x.shape[0]
    tm = 256

    def body(*refs):
        refs = _host_exchange(exchange, refs, 8, 2, pl.program_id(0), T // tm)
        x_ref, dh1_ref, g_ref, da_ref, dq_ref, dk_ref, dv_ref, w_hbm, dx_ref, dg_ref, w_vmem, sem = refs

        @pl.when(pl.program_id(0) == 0)
        def _():
            _load_once(w_hbm, w_vmem, sem)
            dg_ref[...] = jnp.zeros_like(dg_ref)

        dnx = (_dot_nt(da_ref[...], w_vmem[:, 0:P_A]) + _dot_nt(dq_ref[...], w_vmem[:, P_DQ:P_DQ + DSA_WIDTH])
               + _dot_nt(dk_ref[...], w_vmem[:, P_DK:P_DK + DSA_WIDTH]) + _dot_nt(dv_ref[...], w_vmem[:, P_DV:P_DV + DSA_WIDTH]))
        xf = x_ref[...]
        rs = _rstd(xf)
        hn = xf * rs
        dg_ref[...] += jnp.sum(dnx * hn, axis=0, keepdims=True)
        dhn = dnx * g_ref[...]
        dx_ref[...] = dh1_ref[...] + rs * (dhn - hn * jnp.mean(dhn * hn, axis=-1, keepdims=True))

    row = lambda w: pl.BlockSpec((tm, w), lambda i: (i, 0))
    vec = pl.BlockSpec((1, D_MODEL), lambda i: (0, 0))
    return _hosted_call(
        exchange,
        body,
        grid=(T // tm,),
        in_specs=[row(D_MODEL), row(D_MODEL), vec, row(P_A), row(DSA_WIDTH), row(DSA_WIDTH), row(DSA_WIDTH), ANY],
        out_specs=[row(D_MODEL), vec],
        out_shape=[jax.ShapeDtypeStruct((T, D_MODEL), F32), jax.ShapeDtypeStruct((1, D_MODEL), F32)],
        scratch_shapes=[pltpu.VMEM((D_MODEL, P_ALL), BF16), pltpu.SemaphoreType.DMA],
        compiler_params=_params(),
        name="dx_final",
        args=(x, dh1, g1, da, dq, dk, dv, wp),
    )


def adamw(w, g, m, v, name):
    R, C = w.shape
    br = 256 if R % 256 == 0 else R

    def body(w_ref, g_ref, m_ref, v_ref, d_ref, nm_ref, nv_ref):
        d_ref[...], nm_ref[...], nv_ref[...] = _adamw_math(w_ref[...], g_ref[...], m_ref[...], v_ref[...])

    spec = pl.BlockSpec((br, C), lambda i: (i, 0))
    return pl.pallas_call(
        body,
        grid=(R // br,),
        in_specs=[spec] * 4,
        out_specs=[spec] * 3,
        out_shape=[jax.ShapeDtypeStruct((R, C), F32)] * 3,
        compiler_params=_params(),
        name=name,
    )(w, g, m, v)


def _place():
    return lax.axis_index("x"), lax.axis_index("y"), lax.axis_index("c")


def _other_chips(x, y):
    return [(1 - x, y), (x, 1 - y), (1 - x, 1 - y)]


class Exchange:
    def __init__(self, kind, arrays):
        self.kind, self.arrays, self.n = kind, arrays, len(arrays)
        self.slots = 4 if kind == "gather" else 8

    def out_shapes(self):
        if self.kind == "gather":
            return [jax.ShapeDtypeStruct((4,) + s.shape, s.dtype) for s in self.arrays]
        return [jax.ShapeDtypeStruct((8,) + s.shape[1:], s.dtype) for s in self.arrays]

    def sems(self):
        return [pltpu.SemaphoreType.DMA((self.n, 19)), pltpu.SemaphoreType.DMA((self.n, 19))]

    def phases(self, ins, outs, send_sems, recv_sems):
        n, scatter = self.n, self.kind == "scatter"
        x, y, c = _place()
        me, sib = (x, y, c), (x, y, 1 - c)
        mine = 2 * x + y
        chips = _other_chips(x, y)
        own_pair = 18

        def region(a, slot, half):
            h = outs[a].shape[1] // 2
            return outs[a].at[slot, pl.ds(half * h, h)]

        def copy(a, k, slot, half, to, src=None):
            return pltpu.make_async_remote_copy(
                src_ref=region(a, slot, half) if src is None else src, dst_ref=region(a, slot, half),
                send_sem=send_sems.at[a, k], recv_sem=recv_sems.at[a, k], device_id=to, device_id_type=MESH)

        def over_ici(t, to_core, from_core):
            return 4 * t + 2 * to_core + from_core

        def passed_on(t, from_core):
            return 12 + 2 * t + from_core

        senders = [(t, cc) for t in range(3) for cc in ((0, 1) if scatter else (c,))]

        def slot_of(t, cc):
            cx, cy = chips[t]
            return 2 * (2 * cx + cy) + cc if scatter else 2 * cx + cy

        def first_copies():
            cps = []
            for a in range(n):
                h = outs[a].shape[1] // 2
                for t, (cx, cy) in enumerate(chips):
                    if scatter:
                        for half in (0, 1):
                            cps.append(copy(a, over_ici(t, half, c), 2 * mine + c, half, (cx, cy, half),
                                            src=ins[a].at[2 * cx + cy, pl.ds(half * h, h)]))
                    else:
                        cps.append(copy(a, over_ici(t, c, c), mine, c, (cx, cy, c), src=ins[a].at[pl.ds(c * h, h)]))
                if scatter:
                    cps.append(pltpu.make_async_remote_copy(
                        src_ref=ins[a].at[mine], dst_ref=outs[a].at[2 * mine + c], send_sem=send_sems.at[a, own_pair],
                        recv_sem=recv_sems.at[a, own_pair], device_id=sib, device_id_type=MESH))
            return cps

        def forward_copies():
            return [copy(a, passed_on(t, cc), slot_of(t, cc), c, sib) for a in range(n) for t, cc in senders]

        def start():
            for cp in first_copies():
                cp.start()

        def forward():
            fws = iter(forward_copies())
            for a in range(n):
                for t, cc in senders:
                    copy(a, over_ici(t, c, cc), slot_of(t, cc), c, me).wait_recv()
                    next(fws).start()

        def finish():
            for a in range(n):
                for t, cc in senders:
                    from_core = cc if scatter else 1 - c
                    copy(a, passed_on(t, from_core), slot_of(t, from_core), 1 - c, me).wait_recv()
                if scatter:
                    pltpu.make_async_remote_copy(
                        src_ref=ins[a].at[mine], dst_ref=outs[a].at[2 * mine + 1 - c], send_sem=send_sems.at[a, own_pair],
                        recv_sem=recv_sems.at[a, own_pair], device_id=me, device_id_type=MESH).wait_recv()
            for cp in first_copies() + forward_copies():
                cp.wait_send()

        return start, forward, finish

    def fill_own(self, outs):
        x, y, c = _place()
        if self.kind == "gather":
            return [lax.dynamic_update_index_in_dim(o, s, 2 * x + y, 0) for o, s in zip(outs, self.arrays)]
        return [lax.dynamic_update_index_in_dim(o, lax.dynamic_index_in_dim(s, 2 * x + y, 0, keepdims=False), 2 * (2 * x + y) + c, 0)
                for o, s in zip(outs, self.arrays)]

    def run(self, name):
        n = self.n

        def body(*refs):
            start, forward, finish = self.phases(refs[:n], refs[n:2 * n], *refs[2 * n:])
            start()
            forward()
            finish()

        outs = pl.pallas_call(
            body, in_specs=[ANY] * n, out_specs=[ANY] * n, out_shape=self.out_shapes(), scratch_shapes=self.sems(), name=name,
        )(*self.arrays)
        return self.fill_own(outs)


def _host_exchange(exchange, refs, n_in, n_out, step, n_steps):
    if exchange is None:
        return refs
    n = exchange.n
    own_in, ex_in = refs[:n_in], refs[n_in:n_in + n]
    own_out, ex_out = refs[n_in + n:n_in + n + n_out], refs[n_in + n + n_out:n_in + 2 * n + n_out]
    rest = refs[n_in + 2 * n + n_out:]
    start, forward, finish = exchange.phases(ex_in, ex_out, rest[-2], rest[-1])
    pl.when(step == 0)(start)
    pl.when(step == (2 * n_steps) // 3)(forward)
    pl.when(step == n_steps - 1)(finish)
    return own_in + own_out + rest[:-2]


def _hosted_call(exchange, body, *, grid, in_specs, out_specs, out_shape, scratch_shapes, compiler_params, name, args):
    if exchange is None:
        return pl.pallas_call(body, grid=grid, in_specs=in_specs, out_specs=out_specs, out_shape=out_shape,
                              scratch_shapes=scratch_shapes, compiler_params=compiler_params, name=name)(*args)
    n = exchange.n
    res = pl.pallas_call(
        body, grid=grid, in_specs=list(in_specs) + [ANY] * n, out_specs=list(out_specs) + [ANY] * n,
        out_shape=list(out_shape) + exchange.out_shapes(), scratch_shapes=list(scratch_shapes) + exchange.sems(),
        compiler_params=compiler_params, name=name)(*args, *exchange.arrays)
    return list(res[:len(out_shape)]) + [exchange.fill_own(res[len(out_shape):])]


def sum_slots(parts, name):
    S, R, C = parts.shape
    br = 128 if R % 128 == 0 else R

    def body(p_ref, o_ref):
        acc = p_ref[0].astype(F32)
        for s in range(1, S):
            acc = acc + p_ref[s].astype(F32)
        o_ref[...] = acc

    return pl.pallas_call(
        body,
        grid=(R // br,),
        in_specs=[pl.BlockSpec((S, br, C), lambda i: (0, i, 0))],
        out_specs=pl.BlockSpec((br, C), lambda i: (i, 0)),
        out_shape=jax.ShapeDtypeStruct((R, C), F32),
        compiler_params=_params(),
        name=name,
    )(parts)


def _adamw_math(w, g, m, v):
    m_new = ADAM_B1 * m + (1.0 - ADAM_B1) * g
    v_new = ADAM_B2 * v + (1.0 - ADAM_B2) * (g * g)
    m_hat = m_new / (1.0 - ADAM_B1 ** ADAM_STEP)
    v_hat = v_new / (1.0 - ADAM_B2 ** ADAM_STEP)
    return -ADAM_LR * (m_hat / (jnp.sqrt(v_hat) + ADAM_EPS) + ADAM_WD * w), m_new, v_new


def reduce_adamw(slots, w, m, v, name):
    S, R, C = slots.shape
    br = 128

    def body(p_ref, w_ref, m_ref, v_ref, g_ref, d_ref, nm_ref, nv_ref):
        g = p_ref[0].astype(F32)
        for s in range(1, S):
            g = g + p_ref[s].astype(F32)
        g_ref[...] = g
        d_ref[...], nm_ref[...], nv_ref[...] = _adamw_math(w_ref[...], g, m_ref[...], v_ref[...])

    spec = pl.BlockSpec((br, C), lambda i: (i, 0))
    return pl.pallas_call(
        body,
        grid=(R // br,),
        in_specs=[pl.BlockSpec((S, br, C), lambda i: (0, i, 0)), spec, spec, spec],
        out_specs=[spec] * 4,
        out_shape=[jax.ShapeDtypeStruct((R, C), F32)] * 4,
        compiler_params=_params(),
        name=name,
    )(slots, w, m, v)


SMALL_ROWS = 72


def gather_small(vec):
    def body(v_ref, o_ref, send_sems, recv_sems, local_sem):
        x, y, c = _place()
        flips = [(fx, fy, fc) for fx in (0, 1) for fy in (0, 1) for fc in (0, 1)][1:]

        def peer(f):
            return (1 - x if f[0] else x, 1 - y if f[1] else y, 1 - c if f[2] else c)

        slot = lambda p: 4 * p[0] + 2 * p[1] + p[2]
        own = pltpu.make_async_copy(v_ref, o_ref.at[slot((x, y, c))], local_sem)
        own.start()
        cps = [pltpu.make_async_remote_copy(
            src_ref=v_ref, dst_ref=o_ref.at[slot((x, y, c))], send_sem=send_sems.at[k], recv_sem=recv_sems.at[k],
            device_id=peer(f), device_id_type=MESH) for k, f in enumerate(flips)]
        for cp in cps:
            cp.start()
        for k, f in enumerate(flips):
            pltpu.make_async_remote_copy(
                src_ref=v_ref, dst_ref=o_ref.at[slot(peer(f))], send_sem=send_sems.at[k], recv_sem=recv_sems.at[k],
                device_id=(x, y, c), device_id_type=MESH).wait_recv()
        for cp in cps:
            cp.wait_send()
        own.wait()

    return pl.pallas_call(
        body,
        in_specs=[ANY],
        out_specs=ANY,
        out_shape=jax.ShapeDtypeStruct((8,) + vec.shape, vec.dtype),
        scratch_shapes=[pltpu.SemaphoreType.DMA((7,)), pltpu.SemaphoreType.DMA((7,)), pltpu.SemaphoreType.DMA],
        name="gather_small",
    )(vec)


GLOW_PAD = LANE - GLA_RANK


def kernel(x, attn_norm_g, w_in, gla_gate_w2, gla_gate_b, gla_norm_g, rel_bias, w_out, mlp_norm_g, w_ff1, w_ff2, final_norm_g, loss_target, m_attn_norm_g, m_w_in, m_gla_gate_w2, m_gla_gate_b, m_gla_norm_g, m_rel_bias, m_w_out, m_mlp_norm_g, m_w_ff1, m_w_ff2, m_final_norm_g, v_attn_norm_g, v_w_in, v_gla_gate_w2, v_gla_gate_b, v_gla_norm_g, v_rel_bias, v_w_out, v_mlp_norm_g, v_w_ff1, v_w_ff2, v_final_norm_g):
    xs, tgt = x[0], loss_target[0]
    T = xs.shape[0]
    cx, cy, _ = _place()
    chip = 2 * cx + cy
    gf = final_norm_g.reshape(1, D_MODEL)

    win_g, w2_g = Exchange("gather", [w_in[0].astype(BF16), gla_gate_w2[0]]).run("gather_w_in")
    win = jnp.transpose(win_g, (1, 0, 2)).reshape(D_MODEL, D_IN)
    n_glow = R_GLOW + GLA_RANK
    wp = jnp.concatenate([win[:, :n_glow], jnp.zeros((D_MODEL, GLOW_PAD), BF16), win[:, n_glow:]], axis=1)
    w2 = jnp.transpose(w2_g, (1, 0, 2)).reshape(GLA_RANK, GLA_QK)
    w2p = jnp.concatenate([w2, jnp.zeros((GLOW_PAD, GLA_QK), F32)], axis=0)

    proj, nx = inproj(xs, attn_norm_g, wp)
    tab = bias_tables(rel_bias)
    ob, lse, (wout_g, wff1, wff2_g) = dsa_fwd(
        proj, tab, Exchange("gather", [w_out[0].astype(BF16), w_ff1[0].astype(BF16), w_ff2[0].astype(BF16)]))
    wout = wout_g.reshape(D_MODEL, D_MODEL)
    wff2 = wff2_g.reshape(D_FF, D_MODEL)
    oa, opre, sprev = gla_fwd(proj, w2p, gla_gate_b, gla_norm_g)
    mixed, nm, act, dpre, dh2, dh1, dmixed, loss, dgf, dg2 = post_fused(xs, oa, ob, tgt, mlp_norm_g, gf, wout, wff1, wff2)

    late = [
        wgrad(mixed, dh1, "wgrad_out").reshape(4, D_MODEL // 4, D_MODEL),
        wgrad(nm, dpre, "wgrad_ff1", bn=FF_BLOCK, col_blocked=True),
        wgrad(act, dh2, "wgrad_ff2", bm=FF_BLOCK).reshape(4, FF_BLOCK, D_MODEL),
    ]
    da, dw2p, dbg, dgn = gla_bwd(proj, w2p, gla_gate_b, gla_norm_g, opre, sprev, dmixed)
    dq, dk, dv, dtab, late_slots = dsa_bwd(proj, tab, ob, lse, dmixed, Exchange("scatter", late))
    slots = dict(zip(["w_out", "w_ff1", "w_ff2"], late_slots))
    drel = bias_tables_bwd(dtab)

    dwp = wgrad_cat(nx, [da, dq, dk, dv], "wgrad_in")
    dwin = jnp.concatenate([dwp[:, :n_glow], dwp[:, P_A:]], axis=1)
    dwin = [jnp.transpose(dwin.reshape(D_MODEL, 4, D_IN // 4), (1, 0, 2))]
    dxs, dg1, (slots["w_in"],) = dx_final(xs, dh1, attn_norm_g, da, dq, dk, dv, wp, Exchange("scatter", dwin))

    sizes = [D_MODEL, GLA_QK, GLA_WIDTH, REL_BUCKETS * DSA_HEADS, D_MODEL, D_MODEL, GLA_RANK * GLA_QK, 1]
    small = jnp.concatenate([dg1.reshape(-1), dbg.reshape(-1), dgn.reshape(-1), drel.reshape(-1), dg2.reshape(-1),
                             dgf.reshape(-1), dw2p[:GLA_RANK].reshape(-1), loss.reshape(-1),
                             jnp.zeros((SMALL_ROWS * LANE - sum(sizes),), F32)]).reshape(SMALL_ROWS, LANE)
    tot = sum_slots(gather_small(small), "sum_small").reshape(-1)
    offs = np.concatenate([[0], np.cumsum(sizes)])
    piece = lambda i: tot[int(offs[i]):int(offs[i + 1])]
    g_g1 = piece(0).reshape(1, D_MODEL)
    g_bg = piece(1).reshape(1, GLA_QK)
    g_gn = piece(2).reshape(1, GLA_WIDTH)
    g_rel = piece(3).reshape(REL_BUCKETS, DSA_HEADS)
    g_g2 = piece(4).reshape(1, D_MODEL)
    g_gf = piece(5).reshape(1, D_MODEL)
    g_w2 = lax.dynamic_slice_in_dim(piece(6).reshape(GLA_RANK, GLA_QK), chip * (GLA_QK // 4), GLA_QK // 4, axis=1)

    loss_all = piece(7)[0]

    upd = [
        ("attn_norm_g", attn_norm_g, g_g1, m_attn_norm_g, v_attn_norm_g),
        ("w_in", w_in[0], None, m_w_in[0], v_w_in[0]),
        ("gla_gate_w2", gla_gate_w2[0], g_w2, m_gla_gate_w2[0], v_gla_gate_w2[0]),
        ("gla_gate_b", gla_gate_b, g_bg, m_gla_gate_b, v_gla_gate_b),
        ("gla_norm_g", gla_norm_g, g_gn, m_gla_norm_g, v_gla_norm_g),
        ("rel_bias", rel_bias, g_rel, m_rel_bias, v_rel_bias),
        ("w_out", w_out[0], None, m_w_out[0], v_w_out[0]),
        ("mlp_norm_g", mlp_norm_g, g_g2, m_mlp_norm_g, v_mlp_norm_g),
        ("w_ff1", w_ff1[0], None, m_w_ff1[0], v_w_ff1[0]),
        ("w_ff2", w_ff2[0], None, m_w_ff2[0], v_w_ff2[0]),
        ("final_norm_g", gf, g_gf, m_final_norm_g.reshape(1, D_MODEL), v_final_norm_g.reshape(1, D_MODEL)),
    ]
    shapes = [attn_norm_g.shape, w_in.shape, gla_gate_w2.shape, gla_gate_b.shape, gla_norm_g.shape, rel_bias.shape,
              w_out.shape, mlp_norm_g.shape, w_ff1.shape, w_ff2.shape, final_norm_g.shape]
    grads, deltas, new_m, new_v = [], [], [], []
    for (name, w, g, m, v), shape in zip(upd, shapes):
        if name in slots:
            g, d, nm_, nv_ = reduce_adamw(slots[name], w, m, v, "reduce_adamw_" + name)
        else:
            d, nm_, nv_ = adamw(w, g, m, v, "adamw_" + name)
        grads.append(g.reshape(shape))
        deltas.append(d.reshape(shape))
        new_m.append(nm_.reshape(shape))
        new_v.append(nv_.reshape(shape))
    return (loss_all, dxs.reshape(1, T, D_MODEL), *grads, *deltas, *new_m, *new_v)
```

```python
import functools
import math

import jax
import jax.numpy as jnp
import numpy as np
from jax import lax
from jax.experimental import pallas as pl
from jax.experimental.pallas import tpu as pltpu

F32 = jnp.float32
BF16 = jnp.bfloat16
MESH = pl.DeviceIdType.MESH

D_MODEL = 1024
GLA_WIDTH = 512
GLA_HEADS = 4
GLA_DK = 64
GLA_DV = 128
GLA_QK = GLA_HEADS * GLA_DK
GLA_RANK = 16
GLA_TAU = 16.0
GLA_CHUNK = 64
DSA_WIDTH = 512
DSA_HEADS = 8
DSA_DH = 64
DSA_DILATIONS = (1, 4, 16)
DSA_SPAN = 128
DSA_BLOCK = 128
DSA_SUPER = DSA_BLOCK * DSA_DILATIONS[-1]
REL_BUCKETS = 32
REL_MAX_DIST = 2048
D_FF = 4096
D_IN = 3088
EPS = 1e-6
NEG = -1e30
QK_SCALE = 0.125

ADAM_LR = 0.001
ADAM_B1 = 0.9
ADAM_B2 = 0.999
ADAM_EPS = 1e-08
ADAM_WD = 0.01
ADAM_STEP = 10

LANE = 128
P_GQ, P_GK, P_GV, P_GR = 0, 256, 512, 1024
P_GLOW = 1536
P_A = 1664
P_DQ, P_DK, P_DV = 1664, 2176, 2688
P_ALL = 3200
R_GLOW = 1536

VMEM_LIMIT = 56 * 1024 * 1024


def _params(sem=("arbitrary",), vmem=VMEM_LIMIT):
    return pltpu.CompilerParams(dimension_semantics=sem, vmem_limit_bytes=vmem)


def _dot(a, b):
    return jnp.dot(a, b, preferred_element_type=F32)


def _dot_nt(a, b):
    return lax.dot_general(a, b, (((1,), (1,)), ((), ())), preferred_element_type=F32)


def _dot_tn(a, b):
    return lax.dot_general(a, b, (((0,), (0,)), ((), ())), preferred_element_type=F32)


def _split3(x):
    x1 = x.astype(BF16)
    r1 = x - x1.astype(F32)
    x2 = r1.astype(BF16)
    x3 = (r1 - x2.astype(F32)).astype(BF16)
    return x1, x2, x3


def _dot_exact_lhs(m_bf16, x):
    x1, x2, x3 = _split3(x)
    return _dot(m_bf16, x1) + _dot(m_bf16, x2) + _dot(m_bf16, x3)


def _rstd(xf):
    return lax.rsqrt(jnp.mean(xf * xf, axis=-1, keepdims=True) + EPS)


def _load_once(hbm_ref, vmem_ref, sem):
    cp = pltpu.make_async_copy(hbm_ref, vmem_ref, sem)
    cp.start()
    cp.wait()


ANY = pl.BlockSpec(memory_space=pl.ANY)


def inproj(x, g1, wp):
    T = x.shape[0]
    tm = 256

    def body(x_ref, g_ref, w_hbm, proj_ref, nx_ref, w_vmem, sem):
        @pl.when(pl.program_id(0) == 0)
        def _():
            _load_once(w_hbm, w_vmem, sem)

        xf = x_ref[...]
        nx = ((xf * _rstd(xf)) * g_ref[...]).astype(BF16)
        nx_ref[...] = nx
        proj_ref[...] = _dot(nx, w_vmem[...])

    return pl.pallas_call(
        body,
        grid=(T // tm,),
        in_specs=[pl.BlockSpec((tm, D_MODEL), lambda i: (i, 0)), pl.BlockSpec((1, D_MODEL), lambda i: (0, 0)), ANY],
        out_specs=[pl.BlockSpec((tm, P_ALL), lambda i: (i, 0)), pl.BlockSpec((tm, D_MODEL), lambda i: (i, 0))],
        out_shape=[jax.ShapeDtypeStruct((T, P_ALL), F32), jax.ShapeDtypeStruct((T, D_MODEL), BF16)],
        scratch_shapes=[pltpu.VMEM((D_MODEL, P_ALL), BF16), pltpu.SemaphoreType.DMA],
        compiler_params=_params(),
        name="inproj",
    )(x, g1, wp)


GLA_CHUNKS_PER_STEP = 16
GLA_ROWS = GLA_CHUNK * GLA_CHUNKS_PER_STEP


def _gla_masks():
    lane = lax.broadcasted_iota(jnp.int32, (1, GLA_QK), 1)
    return [(lane >= h * GLA_DK) & (lane < (h + 1) * GLA_DK) for h in range(GLA_HEADS)]


def _log_sigmoid(x):
    return jnp.minimum(x, 0.0) - jnp.log(1.0 + jnp.exp(-jnp.abs(x)))


def _sigmoid(x):
    return 1.0 / (1.0 + jnp.exp(-x))


def _head_cols(h):
    return slice(h * GLA_DV, (h + 1) * GLA_DV)


GLA_GROUP = 256


def _gla_step_constants():
    ri = lax.broadcasted_iota(jnp.int32, (GLA_GROUP, GLA_GROUP), 0)
    ci = lax.broadcasted_iota(jnp.int32, (GLA_GROUP, GLA_GROUP), 1)
    shift = GLA_CHUNK.bit_length() - 1
    same = lax.shift_right_logical(ri, shift) == lax.shift_right_logical(ci, shift)
    return same & (ri >= ci), same & (ri <= ci), _gla_masks()


def _by_group(fn, *arrays):
    outs = [fn(*[a[g * GLA_GROUP:(g + 1) * GLA_GROUP] for a in arrays]) for g in range(GLA_ROWS // GLA_GROUP)]
    if isinstance(outs[0], tuple):
        return tuple(jnp.concatenate(parts, axis=0) for parts in zip(*outs))
    return jnp.concatenate(outs, axis=0)


def _per_chunk(x):
    return x.reshape(GLA_CHUNKS_PER_STEP, GLA_CHUNK, x.shape[-1])


def _chunk_rows_of(x, c):
    return x[c * GLA_CHUNK:(c + 1) * GLA_CHUNK]


def _stack_masked(x, masks):
    return jnp.concatenate([jnp.where(m, x, 0.0) for m in masks], axis=0)


def _stack_head_cols(x):
    return jnp.concatenate([x[:, _head_cols(h)] for h in range(GLA_HEADS)], axis=0)


def _diag_blocks(full, masks):
    out = jnp.where(masks[0], full[:GLA_DV], 0.0)
    for h in range(1, GLA_HEADS):
        out = out + jnp.where(masks[h], full[h * GLA_DV:(h + 1) * GLA_DV], 0.0)
    return out


def _row_blocks_masked(full, masks):
    out = jnp.where(masks[0], full[:GLA_CHUNK], 0.0)
    for h in range(1, GLA_HEADS):
        out = out + jnp.where(masks[h], full[h * GLA_CHUNK:(h + 1) * GLA_CHUNK], 0.0)
    return out


def _gla_step_common(q, k, glow_b, w2, bg, tri):
    gpre = _dot(glow_b, w2) + bg
    glog = _log_sigmoid(gpre) / GLA_TAU
    b = _by_group(lambda g: _dot_exact_lhs(tri, g), glog)
    bl = jnp.sum(_per_chunk(glog), axis=1, keepdims=True)
    eb = jnp.exp(b)
    enb = jnp.exp(-b)
    eke = jnp.exp(jnp.broadcast_to(bl, (GLA_CHUNKS_PER_STEP, GLA_CHUNK, GLA_QK)).reshape(GLA_ROWS, GLA_QK) - b)
    return gpre, eb, enb, eke, jnp.exp(bl), (q * QK_SCALE) * eb, k * enb, k * eke


def gla_fwd(proj, w2p, bg, gn):
    T = proj.shape[0]
    n_steps = T // GLA_ROWS
    n_chunks = T // GLA_CHUNK

    def body(proj_ref, w2_ref, bg_ref, gn_ref, oa_ref, opre_ref, sprev_ref, st_ref):
        @pl.when(pl.program_id(0) == 0)
        def _():
            st_ref[...] = jnp.zeros_like(st_ref)

        causal, _, masks = _gla_step_constants()
        q = proj_ref[:, P_GQ:P_GQ + GLA_QK]
        k = proj_ref[:, P_GK:P_GK + GLA_QK]
        v = proj_ref[:, P_GV:P_GV + GLA_WIDTH]
        r = proj_ref[:, P_GR:P_GR + GLA_WIDTH]
        glow = proj_ref[:, P_GLOW:P_GLOW + LANE].astype(BF16)
        _, _, _, _, ebl, qd, ki, ke = _gla_step_common(q, k, glow, w2_ref[...].astype(BF16), bg_ref[...], causal.astype(BF16))
        ki_b = ki.astype(BF16)
        v_b = v.astype(BF16)
        o_heads = []
        for h in range(GLA_HEADS):
            def intra(qd_g, ki_g, v_g):
                att = jnp.where(causal, _dot_nt(qd_g, ki_g), 0.0)
                return _dot(att.astype(BF16), v_g)

            o_heads.append(_by_group(intra, jnp.where(masks[h], qd, 0.0).astype(BF16), ki_b, v_b[:, _head_cols(h)]))
        st = st_ref[...]
        states = []
        for c in range(GLA_CHUNKS_PER_STEP):
            states.append(st)
            sprev_ref[c] = st
            inc = _diag_blocks(_dot_tn(_chunk_rows_of(v_b, c), _chunk_rows_of(ke, c).astype(BF16)), masks)
            st = st * ebl[c] + inc
        st_ref[...] = st
        inter = []
        for c in range(GLA_CHUNKS_PER_STEP):
            qd_c = _stack_masked(_chunk_rows_of(qd, c), masks).astype(BF16)
            got = _dot_nt(qd_c, states[c].astype(BF16))
            inter.append(jnp.concatenate([got[h * GLA_CHUNK:(h + 1) * GLA_CHUNK] for h in range(GLA_HEADS)], axis=1))
        o = jnp.concatenate(o_heads, axis=1) + jnp.concatenate(inter, axis=0)
        opre_ref[...] = o
        on = jnp.concatenate([o[:, _head_cols(h)] * _rstd(o[:, _head_cols(h)]) for h in range(GLA_HEADS)], axis=1)
        oa_ref[...] = ((on * gn_ref[...]) * (r * _sigmoid(r))).astype(BF16)

    return pl.pallas_call(
        body,
        grid=(n_steps,),
        in_specs=[
            pl.BlockSpec((GLA_ROWS, P_A), lambda i: (i, 0)),
            pl.BlockSpec((LANE, GLA_QK), lambda i: (0, 0)),
            pl.BlockSpec((1, GLA_QK), lambda i: (0, 0)),
            pl.BlockSpec((1, GLA_WIDTH), lambda i: (0, 0)),
        ],
        out_specs=[
            pl.BlockSpec((GLA_ROWS, GLA_WIDTH), lambda i: (i, 0)),
            pl.BlockSpec((GLA_ROWS, GLA_WIDTH), lambda i: (i, 0)),
            pl.BlockSpec((GLA_CHUNKS_PER_STEP, GLA_DV, GLA_QK), lambda i: (i, 0, 0)),
        ],
        out_shape=[
            jax.ShapeDtypeStruct((T, GLA_WIDTH), BF16),
            jax.ShapeDtypeStruct((T, GLA_WIDTH), F32),
            jax.ShapeDtypeStruct((n_chunks, GLA_DV, GLA_QK), F32),
        ],
        scratch_shapes=[pltpu.VMEM((GLA_DV, GLA_QK), F32)],
        compiler_params=_params(),
        name="gla_fwd",
    )(proj, w2p, bg, gn)


def gla_bwd(proj, w2p, bg, gn, opre, sprev, dmixed, exchange=None):
    T = proj.shape[0]
    n_steps = T // GLA_ROWS

    def body(*refs):
        refs = _host_exchange(exchange, refs, 7, 4, pl.program_id(0), n_steps)
        proj_ref, w2_ref, bg_ref, gn_ref, opre_ref, sprev_ref, doa_ref, da_ref, dw2_ref, dbg_ref, dgn_ref, dst_ref = refs

        @pl.when(pl.program_id(0) == 0)
        def _():
            dst_ref[...] = jnp.zeros_like(dst_ref)
            dw2_ref[...] = jnp.zeros_like(dw2_ref)
            dbg_ref[...] = jnp.zeros_like(dbg_ref)
            dgn_ref[...] = jnp.zeros_like(dgn_ref)

        causal, causal_t, masks = _gla_step_constants()
        w2 = w2_ref[...].astype(BF16)
        gn = gn_ref[...]
        q = proj_ref[:, P_GQ:P_GQ + GLA_QK]
        k = proj_ref[:, P_GK:P_GK + GLA_QK]
        v_b = proj_ref[:, P_GV:P_GV + GLA_WIDTH].astype(BF16)
        r = proj_ref[:, P_GR:P_GR + GLA_WIDTH]
        glow = proj_ref[:, P_GLOW:P_GLOW + LANE].astype(BF16)
        o = opre_ref[...]
        doa = doa_ref[...]
        gpre, eb, enb, eke, ebl, qd, ki, ke = _gla_step_common(q, k, glow, w2, bg_ref[...], causal.astype(BF16))
        sig = _sigmoid(r)
        rs = jnp.concatenate([jnp.broadcast_to(_rstd(o[:, _head_cols(h)]), (GLA_ROWS, GLA_DV)) for h in range(GLA_HEADS)], axis=1)
        on = o * rs
        d_ong = doa * (r * sig)
        dr = doa * (on * gn) * (sig * (1.0 + r * (1.0 - sig)))
        dgn_ref[...] += jnp.sum(d_ong * on, axis=0, keepdims=True)
        d_on = d_ong * gn
        t = d_on * on
        mean_t = jnp.concatenate([jnp.broadcast_to(jnp.mean(t[:, _head_cols(h)], axis=-1, keepdims=True), (GLA_ROWS, GLA_DV))
                                  for h in range(GLA_HEADS)], axis=1)
        do_b = (rs * (d_on - on * mean_t)).astype(BF16)
        ki_b = ki.astype(BF16)
        ke_b = ke.astype(BF16)
        dqd = jnp.zeros_like(qd)
        dki = jnp.zeros_like(qd)
        dv_heads = []
        for h in range(GLA_HEADS):
            qd_h = jnp.where(masks[h], qd, 0.0).astype(BF16)
            do_h = do_b[:, _head_cols(h)]

            def intra(qd_g, ki_g, v_g, do_g):
                att = jnp.where(causal, _dot_nt(qd_g, ki_g), 0.0).astype(BF16)
                d_att = jnp.where(causal, _dot_nt(do_g, v_g), 0.0).astype(BF16)
                return _dot_tn(att, do_g), _dot(d_att, ki_g), _dot_tn(d_att, qd_g)

            dv_h, dqd_h, dki_h = _by_group(intra, qd_h, ki_b, v_b[:, _head_cols(h)], do_h)
            dv_heads.append(dv_h)
            dqd = dqd + jnp.where(masks[h], dqd_h, 0.0)
            dki = dki + dki_h
        states = [sprev_ref[c] for c in range(GLA_CHUNKS_PER_STEP)]
        dqd_inter, dst_adds = [], []
        for c in range(GLA_CHUNKS_PER_STEP):
            do_c = _stack_head_cols(_chunk_rows_of(do_b, c))
            dqd_inter.append(_row_blocks_masked(_dot(do_c, states[c].astype(BF16)), masks))
            dst_adds.append(_diag_blocks(_dot_tn(_chunk_rows_of(do_b, c), _chunk_rows_of(qd, c).astype(BF16)), masks))
        dst = dst_ref[...]
        dsts, debls = [None] * GLA_CHUNKS_PER_STEP, [None] * GLA_CHUNKS_PER_STEP
        for c in reversed(range(GLA_CHUNKS_PER_STEP)):
            dsts[c] = dst
            debls[c] = jnp.sum(dst * states[c], axis=0, keepdims=True)
            dst = dst * ebl[c] + dst_adds[c]
        dst_ref[...] = dst
        dv_inter, dke = [], []
        for c in range(GLA_CHUNKS_PER_STEP):
            dst_b = dsts[c].astype(BF16)
            got = _dot_nt(_stack_masked(_chunk_rows_of(ke, c), masks).astype(BF16), dst_b)
            dv_inter.append(jnp.concatenate([got[h * GLA_CHUNK:(h + 1) * GLA_CHUNK] for h in range(GLA_HEADS)], axis=1))
            dke.append(_row_blocks_masked(_dot(_stack_head_cols(_chunk_rows_of(v_b, c)), dst_b), masks))
        dqd = dqd + jnp.concatenate(dqd_inter, axis=0)
        dke = jnp.concatenate(dke, axis=0)
        dv = jnp.concatenate(dv_heads, axis=1) + jnp.concatenate(dv_inter, axis=0)
        dkk = dke * ke
        dbl = jnp.sum(_per_chunk(dkk), axis=1, keepdims=True) + jnp.stack(debls) * ebl
        last_row = lax.broadcasted_iota(jnp.int32, (GLA_CHUNKS_PER_STEP, GLA_CHUNK, GLA_QK), 1) == GLA_CHUNK - 1
        db = dqd * qd - dki * ki - dkk + jnp.where(last_row, dbl, 0.0).reshape(GLA_ROWS, GLA_QK)
        tri_t = causal_t.astype(BF16)
        dglog = _by_group(lambda g: _dot_exact_lhs(tri_t, g), db)
        dgpre = (dglog / GLA_TAU) * _sigmoid(-gpre)
        dgpre_b = dgpre.astype(BF16)
        da_ref[...] = jnp.concatenate(
            [dqd * eb * QK_SCALE, dki * enb + dke * eke, dv, dr, _dot_nt(dgpre_b, w2)], axis=1).astype(BF16)
        dw2_ref[...] += _dot_tn(glow, dgpre_b)
        dbg_ref[...] += jnp.sum(dgpre, axis=0, keepdims=True)

    rev = lambda i: (n_steps - 1 - i, 0)
    return _hosted_call(
        exchange,
        body,
        grid=(n_steps,),
        in_specs=[
            pl.BlockSpec((GLA_ROWS, P_A), rev),
            pl.BlockSpec((LANE, GLA_QK), lambda i: (0, 0)),
            pl.BlockSpec((1, GLA_QK), lambda i: (0, 0)),
            pl.BlockSpec((1, GLA_WIDTH), lambda i: (0, 0)),
            pl.BlockSpec((GLA_ROWS, GLA_WIDTH), rev),
            pl.BlockSpec((GLA_CHUNKS_PER_STEP, GLA_DV, GLA_QK), lambda i: (n_steps - 1 - i, 0, 0)),
            pl.BlockSpec((GLA_ROWS, GLA_WIDTH), rev),
        ],
        out_specs=[
            pl.BlockSpec((GLA_ROWS, P_A), rev),
            pl.BlockSpec((LANE, GLA_QK), lambda i: (0, 0)),
            pl.BlockSpec((1, GLA_QK), lambda i: (0, 0)),
            pl.BlockSpec((1, GLA_WIDTH), lambda i: (0, 0)),
        ],
        out_shape=[
            jax.ShapeDtypeStruct((T, P_A), BF16),
            jax.ShapeDtypeStruct((LANE, GLA_QK), F32),
            jax.ShapeDtypeStruct((1, GLA_QK), F32),
            jax.ShapeDtypeStruct((1, GLA_WIDTH), F32),
        ],
        scratch_shapes=[pltpu.VMEM((GLA_DV, GLA_QK), F32)],
        compiler_params=_params(),
        name="gla_bwd",
        args=(proj, w2p, bg, gn, opre, sprev, dmixed),
    )


def _t5_bucket(dist):
    max_exact = REL_BUCKETS // 2
    n = np.maximum(dist, 0)
    large = max_exact + (np.log(np.maximum(n, 1) / max_exact) / math.log(REL_MAX_DIST / max_exact)
                         * (REL_BUCKETS - max_exact)).astype(np.int32)
    large = np.minimum(large, REL_BUCKETS - 1)
    return np.where(n < max_exact, n, large).astype(np.int32)


SUBLANES = 8


def _bucket_rows():
    steps = DSA_BLOCK - np.arange(2 * DSA_BLOCK)
    in_band = (steps >= 0) & (steps <= DSA_SPAN)
    rows = np.stack([np.where(in_band, _t5_bucket(steps * d), -1) for d in DSA_DILATIONS]).astype(np.int32)
    return np.broadcast_to(rows[:, None, :], (len(DSA_DILATIONS), SUBLANES, 2 * DSA_BLOCK)).copy()


def bias_tables(rel_bias):
    ids = jnp.asarray(_bucket_rows())
    nd = len(DSA_DILATIONS)

    def body(rel_ref, ids_ref, tab_ref):
        idt = ids_ref[0]
        for h in range(DSA_HEADS):
            row = jnp.where(idt < 0, NEG, 0.0).astype(F32)
            for b in range(REL_BUCKETS):
                row = jnp.where(idt == b, rel_ref[b, h], row)
            full = jnp.broadcast_to(row[0:1], (DSA_BLOCK, 2 * DSA_BLOCK))
            tab_ref[0, h] = pltpu.roll(full, 0, 1, stride=1, stride_axis=0)

    return pl.pallas_call(
        body,
        grid=(nd,),
        in_specs=[pl.BlockSpec(memory_space=pltpu.SMEM), pl.BlockSpec((1, SUBLANES, 2 * DSA_BLOCK), lambda d: (d, 0, 0))],
        out_specs=pl.BlockSpec((1, DSA_HEADS, DSA_BLOCK, 2 * DSA_BLOCK), lambda d: (d, 0, 0, 0)),
        out_shape=jax.ShapeDtypeStruct((nd, DSA_HEADS, DSA_BLOCK, 2 * DSA_BLOCK), F32),
        compiler_params=_params(),
        name="bias_tables",
    )(rel_bias, ids)


def _bucket_ids():
    L = DSA_BLOCK
    steps = L + np.arange(L)[:, None] - np.arange(2 * L)[None, :]
    in_band = (steps >= 0) & (steps <= DSA_SPAN)
    return np.stack([np.where(in_band, _t5_bucket(steps * d), -1) for d in DSA_DILATIONS]).astype(np.int32)


def bias_tables_bwd(dtab):
    ids = jnp.asarray(_bucket_ids())
    nd = len(DSA_DILATIONS)

    def body(dtab_ref, ids_ref, drel_ref):
        @pl.when(pl.program_id(0) == 0)
        def _():
            for b in range(REL_BUCKETS):
                for h in range(DSA_HEADS):
                    drel_ref[b, h] = 0.0

        idt = ids_ref[0]
        for b in range(REL_BUCKETS):
            in_bucket = idt == b
            for h in range(DSA_HEADS):
                drel_ref[b, h] += jnp.sum(jnp.where(in_bucket, dtab_ref[0, h], 0.0))

    return pl.pallas_call(
        body,
        grid=(nd,),
        in_specs=[pl.BlockSpec((1, DSA_HEADS, DSA_BLOCK, 2 * DSA_BLOCK), lambda d: (d, 0, 0, 0)),
                  pl.BlockSpec((1, DSA_BLOCK, 2 * DSA_BLOCK), lambda d: (d, 0, 0))],
        out_specs=pl.BlockSpec(memory_space=pltpu.SMEM),
        out_shape=jax.ShapeDtypeStruct((REL_BUCKETS, DSA_HEADS), F32),
        compiler_params=_params(),
        name="bias_tables_bwd",
    )(dtab, ids)


DSA_PAIRS = DSA_HEADS // 2
DSA_UNROLL = 16
DSA_COMBINE_ROWS = 256


def _dsa_units(d):
    return d, DSA_SUPER // (DSA_BLOCK * d)


def _dsa_specs(T):
    nsb = T // DSA_SUPER
    qcol, kcol, vcol = P_DQ // LANE, P_DK // LANE, P_DV // LANE
    return nsb, qcol, kcol, vcol


def _head_lane_mask():
    return lax.broadcasted_iota(jnp.int32, (1, LANE), 1) < DSA_DH


def _fill_tile_variants(tab_ref, variants):
    col = lax.broadcasted_iota(jnp.int32, (2 * DSA_BLOCK, 2 * DSA_BLOCK), 1)
    for di in range(len(DSA_DILATIONS)):
        tile = tab_ref[di, 0]
        variants[di, 0] = tile
        variants[di, 1] = jnp.where(col < DSA_BLOCK, NEG, tile)


def _tile_variants_scratch():
    return pltpu.VMEM((len(DSA_DILATIONS), 2, 2 * DSA_BLOCK, 2 * DSA_BLOCK), F32)


def _pair_tiles(tab):
    return tab.reshape(len(DSA_DILATIONS), DSA_PAIRS, 2 * DSA_BLOCK, 2 * DSA_BLOCK)


def _stack_heads(t, head0):
    return jnp.concatenate([jnp.where(head0, t, 0.0), jnp.where(head0, 0.0, t)], axis=0)


def dsa_fwd(proj, tab, exchange=None):
    T = proj.shape[0]
    nsb, qcol, kcol, vcol = _dsa_specs(T)
    S = DSA_SUPER

    def body(*refs):
        refs = _host_exchange(exchange, refs, 6, 2, pl.program_id(0) * nsb + pl.program_id(1), DSA_PAIRS * nsb)
        q_ref, kp_ref, kc_ref, vp_ref, vc_ref, tab_ref, out_ref, lse_ref, kk, vv, ob, lb, tiles = refs
        sb = pl.program_id(1)
        kk[0:S, :] = kp_ref[...]
        kk[S:2 * S, :] = kc_ref[...]
        vv[0:S, :] = vp_ref[...]
        vv[S:2 * S, :] = vc_ref[...]
        head0 = _head_lane_mask()
        pl.when(sb == 0)(functools.partial(_fill_tile_variants, tab_ref, tiles))

        for di, d in enumerate(DSA_DILATIONS):
            n_res, n_blk = _dsa_units(d)

            def unit(u, carry, di=di, d=d, n_blk=n_blk):
                r = u // n_blk
                c = u % n_blk
                q0 = r + d * DSA_BLOCK * c
                qrows = pl.ds(q0, DSA_BLOCK, stride=d) if d > 1 else pl.ds(q0, DSA_BLOCK)
                krows = pl.ds(S + q0 - d * DSA_BLOCK, 2 * DSA_BLOCK, stride=d) if d > 1 else pl.ds(S + q0 - DSA_BLOCK, 2 * DSA_BLOCK)
                q2 = q_ref[qrows, :] * QK_SCALE
                k2 = kk[krows, :].astype(BF16)
                v2 = vv[krows, :].astype(BF16)
                qs = _stack_heads(q2, head0).astype(BF16)
                s = _dot_nt(qs, k2) + tiles[di, ((sb == 0) & (c == 0)).astype(jnp.int32)]
                m = jnp.max(s, axis=-1, keepdims=True)
                p = jnp.exp(s - m)
                den = jnp.sum(p, axis=-1, keepdims=True)
                o = _dot(p.astype(BF16), v2) / den
                l = jnp.broadcast_to(m + jnp.log(den), (2 * DSA_BLOCK, LANE))
                ob[di, qrows, :] = jnp.where(head0, o[:DSA_BLOCK], o[DSA_BLOCK:])
                lb[di, qrows, :] = jnp.where(head0, l[:DSA_BLOCK], l[DSA_BLOCK:])
                return carry

            lax.fori_loop(0, n_res * n_blk, unit, 0, unroll=DSA_UNROLL)

        def combine(i, carry):
            rows = pl.ds(pl.multiple_of(i * DSA_COMBINE_ROWS, DSA_COMBINE_ROWS), DSA_COMBINE_ROWS)
            l0, l1, l2 = lb[0, rows, :], lb[1, rows, :], lb[2, rows, :]
            mx = jnp.maximum(jnp.maximum(l0, l1), l2)
            e0, e1, e2 = jnp.exp(l0 - mx), jnp.exp(l1 - mx), jnp.exp(l2 - mx)
            den = e0 + e1 + e2
            out_ref[rows, :] = (e0 * ob[0, rows, :] + e1 * ob[1, rows, :] + e2 * ob[2, rows, :]) / den
            lse_ref[rows, :] = mx + jnp.log(den)
            return carry

        lax.fori_loop(0, S // DSA_COMBINE_ROWS, combine, 0)

    prev = lambda col: (lambda hp, sb: (jnp.maximum(sb - 1, 0), col + hp))
    cur = lambda col: (lambda hp, sb: (sb, col + hp))
    blk = lambda f: pl.BlockSpec((S, LANE), f)
    return _hosted_call(
        exchange,
        body,
        grid=(DSA_PAIRS, nsb),
        in_specs=[blk(cur(qcol)), blk(prev(kcol)), blk(cur(kcol)), blk(prev(vcol)), blk(cur(vcol)),
                  pl.BlockSpec((len(DSA_DILATIONS), 1, 2 * DSA_BLOCK, 2 * DSA_BLOCK), lambda hp, sb: (0, hp, 0, 0))],
        out_specs=[blk(lambda hp, sb: (sb, hp)), blk(lambda hp, sb: (sb, hp))],
        out_shape=[jax.ShapeDtypeStruct((T, DSA_WIDTH), F32), jax.ShapeDtypeStruct((T, DSA_WIDTH), F32)],
        scratch_shapes=[pltpu.VMEM((2 * S, LANE), F32), pltpu.VMEM((2 * S, LANE), F32),
                        pltpu.VMEM((len(DSA_DILATIONS), S, LANE), F32), pltpu.VMEM((len(DSA_DILATIONS), S, LANE), F32),
                        _tile_variants_scratch()],
        compiler_params=_params(("arbitrary", "arbitrary")),
        name="dsa_fwd",
        args=(proj, proj, proj, proj, proj, _pair_tiles(tab)),
    )


def dsa_bwd(proj, tab, ob_out, lse, dmixed, exchange=None):
    T = proj.shape[0]
    nsb, qcol, kcol, vcol = _dsa_specs(T)
    S = DSA_SUPER
    nd = len(DSA_DILATIONS)
    ocol = GLA_WIDTH // LANE

    def body(*refs):
        refs = _host_exchange(exchange, refs, 9, 4, pl.program_id(0) * nsb + pl.program_id(1), DSA_PAIRS * nsb)
        (q_ref, kp_ref, kc_ref, vp_ref, vc_ref, tab_ref, o_ref, lse_ref, do_ref,
         dq_ref, dk_ref, dv_ref, dtab_ref, kk, vv, dqa, dkk, dvv, tiles) = refs
        j = pl.program_id(1)
        sb = nsb - 1 - j
        kk[0:S, :] = kp_ref[...]
        kk[S:2 * S, :] = kc_ref[...]
        vv[0:S, :] = vp_ref[...]
        vv[S:2 * S, :] = vc_ref[...]
        head0 = _head_lane_mask()
        pl.when(j == 0)(functools.partial(_fill_tile_variants, tab_ref, tiles))

        @pl.when(j == 0)
        def _():
            dtab_ref[...] = jnp.zeros_like(dtab_ref)
            dkk[S:2 * S, :] = jnp.zeros((S, LANE), F32)
            dvv[S:2 * S, :] = jnp.zeros((S, LANE), F32)

        @pl.when(j > 0)
        def _():
            dkk[S:2 * S, :] = dkk[0:S, :]
            dvv[S:2 * S, :] = dvv[0:S, :]

        dkk[0:S, :] = jnp.zeros((S, LANE), F32)
        dvv[0:S, :] = jnp.zeros((S, LANE), F32)
        dqa[...] = jnp.zeros_like(dqa)

        for di, d in enumerate(DSA_DILATIONS):
            n_res, n_blk = _dsa_units(d)

            def unit(u, carry, di=di, d=d, n_blk=n_blk):
                r = u // n_blk
                c = u % n_blk
                q0 = r + d * DSA_BLOCK * c
                qrows = pl.ds(q0, DSA_BLOCK, stride=d) if d > 1 else pl.ds(q0, DSA_BLOCK)
                krows = pl.ds(S + q0 - d * DSA_BLOCK, 2 * DSA_BLOCK, stride=d) if d > 1 else pl.ds(S + q0 - DSA_BLOCK, 2 * DSA_BLOCK)
                q2 = q_ref[qrows, :] * QK_SCALE
                k2 = kk[krows, :].astype(BF16)
                v2 = vv[krows, :].astype(BF16)
                do2 = do_ref[qrows, :]
                o2 = o_ref[qrows, :]
                l2 = lse_ref[qrows, :]
                qs = _stack_heads(q2, head0).astype(BF16)
                dos = _stack_heads(do2, head0)
                dos_b = dos.astype(BF16)
                delta = jnp.sum(dos * jnp.concatenate([o2, o2], axis=0), axis=-1, keepdims=True)
                lse = jnp.concatenate([jnp.max(jnp.where(head0, l2, -jnp.inf), axis=-1, keepdims=True),
                                       jnp.max(jnp.where(head0, -jnp.inf, l2), axis=-1, keepdims=True)], axis=0)
                s = _dot_nt(qs, k2) + tiles[di, ((sb == 0) & (c == 0)).astype(jnp.int32)]
                p = jnp.exp(s - lse)
                ds = p * (_dot_nt(dos_b, v2) - delta)
                dtab_ref[di, 0] += ds
                ds_b = ds.astype(BF16)
                dq = _dot(ds_b, k2)
                dqa[qrows, :] += jnp.where(head0, dq[:DSA_BLOCK], dq[DSA_BLOCK:]) * QK_SCALE
                dkk[krows, :] += _dot_tn(ds_b, qs)
                dvv[krows, :] += _dot_tn(p.astype(BF16), dos_b)
                return carry

            lax.fori_loop(0, n_res * n_blk, unit, 0, unroll=DSA_UNROLL)

        dq_ref[...] = dqa[...].astype(BF16)
        dk_ref[...] = dkk[S:2 * S, :].astype(BF16)
        dv_ref[...] = dvv[S:2 * S, :].astype(BF16)

    prev = lambda col: (lambda hp, j: (jnp.maximum(nsb - 2 - j, 0), col + hp))
    cur = lambda col: (lambda hp, j: (nsb - 1 - j, col + hp))
    blk = lambda f: pl.BlockSpec((S, LANE), f)
    out_blk = blk(lambda hp, j: (nsb - 1 - j, hp))
    tab_blk = pl.BlockSpec((nd, 1, 2 * DSA_BLOCK, 2 * DSA_BLOCK), lambda hp, j: (0, hp, 0, 0))
    dq, dk, dv, dtab, *carried = _hosted_call(
        exchange,
        body,
        grid=(DSA_PAIRS, nsb),
        in_specs=[blk(cur(qcol)), blk(prev(kcol)), blk(cur(kcol)), blk(prev(vcol)), blk(cur(vcol)), tab_blk,
                  out_blk, out_blk, blk(cur(ocol))],
        out_specs=[out_blk, out_blk, out_blk, tab_blk],
        out_shape=[jax.ShapeDtypeStruct((T, DSA_WIDTH), BF16)] * 3
        + [jax.ShapeDtypeStruct((nd, DSA_PAIRS, 2 * DSA_BLOCK, 2 * DSA_BLOCK), F32)],
        scratch_shapes=[pltpu.VMEM((2 * S, LANE), F32), pltpu.VMEM((2 * S, LANE), F32), pltpu.VMEM((S, LANE), F32),
                        pltpu.VMEM((2 * S, LANE), F32), pltpu.VMEM((2 * S, LANE), F32), _tile_variants_scratch()],
        compiler_params=_params(("arbitrary", "arbitrary")),
        name="dsa_bwd",
        args=(proj, proj, proj, proj, proj, _pair_tiles(tab), ob_out, lse, dmixed),
    )
    return (dq, dk, dv, dtab.reshape(nd, DSA_HEADS, DSA_BLOCK, 2 * DSA_BLOCK), *carried)


FF_BLOCKS = 4
FF_BLOCK = D_FF // FF_BLOCKS


def post_fused(x, oa, ob, tgt, g2, gf, wout, wff1, wff2):
    T = x.shape[0]
    tm = 256
    inv_d = 1.0 / D_MODEL

    def body(x_ref, oa_ref, ob_ref, tgt_ref, g2_ref, gf_ref, wout_hbm, wff1_hbm, wff2_hbm,
             mixed_ref, nm_ref, a_ref, dpre_ref, dh2_ref, dh1_ref, dmixed_ref, loss_ref, dgf_ref, dg2_ref,
             wout_v, wff1_v, wff2_v, sems):
        @pl.when(pl.program_id(0) == 0)
        def _():
            cps = [pltpu.make_async_copy(s, d, sems.at[i])
                   for i, (s, d) in enumerate([(wout_hbm, wout_v), (wff1_hbm, wff1_v), (wff2_hbm, wff2_v)])]
            for cp in cps:
                cp.start()
            for cp in cps:
                cp.wait()
            loss_ref[...] = jnp.zeros_like(loss_ref)
            dgf_ref[...] = jnp.zeros_like(dgf_ref)
            dg2_ref[...] = jnp.zeros_like(dg2_ref)

        mixed = jnp.concatenate([oa_ref[...], ob_ref[...].astype(BF16)], axis=1)
        mixed_ref[...] = mixed
        h1 = x_ref[...] + _dot(mixed, wout_v[...])
        rs1 = _rstd(h1)
        hn1 = h1 * rs1
        g2 = g2_ref[...]
        nm = (hn1 * g2).astype(BF16)
        nm_ref[...] = nm
        relu = []
        mlp = jnp.zeros((tm, D_MODEL), F32)
        for j in range(FF_BLOCKS):
            cols = slice(j * FF_BLOCK, (j + 1) * FF_BLOCK)
            r_j = jnp.maximum(_dot(nm, wff1_v[j]), 0.0)
            a_j = (r_j * r_j).astype(BF16)
            a_ref[:, cols] = a_j
            relu.append(r_j)
            mlp = mlp + _dot(a_j, wff2_v[cols, :])
        h2 = h1 + mlp
        rsf = _rstd(h2)
        hnf = h2 * rsf
        gf = gf_ref[...]
        diff = hnf * gf - tgt_ref[...]
        loss_ref[...] += 0.5 * jnp.sum(jnp.sum(diff * diff, axis=-1, keepdims=True) * inv_d, axis=0, keepdims=True)
        dy = diff * inv_d
        dgf_ref[...] += jnp.sum(dy * hnf, axis=0, keepdims=True)
        dhnf = dy * gf
        dh2 = rsf * (dhnf - hnf * jnp.mean(dhnf * hnf, axis=-1, keepdims=True))
        dh2_b = dh2.astype(BF16)
        dh2_ref[...] = dh2_b
        dnm = jnp.zeros((tm, D_MODEL), F32)
        for j in range(FF_BLOCKS):
            cols = slice(j * FF_BLOCK, (j + 1) * FF_BLOCK)
            dpre_j = (_dot_nt(dh2_b, wff2_v[cols, :]) * (2.0 * relu[j])).astype(BF16)
            dpre_ref[:, cols] = dpre_j
            dnm = dnm + _dot_nt(dpre_j, wff1_v[j])
        dg2_ref[...] += jnp.sum(dnm * hn1, axis=0, keepdims=True)
        dhn1 = dnm * g2
        dh1 = dh2 + rs1 * (dhn1 - hn1 * jnp.mean(dhn1 * hn1, axis=-1, keepdims=True))
        dh1_ref[...] = dh1
        dmixed_ref[...] = _dot_nt(dh1.astype(BF16), wout_v[...])

    row = lambda w: pl.BlockSpec((tm, w), lambda i: (i, 0))
    vec = lambda w: pl.BlockSpec((1, w), lambda i: (0, 0))
    return pl.pallas_call(
        body,
        grid=(T // tm,),
        in_specs=[row(D_MODEL), row(GLA_WIDTH), row(DSA_WIDTH), row(D_MODEL), vec(D_MODEL), vec(D_MODEL), ANY, ANY, ANY],
        out_specs=[row(D_MODEL), row(D_MODEL), row(D_FF), row(D_FF), row(D_MODEL), row(D_MODEL), row(D_MODEL),
                   vec(1), vec(D_MODEL), vec(D_MODEL)],
        out_shape=[
            jax.ShapeDtypeStruct((T, D_MODEL), BF16),
            jax.ShapeDtypeStruct((T, D_MODEL), BF16),
            jax.ShapeDtypeStruct((T, D_FF), BF16),
            jax.ShapeDtypeStruct((T, D_FF), BF16),
            jax.ShapeDtypeStruct((T, D_MODEL), BF16),
            jax.ShapeDtypeStruct((T, D_MODEL), F32),
            jax.ShapeDtypeStruct((T, D_MODEL), F32),
            jax.ShapeDtypeStruct((1, 1), F32),
            jax.ShapeDtypeStruct((1, D_MODEL), F32),
            jax.ShapeDtypeStruct((1, D_MODEL), F32),
        ],
        scratch_shapes=[pltpu.VMEM((D_MODEL, D_MODEL), BF16), pltpu.VMEM((FF_BLOCKS, D_MODEL, FF_BLOCK), BF16),
                        pltpu.VMEM((D_FF, D_MODEL), BF16), pltpu.SemaphoreType.DMA((3,))],
        compiler_params=_params(),
        name="post_fused",
    )(x, oa, ob, tgt, g2, gf, wout, wff1, wff2)


WGRAD_TOKENS = 2048


def wgrad(a, b, name, bm=None, bn=None, col_blocked=False):
    T, M = a.shape
    N = b.shape[1]
    bm = M if bm is None else bm
    bn = N if bn is None else bn
    tk = min(WGRAD_TOKENS, T)
    n_k = T // tk

    def body(a_ref, b_ref, o_ref, acc_ref):
        part = _dot_tn(a_ref[...].astype(BF16), b_ref[...].astype(BF16))
        out = o_ref.at[0] if col_blocked else o_ref
        k = pl.program_id(2)
        if n_k == 1:
            out[...] = part.astype(BF16)
            return

        @pl.when(k == 0)
        def _():
            acc_ref[...] = part

        @pl.when((k > 0) & (k < n_k - 1))
        def _():
            acc_ref[...] += part

        @pl.when(k == n_k - 1)
        def _():
            out[...] = (acc_ref[...] + part).astype(BF16)

    if col_blocked:
        assert bm == M
        out_spec = pl.BlockSpec((1, M, bn), lambda i, j, k: (j, 0, 0))
        out_shape = jax.ShapeDtypeStruct((N // bn, M, bn), BF16)
    else:
        out_spec = pl.BlockSpec((bm, bn), lambda i, j, k: (i, j))
        out_shape = jax.ShapeDtypeStruct((M, N), BF16)
    return pl.pallas_call(
        body,
        grid=(M // bm, N // bn, n_k),
        in_specs=[pl.BlockSpec((tk, bm), lambda i, j, k: (k, i)), pl.BlockSpec((tk, bn), lambda i, j, k: (k, j))],
        out_specs=out_spec,
        out_shape=out_shape,
        scratch_shapes=[pltpu.VMEM((bm, bn), F32)],
        compiler_params=_params(("arbitrary", "arbitrary", "arbitrary")),
        name=name,
    )(a, b)


def wgrad_cat(a, bs, name):
    T, M = a.shape
    widths = [b.shape[1] for b in bs]
    starts = [sum(widths[:i]) for i in range(len(bs))]
    N = sum(widths)
    tk = min(WGRAD_TOKENS // 2, T)
    n_k = T // tk

    def body(a_ref, *rest):
        b_refs, o_ref, acc_ref = rest[:len(bs)], rest[len(bs)], rest[len(bs) + 1]
        k = pl.program_id(0)

        @pl.when(k == 0)
        def _():
            acc_ref[...] = jnp.zeros_like(acc_ref)

        a_t = a_ref[...]
        for b_ref, start, width in zip(b_refs, starts, widths):
            acc_ref[:, start:start + width] += _dot_tn(a_t, b_ref[...])

        @pl.when(k == n_k - 1)
        def _():
            o_ref[...] = acc_ref[...].astype(BF16)

    return pl.pallas_call(
        body,
        grid=(n_k,),
        in_specs=[pl.BlockSpec((tk, M), lambda k: (k, 0))] + [pl.BlockSpec((tk, w), lambda k: (k, 0)) for w in widths],
        out_specs=pl.BlockSpec((M, N), lambda k: (0, 0)),
        out_shape=jax.ShapeDtypeStruct((M, N), BF16),
        scratch_shapes=[pltpu.VMEM((M, N), F32)],
        compiler_params=_params(),
        name=name,
    )(a, *bs)


def dx_final(x, dh1, g1, da, dq, dk, dv, wp, exchange=None):
    T = x.shape[0]
    tm = 256

    def body(*refs):
        refs = _host_exchange(exchange, refs, 8, 2, pl.program_id(0), T // tm)
        x_ref, dh1_ref, g_ref, da_ref, dq_ref, dk_ref, dv_ref, w_hbm, dx_ref, dg_ref, w_vmem, sem = refs

        @pl.when(pl.program_id(0) == 0)
        def _():
            _load_once(w_hbm, w_vmem, sem)
            dg_ref[...] = jnp.zeros_like(dg_ref)

        dnx = (_dot_nt(da_ref[...], w_vmem[:, 0:P_A]) + _dot_nt(dq_ref[...], w_vmem[:, P_DQ:P_DQ + DSA_WIDTH])
               + _dot_nt(dk_ref[...], w_vmem[:, P_DK:P_DK + DSA_WIDTH]) + _dot_nt(dv_ref[...], w_vmem[:, P_DV:P_DV + DSA_WIDTH]))
        xf = x_ref[...]
        rs = _rstd(xf)
        hn = xf * rs
        dg_ref[...] += jnp.sum(dnx * hn, axis=0, keepdims=True)
        dhn = dnx * g_ref[...]
        dx_ref[...] = dh1_ref[...] + rs * (dhn - hn * jnp.mean(dhn * hn, axis=-1, keepdims=True))

    row = lambda w: pl.BlockSpec((tm, w), lambda i: (i, 0))
    vec = pl.BlockSpec((1, D_MODEL), lambda i: (0, 0))
    return _hosted_call(
        exchange,
        body,
        grid=(T // tm,),
        in_specs=[row(D_MODEL), row(D_MODEL), vec, row(P_A), row(DSA_WIDTH), row(DSA_WIDTH), row(DSA_WIDTH), ANY],
        out_specs=[row(D_MODEL), vec],
        out_shape=[jax.ShapeDtypeStruct((T, D_MODEL), F32), jax.ShapeDtypeStruct((1, D_MODEL), F32)],
        scratch_shapes=[pltpu.VMEM((D_MODEL, P_ALL), BF16), pltpu.SemaphoreType.DMA],
        compiler_params=_params(),
        name="dx_final",
        args=(x, dh1, g1, da, dq, dk, dv, wp),
    )


def adamw(w, g, m, v, name):
    R, C = w.shape
    br = 256 if R % 256 == 0 else R

    def body(w_ref, g_ref, m_ref, v_ref, d_ref, nm_ref, nv_ref):
        d_ref[...], nm_ref[...], nv_ref[...] = _adamw_math(w_ref[...], g_ref[...], m_ref[...], v_ref[...])

    spec = pl.BlockSpec((br, C), lambda i: (i, 0))
    return pl.pallas_call(
        body,
        grid=(R // br,),
        in_specs=[spec] * 4,
        out_specs=[spec] * 3,
        out_shape=[jax.ShapeDtypeStruct((R, C), F32)] * 3,
        compiler_params=_params(),
        name=name,
    )(w, g, m, v)


def _place():
    return lax.axis_index("x"), lax.axis_index("y"), lax.axis_index("c")


def _other_chips(x, y):
    return [(1 - x, y), (x, 1 - y), (1 - x, 1 - y)]


class Exchange:
    def __init__(self, kind, arrays):
        self.kind, self.arrays, self.n = kind, arrays, len(arrays)
        self.slots = 4 if kind == "gather" else 8

    def out_shapes(self):
        if self.kind == "gather":
            return [jax.ShapeDtypeStruct((4,) + s.shape, s.dtype) for s in self.arrays]
        return [jax.ShapeDtypeStruct((8,) + s.shape[1:], s.dtype) for s in self.arrays]

    def sems(self):
        return [pltpu.SemaphoreType.DMA((self.n, 19)), pltpu.SemaphoreType.DMA((self.n, 19))]

    def phases(self, ins, outs, send_sems, recv_sems):
        n, scatter = self.n, self.kind == "scatter"
        x, y, c = _place()
        me, sib = (x, y, c), (x, y, 1 - c)
        mine = 2 * x + y
        chips = _other_chips(x, y)
        own_pair = 18

        def region(a, slot, half):
            h = outs[a].shape[1] // 2
            return outs[a].at[slot, pl.ds(half * h, h)]

        def copy(a, k, slot, half, to, src=None):
            return pltpu.make_async_remote_copy(
                src_ref=region(a, slot, half) if src is None else src, dst_ref=region(a, slot, half),
                send_sem=send_sems.at[a, k], recv_sem=recv_sems.at[a, k], device_id=to, device_id_type=MESH)

        def over_ici(t, to_core, from_core):
            return 4 * t + 2 * to_core + from_core

        def passed_on(t, from_core):
            return 12 + 2 * t + from_core

        senders = [(t, cc) for t in range(3) for cc in ((0, 1) if scatter else (c,))]

        def slot_of(t, cc):
            cx, cy = chips[t]
            return 2 * (2 * cx + cy) + cc if scatter else 2 * cx + cy

        def first_copies():
            cps = []
            for a in range(n):
                h = outs[a].shape[1] // 2
                for t, (cx, cy) in enumerate(chips):
                    if scatter:
                        for half in (0, 1):
                            cps.append(copy(a, over_ici(t, half, c), 2 * mine + c, half, (cx, cy, half),
                                            src=ins[a].at[2 * cx + cy, pl.ds(half * h, h)]))
                    else:
                        cps.append(copy(a, over_ici(t, c, c), mine, c, (cx, cy, c), src=ins[a].at[pl.ds(c * h, h)]))
                if scatter:
                    cps.append(pltpu.make_async_remote_copy(
                        src_ref=ins[a].at[mine], dst_ref=outs[a].at[2 * mine + c], send_sem=send_sems.at[a, own_pair],
                        recv_sem=recv_sems.at[a, own_pair], device_id=sib, device_id_type=MESH))
            return cps

        def forward_copies():
            return [copy(a, passed_on(t, cc), slot_of(t, cc), c, sib) for a in range(n) for t, cc in senders]

        def start():
            for cp in first_copies():
                cp.start()

        def forward():
            fws = iter(forward_copies())
            for a in range(n):
                for t, cc in senders:
                    copy(a, over_ici(t, c, cc), slot_of(t, cc), c, me).wait_recv()
                    next(fws).start()

        def finish():
            for a in range(n):
                for t, cc in senders:
                    from_core = cc if scatter else 1 - c
                    copy(a, passed_on(t, from_core), slot_of(t, from_core), 1 - c, me).wait_recv()
                if scatter:
                    pltpu.make_async_remote_copy(
                        src_ref=ins[a].at[mine], dst_ref=outs[a].at[2 * mine + 1 - c], send_sem=send_sems.at[a, own_pair],
                        recv_sem=recv_sems.at[a, own_pair], device_id=me, device_id_type=MESH).wait_recv()
            for cp in first_copies() + forward_copies():
                cp.wait_send()

        return start, forward, finish

    def fill_own(self, outs):
        x, y, c = _place()
        if self.kind == "gather":
            return [lax.dynamic_update_index_in_dim(o, s, 2 * x + y, 0) for o, s in zip(outs, self.arrays)]
        return [lax.dynamic_update_index_in_dim(o, lax.dynamic_index_in_dim(s, 2 * x + y, 0, keepdims=False), 2 * (2 * x + y) + c, 0)
                for o, s in zip(outs, self.arrays)]

    def run(self, name):
        n = self.n

        def body(*refs):
            start, forward, finish = self.phases(refs[:n], refs[n:2 * n], *refs[2 * n:])
            start()
            forward()
            finish()

        outs = pl.pallas_call(
            body, in_specs=[ANY] * n, out_specs=[ANY] * n, out_shape=self.out_shapes(), scratch_shapes=self.sems(), name=name,
        )(*self.arrays)
        return self.fill_own(outs)


def _host_exchange(exchange, refs, n_in, n_out, step, n_steps):
    if exchange is None:
        return refs
    n = exchange.n
    own_in, ex_in = refs[:n_in], refs[n_in:n_in + n]
    own_out, ex_out = refs[n_in + n:n_in + n + n_out], refs[n_in + n + n_out:n_in + 2 * n + n_out]
    rest = refs[n_in + 2 * n + n_out:]
    start, forward, finish = exchange.phases(ex_in, ex_out, rest[-2], rest[-1])
    pl.when(step == 0)(start)
    pl.when(step == (2 * n_steps) // 3)(forward)
    pl.when(step == n_steps - 1)(finish)
    return own_in + own_out + rest[:-2]


def _hosted_call(exchange, body, *, grid, in_specs, out_specs, out_shape, scratch_shapes, compiler_params, name, args):
    if exchange is None:
        return pl.pallas_call(body, grid=grid, in_specs=in_specs, out_specs=out_specs, out_shape=out_shape,
                              scratch_shapes=scratch_shapes, compiler_params=compiler_params, name=name)(*args)
    n = exchange.n
    res = pl.pallas_call(
        body, grid=grid, in_specs=list(in_specs) + [ANY] * n, out_specs=list(out_specs) + [ANY] * n,
        out_shape=list(out_shape) + exchange.out_shapes(), scratch_shapes=list(scratch_shapes) + exchange.sems(),
        compiler_params=compiler_params, name=name)(*args, *exchange.arrays)
    return list(res[:len(out_shape)]) + [exchange.fill_own(res[len(out_shape):])]


def sum_slots(parts, name):
    S, R, C = parts.shape
    br = 128 if R % 128 == 0 else R

    def body(p_ref, o_ref):
        acc = p_ref[0].astype(F32)
        for s in range(1, S):
            acc = acc + p_ref[s].astype(F32)
        o_ref[...] = acc

    return pl.pallas_call(
        body,
        grid=(R // br,),
        in_specs=[pl.BlockSpec((S, br, C), lambda i: (0, i, 0))],
        out_specs=pl.BlockSpec((br, C), lambda i: (i, 0)),
        out_shape=jax.ShapeDtypeStruct((R, C), F32),
        compiler_params=_params(),
        name=name,
    )(parts)


def _adamw_math(w, g, m, v):
    m_new = ADAM_B1 * m + (1.0 - ADAM_B1) * g
    v_new = ADAM_B2 * v + (1.0 - ADAM_B2) * (g * g)
    m_hat = m_new / (1.0 - ADAM_B1 ** ADAM_STEP)
    v_hat = v_new / (1.0 - ADAM_B2 ** ADAM_STEP)
    return -ADAM_LR * (m_hat / (jnp.sqrt(v_hat) + ADAM_EPS) + ADAM_WD * w), m_new, v_new


def reduce_adamw(slots, w, m, v, name):
    S, R, C = slots.shape
    br = 128

    def body(p_ref, w_ref, m_ref, v_ref, g_ref, d_ref, nm_ref, nv_ref):
        g = p_ref[0].astype(F32)
        for s in range(1, S):
            g = g + p_ref[s].astype(F32)
        g_ref[...] = g
        d_ref[...], nm_ref[...], nv_ref[...] = _adamw_math(w_ref[...], g, m_ref[...], v_ref[...])

    spec = pl.BlockSpec((br, C), lambda i: (i, 0))
    return pl.pallas_call(
        body,
        grid=(R // br,),
        in_specs=[pl.BlockSpec((S, br, C), lambda i: (0, i, 0)), spec, spec, spec],
        out_specs=[spec] * 4,
        out_shape=[jax.ShapeDtypeStruct((R, C), F32)] * 4,
        compiler_params=_params(),
        name=name,
    )(slots, w, m, v)


SMALL_ROWS = 72


def gather_small(vec):
    def body(v_ref, o_ref, send_sems, recv_sems, local_sem):
        x, y, c = _place()
        flips = [(fx, fy, fc) for fx in (0, 1) for fy in (0, 1) for fc in (0, 1)][1:]

        def peer(f):
            return (1 - x if f[0] else x, 1 - y if f[1] else y, 1 - c if f[2] else c)

        slot = lambda p: 4 * p[0] + 2 * p[1] + p[2]
        own = pltpu.make_async_copy(v_ref, o_ref.at[slot((x, y, c))], local_sem)
        own.start()
        cps = [pltpu.make_async_remote_copy(
            src_ref=v_ref, dst_ref=o_ref.at[slot((x, y, c))], send_sem=send_sems.at[k], recv_sem=recv_sems.at[k],
            device_id=peer(f), device_id_type=MESH) for k, f in enumerate(flips)]
        for cp in cps:
            cp.start()
        for k, f in enumerate(flips):
            pltpu.make_async_remote_copy(
                src_ref=v_ref, dst_ref=o_ref.at[slot(peer(f))], send_sem=send_sems.at[k], recv_sem=recv_sems.at[k],
                device_id=(x, y, c), device_id_type=MESH).wait_recv()
        for cp in cps:
            cp.wait_send()
        own.wait()

    return pl.pallas_call(
        body,
        in_specs=[ANY],
        out_specs=ANY,
        out_shape=jax.ShapeDtypeStruct((8,) + vec.shape, vec.dtype),
        scratch_shapes=[pltpu.SemaphoreType.DMA((7,)), pltpu.SemaphoreType.DMA((7,)), pltpu.SemaphoreType.DMA],
        name="gather_small",
    )(vec)


GLOW_PAD = LANE - GLA_RANK


def kernel(x, attn_norm_g, w_in, gla_gate_w2, gla_gate_b, gla_norm_g, rel_bias, w_out, mlp_norm_g, w_ff1, w_ff2, final_norm_g, loss_target, m_attn_norm_g, m_w_in, m_gla_gate_w2, m_gla_gate_b, m_gla_norm_g, m_rel_bias, m_w_out, m_mlp_norm_g, m_w_ff1, m_w_ff2, m_final_norm_g, v_attn_norm_g, v_w_in, v_gla_gate_w2, v_gla_gate_b, v_gla_norm_g, v_rel_bias, v_w_out, v_mlp_norm_g, v_w_ff1, v_w_ff2, v_final_norm_g):
    xs, tgt = x[0], loss_target[0]
    T = xs.shape[0]
    cx, cy, _ = _place()
    chip = 2 * cx + cy
    gf = final_norm_g.reshape(1, D_MODEL)

    win_g, w2_g = Exchange("gather", [w_in[0].astype(BF16), gla_gate_w2[0]]).run("gather_w_in")
    win = jnp.transpose(win_g, (1, 0, 2)).reshape(D_MODEL, D_IN)
    n_glow = R_GLOW + GLA_RANK
    wp = jnp.concatenate([win[:, :n_glow], jnp.zeros((D_MODEL, GLOW_PAD), BF16), win[:, n_glow:]], axis=1)
    w2 = jnp.transpose(w2_g, (1, 0, 2)).reshape(GLA_RANK, GLA_QK)
    w2p = jnp.concatenate([w2, jnp.zeros((GLOW_PAD, GLA_QK), F32)], axis=0)

    proj, nx = inproj(xs, attn_norm_g, wp)
    tab = bias_tables(rel_bias)
    ob, lse, (wout_g, wff1, wff2_g) = dsa_fwd(
        proj, tab, Exchange("gather", [w_out[0].astype(BF16), w_ff1[0].astype(BF16), w_ff2[0].astype(BF16)]))
    wout = wout_g.reshape(D_MODEL, D_MODEL)
    wff2 = wff2_g.reshape(D_FF, D_MODEL)
    oa, opre, sprev = gla_fwd(proj, w2p, gla_gate_b, gla_norm_g)
    mixed, nm, act, dpre, dh2, dh1, dmixed, loss, dgf, dg2 = post_fused(xs, oa, ob, tgt, mlp_norm_g, gf, wout, wff1, wff2)

    late = [
        wgrad(mixed, dh1, "wgrad_out").reshape(4, D_MODEL // 4, D_MODEL),
        wgrad(nm, dpre, "wgrad_ff1", bn=FF_BLOCK, col_blocked=True),
        wgrad(act, dh2, "wgrad_ff2", bm=FF_BLOCK).reshape(4, FF_BLOCK, D_MODEL),
    ]
    da, dw2p, dbg, dgn = gla_bwd(proj, w2p, gla_gate_b, gla_norm_g, opre, sprev, dmixed)
    dq, dk, dv, dtab, late_slots = dsa_bwd(proj, tab, ob, lse, dmixed, Exchange("scatter", late))
    slots = dict(zip(["w_out", "w_ff1", "w_ff2"], late_slots))
    drel = bias_tables_bwd(dtab)

    dwp = wgrad_cat(nx, [da, dq, dk, dv], "wgrad_in")
    dwin = jnp.concatenate([dwp[:, :n_glow], dwp[:, P_A:]], axis=1)
    dwin = [jnp.transpose(dwin.reshape(D_MODEL, 4, D_IN // 4), (1, 0, 2))]
    dxs, dg1, (slots["w_in"],) = dx_final(xs, dh1, attn_norm_g, da, dq, dk, dv, wp, Exchange("scatter", dwin))

    sizes = [D_MODEL, GLA_QK, GLA_WIDTH, REL_BUCKETS * DSA_HEADS, D_MODEL, D_MODEL, GLA_RANK * GLA_QK, 1]
    small = jnp.concatenate([dg1.reshape(-1), dbg.reshape(-1), dgn.reshape(-1), drel.reshape(-1), dg2.reshape(-1),
                             dgf.reshape(-1), dw2p[:GLA_RANK].reshape(-1), loss.reshape(-1),
                             jnp.zeros((SMALL_ROWS * LANE - sum(sizes),), F32)]).reshape(SMALL_ROWS, LANE)
    tot = sum_slots(gather_small(small), "sum_small").reshape(-1)
    offs = np.concatenate([[0], np.cumsum(sizes)])
    piece = lambda i: tot[int(offs[i]):int(offs[i + 1])]
    g_g1 = piece(0).reshape(1, D_MODEL)
    g_bg = piece(1).reshape(1, GLA_QK)
    g_gn = piece(2).reshape(1, GLA_WIDTH)
    g_rel = piece(3).reshape(REL_BUCKETS, DSA_HEADS)
    g_g2 = piece(4).reshape(1, D_MODEL)
    g_gf = piece(5).reshape(1, D_MODEL)
    g_w2 = lax.dynamic_slice_in_dim(piece(6).reshape(GLA_RANK, GLA_QK), chip * (GLA_QK // 4), GLA_QK // 4, axis=1)

    loss_all = piece(7)[0]

    upd = [
        ("attn_norm_g", attn_norm_g, g_g1, m_attn_norm_g, v_attn_norm_g),
        ("w_in", w_in[0], None, m_w_in[0], v_w_in[0]),
        ("gla_gate_w2", gla_gate_w2[0], g_w2, m_gla_gate_w2[0], v_gla_gate_w2[0]),
        ("gla_gate_b", gla_gate_b, g_bg, m_gla_gate_b, v_gla_gate_b),
        ("gla_norm_g", gla_norm_g, g_gn, m_gla_norm_g, v_gla_norm_g),
        ("rel_bias", rel_bias, g_rel, m_rel_bias, v_rel_bias),
        ("w_out", w_out[0], None, m_w_out[0], v_w_out[0]),
        ("mlp_norm_g", mlp_norm_g, g_g2, m_mlp_norm_g, v_mlp_norm_g),
        ("w_ff1", w_ff1[0], None, m_w_ff1[0], v_w_ff1[0]),
        ("w_ff2", w_ff2[0], None, m_w_ff2[0], v_w_ff2[0]),
        ("final_norm_g", gf, g_gf, m_final_norm_g.reshape(1, D_MODEL), v_final_norm_g.reshape(1, D_MODEL)),
    ]
    shapes = [attn_norm_g.shape, w_in.shape, gla_gate_w2.shape, gla_gate_b.shape, gla_norm_g.shape, rel_bias.shape,
              w_out.shape, mlp_norm_g.shape, w_ff1.shape, w_ff2.shape, final_norm_g.shape]
    grads, deltas, new_m, new_v = [], [], [], []
    for (name, w, g, m, v), shape in zip(upd, shapes):
        if name in slots:
            g, d, nm_, nv_ = reduce_adamw(slots[name], w, m, v, "reduce_adamw_" + name)
        else:
            d, nm_, nv_ = adamw(w, g, m, v, "adamw_" + name)
        grads.append(g.reshape(shape))
        deltas.append(d.reshape(shape))
        new_m.append(nm_.reshape(shape))
        new_v.append(nv_.reshape(shape))
    return (loss_all, dxs.reshape(1, T, D_MODEL), *grads, *deltas, *new_m, *new_v)
```

```python
import functools
import math

import jax
import jax.numpy as jnp
import numpy as np
from jax import lax
from jax.experimental import pallas as pl
from jax.experimental.pallas import tpu as pltpu

F32 = jnp.float32
BF16 = jnp.bfloat16
MESH = pl.DeviceIdType.MESH

D_MODEL = 1024
GLA_WIDTH = 512
GLA_HEADS = 4
GLA_DK = 64
GLA_DV = 128
GLA_QK = GLA_HEADS * GLA_DK
GLA_RANK = 16
GLA_TAU = 16.0
GLA_CHUNK = 64
DSA_WIDTH = 512
DSA_HEADS = 8
DSA_DH = 64
DSA_DILATIONS = (1, 4, 16)
DSA_SPAN = 128
DSA_BLOCK = 128
DSA_SUPER = DSA_BLOCK * DSA_DILATIONS[-1]
REL_BUCKETS = 32
REL_MAX_DIST = 2048
D_FF = 4096
D_IN = 3088
EPS = 1e-6
NEG = -1e30
QK_SCALE = 0.125

ADAM_LR = 0.001
ADAM_B1 = 0.9
ADAM_B2 = 0.999
ADAM_EPS = 1e-08
ADAM_WD = 0.01
ADAM_STEP = 10

LANE = 128
P_GQ, P_GK, P_GV, P_GR = 0, 256, 512, 1024
P_GLOW = 1536
P_A = 1664
P_DQ, P_DK, P_DV = 1664, 2176, 2688
P_ALL = 3200
R_GLOW = 1536

VMEM_LIMIT = 56 * 1024 * 1024


def _params(sem=("arbitrary",), vmem=VMEM_LIMIT):
    return pltpu.CompilerParams(dimension_semantics=sem, vmem_limit_bytes=vmem)


def _dot(a, b):
    return jnp.dot(a, b, preferred_element_type=F32)


def _dot_nt(a, b):
    return lax.dot_general(a, b, (((1,), (1,)), ((), ())), preferred_element_type=F32)


def _dot_tn(a, b):
    return lax.dot_general(a, b, (((0,), (0,)), ((), ())), preferred_element_type=F32)


def _split3(x):
    x1 = x.astype(BF16)
    r1 = x - x1.astype(F32)
    x2 = r1.astype(BF16)
    x3 = (r1 - x2.astype(F32)).astype(BF16)
    return x1, x2, x3


def _dot_exact_lhs(m_bf16, x):
    x1, x2, x3 = _split3(x)
    return _dot(m_bf16, x1) + _dot(m_bf16, x2) + _dot(m_bf16, x3)


def _rstd(xf):
    return lax.rsqrt(jnp.mean(xf * xf, axis=-1, keepdims=True) + EPS)


def _load_once(hbm_ref, vmem_ref, sem):
    cp = pltpu.make_async_copy(hbm_ref, vmem_ref, sem)
    cp.start()
    cp.wait()


ANY = pl.BlockSpec(memory_space=pl.ANY)


def inproj(x, g1, wp):
    T = x.shape[0]
    tm = 512

    def body(x_ref, g_ref, w_hbm, proj_ref, nx_ref, w_vmem, sem):
        @pl.when(pl.program_id(0) == 0)
        def _():
            _load_once(w_hbm, w_vmem, sem)

        xf = x_ref[...]
        nx = ((xf * _rstd(xf)) * g_ref[...]).astype(BF16)
        nx_ref[...] = nx
        proj_ref[...] = _dot(nx, w_vmem[...])

    return pl.pallas_call(
        body,
        grid=(T // tm,),
        in_specs=[pl.BlockSpec((tm, D_MODEL), lambda i: (i, 0)), pl.BlockSpec((1, D_MODEL), lambda i: (0, 0)), ANY],
        out_specs=[pl.BlockSpec((tm, P_ALL), lambda i: (i, 0)), pl.BlockSpec((tm, D_MODEL), lambda i: (i, 0))],
        out_shape=[jax.ShapeDtypeStruct((T, P_ALL), F32), jax.ShapeDtypeStruct((T, D_MODEL), BF16)],
        scratch_shapes=[pltpu.VMEM((D_MODEL, P_ALL), BF16), pltpu.SemaphoreType.DMA],
        compiler_params=_params(),
        name="inproj",
    )(x, g1, wp)


GLA_CHUNKS_PER_STEP = 16
GLA_ROWS = GLA_CHUNK * GLA_CHUNKS_PER_STEP


def _gla_masks():
    lane = lax.broadcasted_iota(jnp.int32, (1, GLA_QK), 1)
    return [(lane >= h * GLA_DK) & (lane < (h + 1) * GLA_DK) for h in range(GLA_HEADS)]


def _log_sigmoid(x):
    return jnp.minimum(x, 0.0) - jnp.log(1.0 + jnp.exp(-jnp.abs(x)))


def _sigmoid(x):
    return 1.0 / (1.0 + jnp.exp(-x))


def _head_cols(h):
    return slice(h * GLA_DV, (h + 1) * GLA_DV)


GLA_GROUP = 256


def _gla_step_constants():
    ri = lax.broadcasted_iota(jnp.int32, (GLA_GROUP, GLA_GROUP), 0)
    ci = lax.broadcasted_iota(jnp.int32, (GLA_GROUP, GLA_GROUP), 1)
    shift = GLA_CHUNK.bit_length() - 1
    same = lax.shift_right_logical(ri, shift) == lax.shift_right_logical(ci, shift)
    return same & (ri >= ci), same & (ri <= ci), _gla_masks()


def _by_group(fn, *arrays):
    outs = [fn(*[a[g * GLA_GROUP:(g + 1) * GLA_GROUP] for a in arrays]) for g in range(GLA_ROWS // GLA_GROUP)]
    if isinstance(outs[0], tuple):
        return tuple(jnp.concatenate(parts, axis=0) for parts in zip(*outs))
    return jnp.concatenate(outs, axis=0)


def _per_chunk(x):
    return x.reshape(GLA_CHUNKS_PER_STEP, GLA_CHUNK, x.shape[-1])


def _chunk_rows_of(x, c):
    return x[c * GLA_CHUNK:(c + 1) * GLA_CHUNK]


def _stack_masked(x, masks):
    return jnp.concatenate([jnp.where(m, x, 0.0) for m in masks], axis=0)


def _stack_head_cols(x):
    return jnp.concatenate([x[:, _head_cols(h)] for h in range(GLA_HEADS)], axis=0)


def _diag_blocks(full, masks):
    out = jnp.where(masks[0], full[:GLA_DV], 0.0)
    for h in range(1, GLA_HEADS):
        out = out + jnp.where(masks[h], full[h * GLA_DV:(h + 1) * GLA_DV], 0.0)
    return out


def _row_blocks_masked(full, masks):
    out = jnp.where(masks[0], full[:GLA_CHUNK], 0.0)
    for h in range(1, GLA_HEADS):
        out = out + jnp.where(masks[h], full[h * GLA_CHUNK:(h + 1) * GLA_CHUNK], 0.0)
    return out


def _gla_step_common(q, k, glow_b, w2, bg, tri):
    gpre = _dot(glow_b, w2) + bg
    glog = _log_sigmoid(gpre) / GLA_TAU
    b = _by_group(lambda g: _dot_exact_lhs(tri, g), glog)
    bl = jnp.sum(_per_chunk(glog), axis=1, keepdims=True)
    eb = jnp.exp(b)
    enb = jnp.exp(-b)
    eke = jnp.exp(jnp.broadcast_to(bl, (GLA_CHUNKS_PER_STEP, GLA_CHUNK, GLA_QK)).reshape(GLA_ROWS, GLA_QK) - b)
    return gpre, eb, enb, eke, jnp.exp(bl), (q * QK_SCALE) * eb, k * enb, k * eke


def gla_fwd(proj, w2p, bg, gn):
    T = proj.shape[0]
    n_steps = T // GLA_ROWS
    n_chunks = T // GLA_CHUNK

    def body(proj_ref, w2_ref, bg_ref, gn_ref, oa_ref, opre_ref, sprev_ref, st_ref):
        @pl.when(pl.program_id(0) == 0)
        def _():
            st_ref[...] = jnp.zeros_like(st_ref)

        causal, _, masks = _gla_step_constants()
        q = proj_ref[:, P_GQ:P_GQ + GLA_QK]
        k = proj_ref[:, P_GK:P_GK + GLA_QK]
        v = proj_ref[:, P_GV:P_GV + GLA_WIDTH]
        r = proj_ref[:, P_GR:P_GR + GLA_WIDTH]
        glow = proj_ref[:, P_GLOW:P_GLOW + LANE].astype(BF16)
        _, _, _, _, ebl, qd, ki, ke = _gla_step_common(q, k, glow, w2_ref[...].astype(BF16), bg_ref[...], causal.astype(BF16))
        ki_b = ki.astype(BF16)
        v_b = v.astype(BF16)
        o_heads = []
        for h in range(GLA_HEADS):
            def intra(qd_g, ki_g, v_g):
                att = jnp.where(causal, _dot_nt(qd_g, ki_g), 0.0)
                return _dot(att.astype(BF16), v_g)

            o_heads.append(_by_group(intra, jnp.where(masks[h], qd, 0.0).astype(BF16), ki_b, v_b[:, _head_cols(h)]))
        st = st_ref[...]
        states = []
        for c in range(GLA_CHUNKS_PER_STEP):
            states.append(st)
            sprev_ref[c] = st
            inc = _diag_blocks(_dot_tn(_chunk_rows_of(v_b, c), _chunk_rows_of(ke, c).astype(BF16)), masks)
            st = st * ebl[c] + inc
        st_ref[...] = st
        inter = []
        for c in range(GLA_CHUNKS_PER_STEP):
            qd_c = _stack_masked(_chunk_rows_of(qd, c), masks).astype(BF16)
            got = _dot_nt(qd_c, states[c].astype(BF16))
            inter.append(jnp.concatenate([got[h * GLA_CHUNK:(h + 1) * GLA_CHUNK] for h in range(GLA_HEADS)], axis=1))
        o = jnp.concatenate(o_heads, axis=1) + jnp.concatenate(inter, axis=0)
        opre_ref[...] = o
        on = jnp.concatenate([o[:, _head_cols(h)] * _rstd(o[:, _head_cols(h)]) for h in range(GLA_HEADS)], axis=1)
        oa_ref[...] = ((on * gn_ref[...]) * (r * _sigmoid(r))).astype(BF16)

    return pl.pallas_call(
        body,
        grid=(n_steps,),
        in_specs=[
            pl.BlockSpec((GLA_ROWS, P_A), lambda i: (i, 0)),
            pl.BlockSpec((LANE, GLA_QK), lambda i: (0, 0)),
            pl.BlockSpec((1, GLA_QK), lambda i: (0, 0)),
            pl.BlockSpec((1, GLA_WIDTH), lambda i: (0, 0)),
        ],
        out_specs=[
            pl.BlockSpec((GLA_ROWS, GLA_WIDTH), lambda i: (i, 0)),
            pl.BlockSpec((GLA_ROWS, GLA_WIDTH), lambda i: (i, 0)),
            pl.BlockSpec((GLA_CHUNKS_PER_STEP, GLA_DV, GLA_QK), lambda i: (i, 0, 0)),
        ],
        out_shape=[
            jax.ShapeDtypeStruct((T, GLA_WIDTH), BF16),
            jax.ShapeDtypeStruct((T, GLA_WIDTH), F32),
            jax.ShapeDtypeStruct((n_chunks, GLA_DV, GLA_QK), F32),
        ],
        scratch_shapes=[pltpu.VMEM((GLA_DV, GLA_QK), F32)],
        compiler_params=_params(),
        name="gla_fwd",
    )(proj, w2p, bg, gn)


def gla_bwd(proj, w2p, bg, gn, opre, sprev, dmixed, exchange=None):
    T = proj.shape[0]
    n_steps = T // GLA_ROWS

    def body(*refs):
        refs = _host_exchange(exchange, refs, 7, 4, pl.program_id(0), n_steps)
        proj_ref, w2_ref, bg_ref, gn_ref, opre_ref, sprev_ref, doa_ref, da_ref, dw2_ref, dbg_ref, dgn_ref, dst_ref = refs

        @pl.when(pl.program_id(0) == 0)
        def _():
            dst_ref[...] = jnp.zeros_like(dst_ref)
            dw2_ref[...] = jnp.zeros_like(dw2_ref)
            dbg_ref[...] = jnp.zeros_like(dbg_ref)
            dgn_ref[...] = jnp.zeros_like(dgn_ref)

        causal, causal_t, masks = _gla_step_constants()
        w2 = w2_ref[...].astype(BF16)
        gn = gn_ref[...]
        q = proj_ref[:, P_GQ:P_GQ + GLA_QK]
        k = proj_ref[:, P_GK:P_GK + GLA_QK]
        v_b = proj_ref[:, P_GV:P_GV + GLA_WIDTH].astype(BF16)
        r = proj_ref[:, P_GR:P_GR + GLA_WIDTH]
        glow = proj_ref[:, P_GLOW:P_GLOW + LANE].astype(BF16)
        o = opre_ref[...]
        doa = doa_ref[...]
        gpre, eb, enb, eke, ebl, qd, ki, ke = _gla_step_common(q, k, glow, w2, bg_ref[...], causal.astype(BF16))
        sig = _sigmoid(r)
        rs = jnp.concatenate([jnp.broadcast_to(_rstd(o[:, _head_cols(h)]), (GLA_ROWS, GLA_DV)) for h in range(GLA_HEADS)], axis=1)
        on = o * rs
        d_ong = doa * (r * sig)
        dr = doa * (on * gn) * (sig * (1.0 + r * (1.0 - sig)))
        dgn_ref[...] += jnp.sum(d_ong * on, axis=0, keepdims=True)
        d_on = d_ong * gn
        t = d_on * on
        mean_t = jnp.concatenate([jnp.broadcast_to(jnp.mean(t[:, _head_cols(h)], axis=-1, keepdims=True), (GLA_ROWS, GLA_DV))
                                  for h in range(GLA_HEADS)], axis=1)
        do_b = (rs * (d_on - on * mean_t)).astype(BF16)
        ki_b = ki.astype(BF16)
        ke_b = ke.astype(BF16)
        dqd = jnp.zeros_like(qd)
        dki = jnp.zeros_like(qd)
        dv_heads = []
        for h in range(GLA_HEADS):
            qd_h = jnp.where(masks[h], qd, 0.0).astype(BF16)
            do_h = do_b[:, _head_cols(h)]

            def intra(qd_g, ki_g, v_g, do_g):
                att = jnp.where(causal, _dot_nt(qd_g, ki_g), 0.0).astype(BF16)
                d_att = jnp.where(causal, _dot_nt(do_g, v_g), 0.0).astype(BF16)
                return _dot_tn(att, do_g), _dot(d_att, ki_g), _dot_tn(d_att, qd_g)

            dv_h, dqd_h, dki_h = _by_group(intra, qd_h, ki_b, v_b[:, _head_cols(h)], do_h)
            dv_heads.append(dv_h)
            dqd = dqd + jnp.where(masks[h], dqd_h, 0.0)
            dki = dki + dki_h
        states = [sprev_ref[c] for c in range(GLA_CHUNKS_PER_STEP)]
        dqd_inter, dst_adds = [], []
        for c in range(GLA_CHUNKS_PER_STEP):
            do_c = _stack_head_cols(_chunk_rows_of(do_b, c))
            dqd_inter.append(_row_blocks_masked(_dot(do_c, states[c].astype(BF16)), masks))
            dst_adds.append(_diag_blocks(_dot_tn(_chunk_rows_of(do_b, c), _chunk_rows_of(qd, c).astype(BF16)), masks))
        dst = dst_ref[...]
        dsts, debls = [None] * GLA_CHUNKS_PER_STEP, [None] * GLA_CHUNKS_PER_STEP
        for c in reversed(range(GLA_CHUNKS_PER_STEP)):
            dsts[c] = dst
            debls[c] = jnp.sum(dst * states[c], axis=0, keepdims=True)
            dst = dst * ebl[c] + dst_adds[c]
        dst_ref[...] = dst
        dv_inter, dke = [], []
        for c in range(GLA_CHUNKS_PER_STEP):
            dst_b = dsts[c].astype(BF16)
            got = _dot_nt(_stack_masked(_chunk_rows_of(ke, c), masks).astype(BF16), dst_b)
            dv_inter.append(jnp.concatenate([got[h * GLA_CHUNK:(h + 1) * GLA_CHUNK] for h in range(GLA_HEADS)], axis=1))
            dke.append(_row_blocks_masked(_dot(_stack_head_cols(_chunk_rows_of(v_b, c)), dst_b), masks))
        dqd = dqd + jnp.concatenate(dqd_inter, axis=0)
        dke = jnp.concatenate(dke, axis=0)
        dv = jnp.concatenate(dv_heads, axis=1) + jnp.concatenate(dv_inter, axis=0)
        dkk = dke * ke
        dbl = jnp.sum(_per_chunk(dkk), axis=1, keepdims=True) + jnp.stack(debls) * ebl
        last_row = lax.broadcasted_iota(jnp.int32, (GLA_CHUNKS_PER_STEP, GLA_CHUNK, GLA_QK), 1) == GLA_CHUNK - 1
        db = dqd * qd - dki * ki - dkk + jnp.where(last_row, dbl, 0.0).reshape(GLA_ROWS, GLA_QK)
        tri_t = causal_t.astype(BF16)
        dglog = _by_group(lambda g: _dot_exact_lhs(tri_t, g), db)
        dgpre = (dglog / GLA_TAU) * _sigmoid(-gpre)
        dgpre_b = dgpre.astype(BF16)
        da_ref[...] = jnp.concatenate(
            [dqd * eb * QK_SCALE, dki * enb + dke * eke, dv, dr, _dot_nt(dgpre_b, w2)], axis=1).astype(BF16)
        dw2_ref[...] += _dot_tn(glow, dgpre_b)
        dbg_ref[...] += jnp.sum(dgpre, axis=0, keepdims=True)

    rev = lambda i: (n_steps - 1 - i, 0)
    return _hosted_call(
        exchange,
        body,
        grid=(n_steps,),
        in_specs=[
            pl.BlockSpec((GLA_ROWS, P_A), rev),
            pl.BlockSpec((LANE, GLA_QK), lambda i: (0, 0)),
            pl.BlockSpec((1, GLA_QK), lambda i: (0, 0)),
            pl.BlockSpec((1, GLA_WIDTH), lambda i: (0, 0)),
            pl.BlockSpec((GLA_ROWS, GLA_WIDTH), rev),
            pl.BlockSpec((GLA_CHUNKS_PER_STEP, GLA_DV, GLA_QK), lambda i: (n_steps - 1 - i, 0, 0)),
            pl.BlockSpec((GLA_ROWS, GLA_WIDTH), rev),
        ],
        out_specs=[
            pl.BlockSpec((GLA_ROWS, P_A), rev),
            pl.BlockSpec((LANE, GLA_QK), lambda i: (0, 0)),
            pl.BlockSpec((1, GLA_QK), lambda i: (0, 0)),
            pl.BlockSpec((1, GLA_WIDTH), lambda i: (0, 0)),
        ],
        out_shape=[
            jax.ShapeDtypeStruct((T, P_A), BF16),
            jax.ShapeDtypeStruct((LANE, GLA_QK), F32),
            jax.ShapeDtypeStruct((1, GLA_QK), F32),
            jax.ShapeDtypeStruct((1, GLA_WIDTH), F32),
        ],
        scratch_shapes=[pltpu.VMEM((GLA_DV, GLA_QK), F32)],
        compiler_params=_params(),
        name="gla_bwd",
        args=(proj, w2p, bg, gn, opre, sprev, dmixed),
    )


def _t5_bucket(dist):
    max_exact = REL_BUCKETS // 2
    n = np.maximum(dist, 0)
    large = max_exact + (np.log(np.maximum(n, 1) / max_exact) / math.log(REL_MAX_DIST / max_exact)
                         * (REL_BUCKETS - max_exact)).astype(np.int32)
    large = np.minimum(large, REL_BUCKETS - 1)
    return np.where(n < max_exact, n, large).astype(np.int32)


SUBLANES = 8


def _bucket_rows():
    steps = DSA_BLOCK - np.arange(2 * DSA_BLOCK)
    in_band = (steps >= 0) & (steps <= DSA_SPAN)
    rows = np.stack([np.where(in_band, _t5_bucket(steps * d), -1) for d in DSA_DILATIONS]).astype(np.int32)
    return np.broadcast_to(rows[:, None, :], (len(DSA_DILATIONS), SUBLANES, 2 * DSA_BLOCK)).copy()


def bias_tables(rel_bias):
    ids = jnp.asarray(_bucket_rows())
    nd = len(DSA_DILATIONS)

    def body(rel_ref, ids_ref, tab_ref):
        idt = ids_ref[0]
        for h in range(DSA_HEADS):
            row = jnp.where(idt < 0, NEG, 0.0).astype(F32)
            for b in range(REL_BUCKETS):
                row = jnp.where(idt == b, rel_ref[b, h], row)
            full = jnp.broadcast_to(row[0:1], (DSA_BLOCK, 2 * DSA_BLOCK))
            tab_ref[0, h] = pltpu.roll(full, 0, 1, stride=1, stride_axis=0)

    return pl.pallas_call(
        body,
        grid=(nd,),
        in_specs=[pl.BlockSpec(memory_space=pltpu.SMEM), pl.BlockSpec((1, SUBLANES, 2 * DSA_BLOCK), lambda d: (d, 0, 0))],
        out_specs=pl.BlockSpec((1, DSA_HEADS, DSA_BLOCK, 2 * DSA_BLOCK), lambda d: (d, 0, 0, 0)),
        out_shape=jax.ShapeDtypeStruct((nd, DSA_HEADS, DSA_BLOCK, 2 * DSA_BLOCK), F32),
        compiler_params=_params(),
        name="bias_tables",
    )(rel_bias, ids)


def _bucket_ids():
    L = DSA_BLOCK
    steps = L + np.arange(L)[:, None] - np.arange(2 * L)[None, :]
    in_band = (steps >= 0) & (steps <= DSA_SPAN)
    return np.stack([np.where(in_band, _t5_bucket(steps * d), -1) for d in DSA_DILATIONS]).astype(np.int32)


def bias_tables_bwd(dtab):
    ids = jnp.asarray(_bucket_ids())
    nd = len(DSA_DILATIONS)

    def body(dtab_ref, ids_ref, drel_ref):
        @pl.when(pl.program_id(0) == 0)
        def _():
            for b in range(REL_BUCKETS):
                for h in range(DSA_HEADS):
                    drel_ref[b, h] = 0.0

        idt = ids_ref[0]
        for b in range(REL_BUCKETS):
            in_bucket = idt == b
            for h in range(DSA_HEADS):
                drel_ref[b, h] += jnp.sum(jnp.where(in_bucket, dtab_ref[0, h], 0.0))

    return pl.pallas_call(
        body,
        grid=(nd,),
        in_specs=[pl.BlockSpec((1, DSA_HEADS, DSA_BLOCK, 2 * DSA_BLOCK), lambda d: (d, 0, 0, 0)),
                  pl.BlockSpec((1, DSA_BLOCK, 2 * DSA_BLOCK), lambda d: (d, 0, 0))],
        out_specs=pl.BlockSpec(memory_space=pltpu.SMEM),
        out_shape=jax.ShapeDtypeStruct((REL_BUCKETS, DSA_HEADS), F32),
        compiler_params=_params(),
        name="bias_tables_bwd",
    )(dtab, ids)


DSA_PAIRS = DSA_HEADS // 2
DSA_UNROLL = 16
DSA_COMBINE_ROWS = 256


def _dsa_units(d):
    return d, DSA_SUPER // (DSA_BLOCK * d)


def _dsa_specs(T):
    nsb = T // DSA_SUPER
    qcol, kcol, vcol = P_DQ // LANE, P_DK // LANE, P_DV // LANE
    return nsb, qcol, kcol, vcol


def _head_lane_mask():
    return lax.broadcasted_iota(jnp.int32, (1, LANE), 1) < DSA_DH


def _fill_tile_variants(tab_ref, variants):
    col = lax.broadcasted_iota(jnp.int32, (2 * DSA_BLOCK, 2 * DSA_BLOCK), 1)
    for di in range(len(DSA_DILATIONS)):
        tile = tab_ref[di, 0]
        variants[di, 0] = tile
        variants[di, 1] = jnp.where(col < DSA_BLOCK, NEG, tile)


def _tile_variants_scratch():
    return pltpu.VMEM((len(DSA_DILATIONS), 2, 2 * DSA_BLOCK, 2 * DSA_BLOCK), F32)


def _pair_tiles(tab):
    return tab.reshape(len(DSA_DILATIONS), DSA_PAIRS, 2 * DSA_BLOCK, 2 * DSA_BLOCK)


def _stack_heads(t, head0):
    return jnp.concatenate([jnp.where(head0, t, 0.0), jnp.where(head0, 0.0, t)], axis=0)


def dsa_fwd(proj, tab, exchange=None):
    T = proj.shape[0]
    nsb, qcol, kcol, vcol = _dsa_specs(T)
    S = DSA_SUPER

    def body(*refs):
        refs = _host_exchange(exchange, refs, 6, 2, pl.program_id(0) * nsb + pl.program_id(1), DSA_PAIRS * nsb)
        q_ref, kp_ref, kc_ref, vp_ref, vc_ref, tab_ref, out_ref, lse_ref, kk, vv, ob, lb, tiles = refs
        sb = pl.program_id(1)
        kk[0:S, :] = kp_ref[...]
        kk[S:2 * S, :] = kc_ref[...]
        vv[0:S, :] = vp_ref[...]
        vv[S:2 * S, :] = vc_ref[...]
        head0 = _head_lane_mask()
        pl.when(sb == 0)(functools.partial(_fill_tile_variants, tab_ref, tiles))

        for di, d in enumerate(DSA_DILATIONS):
            n_res, n_blk = _dsa_units(d)

            def unit(u, carry, di=di, d=d, n_blk=n_blk):
                r = u // n_blk
                c = u % n_blk
                q0 = r + d * DSA_BLOCK * c
                qrows = pl.ds(q0, DSA_BLOCK, stride=d) if d > 1 else pl.ds(q0, DSA_BLOCK)
                krows = pl.ds(S + q0 - d * DSA_BLOCK, 2 * DSA_BLOCK, stride=d) if d > 1 else pl.ds(S + q0 - DSA_BLOCK, 2 * DSA_BLOCK)
                q2 = q_ref[qrows, :] * QK_SCALE
                k2 = kk[krows, :].astype(BF16)
                v2 = vv[krows, :].astype(BF16)
                qs = _stack_heads(q2, head0).astype(BF16)
                s = _dot_nt(qs, k2) + tiles[di, ((sb == 0) & (c == 0)).astype(jnp.int32)]
                m = jnp.max(s, axis=-1, keepdims=True)
                p = jnp.exp(s - m)
                den = jnp.sum(p, axis=-1, keepdims=True)
                o = _dot(p.astype(BF16), v2) / den
                l = jnp.broadcast_to(m + jnp.log(den), (2 * DSA_BLOCK, LANE))
                ob[di, qrows, :] = jnp.where(head0, o[:DSA_BLOCK], o[DSA_BLOCK:])
                lb[di, qrows, :] = jnp.where(head0, l[:DSA_BLOCK], l[DSA_BLOCK:])
                return carry

            lax.fori_loop(0, n_res * n_blk, unit, 0, unroll=DSA_UNROLL)

        def combine(i, carry):
            rows = pl.ds(pl.multiple_of(i * DSA_COMBINE_ROWS, DSA_COMBINE_ROWS), DSA_COMBINE_ROWS)
            l0, l1, l2 = lb[0, rows, :], lb[1, rows, :], lb[2, rows, :]
            mx = jnp.maximum(jnp.maximum(l0, l1), l2)
            e0, e1, e2 = jnp.exp(l0 - mx), jnp.exp(l1 - mx), jnp.exp(l2 - mx)
            den = e0 + e1 + e2
            out_ref[rows, :] = (e0 * ob[0, rows, :] + e1 * ob[1, rows, :] + e2 * ob[2, rows, :]) / den
            lse_ref[rows, :] = mx + jnp.log(den)
            return carry

        lax.fori_loop(0, S // DSA_COMBINE_ROWS, combine, 0)

    prev = lambda col: (lambda hp, sb: (jnp.maximum(sb - 1, 0), col + hp))
    cur = lambda col: (lambda hp, sb: (sb, col + hp))
    blk = lambda f: pl.BlockSpec((S, LANE), f)
    return _hosted_call(
        exchange,
        body,
        grid=(DSA_PAIRS, nsb),
        in_specs=[blk(cur(qcol)), blk(prev(kcol)), blk(cur(kcol)), blk(prev(vcol)), blk(cur(vcol)),
                  pl.BlockSpec((len(DSA_DILATIONS), 1, 2 * DSA_BLOCK, 2 * DSA_BLOCK), lambda hp, sb: (0, hp, 0, 0))],
        out_specs=[blk(lambda hp, sb: (sb, hp)), blk(lambda hp, sb: (sb, hp))],
        out_shape=[jax.ShapeDtypeStruct((T, DSA_WIDTH), F32), jax.ShapeDtypeStruct((T, DSA_WIDTH), F32)],
        scratch_shapes=[pltpu.VMEM((2 * S, LANE), F32), pltpu.VMEM((2 * S, LANE), F32),
                        pltpu.VMEM((len(DSA_DILATIONS), S, LANE), F32), pltpu.VMEM((len(DSA_DILATIONS), S, LANE), F32),
                        _tile_variants_scratch()],
        compiler_params=_params(("arbitrary", "arbitrary")),
        name="dsa_fwd",
        args=(proj, proj, proj, proj, proj, _pair_tiles(tab)),
    )


def dsa_bwd(proj, tab, ob_out, lse, dmixed, exchange=None):
    T = proj.shape[0]
    nsb, qcol, kcol, vcol = _dsa_specs(T)
    S = DSA_SUPER
    nd = len(DSA_DILATIONS)
    ocol = GLA_WIDTH // LANE

    def body(*refs):
        refs = _host_exchange(exchange, refs, 9, 4, pl.program_id(0) * nsb + pl.program_id(1), DSA_PAIRS * nsb)
        (q_ref, kp_ref, kc_ref, vp_ref, vc_ref, tab_ref, o_ref, lse_ref, do_ref,
         dq_ref, dk_ref, dv_ref, dtab_ref, kk, vv, dqa, dkk, dvv, tiles) = refs
        j = pl.program_id(1)
        sb = nsb - 1 - j
        kk[0:S, :] = kp_ref[...]
        kk[S:2 * S, :] = kc_ref[...]
        vv[0:S, :] = vp_ref[...]
        vv[S:2 * S, :] = vc_ref[...]
        head0 = _head_lane_mask()
        pl.when(j == 0)(functools.partial(_fill_tile_variants, tab_ref, tiles))

        @pl.when(j == 0)
        def _():
            dtab_ref[...] = jnp.zeros_like(dtab_ref)
            dkk[S:2 * S, :] = jnp.zeros((S, LANE), F32)
            dvv[S:2 * S, :] = jnp.zeros((S, LANE), F32)

        @pl.when(j > 0)
        def _():
            dkk[S:2 * S, :] = dkk[0:S, :]
            dvv[S:2 * S, :] = dvv[0:S, :]

        dkk[0:S, :] = jnp.zeros((S, LANE), F32)
        dvv[0:S, :] = jnp.zeros((S, LANE), F32)
        dqa[...] = jnp.zeros_like(dqa)

        for di, d in enumerate(DSA_DILATIONS):
            n_res, n_blk = _dsa_units(d)

            def unit(u, carry, di=di, d=d, n_blk=n_blk):
                r = u // n_blk
                c = u % n_blk
                q0 = r + d * DSA_BLOCK * c
                qrows = pl.ds(q0, DSA_BLOCK, stride=d) if d > 1 else pl.ds(q0, DSA_BLOCK)
                krows = pl.ds(S + q0 - d * DSA_BLOCK, 2 * DSA_BLOCK, stride=d) if d > 1 else pl.ds(S + q0 - DSA_BLOCK, 2 * DSA_BLOCK)
                q2 = q_ref[qrows, :] * QK_SCALE
                k2 = kk[krows, :].astype(BF16)
                v2 = vv[krows, :].astype(BF16)
                do2 = do_ref[qrows, :]
                o2 = o_ref[qrows, :]
                l2 = lse_ref[qrows, :]
                qs = _stack_heads(q2, head0).astype(BF16)
                dos = _stack_heads(do2, head0)
                dos_b = dos.astype(BF16)
                delta = jnp.sum(dos * jnp.concatenate([o2, o2], axis=0), axis=-1, keepdims=True)
                lse = jnp.concatenate([jnp.max(jnp.where(head0, l2, -jnp.inf), axis=-1, keepdims=True),
                                       jnp.max(jnp.where(head0, -jnp.inf, l2), axis=-1, keepdims=True)], axis=0)
                s = _dot_nt(qs, k2) + tiles[di, ((sb == 0) & (c == 0)).astype(jnp.int32)]
                p = jnp.exp(s - lse)
                ds = p * (_dot_nt(dos_b, v2) - delta)
                dtab_ref[di, 0] += ds
                ds_b = ds.astype(BF16)
                dq = _dot(ds_b, k2)
                dqa[qrows, :] += jnp.where(head0, dq[:DSA_BLOCK], dq[DSA_BLOCK:]) * QK_SCALE
                dkk[krows, :] += _dot_tn(ds_b, qs)
                dvv[krows, :] += _dot_tn(p.astype(BF16), dos_b)
                return carry

            lax.fori_loop(0, n_res * n_blk, unit, 0, unroll=DSA_UNROLL)

        dq_ref[...] = dqa[...].astype(BF16)
        dk_ref[...] = dkk[S:2 * S, :].astype(BF16)
        dv_ref[...] = dvv[S:2 * S, :].astype(BF16)

    prev = lambda col: (lambda hp, j: (jnp.maximum(nsb - 2 - j, 0), col + hp))
    cur = lambda col: (lambda hp, j: (nsb - 1 - j, col + hp))
    blk = lambda f: pl.BlockSpec((S, LANE), f)
    out_blk = blk(lambda hp, j: (nsb - 1 - j, hp))
    tab_blk = pl.BlockSpec((nd, 1, 2 * DSA_BLOCK, 2 * DSA_BLOCK), lambda hp, j: (0, hp, 0, 0))
    dq, dk, dv, dtab, *carried = _hosted_call(
        exchange,
        body,
        grid=(DSA_PAIRS, nsb),
        in_specs=[blk(cur(qcol)), blk(prev(kcol)), blk(cur(kcol)), blk(prev(vcol)), blk(cur(vcol)), tab_blk,
                  out_blk, out_blk, blk(cur(ocol))],
        out_specs=[out_blk, out_blk, out_blk, tab_blk],
        out_shape=[jax.ShapeDtypeStruct((T, DSA_WIDTH), BF16)] * 3
        + [jax.ShapeDtypeStruct((nd, DSA_PAIRS, 2 * DSA_BLOCK, 2 * DSA_BLOCK), F32)],
        scratch_shapes=[pltpu.VMEM((2 * S, LANE), F32), pltpu.VMEM((2 * S, LANE), F32), pltpu.VMEM((S, LANE), F32),
                        pltpu.VMEM((2 * S, LANE), F32), pltpu.VMEM((2 * S, LANE), F32), _tile_variants_scratch()],
        compiler_params=_params(("arbitrary", "arbitrary")),
        name="dsa_bwd",
        args=(proj, proj, proj, proj, proj, _pair_tiles(tab), ob_out, lse, dmixed),
    )
    return (dq, dk, dv, dtab.reshape(nd, DSA_HEADS, DSA_BLOCK, 2 * DSA_BLOCK), *carried)


FF_BLOCKS = 4
FF_BLOCK = D_FF // FF_BLOCKS


def post_fused(x, oa, ob, tgt, g2, gf, wout, wff1, wff2):
    T = x.shape[0]
    tm = 256
    inv_d = 1.0 / D_MODEL

    def body(x_ref, oa_ref, ob_ref, tgt_ref, g2_ref, gf_ref, wout_hbm, wff1_hbm, wff2_hbm,
             mixed_ref, nm_ref, a_ref, dpre_ref, dh2_ref, dh1_ref, dmixed_ref, loss_ref, dgf_ref, dg2_ref,
             wout_v, wff1_v, wff2_v, sems):
        @pl.when(pl.program_id(0) == 0)
        def _():
            cps = [pltpu.make_async_copy(s, d, sems.at[i])
                   for i, (s, d) in enumerate([(wout_hbm, wout_v), (wff1_hbm, wff1_v), (wff2_hbm, wff2_v)])]
            for cp in cps:
                cp.start()
            for cp in cps:
                cp.wait()
            loss_ref[...] = jnp.zeros_like(loss_ref)
            dgf_ref[...] = jnp.zeros_like(dgf_ref)
            dg2_ref[...] = jnp.zeros_like(dg2_ref)

        mixed = jnp.concatenate([oa_ref[...], ob_ref[...].astype(BF16)], axis=1)
        mixed_ref[...] = mixed
        h1 = x_ref[...] + _dot(mixed, wout_v[...])
        rs1 = _rstd(h1)
        hn1 = h1 * rs1
        g2 = g2_ref[...]
        nm = (hn1 * g2).astype(BF16)
        nm_ref[...] = nm
        relu = []
        mlp = jnp.zeros((tm, D_MODEL), F32)
        for j in range(FF_BLOCKS):
            cols = slice(j * FF_BLOCK, (j + 1) * FF_BLOCK)
            r_j = jnp.maximum(_dot(nm, wff1_v[j]), 0.0)
            a_j = (r_j * r_j).astype(BF16)
            a_ref[:, cols] = a_j
            relu.append(r_j)
            mlp = mlp + _dot(a_j, wff2_v[cols, :])
        h2 = h1 + mlp
        rsf = _rstd(h2)
        hnf = h2 * rsf
        gf = gf_ref[...]
        diff = hnf * gf - tgt_ref[...]
        loss_ref[...] += 0.5 * jnp.sum(jnp.sum(diff * diff, axis=-1, keepdims=True) * inv_d, axis=0, keepdims=True)
        dy = diff * inv_d
        dgf_ref[...] += jnp.sum(dy * hnf, axis=0, keepdims=True)
        dhnf = dy * gf
        dh2 = rsf * (dhnf - hnf * jnp.mean(dhnf * hnf, axis=-1, keepdims=True))
        dh2_b = dh2.astype(BF16)
        dh2_ref[...] = dh2_b
        dnm = jnp.zeros((tm, D_MODEL), F32)
        for j in range(FF_BLOCKS):
            cols = slice(j * FF_BLOCK, (j + 1) * FF_BLOCK)
            dpre_j = (_dot_nt(dh2_b, wff2_v[cols, :]) * (2.0 * relu[j])).astype(BF16)
            dpre_ref[:, cols] = dpre_j
            dnm = dnm + _dot_nt(dpre_j, wff1_v[j])
        dg2_ref[...] += jnp.sum(dnm * hn1, axis=0, keepdims=True)
        dhn1 = dnm * g2
        dh1 = dh2 + rs1 * (dhn1 - hn1 * jnp.mean(dhn1 * hn1, axis=-1, keepdims=True))
        dh1_ref[...] = dh1
        dmixed_ref[...] = _dot_nt(dh1.astype(BF16), wout_v[...])

    row = lambda w: pl.BlockSpec((tm, w), lambda i: (i, 0))
    vec = lambda w: pl.BlockSpec((1, w), lambda i: (0, 0))
    return pl.pallas_call(
        body,
        grid=(T // tm,),
        in_specs=[row(D_MODEL), row(GLA_WIDTH), row(DSA_WIDTH), row(D_MODEL), vec(D_MODEL), vec(D_MODEL), ANY, ANY, ANY],
        out_specs=[row(D_MODEL), row(D_MODEL), row(D_FF), row(D_FF), row(D_MODEL), row(D_MODEL), row(D_MODEL),
                   vec(1), vec(D_MODEL), vec(D_MODEL)],
        out_shape=[
            jax.ShapeDtypeStruct((T, D_MODEL), BF16),
            jax.ShapeDtypeStruct((T, D_MODEL), BF16),
            jax.ShapeDtypeStruct((T, D_FF), BF16),
            jax.ShapeDtypeStruct((T, D_FF), BF16),
            jax.ShapeDtypeStruct((T, D_MODEL), BF16),
            jax.ShapeDtypeStruct((T, D_MODEL), F32),
            jax.ShapeDtypeStruct((T, D_MODEL), F32),
            jax.ShapeDtypeStruct((1, 1), F32),
            jax.ShapeDtypeStruct((1, D_MODEL), F32),
            jax.ShapeDtypeStruct((1, D_MODEL), F32),
        ],
        scratch_shapes=[pltpu.VMEM((D_MODEL, D_MODEL), BF16), pltpu.VMEM((FF_BLOCKS, D_MODEL, FF_BLOCK), BF16),
                        pltpu.VMEM((D_FF, D_MODEL), BF16), pltpu.SemaphoreType.DMA((3,))],
        compiler_params=_params(),
        name="post_fused",
    )(x, oa, ob, tgt, g2, gf, wout, wff1, wff2)


WGRAD_TOKENS = 2048


def wgrad(a, b, name, bm=None, bn=None, col_blocked=False):
    T, M = a.shape
    N = b.shape[1]
    bm = M if bm is None else bm
    bn = N if bn is None else bn
    tk = min(WGRAD_TOKENS, T)
    n_k = T // tk

    def body(a_ref, b_ref, o_ref, acc_ref):
        part = _dot_tn(a_ref[...].astype(BF16), b_ref[...].astype(BF16))
        out = o_ref.at[0] if col_blocked else o_ref
        k = pl.program_id(2)
        if n_k == 1:
            out[...] = part.astype(BF16)
            return

        @pl.when(k == 0)
        def _():
            acc_ref[...] = part

        @pl.when((k > 0) & (k < n_k - 1))
        def _():
            acc_ref[...] += part

        @pl.when(k == n_k - 1)
        def _():
            out[...] = (acc_ref[...] + part).astype(BF16)

    if col_blocked:
        assert bm == M
        out_spec = pl.BlockSpec((1, M, bn), lambda i, j, k: (j, 0, 0))
        out_shape = jax.ShapeDtypeStruct((N // bn, M, bn), BF16)
    else:
        out_spec = pl.BlockSpec((bm, bn), lambda i, j, k: (i, j))
        out_shape = jax.ShapeDtypeStruct((M, N), BF16)
    return pl.pallas_call(
        body,
        grid=(M // bm, N // bn, n_k),
        in_specs=[pl.BlockSpec((tk, bm), lambda i, j, k: (k, i)), pl.BlockSpec((tk, bn), lambda i, j, k: (k, j))],
        out_specs=out_spec,
        out_shape=out_shape,
        scratch_shapes=[pltpu.VMEM((bm, bn), F32)],
        compiler_params=_params(("arbitrary", "arbitrary", "arbitrary")),
        name=name,
    )(a, b)


def wgrad_cat(a, bs, name):
    T, M = a.shape
    widths = [b.shape[1] for b in bs]
    starts = [sum(widths[:i]) for i in range(len(bs))]
    N = sum(widths)
    tk = min(WGRAD_TOKENS // 2, T)
    n_k = T // tk

    def body(a_ref, *rest):
        b_refs, o_ref, acc_ref = rest[:len(bs)], rest[len(bs)], rest[len(bs) + 1]
        k = pl.program_id(0)

        @pl.when(k == 0)
        def _():
            acc_ref[...] = jnp.zeros_like(acc_ref)

        a_t = a_ref[...]
        for b_ref, start, width in zip(b_refs, starts, widths):
            acc_ref[:, start:start + width] += _dot_tn(a_t, b_ref[...])

        @pl.when(k == n_k - 1)
        def _():
            o_ref[...] = acc_ref[...].astype(BF16)

    return pl.pallas_call(
        body,
        grid=(n_k,),
        in_specs=[pl.BlockSpec((tk, M), lambda k: (k, 0))] + [pl.BlockSpec((tk, w), lambda k: (k, 0)) for w in widths],
        out_specs=pl.BlockSpec((M, N), lambda k: (0, 0)),
        out_shape=jax.ShapeDtypeStruct((M, N), BF16),
        scratch_shapes=[pltpu.VMEM((M, N), F32)],
        compiler_params=_params(),
        name=name,
    )(a, *bs)


def dx_final(x, dh1, g1, da, dq, dk, dv, wp, exchange=None):
    T = x.shape[0]
    tm = 256

    def body(*refs):
        refs = _host_exchange(exchange, refs, 8, 2, pl.program_id(0), T // tm)
        x_ref, dh1_ref, g_ref, da_ref, dq_ref, dk_ref, dv_ref, w_hbm, dx_ref, dg_ref, w_vmem, sem = refs

        @pl.when(pl.program_id(0) == 0)
        def _():
            _load_once(w_hbm, w_vmem, sem)
            dg_ref[...] = jnp.zeros_like(dg_ref)

        dnx = (_dot_nt(da_ref[...], w_vmem[:, 0:P_A]) + _dot_nt(dq_ref[...], w_vmem[:, P_DQ:P_DQ + DSA_WIDTH])
               + _dot_nt(dk_ref[...], w_vmem[:, P_DK:P_DK + DSA_WIDTH]) + _dot_nt(dv_ref[...], w_vmem[:, P_DV:P_DV + DSA_WIDTH]))
        xf = x_ref[...]
        rs = _rstd(xf)
        hn = xf * rs
        dg_ref[...] += jnp.sum(dnx * hn, axis=0, keepdims=True)
        dhn = dnx * g_ref[...]
        dx_ref[...] = dh1_ref[...] + rs * (dhn - hn * jnp.mean(dhn * hn, axis=-1, keepdims=True))

    row = lambda w: pl.BlockSpec((tm, w), lambda i: (i, 0))
    vec = pl.BlockSpec((1, D_MODEL), lambda i: (0, 0))
    return _hosted_call(
        exchange,
        body,
        grid=(T // tm,),
        in_specs=[row(D_MODEL), row(D_MODEL), vec, row(P_A), row(DSA_WIDTH), row(DSA_WIDTH), row(DSA_WIDTH), ANY],
        out_specs=[row(D_MODEL), vec],
        out_shape=[jax.ShapeDtypeStruct((T, D_MODEL), F32), jax.ShapeDtypeStruct((1, D_MODEL), F32)],
        scratch_shapes=[pltpu.VMEM((D_MODEL, P_ALL), BF16), pltpu.SemaphoreType.DMA],
        compiler_params=_params(),
        name="dx_final",
        args=(x, dh1, g1, da, dq, dk, dv, wp),
    )


def adamw(w, g, m, v, name):
    R, C = w.shape
    br = 256 if R % 256 == 0 else R

    def body(w_ref, g_ref, m_ref, v_ref, d_ref, nm_ref, nv_ref):
        d_ref[...], nm_ref[...], nv_ref[...] = _adamw_math(w_ref[...], g_ref[...], m_ref[...], v_ref[...])

    spec = pl.BlockSpec((br, C), lambda i: (i, 0))
    return pl.pallas_call(
        body,
        grid=(R // br,),
        in_specs=[spec] * 4,
        out_specs=[spec] * 3,
        out_shape=[jax.ShapeDtypeStruct((R, C), F32)] * 3,
        compiler_params=_params(),
        name=name,
    )(w, g, m, v)


def _place():
    return lax.axis_index("x"), lax.axis_index("y"), lax.axis_index("c")


def _other_chips(x, y):
    return [(1 - x, y), (x, 1 - y), (1 - x, 1 - y)]


class Exchange:
    def __init__(self, kind, arrays):
        self.kind, self.arrays, self.n = kind, arrays, len(arrays)
        self.slots = 4 if kind == "gather" else 8

    def out_shapes(self):
        if self.kind == "gather":
            return [jax.ShapeDtypeStruct((4,) + s.shape, s.dtype) for s in self.arrays]
        return [jax.ShapeDtypeStruct((8,) + s.shape[1:], s.dtype) for s in self.arrays]

    def sems(self):
        return [pltpu.SemaphoreType.DMA((self.n, 19)), pltpu.SemaphoreType.DMA((self.n, 19))]

    def phases(self, ins, outs, send_sems, recv_sems):
        n, scatter = self.n, self.kind == "scatter"
        x, y, c = _place()
        me, sib = (x, y, c), (x, y, 1 - c)
        mine = 2 * x + y
        chips = _other_chips(x, y)
        own_pair = 18

        def region(a, slot, half):
            h = outs[a].shape[1] // 2
            return outs[a].at[slot, pl.ds(half * h, h)]

        def copy(a, k, slot, half, to, src=None):
            return pltpu.make_async_remote_copy(
                src_ref=region(a, slot, half) if src is None else src, dst_ref=region(a, slot, half),
                send_sem=send_sems.at[a, k], recv_sem=recv_sems.at[a, k], device_id=to, device_id_type=MESH)

        def over_ici(t, to_core, from_core):
            return 4 * t + 2 * to_core + from_core

        def passed_on(t, from_core):
            return 12 + 2 * t + from_core

        senders = [(t, cc) for t in range(3) for cc in ((0, 1) if scatter else (c,))]

        def slot_of(t, cc):
            cx, cy = chips[t]
            return 2 * (2 * cx + cy) + cc if scatter else 2 * cx + cy

        def first_copies():
            cps = []
            for a in range(n):
                h = outs[a].shape[1] // 2
                for t, (cx, cy) in enumerate(chips):
                    if scatter:
                        for half in (0, 1):
                            cps.append(copy(a, over_ici(t, half, c), 2 * mine + c, half, (cx, cy, half),
                                            src=ins[a].at[2 * cx + cy, pl.ds(half * h, h)]))
                    else:
                        cps.append(copy(a, over_ici(t, c, c), mine, c, (cx, cy, c), src=ins[a].at[pl.ds(c * h, h)]))
                if scatter:
                    cps.append(pltpu.make_async_remote_copy(
                        src_ref=ins[a].at[mine], dst_ref=outs[a].at[2 * mine + c], send_sem=send_sems.at[a, own_pair],
                        recv_sem=recv_sems.at[a, own_pair], device_id=sib, device_id_type=MESH))
            return cps

        def forward_copies():
            return [copy(a, passed_on(t, cc), slot_of(t, cc), c, sib) for a in range(n) for t, cc in senders]

        def start():
            for cp in first_copies():
                cp.start()

        def forward():
            fws = iter(forward_copies())
            for a in range(n):
                for t, cc in senders:
                    copy(a, over_ici(t, c, cc), slot_of(t, cc), c, me).wait_recv()
                    next(fws).start()

        def finish():
            for a in range(n):
                for t, cc in senders:
                    from_core = cc if scatter else 1 - c
                    copy(a, passed_on(t, from_core), slot_of(t, from_core), 1 - c, me).wait_recv()
                if scatter:
                    pltpu.make_async_remote_copy(
                        src_ref=ins[a].at[mine], dst_ref=outs[a].at[2 * mine + 1 - c], send_sem=send_sems.at[a, own_pair],
                        recv_sem=recv_sems.at[a, own_pair], device_id=me, device_id_type=MESH).wait_recv()
            for cp in first_copies() + forward_copies():
                cp.wait_send()

        return start, forward, finish

    def fill_own(self, outs):
        x, y, c = _place()
        if self.kind == "gather":
            return [lax.dynamic_update_index_in_dim(o, s, 2 * x + y, 0) for o, s in zip(outs, self.arrays)]
        return [lax.dynamic_update_index_in_dim(o, lax.dynamic_index_in_dim(s, 2 * x + y, 0, keepdims=False), 2 * (2 * x + y) + c, 0)
                for o, s in zip(outs, self.arrays)]

    def run(self, name):
        n = self.n

        def body(*refs):
            start, forward, finish = self.phases(refs[:n], refs[n:2 * n], *refs[2 * n:])
            start()
            forward()
            finish()

        outs = pl.pallas_call(
            body, in_specs=[ANY] * n, out_specs=[ANY] * n, out_shape=self.out_shapes(), scratch_shapes=self.sems(), name=name,
        )(*self.arrays)
        return self.fill_own(outs)


def _host_exchange(exchange, refs, n_in, n_out, step, n_steps):
    if exchange is None:
        return refs
    n = exchange.n
    own_in, ex_in = refs[:n_in], refs[n_in:n_in + n]
    own_out, ex_out = refs[n_in + n:n_in + n + n_out], refs[n_in + n + n_out:n_in + 2 * n + n_out]
    rest = refs[n_in + 2 * n + n_out:]
    start, forward, finish = exchange.phases(ex_in, ex_out, rest[-2], rest[-1])
    pl.when(step == 0)(start)
    pl.when(step == (2 * n_steps) // 3)(forward)
    pl.when(step == n_steps - 1)(finish)
    return own_in + own_out + rest[:-2]


def _hosted_call(exchange, body, *, grid, in_specs, out_specs, out_shape, scratch_shapes, compiler_params, name, args):
    if exchange is None:
        return pl.pallas_call(body, grid=grid, in_specs=in_specs, out_specs=out_specs, out_shape=out_shape,
                              scratch_shapes=scratch_shapes, compiler_params=compiler_params, name=name)(*args)
    n = exchange.n
    res = pl.pallas_call(
        body, grid=grid, in_specs=list(in_specs) + [ANY] * n, out_specs=list(out_specs) + [ANY] * n,
        out_shape=list(out_shape) + exchange.out_shapes(), scratch_shapes=list(scratch_shapes) + exchange.sems(),
        compiler_params=compiler_params, name=name)(*args, *exchange.arrays)
    return list(res[:len(out_shape)]) + [exchange.fill_own(res[len(out_shape):])]


def sum_slots(parts, name):
    S, R, C = parts.shape
    br = 128 if R % 128 == 0 else R

    def body(p_ref, o_ref):
        acc = p_ref[0].astype(F32)
        for s in range(1, S):
            acc = acc + p_ref[s].astype(F32)
        o_ref[...] = acc

    return pl.pallas_call(
        body,
        grid=(R // br,),
        in_specs=[pl.BlockSpec((S, br, C), lambda i: (0, i, 0))],
        out_specs=pl.BlockSpec((br, C), lambda i: (i, 0)),
        out_shape=jax.ShapeDtypeStruct((R, C), F32),
        compiler_params=_params(),
        name=name,
    )(parts)


def _adamw_math(w, g, m, v):
    m_new = ADAM_B1 * m + (1.0 - ADAM_B1) * g
    v_new = ADAM_B2 * v + (1.0 - ADAM_B2) * (g * g)
    m_hat = m_new / (1.0 - ADAM_B1 ** ADAM_STEP)
    v_hat = v_new / (1.0 - ADAM_B2 ** ADAM_STEP)
    return -ADAM_LR * (m_hat / (jnp.sqrt(v_hat) + ADAM_EPS) + ADAM_WD * w), m_new, v_new


def reduce_adamw(slots, w, m, v, name):
    S, R, C = slots.shape
    br = 128

    def body(p_ref, w_ref, m_ref, v_ref, g_ref, d_ref, nm_ref, nv_ref):
        g = p_ref[0].astype(F32)
        for s in range(1, S):
            g = g + p_ref[s].astype(F32)
        g_ref[...] = g
        d_ref[...], nm_ref[...], nv_ref[...] = _adamw_math(w_ref[...], g, m_ref[...], v_ref[...])

    spec = pl.BlockSpec((br, C), lambda i: (i, 0))
    return pl.pallas_call(
        body,
        grid=(R // br,),
        in_specs=[pl.BlockSpec((S, br, C), lambda i: (0, i, 0)), spec, spec, spec],
        out_specs=[spec] * 4,
        out_shape=[jax.ShapeDtypeStruct((R, C), F32)] * 4,
        compiler_params=_params(),
        name=name,
    )(slots, w, m, v)


SMALL_ROWS = 72


def gather_small(vec):
    def body(v_ref, o_ref, send_sems, recv_sems, local_sem):
        x, y, c = _place()
        flips = [(fx, fy, fc) for fx in (0, 1) for fy in (0, 1) for fc in (0, 1)][1:]

        def peer(f):
            return (1 - x if f[0] else x, 1 - y if f[1] else y, 1 - c if f[2] else c)

        slot = lambda p: 4 * p[0] + 2 * p[1] + p[2]
        own = pltpu.make_async_copy(v_ref, o_ref.at[slot((x, y, c))], local_sem)
        own.start()
        cps = [pltpu.make_async_remote_copy(
            src_ref=v_ref, dst_ref=o_ref.at[slot((x, y, c))], send_sem=send_sems.at[k], recv_sem=recv_sems.at[k],
            device_id=peer(f), device_id_type=MESH) for k, f in enumerate(flips)]
        for cp in cps:
            cp.start()
        for k, f in enumerate(flips):
            pltpu.make_async_remote_copy(
                src_ref=v_ref, dst_ref=o_ref.at[slot(peer(f))], send_sem=send_sems.at[k], recv_sem=recv_sems.at[k],
                device_id=(x, y, c), device_id_type=MESH).wait_recv()
        for cp in cps:
            cp.wait_send()
        own.wait()

    return pl.pallas_call(
        body,
        in_specs=[ANY],
        out_specs=ANY,
        out_shape=jax.ShapeDtypeStruct((8,) + vec.shape, vec.dtype),
        scratch_shapes=[pltpu.SemaphoreType.DMA((7,)), pltpu.SemaphoreType.DMA((7,)), pltpu.SemaphoreType.DMA],
        name="gather_small",
    )(vec)


GLOW_PAD = LANE - GLA_RANK


def kernel(x, attn_norm_g, w_in, gla_gate_w2, gla_gate_b, gla_norm_g, rel_bias, w_out, mlp_norm_g, w_ff1, w_ff2, final_norm_g, loss_target, m_attn_norm_g, m_w_in, m_gla_gate_w2, m_gla_gate_b, m_gla_norm_g, m_rel_bias, m_w_out, m_mlp_norm_g, m_w_ff1, m_w_ff2, m_final_norm_g, v_attn_norm_g, v_w_in, v_gla_gate_w2, v_gla_gate_b, v_gla_norm_g, v_rel_bias, v_w_out, v_mlp_norm_g, v_w_ff1, v_w_ff2, v_final_norm_g):
    xs, tgt = x[0], loss_target[0]
    T = xs.shape[0]
    cx, cy, _ = _place()
    chip = 2 * cx + cy
    gf = final_norm_g.reshape(1, D_MODEL)

    win_g, w2_g = Exchange("gather", [w_in[0].astype(BF16), gla_gate_w2[0]]).run("gather_w_in")
    win = jnp.transpose(win_g, (1, 0, 2)).reshape(D_MODEL, D_IN)
    n_glow = R_GLOW + GLA_RANK
    wp = jnp.concatenate([win[:, :n_glow], jnp.zeros((D_MODEL, GLOW_PAD), BF16), win[:, n_glow:]], axis=1)
    w2 = jnp.transpose(w2_g, (1, 0, 2)).reshape(GLA_RANK, GLA_QK)
    w2p = jnp.concatenate([w2, jnp.zeros((GLOW_PAD, GLA_QK), F32)], axis=0)

    proj, nx = inproj(xs, attn_norm_g, wp)
    tab = bias_tables(rel_bias)
    ob, lse, (wout_g, wff1, wff2_g) = dsa_fwd(
        proj, tab, Exchange("gather", [w_out[0].astype(BF16), w_ff1[0].astype(BF16), w_ff2[0].astype(BF16)]))
    wout = wout_g.reshape(D_MODEL, D_MODEL)
    wff2 = wff2_g.reshape(D_FF, D_MODEL)
    oa, opre, sprev = gla_fwd(proj, w2p, gla_gate_b, gla_norm_g)
    mixed, nm, act, dpre, dh2, dh1, dmixed, loss, dgf, dg2 = post_fused(xs, oa, ob, tgt, mlp_norm_g, gf, wout, wff1, wff2)

    late = [
        wgrad(mixed, dh1, "wgrad_out").reshape(4, D_MODEL // 4, D_MODEL),
        wgrad(nm, dpre, "wgrad_ff1", bn=FF_BLOCK, col_blocked=True),
        wgrad(act, dh2, "wgrad_ff2", bm=FF_BLOCK).reshape(4, FF_BLOCK, D_MODEL),
    ]
    da, dw2p, dbg, dgn = gla_bwd(proj, w2p, gla_gate_b, gla_norm_g, opre, sprev, dmixed)
    dq, dk, dv, dtab, late_slots = dsa_bwd(proj, tab, ob, lse, dmixed, Exchange("scatter", late))
    slots = dict(zip(["w_out", "w_ff1", "w_ff2"], late_slots))
    drel = bias_tables_bwd(dtab)

    dwp = wgrad_cat(nx, [da, dq, dk, dv], "wgrad_in")
    dwin = jnp.concatenate([dwp[:, :n_glow], dwp[:, P_A:]], axis=1)
    dwin = [jnp.transpose(dwin.reshape(D_MODEL, 4, D_IN // 4), (1, 0, 2))]
    dxs, dg1, (slots["w_in"],) = dx_final(xs, dh1, attn_norm_g, da, dq, dk, dv, wp, Exchange("scatter", dwin))

    sizes = [D_MODEL, GLA_QK, GLA_WIDTH, REL_BUCKETS * DSA_HEADS, D_MODEL, D_MODEL, GLA_RANK * GLA_QK, 1]
    small = jnp.concatenate([dg1.reshape(-1), dbg.reshape(-1), dgn.reshape(-1), drel.reshape(-1), dg2.reshape(-1),
                             dgf.reshape(-1), dw2p[:GLA_RANK].reshape(-1), loss.reshape(-1),
                             jnp.zeros((SMALL_ROWS * LANE - sum(sizes),), F32)]).reshape(SMALL_ROWS, LANE)
    tot = sum_slots(gather_small(small), "sum_small").reshape(-1)
    offs = np.concatenate([[0], np.cumsum(sizes)])
    piece = lambda i: tot[int(offs[i]):int(offs[i + 1])]
    g_g1 = piece(0).reshape(1, D_MODEL)
    g_bg = piece(1).reshape(1, GLA_QK)
    g_gn = piece(2).reshape(1, GLA_WIDTH)
    g_rel = piece(3).reshape(REL_BUCKETS, DSA_HEADS)
    g_g2 = piece(4).reshape(1, D_MODEL)
    g_gf = piece(5).reshape(1, D_MODEL)
    g_w2 = lax.dynamic_slice_in_dim(piece(6).reshape(GLA_RANK, GLA_QK), chip * (GLA_QK // 4), GLA_QK // 4, axis=1)

    loss_all = piece(7)[0]

    upd = [
        ("attn_norm_g", attn_norm_g, g_g1, m_attn_norm_g, v_attn_norm_g),
        ("w_in", w_in[0], None, m_w_in[0], v_w_in[0]),
        ("gla_gate_w2", gla_gate_w2[0], g_w2, m_gla_gate_w2[0], v_gla_gate_w2[0]),
        ("gla_gate_b", gla_gate_b, g_bg, m_gla_gate_b, v_gla_gate_b),
        ("gla_norm_g", gla_norm_g, g_gn, m_gla_norm_g, v_gla_norm_g),
        ("rel_bias", rel_bias, g_rel, m_rel_bias, v_rel_bias),
        ("w_out", w_out[0], None, m_w_out[0], v_w_out[0]),
        ("mlp_norm_g", mlp_norm_g, g_g2, m_mlp_norm_g, v_mlp_norm_g),
        ("w_ff1", w_ff1[0], None, m_w_ff1[0], v_w_ff1[0]),
        ("w_ff2", w_ff2[0], None, m_w_ff2[0], v_w_ff2[0]),
        ("final_norm_g", gf, g_gf, m_final_norm_g.reshape(1, D_MODEL), v_final_norm_g.reshape(1, D_MODEL)),
    ]
    shapes = [attn_norm_g.shape, w_in.shape, gla_gate_w2.shape, gla_gate_b.shape, gla_norm_g.shape, rel_bias.shape,
              w_out.shape, mlp_norm_g.shape, w_ff1.shape, w_ff2.shape, final_norm_g.shape]
    grads, deltas, new_m, new_v = [], [], [], []
    for (name, w, g, m, v), shape in zip(upd, shapes):
        if name in slots:
            g, d, nm_, nv_ = reduce_adamw(slots[name], w, m, v, "reduce_adamw_" + name)
        else:
            d, nm_, nv_ = adamw(w, g, m, v, "adamw_" + name)
        grads.append(g.reshape(shape))
        deltas.append(d.reshape(shape))
        new_m.append(nm_.reshape(shape))
        new_v.append(nv_.reshape(shape))
    return (loss_all, dxs.reshape(1, T, D_MODEL), *grads, *deltas, *new_m, *new_v)
```

```python
import functools
import math

import jax
import jax.numpy as jnp
import numpy as np
from jax import lax
from jax.experimental import pallas as pl
from jax.experimental.pallas import tpu as pltpu

F32 = jnp.float32
BF16 = jnp.bfloat16
MESH = pl.DeviceIdType.MESH

D_MODEL = 1024
GLA_WIDTH = 512
GLA_HEADS = 4
GLA_DK = 64
GLA_DV = 128
GLA_QK = GLA_HEADS * GLA_DK
GLA_RANK = 16
GLA_TAU = 16.0
GLA_CHUNK = 64
DSA_WIDTH = 512
DSA_HEADS = 8
DSA_DH = 64
DSA_DILATIONS = (1, 4, 16)
DSA_SPAN = 128
DSA_BLOCK = 128
DSA_SUPER = DSA_BLOCK * DSA_DILATIONS[-1]
REL_BUCKETS = 32
REL_MAX_DIST = 2048
D_FF = 4096
D_IN = 3088
EPS = 1e-6
NEG = -1e30
QK_SCALE = 0.125

ADAM_LR = 0.001
ADAM_B1 = 0.9
ADAM_B2 = 0.999
ADAM_EPS = 1e-08
ADAM_WD = 0.01
ADAM_STEP = 10

LANE = 128
P_GQ, P_GK, P_GV, P_GR = 0, 256, 512, 1024
P_GLOW = 1536
P_A = 1664
P_DQ, P_DK, P_DV = 1664, 2176, 2688
P_ALL = 3200
R_GLOW = 1536

VMEM_LIMIT = 56 * 1024 * 1024


def _params(sem=("arbitrary",), vmem=VMEM_LIMIT):
    return pltpu.CompilerParams(dimension_semantics=sem, vmem_limit_bytes=vmem)


def _dot(a, b):
    return jnp.dot(a, b, preferred_element_type=F32)


def _dot_nt(a, b):
    return lax.dot_general(a, b, (((1,), (1,)), ((), ())), preferred_element_type=F32)


def _dot_tn(a, b):
    return lax.dot_general(a, b, (((0,), (0,)), ((), ())), preferred_element_type=F32)


def _split3(x):
    x1 = x.astype(BF16)
    r1 = x - x1.astype(F32)
    x2 = r1.astype(BF16)
    x3 = (r1 - x2.astype(F32)).astype(BF16)
    return x1, x2, x3


def _dot_exact_lhs(m_bf16, x):
    x1, x2, x3 = _split3(x)
    return _dot(m_bf16, x1) + _dot(m_bf16, x2) + _dot(m_bf16, x3)


def _rstd(xf):
    return lax.rsqrt(jnp.mean(xf * xf, axis=-1, keepdims=True) + EPS)


def _load_once(hbm_ref, vmem_ref, sem):
    cp = pltpu.make_async_copy(hbm_ref, vmem_ref, sem)
    cp.start()
    cp.wait()


ANY = pl.BlockSpec(memory_space=pl.ANY)


def inproj(x, g1, wp):
    T = x.shape[0]
    tm = 512

    def body(x_ref, g_ref, w_hbm, proj_ref, nx_ref, w_vmem, sem):
        @pl.when(pl.program_id(0) == 0)
        def _():
            _load_once(w_hbm, w_vmem, sem)

        xf = x_ref[...]
        nx = ((xf * _rstd(xf)) * g_ref[...]).astype(BF16)
        nx_ref[...] = nx
        proj_ref[...] = _dot(nx, w_vmem[...])

    return pl.pallas_call(
        body,
        grid=(T // tm,),
        in_specs=[pl.BlockSpec((tm, D_MODEL), lambda i: (i, 0)), pl.BlockSpec((1, D_MODEL), lambda i: (0, 0)), ANY],
        out_specs=[pl.BlockSpec((tm, P_ALL), lambda i: (i, 0)), pl.BlockSpec((tm, D_MODEL), lambda i: (i, 0))],
        out_shape=[jax.ShapeDtypeStruct((T, P_ALL), F32), jax.ShapeDtypeStruct((T, D_MODEL), BF16)],
        scratch_shapes=[pltpu.VMEM((D_MODEL, P_ALL), BF16), pltpu.SemaphoreType.DMA],
        compiler_params=_params(),
        name="inproj",
    )(x, g1, wp)


GLA_CHUNKS_PER_STEP = 16
GLA_ROWS = GLA_CHUNK * GLA_CHUNKS_PER_STEP


def _gla_masks():
    lane = lax.broadcasted_iota(jnp.int32, (1, GLA_QK), 1)
    return [(lane >= h * GLA_DK) & (lane < (h + 1) * GLA_DK) for h in range(GLA_HEADS)]


def _log_sigmoid(x):
    return jnp.minimum(x, 0.0) - jnp.log(1.0 + jnp.exp(-jnp.abs(x)))


def _sigmoid(x):
    return 1.0 / (1.0 + jnp.exp(-x))


def _head_cols(h):
    return slice(h * GLA_DV, (h + 1) * GLA_DV)


GLA_GROUP = 256


def _gla_step_constants():
    ri = lax.broadcasted_iota(jnp.int32, (GLA_GROUP, GLA_GROUP), 0)
    ci = lax.broadcasted_iota(jnp.int32, (GLA_GROUP, GLA_GROUP), 1)
    shift = GLA_CHUNK.bit_length() - 1
    same = lax.shift_right_logical(ri, shift) == lax.shift_right_logical(ci, shift)
    return same & (ri >= ci), same & (ri <= ci), _gla_masks()


def _by_group(fn, *arrays):
    outs = [fn(*[a[g * GLA_GROUP:(g + 1) * GLA_GROUP] for a in arrays]) for g in range(GLA_ROWS // GLA_GROUP)]
    if isinstance(outs[0], tuple):
        return tuple(jnp.concatenate(parts, axis=0) for parts in zip(*outs))
    return jnp.concatenate(outs, axis=0)


def _per_chunk(x):
    return x.reshape(GLA_CHUNKS_PER_STEP, GLA_CHUNK, x.shape[-1])


def _chunk_rows_of(x, c):
    return x[c * GLA_CHUNK:(c + 1) * GLA_CHUNK]


def _stack_masked(x, masks):
    return jnp.concatenate([jnp.where(m, x, 0.0) for m in masks], axis=0)


def _stack_head_cols(x):
    return jnp.concatenate([x[:, _head_cols(h)] for h in range(GLA_HEADS)], axis=0)


def _diag_blocks(full, masks):
    out = jnp.where(masks[0], full[:GLA_DV], 0.0)
    for h in range(1, GLA_HEADS):
        out = out + jnp.where(masks[h], full[h * GLA_DV:(h + 1) * GLA_DV], 0.0)
    return out


def _row_blocks_masked(full, masks):
    out = jnp.where(masks[0], full[:GLA_CHUNK], 0.0)
    for h in range(1, GLA_HEADS):
        out = out + jnp.where(masks[h], full[h * GLA_CHUNK:(h + 1) * GLA_CHUNK], 0.0)
    return out


def _gla_step_common(q, k, glow_b, w2, bg, tri):
    gpre = _dot(glow_b, w2) + bg
    glog = _log_sigmoid(gpre) / GLA_TAU
    b = _by_group(lambda g: _dot_exact_lhs(tri, g), glog)
    bl = jnp.sum(_per_chunk(glog), axis=1, keepdims=True)
    eb = jnp.exp(b)
    enb = jnp.exp(-b)
    eke = jnp.exp(jnp.broadcast_to(bl, (GLA_CHUNKS_PER_STEP, GLA_CHUNK, GLA_QK)).reshape(GLA_ROWS, GLA_QK) - b)
    return gpre, eb, enb, eke, jnp.exp(bl), (q * QK_SCALE) * eb, k * enb, k * eke


def gla_fwd(proj, w2p, bg, gn):
    T = proj.shape[0]
    n_steps = T // GLA_ROWS
    n_chunks = T // GLA_CHUNK

    def body(proj_ref, w2_ref, bg_ref, gn_ref, oa_ref, opre_ref, sprev_ref, st_ref):
        @pl.when(pl.program_id(0) == 0)
        def _():
            st_ref[...] = jnp.zeros_like(st_ref)

        causal, _, masks = _gla_step_constants()
        q = proj_ref[:, P_GQ:P_GQ + GLA_QK]
        k = proj_ref[:, P_GK:P_GK + GLA_QK]
        v = proj_ref[:, P_GV:P_GV + GLA_WIDTH]
        r = proj_ref[:, P_GR:P_GR + GLA_WIDTH]
        glow = proj_ref[:, P_GLOW:P_GLOW + LANE].astype(BF16)
        _, _, _, _, ebl, qd, ki, ke = _gla_step_common(q, k, glow, w2_ref[...].astype(BF16), bg_ref[...], causal.astype(BF16))
        ki_b = ki.astype(BF16)
        v_b = v.astype(BF16)
        o_heads = []
        for h in range(GLA_HEADS):
            def intra(qd_g, ki_g, v_g):
                att = jnp.where(causal, _dot_nt(qd_g, ki_g), 0.0)
                return _dot(att.astype(BF16), v_g)

            o_heads.append(_by_group(intra, jnp.where(masks[h], qd, 0.0).astype(BF16), ki_b, v_b[:, _head_cols(h)]))
        st = st_ref[...]
        states = []
        for c in range(GLA_CHUNKS_PER_STEP):
            states.append(st)
            sprev_ref[c] = st
            inc = _diag_blocks(_dot_tn(_chunk_rows_of(v_b, c), _chunk_rows_of(ke, c).astype(BF16)), masks)
            st = st * ebl[c] + inc
        st_ref[...] = st
        inter = []
        for c in range(GLA_CHUNKS_PER_STEP):
            qd_c = _stack_masked(_chunk_rows_of(qd, c), masks).astype(BF16)
            got = _dot_nt(qd_c, states[c].astype(BF16))
            inter.append(jnp.concatenate([got[h * GLA_CHUNK:(h + 1) * GLA_CHUNK] for h in range(GLA_HEADS)], axis=1))
        o = jnp.concatenate(o_heads, axis=1) + jnp.concatenate(inter, axis=0)
        opre_ref[...] = o
        on = jnp.concatenate([o[:, _head_cols(h)] * _rstd(o[:, _head_cols(h)]) for h in range(GLA_HEADS)], axis=1)
        oa_ref[...] = ((on * gn_ref[...]) * (r * _sigmoid(r))).astype(BF16)

    return pl.pallas_call(
        body,
        grid=(n_steps,),
        in_specs=[
            pl.BlockSpec((GLA_ROWS, P_A), lambda i: (i, 0)),
            pl.BlockSpec((LANE, GLA_QK), lambda i: (0, 0)),
            pl.BlockSpec((1, GLA_QK), lambda i: (0, 0)),
            pl.BlockSpec((1, GLA_WIDTH), lambda i: (0, 0)),
        ],
        out_specs=[
            pl.BlockSpec((GLA_ROWS, GLA_WIDTH), lambda i: (i, 0)),
            pl.BlockSpec((GLA_ROWS, GLA_WIDTH), lambda i: (i, 0)),
            pl.BlockSpec((GLA_CHUNKS_PER_STEP, GLA_DV, GLA_QK), lambda i: (i, 0, 0)),
        ],
        out_shape=[
            jax.ShapeDtypeStruct((T, GLA_WIDTH), BF16),
            jax.ShapeDtypeStruct((T, GLA_WIDTH), F32),
            jax.ShapeDtypeStruct((n_chunks, GLA_DV, GLA_QK), F32),
        ],
        scratch_shapes=[pltpu.VMEM((GLA_DV, GLA_QK), F32)],
        compiler_params=_params(),
        name="gla_fwd",
    )(proj, w2p, bg, gn)


def gla_bwd(proj, w2p, bg, gn, opre, sprev, dmixed, exchange=None):
    T = proj.shape[0]
    n_steps = T // GLA_ROWS

    def body(*refs):
        refs = _host_exchange(exchange, refs, 7, 4, pl.program_id(0), n_steps)
        proj_ref, w2_ref, bg_ref, gn_ref, opre_ref, sprev_ref, doa_ref, da_ref, dw2_ref, dbg_ref, dgn_ref, dst_ref = refs

        @pl.when(pl.program_id(0) == 0)
        def _():
            dst_ref[...] = jnp.zeros_like(dst_ref)
            dw2_ref[...] = jnp.zeros_like(dw2_ref)
            dbg_ref[...] = jnp.zeros_like(dbg_ref)
            dgn_ref[...] = jnp.zeros_like(dgn_ref)

        causal, causal_t, masks = _gla_step_constants()
        w2 = w2_ref[...].astype(BF16)
        gn = gn_ref[...]
        q = proj_ref[:, P_GQ:P_GQ + GLA_QK]
        k = proj_ref[:, P_GK:P_GK + GLA_QK]
        v_b = proj_ref[:, P_GV:P_GV + GLA_WIDTH].astype(BF16)
        r = proj_ref[:, P_GR:P_GR + GLA_WIDTH]
        glow = proj_ref[:, P_GLOW:P_GLOW + LANE].astype(BF16)
        o = opre_ref[...]
        doa = doa_ref[...]
        gpre, eb, enb, eke, ebl, qd, ki, ke = _gla_step_common(q, k, glow, w2, bg_ref[...], causal.astype(BF16))
        sig = _sigmoid(r)
        rs = jnp.concatenate([jnp.broadcast_to(_rstd(o[:, _head_cols(h)]), (GLA_ROWS, GLA_DV)) for h in range(GLA_HEADS)], axis=1)
        on = o * rs
        d_ong = doa * (r * sig)
        dr = doa * (on * gn) * (sig * (1.0 + r * (1.0 - sig)))
        dgn_ref[...] += jnp.sum(d_ong * on, axis=0, keepdims=True)
        d_on = d_ong * gn
        t = d_on * on
        mean_t = jnp.concatenate([jnp.broadcast_to(jnp.mean(t[:, _head_cols(h)], axis=-1, keepdims=True), (GLA_ROWS, GLA_DV))
                                  for h in range(GLA_HEADS)], axis=1)
        do_b = (rs * (d_on - on * mean_t)).astype(BF16)
        ki_b = ki.astype(BF16)
        ke_b = ke.astype(BF16)
        dqd = jnp.zeros_like(qd)
        dki = jnp.zeros_like(qd)
        dv_heads = []
        for h in range(GLA_HEADS):
            qd_h = jnp.where(masks[h], qd, 0.0).astype(BF16)
            do_h = do_b[:, _head_cols(h)]

            def intra(qd_g, ki_g, v_g, do_g):
                att = jnp.where(causal, _dot_nt(qd_g, ki_g), 0.0).astype(BF16)
                d_att = jnp.where(causal, _dot_nt(do_g, v_g), 0.0).astype(BF16)
                return _dot_tn(att, do_g), _dot(d_att, ki_g), _dot_tn(d_att, qd_g)

            dv_h, dqd_h, dki_h = _by_group(intra, qd_h, ki_b, v_b[:, _head_cols(h)], do_h)
            dv_heads.append(dv_h)
            dqd = dqd + jnp.where(masks[h], dqd_h, 0.0)
            dki = dki + dki_h
        states = [sprev_ref[c] for c in range(GLA_CHUNKS_PER_STEP)]
        dqd_inter, dst_adds = [], []
        for c in range(GLA_CHUNKS_PER_STEP):
            do_c = _stack_head_cols(_chunk_rows_of(do_b, c))
            dqd_inter.append(_row_blocks_masked(_dot(do_c, states[c].astype(BF16)), masks))
            dst_adds.append(_diag_blocks(_dot_tn(_chunk_rows_of(do_b, c), _chunk_rows_of(qd, c).astype(BF16)), masks))
        dst = dst_ref[...]
        dsts, debls = [None] * GLA_CHUNKS_PER_STEP, [None] * GLA_CHUNKS_PER_STEP
        for c in reversed(range(GLA_CHUNKS_PER_STEP)):
            dsts[c] = dst
            debls[c] = jnp.sum(dst * states[c], axis=0, keepdims=True)
            dst = dst * ebl[c] + dst_adds[c]
        dst_ref[...] = dst
        dv_inter, dke = [], []
        for c in range(GLA_CHUNKS_PER_STEP):
            dst_b = dsts[c].astype(BF16)
            got = _dot_nt(_stack_masked(_chunk_rows_of(ke, c), masks).astype(BF16), dst_b)
            dv_inter.append(jnp.concatenate([got[h * GLA_CHUNK:(h + 1) * GLA_CHUNK] for h in range(GLA_HEADS)], axis=1))
            dke.append(_row_blocks_masked(_dot(_stack_head_cols(_chunk_rows_of(v_b, c)), dst_b), masks))
        dqd = dqd + jnp.concatenate(dqd_inter, axis=0)
        dke = jnp.concatenate(dke, axis=0)
        dv = jnp.concatenate(dv_heads, axis=1) + jnp.concatenate(dv_inter, axis=0)
        dkk = dke * ke
        dbl = jnp.sum(_per_chunk(dkk), axis=1, keepdims=True) + jnp.stack(debls) * ebl
        last_row = lax.broadcasted_iota(jnp.int32, (GLA_CHUNKS_PER_STEP, GLA_CHUNK, GLA_QK), 1) == GLA_CHUNK - 1
        db = dqd * qd - dki * ki - dkk + jnp.where(last_row, dbl, 0.0).reshape(GLA_ROWS, GLA_QK)
        tri_t = causal_t.astype(BF16)
        dglog = _by_group(lambda g: _dot_exact_lhs(tri_t, g), db)
        dgpre = (dglog / GLA_TAU) * _sigmoid(-gpre)
        dgpre_b = dgpre.astype(BF16)
        da_ref[...] = jnp.concatenate(
            [dqd * eb * QK_SCALE, dki * enb + dke * eke, dv, dr, _dot_nt(dgpre_b, w2)], axis=1).astype(BF16)
        dw2_ref[...] += _dot_tn(glow, dgpre_b)
        dbg_ref[...] += jnp.sum(dgpre, axis=0, keepdims=True)

    rev = lambda i: (n_steps - 1 - i, 0)
    return _hosted_call(
        exchange,
        body,
        grid=(n_steps,),
        in_specs=[
            pl.BlockSpec((GLA_ROWS, P_A), rev),
            pl.BlockSpec((LANE, GLA_QK), lambda i: (0, 0)),
            pl.BlockSpec((1, GLA_QK), lambda i: (0, 0)),
            pl.BlockSpec((1, GLA_WIDTH), lambda i: (0, 0)),
            pl.BlockSpec((GLA_ROWS, GLA_WIDTH), rev),
            pl.BlockSpec((GLA_CHUNKS_PER_STEP, GLA_DV, GLA_QK), lambda i: (n_steps - 1 - i, 0, 0)),
            pl.BlockSpec((GLA_ROWS, GLA_WIDTH), rev),
        ],
        out_specs=[
            pl.BlockSpec((GLA_ROWS, P_A), rev),
            pl.BlockSpec((LANE, GLA_QK), lambda i: (0, 0)),
            pl.BlockSpec((1, GLA_QK), lambda i: (0, 0)),
            pl.BlockSpec((1, GLA_WIDTH), lambda i: (0, 0)),
        ],
        out_shape=[
            jax.ShapeDtypeStruct((T, P_A), BF16),
            jax.ShapeDtypeStruct((LANE, GLA_QK), F32),
            jax.ShapeDtypeStruct((1, GLA_QK), F32),
            jax.ShapeDtypeStruct((1, GLA_WIDTH), F32),
        ],
        scratch_shapes=[pltpu.VMEM((GLA_DV, GLA_QK), F32)],
        compiler_params=_params(),
        name="gla_bwd",
        args=(proj, w2p, bg, gn, opre, sprev, dmixed),
    )


def _t5_bucket(dist):
    max_exact = REL_BUCKETS // 2
    n = np.maximum(dist, 0)
    large = max_exact + (np.log(np.maximum(n, 1) / max_exact) / math.log(REL_MAX_DIST / max_exact)
                         * (REL_BUCKETS - max_exact)).astype(np.int32)
    large = np.minimum(large, REL_BUCKETS - 1)
    return np.where(n < max_exact, n, large).astype(np.int32)


SUBLANES = 8


def _bucket_rows():
    steps = DSA_BLOCK - np.arange(2 * DSA_BLOCK)
    in_band = (steps >= 0) & (steps <= DSA_SPAN)
    rows = np.stack([np.where(in_band, _t5_bucket(steps * d), -1) for d in DSA_DILATIONS]).astype(np.int32)
    return np.broadcast_to(rows[:, None, :], (len(DSA_DILATIONS), SUBLANES, 2 * DSA_BLOCK)).copy()


def bias_tables(rel_bias):
    ids = jnp.asarray(_bucket_rows())
    nd = len(DSA_DILATIONS)

    def body(rel_ref, ids_ref, tab_ref):
        idt = ids_ref[0]
        for h in range(DSA_HEADS):
            row = jnp.where(idt < 0, NEG, 0.0).astype(F32)
            for b in range(REL_BUCKETS):
                row = jnp.where(idt == b, rel_ref[b, h], row)
            full = jnp.broadcast_to(row[0:1], (DSA_BLOCK, 2 * DSA_BLOCK))
            tab_ref[0, h] = pltpu.roll(full, 0, 1, stride=1, stride_axis=0)

    return pl.pallas_call(
        body,
        grid=(nd,),
        in_specs=[pl.BlockSpec(memory_space=pltpu.SMEM), pl.BlockSpec((1, SUBLANES, 2 * DSA_BLOCK), lambda d: (d, 0, 0))],
        out_specs=pl.BlockSpec((1, DSA_HEADS, DSA_BLOCK, 2 * DSA_BLOCK), lambda d: (d, 0, 0, 0)),
        out_shape=jax.ShapeDtypeStruct((nd, DSA_HEADS, DSA_BLOCK, 2 * DSA_BLOCK), F32),
        compiler_params=_params(),
        name="bias_tables",
    )(rel_bias, ids)


def _bucket_ids():
    L = DSA_BLOCK
    steps = L + np.arange(L)[:, None] - np.arange(2 * L)[None, :]
    in_band = (steps >= 0) & (steps <= DSA_SPAN)
    return np.stack([np.where(in_band, _t5_bucket(steps * d), -1) for d in DSA_DILATIONS]).astype(np.int32)


def bias_tables_bwd(dtab):
    ids = jnp.asarray(_bucket_ids())
    nd = len(DSA_DILATIONS)

    def body(dtab_ref, ids_ref, drel_ref):
        @pl.when(pl.program_id(0) == 0)
        def _():
            for b in range(REL_BUCKETS):
                for h in range(DSA_HEADS):
                    drel_ref[b, h] = 0.0

        idt = ids_ref[0]
        for b in range(REL_BUCKETS):
            in_bucket = idt == b
            for h in range(DSA_HEADS):
                drel_ref[b, h] += jnp.sum(jnp.where(in_bucket, dtab_ref[0, h], 0.0))

    return pl.pallas_call(
        body,
        grid=(nd,),
        in_specs=[pl.BlockSpec((1, DSA_HEADS, DSA_BLOCK, 2 * DSA_BLOCK), lambda d: (d, 0, 0, 0)),
                  pl.BlockSpec((1, DSA_BLOCK, 2 * DSA_BLOCK), lambda d: (d, 0, 0))],
        out_specs=pl.BlockSpec(memory_space=pltpu.SMEM),
        out_shape=jax.ShapeDtypeStruct((REL_BUCKETS, DSA_HEADS), F32),
        compiler_params=_params(),
        name="bias_tables_bwd",
    )(dtab, ids)


DSA_PAIRS = DSA_HEADS // 2
DSA_UNROLL = 16
DSA_COMBINE_ROWS = 256


def _dsa_units(d):
    return d, DSA_SUPER // (DSA_BLOCK * d)


def _dsa_specs(T):
    nsb = T // DSA_SUPER
    qcol, kcol, vcol = P_DQ // LANE, P_DK // LANE, P_DV // LANE
    return nsb, qcol, kcol, vcol


def _head_lane_mask():
    return lax.broadcasted_iota(jnp.int32, (1, LANE), 1) < DSA_DH


def _fill_tile_variants(tab_ref, variants):
    col = lax.broadcasted_iota(jnp.int32, (2 * DSA_BLOCK, 2 * DSA_BLOCK), 1)
    for di in range(len(DSA_DILATIONS)):
        tile = tab_ref[di, 0]
        variants[di, 0] = tile
        variants[di, 1] = jnp.where(col < DSA_BLOCK, NEG, tile)


def _tile_variants_scratch():
    return pltpu.VMEM((len(DSA_DILATIONS), 2, 2 * DSA_BLOCK, 2 * DSA_BLOCK), F32)


def _pair_tiles(tab):
    return tab.reshape(len(DSA_DILATIONS), DSA_PAIRS, 2 * DSA_BLOCK, 2 * DSA_BLOCK)


def _stack_heads(t, head0):
    return jnp.concatenate([jnp.where(head0, t, 0.0), jnp.where(head0, 0.0, t)], axis=0)


def dsa_fwd(proj, tab, exchange=None):
    T = proj.shape[0]
    nsb, qcol, kcol, vcol = _dsa_specs(T)
    S = DSA_SUPER

    def body(*refs):
        refs = _host_exchange(exchange, refs, 6, 2, pl.program_id(0) * nsb + pl.program_id(1), DSA_PAIRS * nsb)
        q_ref, kp_ref, kc_ref, vp_ref, vc_ref, tab_ref, out_ref, lse_ref, kk, vv, ob, lb, tiles = refs
        sb = pl.program_id(1)
        kk[0:S, :] = kp_ref[...]
        kk[S:2 * S, :] = kc_ref[...]
        vv[0:S, :] = vp_ref[...]
        vv[S:2 * S, :] = vc_ref[...]
        head0 = _head_lane_mask()
        pl.when(sb == 0)(functools.partial(_fill_tile_variants, tab_ref, tiles))

        for di, d in enumerate(DSA_DILATIONS):
            n_res, n_blk = _dsa_units(d)

            def unit(u, carry, di=di, d=d, n_blk=n_blk):
                r = u // n_blk
                c = u % n_blk
                q0 = r + d * DSA_BLOCK * c
                qrows = pl.ds(q0, DSA_BLOCK, stride=d) if d > 1 else pl.ds(q0, DSA_BLOCK)
                krows = pl.ds(S + q0 - d * DSA_BLOCK, 2 * DSA_BLOCK, stride=d) if d > 1 else pl.ds(S + q0 - DSA_BLOCK, 2 * DSA_BLOCK)
                q2 = q_ref[qrows, :] * QK_SCALE
                k2 = kk[krows, :].astype(BF16)
                v2 = vv[krows, :].astype(BF16)
                qs = _stack_heads(q2, head0).astype(BF16)
                s = _dot_nt(qs, k2) + tiles[di, ((sb == 0) & (c == 0)).astype(jnp.int32)]
                m = jnp.max(s, axis=-1, keepdims=True)
                p = jnp.exp(s - m)
                den = jnp.sum(p, axis=-1, keepdims=True)
                o = _dot(p.astype(BF16), v2) / den
                l = jnp.broadcast_to(m + jnp.log(den), (2 * DSA_BLOCK, LANE))
                ob[di, qrows, :] = jnp.where(head0, o[:DSA_BLOCK], o[DSA_BLOCK:])
                lb[di, qrows, :] = jnp.where(head0, l[:DSA_BLOCK], l[DSA_BLOCK:])
                return carry

            lax.fori_loop(0, n_res * n_blk, unit, 0, unroll=DSA_UNROLL)

        def combine(i, carry):
            rows = pl.ds(pl.multiple_of(i * DSA_COMBINE_ROWS, DSA_COMBINE_ROWS), DSA_COMBINE_ROWS)
            l0, l1, l2 = lb[0, rows, :], lb[1, rows, :], lb[2, rows, :]
            mx = jnp.maximum(jnp.maximum(l0, l1), l2)
            e0, e1, e2 = jnp.exp(l0 - mx), jnp.exp(l1 - mx), jnp.exp(l2 - mx)
            den = e0 + e1 + e2
            out_ref[rows, :] = (e0 * ob[0, rows, :] + e1 * ob[1, rows, :] + e2 * ob[2, rows, :]) / den
            lse_ref[rows, :] = mx + jnp.log(den)
            return carry

        lax.fori_loop(0, S // DSA_COMBINE_ROWS, combine, 0)

    prev = lambda col: (lambda hp, sb: (jnp.maximum(sb - 1, 0), col + hp))
    cur = lambda col: (lambda hp, sb: (sb, col + hp))
    blk = lambda f: pl.BlockSpec((S, LANE), f)
    return _hosted_call(
        exchange,
        body,
        grid=(DSA_PAIRS, nsb),
        in_specs=[blk(cur(qcol)), blk(prev(kcol)), blk(cur(kcol)), blk(prev(vcol)), blk(cur(vcol)),
                  pl.BlockSpec((len(DSA_DILATIONS), 1, 2 * DSA_BLOCK, 2 * DSA_BLOCK), lambda hp, sb: (0, hp, 0, 0))],
        out_specs=[blk(lambda hp, sb: (sb, hp)), blk(lambda hp, sb: (sb, hp))],
        out_shape=[jax.ShapeDtypeStruct((T, DSA_WIDTH), F32), jax.ShapeDtypeStruct((T, DSA_WIDTH), F32)],
        scratch_shapes=[pltpu.VMEM((2 * S, LANE), F32), pltpu.VMEM((2 * S, LANE), F32),
                        pltpu.VMEM((len(DSA_DILATIONS), S, LANE), F32), pltpu.VMEM((len(DSA_DILATIONS), S, LANE), F32),
                        _tile_variants_scratch()],
        compiler_params=_params(("arbitrary", "arbitrary")),
        name="dsa_fwd",
        args=(proj, proj, proj, proj, proj, _pair_tiles(tab)),
    )


def dsa_bwd(proj, tab, ob_out, lse, dmixed, exchange=None):
    T = proj.shape[0]
    nsb, qcol, kcol, vcol = _dsa_specs(T)
    S = DSA_SUPER
    nd = len(DSA_DILATIONS)
    ocol = GLA_WIDTH // LANE

    def body(*refs):
        refs = _host_exchange(exchange, refs, 9, 4, pl.program_id(0) * nsb + pl.program_id(1), DSA_PAIRS * nsb)
        (q_ref, kp_ref, kc_ref, vp_ref, vc_ref, tab_ref, o_ref, lse_ref, do_ref,
         dq_ref, dk_ref, dv_ref, dtab_ref, kk, vv, dqa, dkk, dvv, tiles) = refs
        j = pl.program_id(1)
        sb = nsb - 1 - j
        kk[0:S, :] = kp_ref[...]
        kk[S:2 * S, :] = kc_ref[...]
        vv[0:S, :] = vp_ref[...]
        vv[S:2 * S, :] = vc_ref[...]
        head0 = _head_lane_mask()
        pl.when(j == 0)(functools.partial(_fill_tile_variants, tab_ref, tiles))

        @pl.when(j == 0)
        def _():
            dtab_ref[...] = jnp.zeros_like(dtab_ref)
            dkk[S:2 * S, :] = jnp.zeros((S, LANE), F32)
            dvv[S:2 * S, :] = jnp.zeros((S, LANE), F32)

        @pl.when(j > 0)
        def _():
            dkk[S:2 * S, :] = dkk[0:S, :]
            dvv[S:2 * S, :] = dvv[0:S, :]

        dkk[0:S, :] = jnp.zeros((S, LANE), F32)
        dvv[0:S, :] = jnp.zeros((S, LANE), F32)
        dqa[...] = jnp.zeros_like(dqa)

        for di, d in enumerate(DSA_DILATIONS):
            n_res, n_blk = _dsa_units(d)

            def unit(u, carry, di=di, d=d, n_blk=n_blk):
                r = u // n_blk
                c = u % n_blk
                q0 = r + d * DSA_BLOCK * c
                qrows = pl.ds(q0, DSA_BLOCK, stride=d) if d > 1 else pl.ds(q0, DSA_BLOCK)
                krows = pl.ds(S + q0 - d * DSA_BLOCK, 2 * DSA_BLOCK, stride=d) if d > 1 else pl.ds(S + q0 - DSA_BLOCK, 2 * DSA_BLOCK)
                q2 = q_ref[qrows, :] * QK_SCALE
                k2 = kk[krows, :].astype(BF16)
                v2 = vv[krows, :].astype(BF16)
                do2 = do_ref[qrows, :]
                o2 = o_ref[qrows, :]
                l2 = lse_ref[qrows, :]
                qs = _stack_heads(q2, head0).astype(BF16)
                dos = _stack_heads(do2, head0)
                dos_b = dos.astype(BF16)
                delta = jnp.sum(dos * jnp.concatenate([o2, o2], axis=0), axis=-1, keepdims=True)
                lse = jnp.concatenate([jnp.max(jnp.where(head0, l2, -jnp.inf), axis=-1, keepdims=True),
                                       jnp.max(jnp.where(head0, -jnp.inf, l2), axis=-1, keepdims=True)], axis=0)
                s = _dot_nt(qs, k2) + tiles[di, ((sb == 0) & (c == 0)).astype(jnp.int32)]
                p = jnp.exp(s - lse)
                ds = p * (_dot_nt(dos_b, v2) - delta)
                dtab_ref[di, 0] += ds
                ds_b = ds.astype(BF16)
                dq = _dot(ds_b, k2)
                dqa[qrows, :] += jnp.where(head0, dq[:DSA_BLOCK], dq[DSA_BLOCK:]) * QK_SCALE
                dkk[krows, :] += _dot_tn(ds_b, qs)
                dvv[krows, :] += _dot_tn(p.astype(BF16), dos_b)
                return carry

            lax.fori_loop(0, n_res * n_blk, unit, 0, unroll=DSA_UNROLL)

        dq_ref[...] = dqa[...].astype(BF16)
        dk_ref[...] = dkk[S:2 * S, :].astype(BF16)
        dv_ref[...] = dvv[S:2 * S, :].astype(BF16)

    prev = lambda col: (lambda hp, j: (jnp.maximum(nsb - 2 - j, 0), col + hp))
    cur = lambda col: (lambda hp, j: (nsb - 1 - j, col + hp))
    blk = lambda f: pl.BlockSpec((S, LANE), f)
    out_blk = blk(lambda hp, j: (nsb - 1 - j, hp))
    tab_blk = pl.BlockSpec((nd, 1, 2 * DSA_BLOCK, 2 * DSA_BLOCK), lambda hp, j: (0, hp, 0, 0))
    dq, dk, dv, dtab, *carried = _hosted_call(
        exchange,
        body,
        grid=(DSA_PAIRS, nsb),
        in_specs=[blk(cur(qcol)), blk(prev(kcol)), blk(cur(kcol)), blk(prev(vcol)), blk(cur(vcol)), tab_blk,
                  out_blk, out_blk, blk(cur(ocol))],
        out_specs=[out_blk, out_blk, out_blk, tab_blk],
        out_shape=[jax.ShapeDtypeStruct((T, DSA_WIDTH), BF16)] * 3
        + [jax.ShapeDtypeStruct((nd, DSA_PAIRS, 2 * DSA_BLOCK, 2 * DSA_BLOCK), F32)],
        scratch_shapes=[pltpu.VMEM((2 * S, LANE), F32), pltpu.VMEM((2 * S, LANE), F32), pltpu.VMEM((S, LANE), F32),
                        pltpu.VMEM((2 * S, LANE), F32), pltpu.VMEM((2 * S, LANE), F32), _tile_variants_scratch()],
        compiler_params=_params(("arbitrary", "arbitrary")),
        name="dsa_bwd",
        args=(proj, proj, proj, proj, proj, _pair_tiles(tab), ob_out, lse, dmixed),
    )
    return (dq, dk, dv, dtab.reshape(nd, DSA_HEADS, DSA_BLOCK, 2 * DSA_BLOCK), *carried)


FF_BLOCKS = 4
FF_BLOCK = D_FF // FF_BLOCKS


def post_fused(x, oa, ob, tgt, g2, gf, wout, wff1, wff2):
    T = x.shape[0]
    tm = 256
    inv_d = 1.0 / D_MODEL

    def body(x_ref, oa_ref, ob_ref, tgt_ref, g2_ref, gf_ref, wout_hbm, wff1_hbm, wff2_hbm,
             mixed_ref, nm_ref, a_ref, dpre_ref, dh2_ref, dh1_ref, dmixed_ref, loss_ref, dgf_ref, dg2_ref,
             wout_v, wff1_v, wff2_v, sems):
        @pl.when(pl.program_id(0) == 0)
        def _():
            cps = [pltpu.make_async_copy(s, d, sems.at[i])
                   for i, (s, d) in enumerate([(wout_hbm, wout_v), (wff1_hbm, wff1_v), (wff2_hbm, wff2_v)])]
            for cp in cps:
                cp.start()
            for cp in cps:
                cp.wait()
            loss_ref[...] = jnp.zeros_like(loss_ref)
            dgf_ref[...] = jnp.zeros_like(dgf_ref)
            dg2_ref[...] = jnp.zeros_like(dg2_ref)

        mixed = jnp.concatenate([oa_ref[...], ob_ref[...].astype(BF16)], axis=1)
        mixed_ref[...] = mixed
        h1 = x_ref[...] + _dot(mixed, wout_v[...])
        rs1 = _rstd(h1)
        hn1 = h1 * rs1
        g2 = g2_ref[...]
        nm = (hn1 * g2).astype(BF16)
        nm_ref[...] = nm
        relu = []
        mlp = jnp.zeros((tm, D_MODEL), F32)
        for j in range(FF_BLOCKS):
            cols = slice(j * FF_BLOCK, (j + 1) * FF_BLOCK)
            r_j = jnp.maximum(_dot(nm, wff1_v[j]), 0.0)
            a_j = (r_j * r_j).astype(BF16)
            a_ref[:, cols] = a_j
            relu.append(r_j)
            mlp = mlp + _dot(a_j, wff2_v[cols, :])
        h2 = h1 + mlp
        rsf = _rstd(h2)
        hnf = h2 * rsf
        gf = gf_ref[...]
        diff = hnf * gf - tgt_ref[...]
        loss_ref[...] += 0.5 * jnp.sum(jnp.sum(diff * diff, axis=-1, keepdims=True) * inv_d, axis=0, keepdims=True)
        dy = diff * inv_d
        dgf_ref[...] += jnp.sum(dy * hnf, axis=0, keepdims=True)
        dhnf = dy * gf
        dh2 = rsf * (dhnf - hnf * jnp.mean(dhnf * hnf, axis=-1, keepdims=True))
        dh2_b = dh2.astype(BF16)
        dh2_ref[...] = dh2_b
        dnm = jnp.zeros((tm, D_MODEL), F32)
        for j in range(FF_BLOCKS):
            cols = slice(j * FF_BLOCK, (j + 1) * FF_BLOCK)
            dpre_j = (_dot_nt(dh2_b, wff2_v[cols, :]) * (2.0 * relu[j])).astype(BF16)
            dpre_ref[:, cols] = dpre_j
            dnm = dnm + _dot_nt(dpre_j, wff1_v[j])
        dg2_ref[...] += jnp.sum(dnm * hn1, axis=0, keepdims=True)
        dhn1 = dnm * g2
        dh1 = dh2 + rs1 * (dhn1 - hn1 * jnp.mean(dhn1 * hn1, axis=-1, keepdims=True))
        dh1_ref[...] = dh1
        dmixed_ref[...] = _dot_nt(dh1.astype(BF16), wout_v[...])

    row = lambda w: pl.BlockSpec((tm, w), lambda i: (i, 0))
    vec = lambda w: pl.BlockSpec((1, w), lambda i: (0, 0))
    return pl.pallas_call(
        body,
        grid=(T // tm,),
        in_specs=[row(D_MODEL), row(GLA_WIDTH), row(DSA_WIDTH), row(D_MODEL), vec(D_MODEL), vec(D_MODEL), ANY, ANY, ANY],
        out_specs=[row(D_MODEL), row(D_MODEL), row(D_FF), row(D_FF), row(D_MODEL), row(D_MODEL), row(D_MODEL),
                   vec(1), vec(D_MODEL), vec(D_MODEL)],
        out_shape=[
            jax.ShapeDtypeStruct((T, D_MODEL), BF16),
            jax.ShapeDtypeStruct((T, D_MODEL), BF16),
            jax.ShapeDtypeStruct((T, D_FF), BF16),
            jax.ShapeDtypeStruct((T, D_FF), BF16),
            jax.ShapeDtypeStruct((T, D_MODEL), BF16),
            jax.ShapeDtypeStruct((T, D_MODEL), F32),
            jax.ShapeDtypeStruct((T, D_MODEL), F32),
            jax.ShapeDtypeStruct((1, 1), F32),
            jax.ShapeDtypeStruct((1, D_MODEL), F32),
            jax.ShapeDtypeStruct((1, D_MODEL), F32),
        ],
        scratch_shapes=[pltpu.VMEM((D_MODEL, D_MODEL), BF16), pltpu.VMEM((FF_BLOCKS, D_MODEL, FF_BLOCK), BF16),
                        pltpu.VMEM((D_FF, D_MODEL), BF16), pltpu.SemaphoreType.DMA((3,))],
        compiler_params=_params(),
        name="post_fused",
    )(x, oa, ob, tgt, g2, gf, wout, wff1, wff2)


WGRAD_TOKENS = 2048


def wgrad(a, b, name, bm=None, bn=None, col_blocked=False):
    T, M = a.shape
    N = b.shape[1]
    bm = M if bm is None else bm
    bn = N if bn is None else bn
    tk = min(WGRAD_TOKENS, T)
    n_k = T // tk

    def body(a_ref, b_ref, o_ref, acc_ref):
        part = _dot_tn(a_ref[...].astype(BF16), b_ref[...].astype(BF16))
        out = o_ref.at[0] if col_blocked else o_ref
        k = pl.program_id(2)
        if n_k == 1:
            out[...] = part.astype(BF16)
            return

        @pl.when(k == 0)
        def _():
            acc_ref[...] = part

        @pl.when((k > 0) & (k < n_k - 1))
        def _():
            acc_ref[...] += part

        @pl.when(k == n_k - 1)
        def _():
            out[...] = (acc_ref[...] + part).astype(BF16)

    if col_blocked:
        assert bm == M
        out_spec = pl.BlockSpec((1, M, bn), lambda i, j, k: (j, 0, 0))
        out_shape = jax.ShapeDtypeStruct((N // bn, M, bn), BF16)
    else:
        out_spec = pl.BlockSpec((bm, bn), lambda i, j, k: (i, j))
        out_shape = jax.ShapeDtypeStruct((M, N), BF16)
    return pl.pallas_call(
        body,
        grid=(M // bm, N // bn, n_k),
        in_specs=[pl.BlockSpec((tk, bm), lambda i, j, k: (k, i)), pl.BlockSpec((tk, bn), lambda i, j, k: (k, j))],
        out_specs=out_spec,
        out_shape=out_shape,
        scratch_shapes=[pltpu.VMEM((bm, bn), F32)],
        compiler_params=_params(("arbitrary", "arbitrary", "arbitrary")),
        name=name,
    )(a, b)


def wgrad_cat(a, bs, name):
    T, M = a.shape
    widths = [b.shape[1] for b in bs]
    starts = [sum(widths[:i]) for i in range(len(bs))]
    N = sum(widths)
    tk = min(WGRAD_TOKENS // 2, T)
    n_k = T // tk

    def body(a_ref, *rest):
        b_refs, o_ref, acc_ref = rest[:len(bs)], rest[len(bs)], rest[len(bs) + 1]
        k = pl.program_id(0)

        @pl.when(k == 0)
        def _():
            acc_ref[...] = jnp.zeros_like(acc_ref)

        a_t = a_ref[...]
        for b_ref, start, width in zip(b_refs, starts, widths):
            acc_ref[:, start:start + width] += _dot_tn(a_t, b_ref[...])

        @pl.when(k == n_k - 1)
        def _():
            o_ref[...] = acc_ref[...].astype(BF16)

    return pl.pallas_call(
        body,
        grid=(n_k,),
        in_specs=[pl.BlockSpec((tk, M), lambda k: (k, 0))] + [pl.BlockSpec((tk, w), lambda k: (k, 0)) for w in widths],
        out_specs=pl.BlockSpec((M, N), lambda k: (0, 0)),
        out_shape=jax.ShapeDtypeStruct((M, N), BF16),
        scratch_shapes=[pltpu.VMEM((M, N), F32)],
        compiler_params=_params(),
        name=name,
    )(a, *bs)


def dx_final(x, dh1, g1, da, dq, dk, dv, wp, exchange=None):
    T = x.shape[0]
    tm = 512

    def body(*refs):
        refs = _host_exchange(exchange, refs, 8, 2, pl.program_id(0), T // tm)
        x_ref, dh1_ref, g_ref, da_ref, dq_ref, dk_ref, dv_ref, w_hbm, dx_ref, dg_ref, w_vmem, sem = refs

        @pl.when(pl.program_id(0) == 0)
        def _():
            _load_once(w_hbm, w_vmem, sem)
            dg_ref[...] = jnp.zeros_like(dg_ref)

        dnx = (_dot_nt(da_ref[...], w_vmem[:, 0:P_A]) + _dot_nt(dq_ref[...], w_vmem[:, P_DQ:P_DQ + DSA_WIDTH])
               + _dot_nt(dk_ref[...], w_vmem[:, P_DK:P_DK + DSA_WIDTH]) + _dot_nt(dv_ref[...], w_vmem[:, P_DV:P_DV + DSA_WIDTH]))
        xf = x_ref[...]
        rs = _rstd(xf)
        hn = xf * rs
        dg_ref[...] += jnp.sum(dnx * hn, axis=0, keepdims=True)
        dhn = dnx * g_ref[...]
        dx_ref[...] = dh1_ref[...] + rs * (dhn - hn * jnp.mean(dhn * hn, axis=-1, keepdims=True))

    row = lambda w: pl.BlockSpec((tm, w), lambda i: (i, 0))
    vec = pl.BlockSpec((1, D_MODEL), lambda i: (0, 0))
    return _hosted_call(
        exchange,
        body,
        grid=(T // tm,),
        in_specs=[row(D_MODEL), row(D_MODEL), vec, row(P_A), row(DSA_WIDTH), row(DSA_WIDTH), row(DSA_WIDTH), ANY],
        out_specs=[row(D_MODEL), vec],
        out_shape=[jax.ShapeDtypeStruct((T, D_MODEL), F32), jax.ShapeDtypeStruct((1, D_MODEL), F32)],
        scratch_shapes=[pltpu.VMEM((D_MODEL, P_ALL), BF16), pltpu.SemaphoreType.DMA],
        compiler_params=_params(),
        name="dx_final",
        args=(x, dh1, g1, da, dq, dk, dv, wp),
    )


def adamw(w, g, m, v, name):
    R, C = w.shape
    br = 256 if R % 256 == 0 else R

    def body(w_ref, g_ref, m_ref, v_ref, d_ref, nm_ref, nv_ref):
        d_ref[...], nm_ref[...], nv_ref[...] = _adamw_math(w_ref[...], g_ref[...], m_ref[...], v_ref[...])

    spec = pl.BlockSpec((br, C), lambda i: (i, 0))
    return pl.pallas_call(
        body,
        grid=(R // br,),
        in_specs=[spec] * 4,
        out_specs=[spec] * 3,
        out_shape=[jax.ShapeDtypeStruct((R, C), F32)] * 3,
        compiler_params=_params(),
        name=name,
    )(w, g, m, v)


def _place():
    return lax.axis_index("x"), lax.axis_index("y"), lax.axis_index("c")


def _other_chips(x, y):
    return [(1 - x, y), (x, 1 - y), (1 - x, 1 - y)]


class Exchange:
    def __init__(self, kind, arrays):
        self.kind, self.arrays, self.n = kind, arrays, len(arrays)
        self.slots = 4 if kind == "gather" else 8

    def out_shapes(self):
        if self.kind == "gather":
            return [jax.ShapeDtypeStruct((4,) + s.shape, s.dtype) for s in self.arrays]
        return [jax.ShapeDtypeStruct((8,) + s.shape[1:], s.dtype) for s in self.arrays]

    def sems(self):
        return [pltpu.SemaphoreType.DMA((self.n, 19)), pltpu.SemaphoreType.DMA((self.n, 19))]

    def phases(self, ins, outs, send_sems, recv_sems):
        n, scatter = self.n, self.kind == "scatter"
        x, y, c = _place()
        me, sib = (x, y, c), (x, y, 1 - c)
        mine = 2 * x + y
        chips = _other_chips(x, y)
        own_pair = 18

        def region(a, slot, half):
            h = outs[a].shape[1] // 2
            return outs[a].at[slot, pl.ds(half * h, h)]

        def copy(a, k, slot, half, to, src=None):
            return pltpu.make_async_remote_copy(
                src_ref=region(a, slot, half) if src is None else src, dst_ref=region(a, slot, half),
                send_sem=send_sems.at[a, k], recv_sem=recv_sems.at[a, k], device_id=to, device_id_type=MESH)

        def over_ici(t, to_core, from_core):
            return 4 * t + 2 * to_core + from_core

        def passed_on(t, from_core):
            return 12 + 2 * t + from_core

        senders = [(t, cc) for t in range(3) for cc in ((0, 1) if scatter else (c,))]

        def slot_of(t, cc):
            cx, cy = chips[t]
            return 2 * (2 * cx + cy) + cc if scatter else 2 * cx + cy

        def first_copies():
            cps = []
            for a in range(n):
                h = outs[a].shape[1] // 2
                for t, (cx, cy) in enumerate(chips):
                    if scatter:
                        for half in (0, 1):
                            cps.append(copy(a, over_ici(t, half, c), 2 * mine + c, half, (cx, cy, half),
                                            src=ins[a].at[2 * cx + cy, pl.ds(half * h, h)]))
                    else:
                        cps.append(copy(a, over_ici(t, c, c), mine, c, (cx, cy, c), src=ins[a].at[pl.ds(c * h, h)]))
                if scatter:
                    cps.append(pltpu.make_async_remote_copy(
                        src_ref=ins[a].at[mine], dst_ref=outs[a].at[2 * mine + c], send_sem=send_sems.at[a, own_pair],
                        recv_sem=recv_sems.at[a, own_pair], device_id=sib, device_id_type=MESH))
            return cps

        def forward_copies():
            return [copy(a, passed_on(t, cc), slot_of(t, cc), c, sib) for a in range(n) for t, cc in senders]

        def start():
            for cp in first_copies():
                cp.start()

        def forward():
            fws = iter(forward_copies())
            for a in range(n):
                for t, cc in senders:
                    copy(a, over_ici(t, c, cc), slot_of(t, cc), c, me).wait_recv()
                    next(fws).start()

        def finish():
            for a in range(n):
                for t, cc in senders:
                    from_core = cc if scatter else 1 - c
                    copy(a, passed_on(t, from_core), slot_of(t, from_core), 1 - c, me).wait_recv()
                if scatter:
                    pltpu.make_async_remote_copy(
                        src_ref=ins[a].at[mine], dst_ref=outs[a].at[2 * mine + 1 - c], send_sem=send_sems.at[a, own_pair],
                        recv_sem=recv_sems.at[a, own_pair], device_id=me, device_id_type=MESH).wait_recv()
            for cp in first_copies() + forward_copies():
                cp.wait_send()

        return start, forward, finish

    def fill_own(self, outs):
        x, y, c = _place()
        if self.kind == "gather":
            return [lax.dynamic_update_index_in_dim(o, s, 2 * x + y, 0) for o, s in zip(outs, self.arrays)]
        return [lax.dynamic_update_index_in_dim(o, lax.dynamic_index_in_dim(s, 2 * x + y, 0, keepdims=False), 2 * (2 * x + y) + c, 0)
                for o, s in zip(outs, self.arrays)]

    def run(self, name):
        n = self.n

        def body(*refs):
            start, forward, finish = self.phases(refs[:n], refs[n:2 * n], *refs[2 * n:])
            start()
            forward()
            finish()

        outs = pl.pallas_call(
            body, in_specs=[ANY] * n, out_specs=[ANY] * n, out_shape=self.out_shapes(), scratch_shapes=self.sems(), name=name,
        )(*self.arrays)
        return self.fill_own(outs)


def _host_exchange(exchange, refs, n_in, n_out, step, n_steps):
    if exchange is None:
        return refs
    n = exchange.n
    own_in, ex_in = refs[:n_in], refs[n_in:n_in + n]
    own_out, ex_out = refs[n_in + n:n_in + n + n_out], refs[n_in + n + n_out:n_in + 2 * n + n_out]
    rest = refs[n_in + 2 * n + n_out:]
    start, forward, finish = exchange.phases(ex_in, ex_out, rest[-2], rest[-1])
    pl.when(step == 0)(start)
    pl.when(step == (2 * n_steps) // 3)(forward)
    pl.when(step == n_steps - 1)(finish)
    return own_in + own_out + rest[:-2]


def _hosted_call(exchange, body, *, grid, in_specs, out_specs, out_shape, scratch_shapes, compiler_params, name, args):
    if exchange is None:
        return pl.pallas_call(body, grid=grid, in_specs=in_specs, out_specs=out_specs, out_shape=out_shape,
                              scratch_shapes=scratch_shapes, compiler_params=compiler_params, name=name)(*args)
    n = exchange.n
    res = pl.pallas_call(
        body, grid=grid, in_specs=list(in_specs) + [ANY] * n, out_specs=list(out_specs) + [ANY] * n,
        out_shape=list(out_shape) + exchange.out_shapes(), scratch_shapes=list(scratch_shapes) + exchange.sems(),
        compiler_params=compiler_params, name=name)(*args, *exchange.arrays)
    return list(res[:len(out_shape)]) + [exchange.fill_own(res[len(out_shape):])]


def sum_slots(parts, name):
    S, R, C = parts.shape
    br = 128 if R % 128 == 0 else R

    def body(p_ref, o_ref):
        acc = p_ref[0].astype(F32)
        for s in range(1, S):
            acc = acc + p_ref[s].astype(F32)
        o_ref[...] = acc

    return pl.pallas_call(
        body,
        grid=(R // br,),
        in_specs=[pl.BlockSpec((S, br, C), lambda i: (0, i, 0))],
        out_specs=pl.BlockSpec((br, C), lambda i: (i, 0)),
        out_shape=jax.ShapeDtypeStruct((R, C), F32),
        compiler_params=_params(),
        name=name,
    )(parts)


def _adamw_math(w, g, m, v):
    m_new = ADAM_B1 * m + (1.0 - ADAM_B1) * g
    v_new = ADAM_B2 * v + (1.0 - ADAM_B2) * (g * g)
    m_hat = m_new / (1.0 - ADAM_B1 ** ADAM_STEP)
    v_hat = v_new / (1.0 - ADAM_B2 ** ADAM_STEP)
    return -ADAM_LR * (m_hat / (jnp.sqrt(v_hat) + ADAM_EPS) + ADAM_WD * w), m_new, v_new


def reduce_adamw(slots, w, m, v, name):
    S, R, C = slots.shape
    br = 128

    def body(p_ref, w_ref, m_ref, v_ref, g_ref, d_ref, nm_ref, nv_ref):
        g = p_ref[0].astype(F32)
        for s in range(1, S):
            g = g + p_ref[s].astype(F32)
        g_ref[...] = g
        d_ref[...], nm_ref[...], nv_ref[...] = _adamw_math(w_ref[...], g, m_ref[...], v_ref[...])

    spec = pl.BlockSpec((br, C), lambda i: (i, 0))
    return pl.pallas_call(
        body,
        grid=(R // br,),
        in_specs=[pl.BlockSpec((S, br, C), lambda i: (0, i, 0)), spec, spec, spec],
        out_specs=[spec] * 4,
        out_shape=[jax.ShapeDtypeStruct((R, C), F32)] * 4,
        compiler_params=_params(),
        name=name,
    )(slots, w, m, v)


SMALL_ROWS = 72


def gather_small(vec):
    def body(v_ref, o_ref, send_sems, recv_sems, local_sem):
        x, y, c = _place()
        flips = [(fx, fy, fc) for fx in (0, 1) for fy in (0, 1) for fc in (0, 1)][1:]

        def peer(f):
            return (1 - x if f[0] else x, 1 - y if f[1] else y, 1 - c if f[2] else c)

        slot = lambda p: 4 * p[0] + 2 * p[1] + p[2]
        own = pltpu.make_async_copy(v_ref, o_ref.at[slot((x, y, c))], local_sem)
        own.start()
        cps = [pltpu.make_async_remote_copy(
            src_ref=v_ref, dst_ref=o_ref.at[slot((x, y, c))], send_sem=send_sems.at[k], recv_sem=recv_sems.at[k],
            device_id=peer(f), device_id_type=MESH) for k, f in enumerate(flips)]
        for cp in cps:
            cp.start()
        for k, f in enumerate(flips):
            pltpu.make_async_remote_copy(
                src_ref=v_ref, dst_ref=o_ref.at[slot(peer(f))], send_sem=send_sems.at[k], recv_sem=recv_sems.at[k],
                device_id=(x, y, c), device_id_type=MESH).wait_recv()
        for cp in cps:
            cp.wait_send()
        own.wait()

    return pl.pallas_call(
        body,
        in_specs=[ANY],
        out_specs=ANY,
        out_shape=jax.ShapeDtypeStruct((8,) + vec.shape, vec.dtype),
        scratch_shapes=[pltpu.SemaphoreType.DMA((7,)), pltpu.SemaphoreType.DMA((7,)), pltpu.SemaphoreType.DMA],
        name="gather_small",
    )(vec)


GLOW_PAD = LANE - GLA_RANK


def kernel(x, attn_norm_g, w_in, gla_gate_w2, gla_gate_b, gla_norm_g, rel_bias, w_out, mlp_norm_g, w_ff1, w_ff2, final_norm_g, loss_target, m_attn_norm_g, m_w_in, m_gla_gate_w2, m_gla_gate_b, m_gla_norm_g, m_rel_bias, m_w_out, m_mlp_norm_g, m_w_ff1, m_w_ff2, m_final_norm_g, v_attn_norm_g, v_w_in, v_gla_gate_w2, v_gla_gate_b, v_gla_norm_g, v_rel_bias, v_w_out, v_mlp_norm_g, v_w_ff1, v_w_ff2, v_final_norm_g):
    xs, tgt = x[0], loss_target[0]
    T = xs.shape[0]
    cx, cy, _ = _place()
    chip = 2 * cx + cy
    gf = final_norm_g.reshape(1, D_MODEL)

    win_g, w2_g = Exchange("gather", [w_in[0].astype(BF16), gla_gate_w2[0]]).run("gather_w_in")
    win = jnp.transpose(win_g, (1, 0, 2)).reshape(D_MODEL, D_IN)
    n_glow = R_GLOW + GLA_RANK
    wp = jnp.concatenate([win[:, :n_glow], jnp.zeros((D_MODEL, GLOW_PAD), BF16), win[:, n_glow:]], axis=1)
    w2 = jnp.transpose(w2_g, (1, 0, 2)).reshape(GLA_RANK, GLA_QK)
    w2p = jnp.concatenate([w2, jnp.zeros((GLOW_PAD, GLA_QK), F32)], axis=0)

    proj, nx = inproj(xs, attn_norm_g, wp)
    tab = bias_tables(rel_bias)
    ob, lse, (wout_g, wff1, wff2_g) = dsa_fwd(
        proj, tab, Exchange("gather", [w_out[0].astype(BF16), w_ff1[0].astype(BF16), w_ff2[0].astype(BF16)]))
    wout = wout_g.reshape(D_MODEL, D_MODEL)
    wff2 = wff2_g.reshape(D_FF, D_MODEL)
    oa, opre, sprev = gla_fwd(proj, w2p, gla_gate_b, gla_norm_g)
    mixed, nm, act, dpre, dh2, dh1, dmixed, loss, dgf, dg2 = post_fused(xs, oa, ob, tgt, mlp_norm_g, gf, wout, wff1, wff2)

    late = [
        wgrad(mixed, dh1, "wgrad_out").reshape(4, D_MODEL // 4, D_MODEL),
        wgrad(nm, dpre, "wgrad_ff1", bn=FF_BLOCK, col_blocked=True),
        wgrad(act, dh2, "wgrad_ff2", bm=FF_BLOCK).reshape(4, FF_BLOCK, D_MODEL),
    ]
    da, dw2p, dbg, dgn = gla_bwd(proj, w2p, gla_gate_b, gla_norm_g, opre, sprev, dmixed)
    dq, dk, dv, dtab, late_slots = dsa_bwd(proj, tab, ob, lse, dmixed, Exchange("scatter", late))
    slots = dict(zip(["w_out", "w_ff1", "w_ff2"], late_slots))
    drel = bias_tables_bwd(dtab)

    dwp = wgrad_cat(nx, [da, dq, dk, dv], "wgrad_in")
    dwin = jnp.concatenate([dwp[:, :n_glow], dwp[:, P_A:]], axis=1)
    dwin = [jnp.transpose(dwin.reshape(D_MODEL, 4, D_IN // 4), (1, 0, 2))]
    dxs, dg1, (slots["w_in"],) = dx_final(xs, dh1, attn_norm_g, da, dq, dk, dv, wp, Exchange("scatter", dwin))

    sizes = [D_MODEL, GLA_QK, GLA_WIDTH, REL_BUCKETS * DSA_HEADS, D_MODEL, D_MODEL, GLA_RANK * GLA_QK, 1]
    small = jnp.concatenate([dg1.reshape(-1), dbg.reshape(-1), dgn.reshape(-1), drel.reshape(-1), dg2.reshape(-1),
                             dgf.reshape(-1), dw2p[:GLA_RANK].reshape(-1), loss.reshape(-1),
                             jnp.zeros((SMALL_ROWS * LANE - sum(sizes),), F32)]).reshape(SMALL_ROWS, LANE)
    tot = sum_slots(gather_small(small), "sum_small").reshape(-1)
    offs = np.concatenate([[0], np.cumsum(sizes)])
    piece = lambda i: tot[int(offs[i]):int(offs[i + 1])]
    g_g1 = piece(0).reshape(1, D_MODEL)
    g_bg = piece(1).reshape(1, GLA_QK)
    g_gn = piece(2).reshape(1, GLA_WIDTH)
    g_rel = piece(3).reshape(REL_BUCKETS, DSA_HEADS)
    g_g2 = piece(4).reshape(1, D_MODEL)
    g_gf = piece(5).reshape(1, D_MODEL)
    g_w2 = lax.dynamic_slice_in_dim(piece(6).reshape(GLA_RANK, GLA_QK), chip * (GLA_QK // 4), GLA_QK // 4, axis=1)

    loss_all = piece(7)[0]

    upd = [
        ("attn_norm_g", attn_norm_g, g_g1, m_attn_norm_g, v_attn_norm_g),
        ("w_in", w_in[0], None, m_w_in[0], v_w_in[0]),
        ("gla_gate_w2", gla_gate_w2[0], g_w2, m_gla_gate_w2[0], v_gla_gate_w2[0]),
        ("gla_gate_b", gla_gate_b, g_bg, m_gla_gate_b, v_gla_gate_b),
        ("gla_norm_g", gla_norm_g, g_gn, m_gla_norm_g, v_gla_norm_g),
        ("rel_bias", rel_bias, g_rel, m_rel_bias, v_rel_bias),
        ("w_out", w_out[0], None, m_w_out[0], v_w_out[0]),
        ("mlp_norm_g", mlp_norm_g, g_g2, m_mlp_norm_g, v_mlp_norm_g),
        ("w_ff1", w_ff1[0], None, m_w_ff1[0], v_w_ff1[0]),
        ("w_ff2", w_ff2[0], None, m_w_ff2[0], v_w_ff2[0]),
        ("final_norm_g", gf, g_gf, m_final_norm_g.reshape(1, D_MODEL), v_final_norm_g.reshape(1, D_MODEL)),
    ]
    shapes = [attn_norm_g.shape, w_in.shape, gla_gate_w2.shape, gla_gate_b.shape, gla_norm_g.shape, rel_bias.shape,
              w_out.shape, mlp_norm_g.shape, w_ff1.shape, w_ff2.shape, final_norm_g.shape]
    grads, deltas, new_m, new_v = [], [], [], []
    for (name, w, g, m, v), shape in zip(upd, shapes):
        if name in slots:
            g, d, nm_, nv_ = reduce_adamw(slots[name], w, m, v, "reduce_adamw_" + name)
        else:
            d, nm_, nv_ = adamw(w, g, m, v, "adamw_" + name)
        grads.append(g.reshape(shape))
        deltas.append(d.reshape(shape))
        new_m.append(nm_.reshape(shape))
        new_v.append(nv_.reshape(shape))
    return (loss_all, dxs.reshape(1, T, D_MODEL), *grads, *deltas, *new_m, *new_v)
```

```python
import functools
import math

import jax
import jax.numpy as jnp
import numpy as np
from jax import lax
from jax.experimental import pallas as pl
from jax.experimental.pallas import tpu as pltpu

F32 = jnp.float32
BF16 = jnp.bfloat16
MESH = pl.DeviceIdType.MESH

D_MODEL = 1024
GLA_WIDTH = 512
GLA_HEADS = 4
GLA_DK = 64
GLA_DV = 128
GLA_QK = GLA_HEADS * GLA_DK
GLA_RANK = 16
GLA_TAU = 16.0
GLA_CHUNK = 64
DSA_WIDTH = 512
DSA_HEADS = 8
DSA_DH = 64
DSA_DILATIONS = (1, 4, 16)
DSA_SPAN = 128
DSA_BLOCK = 128
DSA_SUPER = DSA_BLOCK * DSA_DILATIONS[-1]
REL_BUCKETS = 32
REL_MAX_DIST = 2048
D_FF = 4096
D_IN = 3088
EPS = 1e-6
NEG = -1e30
QK_SCALE = 0.125

ADAM_LR = 0.001
ADAM_B1 = 0.9
ADAM_B2 = 0.999
ADAM_EPS = 1e-08
ADAM_WD = 0.01
ADAM_STEP = 10

LANE = 128
P_GQ, P_GK, P_GV, P_GR = 0, 256, 512, 1024
P_GLOW = 1536
P_A = 1664
P_DQ, P_DK, P_DV = 1664, 2176, 2688
P_ALL = 3200
R_GLOW = 1536

VMEM_LIMIT = 56 * 1024 * 1024


def _params(sem=("arbitrary",), vmem=VMEM_LIMIT):
    return pltpu.CompilerParams(dimension_semantics=sem, vmem_limit_bytes=vmem)


def _dot(a, b):
    return jnp.dot(a, b, preferred_element_type=F32)


def _dot_nt(a, b):
    return lax.dot_general(a, b, (((1,), (1,)), ((), ())), preferred_element_type=F32)


def _dot_tn(a, b):
    return lax.dot_general(a, b, (((0,), (0,)), ((), ())), preferred_element_type=F32)


def _split3(x):
    x1 = x.astype(BF16)
    r1 = x - x1.astype(F32)
    x2 = r1.astype(BF16)
    x3 = (r1 - x2.astype(F32)).astype(BF16)
    return x1, x2, x3


def _dot_exact_lhs(m_bf16, x):
    x1, x2, x3 = _split3(x)
    return _dot(m_bf16, x1) + _dot(m_bf16, x2) + _dot(m_bf16, x3)


def _rstd(xf):
    return lax.rsqrt(jnp.mean(xf * xf, axis=-1, keepdims=True) + EPS)


def _load_once(hbm_ref, vmem_ref, sem):
    cp = pltpu.make_async_copy(hbm_ref, vmem_ref, sem)
    cp.start()
    cp.wait()


ANY = pl.BlockSpec(memory_space=pl.ANY)


def inproj(x, g1, wp):
    T = x.shape[0]
    tm = 512

    def body(x_ref, g_ref, w_hbm, proj_ref, nx_ref, w_vmem, sem):
        @pl.when(pl.program_id(0) == 0)
        def _():
            _load_once(w_hbm, w_vmem, sem)

        xf = x_ref[...]
        nx = ((xf * _rstd(xf)) * g_ref[...]).astype(BF16)
        nx_ref[...] = nx
        proj_ref[...] = _dot(nx, w_vmem[...])

    return pl.pallas_call(
        body,
        grid=(T // tm,),
        in_specs=[pl.BlockSpec((tm, D_MODEL), lambda i: (i, 0)), pl.BlockSpec((1, D_MODEL), lambda i: (0, 0)), ANY],
        out_specs=[pl.BlockSpec((tm, P_ALL), lambda i: (i, 0)), pl.BlockSpec((tm, D_MODEL), lambda i: (i, 0))],
        out_shape=[jax.ShapeDtypeStruct((T, P_ALL), F32), jax.ShapeDtypeStruct((T, D_MODEL), BF16)],
        scratch_shapes=[pltpu.VMEM((D_MODEL, P_ALL), BF16), pltpu.SemaphoreType.DMA],
        compiler_params=_params(),
        name="inproj",
    )(x, g1, wp)


GLA_CHUNKS_PER_STEP = 16
GLA_ROWS = GLA_CHUNK * GLA_CHUNKS_PER_STEP


def _gla_masks():
    lane = lax.broadcasted_iota(jnp.int32, (1, GLA_QK), 1)
    return [(lane >= h * GLA_DK) & (lane < (h + 1) * GLA_DK) for h in range(GLA_HEADS)]


def _log_sigmoid(x):
    return jnp.minimum(x, 0.0) - jnp.log(1.0 + jnp.exp(-jnp.abs(x)))


def _sigmoid(x):
    return 1.0 / (1.0 + jnp.exp(-x))


def _head_cols(h):
    return slice(h * GLA_DV, (h + 1) * GLA_DV)


GLA_GROUP = 256


def _gla_step_constants():
    ri = lax.broadcasted_iota(jnp.int32, (GLA_GROUP, GLA_GROUP), 0)
    ci = lax.broadcasted_iota(jnp.int32, (GLA_GROUP, GLA_GROUP), 1)
    shift = GLA_CHUNK.bit_length() - 1
    same = lax.shift_right_logical(ri, shift) == lax.shift_right_logical(ci, shift)
    return same & (ri >= ci), same & (ri <= ci), _gla_masks()


def _by_group(fn, *arrays):
    outs = [fn(*[a[g * GLA_GROUP:(g + 1) * GLA_GROUP] for a in arrays]) for g in range(GLA_ROWS // GLA_GROUP)]
    if isinstance(outs[0], tuple):
        return tuple(jnp.concatenate(parts, axis=0) for parts in zip(*outs))
    return jnp.concatenate(outs, axis=0)


def _per_chunk(x):
    return x.reshape(GLA_CHUNKS_PER_STEP, GLA_CHUNK, x.shape[-1])


def _chunk_rows_of(x, c):
    return x[c * GLA_CHUNK:(c + 1) * GLA_CHUNK]


def _stack_masked(x, masks):
    return jnp.concatenate([jnp.where(m, x, 0.0) for m in masks], axis=0)


def _stack_head_cols(x):
    return jnp.concatenate([x[:, _head_cols(h)] for h in range(GLA_HEADS)], axis=0)


def _diag_blocks(full, masks):
    out = jnp.where(masks[0], full[:GLA_DV], 0.0)
    for h in range(1, GLA_HEADS):
        out = out + jnp.where(masks[h], full[h * GLA_DV:(h + 1) * GLA_DV], 0.0)
    return out


def _row_blocks_masked(full, masks):
    out = jnp.where(masks[0], full[:GLA_CHUNK], 0.0)
    for h in range(1, GLA_HEADS):
        out = out + jnp.where(masks[h], full[h * GLA_CHUNK:(h + 1) * GLA_CHUNK], 0.0)
    return out


def _gla_step_common(q, k, glow_b, w2, bg, tri):
    gpre = _dot(glow_b, w2) + bg
    glog = _log_sigmoid(gpre) / GLA_TAU
    b = _by_group(lambda g: _dot_exact_lhs(tri, g), glog)
    bl = jnp.sum(_per_chunk(glog), axis=1, keepdims=True)
    eb = jnp.exp(b)
    enb = jnp.exp(-b)
    eke = jnp.exp(jnp.broadcast_to(bl, (GLA_CHUNKS_PER_STEP, GLA_CHUNK, GLA_QK)).reshape(GLA_ROWS, GLA_QK) - b)
    return gpre, eb, enb, eke, jnp.exp(bl), (q * QK_SCALE) * eb, k * enb, k * eke


def gla_fwd(proj, w2p, bg, gn):
    T = proj.shape[0]
    n_steps = T // GLA_ROWS
    n_chunks = T // GLA_CHUNK

    def body(proj_ref, w2_ref, bg_ref, gn_ref, oa_ref, opre_ref, sprev_ref, st_ref):
        @pl.when(pl.program_id(0) == 0)
        def _():
            st_ref[...] = jnp.zeros_like(st_ref)

        causal, _, masks = _gla_step_constants()
        q = proj_ref[:, P_GQ:P_GQ + GLA_QK]
        k = proj_ref[:, P_GK:P_GK + GLA_QK]
        v = proj_ref[:, P_GV:P_GV + GLA_WIDTH]
        r = proj_ref[:, P_GR:P_GR + GLA_WIDTH]
        glow = proj_ref[:, P_GLOW:P_GLOW + LANE].astype(BF16)
        _, _, _, _, ebl, qd, ki, ke = _gla_step_common(q, k, glow, w2_ref[...].astype(BF16), bg_ref[...], causal.astype(BF16))
        ki_b = ki.astype(BF16)
        v_b = v.astype(BF16)
        o_heads = []
        for h in range(GLA_HEADS):
            def intra(qd_g, ki_g, v_g):
                att = jnp.where(causal, _dot_nt(qd_g, ki_g), 0.0)
                return _dot(att.astype(BF16), v_g)

            o_heads.append(_by_group(intra, jnp.where(masks[h], qd, 0.0).astype(BF16), ki_b, v_b[:, _head_cols(h)]))
        st = st_ref[...]
        states = []
        for c in range(GLA_CHUNKS_PER_STEP):
            states.append(st)
            sprev_ref[c] = st
            inc = _diag_blocks(_dot_tn(_chunk_rows_of(v_b, c), _chunk_rows_of(ke, c).astype(BF16)), masks)
            st = st * ebl[c] + inc
        st_ref[...] = st
        inter = []
        for c in range(GLA_CHUNKS_PER_STEP):
            qd_c = _stack_masked(_chunk_rows_of(qd, c), masks).astype(BF16)
            got = _dot_nt(qd_c, states[c].astype(BF16))
            inter.append(jnp.concatenate([got[h * GLA_CHUNK:(h + 1) * GLA_CHUNK] for h in range(GLA_HEADS)], axis=1))
        o = jnp.concatenate(o_heads, axis=1) + jnp.concatenate(inter, axis=0)
        opre_ref[...] = o
        on = jnp.concatenate([o[:, _head_cols(h)] * _rstd(o[:, _head_cols(h)]) for h in range(GLA_HEADS)], axis=1)
        oa_ref[...] = ((on * gn_ref[...]) * (r * _sigmoid(r))).astype(BF16)

    return pl.pallas_call(
        body,
        grid=(n_steps,),
        in_specs=[
            pl.BlockSpec((GLA_ROWS, P_A), lambda i: (i, 0)),
            pl.BlockSpec((LANE, GLA_QK), lambda i: (0, 0)),
            pl.BlockSpec((1, GLA_QK), lambda i: (0, 0)),
            pl.BlockSpec((1, GLA_WIDTH), lambda i: (0, 0)),
        ],
        out_specs=[
            pl.BlockSpec((GLA_ROWS, GLA_WIDTH), lambda i: (i, 0)),
            pl.BlockSpec((GLA_ROWS, GLA_WIDTH), lambda i: (i, 0)),
            pl.BlockSpec((GLA_CHUNKS_PER_STEP, GLA_DV, GLA_QK), lambda i: (i, 0, 0)),
        ],
        out_shape=[
            jax.ShapeDtypeStruct((T, GLA_WIDTH), BF16),
            jax.ShapeDtypeStruct((T, GLA_WIDTH), F32),
            jax.ShapeDtypeStruct((n_chunks, GLA_DV, GLA_QK), F32),
        ],
        scratch_shapes=[pltpu.VMEM((GLA_DV, GLA_QK), F32)],
        compiler_params=_params(),
        name="gla_fwd",
    )(proj, w2p, bg, gn)


def gla_bwd(proj, w2p, bg, gn, opre, sprev, dmixed, exchange=None):
    T = proj.shape[0]
    n_steps = T // GLA_ROWS

    def body(*refs):
        refs = _host_exchange(exchange, refs, 7, 4, pl.program_id(0), n_steps)
        proj_ref, w2_ref, bg_ref, gn_ref, opre_ref, sprev_ref, doa_ref, da_ref, dw2_ref, dbg_ref, dgn_ref, dst_ref = refs

        @pl.when(pl.program_id(0) == 0)
        def _():
            dst_ref[...] = jnp.zeros_like(dst_ref)
            dw2_ref[...] = jnp.zeros_like(dw2_ref)
            dbg_ref[...] = jnp.zeros_like(dbg_ref)
            dgn_ref[...] = jnp.zeros_like(dgn_ref)

        causal, causal_t, masks = _gla_step_constants()
        w2 = w2_ref[...].astype(BF16)
        gn = gn_ref[...]
        q = proj_ref[:, P_GQ:P_GQ + GLA_QK]
        k = proj_ref[:, P_GK:P_GK + GLA_QK]
        v_b = proj_ref[:, P_GV:P_GV + GLA_WIDTH].astype(BF16)
        r = proj_ref[:, P_GR:P_GR + GLA_WIDTH]
        glow = proj_ref[:, P_GLOW:P_GLOW + LANE].astype(BF16)
        o = opre_ref[...]
        doa = doa_ref[...]
        gpre, eb, enb, eke, ebl, qd, ki, ke = _gla_step_common(q, k, glow, w2, bg_ref[...], causal.astype(BF16))
        sig = _sigmoid(r)
        rs = jnp.concatenate([jnp.broadcast_to(_rstd(o[:, _head_cols(h)]), (GLA_ROWS, GLA_DV)) for h in range(GLA_HEADS)], axis=1)
        on = o * rs
        d_ong = doa * (r * sig)
        dr = doa * (on * gn) * (sig * (1.0 + r * (1.0 - sig)))
        dgn_ref[...] += jnp.sum(d_ong * on, axis=0, keepdims=True)
        d_on = d_ong * gn
        t = d_on * on
        mean_t = jnp.concatenate([jnp.broadcast_to(jnp.mean(t[:, _head_cols(h)], axis=-1, keepdims=True), (GLA_ROWS, GLA_DV))
                                  for h in range(GLA_HEADS)], axis=1)
        do_b = (rs * (d_on - on * mean_t)).astype(BF16)
        ki_b = ki.astype(BF16)
        ke_b = ke.astype(BF16)
        dqd = jnp.zeros_like(qd)
        dki = jnp.zeros_like(qd)
        dv_heads = []
        for h in range(GLA_HEADS):
            qd_h = jnp.where(masks[h], qd, 0.0).astype(BF16)
            do_h = do_b[:, _head_cols(h)]

            def intra(qd_g, ki_g, v_g, do_g):
                att = jnp.where(causal, _dot_nt(qd_g, ki_g), 0.0).astype(BF16)
                d_att = jnp.where(causal, _dot_nt(do_g, v_g), 0.0).astype(BF16)
                return _dot_tn(att, do_g), _dot(d_att, ki_g), _dot_tn(d_att, qd_g)

            dv_h, dqd_h, dki_h = _by_group(intra, qd_h, ki_b, v_b[:, _head_cols(h)], do_h)
            dv_heads.append(dv_h)
            dqd = dqd + jnp.where(masks[h], dqd_h, 0.0)
            dki = dki + dki_h
        states = [sprev_ref[c] for c in range(GLA_CHUNKS_PER_STEP)]
        dqd_inter, dst_adds = [], []
        for c in range(GLA_CHUNKS_PER_STEP):
            do_c = _stack_head_cols(_chunk_rows_of(do_b, c))
            dqd_inter.append(_row_blocks_masked(_dot(do_c, states[c].astype(BF16)), masks))
            dst_adds.append(_diag_blocks(_dot_tn(_chunk_rows_of(do_b, c), _chunk_rows_of(qd, c).astype(BF16)), masks))
        dst = dst_ref[...]
        dsts, debls = [None] * GLA_CHUNKS_PER_STEP, [None] * GLA_CHUNKS_PER_STEP
        for c in reversed(range(GLA_CHUNKS_PER_STEP)):
            dsts[c] = dst
            debls[c] = jnp.sum(dst * states[c], axis=0, keepdims=True)
            dst = dst * ebl[c] + dst_adds[c]
        dst_ref[...] = dst
        dv_inter, dke = [], []
        for c in range(GLA_CHUNKS_PER_STEP):
            dst_b = dsts[c].astype(BF16)
            got = _dot_nt(_stack_masked(_chunk_rows_of(ke, c), masks).astype(BF16), dst_b)
            dv_inter.append(jnp.concatenate([got[h * GLA_CHUNK:(h + 1) * GLA_CHUNK] for h in range(GLA_HEADS)], axis=1))
            dke.append(_row_blocks_masked(_dot(_stack_head_cols(_chunk_rows_of(v_b, c)), dst_b), masks))
        dqd = dqd + jnp.concatenate(dqd_inter, axis=0)
        dke = jnp.concatenate(dke, axis=0)
        dv = jnp.concatenate(dv_heads, axis=1) + jnp.concatenate(dv_inter, axis=0)
        dkk = dke * ke
        dbl = jnp.sum(_per_chunk(dkk), axis=1, keepdims=True) + jnp.stack(debls) * ebl
        last_row = lax.broadcasted_iota(jnp.int32, (GLA_CHUNKS_PER_STEP, GLA_CHUNK, GLA_QK), 1) == GLA_CHUNK - 1
        db = dqd * qd - dki * ki - dkk + jnp.where(last_row, dbl, 0.0).reshape(GLA_ROWS, GLA_QK)
        tri_t = causal_t.astype(BF16)
        dglog = _by_group(lambda g: _dot_exact_lhs(tri_t, g), db)
        dgpre = (dglog / GLA_TAU) * _sigmoid(-gpre)
        dgpre_b = dgpre.astype(BF16)
        da_ref[...] = jnp.concatenate(
            [dqd * eb * QK_SCALE, dki * enb + dke * eke, dv, dr, _dot_nt(dgpre_b, w2)], axis=1).astype(BF16)
        dw2_ref[...] += _dot_tn(glow, dgpre_b)
        dbg_ref[...] += jnp.sum(dgpre, axis=0, keepdims=True)

    rev = lambda i: (n_steps - 1 - i, 0)
    return _hosted_call(
        exchange,
        body,
        grid=(n_steps,),
        in_specs=[
            pl.BlockSpec((GLA_ROWS, P_A), rev),
            pl.BlockSpec((LANE, GLA_QK), lambda i: (0, 0)),
            pl.BlockSpec((1, GLA_QK), lambda i: (0, 0)),
            pl.BlockSpec((1, GLA_WIDTH), lambda i: (0, 0)),
            pl.BlockSpec((GLA_ROWS, GLA_WIDTH), rev),
            pl.BlockSpec((GLA_CHUNKS_PER_STEP, GLA_DV, GLA_QK), lambda i: (n_steps - 1 - i, 0, 0)),
            pl.BlockSpec((GLA_ROWS, GLA_WIDTH), rev),
        ],
        out_specs=[
            pl.BlockSpec((GLA_ROWS, P_A), rev),
            pl.BlockSpec((LANE, GLA_QK), lambda i: (0, 0)),
            pl.BlockSpec((1, GLA_QK), lambda i: (0, 0)),
            pl.BlockSpec((1, GLA_WIDTH), lambda i: (0, 0)),
        ],
        out_shape=[
            jax.ShapeDtypeStruct((T, P_A), BF16),
            jax.ShapeDtypeStruct((LANE, GLA_QK), F32),
            jax.ShapeDtypeStruct((1, GLA_QK), F32),
            jax.ShapeDtypeStruct((1, GLA_WIDTH), F32),
        ],
        scratch_shapes=[pltpu.VMEM((GLA_DV, GLA_QK), F32)],
        compiler_params=_params(),
        name="gla_bwd",
        args=(proj, w2p, bg, gn, opre, sprev, dmixed),
    )


def _t5_bucket(dist):
    max_exact = REL_BUCKETS // 2
    n = np.maximum(dist, 0)
    large = max_exact + (np.log(np.maximum(n, 1) / max_exact) / math.log(REL_MAX_DIST / max_exact)
                         * (REL_BUCKETS - max_exact)).astype(np.int32)
    large = np.minimum(large, REL_BUCKETS - 1)
    return np.where(n < max_exact, n, large).astype(np.int32)


SUBLANES = 8


def _bucket_rows():
    steps = DSA_BLOCK - np.arange(2 * DSA_BLOCK)
    in_band = (steps >= 0) & (steps <= DSA_SPAN)
    rows = np.stack([np.where(in_band, _t5_bucket(steps * d), -1) for d in DSA_DILATIONS]).astype(np.int32)
    return np.broadcast_to(rows[:, None, :], (len(DSA_DILATIONS), SUBLANES, 2 * DSA_BLOCK)).copy()


def bias_tables(rel_bias):
    ids = jnp.asarray(_bucket_rows())
    nd = len(DSA_DILATIONS)

    def body(rel_ref, ids_ref, tab_ref):
        idt = ids_ref[0]
        for h in range(DSA_HEADS):
            row = jnp.where(idt < 0, NEG, 0.0).astype(F32)
            for b in range(REL_BUCKETS):
                row = jnp.where(idt == b, rel_ref[b, h], row)
            full = jnp.broadcast_to(row[0:1], (DSA_BLOCK, 2 * DSA_BLOCK))
            tab_ref[0, h] = pltpu.roll(full, 0, 1, stride=1, stride_axis=0)

    return pl.pallas_call(
        body,
        grid=(nd,),
        in_specs=[pl.BlockSpec(memory_space=pltpu.SMEM), pl.BlockSpec((1, SUBLANES, 2 * DSA_BLOCK), lambda d: (d, 0, 0))],
        out_specs=pl.BlockSpec((1, DSA_HEADS, DSA_BLOCK, 2 * DSA_BLOCK), lambda d: (d, 0, 0, 0)),
        out_shape=jax.ShapeDtypeStruct((nd, DSA_HEADS, DSA_BLOCK, 2 * DSA_BLOCK), F32),
        compiler_params=_params(),
        name="bias_tables",
    )(rel_bias, ids)


def _bucket_ids():
    L = DSA_BLOCK
    steps = L + np.arange(L)[:, None] - np.arange(2 * L)[None, :]
    in_band = (steps >= 0) & (steps <= DSA_SPAN)
    return np.stack([np.where(in_band, _t5_bucket(steps * d), -1) for d in DSA_DILATIONS]).astype(np.int32)


def bias_tables_bwd(dtab):
    ids = jnp.asarray(_bucket_ids())
    nd = len(DSA_DILATIONS)

    def body(dtab_ref, ids_ref, drel_ref):
        @pl.when(pl.program_id(0) == 0)
        def _():
            for b in range(REL_BUCKETS):
                for h in range(DSA_HEADS):
                    drel_ref[b, h] = 0.0

        idt = ids_ref[0]
        for b in range(REL_BUCKETS):
            in_bucket = idt == b
            for h in range(DSA_HEADS):
                drel_ref[b, h] += jnp.sum(jnp.where(in_bucket, dtab_ref[0, h], 0.0))

    return pl.pallas_call(
        body,
        grid=(nd,),
        in_specs=[pl.BlockSpec((1, DSA_HEADS, DSA_BLOCK, 2 * DSA_BLOCK), lambda d: (d, 0, 0, 0)),
                  pl.BlockSpec((1, DSA_BLOCK, 2 * DSA_BLOCK), lambda d: (d, 0, 0))],
        out_specs=pl.BlockSpec(memory_space=pltpu.SMEM),
        out_shape=jax.ShapeDtypeStruct((REL_BUCKETS, DSA_HEADS), F32),
        compiler_params=_params(),
        name="bias_tables_bwd",
    )(dtab, ids)


DSA_PAIRS = DSA_HEADS // 2
DSA_UNROLL = 16
DSA_COMBINE_ROWS = 256


def _dsa_units(d):
    return d, DSA_SUPER // (DSA_BLOCK * d)


def _dsa_specs(T):
    nsb = T // DSA_SUPER
    qcol, kcol, vcol = P_DQ // LANE, P_DK // LANE, P_DV // LANE
    return nsb, qcol, kcol, vcol


def _head_lane_mask():
    return lax.broadcasted_iota(jnp.int32, (1, LANE), 1) < DSA_DH


def _fill_tile_variants(tab_ref, variants):
    col = lax.broadcasted_iota(jnp.int32, (2 * DSA_BLOCK, 2 * DSA_BLOCK), 1)
    for di in range(len(DSA_DILATIONS)):
        tile = tab_ref[di, 0]
        variants[di, 0] = tile
        variants[di, 1] = jnp.where(col < DSA_BLOCK, NEG, tile)


def _tile_variants_scratch():
    return pltpu.VMEM((len(DSA_DILATIONS), 2, 2 * DSA_BLOCK, 2 * DSA_BLOCK), F32)


def _pair_tiles(tab):
    return tab.reshape(len(DSA_DILATIONS), DSA_PAIRS, 2 * DSA_BLOCK, 2 * DSA_BLOCK)


def _stack_heads(t, head0):
    return jnp.concatenate([jnp.where(head0, t, 0.0), jnp.where(head0, 0.0, t)], axis=0)


def dsa_fwd(proj, tab, exchange=None):
    T = proj.shape[0]
    nsb, qcol, kcol, vcol = _dsa_specs(T)
    S = DSA_SUPER

    def body(*refs):
        refs = _host_exchange(exchange, refs, 6, 2, pl.program_id(0) * nsb + pl.program_id(1), DSA_PAIRS * nsb)
        q_ref, kp_ref, kc_ref, vp_ref, vc_ref, tab_ref, out_ref, lse_ref, kk, vv, ob, lb, tiles = refs
        sb = pl.program_id(1)
        kk[0:S, :] = kp_ref[...]
        kk[S:2 * S, :] = kc_ref[...]
        vv[0:S, :] = vp_ref[...]
        vv[S:2 * S, :] = vc_ref[...]
        head0 = _head_lane_mask()
        pl.when(sb == 0)(functools.partial(_fill_tile_variants, tab_ref, tiles))

        for di, d in enumerate(DSA_DILATIONS):
            n_res, n_blk = _dsa_units(d)

            def unit(u, carry, di=di, d=d, n_blk=n_blk):
                r = u // n_blk
                c = u % n_blk
                q0 = r + d * DSA_BLOCK * c
                qrows = pl.ds(q0, DSA_BLOCK, stride=d) if d > 1 else pl.ds(q0, DSA_BLOCK)
                krows = pl.ds(S + q0 - d * DSA_BLOCK, 2 * DSA_BLOCK, stride=d) if d > 1 else pl.ds(S + q0 - DSA_BLOCK, 2 * DSA_BLOCK)
                q2 = q_ref[qrows, :] * QK_SCALE
                k2 = kk[krows, :].astype(BF16)
                v2 = vv[krows, :].astype(BF16)
                qs = _stack_heads(q2, head0).astype(BF16)
                s = _dot_nt(qs, k2) + tiles[di, ((sb == 0) & (c == 0)).astype(jnp.int32)]
                m = jnp.max(s, axis=-1, keepdims=True)
                p = jnp.exp(s - m)
                den = jnp.sum(p, axis=-1, keepdims=True)
                o = _dot(p.astype(BF16), v2) / den
                l = jnp.broadcast_to(m + jnp.log(den), (2 * DSA_BLOCK, LANE))
                ob[di, qrows, :] = jnp.where(head0, o[:DSA_BLOCK], o[DSA_BLOCK:])
                lb[di, qrows, :] = jnp.where(head0, l[:DSA_BLOCK], l[DSA_BLOCK:])
                return carry

            lax.fori_loop(0, n_res * n_blk, unit, 0, unroll=DSA_UNROLL)

        def combine(i, carry):
            rows = pl.ds(pl.multiple_of(i * DSA_COMBINE_ROWS, DSA_COMBINE_ROWS), DSA_COMBINE_ROWS)
            l0, l1, l2 = lb[0, rows, :], lb[1, rows, :], lb[2, rows, :]
            mx = jnp.maximum(jnp.maximum(l0, l1), l2)
            e0, e1, e2 = jnp.exp(l0 - mx), jnp.exp(l1 - mx), jnp.exp(l2 - mx)
            den = e0 + e1 + e2
            out_ref[rows, :] = (e0 * ob[0, rows, :] + e1 * ob[1, rows, :] + e2 * ob[2, rows, :]) / den
            lse_ref[rows, :] = mx + jnp.log(den)
            return carry

        lax.fori_loop(0, S // DSA_COMBINE_ROWS, combine, 0)

    prev = lambda col: (lambda hp, sb: (jnp.maximum(sb - 1, 0), col + hp))
    cur = lambda col: (lambda hp, sb: (sb, col + hp))
    blk = lambda f: pl.BlockSpec((S, LANE), f)
    return _hosted_call(
        exchange,
        body,
        grid=(DSA_PAIRS, nsb),
        in_specs=[blk(cur(qcol)), blk(prev(kcol)), blk(cur(kcol)), blk(prev(vcol)), blk(cur(vcol)),
                  pl.BlockSpec((len(DSA_DILATIONS), 1, 2 * DSA_BLOCK, 2 * DSA_BLOCK), lambda hp, sb: (0, hp, 0, 0))],
        out_specs=[blk(lambda hp, sb: (sb, hp)), blk(lambda hp, sb: (sb, hp))],
        out_shape=[jax.ShapeDtypeStruct((T, DSA_WIDTH), F32), jax.ShapeDtypeStruct((T, DSA_WIDTH), F32)],
        scratch_shapes=[pltpu.VMEM((2 * S, LANE), F32), pltpu.VMEM((2 * S, LANE), F32),
                        pltpu.VMEM((len(DSA_DILATIONS), S, LANE), F32), pltpu.VMEM((len(DSA_DILATIONS), S, LANE), F32),
                        _tile_variants_scratch()],
        compiler_params=_params(("arbitrary", "arbitrary")),
        name="dsa_fwd",
        args=(proj, proj, proj, proj, proj, _pair_tiles(tab)),
    )


def dsa_bwd(proj, tab, ob_out, lse, dmixed, exchange=None):
    T = proj.shape[0]
    nsb, qcol, kcol, vcol = _dsa_specs(T)
    S = DSA_SUPER
    nd = len(DSA_DILATIONS)
    ocol = GLA_WIDTH // LANE

    def body(*refs):
        refs = _host_exchange(exchange, refs, 9, 4, pl.program_id(0) * nsb + pl.program_id(1), DSA_PAIRS * nsb)
        (q_ref, kp_ref, kc_ref, vp_ref, vc_ref, tab_ref, o_ref, lse_ref, do_ref,
         dq_ref, dk_ref, dv_ref, dtab_ref, kk, vv, dqa, dkk, dvv, tiles) = refs
        j = pl.program_id(1)
        sb = nsb - 1 - j
        kk[0:S, :] = kp_ref[...]
        kk[S:2 * S, :] = kc_ref[...]
        vv[0:S, :] = vp_ref[...]
        vv[S:2 * S, :] = vc_ref[...]
        head0 = _head_lane_mask()
        pl.when(j == 0)(functools.partial(_fill_tile_variants, tab_ref, tiles))

        @pl.when(j == 0)
        def _():
            dtab_ref[...] = jnp.zeros_like(dtab_ref)
            dkk[S:2 * S, :] = jnp.zeros((S, LANE), F32)
            dvv[S:2 * S, :] = jnp.zeros((S, LANE), F32)

        @pl.when(j > 0)
        def _():
            dkk[S:2 * S, :] = dkk[0:S, :]
            dvv[S:2 * S, :] = dvv[0:S, :]

        dkk[0:S, :] = jnp.zeros((S, LANE), F32)
        dvv[0:S, :] = jnp.zeros((S, LANE), F32)
        dqa[...] = jnp.zeros_like(dqa)

        for di, d in enumerate(DSA_DILATIONS):
            n_res, n_blk = _dsa_units(d)

            def unit(u, carry, di=di, d=d, n_blk=n_blk):
                r = u // n_blk
                c = u % n_blk
                q0 = r + d * DSA_BLOCK * c
                qrows = pl.ds(q0, DSA_BLOCK, stride=d) if d > 1 else pl.ds(q0, DSA_BLOCK)
                krows = pl.ds(S + q0 - d * DSA_BLOCK, 2 * DSA_BLOCK, stride=d) if d > 1 else pl.ds(S + q0 - DSA_BLOCK, 2 * DSA_BLOCK)
                q2 = q_ref[qrows, :] * QK_SCALE
                k2 = kk[krows, :].astype(BF16)
                v2 = vv[krows, :].astype(BF16)
                do2 = do_ref[qrows, :]
                o2 = o_ref[qrows, :]
                l2 = lse_ref[qrows, :]
                qs = _stack_heads(q2, head0).astype(BF16)
                dos = _stack_heads(do2, head0)
                dos_b = dos.astype(BF16)
                delta = jnp.sum(dos * jnp.concatenate([o2, o2], axis=0), axis=-1, keepdims=True)
                lse = jnp.concatenate([jnp.max(jnp.where(head0, l2, -jnp.inf), axis=-1, keepdims=True),
                                       jnp.max(jnp.where(head0, -jnp.inf, l2), axis=-1, keepdims=True)], axis=0)
                s = _dot_nt(qs, k2) + tiles[di, ((sb == 0) & (c == 0)).astype(jnp.int32)]
                p = jnp.exp(s - lse)
                ds = p * (_dot_nt(dos_b, v2) - delta)
                dtab_ref[di, 0] += ds
                ds_b = ds.astype(BF16)
                dq = _dot(ds_b, k2)
                dqa[qrows, :] += jnp.where(head0, dq[:DSA_BLOCK], dq[DSA_BLOCK:]) * QK_SCALE
                dkk[krows, :] += _dot_tn(ds_b, qs)
                dvv[krows, :] += _dot_tn(p.astype(BF16), dos_b)
                return carry

            lax.fori_loop(0, n_res * n_blk, unit, 0, unroll=DSA_UNROLL)

        dq_ref[...] = dqa[...].astype(BF16)
        dk_ref[...] = dkk[S:2 * S, :].astype(BF16)
        dv_ref[...] = dvv[S:2 * S, :].astype(BF16)

    prev = lambda col: (lambda hp, j: (jnp.maximum(nsb - 2 - j, 0), col + hp))
    cur = lambda col: (lambda hp, j: (nsb - 1 - j, col + hp))
    blk = lambda f: pl.BlockSpec((S, LANE), f)
    out_blk = blk(lambda hp, j: (nsb - 1 - j, hp))
    tab_blk = pl.BlockSpec((nd, 1, 2 * DSA_BLOCK, 2 * DSA_BLOCK), lambda hp, j: (0, hp, 0, 0))
    dq, dk, dv, dtab, *carried = _hosted_call(
        exchange,
        body,
        grid=(DSA_PAIRS, nsb),
        in_specs=[blk(cur(qcol)), blk(prev(kcol)), blk(cur(kcol)), blk(prev(vcol)), blk(cur(vcol)), tab_blk,
                  out_blk, out_blk, blk(cur(ocol))],
        out_specs=[out_blk, out_blk, out_blk, tab_blk],
        out_shape=[jax.ShapeDtypeStruct((T, DSA_WIDTH), BF16)] * 3
        + [jax.ShapeDtypeStruct((nd, DSA_PAIRS, 2 * DSA_BLOCK, 2 * DSA_BLOCK), F32)],
        scratch_shapes=[pltpu.VMEM((2 * S, LANE), F32), pltpu.VMEM((2 * S, LANE), F32), pltpu.VMEM((S, LANE), F32),
                        pltpu.VMEM((2 * S, LANE), F32), pltpu.VMEM((2 * S, LANE), F32), _tile_variants_scratch()],
        compiler_params=_params(("arbitrary", "arbitrary")),
        name="dsa_bwd",
        args=(proj, proj, proj, proj, proj, _pair_tiles(tab), ob_out, lse, dmixed),
    )
    return (dq, dk, dv, dtab.reshape(nd, DSA_HEADS, DSA_BLOCK, 2 * DSA_BLOCK), *carried)


FF_BLOCKS = 4
FF_BLOCK = D_FF // FF_BLOCKS


def post_fused(x, oa, ob, tgt, g2, gf, wout, wff1, wff2):
    T = x.shape[0]
    tm = 256
    inv_d = 1.0 / D_MODEL

    def body(x_ref, oa_ref, ob_ref, tgt_ref, g2_ref, gf_ref, wout_hbm, wff1_hbm, wff2_hbm,
             mixed_ref, nm_ref, a_ref, dpre_ref, dh2_ref, dh1_ref, dmixed_ref, loss_ref, dgf_ref, dg2_ref,
             wout_v, wff1_v, wff2_v, sems):
        @pl.when(pl.program_id(0) == 0)
        def _():
            cps = [pltpu.make_async_copy(s, d, sems.at[i])
                   for i, (s, d) in enumerate([(wout_hbm, wout_v), (wff1_hbm, wff1_v), (wff2_hbm, wff2_v)])]
            for cp in cps:
                cp.start()
            for cp in cps:
                cp.wait()
            loss_ref[...] = jnp.zeros_like(loss_ref)
            dgf_ref[...] = jnp.zeros_like(dgf_ref)
            dg2_ref[...] = jnp.zeros_like(dg2_ref)

        mixed = jnp.concatenate([oa_ref[...], ob_ref[...].astype(BF16)], axis=1)
        mixed_ref[...] = mixed
        h1 = x_ref[...] + _dot(mixed, wout_v[...])
        rs1 = _rstd(h1)
        hn1 = h1 * rs1
        g2 = g2_ref[...]
        nm = (hn1 * g2).astype(BF16)
        nm_ref[...] = nm
        relu = []
        mlp = jnp.zeros((tm, D_MODEL), F32)
        for j in range(FF_BLOCKS):
            cols = slice(j * FF_BLOCK, (j + 1) * FF_BLOCK)
            r_j = jnp.maximum(_dot(nm, wff1_v[j]), 0.0)
            a_j = (r_j * r_j).astype(BF16)
            a_ref[:, cols] = a_j
            relu.append(r_j)
            mlp = mlp + _dot(a_j, wff2_v[cols, :])
        h2 = h1 + mlp
        rsf = _rstd(h2)
        hnf = h2 * rsf
        gf = gf_ref[...]
        diff = hnf * gf - tgt_ref[...]
        loss_ref[...] += 0.5 * jnp.sum(jnp.sum(diff * diff, axis=-1, keepdims=True) * inv_d, axis=0, keepdims=True)
        dy = diff * inv_d
        dgf_ref[...] += jnp.sum(dy * hnf, axis=0, keepdims=True)
        dhnf = dy * gf
        dh2 = rsf * (dhnf - hnf * jnp.mean(dhnf * hnf, axis=-1, keepdims=True))
        dh2_b = dh2.astype(BF16)
        dh2_ref[...] = dh2_b
        dnm = jnp.zeros((tm, D_MODEL), F32)
        for j in range(FF_BLOCKS):
            cols = slice(j * FF_BLOCK, (j + 1) * FF_BLOCK)
            dpre_j = (_dot_nt(dh2_b, wff2_v[cols, :]) * (2.0 * relu[j])).astype(BF16)
            dpre_ref[:, cols] = dpre_j
            dnm = dnm + _dot_nt(dpre_j, wff1_v[j])
        dg2_ref[...] += jnp.sum(dnm * hn1, axis=0, keepdims=True)
        dhn1 = dnm * g2
        dh1 = dh2 + rs1 * (dhn1 - hn1 * jnp.mean(dhn1 * hn1, axis=-1, keepdims=True))
        dh1_ref[...] = dh1
        dmixed_ref[...] = _dot_nt(dh1.astype(BF16), wout_v[...])

    row = lambda w: pl.BlockSpec((tm, w), lambda i: (i, 0))
    vec = lambda w: pl.BlockSpec((1, w), lambda i: (0, 0))
    return pl.pallas_call(
        body,
        grid=(T // tm,),
        in_specs=[row(D_MODEL), row(GLA_WIDTH), row(DSA_WIDTH), row(D_MODEL), vec(D_MODEL), vec(D_MODEL), ANY, ANY, ANY],
        out_specs=[row(D_MODEL), row(D_MODEL), row(D_FF), row(D_FF), row(D_MODEL), row(D_MODEL), row(D_MODEL),
                   vec(1), vec(D_MODEL), vec(D_MODEL)],
        out_shape=[
            jax.ShapeDtypeStruct((T, D_MODEL), BF16),
            jax.ShapeDtypeStruct((T, D_MODEL), BF16),
            jax.ShapeDtypeStruct((T, D_FF), BF16),
            jax.ShapeDtypeStruct((T, D_FF), BF16),
            jax.ShapeDtypeStruct((T, D_MODEL), BF16),
            jax.ShapeDtypeStruct((T, D_MODEL), F32),
            jax.ShapeDtypeStruct((T, D_MODEL), F32),
            jax.ShapeDtypeStruct((1, 1), F32),
            jax.ShapeDtypeStruct((1, D_MODEL), F32),
            jax.ShapeDtypeStruct((1, D_MODEL), F32),
        ],
        scratch_shapes=[pltpu.VMEM((D_MODEL, D_MODEL), BF16), pltpu.VMEM((FF_BLOCKS, D_MODEL, FF_BLOCK), BF16),
                        pltpu.VMEM((D_FF, D_MODEL), BF16), pltpu.SemaphoreType.DMA((3,))],
        compiler_params=_params(),
        name="post_fused",
    )(x, oa, ob, tgt, g2, gf, wout, wff1, wff2)


WGRAD_TOKENS = 2048


def wgrad(a, b, name, bm=None, bn=None, col_blocked=False):
    T, M = a.shape
    N = b.shape[1]
    bm = M if bm is None else bm
    bn = N if bn is None else bn
    tk = min(WGRAD_TOKENS, T)
    n_k = T // tk

    def body(a_ref, b_ref, o_ref, acc_ref):
        part = _dot_tn(a_ref[...].astype(BF16), b_ref[...].astype(BF16))
        out = o_ref.at[0] if col_blocked else o_ref
        k = pl.program_id(2)
        if n_k == 1:
            out[...] = part.astype(BF16)
            return

        @pl.when(k == 0)
        def _():
            acc_ref[...] = part

        @pl.when((k > 0) & (k < n_k - 1))
        def _():
            acc_ref[...] += part

        @pl.when(k == n_k - 1)
        def _():
            out[...] = (acc_ref[...] + part).astype(BF16)

    if col_blocked:
        assert bm == M
        out_spec = pl.BlockSpec((1, M, bn), lambda i, j, k: (j, 0, 0))
        out_shape = jax.ShapeDtypeStruct((N // bn, M, bn), BF16)
    else:
        out_spec = pl.BlockSpec((bm, bn), lambda i, j, k: (i, j))
        out_shape = jax.ShapeDtypeStruct((M, N), BF16)
    return pl.pallas_call(
        body,
        grid=(M // bm, N // bn, n_k),
        in_specs=[pl.BlockSpec((tk, bm), lambda i, j, k: (k, i)), pl.BlockSpec((tk, bn), lambda i, j, k: (k, j))],
        out_specs=out_spec,
        out_shape=out_shape,
        scratch_shapes=[pltpu.VMEM((bm, bn), F32)],
        compiler_params=_params(("arbitrary", "arbitrary", "arbitrary")),
        name=name,
    )(a, b)


def wgrad_cat(a, bs, name):
    T, M = a.shape
    widths = [b.shape[1] for b in bs]
    starts = [sum(widths[:i]) for i in range(len(bs))]
    N = sum(widths)
    tk = min(WGRAD_TOKENS // 2, T)
    n_k = T // tk

    def body(a_ref, *rest):
        b_refs, o_ref, acc_ref = rest[:len(bs)], rest[len(bs)], rest[len(bs) + 1]
        k = pl.program_id(0)

        @pl.when(k == 0)
        def _():
            acc_ref[...] = jnp.zeros_like(acc_ref)

        a_t = a_ref[...]
        for b_ref, start, width in zip(b_refs, starts, widths):
            acc_ref[:, start:start + width] += _dot_tn(a_t, b_ref[...])

        @pl.when(k == n_k - 1)
        def _():
            o_ref[...] = acc_ref[...].astype(BF16)

    return pl.pallas_call(
        body,
        grid=(n_k,),
        in_specs=[pl.BlockSpec((tk, M), lambda k: (k, 0))] + [pl.BlockSpec((tk, w), lambda k: (k, 0)) for w in widths],
        out_specs=pl.BlockSpec((M, N), lambda k: (0, 0)),
        out_shape=jax.ShapeDtypeStruct((M, N), BF16),
        scratch_shapes=[pltpu.VMEM((M, N), F32)],
        compiler_params=_params(),
        name=name,
    )(a, *bs)


def dx_final(x, dh1, g1, da, dq, dk, dv, wp, exchange=None):
    T = x.shape[0]
    tm = 512

    def body(*refs):
        refs = _host_exchange(exchange, refs, 8, 2, pl.program_id(0), T // tm)
        x_ref, dh1_ref, g_ref, da_ref, dq_ref, dk_ref, dv_ref, w_hbm, dx_ref, dg_ref, w_vmem, sem = refs

        @pl.when(pl.program_id(0) == 0)
        def _():
            _load_once(w_hbm, w_vmem, sem)
            dg_ref[...] = jnp.zeros_like(dg_ref)

        dnx = (_dot_nt(da_ref[...], w_vmem[:, 0:P_A]) + _dot_nt(dq_ref[...], w_vmem[:, P_DQ:P_DQ + DSA_WIDTH])
               + _dot_nt(dk_ref[...], w_vmem[:, P_DK:P_DK + DSA_WIDTH]) + _dot_nt(dv_ref[...], w_vmem[:, P_DV:P_DV + DSA_WIDTH]))
        xf = x_ref[...]
        rs = _rstd(xf)
        hn = xf * rs
        dg_ref[...] += jnp.sum(dnx * hn, axis=0, keepdims=True)
        dhn = dnx * g_ref[...]
        dx_ref[...] = dh1_ref[...] + rs * (dhn - hn * jnp.mean(dhn * hn, axis=-1, keepdims=True))

    row = lambda w: pl.BlockSpec((tm, w), lambda i: (i, 0))
    vec = pl.BlockSpec((1, D_MODEL), lambda i: (0, 0))
    return _hosted_call(
        exchange,
        body,
        grid=(T // tm,),
        in_specs=[row(D_MODEL), row(D_MODEL), vec, row(P_A), row(DSA_WIDTH), row(DSA_WIDTH), row(DSA_WIDTH), ANY],
        out_specs=[row(D_MODEL), vec],
        out_shape=[jax.ShapeDtypeStruct((T, D_MODEL), F32), jax.ShapeDtypeStruct((1, D_MODEL), F32)],
        scratch_shapes=[pltpu.VMEM((D_MODEL, P_ALL), BF16), pltpu.SemaphoreType.DMA],
        compiler_params=_params(),
        name="dx_final",
        args=(x, dh1, g1, da, dq, dk, dv, wp),
    )


def adamw(w, g, m, v, name):
    R, C = w.shape
    br = 256 if R % 256 == 0 else R

    def body(w_ref, g_ref, m_ref, v_ref, d_ref, nm_ref, nv_ref):
        d_ref[...], nm_ref[...], nv_ref[...] = _adamw_math(w_ref[...], g_ref[...], m_ref[...], v_ref[...])

    spec = pl.BlockSpec((br, C), lambda i: (i, 0))
    return pl.pallas_call(
        body,
        grid=(R // br,),
        in_specs=[spec] * 4,
        out_specs=[spec] * 3,
        out_shape=[jax.ShapeDtypeStruct((R, C), F32)] * 3,
        compiler_params=_params(),
        name=name,
    )(w, g, m, v)


def _place():
    return lax.axis_index("x"), lax.axis_index("y"), lax.axis_index("c")


def _other_chips(x, y):
    return [(1 - x, y), (x, 1 - y), (1 - x, 1 - y)]


class Exchange:
    def __init__(self, kind, arrays):
        self.kind, self.arrays, self.n = kind, arrays, len(arrays)
        self.slots = 4 if kind == "gather" else 8

    def out_shapes(self):
        if self.kind == "gather":
            return [jax.ShapeDtypeStruct((4,) + s.shape, s.dtype) for s in self.arrays]
        return [jax.ShapeDtypeStruct((8,) + s.shape[1:], s.dtype) for s in self.arrays]

    def sems(self):
        return [pltpu.SemaphoreType.DMA((self.n, 19)), pltpu.SemaphoreType.DMA((self.n, 19))]

    def phases(self, ins, outs, send_sems, recv_sems):
        n, scatter = self.n, self.kind == "scatter"
        x, y, c = _place()
        me, sib = (x, y, c), (x, y, 1 - c)
        mine = 2 * x + y
        chips = _other_chips(x, y)
        own_pair = 18

        def region(a, slot, half):
            h = outs[a].shape[1] // 2
            return outs[a].at[slot, pl.ds(half * h, h)]

        def copy(a, k, slot, half, to, src=None):
            return pltpu.make_async_remote_copy(
                src_ref=region(a, slot, half) if src is None else src, dst_ref=region(a, slot, half),
                send_sem=send_sems.at[a, k], recv_sem=recv_sems.at[a, k], device_id=to, device_id_type=MESH)

        def over_ici(t, to_core, from_core):
            return 4 * t + 2 * to_core + from_core

        def passed_on(t, from_core):
            return 12 + 2 * t + from_core

        senders = [(t, cc) for t in range(3) for cc in ((0, 1) if scatter else (c,))]

        def slot_of(t, cc):
            cx, cy = chips[t]
            return 2 * (2 * cx + cy) + cc if scatter else 2 * cx + cy

        def first_copies():
            cps = []
            for a in range(n):
                h = outs[a].shape[1] // 2
                for t, (cx, cy) in enumerate(chips):
                    if scatter:
                        for half in (0, 1):
                            cps.append(copy(a, over_ici(t, half, c), 2 * mine + c, half, (cx, cy, half),
                                            src=ins[a].at[2 * cx + cy, pl.ds(half * h, h)]))
                    else:
                        cps.append(copy(a, over_ici(t, c, c), mine, c, (cx, cy, c), src=ins[a].at[pl.ds(c * h, h)]))
                if scatter:
                    cps.append(pltpu.make_async_remote_copy(
                        src_ref=ins[a].at[mine], dst_ref=outs[a].at[2 * mine + c], send_sem=send_sems.at[a, own_pair],
                        recv_sem=recv_sems.at[a, own_pair], device_id=sib, device_id_type=MESH))
            return cps

        def forward_copies():
            return [copy(a, passed_on(t, cc), slot_of(t, cc), c, sib) for a in range(n) for t, cc in senders]

        def start():
            for cp in first_copies():
                cp.start()

        def forward():
            fws = iter(forward_copies())
            for a in range(n):
                for t, cc in senders:
                    copy(a, over_ici(t, c, cc), slot_of(t, cc), c, me).wait_recv()
                    next(fws).start()

        def finish():
            for a in range(n):
                for t, cc in senders:
                    from_core = cc if scatter else 1 - c
                    copy(a, passed_on(t, from_core), slot_of(t, from_core), 1 - c, me).wait_recv()
                if scatter:
                    pltpu.make_async_remote_copy(
                        src_ref=ins[a].at[mine], dst_ref=outs[a].at[2 * mine + 1 - c], send_sem=send_sems.at[a, own_pair],
                        recv_sem=recv_sems.at[a, own_pair], device_id=me, device_id_type=MESH).wait_recv()
            for cp in first_copies() + forward_copies():
                cp.wait_send()

        return start, forward, finish

    def fill_own(self, outs):
        x, y, c = _place()
        if self.kind == "gather":
            return [lax.dynamic_update_index_in_dim(o, s, 2 * x + y, 0) for o, s in zip(outs, self.arrays)]
        return list(outs)

    def run(self, name):
        n = self.n

        def body(*refs):
            start, forward, finish = self.phases(refs[:n], refs[n:2 * n], *refs[2 * n:])
            start()
            forward()
            finish()

        outs = pl.pallas_call(
            body, in_specs=[ANY] * n, out_specs=[ANY] * n, out_shape=self.out_shapes(), scratch_shapes=self.sems(), name=name,
        )(*self.arrays)
        return self.fill_own(outs)


def _host_exchange(exchange, refs, n_in, n_out, step, n_steps):
    if exchange is None:
        return refs
    n = exchange.n
    own_in, ex_in = refs[:n_in], refs[n_in:n_in + n]
    own_out, ex_out = refs[n_in + n:n_in + n + n_out], refs[n_in + n + n_out:n_in + 2 * n + n_out]
    rest = refs[n_in + 2 * n + n_out:]
    start, forward, finish = exchange.phases(ex_in, ex_out, rest[-2], rest[-1])
    pl.when(step == 0)(start)
    pl.when(step == (2 * n_steps) // 3)(forward)
    pl.when(step == n_steps - 1)(finish)
    return own_in + own_out + rest[:-2]


def _hosted_call(exchange, body, *, grid, in_specs, out_specs, out_shape, scratch_shapes, compiler_params, name, args):
    if exchange is None:
        return pl.pallas_call(body, grid=grid, in_specs=in_specs, out_specs=out_specs, out_shape=out_shape,
                              scratch_shapes=scratch_shapes, compiler_params=compiler_params, name=name)(*args)
    n = exchange.n
    res = pl.pallas_call(
        body, grid=grid, in_specs=list(in_specs) + [ANY] * n, out_specs=list(out_specs) + [ANY] * n,
        out_shape=list(out_shape) + exchange.out_shapes(), scratch_shapes=list(scratch_shapes) + exchange.sems(),
        compiler_params=compiler_params, name=name)(*args, *exchange.arrays)
    return list(res[:len(out_shape)]) + [exchange.fill_own(res[len(out_shape):])]


def sum_slots(parts, name):
    S, R, C = parts.shape
    br = 128 if R % 128 == 0 else R

    def body(p_ref, o_ref):
        acc = p_ref[0].astype(F32)
        for s in range(1, S):
            acc = acc + p_ref[s].astype(F32)
        o_ref[...] = acc

    return pl.pallas_call(
        body,
        grid=(R // br,),
        in_specs=[pl.BlockSpec((S, br, C), lambda i: (0, i, 0))],
        out_specs=pl.BlockSpec((br, C), lambda i: (i, 0)),
        out_shape=jax.ShapeDtypeStruct((R, C), F32),
        compiler_params=_params(),
        name=name,
    )(parts)


def _adamw_math(w, g, m, v):
    m_new = ADAM_B1 * m + (1.0 - ADAM_B1) * g
    v_new = ADAM_B2 * v + (1.0 - ADAM_B2) * (g * g)
    m_hat = m_new / (1.0 - ADAM_B1 ** ADAM_STEP)
    v_hat = v_new / (1.0 - ADAM_B2 ** ADAM_STEP)
    return -ADAM_LR * (m_hat / (jnp.sqrt(v_hat) + ADAM_EPS) + ADAM_WD * w), m_new, v_new


def reduce_adamw(slots, shares, where, w, m, v, name):
    S, R, C = slots.shape
    br = 128

    def body(where_ref, p_ref, own_ref, w_ref, m_ref, v_ref, g_ref, d_ref, nm_ref, nv_ref):
        own = own_ref[0].astype(F32)
        g = jnp.zeros((br, C), F32)
        for s in range(S):
            g = g + jnp.where(where_ref[0] == s, own, p_ref[s].astype(F32))
        g_ref[...] = g
        d_ref[...], nm_ref[...], nv_ref[...] = _adamw_math(w_ref[...], g, m_ref[...], v_ref[...])

    spec = pl.BlockSpec((br, C), lambda i, where: (i, 0))
    return pl.pallas_call(
        body,
        grid_spec=pltpu.PrefetchScalarGridSpec(
            num_scalar_prefetch=1,
            grid=(R // br,),
            in_specs=[pl.BlockSpec((S, br, C), lambda i, where: (0, i, 0)),
                      pl.BlockSpec((1, br, C), lambda i, where: (where[1], i, 0)), spec, spec, spec],
            out_specs=[spec] * 4,
        ),
        out_shape=[jax.ShapeDtypeStruct((R, C), F32)] * 4,
        compiler_params=_params(),
        name=name,
    )(where, slots, shares, w, m, v)


SMALL_ROWS = 72


def gather_small(vec):
    def body(v_ref, o_ref, send_sems, recv_sems, local_sem):
        x, y, c = _place()
        flips = [(fx, fy, fc) for fx in (0, 1) for fy in (0, 1) for fc in (0, 1)][1:]

        def peer(f):
            return (1 - x if f[0] else x, 1 - y if f[1] else y, 1 - c if f[2] else c)

        slot = lambda p: 4 * p[0] + 2 * p[1] + p[2]
        own = pltpu.make_async_copy(v_ref, o_ref.at[slot((x, y, c))], local_sem)
        own.start()
        cps = [pltpu.make_async_remote_copy(
            src_ref=v_ref, dst_ref=o_ref.at[slot((x, y, c))], send_sem=send_sems.at[k], recv_sem=recv_sems.at[k],
            device_id=peer(f), device_id_type=MESH) for k, f in enumerate(flips)]
        for cp in cps:
            cp.start()
        for k, f in enumerate(flips):
            pltpu.make_async_remote_copy(
                src_ref=v_ref, dst_ref=o_ref.at[slot(peer(f))], send_sem=send_sems.at[k], recv_sem=recv_sems.at[k],
                device_id=(x, y, c), device_id_type=MESH).wait_recv()
        for cp in cps:
            cp.wait_send()
        own.wait()

    return pl.pallas_call(
        body,
        in_specs=[ANY],
        out_specs=ANY,
        out_shape=jax.ShapeDtypeStruct((8,) + vec.shape, vec.dtype),
        scratch_shapes=[pltpu.SemaphoreType.DMA((7,)), pltpu.SemaphoreType.DMA((7,)), pltpu.SemaphoreType.DMA],
        name="gather_small",
    )(vec)


GLOW_PAD = LANE - GLA_RANK


def kernel(x, attn_norm_g, w_in, gla_gate_w2, gla_gate_b, gla_norm_g, rel_bias, w_out, mlp_norm_g, w_ff1, w_ff2, final_norm_g, loss_target, m_attn_norm_g, m_w_in, m_gla_gate_w2, m_gla_gate_b, m_gla_norm_g, m_rel_bias, m_w_out, m_mlp_norm_g, m_w_ff1, m_w_ff2, m_final_norm_g, v_attn_norm_g, v_w_in, v_gla_gate_w2, v_gla_gate_b, v_gla_norm_g, v_rel_bias, v_w_out, v_mlp_norm_g, v_w_ff1, v_w_ff2, v_final_norm_g):
    xs, tgt = x[0], loss_target[0]
    T = xs.shape[0]
    cx, cy, cc = _place()
    chip = 2 * cx + cy
    where = jnp.stack([2 * chip + cc, chip]).astype(jnp.int32)
    gf = final_norm_g.reshape(1, D_MODEL)

    win_g, w2_g = Exchange("gather", [w_in[0].astype(BF16), gla_gate_w2[0]]).run("gather_w_in")
    win = jnp.transpose(win_g, (1, 0, 2)).reshape(D_MODEL, D_IN)
    n_glow = R_GLOW + GLA_RANK
    wp = jnp.concatenate([win[:, :n_glow], jnp.zeros((D_MODEL, GLOW_PAD), BF16), win[:, n_glow:]], axis=1)
    w2 = jnp.transpose(w2_g, (1, 0, 2)).reshape(GLA_RANK, GLA_QK)
    w2p = jnp.concatenate([w2, jnp.zeros((GLOW_PAD, GLA_QK), F32)], axis=0)

    proj, nx = inproj(xs, attn_norm_g, wp)
    tab = bias_tables(rel_bias)
    ob, lse, (wout_g, wff1, wff2_g) = dsa_fwd(
        proj, tab, Exchange("gather", [w_out[0].astype(BF16), w_ff1[0].astype(BF16), w_ff2[0].astype(BF16)]))
    wout = wout_g.reshape(D_MODEL, D_MODEL)
    wff2 = wff2_g.reshape(D_FF, D_MODEL)
    oa, opre, sprev = gla_fwd(proj, w2p, gla_gate_b, gla_norm_g)
    mixed, nm, act, dpre, dh2, dh1, dmixed, loss, dgf, dg2 = post_fused(xs, oa, ob, tgt, mlp_norm_g, gf, wout, wff1, wff2)

    late = [
        wgrad(mixed, dh1, "wgrad_out").reshape(4, D_MODEL // 4, D_MODEL),
        wgrad(nm, dpre, "wgrad_ff1", bn=FF_BLOCK, col_blocked=True),
        wgrad(act, dh2, "wgrad_ff2", bm=FF_BLOCK).reshape(4, FF_BLOCK, D_MODEL),
    ]
    da, dw2p, dbg, dgn = gla_bwd(proj, w2p, gla_gate_b, gla_norm_g, opre, sprev, dmixed)
    dq, dk, dv, dtab, late_slots = dsa_bwd(proj, tab, ob, lse, dmixed, Exchange("scatter", late))
    slots = dict(zip(["w_out", "w_ff1", "w_ff2"], late_slots))
    shares = dict(zip(["w_out", "w_ff1", "w_ff2"], late))
    drel = bias_tables_bwd(dtab)

    dwp = wgrad_cat(nx, [da, dq, dk, dv], "wgrad_in")
    dwin = jnp.concatenate([dwp[:, :n_glow], dwp[:, P_A:]], axis=1)
    dwin = [jnp.transpose(dwin.reshape(D_MODEL, 4, D_IN // 4), (1, 0, 2))]
    dxs, dg1, (slots["w_in"],) = dx_final(xs, dh1, attn_norm_g, da, dq, dk, dv, wp, Exchange("scatter", dwin))
    shares["w_in"] = dwin[0]

    sizes = [D_MODEL, GLA_QK, GLA_WIDTH, REL_BUCKETS * DSA_HEADS, D_MODEL, D_MODEL, GLA_RANK * GLA_QK, 1]
    small = jnp.concatenate([dg1.reshape(-1), dbg.reshape(-1), dgn.reshape(-1), drel.reshape(-1), dg2.reshape(-1),
                             dgf.reshape(-1), dw2p[:GLA_RANK].reshape(-1), loss.reshape(-1),
                             jnp.zeros((SMALL_ROWS * LANE - sum(sizes),), F32)]).reshape(SMALL_ROWS, LANE)
    tot = sum_slots(gather_small(small), "sum_small").reshape(-1)
    offs = np.concatenate([[0], np.cumsum(sizes)])
    piece = lambda i: tot[int(offs[i]):int(offs[i + 1])]
    g_g1 = piece(0).reshape(1, D_MODEL)
    g_bg = piece(1).reshape(1, GLA_QK)
    g_gn = piece(2).reshape(1, GLA_WIDTH)
    g_rel = piece(3).reshape(REL_BUCKETS, DSA_HEADS)
    g_g2 = piece(4).reshape(1, D_MODEL)
    g_gf = piece(5).reshape(1, D_MODEL)
    g_w2 = lax.dynamic_slice_in_dim(piece(6).reshape(GLA_RANK, GLA_QK), chip * (GLA_QK // 4), GLA_QK // 4, axis=1)

    loss_all = piece(7)[0]

    upd = [
        ("attn_norm_g", attn_norm_g, g_g1, m_attn_norm_g, v_attn_norm_g),
        ("w_in", w_in[0], None, m_w_in[0], v_w_in[0]),
        ("gla_gate_w2", gla_gate_w2[0], g_w2, m_gla_gate_w2[0], v_gla_gate_w2[0]),
        ("gla_gate_b", gla_gate_b, g_bg, m_gla_gate_b, v_gla_gate_b),
        ("gla_norm_g", gla_norm_g, g_gn, m_gla_norm_g, v_gla_norm_g),
        ("rel_bias", rel_bias, g_rel, m_rel_bias, v_rel_bias),
        ("w_out", w_out[0], None, m_w_out[0], v_w_out[0]),
        ("mlp_norm_g", mlp_norm_g, g_g2, m_mlp_norm_g, v_mlp_norm_g),
        ("w_ff1", w_ff1[0], None, m_w_ff1[0], v_w_ff1[0]),
        ("w_ff2", w_ff2[0], None, m_w_ff2[0], v_w_ff2[0]),
        ("final_norm_g", gf, g_gf, m_final_norm_g.reshape(1, D_MODEL), v_final_norm_g.reshape(1, D_MODEL)),
    ]
    shapes = [attn_norm_g.shape, w_in.shape, gla_gate_w2.shape, gla_gate_b.shape, gla_norm_g.shape, rel_bias.shape,
              w_out.shape, mlp_norm_g.shape, w_ff1.shape, w_ff2.shape, final_norm_g.shape]
    grads, deltas, new_m, new_v = [], [], [], []
    for (name, w, g, m, v), shape in zip(upd, shapes):
        if name in slots:
            g, d, nm_, nv_ = reduce_adamw(slots[name], shares[name], where, w, m, v, "reduce_adamw_" + name)
        else:
            d, nm_, nv_ = adamw(w, g, m, v, "adamw_" + name)
        grads.append(g.reshape(shape))
        deltas.append(d.reshape(shape))
        new_m.append(nm_.reshape(shape))
        new_v.append(nv_.reshape(shape))
    return (loss_all, dxs.reshape(1, T, D_MODEL), *grads, *deltas, *new_m, *new_v)
```

```python
import functools
import math

import jax
import jax.numpy as jnp
import numpy as np
from jax import lax
from jax.experimental import pallas as pl
from jax.experimental.pallas import tpu as pltpu

F32 = jnp.float32
BF16 = jnp.bfloat16
MESH = pl.DeviceIdType.MESH

D_MODEL = 1024
GLA_WIDTH = 512
GLA_HEADS = 4
GLA_DK = 64
GLA_DV = 128
GLA_QK = GLA_HEADS * GLA_DK
GLA_RANK = 16
GLA_TAU = 16.0
GLA_CHUNK = 64
DSA_WIDTH = 512
DSA_HEADS = 8
DSA_DH = 64
DSA_DILATIONS = (1, 4, 16)
DSA_SPAN = 128
DSA_BLOCK = 128
DSA_SUPER = DSA_BLOCK * DSA_DILATIONS[-1]
REL_BUCKETS = 32
REL_MAX_DIST = 2048
D_FF = 4096
D_IN = 3088
EPS = 1e-6
NEG = -1e30
QK_SCALE = 0.125

ADAM_LR = 0.001
ADAM_B1 = 0.9
ADAM_B2 = 0.999
ADAM_EPS = 1e-08
ADAM_WD = 0.01
ADAM_STEP = 10

LANE = 128
P_GQ, P_GK, P_GV, P_GR = 0, 256, 512, 1024
P_GLOW = 1536
P_A = 1664
P_DQ, P_DK, P_DV = 1664, 2176, 2688
P_ALL = 3200
R_GLOW = 1536

VMEM_LIMIT = 56 * 1024 * 1024


def _params(sem=("arbitrary",), vmem=VMEM_LIMIT):
    return pltpu.CompilerParams(dimension_semantics=sem, vmem_limit_bytes=vmem)


def _dot(a, b):
    return jnp.dot(a, b, preferred_element_type=F32)


def _dot_nt(a, b):
    return lax.dot_general(a, b, (((1,), (1,)), ((), ())), preferred_element_type=F32)


def _dot_tn(a, b):
    return lax.dot_general(a, b, (((0,), (0,)), ((), ())), preferred_element_type=F32)


def _split3(x):
    x1 = x.astype(BF16)
    r1 = x - x1.astype(F32)
    x2 = r1.astype(BF16)
    x3 = (r1 - x2.astype(F32)).astype(BF16)
    return x1, x2, x3


def _dot_exact_lhs(m_bf16, x):
    x1, x2, x3 = _split3(x)
    return _dot(m_bf16, x1) + _dot(m_bf16, x2) + _dot(m_bf16, x3)


def _rstd(xf):
    return lax.rsqrt(jnp.mean(xf * xf, axis=-1, keepdims=True) + EPS)


def _load_once(hbm_ref, vmem_ref, sem):
    cp = pltpu.make_async_copy(hbm_ref, vmem_ref, sem)
    cp.start()
    cp.wait()


ANY = pl.BlockSpec(memory_space=pl.ANY)


def inproj(x, g1, wp):
    T = x.shape[0]
    tm = 512

    def body(x_ref, g_ref, w_hbm, proj_ref, nx_ref, w_vmem, sem):
        @pl.when(pl.program_id(0) == 0)
        def _():
            _load_once(w_hbm, w_vmem, sem)

        xf = x_ref[...]
        nx = ((xf * _rstd(xf)) * g_ref[...]).astype(BF16)
        nx_ref[...] = nx
        proj_ref[...] = _dot(nx, w_vmem[...])

    return pl.pallas_call(
        body,
        grid=(T // tm,),
        in_specs=[pl.BlockSpec((tm, D_MODEL), lambda i: (i, 0)), pl.BlockSpec((1, D_MODEL), lambda i: (0, 0)), ANY],
        out_specs=[pl.BlockSpec((tm, P_ALL), lambda i: (i, 0)), pl.BlockSpec((tm, D_MODEL), lambda i: (i, 0))],
        out_shape=[jax.ShapeDtypeStruct((T, P_ALL), F32), jax.ShapeDtypeStruct((T, D_MODEL), BF16)],
        scratch_shapes=[pltpu.VMEM((D_MODEL, P_ALL), BF16), pltpu.SemaphoreType.DMA],
        compiler_params=_params(),
        name="inproj",
    )(x, g1, wp)


GLA_CHUNKS_PER_STEP = 16
GLA_ROWS = GLA_CHUNK * GLA_CHUNKS_PER_STEP


def _gla_masks():
    lane = lax.broadcasted_iota(jnp.int32, (1, GLA_QK), 1)
    return [(lane >= h * GLA_DK) & (lane < (h + 1) * GLA_DK) for h in range(GLA_HEADS)]


def _log_sigmoid(x):
    return jnp.minimum(x, 0.0) - jnp.log(1.0 + jnp.exp(-jnp.abs(x)))


def _sigmoid(x):
    return 1.0 / (1.0 + jnp.exp(-x))


def _head_cols(h):
    return slice(h * GLA_DV, (h + 1) * GLA_DV)


GLA_GROUP = 256


def _gla_step_constants():
    ri = lax.broadcasted_iota(jnp.int32, (GLA_GROUP, GLA_GROUP), 0)
    ci = lax.broadcasted_iota(jnp.int32, (GLA_GROUP, GLA_GROUP), 1)
    shift = GLA_CHUNK.bit_length() - 1
    same = lax.shift_right_logical(ri, shift) == lax.shift_right_logical(ci, shift)
    return same & (ri >= ci), same & (ri <= ci), _gla_masks()


def _by_group(fn, *arrays):
    outs = [fn(*[a[g * GLA_GROUP:(g + 1) * GLA_GROUP] for a in arrays]) for g in range(GLA_ROWS // GLA_GROUP)]
    if isinstance(outs[0], tuple):
        return tuple(jnp.concatenate(parts, axis=0) for parts in zip(*outs))
    return jnp.concatenate(outs, axis=0)


def _per_chunk(x):
    return x.reshape(GLA_CHUNKS_PER_STEP, GLA_CHUNK, x.shape[-1])


def _chunk_rows_of(x, c):
    return x[c * GLA_CHUNK:(c + 1) * GLA_CHUNK]


def _stack_masked(x, masks):
    return jnp.concatenate([jnp.where(m, x, 0.0) for m in masks], axis=0)


def _stack_head_cols(x):
    return jnp.concatenate([x[:, _head_cols(h)] for h in range(GLA_HEADS)], axis=0)


def _diag_blocks(full, masks):
    out = jnp.where(masks[0], full[:GLA_DV], 0.0)
    for h in range(1, GLA_HEADS):
        out = out + jnp.where(masks[h], full[h * GLA_DV:(h + 1) * GLA_DV], 0.0)
    return out


def _row_blocks_masked(full, masks):
    out = jnp.where(masks[0], full[:GLA_CHUNK], 0.0)
    for h in range(1, GLA_HEADS):
        out = out + jnp.where(masks[h], full[h * GLA_CHUNK:(h + 1) * GLA_CHUNK], 0.0)
    return out


def _gla_step_common(q, k, glow_b, w2, bg, tri):
    gpre = _dot(glow_b, w2) + bg
    glog = _log_sigmoid(gpre) / GLA_TAU
    b = _by_group(lambda g: _dot_exact_lhs(tri, g), glog)
    bl = jnp.sum(_per_chunk(glog), axis=1, keepdims=True)
    eb = jnp.exp(b)
    enb = jnp.exp(-b)
    eke = jnp.exp(jnp.broadcast_to(bl, (GLA_CHUNKS_PER_STEP, GLA_CHUNK, GLA_QK)).reshape(GLA_ROWS, GLA_QK) - b)
    return gpre, eb, enb, eke, jnp.exp(bl), (q * QK_SCALE) * eb, k * enb, k * eke


def gla_fwd(proj, w2p, bg, gn):
    T = proj.shape[0]
    n_steps = T // GLA_ROWS
    n_chunks = T // GLA_CHUNK

    def body(proj_ref, w2_ref, bg_ref, gn_ref, oa_ref, opre_ref, sprev_ref, st_ref):
        @pl.when(pl.program_id(0) == 0)
        def _():
            st_ref[...] = jnp.zeros_like(st_ref)

        causal, _, masks = _gla_step_constants()
        q = proj_ref[:, P_GQ:P_GQ + GLA_QK]
        k = proj_ref[:, P_GK:P_GK + GLA_QK]
        v = proj_ref[:, P_GV:P_GV + GLA_WIDTH]
        r = proj_ref[:, P_GR:P_GR + GLA_WIDTH]
        glow = proj_ref[:, P_GLOW:P_GLOW + LANE].astype(BF16)
        _, _, _, _, ebl, qd, ki, ke = _gla_step_common(q, k, glow, w2_ref[...].astype(BF16), bg_ref[...], causal.astype(BF16))
        ki_b = ki.astype(BF16)
        v_b = v.astype(BF16)
        o_heads = []
        for h in range(GLA_HEADS):
            def intra(qd_g, ki_g, v_g):
                att = jnp.where(causal, _dot_nt(qd_g, ki_g), 0.0)
                return _dot(att.astype(BF16), v_g)

            o_heads.append(_by_group(intra, jnp.where(masks[h], qd, 0.0).astype(BF16), ki_b, v_b[:, _head_cols(h)]))
        st = st_ref[...]
        states = []
        for c in range(GLA_CHUNKS_PER_STEP):
            states.append(st)
            sprev_ref[c] = st
            inc = _diag_blocks(_dot_tn(_chunk_rows_of(v_b, c), _chunk_rows_of(ke, c).astype(BF16)), masks)
            st = st * ebl[c] + inc
        st_ref[...] = st
        inter = []
        for c in range(GLA_CHUNKS_PER_STEP):
            qd_c = _stack_masked(_chunk_rows_of(qd, c), masks).astype(BF16)
            got = _dot_nt(qd_c, states[c].astype(BF16))
            inter.append(jnp.concatenate([got[h * GLA_CHUNK:(h + 1) * GLA_CHUNK] for h in range(GLA_HEADS)], axis=1))
        o = jnp.concatenate(o_heads, axis=1) + jnp.concatenate(inter, axis=0)
        opre_ref[...] = o
        on = jnp.concatenate([o[:, _head_cols(h)] * _rstd(o[:, _head_cols(h)]) for h in range(GLA_HEADS)], axis=1)
        oa_ref[...] = ((on * gn_ref[...]) * (r * _sigmoid(r))).astype(BF16)

    return pl.pallas_call(
        body,
        grid=(n_steps,),
        in_specs=[
            pl.BlockSpec((GLA_ROWS, P_A), lambda i: (i, 0)),
            pl.BlockSpec((LANE, GLA_QK), lambda i: (0, 0)),
            pl.BlockSpec((1, GLA_QK), lambda i: (0, 0)),
            pl.BlockSpec((1, GLA_WIDTH), lambda i: (0, 0)),
        ],
        out_specs=[
            pl.BlockSpec((GLA_ROWS, GLA_WIDTH), lambda i: (i, 0)),
            pl.BlockSpec((GLA_ROWS, GLA_WIDTH), lambda i: (i, 0)),
            pl.BlockSpec((GLA_CHUNKS_PER_STEP, GLA_DV, GLA_QK), lambda i: (i, 0, 0)),
        ],
        out_shape=[
            jax.ShapeDtypeStruct((T, GLA_WIDTH), BF16),
            jax.ShapeDtypeStruct((T, GLA_WIDTH), F32),
            jax.ShapeDtypeStruct((n_chunks, GLA_DV, GLA_QK), F32),
        ],
        scratch_shapes=[pltpu.VMEM((GLA_DV, GLA_QK), F32)],
        compiler_params=_params(),
        name="gla_fwd",
    )(proj, w2p, bg, gn)


def gla_bwd(proj, w2p, bg, gn, opre, sprev, dmixed, exchange=None):
    T = proj.shape[0]
    n_steps = T // GLA_ROWS

    def body(*refs):
        refs = _host_exchange(exchange, refs, 7, 4, pl.program_id(0), n_steps)
        proj_ref, w2_ref, bg_ref, gn_ref, opre_ref, sprev_ref, doa_ref, da_ref, dw2_ref, dbg_ref, dgn_ref, dst_ref = refs

        @pl.when(pl.program_id(0) == 0)
        def _():
            dst_ref[...] = jnp.zeros_like(dst_ref)
            dw2_ref[...] = jnp.zeros_like(dw2_ref)
            dbg_ref[...] = jnp.zeros_like(dbg_ref)
            dgn_ref[...] = jnp.zeros_like(dgn_ref)

        causal, causal_t, masks = _gla_step_constants()
        w2 = w2_ref[...].astype(BF16)
        gn = gn_ref[...]
        q = proj_ref[:, P_GQ:P_GQ + GLA_QK]
        k = proj_ref[:, P_GK:P_GK + GLA_QK]
        v_b = proj_ref[:, P_GV:P_GV + GLA_WIDTH].astype(BF16)
        r = proj_ref[:, P_GR:P_GR + GLA_WIDTH]
        glow = proj_ref[:, P_GLOW:P_GLOW + LANE].astype(BF16)
        o = opre_ref[...]
        doa = doa_ref[...]
        gpre, eb, enb, eke, ebl, qd, ki, ke = _gla_step_common(q, k, glow, w2, bg_ref[...], causal.astype(BF16))
        sig = _sigmoid(r)
        rs = jnp.concatenate([jnp.broadcast_to(_rstd(o[:, _head_cols(h)]), (GLA_ROWS, GLA_DV)) for h in range(GLA_HEADS)], axis=1)
        on = o * rs
        d_ong = doa * (r * sig)
        dr = doa * (on * gn) * (sig * (1.0 + r * (1.0 - sig)))
        dgn_ref[...] += jnp.sum(d_ong * on, axis=0, keepdims=True)
        d_on = d_ong * gn
        t = d_on * on
        mean_t = jnp.concatenate([jnp.broadcast_to(jnp.mean(t[:, _head_cols(h)], axis=-1, keepdims=True), (GLA_ROWS, GLA_DV))
                                  for h in range(GLA_HEADS)], axis=1)
        do_b = (rs * (d_on - on * mean_t)).astype(BF16)
        ki_b = ki.astype(BF16)
        ke_b = ke.astype(BF16)
        dqd = jnp.zeros_like(qd)
        dki = jnp.zeros_like(qd)
        dv_heads = []
        for h in range(GLA_HEADS):
            qd_h = jnp.where(masks[h], qd, 0.0).astype(BF16)
            do_h = do_b[:, _head_cols(h)]

            def intra(qd_g, ki_g, v_g, do_g):
                att = jnp.where(causal, _dot_nt(qd_g, ki_g), 0.0).astype(BF16)
                d_att = jnp.where(causal, _dot_nt(do_g, v_g), 0.0).astype(BF16)
                return _dot_tn(att, do_g), _dot(d_att, ki_g), _dot_tn(d_att, qd_g)

            dv_h, dqd_h, dki_h = _by_group(intra, qd_h, ki_b, v_b[:, _head_cols(h)], do_h)
            dv_heads.append(dv_h)
            dqd = dqd + jnp.where(masks[h], dqd_h, 0.0)
            dki = dki + dki_h
        states = [sprev_ref[c] for c in range(GLA_CHUNKS_PER_STEP)]
        dqd_inter, dst_adds = [], []
        for c in range(GLA_CHUNKS_PER_STEP):
            do_c = _stack_head_cols(_chunk_rows_of(do_b, c))
            dqd_inter.append(_row_blocks_masked(_dot(do_c, states[c].astype(BF16)), masks))
            dst_adds.append(_diag_blocks(_dot_tn(_chunk_rows_of(do_b, c), _chunk_rows_of(qd, c).astype(BF16)), masks))
        dst = dst_ref[...]
        dsts, debls = [None] * GLA_CHUNKS_PER_STEP, [None] * GLA_CHUNKS_PER_STEP
        for c in reversed(range(GLA_CHUNKS_PER_STEP)):
            dsts[c] = dst
            debls[c] = jnp.sum(dst * states[c], axis=0, keepdims=True)
            dst = dst * ebl[c] + dst_adds[c]
        dst_ref[...] = dst
        dv_inter, dke = [], []
        for c in range(GLA_CHUNKS_PER_STEP):
            dst_b = dsts[c].astype(BF16)
            got = _dot_nt(_stack_masked(_chunk_rows_of(ke, c), masks).astype(BF16), dst_b)
            dv_inter.append(jnp.concatenate([got[h * GLA_CHUNK:(h + 1) * GLA_CHUNK] for h in range(GLA_HEADS)], axis=1))
            dke.append(_row_blocks_masked(_dot(_stack_head_cols(_chunk_rows_of(v_b, c)), dst_b), masks))
        dqd = dqd + jnp.concatenate(dqd_inter, axis=0)
        dke = jnp.concatenate(dke, axis=0)
        dv = jnp.concatenate(dv_heads, axis=1) + jnp.concatenate(dv_inter, axis=0)
        dkk = dke * ke
        dbl = jnp.sum(_per_chunk(dkk), axis=1, keepdims=True) + jnp.stack(debls) * ebl
        last_row = lax.broadcasted_iota(jnp.int32, (GLA_CHUNKS_PER_STEP, GLA_CHUNK, GLA_QK), 1) == GLA_CHUNK - 1
        db = dqd * qd - dki * ki - dkk + jnp.where(last_row, dbl, 0.0).reshape(GLA_ROWS, GLA_QK)
        tri_t = causal_t.astype(BF16)
        dglog = _by_group(lambda g: _dot_exact_lhs(tri_t, g), db)
        dgpre = (dglog / GLA_TAU) * _sigmoid(-gpre)
        dgpre_b = dgpre.astype(BF16)
        da_ref[...] = jnp.concatenate(
            [dqd * eb * QK_SCALE, dki * enb + dke * eke, dv, dr, _dot_nt(dgpre_b, w2)], axis=1).astype(BF16)
        dw2_ref[...] += _dot_tn(glow, dgpre_b)
        dbg_ref[...] += jnp.sum(dgpre, axis=0, keepdims=True)

    rev = lambda i: (n_steps - 1 - i, 0)
    return _hosted_call(
        exchange,
        body,
        grid=(n_steps,),
        in_specs=[
            pl.BlockSpec((GLA_ROWS, P_A), rev),
            pl.BlockSpec((LANE, GLA_QK), lambda i: (0, 0)),
            pl.BlockSpec((1, GLA_QK), lambda i: (0, 0)),
            pl.BlockSpec((1, GLA_WIDTH), lambda i: (0, 0)),
            pl.BlockSpec((GLA_ROWS, GLA_WIDTH), rev),
            pl.BlockSpec((GLA_CHUNKS_PER_STEP, GLA_DV, GLA_QK), lambda i: (n_steps - 1 - i, 0, 0)),
            pl.BlockSpec((GLA_ROWS, GLA_WIDTH), rev),
        ],
        out_specs=[
            pl.BlockSpec((GLA_ROWS, P_A), rev),
            pl.BlockSpec((LANE, GLA_QK), lambda i: (0, 0)),
            pl.BlockSpec((1, GLA_QK), lambda i: (0, 0)),
            pl.BlockSpec((1, GLA_WIDTH), lambda i: (0, 0)),
        ],
        out_shape=[
            jax.ShapeDtypeStruct((T, P_A), BF16),
            jax.ShapeDtypeStruct((LANE, GLA_QK), F32),
            jax.ShapeDtypeStruct((1, GLA_QK), F32),
            jax.ShapeDtypeStruct((1, GLA_WIDTH), F32),
        ],
        scratch_shapes=[pltpu.VMEM((GLA_DV, GLA_QK), F32)],
        compiler_params=_params(),
        name="gla_bwd",
        args=(proj, w2p, bg, gn, opre, sprev, dmixed),
    )


def _t5_bucket(dist):
    max_exact = REL_BUCKETS // 2
    n = np.maximum(dist, 0)
    large = max_exact + (np.log(np.maximum(n, 1) / max_exact) / math.log(REL_MAX_DIST / max_exact)
                         * (REL_BUCKETS - max_exact)).astype(np.int32)
    large = np.minimum(large, REL_BUCKETS - 1)
    return np.where(n < max_exact, n, large).astype(np.int32)


SUBLANES = 8


def _bucket_rows():
    steps = DSA_BLOCK - np.arange(2 * DSA_BLOCK)
    in_band = (steps >= 0) & (steps <= DSA_SPAN)
    rows = np.stack([np.where(in_band, _t5_bucket(steps * d), -1) for d in DSA_DILATIONS]).astype(np.int32)
    return np.broadcast_to(rows[:, None, :], (len(DSA_DILATIONS), SUBLANES, 2 * DSA_BLOCK)).copy()


def bias_tables(rel_bias):
    ids = jnp.asarray(_bucket_rows())
    nd = len(DSA_DILATIONS)

    def body(rel_ref, ids_ref, tab_ref):
        idt = ids_ref[0]
        for h in range(DSA_HEADS):
            row = jnp.where(idt < 0, NEG, 0.0).astype(F32)
            for b in range(REL_BUCKETS):
                row = jnp.where(idt == b, rel_ref[b, h], row)
            full = jnp.broadcast_to(row[0:1], (DSA_BLOCK, 2 * DSA_BLOCK))
            tab_ref[0, h] = pltpu.roll(full, 0, 1, stride=1, stride_axis=0)

    return pl.pallas_call(
        body,
        grid=(nd,),
        in_specs=[pl.BlockSpec(memory_space=pltpu.SMEM), pl.BlockSpec((1, SUBLANES, 2 * DSA_BLOCK), lambda d: (d, 0, 0))],
        out_specs=pl.BlockSpec((1, DSA_HEADS, DSA_BLOCK, 2 * DSA_BLOCK), lambda d: (d, 0, 0, 0)),
        out_shape=jax.ShapeDtypeStruct((nd, DSA_HEADS, DSA_BLOCK, 2 * DSA_BLOCK), F32),
        compiler_params=_params(),
        name="bias_tables",
    )(rel_bias, ids)


def _bucket_ids():
    L = DSA_BLOCK
    steps = L + np.arange(L)[:, None] - np.arange(2 * L)[None, :]
    in_band = (steps >= 0) & (steps <= DSA_SPAN)
    return np.stack([np.where(in_band, _t5_bucket(steps * d), -1) for d in DSA_DILATIONS]).astype(np.int32)


def bias_tables_bwd(dtab):
    ids = jnp.asarray(_bucket_ids())
    nd = len(DSA_DILATIONS)

    def body(dtab_ref, ids_ref, drel_ref):
        @pl.when(pl.program_id(0) == 0)
        def _():
            for b in range(REL_BUCKETS):
                for h in range(DSA_HEADS):
                    drel_ref[b, h] = 0.0

        idt = ids_ref[0]
        for b in range(REL_BUCKETS):
            in_bucket = idt == b
            for h in range(DSA_HEADS):
                drel_ref[b, h] += jnp.sum(jnp.where(in_bucket, dtab_ref[0, h], 0.0))

    return pl.pallas_call(
        body,
        grid=(nd,),
        in_specs=[pl.BlockSpec((1, DSA_HEADS, DSA_BLOCK, 2 * DSA_BLOCK), lambda d: (d, 0, 0, 0)),
                  pl.BlockSpec((1, DSA_BLOCK, 2 * DSA_BLOCK), lambda d: (d, 0, 0))],
        out_specs=pl.BlockSpec(memory_space=pltpu.SMEM),
        out_shape=jax.ShapeDtypeStruct((REL_BUCKETS, DSA_HEADS), F32),
        compiler_params=_params(),
        name="bias_tables_bwd",
    )(dtab, ids)


DSA_PAIRS = DSA_HEADS // 2
DSA_UNROLL = 16
DSA_COMBINE_ROWS = 256


def _dsa_units(d):
    return d, DSA_SUPER // (DSA_BLOCK * d)


def _dsa_specs(T):
    nsb = T // DSA_SUPER
    qcol, kcol, vcol = P_DQ // LANE, P_DK // LANE, P_DV // LANE
    return nsb, qcol, kcol, vcol


def _head_lane_mask():
    return lax.broadcasted_iota(jnp.int32, (1, LANE), 1) < DSA_DH


def _fill_tile_variants(tab_ref, variants):
    col = lax.broadcasted_iota(jnp.int32, (2 * DSA_BLOCK, 2 * DSA_BLOCK), 1)
    for di in range(len(DSA_DILATIONS)):
        tile = tab_ref[di, 0]
        variants[di, 0] = tile
        variants[di, 1] = jnp.where(col < DSA_BLOCK, NEG, tile)


def _tile_variants_scratch():
    return pltpu.VMEM((len(DSA_DILATIONS), 2, 2 * DSA_BLOCK, 2 * DSA_BLOCK), F32)


def _pair_tiles(tab):
    return tab.reshape(len(DSA_DILATIONS), DSA_PAIRS, 2 * DSA_BLOCK, 2 * DSA_BLOCK)


def _stack_heads(t, head0):
    return jnp.concatenate([jnp.where(head0, t, 0.0), jnp.where(head0, 0.0, t)], axis=0)


def dsa_fwd(proj, tab, exchange=None):
    T = proj.shape[0]
    nsb, qcol, kcol, vcol = _dsa_specs(T)
    S = DSA_SUPER

    def body(*refs):
        refs = _host_exchange(exchange, refs, 6, 2, pl.program_id(0) * nsb + pl.program_id(1), DSA_PAIRS * nsb)
        q_ref, kp_ref, kc_ref, vp_ref, vc_ref, tab_ref, out_ref, lse_ref, kk, vv, ob, lb, tiles = refs
        sb = pl.program_id(1)
        kk[0:S, :] = kp_ref[...]
        kk[S:2 * S, :] = kc_ref[...]
        vv[0:S, :] = vp_ref[...]
        vv[S:2 * S, :] = vc_ref[...]
        head0 = _head_lane_mask()
        pl.when(sb == 0)(functools.partial(_fill_tile_variants, tab_ref, tiles))

        for di, d in enumerate(DSA_DILATIONS):
            n_res, n_blk = _dsa_units(d)

            def unit(u, carry, di=di, d=d, n_blk=n_blk):
                r = u // n_blk
                c = u % n_blk
                q0 = r + d * DSA_BLOCK * c
                qrows = pl.ds(q0, DSA_BLOCK, stride=d) if d > 1 else pl.ds(q0, DSA_BLOCK)
                krows = pl.ds(S + q0 - d * DSA_BLOCK, 2 * DSA_BLOCK, stride=d) if d > 1 else pl.ds(S + q0 - DSA_BLOCK, 2 * DSA_BLOCK)
                q2 = q_ref[qrows, :] * QK_SCALE
                k2 = kk[krows, :].astype(BF16)
                v2 = vv[krows, :].astype(BF16)
                qs = _stack_heads(q2, head0).astype(BF16)
                s = _dot_nt(qs, k2) + tiles[di, ((sb == 0) & (c == 0)).astype(jnp.int32)]
                m = jnp.max(s, axis=-1, keepdims=True)
                p = jnp.exp(s - m)
                den = jnp.sum(p, axis=-1, keepdims=True)
                o = _dot(p.astype(BF16), v2) / den
                l = jnp.broadcast_to(m + jnp.log(den), (2 * DSA_BLOCK, LANE))
                ob[di, qrows, :] = jnp.where(head0, o[:DSA_BLOCK], o[DSA_BLOCK:])
                lb[di, qrows, :] = jnp.where(head0, l[:DSA_BLOCK], l[DSA_BLOCK:])
                return carry

            lax.fori_loop(0, n_res * n_blk, unit, 0, unroll=DSA_UNROLL)

        def combine(i, carry):
            rows = pl.ds(pl.multiple_of(i * DSA_COMBINE_ROWS, DSA_COMBINE_ROWS), DSA_COMBINE_ROWS)
            l0, l1, l2 = lb[0, rows, :], lb[1, rows, :], lb[2, rows, :]
            mx = jnp.maximum(jnp.maximum(l0, l1), l2)
            e0, e1, e2 = jnp.exp(l0 - mx), jnp.exp(l1 - mx), jnp.exp(l2 - mx)
            den = e0 + e1 + e2
            out_ref[rows, :] = (e0 * ob[0, rows, :] + e1 * ob[1, rows, :] + e2 * ob[2, rows, :]) / den
            lse_ref[rows, :] = mx + jnp.log(den)
            return carry

        lax.fori_loop(0, S // DSA_COMBINE_ROWS, combine, 0)

    prev = lambda col: (lambda hp, sb: (jnp.maximum(sb - 1, 0), col + hp))
    cur = lambda col: (lambda hp, sb: (sb, col + hp))
    blk = lambda f: pl.BlockSpec((S, LANE), f)
    return _hosted_call(
        exchange,
        body,
        grid=(DSA_PAIRS, nsb),
        in_specs=[blk(cur(qcol)), blk(prev(kcol)), blk(cur(kcol)), blk(prev(vcol)), blk(cur(vcol)),
                  pl.BlockSpec((len(DSA_DILATIONS), 1, 2 * DSA_BLOCK, 2 * DSA_BLOCK), lambda hp, sb: (0, hp, 0, 0))],
        out_specs=[blk(lambda hp, sb: (sb, hp)), blk(lambda hp, sb: (sb, hp))],
        out_shape=[jax.ShapeDtypeStruct((T, DSA_WIDTH), F32), jax.ShapeDtypeStruct((T, DSA_WIDTH), F32)],
        scratch_shapes=[pltpu.VMEM((2 * S, LANE), F32), pltpu.VMEM((2 * S, LANE), F32),
                        pltpu.VMEM((len(DSA_DILATIONS), S, LANE), F32), pltpu.VMEM((len(DSA_DILATIONS), S, LANE), F32),
                        _tile_variants_scratch()],
        compiler_params=_params(("arbitrary", "arbitrary")),
        name="dsa_fwd",
        args=(proj, proj, proj, proj, proj, _pair_tiles(tab)),
    )


def dsa_bwd(proj, tab, ob_out, lse, dmixed, exchange=None):
    T = proj.shape[0]
    nsb, qcol, kcol, vcol = _dsa_specs(T)
    S = DSA_SUPER
    nd = len(DSA_DILATIONS)
    ocol = GLA_WIDTH // LANE

    def body(*refs):
        refs = _host_exchange(exchange, refs, 9, 4, pl.program_id(0) * nsb + pl.program_id(1), DSA_PAIRS * nsb)
        (q_ref, kp_ref, kc_ref, vp_ref, vc_ref, tab_ref, o_ref, lse_ref, do_ref,
         dq_ref, dk_ref, dv_ref, dtab_ref, kk, vv, dqa, dkk, dvv, tiles) = refs
        j = pl.program_id(1)
        sb = nsb - 1 - j
        kk[0:S, :] = kp_ref[...]
        kk[S:2 * S, :] = kc_ref[...]
        vv[0:S, :] = vp_ref[...]
        vv[S:2 * S, :] = vc_ref[...]
        head0 = _head_lane_mask()
        pl.when(j == 0)(functools.partial(_fill_tile_variants, tab_ref, tiles))

        @pl.when(j == 0)
        def _():
            dtab_ref[...] = jnp.zeros_like(dtab_ref)
            dkk[S:2 * S, :] = jnp.zeros((S, LANE), F32)
            dvv[S:2 * S, :] = jnp.zeros((S, LANE), F32)

        @pl.when(j > 0)
        def _():
            dkk[S:2 * S, :] = dkk[0:S, :]
            dvv[S:2 * S, :] = dvv[0:S, :]

        dkk[0:S, :] = jnp.zeros((S, LANE), F32)
        dvv[0:S, :] = jnp.zeros((S, LANE), F32)
        dqa[...] = jnp.zeros_like(dqa)

        for di, d in enumerate(DSA_DILATIONS):
            n_res, n_blk = _dsa_units(d)

            def unit(u, carry, di=di, d=d, n_blk=n_blk):
                r = u // n_blk
                c = u % n_blk
                q0 = r + d * DSA_BLOCK * c
                qrows = pl.ds(q0, DSA_BLOCK, stride=d) if d > 1 else pl.ds(q0, DSA_BLOCK)
                krows = pl.ds(S + q0 - d * DSA_BLOCK, 2 * DSA_BLOCK, stride=d) if d > 1 else pl.ds(S + q0 - DSA_BLOCK, 2 * DSA_BLOCK)
                q2 = q_ref[qrows, :] * QK_SCALE
                k2 = kk[krows, :].astype(BF16)
                v2 = vv[krows, :].astype(BF16)
                do2 = do_ref[qrows, :]
                o2 = o_ref[qrows, :]
                l2 = lse_ref[qrows, :]
                qs = _stack_heads(q2, head0).astype(BF16)
                dos = _stack_heads(do2, head0)
                dos_b = dos.astype(BF16)
                delta = jnp.sum(dos * jnp.concatenate([o2, o2], axis=0), axis=-1, keepdims=True)
                lse = jnp.concatenate([jnp.max(jnp.where(head0, l2, -jnp.inf), axis=-1, keepdims=True),
                                       jnp.max(jnp.where(head0, -jnp.inf, l2), axis=-1, keepdims=True)], axis=0)
                s = _dot_nt(qs, k2) + tiles[di, ((sb == 0) & (c == 0)).astype(jnp.int32)]
                p = jnp.exp(s - lse)
                ds = p * (_dot_nt(dos_b, v2) - delta)
                dtab_ref[di, 0] += ds
                ds_b = ds.astype(BF16)
                dq = _dot(ds_b, k2)
                dqa[qrows, :] += jnp.where(head0, dq[:DSA_BLOCK], dq[DSA_BLOCK:]) * QK_SCALE
                dkk[krows, :] += _dot_tn(ds_b, qs)
                dvv[krows, :] += _dot_tn(p.astype(BF16), dos_b)
                return carry

            lax.fori_loop(0, n_res * n_blk, unit, 0, unroll=DSA_UNROLL)

        dq_ref[...] = dqa[...].astype(BF16)
        dk_ref[...] = dkk[S:2 * S, :].astype(BF16)
        dv_ref[...] = dvv[S:2 * S, :].astype(BF16)

    prev = lambda col: (lambda hp, j: (jnp.maximum(nsb - 2 - j, 0), col + hp))
    cur = lambda col: (lambda hp, j: (nsb - 1 - j, col + hp))
    blk = lambda f: pl.BlockSpec((S, LANE), f)
    out_blk = blk(lambda hp, j: (nsb - 1 - j, hp))
    tab_blk = pl.BlockSpec((nd, 1, 2 * DSA_BLOCK, 2 * DSA_BLOCK), lambda hp, j: (0, hp, 0, 0))
    dq, dk, dv, dtab, *carried = _hosted_call(
        exchange,
        body,
        grid=(DSA_PAIRS, nsb),
        in_specs=[blk(cur(qcol)), blk(prev(kcol)), blk(cur(kcol)), blk(prev(vcol)), blk(cur(vcol)), tab_blk,
                  out_blk, out_blk, blk(cur(ocol))],
        out_specs=[out_blk, out_blk, out_blk, tab_blk],
        out_shape=[jax.ShapeDtypeStruct((T, DSA_WIDTH), BF16)] * 3
        + [jax.ShapeDtypeStruct((nd, DSA_PAIRS, 2 * DSA_BLOCK, 2 * DSA_BLOCK), F32)],
        scratch_shapes=[pltpu.VMEM((2 * S, LANE), F32), pltpu.VMEM((2 * S, LANE), F32), pltpu.VMEM((S, LANE), F32),
                        pltpu.VMEM((2 * S, LANE), F32), pltpu.VMEM((2 * S, LANE), F32), _tile_variants_scratch()],
        compiler_params=_params(("arbitrary", "arbitrary")),
        name="dsa_bwd",
        args=(proj, proj, proj, proj, proj, _pair_tiles(tab), ob_out, lse, dmixed),
    )
    return (dq, dk, dv, dtab.reshape(nd, DSA_HEADS, DSA_BLOCK, 2 * DSA_BLOCK), *carried)


FF_BLOCKS = 4
FF_BLOCK = D_FF // FF_BLOCKS


def post_fused(x, oa, ob, tgt, g2, gf, wout, wff1, wff2):
    T = x.shape[0]
    tm = 256
    inv_d = 1.0 / D_MODEL

    def body(x_ref, oa_ref, ob_ref, tgt_ref, g2_ref, gf_ref, wout_hbm, wff1_hbm, wff2_hbm,
             mixed_ref, nm_ref, a_ref, dpre_ref, dh2_ref, dh1_ref, dmixed_ref, loss_ref, dgf_ref, dg2_ref,
             wout_v, wff1_v, wff2_v, sems):
        @pl.when(pl.program_id(0) == 0)
        def _():
            cps = [pltpu.make_async_copy(s, d, sems.at[i])
                   for i, (s, d) in enumerate([(wout_hbm, wout_v), (wff1_hbm, wff1_v), (wff2_hbm, wff2_v)])]
            for cp in cps:
                cp.start()
            for cp in cps:
                cp.wait()
            loss_ref[...] = jnp.zeros_like(loss_ref)
            dgf_ref[...] = jnp.zeros_like(dgf_ref)
            dg2_ref[...] = jnp.zeros_like(dg2_ref)

        mixed = jnp.concatenate([oa_ref[...], ob_ref[...].astype(BF16)], axis=1)
        mixed_ref[...] = mixed
        h1 = x_ref[...] + _dot(mixed, wout_v[...])
        rs1 = _rstd(h1)
        hn1 = h1 * rs1
        g2 = g2_ref[...]
        nm = (hn1 * g2).astype(BF16)
        nm_ref[...] = nm
        relu = []
        mlp = jnp.zeros((tm, D_MODEL), F32)
        for j in range(FF_BLOCKS):
            cols = slice(j * FF_BLOCK, (j + 1) * FF_BLOCK)
            r_j = jnp.maximum(_dot(nm, wff1_v[j]), 0.0)
            a_j = (r_j * r_j).astype(BF16)
            a_ref[:, cols] = a_j
            relu.append(r_j)
            mlp = mlp + _dot(a_j, wff2_v[cols, :])
        h2 = h1 + mlp
        rsf = _rstd(h2)
        hnf = h2 * rsf
        gf = gf_ref[...]
        diff = hnf * gf - tgt_ref[...]
        loss_ref[...] += 0.5 * jnp.sum(jnp.sum(diff * diff, axis=-1, keepdims=True) * inv_d, axis=0, keepdims=True)
        dy = diff * inv_d
        dgf_ref[...] += jnp.sum(dy * hnf, axis=0, keepdims=True)
        dhnf = dy * gf
        dh2 = rsf * (dhnf - hnf * jnp.mean(dhnf * hnf, axis=-1, keepdims=True))
        dh2_b = dh2.astype(BF16)
        dh2_ref[...] = dh2_b
        dnm = jnp.zeros((tm, D_MODEL), F32)
        for j in range(FF_BLOCKS):
            cols = slice(j * FF_BLOCK, (j + 1) * FF_BLOCK)
            dpre_j = (_dot_nt(dh2_b, wff2_v[cols, :]) * (2.0 * relu[j])).astype(BF16)
            dpre_ref[:, cols] = dpre_j
            dnm = dnm + _dot_nt(dpre_j, wff1_v[j])
        dg2_ref[...] += jnp.sum(dnm * hn1, axis=0, keepdims=True)
        dhn1 = dnm * g2
        dh1 = dh2 + rs1 * (dhn1 - hn1 * jnp.mean(dhn1 * hn1, axis=-1, keepdims=True))
        dh1_ref[...] = dh1
        dmixed_ref[...] = _dot_nt(dh1.astype(BF16), wout_v[...])

    row = lambda w: pl.BlockSpec((tm, w), lambda i: (i, 0))
    vec = lambda w: pl.BlockSpec((1, w), lambda i: (0, 0))
    return pl.pallas_call(
        body,
        grid=(T // tm,),
        in_specs=[row(D_MODEL), row(GLA_WIDTH), row(DSA_WIDTH), row(D_MODEL), vec(D_MODEL), vec(D_MODEL), ANY, ANY, ANY],
        out_specs=[row(D_MODEL), row(D_MODEL), row(D_FF), row(D_FF), row(D_MODEL), row(D_MODEL), row(D_MODEL),
                   vec(1), vec(D_MODEL), vec(D_MODEL)],
        out_shape=[
            jax.ShapeDtypeStruct((T, D_MODEL), BF16),
            jax.ShapeDtypeStruct((T, D_MODEL), BF16),
            jax.ShapeDtypeStruct((T, D_FF), BF16),
            jax.ShapeDtypeStruct((T, D_FF), BF16),
            jax.ShapeDtypeStruct((T, D_MODEL), BF16),
            jax.ShapeDtypeStruct((T, D_MODEL), F32),
            jax.ShapeDtypeStruct((T, D_MODEL), F32),
            jax.ShapeDtypeStruct((1, 1), F32),
            jax.ShapeDtypeStruct((1, D_MODEL), F32),
            jax.ShapeDtypeStruct((1, D_MODEL), F32),
        ],
        scratch_shapes=[pltpu.VMEM((D_MODEL, D_MODEL), BF16), pltpu.VMEM((FF_BLOCKS, D_MODEL, FF_BLOCK), BF16),
                        pltpu.VMEM((D_FF, D_MODEL), BF16), pltpu.SemaphoreType.DMA((3,))],
        compiler_params=_params(),
        name="post_fused",
    )(x, oa, ob, tgt, g2, gf, wout, wff1, wff2)


WGRAD_TOKENS = 2048


def wgrad(a, b, name, bm=None, bn=None, col_blocked=False, tokens=WGRAD_TOKENS):
    T, M = a.shape
    N = b.shape[1]
    bm = M if bm is None else bm
    bn = N if bn is None else bn
    tk = min(tokens, T)
    n_k = T // tk

    def body(a_ref, b_ref, o_ref, acc_ref):
        part = _dot_tn(a_ref[...].astype(BF16), b_ref[...].astype(BF16))
        out = o_ref.at[0] if col_blocked else o_ref
        k = pl.program_id(2)
        if n_k == 1:
            out[...] = part.astype(BF16)
            return

        @pl.when(k == 0)
        def _():
            acc_ref[...] = part

        @pl.when((k > 0) & (k < n_k - 1))
        def _():
            acc_ref[...] += part

        @pl.when(k == n_k - 1)
        def _():
            out[...] = (acc_ref[...] + part).astype(BF16)

    if col_blocked:
        assert bm == M
        out_spec = pl.BlockSpec((1, M, bn), lambda i, j, k: (j, 0, 0))
        out_shape = jax.ShapeDtypeStruct((N // bn, M, bn), BF16)
    else:
        out_spec = pl.BlockSpec((bm, bn), lambda i, j, k: (i, j))
        out_shape = jax.ShapeDtypeStruct((M, N), BF16)
    return pl.pallas_call(
        body,
        grid=(M // bm, N // bn, n_k),
        in_specs=[pl.BlockSpec((tk, bm), lambda i, j, k: (k, i)), pl.BlockSpec((tk, bn), lambda i, j, k: (k, j))],
        out_specs=out_spec,
        out_shape=out_shape,
        scratch_shapes=[pltpu.VMEM((bm, bn), F32)],
        compiler_params=_params(("arbitrary", "arbitrary", "arbitrary")),
        name=name,
    )(a, b)


def wgrad_cat(a, bs, name):
    T, M = a.shape
    widths = [b.shape[1] for b in bs]
    starts = [sum(widths[:i]) for i in range(len(bs))]
    N = sum(widths)
    tk = min(WGRAD_TOKENS // 2, T)
    n_k = T // tk

    def body(a_ref, *rest):
        b_refs, o_ref, acc_ref = rest[:len(bs)], rest[len(bs)], rest[len(bs) + 1]
        k = pl.program_id(0)

        @pl.when(k == 0)
        def _():
            acc_ref[...] = jnp.zeros_like(acc_ref)

        a_t = a_ref[...]
        for b_ref, start, width in zip(b_refs, starts, widths):
            acc_ref[:, start:start + width] += _dot_tn(a_t, b_ref[...])

        @pl.when(k == n_k - 1)
        def _():
            o_ref[...] = acc_ref[...].astype(BF16)

    return pl.pallas_call(
        body,
        grid=(n_k,),
        in_specs=[pl.BlockSpec((tk, M), lambda k: (k, 0))] + [pl.BlockSpec((tk, w), lambda k: (k, 0)) for w in widths],
        out_specs=pl.BlockSpec((M, N), lambda k: (0, 0)),
        out_shape=jax.ShapeDtypeStruct((M, N), BF16),
        scratch_shapes=[pltpu.VMEM((M, N), F32)],
        compiler_params=_params(),
        name=name,
    )(a, *bs)


def dx_final(x, dh1, g1, da, dq, dk, dv, wp, exchange=None):
    T = x.shape[0]
    tm = 512

    def body(*refs):
        refs = _host_exchange(exchange, refs, 8, 2, pl.program_id(0), T // tm)
        x_ref, dh1_ref, g_ref, da_ref, dq_ref, dk_ref, dv_ref, w_hbm, dx_ref, dg_ref, w_vmem, sem = refs

        @pl.when(pl.program_id(0) == 0)
        def _():
            _load_once(w_hbm, w_vmem, sem)
            dg_ref[...] = jnp.zeros_like(dg_ref)

        dnx = (_dot_nt(da_ref[...], w_vmem[:, 0:P_A]) + _dot_nt(dq_ref[...], w_vmem[:, P_DQ:P_DQ + DSA_WIDTH])
               + _dot_nt(dk_ref[...], w_vmem[:, P_DK:P_DK + DSA_WIDTH]) + _dot_nt(dv_ref[...], w_vmem[:, P_DV:P_DV + DSA_WIDTH]))
        xf = x_ref[...]
        rs = _rstd(xf)
        hn = xf * rs
        dg_ref[...] += jnp.sum(dnx * hn, axis=0, keepdims=True)
        dhn = dnx * g_ref[...]
        dx_ref[...] = dh1_ref[...] + rs * (dhn - hn * jnp.mean(dhn * hn, axis=-1, keepdims=True))

    row = lambda w: pl.BlockSpec((tm, w), lambda i: (i, 0))
    vec = pl.BlockSpec((1, D_MODEL), lambda i: (0, 0))
    return _hosted_call(
        exchange,
        body,
        grid=(T // tm,),
        in_specs=[row(D_MODEL), row(D_MODEL), vec, row(P_A), row(DSA_WIDTH), row(DSA_WIDTH), row(DSA_WIDTH), ANY],
        out_specs=[row(D_MODEL), vec],
        out_shape=[jax.ShapeDtypeStruct((T, D_MODEL), F32), jax.ShapeDtypeStruct((1, D_MODEL), F32)],
        scratch_shapes=[pltpu.VMEM((D_MODEL, P_ALL), BF16), pltpu.SemaphoreType.DMA],
        compiler_params=_params(),
        name="dx_final",
        args=(x, dh1, g1, da, dq, dk, dv, wp),
    )


def adamw(w, g, m, v, name):
    R, C = w.shape
    br = 256 if R % 256 == 0 else R

    def body(w_ref, g_ref, m_ref, v_ref, d_ref, nm_ref, nv_ref):
        d_ref[...], nm_ref[...], nv_ref[...] = _adamw_math(w_ref[...], g_ref[...], m_ref[...], v_ref[...])

    spec = pl.BlockSpec((br, C), lambda i: (i, 0))
    return pl.pallas_call(
        body,
        grid=(R // br,),
        in_specs=[spec] * 4,
        out_specs=[spec] * 3,
        out_shape=[jax.ShapeDtypeStruct((R, C), F32)] * 3,
        compiler_params=_params(),
        name=name,
    )(w, g, m, v)


def _place():
    return lax.axis_index("x"), lax.axis_index("y"), lax.axis_index("c")


def _other_chips(x, y):
    return [(1 - x, y), (x, 1 - y), (1 - x, 1 - y)]


class Exchange:
    def __init__(self, kind, arrays):
        self.kind, self.arrays, self.n = kind, arrays, len(arrays)
        self.slots = 4 if kind == "gather" else 8

    def out_shapes(self):
        if self.kind == "gather":
            return [jax.ShapeDtypeStruct((4,) + s.shape, s.dtype) for s in self.arrays]
        return [jax.ShapeDtypeStruct((8,) + s.shape[1:], s.dtype) for s in self.arrays]

    def sems(self):
        return [pltpu.SemaphoreType.DMA((self.n, 19)), pltpu.SemaphoreType.DMA((self.n, 19))]

    def phases(self, ins, outs, send_sems, recv_sems):
        n, scatter = self.n, self.kind == "scatter"
        x, y, c = _place()
        me, sib = (x, y, c), (x, y, 1 - c)
        mine = 2 * x + y
        chips = _other_chips(x, y)
        own_pair = 18

        def region(a, slot, half):
            h = outs[a].shape[1] // 2
            return outs[a].at[slot, pl.ds(half * h, h)]

        def copy(a, k, slot, half, to, src=None):
            return pltpu.make_async_remote_copy(
                src_ref=region(a, slot, half) if src is None else src, dst_ref=region(a, slot, half),
                send_sem=send_sems.at[a, k], recv_sem=recv_sems.at[a, k], device_id=to, device_id_type=MESH)

        def over_ici(t, to_core, from_core):
            return 4 * t + 2 * to_core + from_core

        def passed_on(t, from_core):
            return 12 + 2 * t + from_core

        senders = [(t, cc) for t in range(3) for cc in ((0, 1) if scatter else (c,))]

        def slot_of(t, cc):
            cx, cy = chips[t]
            return 2 * (2 * cx + cy) + cc if scatter else 2 * cx + cy

        def first_copies():
            cps = []
            for a in range(n):
                h = outs[a].shape[1] // 2
                for t, (cx, cy) in enumerate(chips):
                    if scatter:
                        for half in (0, 1):
                            cps.append(copy(a, over_ici(t, half, c), 2 * mine + c, half, (cx, cy, half),
                                            src=ins[a].at[2 * cx + cy, pl.ds(half * h, h)]))
                    else:
                        cps.append(copy(a, over_ici(t, c, c), mine, c, (cx, cy, c), src=ins[a].at[pl.ds(c * h, h)]))
                if scatter:
                    cps.append(pltpu.make_async_remote_copy(
                        src_ref=ins[a].at[mine], dst_ref=outs[a].at[2 * mine + c], send_sem=send_sems.at[a, own_pair],
                        recv_sem=recv_sems.at[a, own_pair], device_id=sib, device_id_type=MESH))
            return cps

        def forward_copies():
            return [copy(a, passed_on(t, cc), slot_of(t, cc), c, sib) for a in range(n) for t, cc in senders]

        def start():
            for cp in first_copies():
                cp.start()

        def forward():
            fws = iter(forward_copies())
            for a in range(n):
                for t, cc in senders:
                    copy(a, over_ici(t, c, cc), slot_of(t, cc), c, me).wait_recv()
                    next(fws).start()

        def finish():
            for a in range(n):
                for t, cc in senders:
                    from_core = cc if scatter else 1 - c
                    copy(a, passed_on(t, from_core), slot_of(t, from_core), 1 - c, me).wait_recv()
                if scatter:
                    pltpu.make_async_remote_copy(
                        src_ref=ins[a].at[mine], dst_ref=outs[a].at[2 * mine + 1 - c], send_sem=send_sems.at[a, own_pair],
                        recv_sem=recv_sems.at[a, own_pair], device_id=me, device_id_type=MESH).wait_recv()
            for cp in first_copies() + forward_copies():
                cp.wait_send()

        return start, forward, finish

    def fill_own(self, outs):
        x, y, c = _place()
        if self.kind == "gather":
            return [lax.dynamic_update_index_in_dim(o, s, 2 * x + y, 0) for o, s in zip(outs, self.arrays)]
        return list(outs)

    def run(self, name):
        n = self.n

        def body(*refs):
            start, forward, finish = self.phases(refs[:n], refs[n:2 * n], *refs[2 * n:])
            start()
            forward()
            finish()

        outs = pl.pallas_call(
            body, in_specs=[ANY] * n, out_specs=[ANY] * n, out_shape=self.out_shapes(), scratch_shapes=self.sems(), name=name,
        )(*self.arrays)
        return self.fill_own(outs)


def _host_exchange(exchange, refs, n_in, n_out, step, n_steps):
    if exchange is None:
        return refs
    n = exchange.n
    own_in, ex_in = refs[:n_in], refs[n_in:n_in + n]
    own_out, ex_out = refs[n_in + n:n_in + n + n_out], refs[n_in + n + n_out:n_in + 2 * n + n_out]
    rest = refs[n_in + 2 * n + n_out:]
    start, forward, finish = exchange.phases(ex_in, ex_out, rest[-2], rest[-1])
    pl.when(step == 0)(start)
    pl.when(step == (2 * n_steps) // 3)(forward)
    pl.when(step == n_steps - 1)(finish)
    return own_in + own_out + rest[:-2]


def _hosted_call(exchange, body, *, grid, in_specs, out_specs, out_shape, scratch_shapes, compiler_params, name, args):
    if exchange is None:
        return pl.pallas_call(body, grid=grid, in_specs=in_specs, out_specs=out_specs, out_shape=out_shape,
                              scratch_shapes=scratch_shapes, compiler_params=compiler_params, name=name)(*args)
    n = exchange.n
    res = pl.pallas_call(
        body, grid=grid, in_specs=list(in_specs) + [ANY] * n, out_specs=list(out_specs) + [ANY] * n,
        out_shape=list(out_shape) + exchange.out_shapes(), scratch_shapes=list(scratch_shapes) + exchange.sems(),
        compiler_params=compiler_params, name=name)(*args, *exchange.arrays)
    return list(res[:len(out_shape)]) + [exchange.fill_own(res[len(out_shape):])]


def sum_slots(parts, name):
    S, R, C = parts.shape
    br = 128 if R % 128 == 0 else R

    def body(p_ref, o_ref):
        acc = p_ref[0].astype(F32)
        for s in range(1, S):
            acc = acc + p_ref[s].astype(F32)
        o_ref[...] = acc

    return pl.pallas_call(
        body,
        grid=(R // br,),
        in_specs=[pl.BlockSpec((S, br, C), lambda i: (0, i, 0))],
        out_specs=pl.BlockSpec((br, C), lambda i: (i, 0)),
        out_shape=jax.ShapeDtypeStruct((R, C), F32),
        compiler_params=_params(),
        name=name,
    )(parts)


def _adamw_math(w, g, m, v):
    m_new = ADAM_B1 * m + (1.0 - ADAM_B1) * g
    v_new = ADAM_B2 * v + (1.0 - ADAM_B2) * (g * g)
    m_hat = m_new / (1.0 - ADAM_B1 ** ADAM_STEP)
    v_hat = v_new / (1.0 - ADAM_B2 ** ADAM_STEP)
    return -ADAM_LR * (m_hat / (jnp.sqrt(v_hat) + ADAM_EPS) + ADAM_WD * w), m_new, v_new


def reduce_adamw(slots, shares, where, w, m, v, name):
    S, R, C = slots.shape
    br = 128

    def body(where_ref, p_ref, own_ref, w_ref, m_ref, v_ref, g_ref, d_ref, nm_ref, nv_ref):
        own = own_ref[0].astype(F32)
        g = jnp.zeros((br, C), F32)
        for s in range(S):
            g = g + jnp.where(where_ref[0] == s, own, p_ref[s].astype(F32))
        g_ref[...] = g
        d_ref[...], nm_ref[...], nv_ref[...] = _adamw_math(w_ref[...], g, m_ref[...], v_ref[...])

    spec = pl.BlockSpec((br, C), lambda i, where: (i, 0))
    return pl.pallas_call(
        body,
        grid_spec=pltpu.PrefetchScalarGridSpec(
            num_scalar_prefetch=1,
            grid=(R // br,),
            in_specs=[pl.BlockSpec((S, br, C), lambda i, where: (0, i, 0)),
                      pl.BlockSpec((1, br, C), lambda i, where: (where[1], i, 0)), spec, spec, spec],
            out_specs=[spec] * 4,
        ),
        out_shape=[jax.ShapeDtypeStruct((R, C), F32)] * 4,
        compiler_params=_params(),
        name=name,
    )(where, slots, shares, w, m, v)


SMALL_ROWS = 72


def gather_small(vec):
    def body(v_ref, o_ref, send_sems, recv_sems, local_sem):
        x, y, c = _place()
        flips = [(fx, fy, fc) for fx in (0, 1) for fy in (0, 1) for fc in (0, 1)][1:]

        def peer(f):
            return (1 - x if f[0] else x, 1 - y if f[1] else y, 1 - c if f[2] else c)

        slot = lambda p: 4 * p[0] + 2 * p[1] + p[2]
        own = pltpu.make_async_copy(v_ref, o_ref.at[slot((x, y, c))], local_sem)
        own.start()
        cps = [pltpu.make_async_remote_copy(
            src_ref=v_ref, dst_ref=o_ref.at[slot((x, y, c))], send_sem=send_sems.at[k], recv_sem=recv_sems.at[k],
            device_id=peer(f), device_id_type=MESH) for k, f in enumerate(flips)]
        for cp in cps:
            cp.start()
        for k, f in enumerate(flips):
            pltpu.make_async_remote_copy(
                src_ref=v_ref, dst_ref=o_ref.at[slot(peer(f))], send_sem=send_sems.at[k], recv_sem=recv_sems.at[k],
                device_id=(x, y, c), device_id_type=MESH).wait_recv()
        for cp in cps:
            cp.wait_send()
        own.wait()

    return pl.pallas_call(
        body,
        in_specs=[ANY],
        out_specs=ANY,
        out_shape=jax.ShapeDtypeStruct((8,) + vec.shape, vec.dtype),
        scratch_shapes=[pltpu.SemaphoreType.DMA((7,)), pltpu.SemaphoreType.DMA((7,)), pltpu.SemaphoreType.DMA],
        name="gather_small",
    )(vec)


GLOW_PAD = LANE - GLA_RANK


def kernel(x, attn_norm_g, w_in, gla_gate_w2, gla_gate_b, gla_norm_g, rel_bias, w_out, mlp_norm_g, w_ff1, w_ff2, final_norm_g, loss_target, m_attn_norm_g, m_w_in, m_gla_gate_w2, m_gla_gate_b, m_gla_norm_g, m_rel_bias, m_w_out, m_mlp_norm_g, m_w_ff1, m_w_ff2, m_final_norm_g, v_attn_norm_g, v_w_in, v_gla_gate_w2, v_gla_gate_b, v_gla_norm_g, v_rel_bias, v_w_out, v_mlp_norm_g, v_w_ff1, v_w_ff2, v_final_norm_g):
    xs, tgt = x[0], loss_target[0]
    T = xs.shape[0]
    cx, cy, cc = _place()
    chip = 2 * cx + cy
    where = jnp.stack([2 * chip + cc, chip]).astype(jnp.int32)
    gf = final_norm_g.reshape(1, D_MODEL)

    win_g, w2_g = Exchange("gather", [w_in[0].astype(BF16), gla_gate_w2[0]]).run("gather_w_in")
    win = jnp.transpose(win_g, (1, 0, 2)).reshape(D_MODEL, D_IN)
    n_glow = R_GLOW + GLA_RANK
    wp = jnp.concatenate([win[:, :n_glow], jnp.zeros((D_MODEL, GLOW_PAD), BF16), win[:, n_glow:]], axis=1)
    w2 = jnp.transpose(w2_g, (1, 0, 2)).reshape(GLA_RANK, GLA_QK)
    w2p = jnp.concatenate([w2, jnp.zeros((GLOW_PAD, GLA_QK), F32)], axis=0)

    proj, nx = inproj(xs, attn_norm_g, wp)
    tab = bias_tables(rel_bias)
    ob, lse, (wout_g, wff1, wff2_g) = dsa_fwd(
        proj, tab, Exchange("gather", [w_out[0].astype(BF16), w_ff1[0].astype(BF16), w_ff2[0].astype(BF16)]))
    wout = wout_g.reshape(D_MODEL, D_MODEL)
    wff2 = wff2_g.reshape(D_FF, D_MODEL)
    oa, opre, sprev = gla_fwd(proj, w2p, gla_gate_b, gla_norm_g)
    mixed, nm, act, dpre, dh2, dh1, dmixed, loss, dgf, dg2 = post_fused(xs, oa, ob, tgt, mlp_norm_g, gf, wout, wff1, wff2)

    late = [
        wgrad(mixed, dh1, "wgrad_out").reshape(4, D_MODEL // 4, D_MODEL),
        wgrad(nm, dpre, "wgrad_ff1", bn=FF_BLOCK, col_blocked=True, tokens=2 * WGRAD_TOKENS),
        wgrad(act, dh2, "wgrad_ff2", bm=FF_BLOCK, tokens=2 * WGRAD_TOKENS).reshape(4, FF_BLOCK, D_MODEL),
    ]
    da, dw2p, dbg, dgn = gla_bwd(proj, w2p, gla_gate_b, gla_norm_g, opre, sprev, dmixed)
    dq, dk, dv, dtab, late_slots = dsa_bwd(proj, tab, ob, lse, dmixed, Exchange("scatter", late))
    slots = dict(zip(["w_out", "w_ff1", "w_ff2"], late_slots))
    shares = dict(zip(["w_out", "w_ff1", "w_ff2"], late))
    drel = bias_tables_bwd(dtab)

    dwp = wgrad_cat(nx, [da, dq, dk, dv], "wgrad_in")
    dwin = jnp.concatenate([dwp[:, :n_glow], dwp[:, P_A:]], axis=1)
    dwin = [jnp.transpose(dwin.reshape(D_MODEL, 4, D_IN // 4), (1, 0, 2))]
    dxs, dg1, (slots["w_in"],) = dx_final(xs, dh1, attn_norm_g, da, dq, dk, dv, wp, Exchange("scatter", dwin))
    shares["w_in"] = dwin[0]

    sizes = [D_MODEL, GLA_QK, GLA_WIDTH, REL_BUCKETS * DSA_HEADS, D_MODEL, D_MODEL, GLA_RANK * GLA_QK, 1]
    small = jnp.concatenate([dg1.reshape(-1), dbg.reshape(-1), dgn.reshape(-1), drel.reshape(-1), dg2.reshape(-1),
                             dgf.reshape(-1), dw2p[:GLA_RANK].reshape(-1), loss.reshape(-1),
                             jnp.zeros((SMALL_ROWS * LANE - sum(sizes),), F32)]).reshape(SMALL_ROWS, LANE)
    tot = sum_slots(gather_small(small), "sum_small").reshape(-1)
    offs = np.concatenate([[0], np.cumsum(sizes)])
    piece = lambda i: tot[int(offs[i]):int(offs[i + 1])]
    g_g1 = piece(0).reshape(1, D_MODEL)
    g_bg = piece(1).reshape(1, GLA_QK)
    g_gn = piece(2).reshape(1, GLA_WIDTH)
    g_rel = piece(3).reshape(REL_BUCKETS, DSA_HEADS)
    g_g2 = piece(4).reshape(1, D_MODEL)
    g_gf = piece(5).reshape(1, D_MODEL)
    g_w2 = lax.dynamic_slice_in_dim(piece(6).reshape(GLA_RANK, GLA_QK), chip * (GLA_QK // 4), GLA_QK // 4, axis=1)

    loss_all = piece(7)[0]

    upd = [
        ("attn_norm_g", attn_norm_g, g_g1, m_attn_norm_g, v_attn_norm_g),
        ("w_in", w_in[0], None, m_w_in[0], v_w_in[0]),
        ("gla_gate_w2", gla_gate_w2[0], g_w2, m_gla_gate_w2[0], v_gla_gate_w2[0]),
        ("gla_gate_b", gla_gate_b, g_bg, m_gla_gate_b, v_gla_gate_b),
        ("gla_norm_g", gla_norm_g, g_gn, m_gla_norm_g, v_gla_norm_g),
        ("rel_bias", rel_bias, g_rel, m_rel_bias, v_rel_bias),
        ("w_out", w_out[0], None, m_w_out[0], v_w_out[0]),
        ("mlp_norm_g", mlp_norm_g, g_g2, m_mlp_norm_g, v_mlp_norm_g),
        ("w_ff1", w_ff1[0], None, m_w_ff1[0], v_w_ff1[0]),
        ("w_ff2", w_ff2[0], None, m_w_ff2[0], v_w_ff2[0]),
        ("final_norm_g", gf, g_gf, m_final_norm_g.reshape(1, D_MODEL), v_final_norm_g.reshape(1, D_MODEL)),
    ]
    shapes = [attn_norm_g.shape, w_in.shape, gla_gate_w2.shape, gla_gate_b.shape, gla_norm_g.shape, rel_bias.shape,
              w_out.shape, mlp_norm_g.shape, w_ff1.shape, w_ff2.shape, final_norm_g.shape]
    grads, deltas, new_m, new_v = [], [], [], []
    for (name, w, g, m, v), shape in zip(upd, shapes):
        if name in slots:
            g, d, nm_, nv_ = reduce_adamw(slots[name], shares[name], where, w, m, v, "reduce_adamw_" + name)
        else:
            d, nm_, nv_ = adamw(w, g, m, v, "adamw_" + name)
        grads.append(g.reshape(shape))
        deltas.append(d.reshape(shape))
        new_m.append(nm_.reshape(shape))
        new_v.append(nv_.reshape(shape))
    return (loss_all, dxs.reshape(1, T, D_MODEL), *grads, *deltas, *new_m, *new_v)
```
